```python
import math
import jax, jax.numpy as jnp
from jax import lax
import numpy as np

D_MODEL = 1024
BATCH = 32
SEQ = 2048
DEPTH = 2

N_META = 16
EPS = 1e-6
A_HEADS = 4
A_DK = 128
A_DV = 128
A_CONV = 4
A_CHUNK = 64
A_WK = A_HEADS * A_DK
A_WV = A_HEADS * A_DV
B_HEADS = 4
B_DK = 128
B_DV = 128
B_CHUNK = 16
B_WK = B_HEADS * B_DK
B_WV = B_HEADS * B_DV
PROJ_SPLITS = (A_WK, A_WK, A_WV, A_HEADS, A_HEADS, A_WV, B_WK, B_WK, B_WV, B_WV, D_MODEL, D_MODEL)
PROJ_WIDTH = sum(PROJ_SPLITS)

kernel_name = 'hybrid_gdn_hgrn2_gated_merge'


def _rmsnorm(x, w):
    xf = x.astype(jnp.float32)
    y = xf * lax.rsqrt(jnp.mean(xf * xf, axis=-1, keepdims=True) + EPS)
    return (y * w.astype(jnp.float32)).astype(x.dtype)


def _l2norm(x):
    xf = x.astype(jnp.float32)
    return xf * lax.rsqrt(jnp.sum(xf * xf, axis=-1, keepdims=True) + EPS)


def _causal_conv(x, w):
    k, c = w.shape
    return lax.conv_general_dilated(
        x, w[:, None, :].astype(x.dtype), window_strides=(1,), padding=[(k - 1, 0)],
        dimension_numbers=('NWC', 'WIO', 'NWC'), feature_group_count=c)


def _masked_exp(diff, mask):
    return jnp.where(mask, jnp.exp(jnp.where(mask, diff, 0.0)), 0.0)


def _chunked_scan(step, state0, inputs, chunk):
    meta = tuple(a[:, :N_META] for a in inputs)
    real = tuple(a[:, N_META:] for a in inputs)
    state, out_meta = step(state0, meta)
    b, s = real[0].shape[:2]
    n = s // chunk
    xs = tuple(jnp.moveaxis(a.reshape(b, n, chunk, *a.shape[2:]), 1, 0) for a in real)
    _, out_real = lax.scan(step, state, xs)
    out_real = jnp.moveaxis(out_real, 0, 1).reshape(b, s, *out_real.shape[3:])
    return jnp.concatenate([out_meta, out_real], axis=1)


def _gdn_step(S, inp):
    q, k, v, beta, g = inp
    q, k, v = (jnp.swapaxes(a, 1, 2) for a in (q, k, v))
    beta, g = jnp.swapaxes(beta, 1, 2), jnp.swapaxes(g, 1, 2)
    L = q.shape[2]
    causal = jnp.tril(jnp.ones((L, L), dtype=bool))
    strict = jnp.tril(jnp.ones((L, L), dtype=bool), k=-1)
    G = jnp.cumsum(g, axis=-1)
    decay = _masked_exp(G[..., :, None] - G[..., None, :], causal)
    kb = k * beta[..., None]
    l_mat = jnp.where(strict, jnp.einsum('bhik,bhjk->bhij', kb, k) * decay, 0.0)
    t_mat = l_mat + jnp.eye(L, dtype=l_mat.dtype)
    rhs = jnp.concatenate([v * beta[..., None], kb * jnp.exp(G)[..., None]], axis=-1)
    sol = lax.linalg.triangular_solve(t_mat, rhs, left_side=True, lower=True, unit_diagonal=True)
    dv = v.shape[-1]
    u, w = sol[..., :dv], sol[..., dv:]
    v_new = u - jnp.einsum('bhik,bhkv->bhiv', w, S)
    scores = jnp.einsum('bhik,bhjk->bhij', q, k) * decay
    o = (jnp.einsum('bhik,bhkv->bhiv', q * jnp.exp(G)[..., None], S)
         + jnp.einsum('bhij,bhjv->bhiv', scores, v_new))
    g_last = G[..., -1:]
    S = (S * jnp.exp(g_last)[..., None]
         + jnp.einsum('bhjk,bhjv->bhkv', k * jnp.exp(g_last - G)[..., None], v_new))
    return S, jnp.swapaxes(o, 1, 2)


def _hgrn2_step(S, inp):
    q, k, v, log_f = (jnp.swapaxes(a, 1, 2) for a in inp)
    L = q.shape[2]
    causal = jnp.tril(jnp.ones((L, L), dtype=bool))
    Bc = jnp.cumsum(log_f, axis=2)
    pair = _masked_exp(Bc[:, :, :, None, :] - Bc[:, :, None, :, :],
                       causal[:, :, None])
    scores = jnp.einsum('bhik,bhjk,bhijk->bhij', q, k, pair)
    o = (jnp.einsum('bhik,bhkv->bhiv', q * jnp.exp(Bc), S)
         + jnp.einsum('bhij,bhjv->bhiv', scores, v))
    b_last = Bc[:, :, -1:]
    S = (S * jnp.exp(b_last[:, :, 0])[..., None]
         + jnp.einsum('bhjk,bhjv->bhkv', k * jnp.exp(b_last - Bc), v))
    return S, jnp.swapaxes(o, 1, 2)


def _layer(h, norm_w, w_in, conv_w, a_log, dt_bias, gnorm_a, gnorm_b, lb,
           w_branch_a, w_branch_b, w_out):
    b, t, _ = h.shape
    f32 = jnp.float32
    xn = _rmsnorm(h, norm_w)
    proj = xn @ w_in.astype(h.dtype)
    offsets = np.cumsum(PROJ_SPLITS)[:-1].tolist()
    (a_q, a_k, a_v, a_beta, a_alpha, a_z,
     b_q, b_f, b_i, b_g, gate_a, gate_b) = jnp.split(proj, offsets, axis=-1)

    qkv = jax.nn.silu(_causal_conv(jnp.concatenate([a_q, a_k, a_v], axis=-1), conv_w))
    q, k, v = jnp.split(qkv, [A_WK, 2 * A_WK], axis=-1)
    q = _l2norm(q.reshape(b, t, A_HEADS, A_DK)) * (A_DK ** -0.5)
    k = _l2norm(k.reshape(b, t, A_HEADS, A_DK))
    v = v.reshape(b, t, A_HEADS, A_DV).astype(f32)
    beta = jax.nn.sigmoid(a_beta.astype(f32))
    g = -jnp.exp(a_log.astype(f32)) * jax.nn.softplus(a_alpha.astype(f32) + dt_bias.astype(f32))
    s0_a = jnp.zeros((b, A_HEADS, A_DK, A_DV), f32)
    o_a = _chunked_scan(_gdn_step, s0_a, (q, k, v, beta, g), A_CHUNK)
    y_a = _rmsnorm(o_a, gnorm_a) * jax.nn.silu(a_z.astype(f32).reshape(b, t, A_HEADS, A_DV))
    y_a = y_a.reshape(b, t, A_WV).astype(h.dtype)

    qb = (jax.nn.silu(b_q.astype(f32)) * (B_DK ** -0.5)).reshape(b, t, B_HEADS, B_DK)
    lbf = lb.astype(f32)
    pos = lbf > 0.0
    log_sig = jax.nn.log_sigmoid(b_f.astype(f32))
    log_f = jnp.where(pos,
                      jnp.logaddexp(jnp.log(jnp.where(pos, lbf, 1.0)), jnp.log1p(-lbf) + log_sig),
                      log_sig)
    kb_ = -jnp.expm1(log_f)
    log_f = log_f.reshape(b, t, B_HEADS, B_DK)
    kb_ = kb_.reshape(b, t, B_HEADS, B_DK)
    vb = b_i.astype(f32).reshape(b, t, B_HEADS, B_DV)
    s0_b = jnp.zeros((b, B_HEADS, B_DK, B_DV), f32)
    o_b = _chunked_scan(_hgrn2_step, s0_b, (qb, kb_, vb, log_f), B_CHUNK)
    y_b = _rmsnorm(o_b, gnorm_b) * jax.nn.silu(b_g.astype(f32).reshape(b, t, B_HEADS, B_DV))
    y_b = y_b.reshape(b, t, B_WV).astype(h.dtype)

    y_a = y_a @ w_branch_a.astype(h.dtype)
    y_b = y_b @ w_branch_b.astype(h.dtype)
    mixed = jax.nn.sigmoid(gate_a) * y_a + jax.nn.sigmoid(gate_b) * y_b
    return h + mixed @ w_out.astype(h.dtype)


def _fwd_setup_inputs(seed: int = 0) -> dict:
    key = jax.random.key(seed)
    ks = jax.random.split(key, 15)
    nrm = jax.random.normal
    x = nrm(ks[0], (BATCH, SEQ, D_MODEL), jnp.float32)
    meta_tokens = nrm(ks[1], (N_META, D_MODEL), jnp.float32)
    norm_w = 1.0 + 0.02 * nrm(ks[2], (DEPTH, D_MODEL), jnp.float32)
    w_in = nrm(ks[3], (DEPTH, D_MODEL, PROJ_WIDTH), jnp.float32) * D_MODEL ** -0.5
    conv_w = nrm(ks[4], (DEPTH, A_CONV, 2 * A_WK + A_WV), jnp.float32) * A_CONV ** -0.5
    a_log = jnp.log(jax.random.uniform(ks[5], (DEPTH, A_HEADS), jnp.float32, 1.0, 16.0))
    dt = jnp.exp(jax.random.uniform(ks[6], (DEPTH, A_HEADS), jnp.float32,
                                    math.log(1e-3), math.log(1e-1)))
    dt_bias = dt + jnp.log(-jnp.expm1(-dt))
    gnorm_a = 1.0 + 0.02 * nrm(ks[7], (DEPTH, A_DV), jnp.float32)
    gnorm_b = 1.0 + 0.02 * nrm(ks[8], (DEPTH, B_DV), jnp.float32)
    hgrn_lower_bounds = 0.1 * nrm(ks[9], (DEPTH, B_WK), jnp.float32)
    w_branch_a = nrm(ks[10], (DEPTH, A_WV, D_MODEL), jnp.float32) * A_WV ** -0.5
    w_branch_b = nrm(ks[11], (DEPTH, B_WV, D_MODEL), jnp.float32) * B_WV ** -0.5
    w_out = nrm(ks[12], (DEPTH, D_MODEL, D_MODEL), jnp.float32) * D_MODEL ** -0.5
    final_norm_w = 1.0 + 0.02 * nrm(ks[13], (D_MODEL,), jnp.float32)
    return {'x': x, 'meta_tokens': meta_tokens, 'norm_w': norm_w, 'w_in': w_in,
            'conv_w': conv_w, 'a_log': a_log, 'dt_bias': dt_bias, 'gnorm_a': gnorm_a,
            'gnorm_b': gnorm_b, 'hgrn_lower_bounds': hgrn_lower_bounds,
            'w_branch_a': w_branch_a, 'w_branch_b': w_branch_b, 'w_out': w_out,
            'final_norm_w': final_norm_w}


def _fwd_reference(x, meta_tokens, norm_w, w_in, conv_w, a_log, dt_bias, gnorm_a, gnorm_b,
              hgrn_lower_bounds, w_branch_a, w_branch_b, w_out, final_norm_w):
    b = x.shape[0]
    meta = jnp.broadcast_to(meta_tokens.astype(x.dtype)[None], (b, N_META, D_MODEL))
    h = jnp.concatenate([meta, x], axis=1)
    lb_sm = jax.nn.softmax(hgrn_lower_bounds.astype(jnp.float32), axis=0)
    lb_all = jnp.cumsum(lb_sm, axis=0) - lb_sm[0]
    for l in range(DEPTH):
        h = _layer(h, norm_w[l], w_in[l], conv_w[l], a_log[l], dt_bias[l], gnorm_a[l],
                   gnorm_b[l], lb_all[l], w_branch_a[l], w_branch_b[l], w_out[l])
    return _rmsnorm(h, final_norm_w)[:, N_META:]


import jax as _jax
import jax.numpy as _jnp

TWIN_FORMAT = 'train_step'
FWD_PARAMS = ['x', 'meta_tokens', 'norm_w', 'w_in', 'conv_w', 'a_log', 'dt_bias', 'gnorm_a', 'gnorm_b', 'hgrn_lower_bounds', 'w_branch_a', 'w_branch_b', 'w_out', 'final_norm_w']
TWIN_WEIGHTS = ['meta_tokens', 'norm_w', 'w_in', 'conv_w', 'a_log', 'dt_bias', 'gnorm_a', 'gnorm_b', 'hgrn_lower_bounds', 'w_branch_a', 'w_branch_b', 'w_out', 'final_norm_w']
TWIN_DIFF_INPUT = 'x'
TWIN_INPUTS = ['x', 'meta_tokens', 'norm_w', 'w_in', 'conv_w', 'a_log', 'dt_bias', 'gnorm_a', 'gnorm_b', 'hgrn_lower_bounds', 'w_branch_a', 'w_branch_b', 'w_out', 'final_norm_w', 'loss_target', 'm_meta_tokens', 'm_norm_w', 'm_w_in', 'm_conv_w', 'm_a_log', 'm_dt_bias', 'm_gnorm_a', 'm_gnorm_b', 'm_hgrn_lower_bounds', 'm_w_branch_a', 'm_w_branch_b', 'm_w_out', 'm_final_norm_w', 'v_meta_tokens', 'v_norm_w', 'v_w_in', 'v_conv_w', 'v_a_log', 'v_dt_bias', 'v_gnorm_a', 'v_gnorm_b', 'v_hgrn_lower_bounds', 'v_w_branch_a', 'v_w_branch_b', 'v_w_out', 'v_final_norm_w']
TWIN_OUTPUTS = ['loss', 'grad_x', 'grad_meta_tokens', 'grad_norm_w', 'grad_w_in', 'grad_conv_w', 'grad_a_log', 'grad_dt_bias', 'grad_gnorm_a', 'grad_gnorm_b', 'grad_hgrn_lower_bounds', 'grad_w_branch_a', 'grad_w_branch_b', 'grad_w_out', 'grad_final_norm_w', 'delta_meta_tokens', 'delta_norm_w', 'delta_w_in', 'delta_conv_w', 'delta_a_log', 'delta_dt_bias', 'delta_gnorm_a', 'delta_gnorm_b', 'delta_hgrn_lower_bounds', 'delta_w_branch_a', 'delta_w_branch_b', 'delta_w_out', 'delta_final_norm_w', 'new_m_meta_tokens', 'new_m_norm_w', 'new_m_w_in', 'new_m_conv_w', 'new_m_a_log', 'new_m_dt_bias', 'new_m_gnorm_a', 'new_m_gnorm_b', 'new_m_hgrn_lower_bounds', 'new_m_w_branch_a', 'new_m_w_branch_b', 'new_m_w_out', 'new_m_final_norm_w', 'new_v_meta_tokens', 'new_v_norm_w', 'new_v_w_in', 'new_v_conv_w', 'new_v_a_log', 'new_v_dt_bias', 'new_v_gnorm_a', 'new_v_gnorm_b', 'new_v_hgrn_lower_bounds', 'new_v_w_branch_a', 'new_v_w_branch_b', 'new_v_w_out', 'new_v_final_norm_w']
TWIN_LEAF_KINDS = {'loss': 'loss', 'grad_x': 'grad_x', 'grad_meta_tokens': 'grad_w', 'grad_norm_w': 'grad_w', 'grad_w_in': 'grad_w', 'grad_conv_w': 'grad_w', 'grad_a_log': 'grad_w', 'grad_dt_bias': 'grad_w', 'grad_gnorm_a': 'grad_w', 'grad_gnorm_b': 'grad_w', 'grad_hgrn_lower_bounds': 'grad_w', 'grad_w_branch_a': 'grad_w', 'grad_w_branch_b': 'grad_w', 'grad_w_out': 'grad_w', 'grad_final_norm_w': 'grad_w', 'delta_meta_tokens': 'delta_w', 'delta_norm_w': 'delta_w', 'delta_w_in': 'delta_w', 'delta_conv_w': 'delta_w', 'delta_a_log': 'delta_w', 'delta_dt_bias': 'delta_w', 'delta_gnorm_a': 'delta_w', 'delta_gnorm_b': 'delta_w', 'delta_hgrn_lower_bounds': 'delta_w', 'delta_w_branch_a': 'delta_w', 'delta_w_branch_b': 'delta_w', 'delta_w_out': 'delta_w', 'delta_final_norm_w': 'delta_w', 'new_m_meta_tokens': 'new_m', 'new_m_norm_w': 'new_m', 'new_m_w_in': 'new_m', 'new_m_conv_w': 'new_m', 'new_m_a_log': 'new_m', 'new_m_dt_bias': 'new_m', 'new_m_gnorm_a': 'new_m', 'new_m_gnorm_b': 'new_m', 'new_m_hgrn_lower_bounds': 'new_m', 'new_m_w_branch_a': 'new_m', 'new_m_w_branch_b': 'new_m', 'new_m_w_out': 'new_m', 'new_m_final_norm_w': 'new_m', 'new_v_meta_tokens': 'new_v', 'new_v_norm_w': 'new_v', 'new_v_w_in': 'new_v', 'new_v_conv_w': 'new_v', 'new_v_a_log': 'new_v', 'new_v_dt_bias': 'new_v', 'new_v_gnorm_a': 'new_v', 'new_v_gnorm_b': 'new_v', 'new_v_hgrn_lower_bounds': 'new_v', 'new_v_w_branch_a': 'new_v', 'new_v_w_branch_b': 'new_v', 'new_v_w_out': 'new_v', 'new_v_final_norm_w': 'new_v'}


def _forward(args):
    return _fwd_reference(*[args[k] for k in FWD_PARAMS])


def _output_shape():
    out = _jax.eval_shape(lambda: _forward(_fwd_setup_inputs(0)))
    return out.shape, out.dtype

N_MICROBATCH = 1
ADAM_LR = 0.001
ADAM_B1 = 0.9
ADAM_B2 = 0.999
ADAM_EPS = 1e-08
ADAM_WD = 0.01
ADAM_STEP = 10
PER_EXAMPLE_BATCH_AXIS = {'x': 0, 'loss_target': 0}
SHARED_INPUTS = []
_WEIGHT_DTYPES = {'meta_tokens': _jnp.float32, 'norm_w': _jnp.float32, 'w_in': _jnp.float32, 'conv_w': _jnp.float32, 'a_log': _jnp.float32, 'dt_bias': _jnp.float32, 'gnorm_a': _jnp.float32, 'gnorm_b': _jnp.float32, 'hgrn_lower_bounds': _jnp.float32, 'w_branch_a': _jnp.float32, 'w_branch_b': _jnp.float32, 'w_out': _jnp.float32, 'final_norm_w': _jnp.float32}
MOMENT_SCALE = {'meta_tokens': 4.674351e-03, 'norm_w': 1.684641e-01, 'w_in': 7.023780e-02, 'conv_w': 8.354119e-02, 'a_log': 5.358126e-01, 'dt_bias': 5.205372e-01, 'gnorm_a': 2.031624e-01, 'gnorm_b': 2.259136e-01, 'hgrn_lower_bounds': 9.336927e-03, 'w_branch_a': 7.299451e-02, 'w_branch_b': 7.593258e-02, 'w_out': 1.052470e-01, 'final_norm_w': 6.401986e+01}


def _to_microbatches(a, axis):
    t = _jnp.moveaxis(a, axis, 0)
    t = t.reshape((N_MICROBATCH, t.shape[0] // N_MICROBATCH) + t.shape[1:])
    return _jnp.moveaxis(t, 1, axis + 1)


def setup_inputs(seed: int = 0) -> dict:
    inp = _fwd_setup_inputs(seed)
    key = _jax.random.fold_in(_jax.random.key(seed), 7919)
    shape, _ = _output_shape()
    out = dict(inp)
    out["loss_target"] = _jax.random.normal(_jax.random.fold_in(key, 0), shape, _jnp.float32)
    for i, name in enumerate(TWIN_WEIGHTS):
        w = inp[name].astype(_jnp.float32)
        if MOMENT_SCALE is None:
            s = _jnp.sqrt(_jnp.mean(_jnp.square(w)) + 1e-30)
        else:
            s = MOMENT_SCALE[name]
        km, kv = _jax.random.split(_jax.random.fold_in(key, i + 1))
        out[name] = w
        out["m_" + name] = s * _jax.random.normal(km, w.shape, _jnp.float32)
        out["v_" + name] = (s * s) * _jax.random.uniform(kv, w.shape, _jnp.float32, 0.5, 1.5)
    if N_MICROBATCH > 1:
        for name, axis in PER_EXAMPLE_BATCH_AXIS.items():
            out[name] = _to_microbatches(out[name], axis)
    return {'x': out['x'], 'meta_tokens': out['meta_tokens'], 'norm_w': out['norm_w'], 'w_in': out['w_in'], 'conv_w': out['conv_w'], 'a_log': out['a_log'], 'dt_bias': out['dt_bias'], 'gnorm_a': out['gnorm_a'], 'gnorm_b': out['gnorm_b'], 'hgrn_lower_bounds': out['hgrn_lower_bounds'], 'w_branch_a': out['w_branch_a'], 'w_branch_b': out['w_branch_b'], 'w_out': out['w_out'], 'final_norm_w': out['final_norm_w'], 'loss_target': out['loss_target'], 'm_meta_tokens': out['m_meta_tokens'], 'm_norm_w': out['m_norm_w'], 'm_w_in': out['m_w_in'], 'm_conv_w': out['m_conv_w'], 'm_a_log': out['m_a_log'], 'm_dt_bias': out['m_dt_bias'], 'm_gnorm_a': out['m_gnorm_a'], 'm_gnorm_b': out['m_gnorm_b'], 'm_hgrn_lower_bounds': out['m_hgrn_lower_bounds'], 'm_w_branch_a': out['m_w_branch_a'], 'm_w_branch_b': out['m_w_branch_b'], 'm_w_out': out['m_w_out'], 'm_final_norm_w': out['m_final_norm_w'], 'v_meta_tokens': out['v_meta_tokens'], 'v_norm_w': out['v_norm_w'], 'v_w_in': out['v_w_in'], 'v_conv_w': out['v_conv_w'], 'v_a_log': out['v_a_log'], 'v_dt_bias': out['v_dt_bias'], 'v_gnorm_a': out['v_gnorm_a'], 'v_gnorm_b': out['v_gnorm_b'], 'v_hgrn_lower_bounds': out['v_hgrn_lower_bounds'], 'v_w_branch_a': out['v_w_branch_a'], 'v_w_branch_b': out['v_w_branch_b'], 'v_w_out': out['v_w_out'], 'v_final_norm_w': out['v_final_norm_w']}


def _loss(weights, diff, rest, loss_target):
    with _jax.named_scope("forward"):
        args = {**rest, TWIN_DIFF_INPUT: diff, **{k: w.astype(_WEIGHT_DTYPES[k]) for k, w in weights.items()}}
        y = _forward(args)
    with _jax.named_scope("loss_head"):
        err = _jnp.square(y.astype(_jnp.float32) - loss_target)
        return 0.5 * _jnp.sum(_jnp.mean(err, axis=-1)) if err.ndim else 0.5 * err


def _adamw(w, g, m, v):
    m = ADAM_B1 * m + (1.0 - ADAM_B1) * g
    v = ADAM_B2 * v + (1.0 - ADAM_B2) * _jnp.square(g)
    m_hat = m / (1.0 - ADAM_B1 ** ADAM_STEP)
    v_hat = v / (1.0 - ADAM_B2 ** ADAM_STEP)
    delta = -ADAM_LR * (m_hat / (_jnp.sqrt(v_hat) + ADAM_EPS) + ADAM_WD * w)
    return delta, m, v


def reference(x, meta_tokens, norm_w, w_in, conv_w, a_log, dt_bias, gnorm_a, gnorm_b, hgrn_lower_bounds, w_branch_a, w_branch_b, w_out, final_norm_w, loss_target, m_meta_tokens, m_norm_w, m_w_in, m_conv_w, m_a_log, m_dt_bias, m_gnorm_a, m_gnorm_b, m_hgrn_lower_bounds, m_w_branch_a, m_w_branch_b, m_w_out, m_final_norm_w, v_meta_tokens, v_norm_w, v_w_in, v_conv_w, v_a_log, v_dt_bias, v_gnorm_a, v_gnorm_b, v_hgrn_lower_bounds, v_w_branch_a, v_w_branch_b, v_w_out, v_final_norm_w):
    given = dict(x=x, meta_tokens=meta_tokens, norm_w=norm_w, w_in=w_in, conv_w=conv_w, a_log=a_log, dt_bias=dt_bias, gnorm_a=gnorm_a, gnorm_b=gnorm_b, hgrn_lower_bounds=hgrn_lower_bounds, w_branch_a=w_branch_a, w_branch_b=w_branch_b, w_out=w_out, final_norm_w=final_norm_w, loss_target=loss_target, m_meta_tokens=m_meta_tokens, m_norm_w=m_norm_w, m_w_in=m_w_in, m_conv_w=m_conv_w, m_a_log=m_a_log, m_dt_bias=m_dt_bias, m_gnorm_a=m_gnorm_a, m_gnorm_b=m_gnorm_b, m_hgrn_lower_bounds=m_hgrn_lower_bounds, m_w_branch_a=m_w_branch_a, m_w_branch_b=m_w_branch_b, m_w_out=m_w_out, m_final_norm_w=m_final_norm_w, v_meta_tokens=v_meta_tokens, v_norm_w=v_norm_w, v_w_in=v_w_in, v_conv_w=v_conv_w, v_a_log=v_a_log, v_dt_bias=v_dt_bias, v_gnorm_a=v_gnorm_a, v_gnorm_b=v_gnorm_b, v_hgrn_lower_bounds=v_hgrn_lower_bounds, v_w_branch_a=v_w_branch_a, v_w_branch_b=v_w_branch_b, v_w_out=v_w_out, v_final_norm_w=v_final_norm_w)
    weights = {n: given[n] for n in TWIN_WEIGHTS}
    shared = {n: given[n] for n in SHARED_INPUTS}
    per_example = {n: given[n] for n in ['x']}
    grad_fn = _jax.value_and_grad(_loss, argnums=(0, 1))

    def one_microbatch(ex, loss_target):
        ex = dict(ex)
        diff = ex.pop(TWIN_DIFF_INPUT)
        return grad_fn(weights, diff, {**shared, **ex}, loss_target)

    if N_MICROBATCH == 1:
        loss, (grad_w, grad_x) = one_microbatch(per_example, given["loss_target"])
    else:
        def body(carry, xs):
            loss_sum, grad_sum = carry
            l_k, (gw_k, gx_k) = one_microbatch(xs[0], xs[1])
            with _jax.named_scope("update"):
                return (loss_sum + l_k, _jax.tree.map(_jnp.add, grad_sum, gw_k)), gx_k

        init = (_jnp.zeros((), _jnp.float32), _jax.tree.map(_jnp.zeros_like, weights))
        (loss, grad_w), grad_x = _jax.lax.scan(body, init, (per_example, given["loss_target"]))
    with _jax.named_scope("update"):
        delta_w, new_m, new_v = {}, {}, {}
        for n in TWIN_WEIGHTS:
            delta_w[n], new_m[n], new_v[n] = _adamw(weights[n], grad_w[n], given["m_" + n], given["v_" + n])
    return (loss, grad_x, *[grad_w[n] for n in TWIN_WEIGHTS], *[delta_w[n] for n in TWIN_WEIGHTS],
            *[new_m[n] for n in TWIN_WEIGHTS], *[new_v[n] for n in TWIN_WEIGHTS])
```

```python
import functools

import jax
import jax.numpy as jnp
import numpy as np
from jax import lax
from jax.experimental import pallas as pl
from jax.experimental.pallas import tpu as pltpu

F32 = jnp.float32
BF16 = jnp.bfloat16

D_MODEL = 1024
N_HEADS = 4
D_HEAD = 128
HEADS_W = N_HEADS * D_HEAD
N_META = 16
N_PAD = 48
GDN_CHUNK = 64
HGRN_CHUNK = 16
EPS = 1e-6
N_DEV = 8
LANES = 128
SUBLANES = 8
VMEM_LIMIT = 56 * 1024 * 1024

C_QKV, C_Z, C_BQ, C_BF, C_BI, C_BG, C_GA, C_GB, C_MISC = 0, 1536, 2048, 2560, 3072, 3584, 4096, 5120, 6144
PROJ_W = 6272
REF_W = 6152

ADAM_LR, ADAM_B1, ADAM_B2, ADAM_EPS, ADAM_WD, ADAM_STEP = 0.001, 0.9, 0.999, 1e-08, 0.01, 10

MESH = pl.DeviceIdType.MESH
SDS = jax.ShapeDtypeStruct
BS = pl.BlockSpec


def _params(n_axes):
    return pltpu.CompilerParams(dimension_semantics=("arbitrary",) * n_axes, vmem_limit_bytes=VMEM_LIMIT)


def _pick(n, cands):
    for c in cands:
        if n % c == 0:
            return c
    raise ValueError(f"no tile for {n} among {cands}")


def _dg(a, b, dims, precise):
    if precise:
        return lax.dot_general(a, b, (dims, ((), ())), precision=lax.Precision.HIGHEST, preferred_element_type=F32)
    return lax.dot_general(a.astype(BF16), b.astype(BF16), (dims, ((), ())), preferred_element_type=F32)


def _make_mm(precise):
    @jax.custom_vjp
    def nn(a, b):
        return _dg(a, b, ((1,), (0,)), precise)

    @jax.custom_vjp
    def nt(a, b):
        return _dg(a, b, ((1,), (1,)), precise)

    @jax.custom_vjp
    def tn(a, b):
        return _dg(a, b, ((0,), (0,)), precise)

    nn.defvjp(lambda a, b: (nn(a, b), (a, b)), lambda r, g: (nt(g, r[1]), tn(r[0], g)))
    nt.defvjp(lambda a, b: (nt(a, b), (a, b)), lambda r, g: (nn(g, r[1]), tn(g, r[0])))
    tn.defvjp(lambda a, b: (tn(a, b), (a, b)), lambda r, g: (nt(r[1], g), nn(r[0], g)))
    return nn, nt, tn


_nn, _nt, _tn = _make_mm(False)
_nnp, _ntp, _tnp = _make_mm(True)


def _iota2(shape, dim):
    return lax.broadcasted_iota(jnp.int32, shape, dim)


def _sigmoid(x):
    return jax.nn.sigmoid(x)


def _silu(x):
    return x * _sigmoid(x)


def _softplus(x):
    return jnp.maximum(x, 0.0) + jnp.log1p(jnp.exp(-jnp.abs(x)))


def _rms(x, w):
    return x * lax.rsqrt(jnp.mean(x * x, axis=-1, keepdims=True) + EPS) * w


def _inv_unit_lower(lm):
    n = lm.shape[0]
    a = (_iota2((n, n), 0) == _iota2((n, n), 1)).astype(F32) - lm
    p = lm
    for _ in range(max(1, (n - 1).bit_length()) - 1):
        p = _nn(p, p)
        a = a + _nn(a, p)
    return a


def _gdn_head(q, k, v, b_b, g_b, s):
    n = q.shape[0]
    r, c = _iota2((n, n), 0), _iota2((n, n), 1)
    causal, strict, eye = r >= c, r > c, r == c
    sel = (_iota2((q.shape[1], n), 0) == 0).astype(F32)
    g_cum = _nnp(causal.astype(F32), g_b)
    g_i = _nnp(g_cum, sel)
    g_j = jnp.sum(jnp.where(eye, g_i, 0.0), axis=0, keepdims=True)
    decay = jnp.where(causal, jnp.exp(jnp.where(causal, g_i - g_j, 0.0)), 0.0)
    e_g = jnp.exp(g_cum)
    kb = k * b_b
    a_inv = _inv_unit_lower(jnp.where(strict, _nt(kb, k) * decay, 0.0))
    u = _nn(a_inv, v * b_b)
    w = _nn(a_inv, kb * e_g)
    v_new = u - _nn(w, s)
    o = _nn(q * e_g, s) + _nn(_nt(q, k) * decay, v_new)
    g_last = jnp.sum(jnp.where(_iota2(g_cum.shape, 0) == n - 1, g_cum, 0.0), axis=0, keepdims=True)
    s_new = s * jnp.exp(g_last) + _tn(k * jnp.exp(g_last - g_cum), v_new)
    return o, s_new


def _hgrn_head(q, k, v, lf, st):
    n = q.shape[0]
    b_cum = _nnp((_iota2((n, n), 0) >= _iota2((n, n), 1)).astype(F32), lf)
    o = _nt(q * jnp.exp(b_cum), st)
    rows = _iota2(b_cum.shape, 0)
    for j in range(n):
        pick = rows == j
        b_j = jnp.sum(jnp.where(pick, b_cum, 0.0), axis=0, keepdims=True)
        k_j = jnp.sum(jnp.where(pick, k, 0.0), axis=0, keepdims=True)
        v_j = jnp.sum(jnp.where(pick, v, 0.0), axis=0, keepdims=True)
        m = rows >= j
        p = jnp.where(m, jnp.exp(jnp.where(m, b_cum - b_j, 0.0)), 0.0)
        o = o + jnp.sum(q * k_j * p, axis=1, keepdims=True) * v_j
    b_last = jnp.sum(jnp.where(rows == n - 1, b_cum, 0.0), axis=0, keepdims=True)
    st_new = st * jnp.exp(b_last) + _tn(v, k * jnp.exp(b_last - b_cum))
    return o, st_new


def _l2n_act(y, scale):
    a = _silu(y)
    return a * lax.rsqrt(jnp.sum(a * a, axis=-1, keepdims=True) + EPS) * scale


def _col(x, lane):
    return jnp.sum(jnp.where(_iota2(x.shape, 1) == lane, x, 0.0), axis=1, keepdims=True)


def _elem(x, row, lane):
    m = (_iota2(x.shape, 0) == row) & (_iota2(x.shape, 1) == lane)
    return jnp.sum(jnp.sum(jnp.where(m, x, 0.0), axis=1, keepdims=True), axis=0, keepdims=True)


def _gdn_gates(misc, aux, real, head):
    beta = _sigmoid(_col(misc, head))
    g = -jnp.exp(_elem(aux, 0, head)) * _softplus(_col(misc, N_HEADS + head) + _elem(aux, 1, head))
    g = jnp.where(real, g, 0.0)
    shape = (misc.shape[0], D_HEAD)
    return jnp.broadcast_to(beta, shape), jnp.broadcast_to(g, shape)


def _hgrn_prep(bq, bf, lb, real):
    qb = _silu(bq) * (D_HEAD ** -0.5)
    log_sig = jnp.minimum(bf, 0.0) - jnp.log1p(jnp.exp(-jnp.abs(bf)))
    pos = lb > 0.0
    lbs = jnp.where(pos, lb, 0.5)
    a = jnp.log(lbs)
    b = jnp.log1p(-lbs) + log_sig
    lae = jnp.maximum(a, b) + jnp.log1p(jnp.exp(-jnp.abs(a - b)))
    lf = jnp.where(pos, lae, log_sig)
    kb = jnp.where(pos, 1.0 - lbs, 1.0) * _sigmoid(-bf)
    return qb, jnp.where(real, kb, 0.0), jnp.where(real, lf, 0.0)


def _gated_norm(o, z, gw):
    return o * lax.rsqrt(jnp.mean(o * o, axis=-1, keepdims=True) + EPS) * gw * _silu(z)


def _shift_down(x, j):
    return x if j == 0 else pltpu.roll(x, j, 0)


def _shift_up(x, j):
    return x if j == 0 else pltpu.roll(x, x.shape[0] - j, 0)


def _all_gather_hbm(block, name):
    r, c = block.shape

    def body(x_ref, out_ref, send_sems, recv_sems, local_sem):
        mx, my, mc = lax.axis_index("x"), lax.axis_index("y"), lax.axis_index("c")
        me, sibling = (mx, my, mc), (mx, my, 1 - mc)
        chips = [(1 - mx, my), (mx, 1 - my), (1 - mx, 1 - my)]

        def slab(px, py, pc):
            return out_ref.at[4 * px + 2 * py + pc]

        def copy(k, blk, to, src=None):
            return pltpu.make_async_remote_copy(
                src_ref=slab(*blk) if src is None else src, dst_ref=slab(*blk),
                send_sem=send_sems.at[k], recv_sem=recv_sems.at[k], device_id=to, device_id_type=MESH)

        mine = pltpu.make_async_copy(x_ref, slab(*me), local_sem)
        mine.start()
        first = [copy(0, me, sibling, src=x_ref)]
        first += [copy(1 + j, me, (*chip, mc), src=x_ref) for j, chip in enumerate(chips)]
        for cp in first:
            cp.start()
        passed = [copy(4 + j, (*chip, mc), sibling) for j, chip in enumerate(chips)]
        for j, chip in enumerate(chips):
            copy(1 + j, (*chip, mc), me).wait_recv()
            passed[j].start()
        copy(0, sibling, me).wait_recv()
        for j, chip in enumerate(chips):
            copy(4 + j, (*chip, 1 - mc), me).wait_recv()
        for cp in first + passed:
            cp.wait_send()
        mine.wait()

    return pl.pallas_call(
        body, name=name, out_shape=SDS((N_DEV, r, c), block.dtype),
        in_specs=[BS(memory_space=pl.ANY)], out_specs=BS(memory_space=pl.ANY),
        scratch_shapes=[pltpu.SemaphoreType.DMA((7,)), pltpu.SemaphoreType.DMA((7,)), pltpu.SemaphoreType.DMA],
    )(block)


def _all_reduce_small(block, name):
    r, c = block.shape

    def body(x_ref, out_ref, buf, send_sems, recv_sems):
        mx, my, mc = lax.axis_index("x"), lax.axis_index("y"), lax.axis_index("c")
        me, sibling = (mx, my, mc), (mx, my, 1 - mc)
        chips = [(1 - mx, my), (mx, 1 - my), (1 - mx, 1 - my)]

        def slab(px, py, pc):
            return buf.at[4 * px + 2 * py + pc]

        def copy(k, blk, to, src=None):
            return pltpu.make_async_remote_copy(
                src_ref=slab(*blk) if src is None else src, dst_ref=slab(*blk),
                send_sem=send_sems.at[k], recv_sem=recv_sems.at[k], device_id=to, device_id_type=MESH)

        first = [copy(0, me, sibling, src=x_ref)]
        first += [copy(1 + j, me, (*chip, mc), src=x_ref) for j, chip in enumerate(chips)]
        for cp in first:
            cp.start()
        passed = [copy(4 + j, (*chip, mc), sibling) for j, chip in enumerate(chips)]
        for j, chip in enumerate(chips):
            copy(1 + j, (*chip, mc), me).wait_recv()
            passed[j].start()
        copy(0, sibling, me).wait_recv()
        for j, chip in enumerate(chips):
            copy(4 + j, (*chip, 1 - mc), me).wait_recv()
        for cp in first + passed:
            cp.wait_send()
        buf[4 * mx + 2 * my + mc] = x_ref[...]
        acc = buf[0]
        for d in range(1, N_DEV):
            acc = acc + buf[d]
        out_ref[...] = acc

    return pl.pallas_call(
        body, name=name, out_shape=SDS((r, c), F32),
        in_specs=[BS(memory_space=pltpu.VMEM)], out_specs=BS(memory_space=pltpu.VMEM),
        scratch_shapes=[pltpu.VMEM((N_DEV, r, c), F32), pltpu.SemaphoreType.DMA((7,)), pltpu.SemaphoreType.DMA((7,))],
    )(block)


def _exchange(buf, n_send, flip, name):
    _, r, c = buf.shape

    def body(g_ref, out_ref, send_sems, recv_sems):
        pos = [lax.axis_index("x"), lax.axis_index("y"), lax.axis_index("c")]
        pos[flip] = 1 - pos[flip]
        copies = [pltpu.make_async_remote_copy(
            src_ref=g_ref.at[2 * i + 1], dst_ref=out_ref.at[i], send_sem=send_sems.at[i], recv_sem=recv_sems.at[i],
            device_id=tuple(pos), device_id_type=MESH) for i in range(n_send)]
        for cp in copies:
            cp.start()
        for cp in copies:
            cp.wait_recv()
        for cp in copies:
            cp.wait_send()

    return pl.pallas_call(
        body, name=name, out_shape=SDS((n_send, r, c), buf.dtype),
        in_specs=[BS(memory_space=pl.ANY)], out_specs=BS(memory_space=pl.ANY),
        scratch_shapes=[pltpu.SemaphoreType.DMA((n_send,)), pltpu.SemaphoreType.DMA((n_send,))],
    )(buf)


def _add_even(buf, recv, name):
    n, r, c = recv.shape
    tr = _pick(r, (512, 256, 128, 64, 32, 16, 8))

    def body(a_ref, b_ref, o_ref):
        o_ref[...] = a_ref[...] + b_ref[...]

    return pl.pallas_call(
        body, name=name, grid=(n, r // tr), out_shape=SDS((n, r, c), F32),
        in_specs=[BS((None, tr, c), lambda i, j: (2 * i, j, 0)), BS((None, tr, c), lambda i, j: (i, j, 0))],
        out_specs=BS((None, tr, c), lambda i, j: (i, j, 0)), compiler_params=_params(2),
    )(buf, recv)


def _reduce_scatter(g_rel, tag):
    got = _exchange(g_rel, 4, 2, f"rs_c_{tag}")
    half = _add_even(g_rel, got, f"rs_c_add_{tag}")
    got = _exchange(half, 2, 1, f"rs_y_{tag}")
    quarter = _add_even(half, got, f"rs_y_add_{tag}")
    got = _exchange(quarter, 1, 0, f"rs_x_{tag}")
    return _add_even(quarter, got, f"rs_x_add_{tag}")[0]


def _proj_fwd(h, nw8, wp, tag):
    n = h.shape[0]
    tm = _pick(n, (768, 512, 384, 256, 192, 128, 64))
    tn = 896

    def body(h_ref, nw_ref, w_ref, proj_ref, xn_ref):
        @pl.when(pl.program_id(1) == 0)
        def _():
            xn_ref[...] = _rms(h_ref[...], nw_ref[0:1, :]).astype(BF16)

        proj_ref[...] = jnp.dot(xn_ref[...], w_ref[...], preferred_element_type=F32)

    return pl.pallas_call(
        body, name=f"proj_fwd_{tag}", grid=(n // tm, PROJ_W // tn),
        in_specs=[BS((tm, D_MODEL), lambda i, j: (i, 0)), BS((SUBLANES, D_MODEL), lambda i, j: (0, 0)),
                  BS((D_MODEL, tn), lambda i, j: (0, j))],
        out_specs=[BS((tm, tn), lambda i, j: (i, j)), BS((tm, D_MODEL), lambda i, j: (i, 0))],
        out_shape=[SDS((n, PROJ_W), F32), SDS((n, D_MODEL), BF16)], compiler_params=_params(2),
    )(h, nw8, wp)


def _conv_ext(x_ext, cw_ref):
    y = x_ext * cw_ref[3:4, :]
    for k in range(3):
        y = y + _shift_down(x_ext, 3 - k) * cw_ref[k:k + 1, :]
    return y[SUBLANES:]


def _prep_fwd(proj, cw8, aux, lb8, nseq, t_len, tag):
    n = proj.shape[0]
    tt = _pick(t_len, (192, 128, 64))
    nt_ = t_len // tt
    qkv_w = 3 * HEADS_W

    def body(cur_ref, prev_ref, misc_ref, bq_ref, bf_ref, cw_ref, aux_ref, lb_ref,
             q_ref, k_ref, v_ref, b_ref, g_ref, qb_ref, kb_ref, lf_ref):
        t = pl.program_id(1)
        prev = jnp.where(t == 0, 0.0, prev_ref[...])
        y = _conv_ext(jnp.concatenate([prev, cur_ref[...]], axis=0), cw_ref)
        real = (t * tt + _iota2((tt, 1), 0)) >= N_PAD
        misc = misc_ref[...]
        auxv = aux_ref[...]
        for hd in range(N_HEADS):
            sl = slice(hd * D_HEAD, (hd + 1) * D_HEAD)
            q_ref[:, sl] = _l2n_act(y[:, sl], D_HEAD ** -0.5)
            k_ref[:, sl] = _l2n_act(y[:, HEADS_W + hd * D_HEAD:HEADS_W + (hd + 1) * D_HEAD], 1.0)
            v_ref[:, sl] = _silu(y[:, 2 * HEADS_W + hd * D_HEAD:2 * HEADS_W + (hd + 1) * D_HEAD])
            b_ref[:, sl], g_ref[:, sl] = _gdn_gates(misc, auxv, real, hd)
        qb_ref[...], kb_ref[...], lf_ref[...] = _hgrn_prep(bq_ref[...], bf_ref[...], lb_ref[0:1, :], real)

    rb = tt // SUBLANES
    row = lambda s, t: s * nt_ + t
    wide = BS((tt, HEADS_W), lambda s, t: (row(s, t), 0))
    return pl.pallas_call(
        body, name=f"prep_fwd_{tag}", grid=(nseq, nt_),
        in_specs=[BS((tt, qkv_w), lambda s, t: (row(s, t), 0)),
                  BS((SUBLANES, qkv_w), lambda s, t: (jnp.maximum(row(s, t) * rb - 1, 0), 0)),
                  BS((tt, LANES), lambda s, t: (row(s, t), C_MISC // LANES)),
                  BS((tt, HEADS_W), lambda s, t: (row(s, t), C_BQ // HEADS_W)),
                  BS((tt, HEADS_W), lambda s, t: (row(s, t), C_BF // HEADS_W)),
                  BS((SUBLANES, qkv_w), lambda s, t: (0, 0)), BS((SUBLANES, LANES), lambda s, t: (0, 0)),
                  BS((SUBLANES, HEADS_W), lambda s, t: (0, 0))],
        out_specs=[wide] * 8, out_shape=[SDS((n, HEADS_W), F32)] * 8, compiler_params=_params(2),
    )(proj, proj, proj, proj, proj, cw8, aux, lb8)


def _gdn_fwd(q, k, v, b, g, nseq, t_len, tag):
    n = q.shape[0]
    nc = t_len // GDN_CHUNK

    def body(q_ref, k_ref, v_ref, b_ref, g_ref, o_ref, sck_ref, s_ref):
        @pl.when(pl.program_id(1) == 0)
        def _():
            s_ref[...] = jnp.zeros_like(s_ref)

        for hd in range(N_HEADS):
            sl = slice(hd * D_HEAD, (hd + 1) * D_HEAD)
            s = s_ref[hd]
            sck_ref[hd] = s
            o, s_new = _gdn_head(q_ref[:, sl], k_ref[:, sl], v_ref[:, sl], b_ref[:, sl], g_ref[:, sl], s)
            o_ref[:, sl] = o
            s_ref[hd] = s_new

    blk = BS((GDN_CHUNK, HEADS_W), lambda s, c: (s * nc + c, 0))
    return pl.pallas_call(
        body, name=f"gdn_fwd_{tag}", grid=(nseq, nc), in_specs=[blk] * 5,
        out_specs=[blk, BS((None, None, N_HEADS, D_HEAD, D_HEAD), lambda s, c: (s, c, 0, 0, 0))],
        out_shape=[SDS((n, HEADS_W), F32), SDS((nseq, nc, N_HEADS, D_HEAD, D_HEAD), F32)],
        scratch_shapes=[pltpu.VMEM((N_HEADS, D_HEAD, D_HEAD), F32)], compiler_params=_params(2),
    )(q, k, v, b, g)


def _gdn_bwd(q, k, v, b, g, sck, do, nseq, t_len, tag):
    n = q.shape[0]
    nc = t_len // GDN_CHUNK

    def body(q_ref, k_ref, v_ref, b_ref, g_ref, sck_ref, do_ref, dq_ref, dk_ref, dv_ref, db_ref, dg_ref, ds_ref):
        @pl.when(pl.program_id(1) == 0)
        def _():
            ds_ref[...] = jnp.zeros_like(ds_ref)

        for hd in range(N_HEADS):
            sl = slice(hd * D_HEAD, (hd + 1) * D_HEAD)
            _, vjp = jax.vjp(_gdn_head, q_ref[:, sl], k_ref[:, sl], v_ref[:, sl], b_ref[:, sl], g_ref[:, sl],
                             sck_ref[hd])
            dq, dk, dv, db, dg, ds = vjp((do_ref[:, sl], ds_ref[hd]))
            dq_ref[:, sl], dk_ref[:, sl], dv_ref[:, sl], db_ref[:, sl], dg_ref[:, sl] = dq, dk, dv, db, dg
            ds_ref[hd] = ds

    blk = BS((GDN_CHUNK, HEADS_W), lambda s, c: (s * nc + nc - 1 - c, 0))
    return pl.pallas_call(
        body, name=f"gdn_bwd_{tag}", grid=(nseq, nc),
        in_specs=[blk] * 5 + [BS((None, None, N_HEADS, D_HEAD, D_HEAD), lambda s, c: (s, nc - 1 - c, 0, 0, 0)), blk],
        out_specs=[blk] * 5, out_shape=[SDS((n, HEADS_W), F32)] * 5,
        scratch_shapes=[pltpu.VMEM((N_HEADS, D_HEAD, D_HEAD), F32)], compiler_params=_params(2),
    )(q, k, v, b, g, sck, do)


HGRN_SUB = GDN_CHUNK // HGRN_CHUNK


def _hgrn_fwd(q, k, v, v_col, lf, nseq, t_len, tag):
    n = q.shape[0]
    nc = t_len // GDN_CHUNK

    def body(q_ref, k_ref, v_ref, lf_ref, o_ref, sck_ref, s_ref):
        @pl.when(pl.program_id(1) == 0)
        def _():
            s_ref[...] = jnp.zeros_like(s_ref)

        for sub in range(HGRN_SUB):
            rs = slice(sub * HGRN_CHUNK, (sub + 1) * HGRN_CHUNK)
            for hd in range(N_HEADS):
                sl = slice(hd * D_HEAD, (hd + 1) * D_HEAD)
                s = s_ref[hd]
                sck_ref[sub, hd] = s
                o, s_new = _hgrn_head(q_ref[rs, sl], k_ref[rs, sl], v_ref[rs, sl], lf_ref[rs, sl], s)
                o_ref[rs, sl] = o
                s_ref[hd] = s_new

    blk = BS((GDN_CHUNK, HEADS_W), lambda s, c: (s * nc + c, 0))
    return pl.pallas_call(
        body, name=f"hgrn_fwd_{tag}", grid=(nseq, nc),
        in_specs=[blk, blk, BS((GDN_CHUNK, HEADS_W), lambda s, c: (s * nc + c, v_col)), blk],
        out_specs=[blk, BS((None, HGRN_SUB, N_HEADS, D_HEAD, D_HEAD), lambda s, c: (s, c, 0, 0, 0))],
        out_shape=[SDS((n, HEADS_W), F32), SDS((nseq, nc * HGRN_SUB, N_HEADS, D_HEAD, D_HEAD), F32)],
        scratch_shapes=[pltpu.VMEM((N_HEADS, D_HEAD, D_HEAD), F32)], compiler_params=_params(2),
    )(q, k, v, lf)


def _hgrn_bwd(q, k, v, v_col, lf, sck, do, nseq, t_len, tag):
    n = q.shape[0]
    nc = t_len // GDN_CHUNK

    def body(q_ref, k_ref, v_ref, lf_ref, sck_ref, do_ref, dq_ref, dk_ref, dv_ref, dlf_ref, ds_ref):
        @pl.when(pl.program_id(1) == 0)
        def _():
            ds_ref[...] = jnp.zeros_like(ds_ref)

        for sub in reversed(range(HGRN_SUB)):
            rs = slice(sub * HGRN_CHUNK, (sub + 1) * HGRN_CHUNK)
            for hd in range(N_HEADS):
                sl = slice(hd * D_HEAD, (hd + 1) * D_HEAD)
                _, vjp = jax.vjp(_hgrn_head, q_ref[rs, sl], k_ref[rs, sl], v_ref[rs, sl], lf_ref[rs, sl],
                                 sck_ref[sub, hd])
                dq, dk, dv, dlf, ds = vjp((do_ref[rs, sl], ds_ref[hd]))
                dq_ref[rs, sl], dk_ref[rs, sl], dlf_ref[rs, sl] = dq, dk, dlf
                dv_ref[rs, sl] = dv.astype(BF16)
                ds_ref[hd] = ds

    blk = BS((GDN_CHUNK, HEADS_W), lambda s, c: (s * nc + nc - 1 - c, 0))
    return pl.pallas_call(
        body, name=f"hgrn_bwd_{tag}", grid=(nseq, nc),
        in_specs=[blk, blk, BS((GDN_CHUNK, HEADS_W), lambda s, c: (s * nc + nc - 1 - c, v_col)), blk,
                  BS((None, HGRN_SUB, N_HEADS, D_HEAD, D_HEAD), lambda s, c: (s, nc - 1 - c, 0, 0, 0)), blk],
        out_specs=[blk] * 4,
        out_shape=[SDS((n, HEADS_W), F32), SDS((n, HEADS_W), F32), SDS((n, HEADS_W), BF16), SDS((n, HEADS_W), F32)],
        scratch_shapes=[pltpu.VMEM((N_HEADS, D_HEAD, D_HEAD), F32)], compiler_params=_params(2),
    )(q, k, v, lf, sck, do)


def _post_values(oa_ref, ob_ref, z_ref, bg_ref, ga_ref, gb_ref, gn_ref, wa_ref, wb_ref, ya_ref, yb_ref):
    for hd in range(N_HEADS):
        sl = slice(hd * D_HEAD, (hd + 1) * D_HEAD)
        ya_ref[:, sl] = _gated_norm(oa_ref[:, sl], z_ref[:, sl], gn_ref[0:1, :]).astype(BF16)
        yb_ref[:, sl] = _gated_norm(ob_ref[:, sl], bg_ref[:, sl], gn_ref[1:2, :]).astype(BF16)
    pa = jnp.dot(ya_ref[...], wa_ref[...], preferred_element_type=F32)
    pb = jnp.dot(yb_ref[...], wb_ref[...], preferred_element_type=F32)
    return pa, pb, _sigmoid(ga_ref[...]), _sigmoid(gb_ref[...])


def _post_specs(tm):
    r2 = lambda i: (i, 0)
    return [BS((tm, HEADS_W), r2), BS((tm, HEADS_W), r2),
            BS((tm, HEADS_W), lambda i: (i, C_Z // HEADS_W)), BS((tm, HEADS_W), lambda i: (i, C_BG // HEADS_W)),
            BS((tm, D_MODEL), lambda i: (i, C_GA // D_MODEL)), BS((tm, D_MODEL), lambda i: (i, C_GB // D_MODEL)),
            BS((tm, D_MODEL), r2), BS((SUBLANES, LANES), lambda i: (0, 0))]


def _post_fwd(oa, ob, proj, h, gn8, wa, wb, wout, tag):
    n = h.shape[0]
    tm = _pick(n, (256, 192, 128, 64))

    def body(oa_ref, ob_ref, z_ref, bg_ref, ga_ref, gb_ref, h_ref, gn_ref, wa_ref, wb_ref, wout_ref, out_ref,
             ya_ref, yb_ref):
        pa, pb, sa, sb = _post_values(oa_ref, ob_ref, z_ref, bg_ref, ga_ref, gb_ref, gn_ref, wa_ref, wb_ref,
                                      ya_ref, yb_ref)
        mixed = (sa * pa + sb * pb).astype(BF16)
        out_ref[...] = h_ref[...] + jnp.dot(mixed, wout_ref[...], preferred_element_type=F32)

    full = lambda i: (0, 0)
    return pl.pallas_call(
        body, name=f"post_fwd_{tag}", grid=(n // tm,),
        in_specs=_post_specs(tm) + [BS((HEADS_W, D_MODEL), full), BS((HEADS_W, D_MODEL), full),
                                    BS((D_MODEL, D_MODEL), full)],
        out_specs=BS((tm, D_MODEL), lambda i: (i, 0)), out_shape=SDS((n, D_MODEL), F32),
        scratch_shapes=[pltpu.VMEM((tm, HEADS_W), BF16), pltpu.VMEM((tm, HEADS_W), BF16)], compiler_params=_params(1),
    )(oa, ob, proj, proj, proj, proj, h, gn8, wa, wb, wout)


def _post_bwd(dh, oa, ob, proj, h, gn8, wa, wb, wa_t, wb_t, wout_t, tag):
    n = h.shape[0]
    tm = _pick(n, (256, 192, 128, 64))

    def body(dh_ref, oa_ref, ob_ref, z_ref, bg_ref, ga_ref, gb_ref, h_ref, gn_ref, wa_ref, wb_ref, wat_ref, wbt_ref,
             woutt_ref, doa_ref, dob_ref, dz_ref, dbg_ref, dga_ref, dgb_ref, dwa_ref, dwb_ref, dwout_ref, dgn_ref,
             ya_ref, yb_ref):
        @pl.when(pl.program_id(0) == 0)
        def _():
            dwa_ref[...] = jnp.zeros_like(dwa_ref)
            dwb_ref[...] = jnp.zeros_like(dwb_ref)
            dwout_ref[...] = jnp.zeros_like(dwout_ref)
            dgn_ref[...] = jnp.zeros_like(dgn_ref)

        pa, pb, sa, sb = _post_values(oa_ref, ob_ref, z_ref, bg_ref, ga_ref, gb_ref, gn_ref, wa_ref, wb_ref,
                                      ya_ref, yb_ref)
        mixed = (sa * pa + sb * pb).astype(BF16)
        dout = dh_ref[...].astype(BF16)
        dwout_ref[...] += _dg(mixed, dout, ((0,), (0,)), False)
        dmixed = jnp.dot(dout, woutt_ref[...], preferred_element_type=F32)
        dga_ref[...] = (dmixed * pa * sa * (1.0 - sa)).astype(BF16)
        dgb_ref[...] = (dmixed * pb * sb * (1.0 - sb)).astype(BF16)
        dpa = (dmixed * sa).astype(BF16)
        dpb = (dmixed * sb).astype(BF16)
        dwa_ref[...] += _dg(ya_ref[...], dpa, ((0,), (0,)), False)
        dwb_ref[...] += _dg(yb_ref[...], dpb, ((0,), (0,)), False)
        dya = jnp.dot(dpa, wat_ref[...], preferred_element_type=F32)
        dyb = jnp.dot(dpb, wbt_ref[...], preferred_element_type=F32)
        dgn_a = jnp.zeros((1, D_HEAD), F32)
        dgn_b = jnp.zeros((1, D_HEAD), F32)
        for hd in range(N_HEADS):
            sl = slice(hd * D_HEAD, (hd + 1) * D_HEAD)
            _, vjp = jax.vjp(_gated_norm, oa_ref[:, sl], z_ref[:, sl], gn_ref[0:1, :])
            doa, dz, dgw = vjp(dya[:, sl])
            doa_ref[:, sl], dz_ref[:, sl], dgn_a = doa, dz.astype(BF16), dgn_a + dgw
            _, vjp = jax.vjp(_gated_norm, ob_ref[:, sl], bg_ref[:, sl], gn_ref[1:2, :])
            dob, dbg, dgw = vjp(dyb[:, sl])
            dob_ref[:, sl], dbg_ref[:, sl], dgn_b = dob, dbg.astype(BF16), dgn_b + dgw
        dgn_ref[0:1, :] += dgn_a
        dgn_ref[1:2, :] += dgn_b

    full = lambda i: (0, 0)
    r2 = lambda i: (i, 0)
    return pl.pallas_call(
        body, name=f"post_bwd_{tag}", grid=(n // tm,),
        in_specs=[BS((tm, D_MODEL), r2)] + _post_specs(tm) + [
            BS((HEADS_W, D_MODEL), full), BS((HEADS_W, D_MODEL), full), BS((D_MODEL, HEADS_W), full),
            BS((D_MODEL, HEADS_W), full), BS((D_MODEL, D_MODEL), full)],
        out_specs=[BS((tm, HEADS_W), r2)] * 4 + [BS((tm, D_MODEL), r2)] * 2 + [
            BS((HEADS_W, D_MODEL), full), BS((HEADS_W, D_MODEL), full), BS((D_MODEL, D_MODEL), full),
            BS((SUBLANES, LANES), full)],
        out_shape=[SDS((n, HEADS_W), F32), SDS((n, HEADS_W), F32), SDS((n, HEADS_W), BF16), SDS((n, HEADS_W), BF16),
                   SDS((n, D_MODEL), BF16), SDS((n, D_MODEL), BF16), SDS((HEADS_W, D_MODEL), F32),
                   SDS((HEADS_W, D_MODEL), F32), SDS((D_MODEL, D_MODEL), F32), SDS((SUBLANES, LANES), F32)],
        scratch_shapes=[pltpu.VMEM((tm, HEADS_W), BF16), pltpu.VMEM((tm, HEADS_W), BF16)], compiler_params=_params(1),
    )(dh, oa, ob, proj, proj, proj, proj, h, gn8, wa, wb, wa_t, wb_t, wout_t)


def _loss_head(h, fw8, target, nseq, t_len):
    n = h.shape[0]
    nc = t_len // GDN_CHUNK
    inv_d = 1.0 / D_MODEL

    def body(h_ref, fw_ref, tgt_ref, dh_ref, acc_ref):
        @pl.when((pl.program_id(0) == 0) & (pl.program_id(1) == 0))
        def _():
            acc_ref[...] = jnp.zeros_like(acc_ref)

        frames = (pl.program_id(1) > 0).astype(F32)
        y, vjp = jax.vjp(_rms, h_ref[...], fw_ref[0:1, :])
        err = (y - tgt_ref[...]) * frames
        dx, dfw = vjp(err * inv_d)
        dh_ref[...] = dx
        acc_ref[0:1, :] += dfw
        acc_ref[1:2, :] += (0.5 * inv_d) * jnp.sum(err * err, axis=0, keepdims=True)

    return pl.pallas_call(
        body, name="loss_head", grid=(nseq, nc),
        in_specs=[BS((GDN_CHUNK, D_MODEL), lambda s, c: (s * nc + c, 0)), BS((SUBLANES, D_MODEL), lambda s, c: (0, 0)),
                  BS((None, GDN_CHUNK, D_MODEL), lambda s, c: (s, jnp.maximum(c - 1, 0), 0))],
        out_specs=[BS((GDN_CHUNK, D_MODEL), lambda s, c: (s * nc + c, 0)), BS((SUBLANES, D_MODEL), lambda s, c: (0, 0))],
        out_shape=[SDS((n, D_MODEL), F32), SDS((SUBLANES, D_MODEL), F32)], compiler_params=_params(2),
    )(h, fw8, target)


def _prep_bwd(proj, dq, dk, dv, db, dg, dqb, dkb, dlf, cw8, aux, lb8, nseq, t_len, tag):
    n = proj.shape[0]
    tt = _pick(t_len, (192, 128, 64))
    nt_ = t_len // tt
    qkv_w = 3 * HEADS_W
    rb = tt // SUBLANES
    ext = tt + SUBLANES

    def body(cur_ref, prev_ref, next_ref, misc_ref, bq_ref, bf_ref, dq_ref, dqn_ref, dk_ref, dkn_ref, dv_ref, dvn_ref,
             db_ref, dg_ref, dqb_ref, dkb_ref, dlf_ref, cw_ref, aux_ref, lb_ref,
             dqkv_ref, dmisc_ref, dbq_ref, dbf_ref, dcw_ref, daux_ref, dlb_ref, dy_ref):
        s, t = pl.program_id(0), pl.program_id(1)

        @pl.when((s == 0) & (t == 0))
        def _():
            dcw_ref[...] = jnp.zeros_like(dcw_ref)
            daux_ref[...] = jnp.zeros_like(daux_ref)
            dlb_ref[...] = jnp.zeros_like(dlb_ref)

        prev = jnp.where(t == 0, 0.0, prev_ref[...])
        x_ext = jnp.concatenate([prev, cur_ref[...], next_ref[...]], axis=0)
        y = _conv_ext(x_ext, cw_ref)
        inside = (t < nt_ - 1) | (_iota2((ext, 1), 0) < tt)
        dy_ref[0:SUBLANES, :] = jnp.zeros((SUBLANES, qkv_w), F32)
        for hd in range(N_HEADS):
            for grp, (g_ref, gn_ref, scale) in enumerate(((dq_ref, dqn_ref, D_HEAD ** -0.5), (dk_ref, dkn_ref, 1.0),
                                                          (dv_ref, dvn_ref, None))):
                lo = grp * HEADS_W + hd * D_HEAD
                sl = slice(hd * D_HEAD, (hd + 1) * D_HEAD)
                cot = jnp.concatenate([g_ref[:, sl], gn_ref[:, sl]], axis=0)
                fn = _silu if scale is None else functools.partial(_l2n_act, scale=scale)
                _, vjp = jax.vjp(fn, y[:, lo:lo + D_HEAD])
                dy_ref[SUBLANES:, lo:lo + D_HEAD] = jnp.where(inside, vjp(cot)[0], 0.0)
        dy_ext = dy_ref[...]
        dx = dy_ext * cw_ref[3:4, :]
        for kk in range(3):
            dx = dx + _shift_up(dy_ext, 3 - kk) * cw_ref[kk:kk + 1, :]
        dqkv_ref[...] = dx[SUBLANES:SUBLANES + tt].astype(BF16)
        dy_cur = dy_ext[SUBLANES:SUBLANES + tt]
        for kk in range(4):
            xs = _shift_down(x_ext, 3 - kk)[SUBLANES:SUBLANES + tt]
            dcw_ref[kk:kk + 1, :] += jnp.sum(xs * dy_cur, axis=0, keepdims=True)

        real = (t * tt + _iota2((tt, 1), 0)) >= N_PAD
        dmisc = jnp.zeros((tt, LANES), F32)
        daux = jnp.zeros((SUBLANES, LANES), F32)
        for hd in range(N_HEADS):
            sl = slice(hd * D_HEAD, (hd + 1) * D_HEAD)
            _, vjp = jax.vjp(lambda m, a: _gdn_gates(m, a, real, hd), misc_ref[...], aux_ref[...])
            dm, da = vjp((db_ref[:, sl], dg_ref[:, sl]))
            dmisc, daux = dmisc + dm, daux + da
        dmisc_ref[...] = dmisc.astype(BF16)
        daux_ref[...] += daux
        _, vjp = jax.vjp(lambda a, b, c: _hgrn_prep(a, b, c, real), bq_ref[...], bf_ref[...], lb_ref[0:1, :])
        dbq, dbf, dlb = vjp((dqb_ref[...], dkb_ref[...], dlf_ref[...]))
        dbq_ref[...], dbf_ref[...] = dbq.astype(BF16), dbf.astype(BF16)
        dlb_ref[0:1, :] += dlb

    row = lambda s, t: s * nt_ + t
    cur = lambda s, t: (row(s, t), 0)
    nxt = lambda s, t: (jnp.minimum((row(s, t) + 1) * rb, n // SUBLANES - 1), 0)
    wide = BS((tt, HEADS_W), cur)
    halo = BS((SUBLANES, HEADS_W), nxt)
    full = lambda s, t: (0, 0)
    return pl.pallas_call(
        body, name=f"prep_bwd_{tag}", grid=(nseq, nt_),
        in_specs=[BS((tt, qkv_w), cur), BS((SUBLANES, qkv_w), lambda s, t: (jnp.maximum(row(s, t) * rb - 1, 0), 0)),
                  BS((SUBLANES, qkv_w), nxt), BS((tt, LANES), lambda s, t: (row(s, t), C_MISC // LANES)),
                  BS((tt, HEADS_W), lambda s, t: (row(s, t), C_BQ // HEADS_W)),
                  BS((tt, HEADS_W), lambda s, t: (row(s, t), C_BF // HEADS_W)),
                  wide, halo, wide, halo, wide, halo, wide, wide, wide, wide, wide,
                  BS((SUBLANES, qkv_w), full), BS((SUBLANES, LANES), full), BS((SUBLANES, HEADS_W), full)],
        out_specs=[BS((tt, qkv_w), cur), BS((tt, LANES), cur), wide, wide,
                   BS((SUBLANES, qkv_w), full), BS((SUBLANES, LANES), full), BS((SUBLANES, HEADS_W), full)],
        out_shape=[SDS((n, qkv_w), BF16), SDS((n, LANES), BF16), SDS((n, HEADS_W), BF16), SDS((n, HEADS_W), BF16),
                   SDS((SUBLANES, qkv_w), F32), SDS((SUBLANES, LANES), F32), SDS((SUBLANES, HEADS_W), F32)],
        scratch_shapes=[pltpu.VMEM((tt + 2 * SUBLANES, qkv_w), F32)], compiler_params=_params(2),
    )(proj, proj, proj, proj, proj, proj, dq, dq, dk, dk, dv, dv, db, dg, dqb, dkb, dlf, cw8, aux, lb8)


def _proj_bwd_x(pieces, wp_t, h, nw8, dh_res, tag):
    n = h.shape[0]
    tm = _pick(n, (256, 192, 128, 64))
    widths = [p.shape[1] for p in pieces]
    assert sum(widths) == PROJ_W

    def body(*refs):
        p_refs = refs[:len(pieces)]
        wt_ref, h_ref, nw_ref, dres_ref, dh_ref, cat_ref, dnw_ref = refs[len(pieces):]

        @pl.when(pl.program_id(0) == 0)
        def _():
            dnw_ref[...] = jnp.zeros_like(dnw_ref)

        off = 0
        for p_ref, w in zip(p_refs, widths):
            cat_ref[:, off:off + w] = p_ref[...]
            off += w
        dxn = jnp.dot(cat_ref[...], wt_ref[...], preferred_element_type=F32)
        _, vjp = jax.vjp(_rms, h_ref[...], nw_ref[0:1, :])
        dx, dnw = vjp(dxn)
        dh_ref[...] = dres_ref[...] + dx
        dnw_ref[0:1, :] += dnw

    r2 = lambda i: (i, 0)
    full = lambda i: (0, 0)
    return pl.pallas_call(
        body, name=f"proj_bwd_x_{tag}", grid=(n // tm,),
        in_specs=[BS((tm, w), r2) for w in widths] + [BS((PROJ_W, D_MODEL), full), BS((tm, D_MODEL), r2),
                                                      BS((SUBLANES, D_MODEL), full), BS((tm, D_MODEL), r2)],
        out_specs=[BS((tm, D_MODEL), r2), BS((tm, PROJ_W), r2), BS((SUBLANES, D_MODEL), full)],
        out_shape=[SDS((n, D_MODEL), F32), SDS((n, PROJ_W), BF16), SDS((SUBLANES, D_MODEL), F32)],
        compiler_params=_params(1),
    )(*pieces, wp_t, h, nw8, dh_res)


def _proj_bwd_w(xn, dproj, tag):
    n = xn.shape[0]
    tm = _pick(n, (768, 512, 384, 256, 192, 128, 64))
    tn = 896

    def body(x_ref, d_ref, o_ref):
        @pl.when(pl.program_id(1) == 0)
        def _():
            o_ref[...] = jnp.zeros_like(o_ref)

        o_ref[...] += _dg(x_ref[...], d_ref[...], ((0,), (0,)), False)

    return pl.pallas_call(
        body, name=f"proj_bwd_w_{tag}", grid=(PROJ_W // tn, n // tm),
        in_specs=[BS((tm, D_MODEL), lambda j, i: (i, 0)), BS((tm, tn), lambda j, i: (i, j))],
        out_specs=BS((D_MODEL, tn), lambda j, i: (0, j)), out_shape=SDS((D_MODEL, PROJ_W), F32),
        compiler_params=_params(2),
    )(xn, dproj)


def _adamw(w, g, m, v, name):
    rows, cols = w.shape
    tr = _pick(rows, (256, 128, 64, 32, 16, 8, 4, 2, 1)) if rows > 256 else rows

    def body(w_ref, g_ref, m_ref, v_ref, d_ref, nm_ref, nv_ref):
        gr = g_ref[...]
        m_new = ADAM_B1 * m_ref[...] + (1.0 - ADAM_B1) * gr
        v_new = ADAM_B2 * v_ref[...] + (1.0 - ADAM_B2) * jnp.square(gr)
        m_hat = m_new / (1.0 - ADAM_B1 ** ADAM_STEP)
        v_hat = v_new / (1.0 - ADAM_B2 ** ADAM_STEP)
        d_ref[...] = -ADAM_LR * (m_hat / (jnp.sqrt(v_hat) + ADAM_EPS) + ADAM_WD * w_ref[...])
        nm_ref[...] = m_new
        nv_ref[...] = v_new

    blk = BS((tr, cols), lambda i: (i, 0))
    return pl.pallas_call(
        body, name=name, grid=(rows // tr,), in_specs=[blk] * 4, out_specs=[blk] * 3,
        out_shape=[SDS((rows, cols), F32)] * 3, compiler_params=_params(1),
    )(w, g, m, v)


def _row8(v, width):
    v = jnp.atleast_2d(v).astype(F32)
    return jnp.pad(v, ((0, SUBLANES - v.shape[0]), (0, width - v.shape[1])))


def _to_layout(w_full):
    return jnp.concatenate([w_full[:, 0:1536], w_full[:, 1544:6152], w_full[:, 1536:1544],
                            jnp.zeros((w_full.shape[0], PROJ_W - REF_W), w_full.dtype)], axis=1)


def _from_layout(dw):
    return jnp.concatenate([dw[:, 0:1536], dw[:, C_MISC:C_MISC + 8], dw[:, 1536:C_MISC]], axis=1)


def _lower_bounds(lb):
    sm = jax.nn.softmax(lb.astype(F32), axis=0)
    return jnp.cumsum(sm, axis=0) - sm[0]


def kernel(x, meta_tokens, norm_w, w_in, conv_w, a_log, dt_bias, gnorm_a, gnorm_b, hgrn_lower_bounds, w_branch_a, w_branch_b, w_out, final_norm_w, loss_target, m_meta_tokens, m_norm_w, m_w_in, m_conv_w, m_a_log, m_dt_bias, m_gnorm_a, m_gnorm_b, m_hgrn_lower_bounds, m_w_branch_a, m_w_branch_b, m_w_out, m_final_norm_w, v_meta_tokens, v_norm_w, v_w_in, v_conv_w, v_a_log, v_dt_bias, v_gnorm_a, v_gnorm_b, v_hgrn_lower_bounds, v_w_branch_a, v_w_branch_b, v_w_out, v_final_norm_w):
    nseq, seq, _ = x.shape
    depth = norm_w.shape[0]
    t_len = N_PAD + N_META + seq
    n = nseq * t_len
    win_c, conv_c = w_in.shape[2], conv_w.shape[2]
    my = 4 * lax.axis_index("x") + 2 * lax.axis_index("y") + lax.axis_index("c")

    parts = [w_in.astype(BF16).reshape(-1), w_branch_a.astype(BF16).reshape(-1), w_branch_b.astype(BF16).reshape(-1),
             w_out.astype(BF16).reshape(-1), lax.bitcast_convert_type(conv_w, BF16).reshape(-1),
             lax.bitcast_convert_type(meta_tokens, BF16).reshape(-1)]
    sizes = [p.shape[0] for p in parts]
    flat = jnp.concatenate(parts)
    rows = -(-flat.shape[0] // (D_MODEL * 16)) * 16
    flat = jnp.pad(flat, (0, rows * D_MODEL - flat.shape[0]))
    gathered = _all_gather_hbm(flat.reshape(rows, D_MODEL), "gather_weights").reshape(N_DEV, rows * D_MODEL)
    offs = np.cumsum([0] + sizes)
    seg = lambda i: gathered[:, offs[i]:offs[i + 1]]
    w_in_full = seg(0).reshape(N_DEV, depth, D_MODEL, win_c).transpose(1, 2, 0, 3).reshape(depth, D_MODEL, REF_W)
    wa_full = seg(1).reshape(N_DEV, depth, HEADS_W, LANES).transpose(1, 2, 0, 3).reshape(depth, HEADS_W, D_MODEL)
    wb_full = seg(2).reshape(N_DEV, depth, HEADS_W, LANES).transpose(1, 2, 0, 3).reshape(depth, HEADS_W, D_MODEL)
    wout_full = seg(3).reshape(N_DEV, depth, LANES, D_MODEL).transpose(1, 0, 2, 3).reshape(depth, D_MODEL, D_MODEL)
    conv_full = lax.bitcast_convert_type(seg(4).reshape(N_DEV, depth, 4, conv_c, 2), F32)
    conv_full = conv_full.transpose(1, 2, 0, 3).reshape(depth, 4, 3 * HEADS_W)
    meta_full = lax.bitcast_convert_type(seg(5).reshape(N_DEV, N_META, LANES, 2), F32)
    meta_full = meta_full.transpose(1, 0, 2).reshape(N_META, D_MODEL)

    lb_all, lb_vjp = jax.vjp(_lower_bounds, hgrn_lower_bounds)

    h = jnp.concatenate([jnp.zeros((nseq, N_PAD, D_MODEL), F32),
                         jnp.broadcast_to(meta_full[None], (nseq, N_META, D_MODEL)), x], axis=1).reshape(n, D_MODEL)
    saved = []
    for l in range(depth):
        wp = _to_layout(w_in_full[l])
        nw8 = _row8(norm_w[l], D_MODEL)
        cw8 = _row8(conv_full[l], 3 * HEADS_W)
        aux = _row8(jnp.stack([a_log[l], dt_bias[l]]), LANES)
        lb8 = _row8(lb_all[l], HEADS_W)
        gn8 = _row8(jnp.stack([gnorm_a[l], gnorm_b[l]]), LANES)
        proj, xn = _proj_fwd(h, nw8, wp, l)
        q, k, v, b, g, qb, kb, lf = _prep_fwd(proj, cw8, aux, lb8, nseq, t_len, l)
        oa, sck_a = _gdn_fwd(q, k, v, b, g, nseq, t_len, l)
        ob, sck_b = _hgrn_fwd(qb, kb, proj, C_BI // HEADS_W, lf, nseq, t_len, l)
        h_next = _post_fwd(oa, ob, proj, h, gn8, wa_full[l], wb_full[l], wout_full[l], l)
        saved.append(dict(h=h, wp=wp, nw8=nw8, cw8=cw8, aux=aux, lb8=lb8, gn8=gn8, proj=proj, xn=xn, q=q, k=k, v=v, b=b,
                          g=g, qb=qb, kb=kb, lf=lf, oa=oa, ob=ob, sck_a=sck_a, sck_b=sck_b))
        h = h_next

    dh, acc = _loss_head(h, _row8(final_norm_w, D_MODEL), loss_target, nseq, t_len)

    g_win, g_wa, g_wb, g_wout, g_conv, small = [], [], [], [], [], []
    for l in reversed(range(depth)):
        s = saved[l]
        doa, dob, dz, dbg, dga, dgb, dwa, dwb, dwout, dgn = _post_bwd(
            dh, s["oa"], s["ob"], s["proj"], s["h"], s["gn8"], wa_full[l], wb_full[l], wa_full[l].T, wb_full[l].T,
            wout_full[l].T, l)
        dq, dk, dv, db, dg = _gdn_bwd(s["q"], s["k"], s["v"], s["b"], s["g"], s["sck_a"], doa, nseq, t_len, l)
        dqb, dkb, dbi, dlf = _hgrn_bwd(s["qb"], s["kb"], s["proj"], C_BI // HEADS_W, s["lf"], s["sck_b"], dob, nseq,
                                       t_len, l)
        dqkv, dmisc, dbq, dbf, dcw, daux, dlb = _prep_bwd(s["proj"], dq, dk, dv, db, dg, dqb, dkb, dlf, s["cw8"],
                                                          s["aux"], s["lb8"], nseq, t_len, l)
        dh, dproj, dnw = _proj_bwd_x([dqkv, dz, dbq, dbf, dbi, dbg, dga, dgb, dmisc], s["wp"].T, s["h"], s["nw8"], dh, l)
        g_win.append(_from_layout(_proj_bwd_w(s["xn"], dproj, l)))
        g_wa.append(dwa)
        g_wb.append(dwb)
        g_wout.append(dwout)
        g_conv.append(dcw[:4])
        small.append((dnw[0], dgn[0], dgn[1], daux[0, :N_HEADS], daux[1, :N_HEADS], dlb[0]))
    for lst in (g_win, g_wa, g_wb, g_wout, g_conv, small):
        lst.reverse()
    dh = dh.reshape(nseq, t_len, D_MODEL)
    grad_x = dh[:, N_PAD + N_META:]

    packed = jnp.concatenate([small[0][1], small[1][1], small[0][2], small[1][2], small[0][3], small[1][3],
                              small[0][4], small[1][4]])
    tile = jnp.concatenate([
        jnp.sum(dh[:, N_PAD:N_PAD + N_META], axis=0), _row8(jnp.stack([small[0][0], small[1][0], acc[0]]), D_MODEL),
        _row8(jnp.stack([small[0][5], small[1][5]]), D_MODEL), _row8(packed, D_MODEL), _row8(acc[1], D_MODEL)], axis=0)
    tile = _all_reduce_small(tile, "reduce_small")
    loss = jnp.sum(tile[40])
    g_meta = lax.dynamic_slice_in_dim(tile[0:N_META], my * LANES, LANES, axis=1)
    g_norm, g_final = tile[16:18], tile[18]
    (g_lb,) = lb_vjp(tile[24:26, :HEADS_W])
    r21 = tile[32]
    g_gna, g_gnb = r21[0:256].reshape(2, LANES), r21[256:512].reshape(2, LANES)
    g_alog, g_dtb = r21[512:520].reshape(2, N_HEADS), r21[520:528].reshape(2, N_HEADS)

    dwin, dwa_, dwb_, dwout_, dconv = (jnp.stack(a) for a in (g_win, g_wa, g_wb, g_wout, g_conv))
    g_sizes = [depth * D_MODEL * win_c, depth * HEADS_W * LANES, depth * HEADS_W * LANES, depth * LANES * D_MODEL,
               depth * 4 * conv_c]
    g_rows = -(-sum(g_sizes) // (D_MODEL * SUBLANES)) * SUBLANES
    slabs = []
    for rel in range(N_DEV):
        j = my ^ rel
        col = lambda a, w: lax.dynamic_slice_in_dim(a, j * w, w, axis=2).reshape(-1)
        fl = jnp.concatenate([col(dwin, win_c), col(dwa_, LANES), col(dwb_, LANES),
                              lax.dynamic_slice_in_dim(dwout_, j * LANES, LANES, axis=1).reshape(-1), col(dconv, conv_c)])
        slabs.append(jnp.pad(fl, (0, g_rows * D_MODEL - fl.shape[0])).reshape(g_rows, D_MODEL))
    mine = _reduce_scatter(jnp.stack(slabs), "grads").reshape(-1)
    g_offs = np.cumsum([0] + g_sizes)
    gseg = lambda i, shape: mine[g_offs[i]:g_offs[i + 1]].reshape(shape)
    grads = {
        "meta_tokens": g_meta, "norm_w": g_norm, "w_in": gseg(0, w_in.shape), "conv_w": gseg(4, conv_w.shape),
        "a_log": g_alog, "dt_bias": g_dtb, "gnorm_a": g_gna, "gnorm_b": g_gnb, "hgrn_lower_bounds": g_lb,
        "w_branch_a": gseg(1, w_branch_a.shape), "w_branch_b": gseg(2, w_branch_b.shape), "w_out": gseg(3, w_out.shape),
        "final_norm_w": g_final}
    weights = {
        "meta_tokens": (meta_tokens, m_meta_tokens, v_meta_tokens), "norm_w": (norm_w, m_norm_w, v_norm_w),
        "w_in": (w_in, m_w_in, v_w_in), "conv_w": (conv_w, m_conv_w, v_conv_w), "a_log": (a_log, m_a_log, v_a_log),
        "dt_bias": (dt_bias, m_dt_bias, v_dt_bias), "gnorm_a": (gnorm_a, m_gnorm_a, v_gnorm_a),
        "gnorm_b": (gnorm_b, m_gnorm_b, v_gnorm_b),
        "hgrn_lower_bounds": (hgrn_lower_bounds, m_hgrn_lower_bounds, v_hgrn_lower_bounds),
        "w_branch_a": (w_branch_a, m_w_branch_a, v_w_branch_a), "w_branch_b": (w_branch_b, m_w_branch_b, v_w_branch_b),
        "w_out": (w_out, m_w_out, v_w_out), "final_norm_w": (final_norm_w, m_final_norm_w, v_final_norm_w)}
    names = list(weights)
    deltas, new_m, new_v = [], [], []
    for nm in names:
        w, m, v = weights[nm]
        view = (-1, w.shape[-1])
        d, m2, v2 = _adamw(w.reshape(view), grads[nm].reshape(view), m.reshape(view), v.reshape(view), f"adamw_{nm}")
        deltas.append(d.reshape(w.shape))
        new_m.append(m2.reshape(w.shape))
        new_v.append(v2.reshape(w.shape))
    return (loss, grad_x, *[grads[nm].reshape(weights[nm][0].shape) for nm in names], *deltas, *new_m, *new_v)
```

```python
import functools

import jax
import jax.numpy as jnp
import numpy as np
from jax import lax
from jax.experimental import pallas as pl
from jax.experimental.pallas import tpu as pltpu

F32 = jnp.float32
BF16 = jnp.bfloat16

D_MODEL = 1024
N_HEADS = 4
D_HEAD = 128
HEADS_W = N_HEADS * D_HEAD
N_META = 16
N_PAD = 48
GDN_CHUNK = 64
HGRN_CHUNK = 16
EPS = 1e-6
N_DEV = 8
LANES = 128
SUBLANES = 8
VMEM_LIMIT = 56 * 1024 * 1024

C_QKV, C_Z, C_BQ, C_BF, C_BI, C_BG, C_GA, C_GB, C_MISC = 0, 1536, 2048, 2560, 3072, 3584, 4096, 5120, 6144
PROJ_W = 6272
REF_W = 6152

ADAM_LR, ADAM_B1, ADAM_B2, ADAM_EPS, ADAM_WD, ADAM_STEP = 0.001, 0.9, 0.999, 1e-08, 0.01, 10

MESH = pl.DeviceIdType.MESH
SDS = jax.ShapeDtypeStruct
BS = pl.BlockSpec


def _params(n_axes):
    return pltpu.CompilerParams(dimension_semantics=("arbitrary",) * n_axes, vmem_limit_bytes=VMEM_LIMIT)


def _pick(n, cands):
    for c in cands:
        if n % c == 0:
            return c
    raise ValueError(f"no tile for {n} among {cands}")


def _iota2(shape, dim):
    return lax.broadcasted_iota(jnp.int32, shape, dim)


def _dg(a, b, dims):
    return lax.dot_general(a.astype(BF16), b.astype(BF16), (dims, ((), ())), preferred_element_type=F32)


def _bdg(a, b, ca, cb):
    return lax.dot_general(a.astype(BF16), b.astype(BF16), (((ca,), (cb,)), ((0,), (0,))), preferred_element_type=F32)


@jax.custom_vjp
def _bnn(a, b):
    return _bdg(a, b, 2, 1)


@jax.custom_vjp
def _bnt(a, b):
    return _bdg(a, b, 2, 2)


@jax.custom_vjp
def _btn(a, b):
    return _bdg(a, b, 1, 1)


_bnn.defvjp(lambda a, b: (_bnn(a, b), (a, b)), lambda r, g: (_bnt(g, r[1]), _btn(r[0], g)))
_bnt.defvjp(lambda a, b: (_bnt(a, b), (a, b)), lambda r, g: (_bnn(g, r[1]), _btn(g, r[0])))
_btn.defvjp(lambda a, b: (_btn(a, b), (a, b)), lambda r, g: (_bnt(r[1], g), _bnn(r[0], g)))


def _split2(x):
    hi = x.astype(BF16).astype(F32)
    return hi, x - hi


def _tri(bsz, n):
    return jnp.broadcast_to((_iota2((n, n), 0) >= _iota2((n, n), 1)).astype(F32), (bsz, n, n))


@jax.custom_vjp
def _cumsum_rows(x):
    tri = _tri(x.shape[0], x.shape[1])
    hi, lo = _split2(x)
    return _bdg(tri, hi, 2, 1) + _bdg(tri, lo, 2, 1)


def _cumsum_rows_bwd(_, g):
    tri = _tri(g.shape[0], g.shape[1])
    hi, lo = _split2(g)
    return (_bdg(tri, hi, 1, 1) + _bdg(tri, lo, 1, 1),)


_cumsum_rows.defvjp(lambda x: (_cumsum_rows(x), None), _cumsum_rows_bwd)


def _sigmoid(x):
    return jax.nn.sigmoid(x)


def _silu(x):
    return x * _sigmoid(x)


def _softplus(x):
    return jnp.maximum(x, 0.0) + jnp.log1p(jnp.exp(-jnp.abs(x)))


def _rms(x, w):
    return x * lax.rsqrt(jnp.mean(x * x, axis=-1, keepdims=True) + EPS) * w


def _inv_unit_lower(lm):
    n = lm.shape[1]
    a = (_iota2((n, n), 0) == _iota2((n, n), 1)).astype(F32)[None] - lm
    steps = max(1, (n - 1).bit_length()) - 1
    p = _bnn(lm, lm)
    for i in range(steps):
        if i == steps - 1:
            a = a + _bnn(a, p)
        else:
            both = _bnn(jnp.concatenate([a, p], axis=1), p)
            a, p = a + both[:, :n], both[:, n:]
    return a


def _gdn_chunk(q, k, v, b_b, g_b, s):
    n, dv = q.shape[1], v.shape[2]
    r, c = _iota2((n, n), 0), _iota2((n, n), 1)
    causal, strict, eye = (r >= c)[None], (r > c)[None], (r == c)[None]
    g_cum = _cumsum_rows(g_b)
    g_i = g_cum[:, :, :n]
    g_j = jnp.sum(jnp.where(eye, g_i, 0.0), axis=1, keepdims=True)
    decay = jnp.where(causal, jnp.exp(jnp.where(causal, g_i - g_j, 0.0)), 0.0)
    e_g = jnp.exp(g_cum)
    kb = k * b_b
    kk = _bnt(jnp.concatenate([kb, q], axis=1), k)
    a_inv = _inv_unit_lower(jnp.where(strict, kk[:, :n] * decay, 0.0))
    uw = _bnn(a_inv, jnp.concatenate([v * b_b, kb * e_g], axis=2))
    ws = _bnn(jnp.concatenate([uw[:, :, dv:], q * e_g], axis=1), s)
    v_new = uw[:, :, :dv] - ws[:, :n]
    o = ws[:, n:] + _bnn(kk[:, n:] * decay, v_new)
    g_last = g_cum[:, n - 1:n, :]
    s_new = s * jnp.exp(g_last) + _btn(k * jnp.exp(g_last - g_cum), v_new)
    return o, s_new


def _hgrn_chunk(q, k, v, lf, st):
    n = q.shape[1]
    b_cum = _cumsum_rows(lf)
    o = _bnt(q * jnp.exp(b_cum), st)
    rows = _iota2((1, n, 1), 1)
    for j in range(n):
        pick = rows == j
        b_j = jnp.sum(jnp.where(pick, b_cum, 0.0), axis=1, keepdims=True)
        k_j = jnp.sum(jnp.where(pick, k, 0.0), axis=1, keepdims=True)
        v_j = jnp.sum(jnp.where(pick, v, 0.0), axis=1, keepdims=True)
        m = rows >= j
        p = jnp.where(m, jnp.exp(jnp.where(m, b_cum - b_j, 0.0)), 0.0)
        o = o + jnp.sum(q * k_j * p, axis=2, keepdims=True) * v_j
    b_last = jnp.sum(jnp.where(rows == n - 1, b_cum, 0.0), axis=1, keepdims=True)
    st_new = st * jnp.exp(b_last) + _btn(v, k * jnp.exp(b_last - b_cum))
    return o, st_new


def _l2n_act(y, scale):
    a = _silu(y)
    return a * lax.rsqrt(jnp.sum(a * a, axis=-1, keepdims=True) + EPS) * scale


def _col(x, lane):
    return jnp.sum(jnp.where(_iota2(x.shape, 1) == lane, x, 0.0), axis=1, keepdims=True)


def _elem(x, row, lane):
    m = (_iota2(x.shape, 0) == row) & (_iota2(x.shape, 1) == lane)
    return jnp.sum(jnp.sum(jnp.where(m, x, 0.0), axis=1, keepdims=True), axis=0, keepdims=True)


def _gdn_gates(misc, aux, real, head):
    beta = _sigmoid(_col(misc, head))
    g = -jnp.exp(_elem(aux, 0, head)) * _softplus(_col(misc, N_HEADS + head) + _elem(aux, 1, head))
    g = jnp.where(real, g, 0.0)
    shape = (misc.shape[0], D_HEAD)
    return jnp.broadcast_to(beta, shape), jnp.broadcast_to(g, shape)


def _hgrn_prep(bq, bf, lb, real):
    qb = _silu(bq) * (D_HEAD ** -0.5)
    log_sig = jnp.minimum(bf, 0.0) - jnp.log1p(jnp.exp(-jnp.abs(bf)))
    pos = lb > 0.0
    lbs = jnp.where(pos, lb, 0.5)
    a = jnp.log(lbs)
    b = jnp.log1p(-lbs) + log_sig
    lae = jnp.maximum(a, b) + jnp.log1p(jnp.exp(-jnp.abs(a - b)))
    lf = jnp.where(pos, lae, log_sig)
    kb = jnp.where(pos, 1.0 - lbs, 1.0) * _sigmoid(-bf)
    return qb, jnp.where(real, kb, 0.0), jnp.where(real, lf, 0.0)


def _gated_norm(o, z, gw):
    return o * lax.rsqrt(jnp.mean(o * o, axis=-1, keepdims=True) + EPS) * gw * _silu(z)


def _shift_down(x, j):
    return x if j == 0 else pltpu.roll(x, j, 0)


def _shift_up(x, j):
    return x if j == 0 else pltpu.roll(x, x.shape[0] - j, 0)


def _all_gather_hbm(block, name):
    r, c = block.shape

    def body(x_ref, out_ref, send_sems, recv_sems, local_sem):
        mx, my, mc = lax.axis_index("x"), lax.axis_index("y"), lax.axis_index("c")
        me, sibling = (mx, my, mc), (mx, my, 1 - mc)
        chips = [(1 - mx, my), (mx, 1 - my), (1 - mx, 1 - my)]

        def slab(px, py, pc):
            return out_ref.at[4 * px + 2 * py + pc]

        def copy(k, blk, to, src=None):
            return pltpu.make_async_remote_copy(
                src_ref=slab(*blk) if src is None else src, dst_ref=slab(*blk),
                send_sem=send_sems.at[k], recv_sem=recv_sems.at[k], device_id=to, device_id_type=MESH)

        mine = pltpu.make_async_copy(x_ref, slab(*me), local_sem)
        mine.start()
        first = [copy(0, me, sibling, src=x_ref)]
        first += [copy(1 + j, me, (*chip, mc), src=x_ref) for j, chip in enumerate(chips)]
        for cp in first:
            cp.start()
        passed = [copy(4 + j, (*chip, mc), sibling) for j, chip in enumerate(chips)]
        for j, chip in enumerate(chips):
            copy(1 + j, (*chip, mc), me).wait_recv()
            passed[j].start()
        copy(0, sibling, me).wait_recv()
        for j, chip in enumerate(chips):
            copy(4 + j, (*chip, 1 - mc), me).wait_recv()
        for cp in first + passed:
            cp.wait_send()
        mine.wait()

    return pl.pallas_call(
        body, name=name, out_shape=SDS((N_DEV, r, c), block.dtype),
        in_specs=[BS(memory_space=pl.ANY)], out_specs=BS(memory_space=pl.ANY),
        scratch_shapes=[pltpu.SemaphoreType.DMA((7,)), pltpu.SemaphoreType.DMA((7,)), pltpu.SemaphoreType.DMA],
    )(block)


def _all_reduce_small(block, name):
    r, c = block.shape

    def body(x_ref, out_ref, buf, send_sems, recv_sems):
        mx, my, mc = lax.axis_index("x"), lax.axis_index("y"), lax.axis_index("c")
        me, sibling = (mx, my, mc), (mx, my, 1 - mc)
        chips = [(1 - mx, my), (mx, 1 - my), (1 - mx, 1 - my)]

        def slab(px, py, pc):
            return buf.at[4 * px + 2 * py + pc]

        def copy(k, blk, to, src=None):
            return pltpu.make_async_remote_copy(
                src_ref=slab(*blk) if src is None else src, dst_ref=slab(*blk),
                send_sem=send_sems.at[k], recv_sem=recv_sems.at[k], device_id=to, device_id_type=MESH)

        first = [copy(0, me, sibling, src=x_ref)]
        first += [copy(1 + j, me, (*chip, mc), src=x_ref) for j, chip in enumerate(chips)]
        for cp in first:
            cp.start()
        passed = [copy(4 + j, (*chip, mc), sibling) for j, chip in enumerate(chips)]
        for j, chip in enumerate(chips):
            copy(1 + j, (*chip, mc), me).wait_recv()
            passed[j].start()
        copy(0, sibling, me).wait_recv()
        for j, chip in enumerate(chips):
            copy(4 + j, (*chip, 1 - mc), me).wait_recv()
        for cp in first + passed:
            cp.wait_send()
        buf[4 * mx + 2 * my + mc] = x_ref[...]
        acc = buf[0]
        for d in range(1, N_DEV):
            acc = acc + buf[d]
        out_ref[...] = acc

    return pl.pallas_call(
        body, name=name, out_shape=SDS((r, c), F32),
        in_specs=[BS(memory_space=pltpu.VMEM)], out_specs=BS(memory_space=pltpu.VMEM),
        scratch_shapes=[pltpu.VMEM((N_DEV, r, c), F32), pltpu.SemaphoreType.DMA((7,)), pltpu.SemaphoreType.DMA((7,))],
    )(block)


def _exchange(buf, flip, paired, name):
    n, r, c = buf.shape[0], buf.shape[-2], buf.shape[-1]
    axis = ("x", "y", "c")[flip]

    def body(g_ref, out_ref, send_sems, recv_sems):
        pos = [lax.axis_index("x"), lax.axis_index("y"), lax.axis_index("c")]
        pos[flip] = 1 - pos[flip]
        other = 1 - lax.axis_index(axis)
        copies = [pltpu.make_async_remote_copy(
            src_ref=g_ref.at[i, other] if paired else g_ref.at[i], dst_ref=out_ref.at[i], send_sem=send_sems.at[i],
            recv_sem=recv_sems.at[i], device_id=tuple(pos), device_id_type=MESH) for i in range(n)]
        for cp in copies:
            cp.start()
        for cp in copies:
            cp.wait_recv()
        for cp in copies:
            cp.wait_send()

    return pl.pallas_call(
        body, name=name, out_shape=SDS((n, r, c), buf.dtype),
        in_specs=[BS(memory_space=pl.ANY)], out_specs=BS(memory_space=pl.ANY),
        scratch_shapes=[pltpu.SemaphoreType.DMA((n,)), pltpu.SemaphoreType.DMA((n,))],
    )(buf)


def _rs_tile(r):
    return _pick(r, (704, 512, 352, 256, 192, 128, 64, 32, 16, 8))


def _rs_add_c(g4, recv, coords, name):
    _, _, r, c = g4.shape
    tr = _rs_tile(r)

    def body(co_ref, a0_ref, a1_ref, b0_ref, b1_ref, keep_ref, send_ref):
        s0 = a0_ref[...] + b0_ref[...]
        s1 = a1_ref[...] + b1_ref[...]
        mine = co_ref[1] == 0
        keep_ref[...] = jnp.where(mine, s0, s1)
        send_ref[...] = jnp.where(mine, s1, s0).astype(BF16)

    blk = lambda yy: BS((None, None, tr, c), functools.partial(lambda i, j, co, yy: (2 * i + yy, co[2], j, 0), yy=yy))
    rblk = lambda yy: BS((None, tr, c), functools.partial(lambda i, j, co, yy: (2 * i + yy, j, 0), yy=yy))
    out = BS((None, tr, c), lambda i, j, co: (i, j, 0))
    return pl.pallas_call(
        body, name=name, out_shape=[SDS((2, r, c), F32), SDS((2, r, c), BF16)],
        grid_spec=pltpu.PrefetchScalarGridSpec(num_scalar_prefetch=1, grid=(2, r // tr),
                                               in_specs=[blk(0), blk(1), rblk(0), rblk(1)], out_specs=[out, out]),
        compiler_params=_params(2),
    )(coords, g4, g4, recv, recv)


def _rs_add_y(kept, recv, coords, name):
    _, r, c = kept.shape
    tr = _rs_tile(r)

    def body(co_ref, a_ref, b_ref, keep_ref, send_ref):
        s0 = a_ref[0] + b_ref[0].astype(F32)
        s1 = a_ref[1] + b_ref[1].astype(F32)
        mine = co_ref[0] == 0
        keep_ref[...] = jnp.where(mine, s0, s1)
        send_ref[0] = jnp.where(mine, s1, s0).astype(BF16)

    blk = BS((2, tr, c), lambda j, co: (0, j, 0))
    return pl.pallas_call(
        body, name=name, out_shape=[SDS((r, c), F32), SDS((1, r, c), BF16)],
        grid_spec=pltpu.PrefetchScalarGridSpec(num_scalar_prefetch=1, grid=(r // tr,), in_specs=[blk, blk],
                                               out_specs=[BS((tr, c), lambda j, co: (j, 0)),
                                                          BS((1, tr, c), lambda j, co: (0, j, 0))]),
        compiler_params=_params(1),
    )(coords, kept, recv)


def _rs_add_x(kept, recv, name):
    r, c = kept.shape
    tr = _rs_tile(r)

    def body(a_ref, b_ref, o_ref):
        o_ref[...] = a_ref[...] + b_ref[0].astype(F32)

    return pl.pallas_call(
        body, name=name, grid=(r // tr,), out_shape=SDS((r, c), F32),
        in_specs=[BS((tr, c), lambda j: (j, 0)), BS((1, tr, c), lambda j: (0, j, 0))],
        out_specs=BS((tr, c), lambda j: (j, 0)), compiler_params=_params(1),
    )(kept, recv)


def _reduce_scatter(g_abs, coords, tag):
    _, r, c = g_abs.shape
    g4 = g_abs.reshape(4, 2, r, c)
    got = _exchange(g4, 2, True, f"rs_c_{tag}")
    kept, send = _rs_add_c(g4, got, coords, f"rs_c_add_{tag}")
    got = _exchange(send, 1, False, f"rs_y_{tag}")
    kept, send = _rs_add_y(kept, got, coords, f"rs_y_add_{tag}")
    got = _exchange(send, 0, False, f"rs_x_{tag}")
    return _rs_add_x(kept, got, f"rs_x_add_{tag}")


def _proj_fwd(h, nw8, wp, tag):
    n = h.shape[0]
    tm = _pick(n, (768, 512, 384, 256, 192, 128, 64))
    tn = 896

    def body(h_ref, nw_ref, w_ref, proj_ref, xn_ref):
        @pl.when(pl.program_id(1) == 0)
        def _():
            xn_ref[...] = _rms(h_ref[...], nw_ref[0:1, :]).astype(BF16)

        proj_ref[...] = jnp.dot(xn_ref[...], w_ref[...], preferred_element_type=F32)

    return pl.pallas_call(
        body, name=f"proj_fwd_{tag}", grid=(n // tm, PROJ_W // tn),
        in_specs=[BS((tm, D_MODEL), lambda i, j: (i, 0)), BS((SUBLANES, D_MODEL), lambda i, j: (0, 0)),
                  BS((D_MODEL, tn), lambda i, j: (0, j))],
        out_specs=[BS((tm, tn), lambda i, j: (i, j)), BS((tm, D_MODEL), lambda i, j: (i, 0))],
        out_shape=[SDS((n, PROJ_W), F32), SDS((n, D_MODEL), BF16)], compiler_params=_params(2),
    )(h, nw8, wp)


def _conv_ext(x_ext, cw_ref):
    y = x_ext * cw_ref[3:4, :]
    for k in range(3):
        y = y + _shift_down(x_ext, 3 - k) * cw_ref[k:k + 1, :]
    return y[SUBLANES:]


def _prep_fwd(proj, cw8, aux, lb8, nseq, t_len, tag):
    n = proj.shape[0]
    tt = _pick(t_len, (192, 128, 64))
    nt_ = t_len // tt
    qkv_w = 3 * HEADS_W

    def body(cur_ref, prev_ref, misc_ref, bq_ref, bf_ref, cw_ref, aux_ref, lb_ref,
             q_ref, k_ref, v_ref, b_ref, g_ref, qb_ref, kb_ref, lf_ref):
        t = pl.program_id(1)
        prev = jnp.where(t == 0, 0.0, prev_ref[...])
        y = _conv_ext(jnp.concatenate([prev, cur_ref[...]], axis=0), cw_ref)
        real = (t * tt + _iota2((tt, 1), 0)) >= N_PAD
        misc = misc_ref[...]
        auxv = aux_ref[...]
        for hd in range(N_HEADS):
            sl = slice(hd * D_HEAD, (hd + 1) * D_HEAD)
            q_ref[:, sl] = _l2n_act(y[:, sl], D_HEAD ** -0.5)
            k_ref[:, sl] = _l2n_act(y[:, HEADS_W + hd * D_HEAD:HEADS_W + (hd + 1) * D_HEAD], 1.0)
            v_ref[:, sl] = _silu(y[:, 2 * HEADS_W + hd * D_HEAD:2 * HEADS_W + (hd + 1) * D_HEAD])
            b_ref[:, sl], g_ref[:, sl] = _gdn_gates(misc, auxv, real, hd)
        qb_ref[...], kb_ref[...], lf_ref[...] = _hgrn_prep(bq_ref[...], bf_ref[...], lb_ref[0:1, :], real)

    rb = tt // SUBLANES
    row = lambda s, t: s * nt_ + t
    wide = BS((tt, HEADS_W), lambda s, t: (row(s, t), 0))
    return pl.pallas_call(
        body, name=f"prep_fwd_{tag}", grid=(nseq, nt_),
        in_specs=[BS((tt, qkv_w), lambda s, t: (row(s, t), 0)),
                  BS((SUBLANES, qkv_w), lambda s, t: (jnp.maximum(row(s, t) * rb - 1, 0), 0)),
                  BS((tt, LANES), lambda s, t: (row(s, t), C_MISC // LANES)),
                  BS((tt, HEADS_W), lambda s, t: (row(s, t), C_BQ // HEADS_W)),
                  BS((tt, HEADS_W), lambda s, t: (row(s, t), C_BF // HEADS_W)),
                  BS((SUBLANES, qkv_w), lambda s, t: (0, 0)), BS((SUBLANES, LANES), lambda s, t: (0, 0)),
                  BS((SUBLANES, HEADS_W), lambda s, t: (0, 0))],
        out_specs=[wide] * 8, out_shape=[SDS((n, HEADS_W), F32)] * 8, compiler_params=_params(2),
    )(proj, proj, proj, proj, proj, cw8, aux, lb8)


HGRN_SUB = GDN_CHUNK // HGRN_CHUNK


def _seq_block(nseq):
    return 2 if nseq % 2 == 0 else 1


def _to_chains(x):
    return jnp.concatenate([x[:, :, hd * D_HEAD:(hd + 1) * D_HEAD] for hd in range(N_HEADS)], axis=0)


def _from_chains(ref, rows, val):
    sb = val.shape[0] // N_HEADS
    for hd in range(N_HEADS):
        ref[:, rows, hd * D_HEAD:(hd + 1) * D_HEAD] = val[hd * sb:(hd + 1) * sb].astype(ref.dtype)


def _scan_call(body, name, arrays, col_blocks, reverse, nseq, t_len, n_sub, extra_in, outs):
    sb = _seq_block(nseq)
    nc = t_len // GDN_CHUNK
    chains = N_HEADS * sb
    cidx = (lambda c: nc - 1 - c) if reverse else (lambda c: c)
    ck_shape = (nseq // sb, nc * n_sub, chains, D_HEAD, D_HEAD)
    ck_block = (None, n_sub, chains, D_HEAD, D_HEAD) if n_sub > 1 else (None, None, chains, D_HEAD, D_HEAD)
    ck_spec = BS(ck_block, lambda p, c: (p, cidx(c), 0, 0, 0))
    in_specs = [BS((sb, GDN_CHUNK, HEADS_W), functools.partial(lambda p, c, cb: (p, cidx(c), cb), cb=cb))
                for cb in col_blocks]
    args = [a.reshape(nseq, t_len, a.shape[1]) for a in arrays]
    if extra_in is not None:
        in_specs.append(ck_spec)
        args.append(extra_in)
    out_specs, out_shape = [], []
    for o in outs:
        if o == "ckpt":
            out_specs.append(ck_spec)
            out_shape.append(SDS(ck_shape, F32))
        else:
            out_specs.append(BS((sb, GDN_CHUNK, HEADS_W), lambda p, c: (p, cidx(c), 0)))
            out_shape.append(SDS((nseq, t_len, HEADS_W), o))
    res = pl.pallas_call(
        body, name=name, grid=(nseq // sb, nc), in_specs=in_specs, out_specs=out_specs, out_shape=out_shape,
        scratch_shapes=[pltpu.VMEM((chains, D_HEAD, D_HEAD), F32)], compiler_params=_params(2),
    )(*args)
    return [r if o == "ckpt" else r.reshape(nseq * t_len, HEADS_W) for r, o in zip(res, outs)]


def _gdn_fwd(q, k, v, b, g, nseq, t_len, tag):
    def body(q_ref, k_ref, v_ref, b_ref, g_ref, o_ref, sck_ref, s_ref):
        @pl.when(pl.program_id(1) == 0)
        def _():
            s_ref[...] = jnp.zeros_like(s_ref)

        s = s_ref[...]
        sck_ref[...] = s
        o, s_new = _gdn_chunk(*[_to_chains(r[...]) for r in (q_ref, k_ref, v_ref, b_ref, g_ref)], s)
        _from_chains(o_ref, slice(None), o)
        s_ref[...] = s_new

    return _scan_call(body, f"gdn_fwd_{tag}", [q, k, v, b, g], [0] * 5, False, nseq, t_len, 1, None, [F32, "ckpt"])


def _gdn_bwd(q, k, v, b, g, sck, do, nseq, t_len, tag):
    def body(q_ref, k_ref, v_ref, b_ref, g_ref, do_ref, sck_ref, dq_ref, dk_ref, dv_ref, db_ref, dg_ref, ds_ref):
        @pl.when(pl.program_id(1) == 0)
        def _():
            ds_ref[...] = jnp.zeros_like(ds_ref)

        _, vjp = jax.vjp(_gdn_chunk, *[_to_chains(r[...]) for r in (q_ref, k_ref, v_ref, b_ref, g_ref)], sck_ref[...])
        grads = vjp((_to_chains(do_ref[...]), ds_ref[...]))
        for ref, val in zip((dq_ref, dk_ref, dv_ref, db_ref, dg_ref), grads[:5]):
            _from_chains(ref, slice(None), val)
        ds_ref[...] = grads[5]

    return _scan_call(body, f"gdn_bwd_{tag}", [q, k, v, b, g, do], [0] * 6, True, nseq, t_len, 1, sck, [F32] * 5)


def _hgrn_fwd(q, k, v, v_col, lf, nseq, t_len, tag):
    def body(q_ref, k_ref, v_ref, lf_ref, o_ref, sck_ref, s_ref):
        @pl.when(pl.program_id(1) == 0)
        def _():
            s_ref[...] = jnp.zeros_like(s_ref)

        for sub in range(HGRN_SUB):
            rs = slice(sub * HGRN_CHUNK, (sub + 1) * HGRN_CHUNK)
            s = s_ref[...]
            sck_ref[sub] = s
            o, s_new = _hgrn_chunk(*[_to_chains(r[:, rs, :]) for r in (q_ref, k_ref, v_ref, lf_ref)], s)
            _from_chains(o_ref, rs, o)
            s_ref[...] = s_new

    return _scan_call(body, f"hgrn_fwd_{tag}", [q, k, v, lf], [0, 0, v_col, 0], False, nseq, t_len, HGRN_SUB, None,
                      [F32, "ckpt"])


def _hgrn_bwd(q, k, v, v_col, lf, sck, do, nseq, t_len, tag):
    def body(q_ref, k_ref, v_ref, lf_ref, do_ref, sck_ref, dq_ref, dk_ref, dv_ref, dlf_ref, ds_ref):
        @pl.when(pl.program_id(1) == 0)
        def _():
            ds_ref[...] = jnp.zeros_like(ds_ref)

        for sub in reversed(range(HGRN_SUB)):
            rs = slice(sub * HGRN_CHUNK, (sub + 1) * HGRN_CHUNK)
            _, vjp = jax.vjp(_hgrn_chunk, *[_to_chains(r[:, rs, :]) for r in (q_ref, k_ref, v_ref, lf_ref)],
                             sck_ref[sub])
            grads = vjp((_to_chains(do_ref[:, rs, :]), ds_ref[...]))
            for ref, val in zip((dq_ref, dk_ref, dv_ref, dlf_ref), grads[:4]):
                _from_chains(ref, rs, val)
            ds_ref[...] = grads[4]

    return _scan_call(body, f"hgrn_bwd_{tag}", [q, k, v, lf, do], [0, 0, v_col, 0, 0], True, nseq, t_len, HGRN_SUB, sck,
                      [F32, F32, BF16, F32])


def _post_values(oa_ref, ob_ref, z_ref, bg_ref, ga_ref, gb_ref, gn_ref, wa_ref, wb_ref, ya_ref, yb_ref):
    for hd in range(N_HEADS):
        sl = slice(hd * D_HEAD, (hd + 1) * D_HEAD)
        ya_ref[:, sl] = _gated_norm(oa_ref[:, sl], z_ref[:, sl], gn_ref[0:1, :]).astype(BF16)
        yb_ref[:, sl] = _gated_norm(ob_ref[:, sl], bg_ref[:, sl], gn_ref[1:2, :]).astype(BF16)
    pa = jnp.dot(ya_ref[...], wa_ref[...], preferred_element_type=F32)
    pb = jnp.dot(yb_ref[...], wb_ref[...], preferred_element_type=F32)
    return pa, pb, _sigmoid(ga_ref[...]), _sigmoid(gb_ref[...])


def _post_specs(tm):
    r2 = lambda i: (i, 0)
    return [BS((tm, HEADS_W), r2), BS((tm, HEADS_W), r2),
            BS((tm, HEADS_W), lambda i: (i, C_Z // HEADS_W)), BS((tm, HEADS_W), lambda i: (i, C_BG // HEADS_W)),
            BS((tm, D_MODEL), lambda i: (i, C_GA // D_MODEL)), BS((tm, D_MODEL), lambda i: (i, C_GB // D_MODEL)),
            BS((tm, D_MODEL), r2), BS((SUBLANES, LANES), lambda i: (0, 0))]


def _post_fwd(oa, ob, proj, h, gn8, wa, wb, wout, tag):
    n = h.shape[0]
    tm = _pick(n, (256, 192, 128, 64))

    def body(oa_ref, ob_ref, z_ref, bg_ref, ga_ref, gb_ref, h_ref, gn_ref, wa_ref, wb_ref, wout_ref, out_ref,
             ya_ref, yb_ref):
        pa, pb, sa, sb = _post_values(oa_ref, ob_ref, z_ref, bg_ref, ga_ref, gb_ref, gn_ref, wa_ref, wb_ref,
                                      ya_ref, yb_ref)
        mixed = (sa * pa + sb * pb).astype(BF16)
        out_ref[...] = h_ref[...] + jnp.dot(mixed, wout_ref[...], preferred_element_type=F32)

    full = lambda i: (0, 0)
    return pl.pallas_call(
        body, name=f"post_fwd_{tag}", grid=(n // tm,),
        in_specs=_post_specs(tm) + [BS((HEADS_W, D_MODEL), full), BS((HEADS_W, D_MODEL), full),
                                    BS((D_MODEL, D_MODEL), full)],
        out_specs=BS((tm, D_MODEL), lambda i: (i, 0)), out_shape=SDS((n, D_MODEL), F32),
        scratch_shapes=[pltpu.VMEM((tm, HEADS_W), BF16), pltpu.VMEM((tm, HEADS_W), BF16)], compiler_params=_params(1),
    )(oa, ob, proj, proj, proj, proj, h, gn8, wa, wb, wout)


def _post_bwd(dh, oa, ob, proj, h, gn8, wa, wb, wa_t, wb_t, wout_t, tag):
    n = h.shape[0]
    tm = _pick(n, (256, 192, 128, 64))

    def body(dh_ref, oa_ref, ob_ref, z_ref, bg_ref, ga_ref, gb_ref, h_ref, gn_ref, wa_ref, wb_ref, wat_ref, wbt_ref,
             woutt_ref, doa_ref, dob_ref, dz_ref, dbg_ref, dga_ref, dgb_ref, dwa_ref, dwb_ref, dwout_ref, dgn_ref,
             ya_ref, yb_ref):
        @pl.when(pl.program_id(0) == 0)
        def _():
            dwa_ref[...] = jnp.zeros_like(dwa_ref)
            dwb_ref[...] = jnp.zeros_like(dwb_ref)
            dwout_ref[...] = jnp.zeros_like(dwout_ref)
            dgn_ref[...] = jnp.zeros_like(dgn_ref)

        pa, pb, sa, sb = _post_values(oa_ref, ob_ref, z_ref, bg_ref, ga_ref, gb_ref, gn_ref, wa_ref, wb_ref,
                                      ya_ref, yb_ref)
        mixed = (sa * pa + sb * pb).astype(BF16)
        dout = dh_ref[...].astype(BF16)
        dwout_ref[...] += _dg(mixed, dout, ((0,), (0,)))
        dmixed = jnp.dot(dout, woutt_ref[...], preferred_element_type=F32)
        dga_ref[...] = (dmixed * pa * sa * (1.0 - sa)).astype(BF16)
        dgb_ref[...] = (dmixed * pb * sb * (1.0 - sb)).astype(BF16)
        dpa = (dmixed * sa).astype(BF16)
        dpb = (dmixed * sb).astype(BF16)
        dwa_ref[...] += _dg(ya_ref[...], dpa, ((0,), (0,)))
        dwb_ref[...] += _dg(yb_ref[...], dpb, ((0,), (0,)))
        dya = jnp.dot(dpa, wat_ref[...], preferred_element_type=F32)
        dyb = jnp.dot(dpb, wbt_ref[...], preferred_element_type=F32)
        dgn_a = jnp.zeros((1, D_HEAD), F32)
        dgn_b = jnp.zeros((1, D_HEAD), F32)
        for hd in range(N_HEADS):
            sl = slice(hd * D_HEAD, (hd + 1) * D_HEAD)
            _, vjp = jax.vjp(_gated_norm, oa_ref[:, sl], z_ref[:, sl], gn_ref[0:1, :])
            doa, dz, dgw = vjp(dya[:, sl])
            doa_ref[:, sl], dz_ref[:, sl], dgn_a = doa, dz.astype(BF16), dgn_a + dgw
            _, vjp = jax.vjp(_gated_norm, ob_ref[:, sl], bg_ref[:, sl], gn_ref[1:2, :])
            dob, dbg, dgw = vjp(dyb[:, sl])
            dob_ref[:, sl], dbg_ref[:, sl], dgn_b = dob, dbg.astype(BF16), dgn_b + dgw
        dgn_ref[0:1, :] += dgn_a
        dgn_ref[1:2, :] += dgn_b

    full = lambda i: (0, 0)
    r2 = lambda i: (i, 0)
    return pl.pallas_call(
        body, name=f"post_bwd_{tag}", grid=(n // tm,),
        in_specs=[BS((tm, D_MODEL), r2)] + _post_specs(tm) + [
            BS((HEADS_W, D_MODEL), full), BS((HEADS_W, D_MODEL), full), BS((D_MODEL, HEADS_W), full),
            BS((D_MODEL, HEADS_W), full), BS((D_MODEL, D_MODEL), full)],
        out_specs=[BS((tm, HEADS_W), r2)] * 4 + [BS((tm, D_MODEL), r2)] * 2 + [
            BS((HEADS_W, D_MODEL), full), BS((HEADS_W, D_MODEL), full), BS((D_MODEL, D_MODEL), full),
            BS((SUBLANES, LANES), full)],
        out_shape=[SDS((n, HEADS_W), F32), SDS((n, HEADS_W), F32), SDS((n, HEADS_W), BF16), SDS((n, HEADS_W), BF16),
                   SDS((n, D_MODEL), BF16), SDS((n, D_MODEL), BF16), SDS((HEADS_W, D_MODEL), F32),
                   SDS((HEADS_W, D_MODEL), F32), SDS((D_MODEL, D_MODEL), F32), SDS((SUBLANES, LANES), F32)],
        scratch_shapes=[pltpu.VMEM((tm, HEADS_W), BF16), pltpu.VMEM((tm, HEADS_W), BF16)], compiler_params=_params(1),
    )(dh, oa, ob, proj, proj, proj, proj, h, gn8, wa, wb, wa_t, wb_t, wout_t)


def _loss_head(h, fw8, target, nseq, t_len):
    n = h.shape[0]
    nc = t_len // GDN_CHUNK
    inv_d = 1.0 / D_MODEL

    def body(h_ref, fw_ref, tgt_ref, dh_ref, acc_ref):
        @pl.when((pl.program_id(0) == 0) & (pl.program_id(1) == 0))
        def _():
            acc_ref[...] = jnp.zeros_like(acc_ref)

        frames = (pl.program_id(1) > 0).astype(F32)
        y, vjp = jax.vjp(_rms, h_ref[...], fw_ref[0:1, :])
        err = (y - tgt_ref[...]) * frames
        dx, dfw = vjp(err * inv_d)
        dh_ref[...] = dx
        acc_ref[0:1, :] += dfw
        acc_ref[1:2, :] += (0.5 * inv_d) * jnp.sum(err * err, axis=0, keepdims=True)

    return pl.pallas_call(
        body, name="loss_head", grid=(nseq, nc),
        in_specs=[BS((GDN_CHUNK, D_MODEL), lambda s, c: (s * nc + c, 0)), BS((SUBLANES, D_MODEL), lambda s, c: (0, 0)),
                  BS((None, GDN_CHUNK, D_MODEL), lambda s, c: (s, jnp.maximum(c - 1, 0), 0))],
        out_specs=[BS((GDN_CHUNK, D_MODEL), lambda s, c: (s * nc + c, 0)), BS((SUBLANES, D_MODEL), lambda s, c: (0, 0))],
        out_shape=[SDS((n, D_MODEL), F32), SDS((SUBLANES, D_MODEL), F32)], compiler_params=_params(2),
    )(h, fw8, target)


def _prep_bwd(proj, dq, dk, dv, db, dg, dqb, dkb, dlf, cw8, aux, lb8, nseq, t_len, tag):
    n = proj.shape[0]
    tt = _pick(t_len, (192, 128, 64))
    nt_ = t_len // tt
    qkv_w = 3 * HEADS_W
    rb = tt // SUBLANES
    ext = tt + SUBLANES

    def body(cur_ref, prev_ref, next_ref, misc_ref, bq_ref, bf_ref, dq_ref, dqn_ref, dk_ref, dkn_ref, dv_ref, dvn_ref,
             db_ref, dg_ref, dqb_ref, dkb_ref, dlf_ref, cw_ref, aux_ref, lb_ref,
             dqkv_ref, dmisc_ref, dbq_ref, dbf_ref, dcw_ref, daux_ref, dlb_ref, dy_ref):
        s, t = pl.program_id(0), pl.program_id(1)

        @pl.when((s == 0) & (t == 0))
        def _():
            dcw_ref[...] = jnp.zeros_like(dcw_ref)
            daux_ref[...] = jnp.zeros_like(daux_ref)
            dlb_ref[...] = jnp.zeros_like(dlb_ref)

        prev = jnp.where(t == 0, 0.0, prev_ref[...])
        x_ext = jnp.concatenate([prev, cur_ref[...], next_ref[...]], axis=0)
        y = _conv_ext(x_ext, cw_ref)
        inside = (t < nt_ - 1) | (_iota2((ext, 1), 0) < tt)
        dy_ref[0:SUBLANES, :] = jnp.zeros((SUBLANES, qkv_w), F32)
        for hd in range(N_HEADS):
            for grp, (g_ref, gn_ref, scale) in enumerate(((dq_ref, dqn_ref, D_HEAD ** -0.5), (dk_ref, dkn_ref, 1.0),
                                                          (dv_ref, dvn_ref, None))):
                lo = grp * HEADS_W + hd * D_HEAD
                sl = slice(hd * D_HEAD, (hd + 1) * D_HEAD)
                cot = jnp.concatenate([g_ref[:, sl], gn_ref[:, sl]], axis=0)
                fn = _silu if scale is None else functools.partial(_l2n_act, scale=scale)
                _, vjp = jax.vjp(fn, y[:, lo:lo + D_HEAD])
                dy_ref[SUBLANES:, lo:lo + D_HEAD] = jnp.where(inside, vjp(cot)[0], 0.0)
        dy_ext = dy_ref[...]
        dx = dy_ext * cw_ref[3:4, :]
        for kk in range(3):
            dx = dx + _shift_up(dy_ext, 3 - kk) * cw_ref[kk:kk + 1, :]
        dqkv_ref[...] = dx[SUBLANES:SUBLANES + tt].astype(BF16)
        dy_cur = dy_ext[SUBLANES:SUBLANES + tt]
        for kk in range(4):
            xs = _shift_down(x_ext, 3 - kk)[SUBLANES:SUBLANES + tt]
            dcw_ref[kk:kk + 1, :] += jnp.sum(xs * dy_cur, axis=0, keepdims=True)

        real = (t * tt + _iota2((tt, 1), 0)) >= N_PAD
        dmisc = jnp.zeros((tt, LANES), F32)
        daux = jnp.zeros((SUBLANES, LANES), F32)
        for hd in range(N_HEADS):
            sl = slice(hd * D_HEAD, (hd + 1) * D_HEAD)
            _, vjp = jax.vjp(lambda m, a: _gdn_gates(m, a, real, hd), misc_ref[...], aux_ref[...])
            dm, da = vjp((db_ref[:, sl], dg_ref[:, sl]))
            dmisc, daux = dmisc + dm, daux + da
        dmisc_ref[...] = dmisc.astype(BF16)
        daux_ref[...] += daux
        _, vjp = jax.vjp(lambda a, b, c: _hgrn_prep(a, b, c, real), bq_ref[...], bf_ref[...], lb_ref[0:1, :])
        dbq, dbf, dlb = vjp((dqb_ref[...], dkb_ref[...], dlf_ref[...]))
        dbq_ref[...], dbf_ref[...] = dbq.astype(BF16), dbf.astype(BF16)
        dlb_ref[0:1, :] += dlb

    row = lambda s, t: s * nt_ + t
    cur = lambda s, t: (row(s, t), 0)
    nxt = lambda s, t: (jnp.minimum((row(s, t) + 1) * rb, n // SUBLANES - 1), 0)
    wide = BS((tt, HEADS_W), cur)
    halo = BS((SUBLANES, HEADS_W), nxt)
    full = lambda s, t: (0, 0)
    return pl.pallas_call(
        body, name=f"prep_bwd_{tag}", grid=(nseq, nt_),
        in_specs=[BS((tt, qkv_w), cur), BS((SUBLANES, qkv_w), lambda s, t: (jnp.maximum(row(s, t) * rb - 1, 0), 0)),
                  BS((SUBLANES, qkv_w), nxt), BS((tt, LANES), lambda s, t: (row(s, t), C_MISC // LANES)),
                  BS((tt, HEADS_W), lambda s, t: (row(s, t), C_BQ // HEADS_W)),
                  BS((tt, HEADS_W), lambda s, t: (row(s, t), C_BF // HEADS_W)),
                  wide, halo, wide, halo, wide, halo, wide, wide, wide, wide, wide,
                  BS((SUBLANES, qkv_w), full), BS((SUBLANES, LANES), full), BS((SUBLANES, HEADS_W), full)],
        out_specs=[BS((tt, qkv_w), cur), BS((tt, LANES), cur), wide, wide,
                   BS((SUBLANES, qkv_w), full), BS((SUBLANES, LANES), full), BS((SUBLANES, HEADS_W), full)],
        out_shape=[SDS((n, qkv_w), BF16), SDS((n, LANES), BF16), SDS((n, HEADS_W), BF16), SDS((n, HEADS_W), BF16),
                   SDS((SUBLANES, qkv_w), F32), SDS((SUBLANES, LANES), F32), SDS((SUBLANES, HEADS_W), F32)],
        scratch_shapes=[pltpu.VMEM((tt + 2 * SUBLANES, qkv_w), F32)], compiler_params=_params(2),
    )(proj, proj, proj, proj, proj, proj, dq, dq, dk, dk, dv, dv, db, dg, dqb, dkb, dlf, cw8, aux, lb8)


def _proj_bwd_x(pieces, wp_t, h, nw8, dh_res, tag):
    n = h.shape[0]
    tm = _pick(n, (256, 192, 128, 64))
    widths = [p.shape[1] for p in pieces]
    assert sum(widths) == PROJ_W

    def body(*refs):
        p_refs = refs[:len(pieces)]
        wt_ref, h_ref, nw_ref, dres_ref, dh_ref, cat_ref, dnw_ref = refs[len(pieces):]

        @pl.when(pl.program_id(0) == 0)
        def _():
            dnw_ref[...] = jnp.zeros_like(dnw_ref)

        off = 0
        for p_ref, w in zip(p_refs, widths):
            cat_ref[:, off:off + w] = p_ref[...]
            off += w
        dxn = jnp.dot(cat_ref[...], wt_ref[...], preferred_element_type=F32)
        _, vjp = jax.vjp(_rms, h_ref[...], nw_ref[0:1, :])
        dx, dnw = vjp(dxn)
        dh_ref[...] = dres_ref[...] + dx
        dnw_ref[0:1, :] += dnw

    r2 = lambda i: (i, 0)
    full = lambda i: (0, 0)
    return pl.pallas_call(
        body, name=f"proj_bwd_x_{tag}", grid=(n // tm,),
        in_specs=[BS((tm, w), r2) for w in widths] + [BS((PROJ_W, D_MODEL), full), BS((tm, D_MODEL), r2),
                                                      BS((SUBLANES, D_MODEL), full), BS((tm, D_MODEL), r2)],
        out_specs=[BS((tm, D_MODEL), r2), BS((tm, PROJ_W), r2), BS((SUBLANES, D_MODEL), full)],
        out_shape=[SDS((n, D_MODEL), F32), SDS((n, PROJ_W), BF16), SDS((SUBLANES, D_MODEL), F32)],
        compiler_params=_params(1),
    )(*pieces, wp_t, h, nw8, dh_res)


def _proj_bwd_w(xn, dproj, tag):
    n = xn.shape[0]
    tm = _pick(n, (768, 512, 384, 256, 192, 128, 64))
    tn = 896

    def body(x_ref, d_ref, o_ref):
        @pl.when(pl.program_id(1) == 0)
        def _():
            o_ref[...] = jnp.zeros_like(o_ref)

        o_ref[...] += _dg(x_ref[...], d_ref[...], ((0,), (0,)))

    return pl.pallas_call(
        body, name=f"proj_bwd_w_{tag}", grid=(PROJ_W // tn, n // tm),
        in_specs=[BS((tm, D_MODEL), lambda j, i: (i, 0)), BS((tm, tn), lambda j, i: (i, j))],
        out_specs=BS((D_MODEL, tn), lambda j, i: (0, j)), out_shape=SDS((D_MODEL, PROJ_W), F32),
        compiler_params=_params(2),
    )(xn, dproj)


def _adamw(w, g, m, v, name):
    rows, cols = w.shape
    tr = _pick(rows, (256, 128, 64, 32, 16, 8, 4, 2, 1)) if rows > 256 else rows

    def body(w_ref, g_ref, m_ref, v_ref, d_ref, nm_ref, nv_ref):
        gr = g_ref[...]
        m_new = ADAM_B1 * m_ref[...] + (1.0 - ADAM_B1) * gr
        v_new = ADAM_B2 * v_ref[...] + (1.0 - ADAM_B2) * jnp.square(gr)
        m_hat = m_new / (1.0 - ADAM_B1 ** ADAM_STEP)
        v_hat = v_new / (1.0 - ADAM_B2 ** ADAM_STEP)
        d_ref[...] = -ADAM_LR * (m_hat / (jnp.sqrt(v_hat) + ADAM_EPS) + ADAM_WD * w_ref[...])
        nm_ref[...] = m_new
        nv_ref[...] = v_new

    blk = BS((tr, cols), lambda i: (i, 0))
    return pl.pallas_call(
        body, name=name, grid=(rows // tr,), in_specs=[blk] * 4, out_specs=[blk] * 3,
        out_shape=[SDS((rows, cols), F32)] * 3, compiler_params=_params(1),
    )(w, g, m, v)


def _row8(v, width):
    v = jnp.atleast_2d(v).astype(F32)
    return jnp.pad(v, ((0, SUBLANES - v.shape[0]), (0, width - v.shape[1])))


def _to_layout(w_full):
    return jnp.concatenate([w_full[:, 0:1536], w_full[:, 1544:6152], w_full[:, 1536:1544],
                            jnp.zeros((w_full.shape[0], PROJ_W - REF_W), w_full.dtype)], axis=1)


def _from_layout(dw):
    return jnp.concatenate([dw[:, 0:1536], dw[:, C_MISC:C_MISC + 8], dw[:, 1536:C_MISC]], axis=1)


def _lower_bounds(lb):
    sm = jax.nn.softmax(lb.astype(F32), axis=0)
    return jnp.cumsum(sm, axis=0) - sm[0]


def kernel(x, meta_tokens, norm_w, w_in, conv_w, a_log, dt_bias, gnorm_a, gnorm_b, hgrn_lower_bounds, w_branch_a, w_branch_b, w_out, final_norm_w, loss_target, m_meta_tokens, m_norm_w, m_w_in, m_conv_w, m_a_log, m_dt_bias, m_gnorm_a, m_gnorm_b, m_hgrn_lower_bounds, m_w_branch_a, m_w_branch_b, m_w_out, m_final_norm_w, v_meta_tokens, v_norm_w, v_w_in, v_conv_w, v_a_log, v_dt_bias, v_gnorm_a, v_gnorm_b, v_hgrn_lower_bounds, v_w_branch_a, v_w_branch_b, v_w_out, v_final_norm_w):
    nseq, seq, _ = x.shape
    depth = norm_w.shape[0]
    t_len = N_PAD + N_META + seq
    n = nseq * t_len
    win_c, conv_c = w_in.shape[2], conv_w.shape[2]
    my = 4 * lax.axis_index("x") + 2 * lax.axis_index("y") + lax.axis_index("c")

    parts = [w_in.astype(BF16).reshape(-1), w_branch_a.astype(BF16).reshape(-1), w_branch_b.astype(BF16).reshape(-1),
             w_out.astype(BF16).reshape(-1), lax.bitcast_convert_type(conv_w, BF16).reshape(-1),
             lax.bitcast_convert_type(meta_tokens, BF16).reshape(-1)]
    sizes = [p.shape[0] for p in parts]
    flat = jnp.concatenate(parts)
    rows = -(-flat.shape[0] // (D_MODEL * 16)) * 16
    flat = jnp.pad(flat, (0, rows * D_MODEL - flat.shape[0]))
    gathered = _all_gather_hbm(flat.reshape(rows, D_MODEL), "gather_weights").reshape(N_DEV, rows * D_MODEL)
    offs = np.cumsum([0] + sizes)
    seg = lambda i: gathered[:, offs[i]:offs[i + 1]]
    w_in_full = seg(0).reshape(N_DEV, depth, D_MODEL, win_c).transpose(1, 2, 0, 3).reshape(depth, D_MODEL, REF_W)
    wa_full = seg(1).reshape(N_DEV, depth, HEADS_W, LANES).transpose(1, 2, 0, 3).reshape(depth, HEADS_W, D_MODEL)
    wb_full = seg(2).reshape(N_DEV, depth, HEADS_W, LANES).transpose(1, 2, 0, 3).reshape(depth, HEADS_W, D_MODEL)
    wout_full = seg(3).reshape(N_DEV, depth, LANES, D_MODEL).transpose(1, 0, 2, 3).reshape(depth, D_MODEL, D_MODEL)
    conv_full = lax.bitcast_convert_type(seg(4).reshape(N_DEV, depth, 4, conv_c, 2), F32)
    conv_full = conv_full.transpose(1, 2, 0, 3).reshape(depth, 4, 3 * HEADS_W)
    meta_full = lax.bitcast_convert_type(seg(5).reshape(N_DEV, N_META, LANES, 2), F32)
    meta_full = meta_full.transpose(1, 0, 2).reshape(N_META, D_MODEL)

    lb_all, lb_vjp = jax.vjp(_lower_bounds, hgrn_lower_bounds)

    h = jnp.concatenate([jnp.zeros((nseq, N_PAD, D_MODEL), F32),
                         jnp.broadcast_to(meta_full[None], (nseq, N_META, D_MODEL)), x], axis=1).reshape(n, D_MODEL)
    saved = []
    for l in range(depth):
        wp = _to_layout(w_in_full[l])
        nw8 = _row8(norm_w[l], D_MODEL)
        cw8 = _row8(conv_full[l], 3 * HEADS_W)
        aux = _row8(jnp.stack([a_log[l], dt_bias[l]]), LANES)
        lb8 = _row8(lb_all[l], HEADS_W)
        gn8 = _row8(jnp.stack([gnorm_a[l], gnorm_b[l]]), LANES)
        proj, xn = _proj_fwd(h, nw8, wp, l)
        q, k, v, b, g, qb, kb, lf = _prep_fwd(proj, cw8, aux, lb8, nseq, t_len, l)
        oa, sck_a = _gdn_fwd(q, k, v, b, g, nseq, t_len, l)
        ob, sck_b = _hgrn_fwd(qb, kb, proj, C_BI // HEADS_W, lf, nseq, t_len, l)
        h_next = _post_fwd(oa, ob, proj, h, gn8, wa_full[l], wb_full[l], wout_full[l], l)
        saved.append(dict(h=h, wp=wp, nw8=nw8, cw8=cw8, aux=aux, lb8=lb8, gn8=gn8, proj=proj, xn=xn, q=q, k=k, v=v, b=b,
                          g=g, qb=qb, kb=kb, lf=lf, oa=oa, ob=ob, sck_a=sck_a, sck_b=sck_b))
        h = h_next

    dh, acc = _loss_head(h, _row8(final_norm_w, D_MODEL), loss_target, nseq, t_len)

    g_win, g_wa, g_wb, g_wout, g_conv, small = [], [], [], [], [], []
    for l in reversed(range(depth)):
        s = saved[l]
        doa, dob, dz, dbg, dga, dgb, dwa, dwb, dwout, dgn = _post_bwd(
            dh, s["oa"], s["ob"], s["proj"], s["h"], s["gn8"], wa_full[l], wb_full[l], wa_full[l].T, wb_full[l].T,
            wout_full[l].T, l)
        dq, dk, dv, db, dg = _gdn_bwd(s["q"], s["k"], s["v"], s["b"], s["g"], s["sck_a"], doa, nseq, t_len, l)
        dqb, dkb, dbi, dlf = _hgrn_bwd(s["qb"], s["kb"], s["proj"], C_BI // HEADS_W, s["lf"], s["sck_b"], dob, nseq,
                                       t_len, l)
        dqkv, dmisc, dbq, dbf, dcw, daux, dlb = _prep_bwd(s["proj"], dq, dk, dv, db, dg, dqb, dkb, dlf, s["cw8"],
                                                          s["aux"], s["lb8"], nseq, t_len, l)
        dh, dproj, dnw = _proj_bwd_x([dqkv, dz, dbq, dbf, dbi, dbg, dga, dgb, dmisc], s["wp"].T, s["h"], s["nw8"], dh, l)
        g_win.append(_from_layout(_proj_bwd_w(s["xn"], dproj, l)))
        g_wa.append(dwa)
        g_wb.append(dwb)
        g_wout.append(dwout)
        g_conv.append(dcw[:4])
        small.append((dnw[0], dgn[0], dgn[1], daux[0, :N_HEADS], daux[1, :N_HEADS], dlb[0]))
    for lst in (g_win, g_wa, g_wb, g_wout, g_conv, small):
        lst.reverse()
    dh = dh.reshape(nseq, t_len, D_MODEL)
    grad_x = dh[:, N_PAD + N_META:]

    packed = jnp.concatenate([small[0][1], small[1][1], small[0][2], small[1][2], small[0][3], small[1][3],
                              small[0][4], small[1][4]])
    tile = jnp.concatenate([
        jnp.sum(dh[:, N_PAD:N_PAD + N_META], axis=0), _row8(jnp.stack([small[0][0], small[1][0], acc[0]]), D_MODEL),
        _row8(jnp.stack([small[0][5], small[1][5]]), D_MODEL), _row8(packed, D_MODEL), _row8(acc[1], D_MODEL)], axis=0)
    tile = _all_reduce_small(tile, "reduce_small")
    loss = jnp.sum(tile[40])
    g_meta = lax.dynamic_slice_in_dim(tile[0:N_META], my * LANES, LANES, axis=1)
    g_norm, g_final = tile[16:18], tile[18]
    (g_lb,) = lb_vjp(tile[24:26, :HEADS_W])
    r21 = tile[32]
    g_gna, g_gnb = r21[0:256].reshape(2, LANES), r21[256:512].reshape(2, LANES)
    g_alog, g_dtb = r21[512:520].reshape(2, N_HEADS), r21[520:528].reshape(2, N_HEADS)

    dwin, dwa_, dwb_, dwout_, dconv = (jnp.stack(a) for a in (g_win, g_wa, g_wb, g_wout, g_conv))
    g_sizes = [depth * D_MODEL * win_c, depth * HEADS_W * LANES, depth * HEADS_W * LANES, depth * LANES * D_MODEL,
               depth * 4 * conv_c]
    g_rows = -(-sum(g_sizes) // (D_MODEL * GDN_CHUNK)) * GDN_CHUNK
    slabs = []
    for j in range(N_DEV):
        col = lambda a, w: a[:, :, j * w:(j + 1) * w].reshape(-1)
        fl = jnp.concatenate([col(dwin, win_c), col(dwa_, LANES), col(dwb_, LANES),
                              dwout_[:, j * LANES:(j + 1) * LANES].reshape(-1), col(dconv, conv_c)])
        slabs.append(jnp.pad(fl, (0, g_rows * D_MODEL - fl.shape[0])).reshape(g_rows, D_MODEL))
    coords = jnp.stack([lax.axis_index("x"), lax.axis_index("y"), lax.axis_index("c")]).astype(jnp.int32)
    mine = _reduce_scatter(jnp.stack(slabs), coords, "grads").reshape(-1)
    g_offs = np.cumsum([0] + g_sizes)
    gseg = lambda i, shape: mine[g_offs[i]:g_offs[i + 1]].reshape(shape)
    grads = {
        "meta_tokens": g_meta, "norm_w": g_norm, "w_in": gseg(0, w_in.shape), "conv_w": gseg(4, conv_w.shape),
        "a_log": g_alog, "dt_bias": g_dtb, "gnorm_a": g_gna, "gnorm_b": g_gnb, "hgrn_lower_bounds": g_lb,
        "w_branch_a": gseg(1, w_branch_a.shape), "w_branch_b": gseg(2, w_branch_b.shape), "w_out": gseg(3, w_out.shape),
        "final_norm_w": g_final}
    weights = {
        "meta_tokens": (meta_tokens, m_meta_tokens, v_meta_tokens), "norm_w": (norm_w, m_norm_w, v_norm_w),
        "w_in": (w_in, m_w_in, v_w_in), "conv_w": (conv_w, m_conv_w, v_conv_w), "a_log": (a_log, m_a_log, v_a_log),
        "dt_bias": (dt_bias, m_dt_bias, v_dt_bias), "gnorm_a": (gnorm_a, m_gnorm_a, v_gnorm_a),
        "gnorm_b": (gnorm_b, m_gnorm_b, v_gnorm_b),
        "hgrn_lower_bounds": (hgrn_lower_bounds, m_hgrn_lower_bounds, v_hgrn_lower_bounds),
        "w_branch_a": (w_branch_a, m_w_branch_a, v_w_branch_a), "w_branch_b": (w_branch_b, m_w_branch_b, v_w_branch_b),
        "w_out": (w_out, m_w_out, v_w_out), "final_norm_w": (final_norm_w, m_final_norm_w, v_final_norm_w)}
    names = list(weights)
    deltas, new_m, new_v = [], [], []
    for nm in names:
        w, m, v = weights[nm]
        view = (-1, w.shape[-1])
        d, m2, v2 = _adamw(w.reshape(view), grads[nm].reshape(view), m.reshape(view), v.reshape(view), f"adamw_{nm}")
        deltas.append(d.reshape(w.shape))
        new_m.append(m2.reshape(w.shape))
        new_v.append(v2.reshape(w.shape))
    return (loss, grad_x, *[grads[nm].reshape(weights[nm][0].shape) for nm in names], *deltas, *new_m, *new_v)
```

```python
import functools

import jax
import jax.numpy as jnp
import numpy as np
from jax import lax
from jax.experimental import pallas as pl
from jax.experimental.pallas import tpu as pltpu

F32 = jnp.float32
BF16 = jnp.bfloat16

D_MODEL = 1024
N_HEADS = 4
D_HEAD = 128
HEADS_W = N_HEADS * D_HEAD
N_META = 16
N_PAD = 48
GDN_CHUNK = 64
HGRN_CHUNK = 16
EPS = 1e-6
N_DEV = 8
LANES = 128
SUBLANES = 8
VMEM_LIMIT = 56 * 1024 * 1024

C_QKV, C_Z, C_BQ, C_BF, C_BI, C_BG, C_GA, C_GB, C_MISC = 0, 1536, 2048, 2560, 3072, 3584, 4096, 5120, 6144
PROJ_W = 6272
REF_W = 6152

ADAM_LR, ADAM_B1, ADAM_B2, ADAM_EPS, ADAM_WD, ADAM_STEP = 0.001, 0.9, 0.999, 1e-08, 0.01, 10

MESH = pl.DeviceIdType.MESH
SDS = jax.ShapeDtypeStruct
BS = pl.BlockSpec


def _params(n_axes):
    return pltpu.CompilerParams(dimension_semantics=("arbitrary",) * n_axes, vmem_limit_bytes=VMEM_LIMIT)


def _pick(n, cands):
    for c in cands:
        if n % c == 0:
            return c
    raise ValueError(f"no tile for {n} among {cands}")


def _iota2(shape, dim):
    return lax.broadcasted_iota(jnp.int32, shape, dim)


def _dg(a, b, dims):
    return lax.dot_general(a.astype(BF16), b.astype(BF16), (dims, ((), ())), preferred_element_type=F32)


def _bdg(a, b, ca, cb):
    return lax.dot_general(a.astype(BF16), b.astype(BF16), (((ca,), (cb,)), ((0,), (0,))), preferred_element_type=F32)


@jax.custom_vjp
def _bnn(a, b):
    return _bdg(a, b, 2, 1)


@jax.custom_vjp
def _bnt(a, b):
    return _bdg(a, b, 2, 2)


@jax.custom_vjp
def _btn(a, b):
    return _bdg(a, b, 1, 1)


_bnn.defvjp(lambda a, b: (_bnn(a, b), (a, b)), lambda r, g: (_bnt(g, r[1]), _btn(r[0], g)))
_bnt.defvjp(lambda a, b: (_bnt(a, b), (a, b)), lambda r, g: (_bnn(g, r[1]), _btn(g, r[0])))
_btn.defvjp(lambda a, b: (_btn(a, b), (a, b)), lambda r, g: (_bnt(r[1], g), _bnn(r[0], g)))


def _split2(x):
    hi = x.astype(BF16).astype(F32)
    return hi, x - hi


def _tri(bsz, n):
    return jnp.broadcast_to((_iota2((n, n), 0) >= _iota2((n, n), 1)).astype(F32), (bsz, n, n))


@jax.custom_vjp
def _cumsum_rows(x):
    tri = _tri(x.shape[0], x.shape[1])
    hi, lo = _split2(x)
    return _bdg(tri, hi, 2, 1) + _bdg(tri, lo, 2, 1)


def _cumsum_rows_bwd(_, g):
    tri = _tri(g.shape[0], g.shape[1])
    hi, lo = _split2(g)
    return (_bdg(tri, hi, 1, 1) + _bdg(tri, lo, 1, 1),)


_cumsum_rows.defvjp(lambda x: (_cumsum_rows(x), None), _cumsum_rows_bwd)


def _sigmoid(x):
    return jax.nn.sigmoid(x)


def _silu(x):
    return x * _sigmoid(x)


def _softplus(x):
    return jnp.maximum(x, 0.0) + jnp.log1p(jnp.exp(-jnp.abs(x)))


def _rms(x, w):
    return x * lax.rsqrt(jnp.mean(x * x, axis=-1, keepdims=True) + EPS) * w


def _inv_unit_lower(lm):
    n = lm.shape[1]
    a = (_iota2((n, n), 0) == _iota2((n, n), 1)).astype(F32)[None] - lm
    steps = max(1, (n - 1).bit_length()) - 1
    p = _bnn(lm, lm)
    for i in range(steps):
        if i == steps - 1:
            a = a + _bnn(a, p)
        else:
            both = _bnn(jnp.concatenate([a, p], axis=1), p)
            a, p = a + both[:, :n], both[:, n:]
    return a


def _gdn_chunk(q, k, v, b_b, g_b, s):
    n, dv = q.shape[1], v.shape[2]
    r, c = _iota2((n, n), 0), _iota2((n, n), 1)
    causal, strict, eye = (r >= c)[None], (r > c)[None], (r == c)[None]
    g_cum = _cumsum_rows(g_b)
    g_i = g_cum[:, :, :n]
    g_j = jnp.sum(jnp.where(eye, g_i, 0.0), axis=1, keepdims=True)
    decay = jnp.where(causal, jnp.exp(jnp.where(causal, g_i - g_j, 0.0)), 0.0)
    e_g = jnp.exp(g_cum)
    kb = k * b_b
    kk = _bnt(jnp.concatenate([kb, q], axis=1), k)
    a_inv = _inv_unit_lower(jnp.where(strict, kk[:, :n] * decay, 0.0))
    uw = _bnn(a_inv, jnp.concatenate([v * b_b, kb * e_g], axis=2))
    ws = _bnn(jnp.concatenate([uw[:, :, dv:], q * e_g], axis=1), s)
    v_new = uw[:, :, :dv] - ws[:, :n]
    o = ws[:, n:] + _bnn(kk[:, n:] * decay, v_new)
    g_last = g_cum[:, n - 1:n, :]
    s_new = s * jnp.exp(g_last) + _btn(k * jnp.exp(g_last - g_cum), v_new)
    return o, s_new


@functools.partial(jax.custom_vjp, nondiff_argnums=(1, 2))
def _row(x, j, n):
    return x[:, j:j + 1, :]


def _row_bwd(j, n, _, g):
    return (jnp.where(_iota2((1, n, 1), 1) == j, g, 0.0),)


_row.defvjp(lambda x, j, n: (_row(x, j, n), None), _row_bwd)


def _hgrn_chunk(q, k, v, lf, st):
    n = q.shape[1]
    b_cum = _cumsum_rows(lf)
    o = _bnt(q * jnp.exp(b_cum), st)
    rows = _iota2((1, n, 1), 1)
    for j in range(n):
        p = jnp.exp(jnp.where(rows >= j, b_cum - _row(b_cum, j, n), -1e30))
        o = o + jnp.sum(q * _row(k, j, n) * p, axis=2, keepdims=True) * _row(v, j, n)
    b_last = _row(b_cum, n - 1, n)
    st_new = st * jnp.exp(b_last) + _btn(v, k * jnp.exp(b_last - b_cum))
    return o, st_new


def _l2n_act(y, scale):
    a = _silu(y)
    return a * lax.rsqrt(jnp.sum(a * a, axis=-1, keepdims=True) + EPS) * scale


def _col(x, lane):
    return jnp.sum(jnp.where(_iota2(x.shape, 1) == lane, x, 0.0), axis=1, keepdims=True)


def _elem(x, row, lane):
    m = (_iota2(x.shape, 0) == row) & (_iota2(x.shape, 1) == lane)
    return jnp.sum(jnp.sum(jnp.where(m, x, 0.0), axis=1, keepdims=True), axis=0, keepdims=True)


def _gdn_gates(misc, aux, real, head):
    beta = _sigmoid(_col(misc, head))
    g = -jnp.exp(_elem(aux, 0, head)) * _softplus(_col(misc, N_HEADS + head) + _elem(aux, 1, head))
    g = jnp.where(real, g, 0.0)
    shape = (misc.shape[0], D_HEAD)
    return jnp.broadcast_to(beta, shape), jnp.broadcast_to(g, shape)


def _hgrn_prep(bq, bf, lb, real):
    qb = _silu(bq) * (D_HEAD ** -0.5)
    log_sig = jnp.minimum(bf, 0.0) - jnp.log1p(jnp.exp(-jnp.abs(bf)))
    pos = lb > 0.0
    lbs = jnp.where(pos, lb, 0.5)
    a = jnp.log(lbs)
    b = jnp.log1p(-lbs) + log_sig
    lae = jnp.maximum(a, b) + jnp.log1p(jnp.exp(-jnp.abs(a - b)))
    lf = jnp.where(pos, lae, log_sig)
    kb = jnp.where(pos, 1.0 - lbs, 1.0) * _sigmoid(-bf)
    return qb, jnp.where(real, kb, 0.0), jnp.where(real, lf, 0.0)


def _gated_norm(o, z, gw):
    return o * lax.rsqrt(jnp.mean(o * o, axis=-1, keepdims=True) + EPS) * gw * _silu(z)


def _shift_down(x, j):
    return x if j == 0 else pltpu.roll(x, j, 0)


def _shift_up(x, j):
    return x if j == 0 else pltpu.roll(x, x.shape[0] - j, 0)


def _all_gather_hbm(blocks, name):
    na = len(blocks)

    def body(*refs):
        x_refs, out_refs = refs[:na], refs[na:2 * na]
        send_sems, recv_sems, local_sems = refs[2 * na:]
        mx, my, mc = lax.axis_index("x"), lax.axis_index("y"), lax.axis_index("c")
        me, sibling = (mx, my, mc), (mx, my, 1 - mc)
        chips = [(1 - mx, my), (mx, 1 - my), (1 - mx, 1 - my)]

        def slab(a, px, py, pc):
            return out_refs[a].at[4 * px + 2 * py + pc]

        def copy(a, k, blk, to, own=False):
            return pltpu.make_async_remote_copy(
                src_ref=x_refs[a] if own else slab(a, *blk), dst_ref=slab(a, *blk),
                send_sem=send_sems.at[7 * a + k], recv_sem=recv_sems.at[7 * a + k], device_id=to, device_id_type=MESH)

        mine = [pltpu.make_async_copy(x_refs[a], slab(a, *me), local_sems.at[a]) for a in range(na)]
        for cp in mine:
            cp.start()
        first = [copy(a, 0, me, sibling, own=True) for a in range(na)]
        first += [copy(a, 1 + j, me, (*chip, mc), own=True) for j, chip in enumerate(chips) for a in range(na)]
        for cp in first:
            cp.start()
        passed = []
        for j, chip in enumerate(chips):
            for a in range(na):
                copy(a, 1 + j, (*chip, mc), me).wait_recv()
                passed.append(copy(a, 4 + j, (*chip, mc), sibling))
                passed[-1].start()
        for a in range(na):
            copy(a, 0, sibling, me).wait_recv()
            for j, chip in enumerate(chips):
                copy(a, 4 + j, (*chip, 1 - mc), me).wait_recv()
        for cp in first + passed:
            cp.wait_send()
        for cp in mine:
            cp.wait()

    hbm = BS(memory_space=pl.ANY)
    return pl.pallas_call(
        body, name=name, out_shape=[SDS((N_DEV, *b.shape), b.dtype) for b in blocks],
        in_specs=[hbm] * na, out_specs=[hbm] * na,
        scratch_shapes=[pltpu.SemaphoreType.DMA((7 * na,)), pltpu.SemaphoreType.DMA((7 * na,)),
                        pltpu.SemaphoreType.DMA((na,))],
    )(*blocks)


def _all_reduce_small(block, name):
    r, c = block.shape

    def body(x_ref, out_ref, buf, send_sems, recv_sems):
        mx, my, mc = lax.axis_index("x"), lax.axis_index("y"), lax.axis_index("c")
        me, sibling = (mx, my, mc), (mx, my, 1 - mc)
        chips = [(1 - mx, my), (mx, 1 - my), (1 - mx, 1 - my)]

        def slab(px, py, pc):
            return buf.at[4 * px + 2 * py + pc]

        def copy(k, blk, to, src=None):
            return pltpu.make_async_remote_copy(
                src_ref=slab(*blk) if src is None else src, dst_ref=slab(*blk),
                send_sem=send_sems.at[k], recv_sem=recv_sems.at[k], device_id=to, device_id_type=MESH)

        first = [copy(0, me, sibling, src=x_ref)]
        first += [copy(1 + j, me, (*chip, mc), src=x_ref) for j, chip in enumerate(chips)]
        for cp in first:
            cp.start()
        passed = [copy(4 + j, (*chip, mc), sibling) for j, chip in enumerate(chips)]
        for j, chip in enumerate(chips):
            copy(1 + j, (*chip, mc), me).wait_recv()
            passed[j].start()
        copy(0, sibling, me).wait_recv()
        for j, chip in enumerate(chips):
            copy(4 + j, (*chip, 1 - mc), me).wait_recv()
        for cp in first + passed:
            cp.wait_send()
        buf[4 * mx + 2 * my + mc] = x_ref[...]
        acc = buf[0]
        for d in range(1, N_DEV):
            acc = acc + buf[d]
        out_ref[...] = acc

    return pl.pallas_call(
        body, name=name, out_shape=SDS((r, c), F32),
        in_specs=[BS(memory_space=pltpu.VMEM)], out_specs=BS(memory_space=pltpu.VMEM),
        scratch_shapes=[pltpu.VMEM((N_DEV, r, c), F32), pltpu.SemaphoreType.DMA((7,)), pltpu.SemaphoreType.DMA((7,))],
    )(block)


def _exchange(bufs, flip, paired, name):
    na, n = len(bufs), bufs[0].shape[0]
    axis = ("x", "y", "c")[flip]

    def body(*refs):
        g_refs, out_refs = refs[:na], refs[na:2 * na]
        send_sems, recv_sems = refs[2 * na:]
        pos = [lax.axis_index("x"), lax.axis_index("y"), lax.axis_index("c")]
        pos[flip] = 1 - pos[flip]
        other = 1 - lax.axis_index(axis)
        copies = [pltpu.make_async_remote_copy(
            src_ref=g_refs[a].at[i, other] if paired else g_refs[a].at[i], dst_ref=out_refs[a].at[i],
            send_sem=send_sems.at[n * a + i], recv_sem=recv_sems.at[n * a + i], device_id=tuple(pos),
            device_id_type=MESH) for i in range(n) for a in range(na)]
        for cp in copies:
            cp.start()
        for cp in copies:
            cp.wait_recv()
        for cp in copies:
            cp.wait_send()

    hbm = BS(memory_space=pl.ANY)
    return pl.pallas_call(
        body, name=name, out_shape=[SDS((n, *b.shape[(2 if paired else 1):]), b.dtype) for b in bufs],
        in_specs=[hbm] * na, out_specs=[hbm] * na,
        scratch_shapes=[pltpu.SemaphoreType.DMA((n * na,)), pltpu.SemaphoreType.DMA((n * na,))],
    )(*bufs)


def _rs_tile(r):
    return _pick(r, (704, 512, 352, 256, 192, 128, 64, 32, 16, 8))


def _rs_add_c(g4, recv, coords, name):
    _, _, r, c = g4.shape
    tr = _rs_tile(r)

    def body(co_ref, a0_ref, a1_ref, b0_ref, b1_ref, keep_ref, send_ref):
        s0 = a0_ref[...] + b0_ref[...]
        s1 = a1_ref[...] + b1_ref[...]
        mine = co_ref[1] == 0
        keep_ref[...] = jnp.where(mine, s0, s1)
        send_ref[...] = jnp.where(mine, s1, s0).astype(BF16)

    blk = lambda yy: BS((None, None, tr, c), functools.partial(lambda i, j, co, yy: (2 * i + yy, co[2], j, 0), yy=yy))
    rblk = lambda yy: BS((None, tr, c), functools.partial(lambda i, j, co, yy: (2 * i + yy, j, 0), yy=yy))
    out = BS((None, tr, c), lambda i, j, co: (i, j, 0))
    return pl.pallas_call(
        body, name=name, out_shape=[SDS((2, r, c), F32), SDS((2, r, c), BF16)],
        grid_spec=pltpu.PrefetchScalarGridSpec(num_scalar_prefetch=1, grid=(2, r // tr),
                                               in_specs=[blk(0), blk(1), rblk(0), rblk(1)], out_specs=[out, out]),
        compiler_params=_params(2),
    )(coords, g4, g4, recv, recv)


def _rs_add_y(kept, recv, coords, name):
    _, r, c = kept.shape
    tr = _rs_tile(r)

    def body(co_ref, a_ref, b_ref, keep_ref, send_ref):
        s0 = a_ref[0] + b_ref[0].astype(F32)
        s1 = a_ref[1] + b_ref[1].astype(F32)
        mine = co_ref[0] == 0
        keep_ref[...] = jnp.where(mine, s0, s1)
        send_ref[0] = jnp.where(mine, s1, s0).astype(BF16)

    blk = BS((2, tr, c), lambda j, co: (0, j, 0))
    return pl.pallas_call(
        body, name=name, out_shape=[SDS((r, c), F32), SDS((1, r, c), BF16)],
        grid_spec=pltpu.PrefetchScalarGridSpec(num_scalar_prefetch=1, grid=(r // tr,), in_specs=[blk, blk],
                                               out_specs=[BS((tr, c), lambda j, co: (j, 0)),
                                                          BS((1, tr, c), lambda j, co: (0, j, 0))]),
        compiler_params=_params(1),
    )(coords, kept, recv)


def _rs_add_x(kept, recv, name):
    r, c = kept.shape
    tr = _rs_tile(r)

    def body(a_ref, b_ref, o_ref):
        o_ref[...] = a_ref[...] + b_ref[0].astype(F32)

    return pl.pallas_call(
        body, name=name, grid=(r // tr,), out_shape=SDS((r, c), F32),
        in_specs=[BS((tr, c), lambda j: (j, 0)), BS((1, tr, c), lambda j: (0, j, 0))],
        out_specs=BS((tr, c), lambda j: (j, 0)), compiler_params=_params(1),
    )(kept, recv)


def _reduce_scatter(arrays, coords, tag):
    ids = range(len(arrays))
    g4 = [a.reshape(4, 2, *a.shape[1:]) for a in arrays]
    got = _exchange(g4, 2, True, f"rs_c_{tag}")
    kept, send = zip(*[_rs_add_c(g4[i], got[i], coords, f"rs_c_add_{tag}{i}") for i in ids])
    got = _exchange(list(send), 1, False, f"rs_y_{tag}")
    kept, send = zip(*[_rs_add_y(kept[i], got[i], coords, f"rs_y_add_{tag}{i}") for i in ids])
    got = _exchange(list(send), 0, False, f"rs_x_{tag}")
    return [_rs_add_x(kept[i], got[i], f"rs_x_add_{tag}{i}") for i in ids]


def _proj_fwd(h, nw8, wp, tag):
    n = h.shape[0]
    tm = _pick(n, (768, 512, 384, 256, 192, 128, 64))
    tn = 896

    def body(h_ref, nw_ref, w_ref, proj_ref, xn_ref):
        @pl.when(pl.program_id(1) == 0)
        def _():
            xn_ref[...] = _rms(h_ref[...], nw_ref[0:1, :]).astype(BF16)

        proj_ref[...] = jnp.dot(xn_ref[...], w_ref[...], preferred_element_type=F32)

    return pl.pallas_call(
        body, name=f"proj_fwd_{tag}", grid=(n // tm, PROJ_W // tn),
        in_specs=[BS((tm, D_MODEL), lambda i, j: (i, 0)), BS((SUBLANES, D_MODEL), lambda i, j: (0, 0)),
                  BS((D_MODEL, tn), lambda i, j: (0, j))],
        out_specs=[BS((tm, tn), lambda i, j: (i, j)), BS((tm, D_MODEL), lambda i, j: (i, 0))],
        out_shape=[SDS((n, PROJ_W), F32), SDS((n, D_MODEL), BF16)], compiler_params=_params(2),
    )(h, nw8, wp)


def _conv_ext(x_ext, cw_ref):
    y = x_ext * cw_ref[3:4, :]
    for k in range(3):
        y = y + _shift_down(x_ext, 3 - k) * cw_ref[k:k + 1, :]
    return y[SUBLANES:]


def _prep_fwd(proj, cw8, aux, lb8, nseq, t_len, tag):
    n = proj.shape[0]
    tt = _pick(t_len, (192, 128, 64))
    nt_ = t_len // tt
    qkv_w = 3 * HEADS_W

    def body(cur_ref, prev_ref, misc_ref, bq_ref, bf_ref, cw_ref, aux_ref, lb_ref,
             q_ref, k_ref, v_ref, b_ref, g_ref, qb_ref, kb_ref, lf_ref):
        t = pl.program_id(1)
        prev = jnp.where(t == 0, 0.0, prev_ref[...])
        y = _conv_ext(jnp.concatenate([prev, cur_ref[...]], axis=0), cw_ref)
        real = (t * tt + _iota2((tt, 1), 0)) >= N_PAD
        misc = misc_ref[...]
        auxv = aux_ref[...]
        for hd in range(N_HEADS):
            sl = slice(hd * D_HEAD, (hd + 1) * D_HEAD)
            q_ref[:, sl] = _l2n_act(y[:, sl], D_HEAD ** -0.5)
            k_ref[:, sl] = _l2n_act(y[:, HEADS_W + hd * D_HEAD:HEADS_W + (hd + 1) * D_HEAD], 1.0)
            v_ref[:, sl] = _silu(y[:, 2 * HEADS_W + hd * D_HEAD:2 * HEADS_W + (hd + 1) * D_HEAD])
            b_ref[:, sl], g_ref[:, sl] = _gdn_gates(misc, auxv, real, hd)
        qb_ref[...], kb_ref[...], lf_ref[...] = _hgrn_prep(bq_ref[...], bf_ref[...], lb_ref[0:1, :], real)

    rb = tt // SUBLANES
    row = lambda s, t: s * nt_ + t
    wide = BS((tt, HEADS_W), lambda s, t: (row(s, t), 0))
    return pl.pallas_call(
        body, name=f"prep_fwd_{tag}", grid=(nseq, nt_),
        in_specs=[BS((tt, qkv_w), lambda s, t: (row(s, t), 0)),
                  BS((SUBLANES, qkv_w), lambda s, t: (jnp.maximum(row(s, t) * rb - 1, 0), 0)),
                  BS((tt, LANES), lambda s, t: (row(s, t), C_MISC // LANES)),
                  BS((tt, HEADS_W), lambda s, t: (row(s, t), C_BQ // HEADS_W)),
                  BS((tt, HEADS_W), lambda s, t: (row(s, t), C_BF // HEADS_W)),
                  BS((SUBLANES, qkv_w), lambda s, t: (0, 0)), BS((SUBLANES, LANES), lambda s, t: (0, 0)),
                  BS((SUBLANES, HEADS_W), lambda s, t: (0, 0))],
        out_specs=[wide] * 8, out_shape=[SDS((n, HEADS_W), F32)] * 8, compiler_params=_params(2),
    )(proj, proj, proj, proj, proj, cw8, aux, lb8)


HGRN_SUB = GDN_CHUNK // HGRN_CHUNK


def _seq_block(nseq):
    return 2 if nseq % 2 == 0 else 1


def _to_chains(x):
    return jnp.concatenate([x[:, :, hd * D_HEAD:(hd + 1) * D_HEAD] for hd in range(N_HEADS)], axis=0)


def _from_chains(ref, rows, val):
    sb = val.shape[0] // N_HEADS
    for hd in range(N_HEADS):
        ref[:, rows, hd * D_HEAD:(hd + 1) * D_HEAD] = val[hd * sb:(hd + 1) * sb].astype(ref.dtype)


def _scan_call(body, name, arrays, col_blocks, reverse, nseq, t_len, n_sub, extra_in, outs):
    sb = _seq_block(nseq)
    nc = t_len // GDN_CHUNK
    chains = N_HEADS * sb
    cidx = (lambda c: nc - 1 - c) if reverse else (lambda c: c)
    ck_shape = (nseq // sb, nc * n_sub, chains, D_HEAD, D_HEAD)
    ck_block = (None, n_sub, chains, D_HEAD, D_HEAD) if n_sub > 1 else (None, None, chains, D_HEAD, D_HEAD)
    ck_spec = BS(ck_block, lambda p, c: (p, cidx(c), 0, 0, 0))
    in_specs = [BS((sb, GDN_CHUNK, HEADS_W), functools.partial(lambda p, c, cb: (p, cidx(c), cb), cb=cb))
                for cb in col_blocks]
    args = [a.reshape(nseq, t_len, a.shape[1]) for a in arrays]
    if extra_in is not None:
        in_specs.append(ck_spec)
        args.append(extra_in)
    out_specs, out_shape = [], []
    for o in outs:
        if o == "ckpt":
            out_specs.append(ck_spec)
            out_shape.append(SDS(ck_shape, F32))
        else:
            out_specs.append(BS((sb, GDN_CHUNK, HEADS_W), lambda p, c: (p, cidx(c), 0)))
            out_shape.append(SDS((nseq, t_len, HEADS_W), o))
    res = pl.pallas_call(
        body, name=name, grid=(nseq // sb, nc), in_specs=in_specs, out_specs=out_specs, out_shape=out_shape,
        scratch_shapes=[pltpu.VMEM((chains, D_HEAD, D_HEAD), F32)], compiler_params=_params(2),
    )(*args)
    return [r if o == "ckpt" else r.reshape(nseq * t_len, HEADS_W) for r, o in zip(res, outs)]


def _gdn_fwd(q, k, v, b, g, nseq, t_len, tag):
    def body(q_ref, k_ref, v_ref, b_ref, g_ref, o_ref, sck_ref, s_ref):
        @pl.when(pl.program_id(1) == 0)
        def _():
            s_ref[...] = jnp.zeros_like(s_ref)

        s = s_ref[...]
        sck_ref[...] = s
        o, s_new = _gdn_chunk(*[_to_chains(r[...]) for r in (q_ref, k_ref, v_ref, b_ref, g_ref)], s)
        _from_chains(o_ref, slice(None), o)
        s_ref[...] = s_new

    return _scan_call(body, f"gdn_fwd_{tag}", [q, k, v, b, g], [0] * 5, False, nseq, t_len, 1, None, [F32, "ckpt"])


def _gdn_bwd(q, k, v, b, g, sck, do, nseq, t_len, tag):
    def body(q_ref, k_ref, v_ref, b_ref, g_ref, do_ref, sck_ref, dq_ref, dk_ref, dv_ref, db_ref, dg_ref, ds_ref):
        @pl.when(pl.program_id(1) == 0)
        def _():
            ds_ref[...] = jnp.zeros_like(ds_ref)

        _, vjp = jax.vjp(_gdn_chunk, *[_to_chains(r[...]) for r in (q_ref, k_ref, v_ref, b_ref, g_ref)], sck_ref[...])
        grads = vjp((_to_chains(do_ref[...]), ds_ref[...]))
        for ref, val in zip((dq_ref, dk_ref, dv_ref, db_ref, dg_ref), grads[:5]):
            _from_chains(ref, slice(None), val)
        ds_ref[...] = grads[5]

    return _scan_call(body, f"gdn_bwd_{tag}", [q, k, v, b, g, do], [0] * 6, True, nseq, t_len, 1, sck, [F32] * 5)


def _hgrn_fwd(q, k, v, v_col, lf, nseq, t_len, tag):
    def body(q_ref, k_ref, v_ref, lf_ref, o_ref, sck_ref, s_ref):
        @pl.when(pl.program_id(1) == 0)
        def _():
            s_ref[...] = jnp.zeros_like(s_ref)

        for sub in range(HGRN_SUB):
            rs = slice(sub * HGRN_CHUNK, (sub + 1) * HGRN_CHUNK)
            s = s_ref[...]
            sck_ref[sub] = s
            o, s_new = _hgrn_chunk(*[_to_chains(r[:, rs, :]) for r in (q_ref, k_ref, v_ref, lf_ref)], s)
            _from_chains(o_ref, rs, o)
            s_ref[...] = s_new

    return _scan_call(body, f"hgrn_fwd_{tag}", [q, k, v, lf], [0, 0, v_col, 0], False, nseq, t_len, HGRN_SUB, None,
                      [F32, "ckpt"])


def _hgrn_bwd(q, k, v, v_col, lf, sck, do, nseq, t_len, tag):
    def body(q_ref, k_ref, v_ref, lf_ref, do_ref, sck_ref, dq_ref, dk_ref, dv_ref, dlf_ref, ds_ref):
        @pl.when(pl.program_id(1) == 0)
        def _():
            ds_ref[...] = jnp.zeros_like(ds_ref)

        for sub in reversed(range(HGRN_SUB)):
            rs = slice(sub * HGRN_CHUNK, (sub + 1) * HGRN_CHUNK)
            _, vjp = jax.vjp(_hgrn_chunk, *[_to_chains(r[:, rs, :]) for r in (q_ref, k_ref, v_ref, lf_ref)],
                             sck_ref[sub])
            grads = vjp((_to_chains(do_ref[:, rs, :]), ds_ref[...]))
            for ref, val in zip((dq_ref, dk_ref, dv_ref, dlf_ref), grads[:4]):
                _from_chains(ref, rs, val)
            ds_ref[...] = grads[4]

    return _scan_call(body, f"hgrn_bwd_{tag}", [q, k, v, lf, do], [0, 0, v_col, 0, 0], True, nseq, t_len, HGRN_SUB, sck,
                      [F32, F32, BF16, F32])


def _post_values(oa_ref, ob_ref, z_ref, bg_ref, ga_ref, gb_ref, gn_ref, wa_ref, wb_ref, ya_ref, yb_ref):
    for hd in range(N_HEADS):
        sl = slice(hd * D_HEAD, (hd + 1) * D_HEAD)
        ya_ref[:, sl] = _gated_norm(oa_ref[:, sl], z_ref[:, sl], gn_ref[0:1, :]).astype(BF16)
        yb_ref[:, sl] = _gated_norm(ob_ref[:, sl], bg_ref[:, sl], gn_ref[1:2, :]).astype(BF16)
    pa = jnp.dot(ya_ref[...], wa_ref[...], preferred_element_type=F32)
    pb = jnp.dot(yb_ref[...], wb_ref[...], preferred_element_type=F32)
    return pa, pb, _sigmoid(ga_ref[...]), _sigmoid(gb_ref[...])


def _post_specs(tm):
    r2 = lambda i: (i, 0)
    return [BS((tm, HEADS_W), r2), BS((tm, HEADS_W), r2),
            BS((tm, HEADS_W), lambda i: (i, C_Z // HEADS_W)), BS((tm, HEADS_W), lambda i: (i, C_BG // HEADS_W)),
            BS((tm, D_MODEL), lambda i: (i, C_GA // D_MODEL)), BS((tm, D_MODEL), lambda i: (i, C_GB // D_MODEL)),
            BS((tm, D_MODEL), r2), BS((SUBLANES, LANES), lambda i: (0, 0))]


def _post_fwd(oa, ob, proj, h, gn8, wa, wb, wout, tag):
    n = h.shape[0]
    tm = _pick(n, (256, 192, 128, 64))

    def body(oa_ref, ob_ref, z_ref, bg_ref, ga_ref, gb_ref, h_ref, gn_ref, wa_ref, wb_ref, wout_ref, out_ref,
             ya_ref, yb_ref):
        pa, pb, sa, sb = _post_values(oa_ref, ob_ref, z_ref, bg_ref, ga_ref, gb_ref, gn_ref, wa_ref, wb_ref,
                                      ya_ref, yb_ref)
        mixed = (sa * pa + sb * pb).astype(BF16)
        out_ref[...] = h_ref[...] + jnp.dot(mixed, wout_ref[...], preferred_element_type=F32)

    full = lambda i: (0, 0)
    return pl.pallas_call(
        body, name=f"post_fwd_{tag}", grid=(n // tm,),
        in_specs=_post_specs(tm) + [BS((HEADS_W, D_MODEL), full), BS((HEADS_W, D_MODEL), full),
                                    BS((D_MODEL, D_MODEL), full)],
        out_specs=BS((tm, D_MODEL), lambda i: (i, 0)), out_shape=SDS((n, D_MODEL), F32),
        scratch_shapes=[pltpu.VMEM((tm, HEADS_W), BF16), pltpu.VMEM((tm, HEADS_W), BF16)], compiler_params=_params(1),
    )(oa, ob, proj, proj, proj, proj, h, gn8, wa, wb, wout)


def _post_bwd(dh, oa, ob, proj, h, gn8, wa, wb, wa_t, wb_t, wout_t, tag):
    n = h.shape[0]
    tm = _pick(n, (256, 192, 128, 64))

    def body(dh_ref, oa_ref, ob_ref, z_ref, bg_ref, ga_ref, gb_ref, h_ref, gn_ref, wa_ref, wb_ref, wat_ref, wbt_ref,
             woutt_ref, doa_ref, dob_ref, dz_ref, dbg_ref, dga_ref, dgb_ref, dwa_ref, dwb_ref, dwout_ref, dgn_ref,
             ya_ref, yb_ref):
        @pl.when(pl.program_id(0) == 0)
        def _():
            dwa_ref[...] = jnp.zeros_like(dwa_ref)
            dwb_ref[...] = jnp.zeros_like(dwb_ref)
            dwout_ref[...] = jnp.zeros_like(dwout_ref)
            dgn_ref[...] = jnp.zeros_like(dgn_ref)

        pa, pb, sa, sb = _post_values(oa_ref, ob_ref, z_ref, bg_ref, ga_ref, gb_ref, gn_ref, wa_ref, wb_ref,
                                      ya_ref, yb_ref)
        mixed = (sa * pa + sb * pb).astype(BF16)
        dout = dh_ref[...].astype(BF16)
        dwout_ref[...] += _dg(mixed, dout, ((0,), (0,)))
        dmixed = jnp.dot(dout, woutt_ref[...], preferred_element_type=F32)
        dga_ref[...] = (dmixed * pa * sa * (1.0 - sa)).astype(BF16)
        dgb_ref[...] = (dmixed * pb * sb * (1.0 - sb)).astype(BF16)
        dpa = (dmixed * sa).astype(BF16)
        dpb = (dmixed * sb).astype(BF16)
        dwa_ref[...] += _dg(ya_ref[...], dpa, ((0,), (0,)))
        dwb_ref[...] += _dg(yb_ref[...], dpb, ((0,), (0,)))
        dya = jnp.dot(dpa, wat_ref[...], preferred_element_type=F32)
        dyb = jnp.dot(dpb, wbt_ref[...], preferred_element_type=F32)
        dgn_a = jnp.zeros((1, D_HEAD), F32)
        dgn_b = jnp.zeros((1, D_HEAD), F32)
        for hd in range(N_HEADS):
            sl = slice(hd * D_HEAD, (hd + 1) * D_HEAD)
            _, vjp = jax.vjp(_gated_norm, oa_ref[:, sl], z_ref[:, sl], gn_ref[0:1, :])
            doa, dz, dgw = vjp(dya[:, sl])
            doa_ref[:, sl], dz_ref[:, sl], dgn_a = doa, dz.astype(BF16), dgn_a + dgw
            _, vjp = jax.vjp(_gated_norm, ob_ref[:, sl], bg_ref[:, sl], gn_ref[1:2, :])
            dob, dbg, dgw = vjp(dyb[:, sl])
            dob_ref[:, sl], dbg_ref[:, sl], dgn_b = dob, dbg.astype(BF16), dgn_b + dgw
        dgn_ref[0:1, :] += dgn_a
        dgn_ref[1:2, :] += dgn_b

    full = lambda i: (0, 0)
    r2 = lambda i: (i, 0)
    return pl.pallas_call(
        body, name=f"post_bwd_{tag}", grid=(n // tm,),
        in_specs=[BS((tm, D_MODEL), r2)] + _post_specs(tm) + [
            BS((HEADS_W, D_MODEL), full), BS((HEADS_W, D_MODEL), full), BS((D_MODEL, HEADS_W), full),
            BS((D_MODEL, HEADS_W), full), BS((D_MODEL, D_MODEL), full)],
        out_specs=[BS((tm, HEADS_W), r2)] * 4 + [BS((tm, D_MODEL), r2)] * 2 + [
            BS((HEADS_W, D_MODEL), full), BS((HEADS_W, D_MODEL), full), BS((D_MODEL, D_MODEL), full),
            BS((SUBLANES, LANES), full)],
        out_shape=[SDS((n, HEADS_W), F32), SDS((n, HEADS_W), F32), SDS((n, HEADS_W), BF16), SDS((n, HEADS_W), BF16),
                   SDS((n, D_MODEL), BF16), SDS((n, D_MODEL), BF16), SDS((HEADS_W, D_MODEL), F32),
                   SDS((HEADS_W, D_MODEL), F32), SDS((D_MODEL, D_MODEL), F32), SDS((SUBLANES, LANES), F32)],
        scratch_shapes=[pltpu.VMEM((tm, HEADS_W), BF16), pltpu.VMEM((tm, HEADS_W), BF16)], compiler_params=_params(1),
    )(dh, oa, ob, proj, proj, proj, proj, h, gn8, wa, wb, wa_t, wb_t, wout_t)


def _loss_head(h, fw8, target, nseq, t_len):
    n = h.shape[0]
    nc = t_len // GDN_CHUNK
    inv_d = 1.0 / D_MODEL

    def body(h_ref, fw_ref, tgt_ref, dh_ref, acc_ref):
        @pl.when((pl.program_id(0) == 0) & (pl.program_id(1) == 0))
        def _():
            acc_ref[...] = jnp.zeros_like(acc_ref)

        frames = (pl.program_id(1) > 0).astype(F32)
        y, vjp = jax.vjp(_rms, h_ref[...], fw_ref[0:1, :])
        err = (y - tgt_ref[...]) * frames
        dx, dfw = vjp(err * inv_d)
        dh_ref[...] = dx
        acc_ref[0:1, :] += dfw
        acc_ref[1:2, :] += (0.5 * inv_d) * jnp.sum(err * err, axis=0, keepdims=True)

    return pl.pallas_call(
        body, name="loss_head", grid=(nseq, nc),
        in_specs=[BS((GDN_CHUNK, D_MODEL), lambda s, c: (s * nc + c, 0)), BS((SUBLANES, D_MODEL), lambda s, c: (0, 0)),
                  BS((None, GDN_CHUNK, D_MODEL), lambda s, c: (s, jnp.maximum(c - 1, 0), 0))],
        out_specs=[BS((GDN_CHUNK, D_MODEL), lambda s, c: (s * nc + c, 0)), BS((SUBLANES, D_MODEL), lambda s, c: (0, 0))],
        out_shape=[SDS((n, D_MODEL), F32), SDS((SUBLANES, D_MODEL), F32)], compiler_params=_params(2),
    )(h, fw8, target)


def _prep_bwd(proj, dq, dk, dv, db, dg, dqb, dkb, dlf, cw8, aux, lb8, nseq, t_len, tag):
    n = proj.shape[0]
    tt = _pick(t_len, (192, 128, 64))
    nt_ = t_len // tt
    qkv_w = 3 * HEADS_W
    rb = tt // SUBLANES
    ext = tt + SUBLANES

    def body(cur_ref, prev_ref, next_ref, misc_ref, bq_ref, bf_ref, dq_ref, dqn_ref, dk_ref, dkn_ref, dv_ref, dvn_ref,
             db_ref, dg_ref, dqb_ref, dkb_ref, dlf_ref, cw_ref, aux_ref, lb_ref,
             dqkv_ref, dmisc_ref, dbq_ref, dbf_ref, dcw_ref, daux_ref, dlb_ref, dy_ref):
        s, t = pl.program_id(0), pl.program_id(1)

        @pl.when((s == 0) & (t == 0))
        def _():
            dcw_ref[...] = jnp.zeros_like(dcw_ref)
            daux_ref[...] = jnp.zeros_like(daux_ref)
            dlb_ref[...] = jnp.zeros_like(dlb_ref)

        prev = jnp.where(t == 0, 0.0, prev_ref[...])
        x_ext = jnp.concatenate([prev, cur_ref[...], next_ref[...]], axis=0)
        y = _conv_ext(x_ext, cw_ref)
        inside = (t < nt_ - 1) | (_iota2((ext, 1), 0) < tt)
        dy_ref[0:SUBLANES, :] = jnp.zeros((SUBLANES, qkv_w), F32)
        for hd in range(N_HEADS):
            for grp, (g_ref, gn_ref, scale) in enumerate(((dq_ref, dqn_ref, D_HEAD ** -0.5), (dk_ref, dkn_ref, 1.0),
                                                          (dv_ref, dvn_ref, None))):
                lo = grp * HEADS_W + hd * D_HEAD
                sl = slice(hd * D_HEAD, (hd + 1) * D_HEAD)
                cot = jnp.concatenate([g_ref[:, sl], gn_ref[:, sl]], axis=0)
                fn = _silu if scale is None else functools.partial(_l2n_act, scale=scale)
                _, vjp = jax.vjp(fn, y[:, lo:lo + D_HEAD])
                dy_ref[SUBLANES:, lo:lo + D_HEAD] = jnp.where(inside, vjp(cot)[0], 0.0)
        dy_ext = dy_ref[...]
        dx = dy_ext * cw_ref[3:4, :]
        for kk in range(3):
            dx = dx + _shift_up(dy_ext, 3 - kk) * cw_ref[kk:kk + 1, :]
        dqkv_ref[...] = dx[SUBLANES:SUBLANES + tt].astype(BF16)
        dy_cur = dy_ext[SUBLANES:SUBLANES + tt]
        for kk in range(4):
            xs = _shift_down(x_ext, 3 - kk)[SUBLANES:SUBLANES + tt]
            dcw_ref[kk:kk + 1, :] += jnp.sum(xs * dy_cur, axis=0, keepdims=True)

        real = (t * tt + _iota2((tt, 1), 0)) >= N_PAD
        dmisc = jnp.zeros((tt, LANES), F32)
        daux = jnp.zeros((SUBLANES, LANES), F32)
        for hd in range(N_HEADS):
            sl = slice(hd * D_HEAD, (hd + 1) * D_HEAD)
            _, vjp = jax.vjp(lambda m, a: _gdn_gates(m, a, real, hd), misc_ref[...], aux_ref[...])
            dm, da = vjp((db_ref[:, sl], dg_ref[:, sl]))
            dmisc, daux = dmisc + dm, daux + da
        dmisc_ref[...] = dmisc.astype(BF16)
        daux_ref[...] += daux
        _, vjp = jax.vjp(lambda a, b, c: _hgrn_prep(a, b, c, real), bq_ref[...], bf_ref[...], lb_ref[0:1, :])
        dbq, dbf, dlb = vjp((dqb_ref[...], dkb_ref[...], dlf_ref[...]))
        dbq_ref[...], dbf_ref[...] = dbq.astype(BF16), dbf.astype(BF16)
        dlb_ref[0:1, :] += dlb

    row = lambda s, t: s * nt_ + t
    cur = lambda s, t: (row(s, t), 0)
    nxt = lambda s, t: (jnp.minimum((row(s, t) + 1) * rb, n // SUBLANES - 1), 0)
    wide = BS((tt, HEADS_W), cur)
    halo = BS((SUBLANES, HEADS_W), nxt)
    full = lambda s, t: (0, 0)
    return pl.pallas_call(
        body, name=f"prep_bwd_{tag}", grid=(nseq, nt_),
        in_specs=[BS((tt, qkv_w), cur), BS((SUBLANES, qkv_w), lambda s, t: (jnp.maximum(row(s, t) * rb - 1, 0), 0)),
                  BS((SUBLANES, qkv_w), nxt), BS((tt, LANES), lambda s, t: (row(s, t), C_MISC // LANES)),
                  BS((tt, HEADS_W), lambda s, t: (row(s, t), C_BQ // HEADS_W)),
                  BS((tt, HEADS_W), lambda s, t: (row(s, t), C_BF // HEADS_W)),
                  wide, halo, wide, halo, wide, halo, wide, wide, wide, wide, wide,
                  BS((SUBLANES, qkv_w), full), BS((SUBLANES, LANES), full), BS((SUBLANES, HEADS_W), full)],
        out_specs=[BS((tt, qkv_w), cur), BS((tt, LANES), cur), wide, wide,
                   BS((SUBLANES, qkv_w), full), BS((SUBLANES, LANES), full), BS((SUBLANES, HEADS_W), full)],
        out_shape=[SDS((n, qkv_w), BF16), SDS((n, LANES), BF16), SDS((n, HEADS_W), BF16), SDS((n, HEADS_W), BF16),
                   SDS((SUBLANES, qkv_w), F32), SDS((SUBLANES, LANES), F32), SDS((SUBLANES, HEADS_W), F32)],
        scratch_shapes=[pltpu.VMEM((tt + 2 * SUBLANES, qkv_w), F32)], compiler_params=_params(2),
    )(proj, proj, proj, proj, proj, proj, dq, dq, dk, dk, dv, dv, db, dg, dqb, dkb, dlf, cw8, aux, lb8)


def _proj_bwd_x(pieces, wp_t, h, nw8, dh_res, tag):
    n = h.shape[0]
    tm = _pick(n, (256, 192, 128, 64))
    widths = [p.shape[1] for p in pieces]
    assert sum(widths) == PROJ_W

    def body(*refs):
        p_refs = refs[:len(pieces)]
        wt_ref, h_ref, nw_ref, dres_ref, dh_ref, cat_ref, dnw_ref = refs[len(pieces):]

        @pl.when(pl.program_id(0) == 0)
        def _():
            dnw_ref[...] = jnp.zeros_like(dnw_ref)

        off = 0
        for p_ref, w in zip(p_refs, widths):
            cat_ref[:, off:off + w] = p_ref[...]
            off += w
        dxn = jnp.dot(cat_ref[...], wt_ref[...], preferred_element_type=F32)
        _, vjp = jax.vjp(_rms, h_ref[...], nw_ref[0:1, :])
        dx, dnw = vjp(dxn)
        dh_ref[...] = dres_ref[...] + dx
        dnw_ref[0:1, :] += dnw

    r2 = lambda i: (i, 0)
    full = lambda i: (0, 0)
    return pl.pallas_call(
        body, name=f"proj_bwd_x_{tag}", grid=(n // tm,),
        in_specs=[BS((tm, w), r2) for w in widths] + [BS((PROJ_W, D_MODEL), full), BS((tm, D_MODEL), r2),
                                                      BS((SUBLANES, D_MODEL), full), BS((tm, D_MODEL), r2)],
        out_specs=[BS((tm, D_MODEL), r2), BS((tm, PROJ_W), r2), BS((SUBLANES, D_MODEL), full)],
        out_shape=[SDS((n, D_MODEL), F32), SDS((n, PROJ_W), BF16), SDS((SUBLANES, D_MODEL), F32)],
        compiler_params=_params(1),
    )(*pieces, wp_t, h, nw8, dh_res)


def _proj_bwd_w(xn, dproj, tag):
    n = xn.shape[0]
    tm = _pick(n, (768, 512, 384, 256, 192, 128, 64))
    tn = 896

    def body(x_ref, d_ref, o_ref):
        @pl.when(pl.program_id(1) == 0)
        def _():
            o_ref[...] = jnp.zeros_like(o_ref)

        o_ref[...] += _dg(x_ref[...], d_ref[...], ((0,), (0,)))

    return pl.pallas_call(
        body, name=f"proj_bwd_w_{tag}", grid=(PROJ_W // tn, n // tm),
        in_specs=[BS((tm, D_MODEL), lambda j, i: (i, 0)), BS((tm, tn), lambda j, i: (i, j))],
        out_specs=BS((D_MODEL, tn), lambda j, i: (0, j)), out_shape=SDS((D_MODEL, PROJ_W), F32),
        compiler_params=_params(2),
    )(xn, dproj)


def _adamw(w, g, m, v, name):
    rows, cols = w.shape
    tr = _pick(rows, (256, 128, 64, 32, 16, 8, 4, 2, 1)) if rows > 256 else rows

    def body(w_ref, g_ref, m_ref, v_ref, d_ref, nm_ref, nv_ref):
        gr = g_ref[...]
        m_new = ADAM_B1 * m_ref[...] + (1.0 - ADAM_B1) * gr
        v_new = ADAM_B2 * v_ref[...] + (1.0 - ADAM_B2) * jnp.square(gr)
        m_hat = m_new / (1.0 - ADAM_B1 ** ADAM_STEP)
        v_hat = v_new / (1.0 - ADAM_B2 ** ADAM_STEP)
        d_ref[...] = -ADAM_LR * (m_hat / (jnp.sqrt(v_hat) + ADAM_EPS) + ADAM_WD * w_ref[...])
        nm_ref[...] = m_new
        nv_ref[...] = v_new

    blk = BS((tr, cols), lambda i: (i, 0))
    return pl.pallas_call(
        body, name=name, grid=(rows // tr,), in_specs=[blk] * 4, out_specs=[blk] * 3,
        out_shape=[SDS((rows, cols), F32)] * 3, compiler_params=_params(1),
    )(w, g, m, v)


def _row8(v, width):
    v = jnp.atleast_2d(v).astype(F32)
    return jnp.pad(v, ((0, SUBLANES - v.shape[0]), (0, width - v.shape[1])))


def _to_layout(w_full):
    return jnp.concatenate([w_full[:, 0:1536], w_full[:, 1544:6152], w_full[:, 1536:1544],
                            jnp.zeros((w_full.shape[0], PROJ_W - REF_W), w_full.dtype)], axis=1)


def _from_layout(dw):
    return jnp.concatenate([dw[:, 0:1536], dw[:, C_MISC:C_MISC + 8], dw[:, 1536:C_MISC]], axis=1)


def _lower_bounds(lb):
    sm = jax.nn.softmax(lb.astype(F32), axis=0)
    return jnp.cumsum(sm, axis=0) - sm[0]


def kernel(x, meta_tokens, norm_w, w_in, conv_w, a_log, dt_bias, gnorm_a, gnorm_b, hgrn_lower_bounds, w_branch_a, w_branch_b, w_out, final_norm_w, loss_target, m_meta_tokens, m_norm_w, m_w_in, m_conv_w, m_a_log, m_dt_bias, m_gnorm_a, m_gnorm_b, m_hgrn_lower_bounds, m_w_branch_a, m_w_branch_b, m_w_out, m_final_norm_w, v_meta_tokens, v_norm_w, v_w_in, v_conv_w, v_a_log, v_dt_bias, v_gnorm_a, v_gnorm_b, v_hgrn_lower_bounds, v_w_branch_a, v_w_branch_b, v_w_out, v_final_norm_w):
    nseq, seq, _ = x.shape
    depth = norm_w.shape[0]
    t_len = N_PAD + N_META + seq
    n = nseq * t_len
    win_c, conv_c = w_in.shape[2], conv_w.shape[2]
    my = 4 * lax.axis_index("x") + 2 * lax.axis_index("y") + lax.axis_index("c")

    g_win_, g_wa_, g_wb_, g_wout_, g_conv_, g_meta_ = _all_gather_hbm(
        [w_in.astype(BF16), w_branch_a.astype(BF16), w_branch_b.astype(BF16), w_out.astype(BF16), conv_w, meta_tokens],
        "gather_weights")
    w_in_full = g_win_.transpose(1, 2, 0, 3).reshape(depth, D_MODEL, REF_W)
    wa_full = g_wa_.transpose(1, 2, 0, 3).reshape(depth, HEADS_W, D_MODEL)
    wb_full = g_wb_.transpose(1, 2, 0, 3).reshape(depth, HEADS_W, D_MODEL)
    wout_full = g_wout_.transpose(1, 0, 2, 3).reshape(depth, D_MODEL, D_MODEL)
    conv_full = g_conv_.transpose(1, 2, 0, 3).reshape(depth, 4, 3 * HEADS_W)
    meta_full = g_meta_.transpose(1, 0, 2).reshape(N_META, D_MODEL)

    lb_all, lb_vjp = jax.vjp(_lower_bounds, hgrn_lower_bounds)

    h = jnp.concatenate([jnp.zeros((nseq, N_PAD, D_MODEL), F32),
                         jnp.broadcast_to(meta_full[None], (nseq, N_META, D_MODEL)), x], axis=1).reshape(n, D_MODEL)
    saved = []
    for l in range(depth):
        wp = _to_layout(w_in_full[l])
        nw8 = _row8(norm_w[l], D_MODEL)
        cw8 = _row8(conv_full[l], 3 * HEADS_W)
        aux = _row8(jnp.stack([a_log[l], dt_bias[l]]), LANES)
        lb8 = _row8(lb_all[l], HEADS_W)
        gn8 = _row8(jnp.stack([gnorm_a[l], gnorm_b[l]]), LANES)
        proj, xn = _proj_fwd(h, nw8, wp, l)
        q, k, v, b, g, qb, kb, lf = _prep_fwd(proj, cw8, aux, lb8, nseq, t_len, l)
        oa, sck_a = _gdn_fwd(q, k, v, b, g, nseq, t_len, l)
        ob, sck_b = _hgrn_fwd(qb, kb, proj, C_BI // HEADS_W, lf, nseq, t_len, l)
        h_next = _post_fwd(oa, ob, proj, h, gn8, wa_full[l], wb_full[l], wout_full[l], l)
        saved.append(dict(h=h, wp=wp, nw8=nw8, cw8=cw8, aux=aux, lb8=lb8, gn8=gn8, proj=proj, xn=xn, q=q, k=k, v=v, b=b,
                          g=g, qb=qb, kb=kb, lf=lf, oa=oa, ob=ob, sck_a=sck_a, sck_b=sck_b))
        h = h_next

    dh, acc = _loss_head(h, _row8(final_norm_w, D_MODEL), loss_target, nseq, t_len)

    g_win, g_wa, g_wb, g_wout, g_conv, small = [], [], [], [], [], []
    for l in reversed(range(depth)):
        s = saved[l]
        doa, dob, dz, dbg, dga, dgb, dwa, dwb, dwout, dgn = _post_bwd(
            dh, s["oa"], s["ob"], s["proj"], s["h"], s["gn8"], wa_full[l], wb_full[l], wa_full[l].T, wb_full[l].T,
            wout_full[l].T, l)
        dq, dk, dv, db, dg = _gdn_bwd(s["q"], s["k"], s["v"], s["b"], s["g"], s["sck_a"], doa, nseq, t_len, l)
        dqb, dkb, dbi, dlf = _hgrn_bwd(s["qb"], s["kb"], s["proj"], C_BI // HEADS_W, s["lf"], s["sck_b"], dob, nseq,
                                       t_len, l)
        dqkv, dmisc, dbq, dbf, dcw, daux, dlb = _prep_bwd(s["proj"], dq, dk, dv, db, dg, dqb, dkb, dlf, s["cw8"],
                                                          s["aux"], s["lb8"], nseq, t_len, l)
        dh, dproj, dnw = _proj_bwd_x([dqkv, dz, dbq, dbf, dbi, dbg, dga, dgb, dmisc], s["wp"].T, s["h"], s["nw8"], dh, l)
        g_win.append(_from_layout(_proj_bwd_w(s["xn"], dproj, l)))
        g_wa.append(dwa)
        g_wb.append(dwb)
        g_wout.append(dwout)
        g_conv.append(dcw[:4])
        small.append((dnw[0], dgn[0], dgn[1], daux[0, :N_HEADS], daux[1, :N_HEADS], dlb[0]))
    for lst in (g_win, g_wa, g_wb, g_wout, g_conv, small):
        lst.reverse()
    dh = dh.reshape(nseq, t_len, D_MODEL)
    grad_x = dh[:, N_PAD + N_META:]

    packed = jnp.concatenate([small[0][1], small[1][1], small[0][2], small[1][2], small[0][3], small[1][3],
                              small[0][4], small[1][4]])
    tile = jnp.concatenate([
        jnp.sum(dh[:, N_PAD:N_PAD + N_META], axis=0), _row8(jnp.stack([small[0][0], small[1][0], acc[0]]), D_MODEL),
        _row8(jnp.stack([small[0][5], small[1][5]]), D_MODEL), _row8(packed, D_MODEL), _row8(acc[1], D_MODEL)], axis=0)
    tile = _all_reduce_small(tile, "reduce_small")
    loss = jnp.sum(tile[40])
    g_meta = lax.dynamic_slice_in_dim(tile[0:N_META], my * LANES, LANES, axis=1)
    g_norm, g_final = tile[16:18], tile[18]
    (g_lb,) = lb_vjp(tile[24:26, :HEADS_W])
    r21 = tile[32]
    g_gna, g_gnb = r21[0:256].reshape(2, LANES), r21[256:512].reshape(2, LANES)
    g_alog, g_dtb = r21[512:520].reshape(2, N_HEADS), r21[520:528].reshape(2, N_HEADS)

    dwin, dwa_, dwb_, dwout_, dconv = (jnp.stack(a) for a in (g_win, g_wa, g_wb, g_wout, g_conv))
    by_cols = lambda a, w: a.reshape(depth * a.shape[1], N_DEV, w).transpose(1, 0, 2)
    slabs = [by_cols(dwin, win_c), jnp.concatenate([by_cols(dwa_, LANES), by_cols(dwb_, LANES)], axis=1),
             dwout_.reshape(depth, N_DEV, LANES, D_MODEL).transpose(1, 0, 2, 3).reshape(N_DEV, depth * LANES, D_MODEL),
             by_cols(dconv, conv_c)]
    coords = jnp.stack([lax.axis_index("x"), lax.axis_index("y"), lax.axis_index("c")]).astype(jnp.int32)
    r_win, r_ab, r_wout, r_conv = _reduce_scatter(slabs, coords, "grads")
    half = depth * HEADS_W
    mine = [r_win, r_ab[:half], r_ab[half:], r_wout, r_conv]
    gseg = lambda i, shape: mine[i].reshape(shape)
    grads = {
        "meta_tokens": g_meta, "norm_w": g_norm, "w_in": gseg(0, w_in.shape), "conv_w": gseg(4, conv_w.shape),
        "a_log": g_alog, "dt_bias": g_dtb, "gnorm_a": g_gna, "gnorm_b": g_gnb, "hgrn_lower_bounds": g_lb,
        "w_branch_a": gseg(1, w_branch_a.shape), "w_branch_b": gseg(2, w_branch_b.shape), "w_out": gseg(3, w_out.shape),
        "final_norm_w": g_final}
    weights = {
        "meta_tokens": (meta_tokens, m_meta_tokens, v_meta_tokens), "norm_w": (norm_w, m_norm_w, v_norm_w),
        "w_in": (w_in, m_w_in, v_w_in), "conv_w": (conv_w, m_conv_w, v_conv_w), "a_log": (a_log, m_a_log, v_a_log),
        "dt_bias": (dt_bias, m_dt_bias, v_dt_bias), "gnorm_a": (gnorm_a, m_gnorm_a, v_gnorm_a),
        "gnorm_b": (gnorm_b, m_gnorm_b, v_gnorm_b),
        "hgrn_lower_bounds": (hgrn_lower_bounds, m_hgrn_lower_bounds, v_hgrn_lower_bounds),
        "w_branch_a": (w_branch_a, m_w_branch_a, v_w_branch_a), "w_branch_b": (w_branch_b, m_w_branch_b, v_w_branch_b),
        "w_out": (w_out, m_w_out, v_w_out), "final_norm_w": (final_norm_w, m_final_norm_w, v_final_norm_w)}
    names = list(weights)
    deltas, new_m, new_v = [], [], []
    for nm in names:
        w, m, v = weights[nm]
        view = (-1, w.shape[-1])
        d, m2, v2 = _adamw(w.reshape(view), grads[nm].reshape(view), m.reshape(view), v.reshape(view), f"adamw_{nm}")
        deltas.append(d.reshape(w.shape))
        new_m.append(m2.reshape(w.shape))
        new_v.append(v2.reshape(w.shape))
    return (loss, grad_x, *[grads[nm].reshape(weights[nm][0].shape) for nm in names], *deltas, *new_m, *new_v)
```

```python
import functools

import jax
import jax.numpy as jnp
import numpy as np
from jax import lax
from jax.experimental import pallas as pl
from jax.experimental.pallas import tpu as pltpu

F32 = jnp.float32
BF16 = jnp.bfloat16

D_MODEL = 1024
N_HEADS = 4
D_HEAD = 128
HEADS_W = N_HEADS * D_HEAD
N_META = 16
N_PAD = 48
GDN_CHUNK = 64
HGRN_CHUNK = 16
EPS = 1e-6
N_DEV = 8
LANES = 128
SUBLANES = 8
VMEM_LIMIT = 56 * 1024 * 1024

C_QKV, C_Z, C_BQ, C_BF, C_BI, C_BG, C_GA, C_GB, C_MISC = 0, 1536, 2048, 2560, 3072, 3584, 4096, 5120, 6144
PROJ_W = 6272
REF_W = 6152

ADAM_LR, ADAM_B1, ADAM_B2, ADAM_EPS, ADAM_WD, ADAM_STEP = 0.001, 0.9, 0.999, 1e-08, 0.01, 10

MESH = pl.DeviceIdType.MESH
SDS = jax.ShapeDtypeStruct
BS = pl.BlockSpec


def _params(n_axes):
    return pltpu.CompilerParams(dimension_semantics=("arbitrary",) * n_axes, vmem_limit_bytes=VMEM_LIMIT)


def _pick(n, cands):
    for c in cands:
        if n % c == 0:
            return c
    raise ValueError(f"no tile for {n} among {cands}")


def _iota2(shape, dim):
    return lax.broadcasted_iota(jnp.int32, shape, dim)


def _dg(a, b, dims):
    return lax.dot_general(a.astype(BF16), b.astype(BF16), (dims, ((), ())), preferred_element_type=F32)


def _bdg(a, b, ca, cb):
    return lax.dot_general(a.astype(BF16), b.astype(BF16), (((ca,), (cb,)), ((0,), (0,))), preferred_element_type=F32)


@jax.custom_vjp
def _bnn(a, b):
    return _bdg(a, b, 2, 1)


@jax.custom_vjp
def _bnt(a, b):
    return _bdg(a, b, 2, 2)


@jax.custom_vjp
def _btn(a, b):
    return _bdg(a, b, 1, 1)


_bnn.defvjp(lambda a, b: (_bnn(a, b), (a, b)), lambda r, g: (_bnt(g, r[1]), _btn(r[0], g)))
_bnt.defvjp(lambda a, b: (_bnt(a, b), (a, b)), lambda r, g: (_bnn(g, r[1]), _btn(g, r[0])))
_btn.defvjp(lambda a, b: (_btn(a, b), (a, b)), lambda r, g: (_bnt(r[1], g), _bnn(r[0], g)))


def _split2(x):
    hi = x.astype(BF16).astype(F32)
    return hi, x - hi


def _tri(bsz, n):
    return jnp.broadcast_to((_iota2((n, n), 0) >= _iota2((n, n), 1)).astype(F32), (bsz, n, n))


@jax.custom_vjp
def _cumsum_rows(x):
    tri = _tri(x.shape[0], x.shape[1])
    hi, lo = _split2(x)
    return _bdg(tri, hi, 2, 1) + _bdg(tri, lo, 2, 1)


def _cumsum_rows_bwd(_, g):
    tri = _tri(g.shape[0], g.shape[1])
    hi, lo = _split2(g)
    return (_bdg(tri, hi, 1, 1) + _bdg(tri, lo, 1, 1),)


_cumsum_rows.defvjp(lambda x: (_cumsum_rows(x), None), _cumsum_rows_bwd)


def _sigmoid(x):
    return jax.nn.sigmoid(x)


def _silu(x):
    return x * _sigmoid(x)


def _softplus(x):
    return jnp.maximum(x, 0.0) + jnp.log1p(jnp.exp(-jnp.abs(x)))


def _rms(x, w):
    return x * lax.rsqrt(jnp.mean(x * x, axis=-1, keepdims=True) + EPS) * w


def _inv_unit_lower(lm):
    n = lm.shape[1]
    a = (_iota2((n, n), 0) == _iota2((n, n), 1)).astype(F32)[None] - lm
    steps = max(1, (n - 1).bit_length()) - 1
    p = _bnn(lm, lm)
    for i in range(steps):
        if i == steps - 1:
            a = a + _bnn(a, p)
        else:
            both = _bnn(jnp.concatenate([a, p], axis=1), p)
            a, p = a + both[:, :n], both[:, n:]
    return a


def _gdn_chunk(q, k, v, b_b, g_b, s):
    n, dv = q.shape[1], v.shape[2]
    r, c = _iota2((n, n), 0), _iota2((n, n), 1)
    causal, strict, eye = (r >= c)[None], (r > c)[None], (r == c)[None]
    g_cum = _cumsum_rows(g_b)
    g_i = g_cum[:, :, :n]
    g_j = jnp.sum(jnp.where(eye, g_i, 0.0), axis=1, keepdims=True)
    decay = jnp.where(causal, jnp.exp(jnp.where(causal, g_i - g_j, 0.0)), 0.0)
    e_g = jnp.exp(g_cum)
    kb = k * b_b
    kk = _bnt(jnp.concatenate([kb, q], axis=1), k)
    a_inv = _inv_unit_lower(jnp.where(strict, kk[:, :n] * decay, 0.0))
    uw = _bnn(a_inv, jnp.concatenate([v * b_b, kb * e_g], axis=2))
    ws = _bnn(jnp.concatenate([uw[:, :, dv:], q * e_g], axis=1), s)
    v_new = uw[:, :, :dv] - ws[:, :n]
    o = ws[:, n:] + _bnn(kk[:, n:] * decay, v_new)
    g_last = g_cum[:, n - 1:n, :]
    s_new = s * jnp.exp(g_last) +_btn(k * jnp.exp(g_last - g_cum), v_new)
    return o, s_new


@functools.partial(jax.custom_vjp, nondiff_argnums=(1, 2))
def _row(x, j, n):
    return x[:, j:j + 1, :]


def _row_bwd(j, n, _, g):
    return (jnp.where(_iota2((1, n, 1), 1) == j, g, 0.0),)


_row.defvjp(lambda x, j, n: (_row(x, j, n), None), _row_bwd)


def _hgrn_chunk(q, k, v, lf, st):
    n = q.shape[1]
    b_cum = _cumsum_rows(lf)
    o = _bnt(q * jnp.exp(b_cum), st)
    half = n // 2
    parts = []
    for lo in (0, half):
        qs, bs = q[:, lo:], b_cum[:, lo:]
        rows = _iota2((1, n - lo, 1), 1) + lo
        acc = jnp.zeros_like(qs)
        for j in range(lo, n if lo else half):
            p = jnp.exp(jnp.where(rows >= j, bs - _row(b_cum, j, n), -1e30))
            acc = acc + jnp.sum(qs * _row(k, j, n) * p, axis=2, keepdims=True) * _row(v, j, n)
        parts.append(acc)
    o = o + parts[0] + jnp.concatenate([jnp.zeros_like(parts[1]), parts[1]], axis=1)
    b_last = _row(b_cum, n - 1, n)
    st_new = st * jnp.exp(b_last) + _btn(v, k * jnp.exp(b_last - b_cum))
    return o, st_new


def _hgrn_block(q, k, v, lf, st):
    n = HGRN_CHUNK
    outs = []
    for c in range(q.shape[1] // n):
        rs = slice(c * n, (c + 1) * n)
        o, st = _hgrn_chunk(q[:, rs], k[:, rs], v[:, rs], lf[:, rs], st)
        outs.append(o)
    return jnp.concatenate(outs, axis=1), st


def _l2n_act(y, scale):
    a = _silu(y)
    return a * lax.rsqrt(jnp.sum(a * a, axis=-1, keepdims=True) + EPS) * scale


def _col(x, lane):
    return jnp.sum(jnp.where(_iota2(x.shape, 1) == lane, x, 0.0), axis=1, keepdims=True)


def _elem(x, row, lane):
    m = (_iota2(x.shape, 0) == row) & (_iota2(x.shape, 1) == lane)
    return jnp.sum(jnp.sum(jnp.where(m, x, 0.0), axis=1, keepdims=True), axis=0, keepdims=True)


def _gdn_gates(misc, aux, real, head):
    beta = _sigmoid(_col(misc, head))
    g = -jnp.exp(_elem(aux, 0, head)) * _softplus(_col(misc, N_HEADS + head) + _elem(aux, 1, head))
    g = jnp.where(real, g, 0.0)
    shape = (misc.shape[0], D_HEAD)
    return jnp.broadcast_to(beta, shape), jnp.broadcast_to(g, shape)


def _hgrn_prep(bq, bf, lb, real):
    qb = _silu(bq) * (D_HEAD ** -0.5)
    log_sig = jnp.minimum(bf, 0.0) - jnp.log1p(jnp.exp(-jnp.abs(bf)))
    pos = lb > 0.0
    lbs = jnp.where(pos, lb, 0.5)
    a = jnp.log(lbs)
    b = jnp.log1p(-lbs) + log_sig
    lae = jnp.maximum(a, b) + jnp.log1p(jnp.exp(-jnp.abs(a - b)))
    lf = jnp.where(pos, lae, log_sig)
    kb = jnp.where(pos, 1.0 - lbs, 1.0) * _sigmoid(-bf)
    return qb, jnp.where(real, kb, 0.0), jnp.where(real, lf, 0.0)


def _gated_norm(o, z, gw):
    return o * lax.rsqrt(jnp.mean(o * o, axis=-1, keepdims=True) + EPS) * gw * _silu(z)


def _shift_down(x, j):
    return x if j == 0 else pltpu.roll(x, j, 0)


def _shift_up(x, j):
    return x if j == 0 else pltpu.roll(x, x.shape[0] - j, 0)


def _all_gather_hbm(blocks, name):
    na = len(blocks)

    def body(*refs):
        x_refs, out_refs = refs[:na], refs[na:2 * na]
        send_sems, recv_sems, local_sems = refs[2 * na:]
        mx, my, mc = lax.axis_index("x"), lax.axis_index("y"), lax.axis_index("c")
        me, sibling = (mx, my, mc), (mx, my, 1 - mc)
        chips = [(1 - mx, my), (mx, 1 - my), (1 - mx, 1 - my)]

        def slab(a, px, py, pc):
            return out_refs[a].at[4 * px + 2 * py + pc]

        def copy(a, k, blk, to, own=False):
            return pltpu.make_async_remote_copy(
                src_ref=x_refs[a] if own else slab(a, *blk), dst_ref=slab(a, *blk),
                send_sem=send_sems.at[7 * a + k], recv_sem=recv_sems.at[7 * a + k], device_id=to, device_id_type=MESH)

        mine = [pltpu.make_async_copy(x_refs[a], slab(a, *me), local_sems.at[a]) for a in range(na)]
        for cp in mine:
            cp.start()
        first = [copy(a, 0, me, sibling, own=True) for a in range(na)]
        first += [copy(a, 1 + j, me, (*chip, mc), own=True) for j, chip in enumerate(chips) for a in range(na)]
        for cp in first:
            cp.start()
        passed = []
        for j, chip in enumerate(chips):
            for a in range(na):
                copy(a, 1 + j, (*chip, mc), me).wait_recv()
                passed.append(copy(a, 4 + j, (*chip, mc), sibling))
                passed[-1].start()
        for a in range(na):
            copy(a, 0, sibling, me).wait_recv()
            for j, chip in enumerate(chips):
                copy(a, 4 + j, (*chip, 1 - mc), me).wait_recv()
        for cp in first + passed:
            cp.wait_send()
        for cp in mine:
            cp.wait()

    hbm = BS(memory_space=pl.ANY)
    return pl.pallas_call(
        body, name=name, out_shape=[SDS((N_DEV, *b.shape), b.dtype) for b in blocks],
        in_specs=[hbm] * na, out_specs=[hbm] * na,
        scratch_shapes=[pltpu.SemaphoreType.DMA((7 * na,)), pltpu.SemaphoreType.DMA((7 * na,)),
                        pltpu.SemaphoreType.DMA((na,))],
    )(*blocks)


def _all_reduce_small(block, name):
    r, c = block.shape

    def body(x_ref, out_ref, buf, send_sems, recv_sems):
        mx, my, mc = lax.axis_index("x"), lax.axis_index("y"), lax.axis_index("c")
        me, sibling = (mx, my, mc), (mx, my, 1 - mc)
        chips = [(1 - mx, my), (mx, 1 - my), (1 - mx, 1 - my)]

        def slab(px, py, pc):
            return buf.at[4 * px + 2 * py + pc]

        def copy(k, blk, to, src=None):
            return pltpu.make_async_remote_copy(
                src_ref=slab(*blk) if src is None else src, dst_ref=slab(*blk),
                send_sem=send_sems.at[k], recv_sem=recv_sems.at[k], device_id=to, device_id_type=MESH)

        first = [copy(0, me, sibling, src=x_ref)]
        first += [copy(1 + j, me, (*chip, mc), src=x_ref) for j, chip in enumerate(chips)]
        for cp in first:
            cp.start()
        passed = [copy(4 + j, (*chip, mc), sibling) for j, chip in enumerate(chips)]
        for j, chip in enumerate(chips):
            copy(1 + j, (*chip, mc), me).wait_recv()
            passed[j].start()
        copy(0, sibling, me).wait_recv()
        for j, chip in enumerate(chips):
            copy(4 + j, (*chip, 1 - mc), me).wait_recv()
        for cp in first + passed:
            cp.wait_send()
        buf[4 * mx + 2 * my + mc] = x_ref[...]
        acc = buf[0]
        for d in range(1, N_DEV):
            acc = acc + buf[d]
        out_ref[...] = acc

    return pl.pallas_call(
        body, name=name, out_shape=SDS((r, c), F32),
        in_specs=[BS(memory_space=pltpu.VMEM)], out_specs=BS(memory_space=pltpu.VMEM),
        scratch_shapes=[pltpu.VMEM((N_DEV, r, c), F32), pltpu.SemaphoreType.DMA((7,)), pltpu.SemaphoreType.DMA((7,))],
    )(block)


def _exchange(bufs, flip, paired, name):
    na, n = len(bufs), bufs[0].shape[0]
    axis = ("x", "y", "c")[flip]

    def body(*refs):
        g_refs, out_refs = refs[:na], refs[na:2 * na]
        send_sems, recv_sems = refs[2 * na:]
        pos = [lax.axis_index("x"), lax.axis_index("y"), lax.axis_index("c")]
        pos[flip] = 1 - pos[flip]
        other = 1 - lax.axis_index(axis)
        copies = [pltpu.make_async_remote_copy(
            src_ref=g_refs[a].at[i, other] if paired else g_refs[a].at[i], dst_ref=out_refs[a].at[i],
            send_sem=send_sems.at[n * a + i], recv_sem=recv_sems.at[n * a + i], device_id=tuple(pos),
            device_id_type=MESH) for i in range(n) for a in range(na)]
        for cp in copies:
            cp.start()
        for cp in copies:
            cp.wait_recv()
        for cp in copies:
            cp.wait_send()

    hbm = BS(memory_space=pl.ANY)
    return pl.pallas_call(
        body, name=name, out_shape=[SDS((n, *b.shape[(2 if paired else 1):]), b.dtype) for b in bufs],
        in_specs=[hbm] * na, out_specs=[hbm] * na,
        scratch_shapes=[pltpu.SemaphoreType.DMA((n * na,)), pltpu.SemaphoreType.DMA((n * na,))],
    )(*bufs)


def _rs_tile(r):
    return _pick(r, (704, 512, 352, 256, 192, 128, 64, 32, 16, 8))


def _rs_add_c(g4, recv, coords, name):
    _, _, r, c = g4.shape
    tr = _rs_tile(r)

    def body(co_ref, a0_ref, a1_ref, b0_ref, b1_ref, keep_ref, send_ref):
        s0 = a0_ref[...] + b0_ref[...]
        s1 = a1_ref[...] + b1_ref[...]
        mine = co_ref[1] == 0
        keep_ref[...] = jnp.where(mine, s0, s1)
        send_ref[...] = jnp.where(mine, s1, s0).astype(BF16)

    blk = lambda yy: BS((None, None, tr, c), functools.partial(lambda i, j, co, yy: (2 * i + yy, co[2], j, 0), yy=yy))
    rblk = lambda yy: BS((None, tr, c), functools.partial(lambda i, j, co, yy: (2 * i + yy, j, 0), yy=yy))
    out = BS((None, tr, c), lambda i, j, co: (i, j, 0))
    return pl.pallas_call(
        body, name=name, out_shape=[SDS((2, r, c), F32), SDS((2, r, c), BF16)],
        grid_spec=pltpu.PrefetchScalarGridSpec(num_scalar_prefetch=1, grid=(2, r // tr),
                                               in_specs=[blk(0), blk(1), rblk(0), rblk(1)], out_specs=[out, out]),
        compiler_params=_params(2),
    )(coords, g4, g4, recv, recv)


def _rs_add_y(kept, recv, coords, name):
    _, r, c = kept.shape
    tr = _rs_tile(r)

    def body(co_ref, a_ref, b_ref, keep_ref, send_ref):
        s0 = a_ref[0] + b_ref[0].astype(F32)
        s1 = a_ref[1] + b_ref[1].astype(F32)
        mine = co_ref[0] == 0
        keep_ref[...] = jnp.where(mine, s0, s1)
        send_ref[0] = jnp.where(mine, s1, s0).astype(BF16)

    blk = BS((2, tr, c), lambda j, co: (0, j, 0))
    return pl.pallas_call(
        body, name=name, out_shape=[SDS((r, c), F32), SDS((1, r, c), BF16)],
        grid_spec=pltpu.PrefetchScalarGridSpec(num_scalar_prefetch=1, grid=(r // tr,), in_specs=[blk, blk],
                                               out_specs=[BS((tr, c), lambda j, co: (j, 0)),
                                                          BS((1, tr, c), lambda j, co: (0, j, 0))]),
        compiler_params=_params(1),
    )(coords, kept, recv)


def _rs_add_x(kept, recv, name):
    r, c = kept.shape
    tr = _rs_tile(r)

    def body(a_ref, b_ref, o_ref):
        o_ref[...] = a_ref[...] + b_ref[0].astype(F32)

    return pl.pallas_call(
        body, name=name, grid=(r // tr,), out_shape=SDS((r, c), F32),
        in_specs=[BS((tr, c), lambda j: (j, 0)), BS((1, tr, c), lambda j: (0, j, 0))],
        out_specs=BS((tr, c), lambda j: (j, 0)), compiler_params=_params(1),
    )(kept, recv)


def _reduce_scatter(arrays, coords, tag):
    ids = range(len(arrays))
    g4 = [a.reshape(4, 2, *a.shape[1:]) for a in arrays]
    got = _exchange(g4, 2, True, f"rs_c_{tag}")
    kept, send = zip(*[_rs_add_c(g4[i], got[i], coords, f"rs_c_add_{tag}{i}") for i in ids])
    got = _exchange(list(send), 1, False, f"rs_y_{tag}")
    kept, send = zip(*[_rs_add_y(kept[i], got[i], coords, f"rs_y_add_{tag}{i}") for i in ids])
    got = _exchange(list(send), 0, False, f"rs_x_{tag}")
    return [_rs_add_x(kept[i], got[i], f"rs_x_add_{tag}{i}") for i in ids]


def _proj_fwd(h, nw8, wp, tag):
    n = h.shape[0]
    tm = _pick(n, (768, 512, 384, 256, 192, 128, 64))
    tn = 896

    def body(h_ref, nw_ref, w_ref, proj_ref, xn_ref):
        @pl.when(pl.program_id(1) == 0)
        def _():
            xn_ref[...] = _rms(h_ref[...], nw_ref[0:1, :]).astype(BF16)

        proj_ref[...] = jnp.dot(xn_ref[...], w_ref[...], preferred_element_type=F32)

    return pl.pallas_call(
        body, name=f"proj_fwd_{tag}", grid=(n // tm, PROJ_W // tn),
        in_specs=[BS((tm, D_MODEL), lambda i, j: (i, 0)), BS((SUBLANES, D_MODEL), lambda i, j: (0, 0)),
                  BS((D_MODEL, tn), lambda i, j: (0, j))],
        out_specs=[BS((tm, tn), lambda i, j: (i, j)), BS((tm, D_MODEL), lambda i, j: (i, 0))],
        out_shape=[SDS((n, PROJ_W), F32), SDS((n, D_MODEL), BF16)], compiler_params=_params(2),
    )(h, nw8, wp)


def _conv_ext(x_ext, cw_ref):
    y = x_ext * cw_ref[3:4, :]
    for k in range(3):
        y = y + _shift_down(x_ext, 3 - k) * cw_ref[k:k + 1, :]
    return y[SUBLANES:]


def _prep_fwd(proj, cw8, aux, lb8, nseq, t_len, tag):
    n = proj.shape[0]
    tt = _pick(t_len, (192, 128, 64))
    nt_ = t_len // tt
    qkv_w = 3 * HEADS_W

    def body(cur_ref, prev_ref, misc_ref, bq_ref, bf_ref, cw_ref, aux_ref, lb_ref,
             q_ref, k_ref, v_ref, b_ref, g_ref, qb_ref, kb_ref, lf_ref):
        t = pl.program_id(1)
        prev = jnp.where(t == 0, 0.0, prev_ref[...])
        y = _conv_ext(jnp.concatenate([prev, cur_ref[...]], axis=0), cw_ref)
        real = (t * tt + _iota2((tt, 1), 0)) >= N_PAD
        misc = misc_ref[...]
        auxv = aux_ref[...]
        for hd in range(N_HEADS):
            sl = slice(hd * D_HEAD, (hd + 1) * D_HEAD)
            q_ref[:, sl] = _l2n_act(y[:, sl], D_HEAD ** -0.5)
            k_ref[:, sl] = _l2n_act(y[:, HEADS_W + hd * D_HEAD:HEADS_W + (hd + 1) * D_HEAD], 1.0)
            v_ref[:, sl] = _silu(y[:, 2 * HEADS_W + hd * D_HEAD:2 * HEADS_W + (hd + 1) * D_HEAD])
            b_ref[:, sl], g_ref[:, sl] = _gdn_gates(misc, auxv, real, hd)
        qb_ref[...], kb_ref[...], lf_ref[...] = _hgrn_prep(bq_ref[...], bf_ref[...], lb_ref[0:1, :], real)

    rb = tt // SUBLANES
    row = lambda s, t: s * nt_ + t
    wide = BS((tt, HEADS_W), lambda s, t: (row(s, t), 0))
    return pl.pallas_call(
        body, name=f"prep_fwd_{tag}", grid=(nseq, nt_),
        in_specs=[BS((tt, qkv_w), lambda s, t: (row(s, t), 0)),
                  BS((SUBLANES, qkv_w), lambda s, t: (jnp.maximum(row(s, t) * rb - 1, 0), 0)),
                  BS((tt, LANES), lambda s, t: (row(s, t), C_MISC // LANES)),
                  BS((tt, HEADS_W), lambda s, t: (row(s, t), C_BQ // HEADS_W)),
                  BS((tt, HEADS_W), lambda s, t: (row(s, t), C_BF // HEADS_W)),
                  BS((SUBLANES, qkv_w), lambda s, t: (0, 0)), BS((SUBLANES, LANES), lambda s, t: (0, 0)),
                  BS((SUBLANES, HEADS_W), lambda s, t: (0, 0))],
        out_specs=[wide] * 8, out_shape=[SDS((n, HEADS_W), F32)] * 8, compiler_params=_params(2),
    )(proj, proj, proj, proj, proj, cw8, aux, lb8)


GDN_SEQS = 4
HGRN_SEQS = 2


def _seq_block(nseq, most):
    return max(s for s in (1, 2, 4) if s <= most and nseq % s == 0)


def _to_chains(x):
    return jnp.concatenate([x[:, :, hd * D_HEAD:(hd + 1) * D_HEAD] for hd in range(N_HEADS)], axis=0)


def _from_chains(ref, rows, val):
    sb = val.shape[0] // N_HEADS
    for hd in range(N_HEADS):
        ref[:, rows, hd * D_HEAD:(hd + 1) * D_HEAD] = val[hd * sb:(hd + 1) * sb].astype(ref.dtype)


def _scan_call(body, name, arrays, col_blocks, reverse, nseq, t_len, most_seqs, extra_in, outs):
    sb = _seq_block(nseq, most_seqs)
    nc = t_len // GDN_CHUNK
    chains = N_HEADS * sb
    cidx = (lambda c: nc - 1 - c) if reverse else (lambda c: c)
    ck_shape = (nseq // sb, nc, chains, D_HEAD, D_HEAD)
    ck_spec = BS((None, None, chains, D_HEAD, D_HEAD), lambda p, c: (p, cidx(c), 0, 0, 0))
    in_specs = [BS((sb, GDN_CHUNK, HEADS_W), functools.partial(lambda p, c, cb: (p, cidx(c), cb), cb=cb))
                for cb in col_blocks]
    args = [a.reshape(nseq, t_len, a.shape[1]) for a in arrays]
    if extra_in is not None:
        in_specs.append(ck_spec)
        args.append(extra_in)
    out_specs, out_shape = [], []
    for o in outs:
        if o == "ckpt":
            out_specs.append(ck_spec)
            out_shape.append(SDS(ck_shape, F32))
        else:
            out_specs.append(BS((sb, GDN_CHUNK, HEADS_W), lambda p, c: (p, cidx(c), 0)))
            out_shape.append(SDS((nseq, t_len, HEADS_W), o))
    res = pl.pallas_call(
        body, name=name, grid=(nseq // sb, nc), in_specs=in_specs, out_specs=out_specs, out_shape=out_shape,
        scratch_shapes=[pltpu.VMEM((chains, D_HEAD, D_HEAD), F32)], compiler_params=_params(2),
    )(*args)
    return [r if o == "ckpt" else r.reshape(nseq * t_len, HEADS_W) for r, o in zip(res, outs)]


def _gdn_fwd(q, k, v, b, g, nseq, t_len, tag):
    def body(q_ref, k_ref, v_ref, b_ref, g_ref, o_ref, sck_ref, s_ref):
        @pl.when(pl.program_id(1) == 0)
        def _():
            s_ref[...] = jnp.zeros_like(s_ref)

        s = s_ref[...]
        sck_ref[...] = s
        o, s_new = _gdn_chunk(*[_to_chains(r[...]) for r in (q_ref, k_ref, v_ref, b_ref, g_ref)], s)
        _from_chains(o_ref, slice(None), o)
        s_ref[...] = s_new

    return _scan_call(body, f"gdn_fwd_{tag}", [q, k, v, b, g], [0] * 5, False, nseq, t_len, GDN_SEQS, None,
                      [F32, "ckpt"])


def _gdn_bwd(q, k, v, b, g, sck, do, nseq, t_len, tag):
    def body(q_ref, k_ref, v_ref, b_ref, g_ref, do_ref, sck_ref, dq_ref, dk_ref, dv_ref, db_ref, dg_ref, ds_ref):
        @pl.when(pl.program_id(1) == 0)
        def _():
            ds_ref[...] = jnp.zeros_like(ds_ref)

        _, vjp = jax.vjp(_gdn_chunk, *[_to_chains(r[...]) for r in (q_ref, k_ref, v_ref, b_ref, g_ref)], sck_ref[...])
        grads = vjp((_to_chains(do_ref[...]), ds_ref[...]))
        for ref, val in zip((dq_ref, dk_ref, dv_ref, db_ref, dg_ref), grads[:5]):
            _from_chains(ref, slice(None), val)
        ds_ref[...] = grads[5]

    return _scan_call(body, f"gdn_bwd_{tag}", [q, k, v, b, g, do], [0] * 6, True, nseq, t_len, GDN_SEQS, sck,
                      [F32] * 5)


def _hgrn_fwd(q, k, v, v_col, lf, nseq, t_len, tag):
    def body(q_ref, k_ref, v_ref, lf_ref, o_ref, sck_ref, s_ref):
        @pl.when(pl.program_id(1) == 0)
        def _():
            s_ref[...] = jnp.zeros_like(s_ref)

        s = s_ref[...]
        sck_ref[...] = s
        o, s_new = _hgrn_block(*[_to_chains(r[...]) for r in (q_ref, k_ref, v_ref, lf_ref)], s)
        _from_chains(o_ref, slice(None), o)
        s_ref[...] = s_new

    return _scan_call(body, f"hgrn_fwd_{tag}", [q, k, v, lf], [0, 0, v_col, 0], False, nseq, t_len, HGRN_SEQS, None,
                      [F32, "ckpt"])


def _hgrn_bwd(q, k, v, v_col, lf, sck, do, nseq, t_len, tag):
    def body(q_ref, k_ref, v_ref, lf_ref, do_ref, sck_ref, dq_ref, dk_ref, dv_ref, dlf_ref, ds_ref):
        @pl.when(pl.program_id(1) == 0)
        def _():
            ds_ref[...] = jnp.zeros_like(ds_ref)

        _, vjp = jax.vjp(_hgrn_block, *[_to_chains(r[...]) for r in (q_ref, k_ref, v_ref, lf_ref)], sck_ref[...])
        grads = vjp((_to_chains(do_ref[...]), ds_ref[...]))
        for ref, val in zip((dq_ref, dk_ref, dv_ref, dlf_ref), grads[:4]):
            _from_chains(ref, slice(None), val)
        ds_ref[...] = grads[4]

    return _scan_call(body, f"hgrn_bwd_{tag}", [q, k, v, lf, do], [0, 0, v_col, 0, 0], True, nseq, t_len, HGRN_SEQS, sck,
                      [F32, F32, BF16, F32])


def _post_values(oa_ref, ob_ref, z_ref, bg_ref, ga_ref, gb_ref, gn_ref, wa_ref, wb_ref, ya_ref, yb_ref):
    for hd in range(N_HEADS):
        sl = slice(hd * D_HEAD, (hd + 1) * D_HEAD)
        ya_ref[:, sl] = _gated_norm(oa_ref[:, sl], z_ref[:, sl], gn_ref[0:1, :]).astype(BF16)
        yb_ref[:, sl] = _gated_norm(ob_ref[:, sl], bg_ref[:, sl], gn_ref[1:2, :]).astype(BF16)
    pa = jnp.dot(ya_ref[...], wa_ref[...], preferred_element_type=F32)
    pb = jnp.dot(yb_ref[...], wb_ref[...], preferred_element_type=F32)
    return pa, pb, _sigmoid(ga_ref[...]), _sigmoid(gb_ref[...])


def _post_specs(tm):
    r2 = lambda i: (i, 0)
    return [BS((tm, HEADS_W), r2), BS((tm, HEADS_W), r2),
            BS((tm, HEADS_W), lambda i: (i, C_Z // HEADS_W)), BS((tm, HEADS_W), lambda i: (i, C_BG // HEADS_W)),
            BS((tm, D_MODEL), lambda i: (i, C_GA // D_MODEL)), BS((tm, D_MODEL), lambda i: (i, C_GB // D_MODEL)),
            BS((tm, D_MODEL), r2), BS((SUBLANES, LANES), lambda i: (0, 0))]


def _post_fwd(oa, ob, proj, h, gn8, wa, wb, wout, tag):
    n = h.shape[0]
    tm = _pick(n, (256, 192, 128, 64))

    def body(oa_ref, ob_ref, z_ref, bg_ref, ga_ref, gb_ref, h_ref, gn_ref, wa_ref, wb_ref, wout_ref, out_ref,
             ya_ref, yb_ref):
        pa, pb, sa, sb = _post_values(oa_ref, ob_ref, z_ref, bg_ref, ga_ref, gb_ref, gn_ref, wa_ref, wb_ref,
                                      ya_ref, yb_ref)
        mixed = (sa * pa + sb * pb).astype(BF16)
        out_ref[...] = h_ref[...] + jnp.dot(mixed, wout_ref[...], preferred_element_type=F32)

    full = lambda i: (0, 0)
    return pl.pallas_call(
        body, name=f"post_fwd_{tag}", grid=(n // tm,),
        in_specs=_post_specs(tm) + [BS((HEADS_W, D_MODEL), full), BS((HEADS_W, D_MODEL), full),
                                    BS((D_MODEL, D_MODEL), full)],
        out_specs=BS((tm, D_MODEL), lambda i: (i, 0)), out_shape=SDS((n, D_MODEL), F32),
        scratch_shapes=[pltpu.VMEM((tm, HEADS_W), BF16), pltpu.VMEM((tm, HEADS_W), BF16)], compiler_params=_params(1),
    )(oa, ob, proj, proj, proj, proj, h, gn8, wa, wb, wout)


def _post_bwd(dh, oa, ob, proj, h, gn8, wa, wb, wa_t, wb_t, wout_t, tag):
    n = h.shape[0]
    tm = _pick(n, (256, 192, 128, 64))

    def body(dh_ref, oa_ref, ob_ref, z_ref, bg_ref, ga_ref, gb_ref, h_ref, gn_ref, wa_ref, wb_ref, wat_ref, wbt_ref,
             woutt_ref, doa_ref, dob_ref, dz_ref, dbg_ref, dga_ref, dgb_ref, dwa_ref, dwb_ref, dwout_ref, dgn_ref,
             ya_ref, yb_ref):
        @pl.when(pl.program_id(0) == 0)
        def _():
            dwa_ref[...] = jnp.zeros_like(dwa_ref)
            dwb_ref[...] = jnp.zeros_like(dwb_ref)
            dwout_ref[...] = jnp.zeros_like(dwout_ref)
            dgn_ref[...] = jnp.zeros_like(dgn_ref)

        pa, pb, sa, sb = _post_values(oa_ref, ob_ref, z_ref, bg_ref, ga_ref, gb_ref, gn_ref, wa_ref, wb_ref,
                                      ya_ref, yb_ref)
        mixed = (sa * pa + sb * pb).astype(BF16)
        dout = dh_ref[...].astype(BF16)
        dwout_ref[...] += _dg(mixed, dout, ((0,), (0,)))
        dmixed = jnp.dot(dout, woutt_ref[...], preferred_element_type=F32)
        dga_ref[...] = (dmixed * pa * sa * (1.0 - sa)).astype(BF16)
        dgb_ref[...] = (dmixed * pb * sb * (1.0 - sb)).astype(BF16)
        dpa = (dmixed * sa).astype(BF16)
        dpb = (dmixed * sb).astype(BF16)
        dwa_ref[...] += _dg(ya_ref[...], dpa, ((0,), (0,)))
        dwb_ref[...] += _dg(yb_ref[...], dpb, ((0,), (0,)))
        dya = jnp.dot(dpa, wat_ref[...], preferred_element_type=F32)
        dyb = jnp.dot(dpb, wbt_ref[...], preferred_element_type=F32)
        dgn_a = jnp.zeros((1, D_HEAD), F32)
        dgn_b = jnp.zeros((1, D_HEAD), F32)
        for hd in range(N_HEADS):
            sl = slice(hd * D_HEAD, (hd + 1) * D_HEAD)
            _, vjp = jax.vjp(_gated_norm, oa_ref[:, sl], z_ref[:, sl], gn_ref[0:1, :])
            doa, dz, dgw = vjp(dya[:, sl])
            doa_ref[:, sl], dz_ref[:, sl], dgn_a = doa, dz.astype(BF16), dgn_a + dgw
            _, vjp = jax.vjp(_gated_norm, ob_ref[:, sl], bg_ref[:, sl], gn_ref[1:2, :])
            dob, dbg, dgw = vjp(dyb[:, sl])
            dob_ref[:, sl], dbg_ref[:, sl], dgn_b = dob, dbg.astype(BF16), dgn_b + dgw
        dgn_ref[0:1, :] += dgn_a
        dgn_ref[1:2, :] += dgn_b

    full = lambda i: (0, 0)
    r2 = lambda i: (i, 0)
    return pl.pallas_call(
        body, name=f"post_bwd_{tag}", grid=(n // tm,),
        in_specs=[BS((tm, D_MODEL), r2)] + _post_specs(tm) + [
            BS((HEADS_W, D_MODEL), full), BS((HEADS_W, D_MODEL), full), BS((D_MODEL, HEADS_W), full),
            BS((D_MODEL, HEADS_W), full), BS((D_MODEL, D_MODEL), full)],
        out_specs=[BS((tm, HEADS_W), r2)] * 4 + [BS((tm, D_MODEL), r2)] * 2 + [
            BS((HEADS_W, D_MODEL), full), BS((HEADS_W, D_MODEL), full), BS((D_MODEL, D_MODEL), full),
            BS((SUBLANES, LANES), full)],
        out_shape=[SDS((n, HEADS_W), F32), SDS((n, HEADS_W), F32), SDS((n, HEADS_W), BF16), SDS((n, HEADS_W), BF16),
                   SDS((n, D_MODEL), BF16), SDS((n, D_MODEL), BF16), SDS((HEADS_W, D_MODEL), F32),
                   SDS((HEADS_W, D_MODEL), F32), SDS((D_MODEL, D_MODEL), F32), SDS((SUBLANES, LANES), F32)],
        scratch_shapes=[pltpu.VMEM((tm, HEADS_W), BF16), pltpu.VMEM((tm, HEADS_W), BF16)], compiler_params=_params(1),
    )(dh, oa, ob, proj, proj, proj, proj, h, gn8, wa, wb, wa_t, wb_t, wout_t)


def _loss_head(h, fw8, target, nseq, t_len):
    n = h.shape[0]
    nc = t_len // GDN_CHUNK
    inv_d = 1.0 / D_MODEL

    def body(h_ref, fw_ref, tgt_ref, dh_ref, acc_ref):
        @pl.when((pl.program_id(0) == 0) & (pl.program_id(1) == 0))
        def _():
            acc_ref[...] = jnp.zeros_like(acc_ref)

        frames = (pl.program_id(1) > 0).astype(F32)
        y, vjp = jax.vjp(_rms, h_ref[...], fw_ref[0:1, :])
        err = (y - tgt_ref[...]) * frames
        dx, dfw = vjp(err * inv_d)
        dh_ref[...] = dx
        acc_ref[0:1, :] += dfw
        acc_ref[1:2, :] += (0.5 * inv_d) * jnp.sum(err * err, axis=0, keepdims=True)

    return pl.pallas_call(
        body, name="loss_head", grid=(nseq, nc),
        in_specs=[BS((GDN_CHUNK, D_MODEL), lambda s, c: (s * nc + c, 0)), BS((SUBLANES, D_MODEL), lambda s, c: (0, 0)),
                  BS((None, GDN_CHUNK, D_MODEL), lambda s, c: (s, jnp.maximum(c - 1, 0), 0))],
        out_specs=[BS((GDN_CHUNK, D_MODEL), lambda s, c: (s * nc + c, 0)), BS((SUBLANES, D_MODEL), lambda s, c: (0, 0))],
        out_shape=[SDS((n, D_MODEL), F32), SDS((SUBLANES, D_MODEL), F32)], compiler_params=_params(2),
    )(h, fw8, target)


def _prep_bwd(proj, dq, dk, dv, db, dg, dqb, dkb, dlf, cw8, aux, lb8, nseq, t_len, tag):
    n = proj.shape[0]
    tt = _pick(t_len, (192, 128, 64))
    nt_ = t_len // tt
    qkv_w = 3 * HEADS_W
    rb = tt // SUBLANES
    ext = tt + SUBLANES

    def body(cur_ref, prev_ref, next_ref, misc_ref, bq_ref, bf_ref, dq_ref, dqn_ref, dk_ref, dkn_ref, dv_ref, dvn_ref,
             db_ref, dg_ref, dqb_ref, dkb_ref, dlf_ref, cw_ref, aux_ref, lb_ref,
             dqkv_ref, dmisc_ref, dbq_ref, dbf_ref, dcw_ref, daux_ref, dlb_ref, dy_ref):
        s, t = pl.program_id(0), pl.program_id(1)

        @pl.when((s == 0) & (t == 0))
        def _():
            dcw_ref[...] = jnp.zeros_like(dcw_ref)
            daux_ref[...] = jnp.zeros_like(daux_ref)
            dlb_ref[...] = jnp.zeros_like(dlb_ref)

        prev = jnp.where(t == 0, 0.0, prev_ref[...])
        x_ext = jnp.concatenate([prev, cur_ref[...], next_ref[...]], axis=0)
        y = _conv_ext(x_ext, cw_ref)
        inside = (t < nt_ - 1) | (_iota2((ext, 1), 0) < tt)
        dy_ref[0:SUBLANES, :] = jnp.zeros((SUBLANES, qkv_w), F32)
        for hd in range(N_HEADS):
            for grp, (g_ref, gn_ref, scale) in enumerate(((dq_ref, dqn_ref, D_HEAD ** -0.5), (dk_ref, dkn_ref, 1.0),
                                                          (dv_ref, dvn_ref, None))):
                lo = grp * HEADS_W + hd * D_HEAD
                sl = slice(hd * D_HEAD, (hd + 1) * D_HEAD)
                cot = jnp.concatenate([g_ref[:, sl], gn_ref[:, sl]], axis=0)
                fn = _silu if scale is None else functools.partial(_l2n_act, scale=scale)
                _, vjp = jax.vjp(fn, y[:, lo:lo + D_HEAD])
                dy_ref[SUBLANES:, lo:lo + D_HEAD] = jnp.where(inside, vjp(cot)[0], 0.0)
        dy_ext = dy_ref[...]
        dx = dy_ext * cw_ref[3:4, :]
        for kk in range(3):
            dx = dx + _shift_up(dy_ext, 3 - kk) * cw_ref[kk:kk + 1, :]
        dqkv_ref[...] = dx[SUBLANES:SUBLANES + tt].astype(BF16)
        dy_cur = dy_ext[SUBLANES:SUBLANES + tt]
        for kk in range(4):
            xs = _shift_down(x_ext, 3 - kk)[SUBLANES:SUBLANES + tt]
            dcw_ref[kk:kk + 1, :] += jnp.sum(xs * dy_cur, axis=0, keepdims=True)

        real = (t * tt + _iota2((tt, 1), 0)) >= N_PAD
        dmisc = jnp.zeros((tt, LANES), F32)
        daux = jnp.zeros((SUBLANES, LANES), F32)
        for hd in range(N_HEADS):
            sl = slice(hd * D_HEAD, (hd + 1) * D_HEAD)
            _, vjp = jax.vjp(lambda m, a: _gdn_gates(m, a, real, hd), misc_ref[...], aux_ref[...])
            dm, da = vjp((db_ref[:, sl], dg_ref[:, sl]))
            dmisc, daux = dmisc + dm, daux + da
        dmisc_ref[...] = dmisc.astype(BF16)
        daux_ref[...] += daux
        _, vjp = jax.vjp(lambda a, b, c: _hgrn_prep(a, b, c, real), bq_ref[...], bf_ref[...], lb_ref[0:1, :])
        dbq, dbf, dlb = vjp((dqb_ref[...], dkb_ref[...], dlf_ref[...]))
        dbq_ref[...], dbf_ref[...] = dbq.astype(BF16), dbf.astype(BF16)
        dlb_ref[0:1, :] += dlb

    row = lambda s, t: s * nt_ + t
    cur = lambda s, t: (row(s, t), 0)
    nxt = lambda s, t: (jnp.minimum((row(s, t) + 1) * rb, n // SUBLANES - 1), 0)
    wide = BS((tt, HEADS_W), cur)
    halo = BS((SUBLANES, HEADS_W), nxt)
    full = lambda s, t: (0, 0)
    return pl.pallas_call(
        body, name=f"prep_bwd_{tag}", grid=(nseq, nt_),
        in_specs=[BS((tt, qkv_w), cur), BS((SUBLANES, qkv_w), lambda s, t: (jnp.maximum(row(s, t) * rb - 1, 0), 0)),
                  BS((SUBLANES, qkv_w), nxt), BS((tt, LANES), lambda s, t: (row(s, t), C_MISC // LANES)),
                  BS((tt, HEADS_W), lambda s, t: (row(s, t), C_BQ // HEADS_W)),
                  BS((tt, HEADS_W), lambda s, t: (row(s, t), C_BF // HEADS_W)),
                  wide, halo, wide, halo, wide, halo, wide, wide, wide, wide, wide,
                  BS((SUBLANES, qkv_w), full), BS((SUBLANES, LANES), full), BS((SUBLANES, HEADS_W), full)],
        out_specs=[BS((tt, qkv_w), cur), BS((tt, LANES), cur), wide, wide,
                   BS((SUBLANES, qkv_w), full), BS((SUBLANES, LANES), full), BS((SUBLANES, HEADS_W), full)],
        out_shape=[SDS((n, qkv_w), BF16), SDS((n, LANES), BF16), SDS((n, HEADS_W), BF16), SDS((n, HEADS_W), BF16),
                   SDS((SUBLANES, qkv_w), F32), SDS((SUBLANES, LANES), F32), SDS((SUBLANES, HEADS_W), F32)],
        scratch_shapes=[pltpu.VMEM((tt + 2 * SUBLANES, qkv_w), F32)], compiler_params=_params(2),
    )(proj, proj, proj, proj, proj, proj, dq, dq, dk, dk, dv, dv, db, dg, dqb, dkb, dlf, cw8, aux, lb8)


def _proj_bwd_x(pieces, wp_t, h, nw8, dh_res, tag):
    n = h.shape[0]
    tm = _pick(n, (256, 192, 128, 64))
    widths = [p.shape[1] for p in pieces]
    assert sum(widths) == PROJ_W

    def body(*refs):
        p_refs = refs[:len(pieces)]
        wt_ref, h_ref, nw_ref, dres_ref, dh_ref, cat_ref, dnw_ref = refs[len(pieces):]

        @pl.when(pl.program_id(0) == 0)
        def _():
            dnw_ref[...] = jnp.zeros_like(dnw_ref)

        off = 0
        for p_ref, w in zip(p_refs, widths):
            cat_ref[:, off:off + w] = p_ref[...]
            off += w
        dxn = jnp.dot(cat_ref[...], wt_ref[...], preferred_element_type=F32)
        _, vjp = jax.vjp(_rms, h_ref[...], nw_ref[0:1, :])
        dx, dnw = vjp(dxn)
        dh_ref[...] = dres_ref[...] + dx
        dnw_ref[0:1, :] += dnw

    r2 = lambda i: (i, 0)
    full = lambda i: (0, 0)
    return pl.pallas_call(
        body, name=f"proj_bwd_x_{tag}", grid=(n // tm,),
        in_specs=[BS((tm, w), r2) for w in widths] + [BS((PROJ_W, D_MODEL), full), BS((tm, D_MODEL), r2),
                                                      BS((SUBLANES, D_MODEL), full), BS((tm, D_MODEL), r2)],
        out_specs=[BS((tm, D_MODEL), r2), BS((tm, PROJ_W), r2), BS((SUBLANES, D_MODEL), full)],
        out_shape=[SDS((n, D_MODEL), F32), SDS((n, PROJ_W), BF16), SDS((SUBLANES, D_MODEL), F32)],
        compiler_params=_params(1),
    )(*pieces, wp_t, h, nw8, dh_res)


def _proj_bwd_w(xn, dproj, tag):
    n = xn.shape[0]
    tm = _pick(n, (768, 512, 384, 256, 192, 128, 64))
    tn = 896

    def body(x_ref, d_ref, o_ref):
        @pl.when(pl.program_id(1) == 0)
        def _():
            o_ref[...] = jnp.zeros_like(o_ref)

        o_ref[...] += _dg(x_ref[...], d_ref[...], ((0,), (0,)))

    return pl.pallas_call(
        body, name=f"proj_bwd_w_{tag}", grid=(PROJ_W // tn, n // tm),
        in_specs=[BS((tm, D_MODEL), lambda j, i: (i, 0)), BS((tm, tn), lambda j, i: (i, j))],
        out_specs=BS((D_MODEL, tn), lambda j, i: (0, j)), out_shape=SDS((D_MODEL, PROJ_W), F32),
        compiler_params=_params(2),
    )(xn, dproj)


def _adamw(w, g, m, v, name):
    rows, cols = w.shape
    tr = _pick(rows, (256, 128, 64, 32, 16, 8, 4, 2, 1)) if rows > 256 else rows

    def body(w_ref, g_ref, m_ref, v_ref, d_ref, nm_ref, nv_ref):
        gr = g_ref[...]
        m_new = ADAM_B1 * m_ref[...] + (1.0 - ADAM_B1) * gr
        v_new = ADAM_B2 * v_ref[...] + (1.0 - ADAM_B2) * jnp.square(gr)
        m_hat = m_new / (1.0 - ADAM_B1 ** ADAM_STEP)
        v_hat = v_new / (1.0 - ADAM_B2 ** ADAM_STEP)
        d_ref[...] = -ADAM_LR * (m_hat / (jnp.sqrt(v_hat) + ADAM_EPS) + ADAM_WD * w_ref[...])
        nm_ref[...] = m_new
        nv_ref[...] = v_new

    blk = BS((tr, cols), lambda i: (i, 0))
    return pl.pallas_call(
        body, name=name, grid=(rows // tr,), in_specs=[blk] * 4, out_specs=[blk] * 3,
        out_shape=[SDS((rows, cols), F32)] * 3, compiler_params=_params(1),
    )(w, g, m, v)


def _row8(v, width):
    v = jnp.atleast_2d(v).astype(F32)
    return jnp.pad(v, ((0, SUBLANES - v.shape[0]), (0, width - v.shape[1])))


REF_MISC = 1536
N_MISC = 2 * N_HEADS
LAYOUT_RUNS = ((0, REF_MISC, 0), (REF_MISC + N_MISC, REF_W, REF_MISC), (REF_MISC, REF_MISC + N_MISC, C_MISC))


def _to_layout(w_full):
    runs = [w_full[:, lo:hi] for lo, hi, _ in sorted(LAYOUT_RUNS, key=lambda run: run[2])]
    return jnp.concatenate(runs + [jnp.zeros((w_full.shape[0], PROJ_W - REF_W), w_full.dtype)], axis=1)


def _from_layout(dw, n_slabs):
    width = REF_W // n_slabs
    slabs = []
    for j in range(n_slabs):
        pieces = []
        for lo, hi, at in sorted(LAYOUT_RUNS):
            a, b = max(lo, j * width), min(hi, (j + 1) * width)
            if a < b:
                pieces.append(dw[:, at + a - lo:at + b - lo])
        slabs.append(jnp.concatenate(pieces, axis=1))
    return slabs


def _lower_bounds(lb):
    sm = jax.nn.softmax(lb.astype(F32), axis=0)
    return jnp.cumsum(sm, axis=0) - sm[0]


def kernel(x, meta_tokens, norm_w, w_in, conv_w, a_log, dt_bias, gnorm_a, gnorm_b, hgrn_lower_bounds, w_branch_a, w_branch_b, w_out, final_norm_w, loss_target, m_meta_tokens, m_norm_w, m_w_in, m_conv_w, m_a_log, m_dt_bias, m_gnorm_a, m_gnorm_b, m_hgrn_lower_bounds, m_w_branch_a, m_w_branch_b, m_w_out, m_final_norm_w, v_meta_tokens, v_norm_w, v_w_in, v_conv_w, v_a_log, v_dt_bias, v_gnorm_a, v_gnorm_b, v_hgrn_lower_bounds, v_w_branch_a, v_w_branch_b, v_w_out, v_final_norm_w):
    nseq, seq, _ = x.shape
    depth = norm_w.shape[0]
    t_len = N_PAD + N_META + seq
    n = nseq * t_len
    win_c, conv_c = w_in.shape[2], conv_w.shape[2]
    my = 4 * lax.axis_index("x") + 2 * lax.axis_index("y") + lax.axis_index("c")

    g_win_, g_wa_, g_wb_, g_wout_, g_conv_, g_meta_ = _all_gather_hbm(
        [w_in.astype(BF16), w_branch_a.astype(BF16), w_branch_b.astype(BF16), w_out.astype(BF16), conv_w, meta_tokens],
        "gather_weights")
    w_in_full = g_win_.transpose(1, 2, 0, 3).reshape(depth, D_MODEL, REF_W)
    wa_full = g_wa_.transpose(1, 2, 0, 3).reshape(depth, HEADS_W, D_MODEL)
    wb_full = g_wb_.transpose(1, 2, 0, 3).reshape(depth, HEADS_W, D_MODEL)
    wout_full = g_wout_.transpose(1, 0, 2, 3).reshape(depth, D_MODEL, D_MODEL)
    conv_full = g_conv_.transpose(1, 2, 0, 3).reshape(depth, 4, 3 * HEADS_W)
    meta_full = g_meta_.transpose(1, 0, 2).reshape(N_META, D_MODEL)

    lb_all, lb_vjp = jax.vjp(_lower_bounds, hgrn_lower_bounds)

    h = jnp.concatenate([jnp.zeros((nseq, N_PAD, D_MODEL), F32),
                         jnp.broadcast_to(meta_full[None], (nseq, N_META, D_MODEL)), x], axis=1).reshape(n, D_MODEL)
    saved = []
    for l in range(depth):
        wp = _to_layout(w_in_full[l])
        nw8 = _row8(norm_w[l], D_MODEL)
        cw8 = _row8(conv_full[l], 3 * HEADS_W)
        aux = _row8(jnp.stack([a_log[l], dt_bias[l]]), LANES)
        lb8 = _row8(lb_all[l], HEADS_W)
        gn8 = _row8(jnp.stack([gnorm_a[l], gnorm_b[l]]), LANES)
        proj, xn = _proj_fwd(h, nw8, wp, l)
        q, k, v, b, g, qb, kb, lf = _prep_fwd(proj, cw8, aux, lb8, nseq, t_len, l)
        oa, sck_a = _gdn_fwd(q, k, v, b, g, nseq, t_len, l)
        ob, sck_b = _hgrn_fwd(qb, kb, proj, C_BI // HEADS_W, lf, nseq, t_len, l)
        h_next = _post_fwd(oa, ob, proj, h, gn8, wa_full[l], wb_full[l], wout_full[l], l)
        saved.append(dict(h=h, wp=wp, nw8=nw8, cw8=cw8, aux=aux, lb8=lb8, gn8=gn8, proj=proj, xn=xn, q=q, k=k, v=v, b=b,
                          g=g, qb=qb, kb=kb, lf=lf, oa=oa, ob=ob, sck_a=sck_a, sck_b=sck_b))
        h = h_next

    dh, acc = _loss_head(h, _row8(final_norm_w, D_MODEL), loss_target, nseq, t_len)

    g_win, g_wa, g_wb, g_wout, g_conv, small = [], [], [], [], [], []
    for l in reversed(range(depth)):
        s = saved[l]
        doa, dob, dz, dbg, dga, dgb, dwa, dwb, dwout, dgn = _post_bwd(
            dh, s["oa"], s["ob"], s["proj"], s["h"], s["gn8"], wa_full[l], wb_full[l], wa_full[l].T, wb_full[l].T,
            wout_full[l].T, l)
        dq, dk, dv, db, dg = _gdn_bwd(s["q"], s["k"], s["v"], s["b"], s["g"], s["sck_a"], doa, nseq, t_len, l)
        dqb, dkb, dbi, dlf = _hgrn_bwd(s["qb"], s["kb"], s["proj"], C_BI // HEADS_W, s["lf"], s["sck_b"], dob, nseq,
                                       t_len, l)
        dqkv, dmisc, dbq, dbf, dcw, daux, dlb = _prep_bwd(s["proj"], dq, dk, dv, db, dg, dqb, dkb, dlf, s["cw8"],
                                                          s["aux"], s["lb8"], nseq, t_len, l)
        dh, dproj, dnw = _proj_bwd_x([dqkv, dz, dbq, dbf, dbi, dbg, dga, dgb, dmisc], s["wp"].T, s["h"], s["nw8"], dh, l)
        g_win.append(_from_layout(_proj_bwd_w(s["xn"], dproj, l), N_DEV))
        g_wa.append(dwa)
        g_wb.append(dwb)
        g_wout.append(dwout)
        g_conv.append(dcw[:4])
        small.append((dnw[0], dgn[0], dgn[1], daux[0, :N_HEADS], daux[1, :N_HEADS], dlb[0]))
    for lst in (g_win, g_wa, g_wb, g_wout, g_conv, small):
        lst.reverse()
    dh = dh.reshape(nseq, t_len, D_MODEL)
    grad_x = dh[:, N_PAD + N_META:]

    packed = jnp.concatenate([small[0][1], small[1][1], small[0][2], small[1][2], small[0][3], small[1][3],
                              small[0][4], small[1][4]])
    tile = jnp.concatenate([
        jnp.sum(dh[:, N_PAD:N_PAD + N_META], axis=0), _row8(jnp.stack([small[0][0], small[1][0], acc[0]]), D_MODEL),
        _row8(jnp.stack([small[0][5], small[1][5]]), D_MODEL), _row8(packed, D_MODEL), _row8(acc[1], D_MODEL)], axis=0)
    tile = _all_reduce_small(tile, "reduce_small")
    loss = jnp.sum(tile[40])
    g_meta = lax.dynamic_slice_in_dim(tile[0:N_META], my * LANES, LANES, axis=1)
    g_norm, g_final = tile[16:18], tile[18]
    (g_lb,) = lb_vjp(tile[24:26, :HEADS_W])
    r21 = tile[32]
    g_gna, g_gnb = r21[0:256].reshape(2, LANES), r21[256:512].reshape(2, LANES)
    g_alog, g_dtb = r21[512:520].reshape(2, N_HEADS), r21[520:528].reshape(2, N_HEADS)

    dwa_, dwb_, dwout_, dconv = (jnp.stack(a) for a in (g_wa, g_wb, g_wout, g_conv))
    by_cols = lambda a, w: a.reshape(depth * a.shape[1], N_DEV, w).transpose(1, 0, 2)
    dwin_slabs = jnp.stack([jnp.concatenate([g_win[l][j] for l in range(depth)], axis=0) for j in range(N_DEV)])
    slabs = [dwin_slabs, jnp.concatenate([by_cols(dwa_, LANES), by_cols(dwb_, LANES)], axis=1),
             dwout_.reshape(depth, N_DEV, LANES, D_MODEL).transpose(1, 0, 2, 3).reshape(N_DEV, depth * LANES, D_MODEL),
             by_cols(dconv, conv_c)]
    coords = jnp.stack([lax.axis_index("x"), lax.axis_index("y"), lax.axis_index("c")]).astype(jnp.int32)
    r_win, r_ab, r_wout, r_conv = _reduce_scatter(slabs, coords, "grads")
    half = depth * HEADS_W
    mine = [r_win, r_ab[:half], r_ab[half:], r_wout, r_conv]
    gseg = lambda i, shape: mine[i].reshape(shape)
    grads = {
        "meta_tokens": g_meta, "norm_w": g_norm, "w_in": gseg(0, w_in.shape), "conv_w": gseg(4, conv_w.shape),
        "a_log": g_alog, "dt_bias": g_dtb, "gnorm_a": g_gna, "gnorm_b": g_gnb, "hgrn_lower_bounds": g_lb,
        "w_branch_a": gseg(1, w_branch_a.shape), "w_branch_b": gseg(2, w_branch_b.shape), "w_out": gseg(3, w_out.shape),
        "final_norm_w": g_final}
    weights = {
        "meta_tokens": (meta_tokens, m_meta_tokens, v_meta_tokens), "norm_w": (norm_w, m_norm_w, v_norm_w),
        "w_in": (w_in, m_w_in, v_w_in), "conv_w": (conv_w, m_conv_w, v_conv_w), "a_log": (a_log, m_a_log, v_a_log),
        "dt_bias": (dt_bias, m_dt_bias, v_dt_bias), "gnorm_a": (gnorm_a, m_gnorm_a, v_gnorm_a),
        "gnorm_b": (gnorm_b, m_gnorm_b, v_gnorm_b),
        "hgrn_lower_bounds": (hgrn_lower_bounds, m_hgrn_lower_bounds, v_hgrn_lower_bounds),
        "w_branch_a": (w_branch_a, m_w_branch_a, v_w_branch_a), "w_branch_b": (w_branch_b, m_w_branch_b, v_w_branch_b),
        "w_out": (w_out, m_w_out, v_w_out), "final_norm_w": (final_norm_w, m_final_norm_w, v_final_norm_w)}
    names = list(weights)
    deltas, new_m, new_v = [], [], []
    for nm in names:
        w, m, v = weights[nm]
        view = (-1, w.shape[-1])
        d, m2, v2 = _adamw(w.reshape(view), grads[nm].reshape(view), m.reshape(view), v.reshape(view), f"adamw_{nm}")
        deltas.append(d.reshape(w.shape))
        new_m.append(m2.reshape(w.shape))
        new_v.append(v2.reshape(w.shape))
    return (loss, grad_x, *[grads[nm].reshape(weights[nm][0].shape) for nm in names], *deltas, *new_m, *new_v)
```

```python
import functools

import jax
import jax.numpy as jnp
import numpy as np
from jax import lax
from jax.experimental import pallas as pl
from jax.experimental.pallas import tpu as pltpu

F32 = jnp.float32
BF16 = jnp.bfloat16

D_MODEL = 1024
N_HEADS = 4
D_HEAD = 128
HEADS_W = N_HEADS * D_HEAD
N_META = 16
N_PAD = 48
GDN_CHUNK = 64
HGRN_CHUNK = 16
EPS = 1e-6
N_DEV = 8
LANES = 128
SUBLANES = 8
VMEM_LIMIT = 56 * 1024 * 1024

C_QKV, C_Z, C_BQ, C_BF, C_BI, C_BG, C_GA, C_GB, C_MISC = 0, 1536, 2048, 2560, 3072, 3584, 4096, 5120, 6144
PROJ_W = 6272
REF_W = 6152

ADAM_LR, ADAM_B1, ADAM_B2, ADAM_EPS, ADAM_WD, ADAM_STEP = 0.001, 0.9, 0.999, 1e-08, 0.01, 10

MESH = pl.DeviceIdType.MESH
SDS = jax.ShapeDtypeStruct
BS = pl.BlockSpec


def _params(n_axes):
    return pltpu.CompilerParams(dimension_semantics=("arbitrary",) * n_axes, vmem_limit_bytes=VMEM_LIMIT)


def _pick(n, cands):
    for c in cands:
        if n % c == 0:
            return c
    raise ValueError(f"no tile for {n} among {cands}")


def _iota2(shape, dim):
    return lax.broadcasted_iota(jnp.int32, shape, dim)


def _dg(a, b, dims):
    return lax.dot_general(a.astype(BF16), b.astype(BF16), (dims, ((), ())), preferred_element_type=F32)


def _bdg(a, b, ca, cb):
    return lax.dot_general(a.astype(BF16), b.astype(BF16), (((ca,), (cb,)), ((0,), (0,))), preferred_element_type=F32)


@jax.custom_vjp
def _bnn(a, b):
    return _bdg(a, b, 2, 1)


@jax.custom_vjp
def _bnt(a, b):
    return _bdg(a, b, 2, 2)


@jax.custom_vjp
def _btn(a, b):
    return _bdg(a, b, 1, 1)


_bnn.defvjp(lambda a, b: (_bnn(a, b), (a, b)), lambda r, g: (_bnt(g, r[1]), _btn(r[0], g)))
_bnt.defvjp(lambda a, b: (_bnt(a, b), (a, b)), lambda r, g: (_bnn(g, r[1]), _btn(g, r[0])))
_btn.defvjp(lambda a, b: (_btn(a, b), (a, b)), lambda r, g: (_bnt(r[1], g), _bnn(r[0], g)))


def _split2(x):
    hi = x.astype(BF16).astype(F32)
    return hi, x - hi


def _tri(bsz, n):
    return jnp.broadcast_to((_iota2((n, n), 0) >= _iota2((n, n), 1)).astype(F32), (bsz, n, n))


@jax.custom_vjp
def _cumsum_rows(x):
    tri = _tri(x.shape[0], x.shape[1])
    hi, lo = _split2(x)
    return _bdg(tri, hi, 2, 1) + _bdg(tri, lo, 2, 1)


def _cumsum_rows_bwd(_, g):
    tri = _tri(g.shape[0], g.shape[1])
    hi, lo = _split2(g)
    return (_bdg(tri, hi, 1, 1) + _bdg(tri, lo, 1, 1),)


_cumsum_rows.defvjp(lambda x: (_cumsum_rows(x), None), _cumsum_rows_bwd)


def _sigmoid(x):
    return jax.nn.sigmoid(x)


def _silu(x):
    return x * _sigmoid(x)


def _softplus(x):
    return jnp.maximum(x, 0.0) + jnp.log1p(jnp.exp(-jnp.abs(x)))


def _rms(x, w):
    return x * lax.rsqrt(jnp.mean(x * x, axis=-1, keepdims=True) + EPS) * w


def _inv_unit_lower(lm):
    n = lm.shape[1]
    a = (_iota2((n, n), 0) == _iota2((n, n), 1)).astype(F32)[None] - lm
    steps = max(1, (n - 1).bit_length()) - 1
    p = _bnn(lm, lm)
    for i in range(steps):
        if i == steps - 1:
            a = a + _bnn(a, p)
        else:
            both = _bnn(jnp.concatenate([a, p], axis=1), p)
            a, p = a + both[:, :n], both[:, n:]
    return a


def _gdn_chunk(q, k, v, b_b, g_b, s):
    n, dv = q.shape[1], v.shape[2]
    r, c = _iota2((n, n), 0), _iota2((n, n), 1)
    causal, strict, eye = (r >= c)[None], (r > c)[None], (r == c)[None]
    g_cum = _cumsum_rows(g_b)
    g_i = g_cum[:, :, :n]
    g_j = jnp.sum(jnp.where(eye, g_i, 0.0), axis=1, keepdims=True)
    decay = jnp.where(causal, jnp.exp(jnp.where(causal, g_i - g_j, 0.0)), 0.0)
    e_g = jnp.exp(g_cum)
    kb = k * b_b
    kk = _bnt(jnp.concatenate([kb, q], axis=1), k)
    a_inv = _inv_unit_lower(jnp.where(strict, kk[:, :n] * decay, 0.0))
    uw = _bnn(a_inv, jnp.concatenate([v * b_b, kb * e_g], axis=2))
    ws = _bnn(jnp.concatenate([uw[:, :, dv:], q * e_g], axis=1), s)
    v_new = uw[:, :, :dv] - ws[:, :n]
    o = ws[:, n:] + _bnn(kk[:, n:] * decay, v_new)
    g_last = g_cum[:, n - 1:n, :]
    s_new = s * jnp.exp(g_last) +_btn(k * jnp.exp(g_last - g_cum), v_new)
    return o, s_new


@functools.partial(jax.custom_vjp, nondiff_argnums=(1, 2))
def _row(x, j, n):
    return x[:, j:j + 1, :]


def _row_bwd(j, n, _, g):
    return (jnp.where(_iota2((1, n, 1), 1) == j, g, 0.0),)


_row.defvjp(lambda x, j, n: (_row(x, j, n), None), _row_bwd)


def _hgrn_chunk(q, k, v, lf, st):
    n = q.shape[1]
    b_cum = _cumsum_rows(lf)
    o = _bnt(q * jnp.exp(b_cum), st)
    half = n // 2
    parts = []
    for lo in (0, half):
        qs, bs = q[:, lo:], b_cum[:, lo:]
        rows = _iota2((1, n - lo, 1), 1) + lo
        acc = jnp.zeros_like(qs)
        for j in range(lo, n if lo else half):
            p = jnp.exp(jnp.where(rows >= j, bs - _row(b_cum, j, n), -1e30))
            acc = acc + jnp.sum(qs * _row(k, j, n) * p, axis=2, keepdims=True) * _row(v, j, n)
        parts.append(acc)
    o = o + parts[0] + jnp.concatenate([jnp.zeros_like(parts[1]), parts[1]], axis=1)
    b_last = _row(b_cum, n - 1, n)
    st_new = st * jnp.exp(b_last) + _btn(v, k * jnp.exp(b_last - b_cum))
    return o, st_new


def _hgrn_block(q, k, v, lf, st):
    n = HGRN_CHUNK
    outs = []
    for c in range(q.shape[1] // n):
        rs = slice(c * n, (c + 1) * n)
        o, st = _hgrn_chunk(q[:, rs], k[:, rs], v[:, rs], lf[:, rs], st)
        outs.append(o)
    return jnp.concatenate(outs, axis=1), st


def _l2n_act(y, scale):
    a = _silu(y)
    return a * lax.rsqrt(jnp.sum(a * a, axis=-1, keepdims=True) + EPS) * scale


def _col(x, lane):
    return jnp.sum(jnp.where(_iota2(x.shape, 1) == lane, x, 0.0), axis=1, keepdims=True)


def _elem(x, row, lane):
    m = (_iota2(x.shape, 0) == row) & (_iota2(x.shape, 1) == lane)
    return jnp.sum(jnp.sum(jnp.where(m, x, 0.0), axis=1, keepdims=True), axis=0, keepdims=True)


def _gdn_gates(misc, aux, real, head):
    beta = _sigmoid(_col(misc, head))
    g = -jnp.exp(_elem(aux, 0, head)) * _softplus(_col(misc, N_HEADS + head) + _elem(aux, 1, head))
    g = jnp.where(real, g, 0.0)
    shape = (misc.shape[0], D_HEAD)
    return jnp.broadcast_to(beta, shape), jnp.broadcast_to(g, shape)


def _hgrn_prep(bq, bf, lb, real):
    qb = _silu(bq) * (D_HEAD ** -0.5)
    log_sig = jnp.minimum(bf, 0.0) - jnp.log1p(jnp.exp(-jnp.abs(bf)))
    pos = lb > 0.0
    lbs = jnp.where(pos, lb, 0.5)
    a = jnp.log(lbs)
    b = jnp.log1p(-lbs) + log_sig
    lae = jnp.maximum(a, b) + jnp.log1p(jnp.exp(-jnp.abs(a - b)))
    lf = jnp.where(pos, lae, log_sig)
    kb = jnp.where(pos, 1.0 - lbs, 1.0) * _sigmoid(-bf)
    return qb, jnp.where(real, kb, 0.0), jnp.where(real, lf, 0.0)


def _gated_norm(o, z, gw):
    return o * lax.rsqrt(jnp.mean(o * o, axis=-1, keepdims=True) + EPS) * gw * _silu(z)


def _shift_down(x, j):
    return x if j == 0 else pltpu.roll(x, j, 0)


def _shift_up(x, j):
    return x if j == 0 else pltpu.roll(x, x.shape[0] - j, 0)


def _all_gather_hbm(blocks, name):
    na = len(blocks)

    def body(*refs):
        x_refs, out_refs = refs[:na], refs[na:2 * na]
        send_sems, recv_sems, local_sems = refs[2 * na:]
        mx, my, mc = lax.axis_index("x"), lax.axis_index("y"), lax.axis_index("c")
        me, sibling = (mx, my, mc), (mx, my, 1 - mc)
        chips = [(1 - mx, my), (mx, 1 - my), (1 - mx, 1 - my)]

        def slab(a, px, py, pc):
            return out_refs[a].at[4 * px + 2 * py + pc]

        def copy(a, k, blk, to, own=False):
            return pltpu.make_async_remote_copy(
                src_ref=x_refs[a] if own else slab(a, *blk), dst_ref=slab(a, *blk),
                send_sem=send_sems.at[7 * a + k], recv_sem=recv_sems.at[7 * a + k], device_id=to, device_id_type=MESH)

        mine = [pltpu.make_async_copy(x_refs[a], slab(a, *me), local_sems.at[a]) for a in range(na)]
        for cp in mine:
            cp.start()
        first = [copy(a, 0, me, sibling, own=True) for a in range(na)]
        first += [copy(a, 1 + j, me, (*chip, mc), own=True) for j, chip in enumerate(chips) for a in range(na)]
        for cp in first:
            cp.start()
        passed = []
        for j, chip in enumerate(chips):
            for a in range(na):
                copy(a, 1 + j, (*chip, mc), me).wait_recv()
                passed.append(copy(a, 4 + j, (*chip, mc), sibling))
                passed[-1].start()
        for a in range(na):
            copy(a, 0, sibling, me).wait_recv()
            for j, chip in enumerate(chips):
                copy(a, 4 + j, (*chip, 1 - mc), me).wait_recv()
        for cp in first + passed:
            cp.wait_send()
        for cp in mine:
            cp.wait()

    hbm = BS(memory_space=pl.ANY)
    return pl.pallas_call(
        body, name=name, out_shape=[SDS((N_DEV, *b.shape), b.dtype) for b in blocks],
        in_specs=[hbm] * na, out_specs=[hbm] * na,
        scratch_shapes=[pltpu.SemaphoreType.DMA((7 * na,)), pltpu.SemaphoreType.DMA((7 * na,)),
                        pltpu.SemaphoreType.DMA((na,))],
    )(*blocks)


def _all_reduce_small(block, name):
    r, c = block.shape

    def body(x_ref, out_ref, buf, send_sems, recv_sems):
        mx, my, mc = lax.axis_index("x"), lax.axis_index("y"), lax.axis_index("c")
        me, sibling = (mx, my, mc), (mx, my, 1 - mc)
        chips = [(1 - mx, my), (mx, 1 - my), (1 - mx, 1 - my)]

        def slab(px, py, pc):
            return buf.at[4 * px + 2 * py + pc]

        def copy(k, blk, to, src=None):
            return pltpu.make_async_remote_copy(
                src_ref=slab(*blk) if src is None else src, dst_ref=slab(*blk),
                send_sem=send_sems.at[k], recv_sem=recv_sems.at[k], device_id=to, device_id_type=MESH)

        first = [copy(0, me, sibling, src=x_ref)]
        first += [copy(1 + j, me, (*chip, mc), src=x_ref) for j, chip in enumerate(chips)]
        for cp in first:
            cp.start()
        passed = [copy(4 + j, (*chip, mc), sibling) for j, chip in enumerate(chips)]
        for j, chip in enumerate(chips):
            copy(1 + j, (*chip, mc), me).wait_recv()
            passed[j].start()
        copy(0, sibling, me).wait_recv()
        for j, chip in enumerate(chips):
            copy(4 + j, (*chip, 1 - mc), me).wait_recv()
        for cp in first + passed:
            cp.wait_send()
        buf[4 * mx + 2 * my + mc] = x_ref[...]
        acc = buf[0]
        for d in range(1, N_DEV):
            acc = acc + buf[d]
        out_ref[...] = acc

    return pl.pallas_call(
        body, name=name, out_shape=SDS((r, c), F32),
        in_specs=[BS(memory_space=pltpu.VMEM)], out_specs=BS(memory_space=pltpu.VMEM),
        scratch_shapes=[pltpu.VMEM((N_DEV, r, c), F32), pltpu.SemaphoreType.DMA((7,)), pltpu.SemaphoreType.DMA((7,))],
    )(block)


HBM_SPEC = BS(memory_space=pltpu.HBM)
SEM_SPEC = BS(memory_space=pltpu.SEMAPHORE)
SIDE_EFFECT = pltpu.SideEffectType.DATAFLOW_SIDE_EFFECTING


def _peer(rel):
    flip = lambda v, bit: 1 - v if bit else v
    return (flip(lax.axis_index("x"), rel >> 2 & 1), flip(lax.axis_index("y"), rel >> 1 & 1),
            flip(lax.axis_index("c"), rel & 1))


def _send_all_start(blocks, scatter, name):
    na = len(blocks)
    shapes = [b.shape[1:] if scatter else b.shape for b in blocks]

    def body(*refs):
        srcs, lands = refs[:na], refs[na:2 * na]
        send_sems, recv_sems, token = refs[2 * na], refs[2 * na + 1], refs[-1]
        me = 4 * lax.axis_index("x") + 2 * lax.axis_index("y") + lax.axis_index("c")
        for a in range(na):
            for rel in range(1, N_DEV):
                px, py, pc = _peer(rel)
                pltpu.make_async_remote_copy(
                    src_ref=srcs[a].at[4 * px + 2 * py + pc] if scatter else srcs[a], dst_ref=lands[a].at[me],
                    send_sem=send_sems.at[7 * a + rel - 1], recv_sem=recv_sems.at[7 * a + rel - 1],
                    device_id=(px, py, pc), device_id_type=MESH).start()
        token[...] = jnp.zeros_like(token)

    lands = [lax.empty((N_DEV, *s), b.dtype) for s, b in zip(shapes, blocks)]
    res = pl.pallas_call(
        body, name=name,
        out_shape=([pltpu.SemaphoreType.DMA((7 * na,)), pltpu.SemaphoreType.DMA((7 * na,))]
                   + [pltpu.HBM(b.shape, b.dtype) for b in blocks] + [pltpu.HBM(ld.shape, ld.dtype) for ld in lands]
                   + [SDS((SUBLANES, LANES), F32)]),
        in_specs=[HBM_SPEC] * (2 * na), out_specs=[SEM_SPEC, SEM_SPEC] + [HBM_SPEC] * (2 * na) + [BS(memory_space=pltpu.VMEM)],
        input_output_aliases={i: 2 + i for i in range(2 * na)},
        compiler_params=pltpu.CompilerParams(has_side_effects=SIDE_EFFECT),
    )(*[pltpu.with_memory_space_constraint(b, pltpu.HBM) for b in blocks],
      *[pltpu.with_memory_space_constraint(ld, pltpu.HBM) for ld in lands])
    return dict(send=res[0], recv=res[1], srcs=res[2:2 + na], lands=res[2 + na:2 + 2 * na], scatter=scatter), res[-1]


def _send_all_wait(flight, after, name):
    na = len(flight["srcs"])

    def body(*refs):
        srcs, lands = refs[:na], refs[na:2 * na]
        send_sems, recv_sems = refs[2 * na], refs[2 * na + 1]
        for a in range(na):
            for rel in range(1, N_DEV):
                cp = pltpu.make_async_remote_copy(
                    src_ref=srcs[a].at[0] if flight["scatter"] else srcs[a], dst_ref=lands[a].at[0],
                    send_sem=send_sems.at[7 * a + rel - 1], recv_sem=recv_sems.at[7 * a + rel - 1],
                    device_id=_peer(rel), device_id_type=MESH)
                cp.wait_send()
                cp.wait_recv()

    arrays = list(flight["srcs"]) + list(flight["lands"])
    res = pl.pallas_call(
        body, name=name, out_shape=[pltpu.HBM(a.shape, a.dtype) for a in arrays],
        in_specs=[HBM_SPEC] * (2 * na) + [SEM_SPEC, SEM_SPEC, BS(memory_space=pl.ANY)], out_specs=[HBM_SPEC] * (2 * na),
        input_output_aliases={i: i for i in range(2 * na)},
        compiler_params=pltpu.CompilerParams(has_side_effects=SIDE_EFFECT),
    )(*arrays, flight["send"], flight["recv"], after)
    return res[:na], res[na:]


def _sum_slabs(land, name):
    _, r, c = land.shape
    tr = _pick(r, (256, 128, 64, 32, 16, 8))

    def body(l_ref, o_ref):
        acc = l_ref[0].astype(F32)
        for d in range(1, N_DEV):
            acc = acc + l_ref[d].astype(F32)
        o_ref[...] = acc

    return pl.pallas_call(
        body, name=name, grid=(r // tr,), out_shape=SDS((r, c), F32),
        in_specs=[BS((N_DEV, tr, c), lambda j: (0, j, 0))], out_specs=BS((tr, c), lambda j: (j, 0)),
        compiler_params=_params(1),
    )(land)


def _exchange(bufs, flip, paired, name):
    na, n = len(bufs), bufs[0].shape[0]
    axis = ("x", "y", "c")[flip]

    def body(*refs):
        g_refs, out_refs = refs[:na], refs[na:2 * na]
        send_sems, recv_sems = refs[2 * na:]
        pos = [lax.axis_index("x"), lax.axis_index("y"), lax.axis_index("c")]
        pos[flip] = 1 - pos[flip]
        other = 1 - lax.axis_index(axis)
        copies = [pltpu.make_async_remote_copy(
            src_ref=g_refs[a].at[i, other] if paired else g_refs[a].at[i], dst_ref=out_refs[a].at[i],
            send_sem=send_sems.at[n * a + i], recv_sem=recv_sems.at[n * a + i], device_id=tuple(pos),
            device_id_type=MESH) for i in range(n) for a in range(na)]
        for cp in copies:
            cp.start()
        for cp in copies:
            cp.wait_recv()
        for cp in copies:
            cp.wait_send()

    hbm = BS(memory_space=pl.ANY)
    return pl.pallas_call(
        body, name=name, out_shape=[SDS((n, *b.shape[(2 if paired else 1):]), b.dtype) for b in bufs],
        in_specs=[hbm] * na, out_specs=[hbm] * na,
        scratch_shapes=[pltpu.SemaphoreType.DMA((n * na,)), pltpu.SemaphoreType.DMA((n * na,))],
    )(*bufs)


def _rs_tile(r):
    return _pick(r, (704, 512, 352, 256, 192, 128, 64, 32, 16, 8))


def _rs_add_c(g4, recv, coords, name):
    _, _, r, c = g4.shape
    tr = _rs_tile(r)

    def body(co_ref, a0_ref, a1_ref, b0_ref, b1_ref, keep_ref, send_ref):
        s0 = a0_ref[...] + b0_ref[...]
        s1 = a1_ref[...] + b1_ref[...]
        mine = co_ref[1] == 0
        keep_ref[...] = jnp.where(mine, s0, s1)
        send_ref[...] = jnp.where(mine, s1, s0).astype(BF16)

    blk = lambda yy: BS((None, None, tr, c), functools.partial(lambda i, j, co, yy: (2 * i + yy, co[2], j, 0), yy=yy))
    rblk = lambda yy: BS((None, tr, c), functools.partial(lambda i, j, co, yy: (2 * i + yy, j, 0), yy=yy))
    out = BS((None, tr, c), lambda i, j, co: (i, j, 0))
    return pl.pallas_call(
        body, name=name, out_shape=[SDS((2, r, c), F32), SDS((2, r, c), BF16)],
        grid_spec=pltpu.PrefetchScalarGridSpec(num_scalar_prefetch=1, grid=(2, r // tr),
                                               in_specs=[blk(0), blk(1), rblk(0), rblk(1)], out_specs=[out, out]),
        compiler_params=_params(2),
    )(coords, g4, g4, recv, recv)


def _rs_add_y(kept, recv, coords, name):
    _, r, c = kept.shape
    tr = _rs_tile(r)

    def body(co_ref, a_ref, b_ref, keep_ref, send_ref):
        s0 = a_ref[0] + b_ref[0].astype(F32)
        s1 = a_ref[1] + b_ref[1].astype(F32)
        mine = co_ref[0] == 0
        keep_ref[...] = jnp.where(mine, s0, s1)
        send_ref[0] = jnp.where(mine, s1, s0).astype(BF16)

    blk = BS((2, tr, c), lambda j, co: (0, j, 0))
    return pl.pallas_call(
        body, name=name, out_shape=[SDS((r, c), F32), SDS((1, r, c), BF16)],
        grid_spec=pltpu.PrefetchScalarGridSpec(num_scalar_prefetch=1, grid=(r // tr,), in_specs=[blk, blk],
                                               out_specs=[BS((tr, c), lambda j, co: (j, 0)),
                                                          BS((1, tr, c), lambda j, co: (0, j, 0))]),
        compiler_params=_params(1),
    )(coords, kept, recv)


def _rs_add_x(kept, recv, name):
    r, c = kept.shape
    tr = _rs_tile(r)

    def body(a_ref, b_ref, o_ref):
        o_ref[...] = a_ref[...] + b_ref[0].astype(F32)

    return pl.pallas_call(
        body, name=name, grid=(r // tr,), out_shape=SDS((r, c), F32),
        in_specs=[BS((tr, c), lambda j: (j, 0)), BS((1, tr, c), lambda j: (0, j, 0))],
        out_specs=BS((tr, c), lambda j: (j, 0)), compiler_params=_params(1),
    )(kept, recv)


def _reduce_scatter(arrays, coords, tag):
    ids = range(len(arrays))
    g4 = [a.reshape(4, 2, *a.shape[1:]) for a in arrays]
    got = _exchange(g4, 2, True, f"rs_c_{tag}")
    kept, send = zip(*[_rs_add_c(g4[i], got[i], coords, f"rs_c_add_{tag}{i}") for i in ids])
    got = _exchange(list(send), 1, False, f"rs_y_{tag}")
    kept, send = zip(*[_rs_add_y(kept[i], got[i], coords, f"rs_y_add_{tag}{i}") for i in ids])
    got = _exchange(list(send), 0, False, f"rs_x_{tag}")
    return [_rs_add_x(kept[i], got[i], f"rs_x_add_{tag}{i}") for i in ids]


def _proj_fwd(h, nw8, wp, tag):
    n = h.shape[0]
    tm = _pick(n, (768, 512, 384, 256, 192, 128, 64))
    tn = 896

    def body(h_ref, nw_ref, w_ref, proj_ref, xn_ref):
        @pl.when(pl.program_id(1) == 0)
        def _():
            xn_ref[...] = _rms(h_ref[...], nw_ref[0:1, :]).astype(BF16)

        proj_ref[...] = jnp.dot(xn_ref[...], w_ref[...], preferred_element_type=F32)

    return pl.pallas_call(
        body, name=f"proj_fwd_{tag}", grid=(n // tm, PROJ_W // tn),
        in_specs=[BS((tm, D_MODEL), lambda i, j: (i, 0)), BS((SUBLANES, D_MODEL), lambda i, j: (0, 0)),
                  BS((D_MODEL, tn), lambda i, j: (0, j))],
        out_specs=[BS((tm, tn), lambda i, j: (i, j)), BS((tm, D_MODEL), lambda i, j: (i, 0))],
        out_shape=[SDS((n, PROJ_W), F32), SDS((n, D_MODEL), BF16)], compiler_params=_params(2),
    )(h, nw8, wp)


def _conv_ext(x_ext, cw_ref):
    y = x_ext * cw_ref[3:4, :]
    for k in range(3):
        y = y + _shift_down(x_ext, 3 - k) * cw_ref[k:k + 1, :]
    return y[SUBLANES:]


def _prep_fwd(proj, cw8, aux, lb8, nseq, t_len, tag):
    n = proj.shape[0]
    tt = _pick(t_len, (192, 128, 64))
    nt_ = t_len // tt
    qkv_w = 3 * HEADS_W

    def body(cur_ref, prev_ref, misc_ref, bq_ref, bf_ref, cw_ref, aux_ref, lb_ref,
             q_ref, k_ref, v_ref, b_ref, g_ref, qb_ref, kb_ref, lf_ref):
        t = pl.program_id(1)
        prev = jnp.where(t == 0, 0.0, prev_ref[...])
        y = _conv_ext(jnp.concatenate([prev, cur_ref[...]], axis=0), cw_ref)
        real = (t * tt + _iota2((tt, 1), 0)) >= N_PAD
        misc = misc_ref[...]
        auxv = aux_ref[...]
        for hd in range(N_HEADS):
            sl = slice(hd * D_HEAD, (hd + 1) * D_HEAD)
            q_ref[:, sl] = _l2n_act(y[:, sl], D_HEAD ** -0.5)
            k_ref[:, sl] = _l2n_act(y[:, HEADS_W + hd * D_HEAD:HEADS_W + (hd + 1) * D_HEAD], 1.0)
            v_ref[:, sl] = _silu(y[:, 2 * HEADS_W + hd * D_HEAD:2 * HEADS_W + (hd + 1) * D_HEAD])
            b_ref[:, sl], g_ref[:, sl] = _gdn_gates(misc, auxv, real, hd)
        qb_ref[...], kb_ref[...], lf_ref[...] = _hgrn_prep(bq_ref[...], bf_ref[...], lb_ref[0:1, :], real)

    rb = tt // SUBLANES
    row = lambda s, t: s * nt_ + t
    wide = BS((tt, HEADS_W), lambda s, t: (row(s, t), 0))
    return pl.pallas_call(
        body, name=f"prep_fwd_{tag}", grid=(nseq, nt_),
        in_specs=[BS((tt, qkv_w), lambda s, t: (row(s, t), 0)),
                  BS((SUBLANES, qkv_w), lambda s, t: (jnp.maximum(row(s, t) * rb - 1, 0), 0)),
                  BS((tt, LANES), lambda s, t: (row(s, t), C_MISC // LANES)),
                  BS((tt, HEADS_W), lambda s, t: (row(s, t), C_BQ // HEADS_W)),
                  BS((tt, HEADS_W), lambda s, t: (row(s, t), C_BF // HEADS_W)),
                  BS((SUBLANES, qkv_w), lambda s, t: (0, 0)), BS((SUBLANES, LANES), lambda s, t: (0, 0)),
                  BS((SUBLANES, HEADS_W), lambda s, t: (0, 0))],
        out_specs=[wide] * 8, out_shape=[SDS((n, HEADS_W), F32)] * 8, compiler_params=_params(2),
    )(proj, proj, proj, proj, proj, cw8, aux, lb8)


GDN_SEQS = 4
HGRN_SEQS = 2


def _seq_block(nseq, most):
    return max(s for s in (1, 2, 4) if s <= most and nseq % s == 0)


def _to_chains(x):
    return jnp.concatenate([x[:, :, hd * D_HEAD:(hd + 1) * D_HEAD] for hd in range(N_HEADS)], axis=0)


def _from_chains(ref, rows, val):
    sb = val.shape[0] // N_HEADS
    for hd in range(N_HEADS):
        ref[:, rows, hd * D_HEAD:(hd + 1) * D_HEAD] = val[hd * sb:(hd + 1) * sb].astype(ref.dtype)


def _scan_call(body, name, arrays, col_blocks, reverse, nseq, t_len, most_seqs, extra_in, outs):
    sb = _seq_block(nseq, most_seqs)
    nc = t_len // GDN_CHUNK
    chains = N_HEADS * sb
    cidx = (lambda c: nc - 1 - c) if reverse else (lambda c: c)
    ck_shape = (nseq // sb, nc, chains, D_HEAD, D_HEAD)
    ck_spec = BS((None, None, chains, D_HEAD, D_HEAD), lambda p, c: (p, cidx(c), 0, 0, 0))
    in_specs = [BS((sb, GDN_CHUNK, HEADS_W), functools.partial(lambda p, c, cb: (p, cidx(c), cb), cb=cb))
                for cb in col_blocks]
    args = [a.reshape(nseq, t_len, a.shape[1]) for a in arrays]
    if extra_in is not None:
        in_specs.append(ck_spec)
        args.append(extra_in)
    out_specs, out_shape = [], []
    for o in outs:
        if o == "ckpt":
            out_specs.append(ck_spec)
            out_shape.append(SDS(ck_shape, F32))
        else:
            out_specs.append(BS((sb, GDN_CHUNK, HEADS_W), lambda p, c: (p, cidx(c), 0)))
            out_shape.append(SDS((nseq, t_len, HEADS_W), o))
    res = pl.pallas_call(
        body, name=name, grid=(nseq // sb, nc), in_specs=in_specs, out_specs=out_specs, out_shape=out_shape,
        scratch_shapes=[pltpu.VMEM((chains, D_HEAD, D_HEAD), F32)], compiler_params=_params(2),
    )(*args)
    return [r if o == "ckpt" else r.reshape(nseq * t_len, HEADS_W) for r, o in zip(res, outs)]


def _gdn_fwd(q, k, v, b, g, nseq, t_len, tag):
    def body(q_ref, k_ref, v_ref, b_ref, g_ref, o_ref, sck_ref, s_ref):
        @pl.when(pl.program_id(1) == 0)
        def _():
            s_ref[...] = jnp.zeros_like(s_ref)

        s = s_ref[...]
        sck_ref[...] = s
        o, s_new = _gdn_chunk(*[_to_chains(r[...]) for r in (q_ref, k_ref, v_ref, b_ref, g_ref)], s)
        _from_chains(o_ref, slice(None), o)
        s_ref[...] = s_new

    return _scan_call(body, f"gdn_fwd_{tag}", [q, k, v, b, g], [0] * 5, False, nseq, t_len, GDN_SEQS, None,
                      [F32, "ckpt"])


def _gdn_bwd(q, k, v, b, g, sck, do, nseq, t_len, tag):
    def body(q_ref, k_ref, v_ref, b_ref, g_ref, do_ref, sck_ref, dq_ref, dk_ref, dv_ref, db_ref, dg_ref, ds_ref):
        @pl.when(pl.program_id(1) == 0)
        def _():
            ds_ref[...] = jnp.zeros_like(ds_ref)

        _, vjp = jax.vjp(_gdn_chunk, *[_to_chains(r[...]) for r in (q_ref, k_ref, v_ref, b_ref, g_ref)], sck_ref[...])
        grads = vjp((_to_chains(do_ref[...]), ds_ref[...]))
        for ref, val in zip((dq_ref, dk_ref, dv_ref, db_ref, dg_ref), grads[:5]):
            _from_chains(ref, slice(None), val)
        ds_ref[...] = grads[5]

    return _scan_call(body, f"gdn_bwd_{tag}", [q, k, v, b, g, do], [0] * 6, True, nseq, t_len, GDN_SEQS, sck,
                      [F32] * 5)


def _hgrn_fwd(q, k, v, v_col, lf, nseq, t_len, tag):
    def body(q_ref, k_ref, v_ref, lf_ref, o_ref, sck_ref, s_ref):
        @pl.when(pl.program_id(1) == 0)
        def _():
            s_ref[...] = jnp.zeros_like(s_ref)

        s = s_ref[...]
        sck_ref[...] = s
        o, s_new = _hgrn_block(*[_to_chains(r[...]) for r in (q_ref, k_ref, v_ref, lf_ref)], s)
        _from_chains(o_ref, slice(None), o)
        s_ref[...] = s_new

    return _scan_call(body, f"hgrn_fwd_{tag}", [q, k, v, lf], [0, 0, v_col, 0], False, nseq, t_len, HGRN_SEQS, None,
                      [F32, "ckpt"])


def _hgrn_bwd(q, k, v, v_col, lf, sck, do, nseq, t_len, tag):
    def body(q_ref, k_ref, v_ref, lf_ref, do_ref, sck_ref, dq_ref, dk_ref, dv_ref, dlf_ref, ds_ref):
        @pl.when(pl.program_id(1) == 0)
        def _():
            ds_ref[...] = jnp.zeros_like(ds_ref)

        _, vjp = jax.vjp(_hgrn_block, *[_to_chains(r[...]) for r in (q_ref, k_ref, v_ref, lf_ref)], sck_ref[...])
        grads = vjp((_to_chains(do_ref[...]), ds_ref[...]))
        for ref, val in zip((dq_ref, dk_ref, dv_ref, dlf_ref), grads[:4]):
            _from_chains(ref, slice(None), val)
        ds_ref[...] = grads[4]

    return _scan_call(body, f"hgrn_bwd_{tag}", [q, k, v, lf, do], [0, 0, v_col, 0, 0], True, nseq, t_len, HGRN_SEQS, sck,
                      [F32, F32, BF16, F32])


def _post_values(oa_ref, ob_ref, z_ref, bg_ref, ga_ref, gb_ref, gn_ref, wa_ref, wb_ref, ya_ref, yb_ref):
    for hd in range(N_HEADS):
        sl = slice(hd * D_HEAD, (hd + 1) * D_HEAD)
        ya_ref[:, sl] = _gated_norm(oa_ref[:, sl], z_ref[:, sl], gn_ref[0:1, :]).astype(BF16)
        yb_ref[:, sl] = _gated_norm(ob_ref[:, sl], bg_ref[:, sl], gn_ref[1:2, :]).astype(BF16)
    pa = jnp.dot(ya_ref[...], wa_ref[...], preferred_element_type=F32)
    pb = jnp.dot(yb_ref[...], wb_ref[...], preferred_element_type=F32)
    return pa, pb, _sigmoid(ga_ref[...]), _sigmoid(gb_ref[...])


def _post_specs(tm):
    r2 = lambda i: (i, 0)
    return [BS((tm, HEADS_W), r2), BS((tm, HEADS_W), r2),
            BS((tm, HEADS_W), lambda i: (i, C_Z // HEADS_W)), BS((tm, HEADS_W), lambda i: (i, C_BG // HEADS_W)),
            BS((tm, D_MODEL), lambda i: (i, C_GA // D_MODEL)), BS((tm, D_MODEL), lambda i: (i, C_GB // D_MODEL)),
            BS((tm, D_MODEL), r2), BS((SUBLANES, LANES), lambda i: (0, 0))]


def _post_fwd(oa, ob, proj, h, gn8, wa, wb, wout, tag):
    n = h.shape[0]
    tm = _pick(n, (256, 192, 128, 64))

    def body(oa_ref, ob_ref, z_ref, bg_ref, ga_ref, gb_ref, h_ref, gn_ref, wa_ref, wb_ref, wout_ref, out_ref,
             ya_ref, yb_ref):
        pa, pb, sa, sb = _post_values(oa_ref, ob_ref, z_ref, bg_ref, ga_ref, gb_ref, gn_ref, wa_ref, wb_ref,
                                      ya_ref, yb_ref)
        mixed = (sa * pa + sb * pb).astype(BF16)
        out_ref[...] = h_ref[...] + jnp.dot(mixed, wout_ref[...], preferred_element_type=F32)

    full = lambda i: (0, 0)
    return pl.pallas_call(
        body, name=f"post_fwd_{tag}", grid=(n // tm,),
        in_specs=_post_specs(tm) + [BS((HEADS_W, D_MODEL), full), BS((HEADS_W, D_MODEL), full),
                                    BS((D_MODEL, D_MODEL), full)],
        out_specs=BS((tm, D_MODEL), lambda i: (i, 0)), out_shape=SDS((n, D_MODEL), F32),
        scratch_shapes=[pltpu.VMEM((tm, HEADS_W), BF16), pltpu.VMEM((tm, HEADS_W), BF16)], compiler_params=_params(1),
    )(oa, ob, proj, proj, proj, proj, h, gn8, wa, wb, wout)


def _post_bwd(dh, oa, ob, proj, h, gn8, wa, wb, wa_t, wb_t, wout_t, tag):
    n = h.shape[0]
    tm = _pick(n, (256, 192, 128, 64))

    def body(dh_ref, oa_ref, ob_ref, z_ref, bg_ref, ga_ref, gb_ref, h_ref, gn_ref, wa_ref, wb_ref, wat_ref, wbt_ref,
             woutt_ref, doa_ref, dob_ref, dz_ref, dbg_ref, dga_ref, dgb_ref, dwa_ref, dwb_ref, dwout_ref, dgn_ref,
             ya_ref, yb_ref):
        @pl.when(pl.program_id(0) == 0)
        def _():
            dwa_ref[...] = jnp.zeros_like(dwa_ref)
            dwb_ref[...] = jnp.zeros_like(dwb_ref)
            dwout_ref[...] = jnp.zeros_like(dwout_ref)
            dgn_ref[...] = jnp.zeros_like(dgn_ref)

        pa, pb, sa, sb = _post_values(oa_ref, ob_ref, z_ref, bg_ref, ga_ref, gb_ref, gn_ref, wa_ref, wb_ref,
                                      ya_ref, yb_ref)
        mixed = (sa * pa + sb * pb).astype(BF16)
        dout = dh_ref[...].astype(BF16)
        dwout_ref[...] += _dg(mixed, dout, ((0,), (0,)))
        dmixed = jnp.dot(dout, woutt_ref[...], preferred_element_type=F32)
        dga_ref[...] = (dmixed * pa * sa * (1.0 - sa)).astype(BF16)
        dgb_ref[...] = (dmixed * pb * sb * (1.0 - sb)).astype(BF16)
        dpa = (dmixed * sa).astype(BF16)
        dpb = (dmixed * sb).astype(BF16)
        dwa_ref[...] += _dg(ya_ref[...], dpa, ((0,), (0,)))
        dwb_ref[...] += _dg(yb_ref[...], dpb, ((0,), (0,)))
        dya = jnp.dot(dpa, wat_ref[...], preferred_element_type=F32)
        dyb = jnp.dot(dpb, wbt_ref[...], preferred_element_type=F32)
        dgn_a = jnp.zeros((1, D_HEAD), F32)
        dgn_b = jnp.zeros((1, D_HEAD), F32)
        for hd in range(N_HEADS):
            sl = slice(hd * D_HEAD, (hd + 1) * D_HEAD)
            _, vjp = jax.vjp(_gated_norm, oa_ref[:, sl], z_ref[:, sl], gn_ref[0:1, :])
            doa, dz, dgw = vjp(dya[:, sl])
            doa_ref[:, sl], dz_ref[:, sl], dgn_a = doa, dz.astype(BF16), dgn_a + dgw
            _, vjp = jax.vjp(_gated_norm, ob_ref[:, sl], bg_ref[:, sl], gn_ref[1:2, :])
            dob, dbg, dgw = vjp(dyb[:, sl])
            dob_ref[:, sl], dbg_ref[:, sl], dgn_b = dob, dbg.astype(BF16), dgn_b + dgw
        dgn_ref[0:1, :] += dgn_a
        dgn_ref[1:2, :] += dgn_b

    full = lambda i: (0, 0)
    r2 = lambda i: (i, 0)
    return pl.pallas_call(
        body, name=f"post_bwd_{tag}", grid=(n // tm,),
        in_specs=[BS((tm, D_MODEL), r2)] + _post_specs(tm) + [
            BS((HEADS_W, D_MODEL), full), BS((HEADS_W, D_MODEL), full), BS((D_MODEL, HEADS_W), full),
            BS((D_MODEL, HEADS_W), full), BS((D_MODEL, D_MODEL), full)],
        out_specs=[BS((tm, HEADS_W), r2)] * 4 + [BS((tm, D_MODEL), r2)] * 2 + [
            BS((HEADS_W, D_MODEL), full), BS((HEADS_W, D_MODEL), full), BS((D_MODEL, D_MODEL), full),
            BS((SUBLANES, LANES), full)],
        out_shape=[SDS((n, HEADS_W), F32), SDS((n, HEADS_W), F32), SDS((n, HEADS_W), BF16), SDS((n, HEADS_W), BF16),
                   SDS((n, D_MODEL), BF16), SDS((n, D_MODEL), BF16), SDS((HEADS_W, D_MODEL), F32),
                   SDS((HEADS_W, D_MODEL), F32), SDS((D_MODEL, D_MODEL), F32), SDS((SUBLANES, LANES), F32)],
        scratch_shapes=[pltpu.VMEM((tm, HEADS_W), BF16), pltpu.VMEM((tm, HEADS_W), BF16)], compiler_params=_params(1),
    )(dh, oa, ob, proj, proj, proj, proj, h, gn8, wa, wb, wa_t, wb_t, wout_t)


def _loss_head(h, fw8, target, nseq, t_len):
    n = h.shape[0]
    nc = t_len // GDN_CHUNK
    inv_d = 1.0 / D_MODEL

    def body(h_ref, fw_ref, tgt_ref, dh_ref, acc_ref):
        @pl.when((pl.program_id(0) == 0) & (pl.program_id(1) == 0))
        def _():
            acc_ref[...] = jnp.zeros_like(acc_ref)

        frames = (pl.program_id(1) > 0).astype(F32)
        y, vjp = jax.vjp(_rms, h_ref[...], fw_ref[0:1, :])
        err = (y - tgt_ref[...]) * frames
        dx, dfw = vjp(err * inv_d)
        dh_ref[...] = dx
        acc_ref[0:1, :] += dfw
        acc_ref[1:2, :] += (0.5 * inv_d) * jnp.sum(err * err, axis=0, keepdims=True)

    return pl.pallas_call(
        body, name="loss_head", grid=(nseq, nc),
        in_specs=[BS((GDN_CHUNK, D_MODEL), lambda s, c: (s * nc + c, 0)), BS((SUBLANES, D_MODEL), lambda s, c: (0, 0)),
                  BS((None, GDN_CHUNK, D_MODEL), lambda s, c: (s, jnp.maximum(c - 1, 0), 0))],
        out_specs=[BS((GDN_CHUNK, D_MODEL), lambda s, c: (s * nc + c, 0)), BS((SUBLANES, D_MODEL), lambda s, c: (0, 0))],
        out_shape=[SDS((n, D_MODEL), F32), SDS((SUBLANES, D_MODEL), F32)], compiler_params=_params(2),
    )(h, fw8, target)


def _prep_bwd(proj, dq, dk, dv, db, dg, dqb, dkb, dlf, cw8, aux, lb8, nseq, t_len, tag):
    n = proj.shape[0]
    tt = _pick(t_len, (192, 128, 64))
    nt_ = t_len // tt
    qkv_w = 3 * HEADS_W
    rb = tt // SUBLANES
    ext = tt + SUBLANES

    def body(cur_ref, prev_ref, next_ref, misc_ref, bq_ref, bf_ref, dq_ref, dqn_ref, dk_ref, dkn_ref, dv_ref, dvn_ref,
             db_ref, dg_ref, dqb_ref, dkb_ref, dlf_ref, cw_ref, aux_ref, lb_ref,
             dqkv_ref, dmisc_ref, dbq_ref, dbf_ref, dcw_ref, daux_ref, dlb_ref, dy_ref):
        s, t = pl.program_id(0), pl.program_id(1)

        @pl.when((s == 0) & (t == 0))
        def _():
            dcw_ref[...] = jnp.zeros_like(dcw_ref)
            daux_ref[...] = jnp.zeros_like(daux_ref)
            dlb_ref[...] = jnp.zeros_like(dlb_ref)

        prev = jnp.where(t == 0, 0.0, prev_ref[...])
        x_ext = jnp.concatenate([prev, cur_ref[...], next_ref[...]], axis=0)
        y = _conv_ext(x_ext, cw_ref)
        inside = (t < nt_ - 1) | (_iota2((ext, 1), 0) < tt)
        dy_ref[0:SUBLANES, :] = jnp.zeros((SUBLANES, qkv_w), F32)
        for hd in range(N_HEADS):
            for grp, (g_ref, gn_ref, scale) in enumerate(((dq_ref, dqn_ref, D_HEAD ** -0.5), (dk_ref, dkn_ref, 1.0),
                                                          (dv_ref, dvn_ref, None))):
                lo = grp * HEADS_W + hd * D_HEAD
                sl = slice(hd * D_HEAD, (hd + 1) * D_HEAD)
                cot = jnp.concatenate([g_ref[:, sl], gn_ref[:, sl]], axis=0)
                fn = _silu if scale is None else functools.partial(_l2n_act, scale=scale)
                _, vjp = jax.vjp(fn, y[:, lo:lo + D_HEAD])
                dy_ref[SUBLANES:, lo:lo + D_HEAD] = jnp.where(inside, vjp(cot)[0], 0.0)
        dy_ext = dy_ref[...]
        dx = dy_ext * cw_ref[3:4, :]
        for kk in range(3):
            dx = dx + _shift_up(dy_ext, 3 - kk) * cw_ref[kk:kk + 1, :]
        dqkv_ref[...] = dx[SUBLANES:SUBLANES + tt].astype(BF16)
        dy_cur = dy_ext[SUBLANES:SUBLANES + tt]
        for kk in range(4):
            xs = _shift_down(x_ext, 3 - kk)[SUBLANES:SUBLANES + tt]
            dcw_ref[kk:kk + 1, :] += jnp.sum(xs * dy_cur, axis=0, keepdims=True)

        real = (t * tt + _iota2((tt, 1), 0)) >= N_PAD
        dmisc = jnp.zeros((tt, LANES), F32)
        daux = jnp.zeros((SUBLANES, LANES), F32)
        for hd in range(N_HEADS):
            sl = slice(hd * D_HEAD, (hd + 1) * D_HEAD)
            _, vjp = jax.vjp(lambda m, a: _gdn_gates(m, a, real, hd), misc_ref[...], aux_ref[...])
            dm, da = vjp((db_ref[:, sl], dg_ref[:, sl]))
            dmisc, daux = dmisc + dm, daux + da
        dmisc_ref[...] = dmisc.astype(BF16)
        daux_ref[...] += daux
        _, vjp = jax.vjp(lambda a, b, c: _hgrn_prep(a, b, c, real), bq_ref[...], bf_ref[...], lb_ref[0:1, :])
        dbq, dbf, dlb = vjp((dqb_ref[...], dkb_ref[...], dlf_ref[...]))
        dbq_ref[...], dbf_ref[...] = dbq.astype(BF16), dbf.astype(BF16)
        dlb_ref[0:1, :] += dlb

    row = lambda s, t: s * nt_ + t
    cur = lambda s, t: (row(s, t), 0)
    nxt = lambda s, t: (jnp.minimum((row(s, t) + 1) * rb, n // SUBLANES - 1), 0)
    wide = BS((tt, HEADS_W), cur)
    halo = BS((SUBLANES, HEADS_W), nxt)
    full = lambda s, t: (0, 0)
    return pl.pallas_call(
        body, name=f"prep_bwd_{tag}", grid=(nseq, nt_),
        in_specs=[BS((tt, qkv_w), cur), BS((SUBLANES, qkv_w), lambda s, t: (jnp.maximum(row(s, t) * rb - 1, 0), 0)),
                  BS((SUBLANES, qkv_w), nxt), BS((tt, LANES), lambda s, t: (row(s, t), C_MISC // LANES)),
                  BS((tt, HEADS_W), lambda s, t: (row(s, t), C_BQ // HEADS_W)),
                  BS((tt, HEADS_W), lambda s, t: (row(s, t), C_BF // HEADS_W)),
                  wide, halo, wide, halo, wide, halo, wide, wide, wide, wide, wide,
                  BS((SUBLANES, qkv_w), full), BS((SUBLANES, LANES), full), BS((SUBLANES, HEADS_W), full)],
        out_specs=[BS((tt, qkv_w), cur), BS((tt, LANES), cur), wide, wide,
                   BS((SUBLANES, qkv_w), full), BS((SUBLANES, LANES), full), BS((SUBLANES, HEADS_W), full)],
        out_shape=[SDS((n, qkv_w), BF16), SDS((n, LANES), BF16), SDS((n, HEADS_W), BF16), SDS((n, HEADS_W), BF16),
                   SDS((SUBLANES, qkv_w), F32), SDS((SUBLANES, LANES), F32), SDS((SUBLANES, HEADS_W), F32)],
        scratch_shapes=[pltpu.VMEM((tt + 2 * SUBLANES, qkv_w), F32)], compiler_params=_params(2),
    )(proj, proj, proj, proj, proj, proj, dq, dq, dk, dk, dv, dv, db, dg, dqb, dkb, dlf, cw8, aux, lb8)


def _proj_bwd_x(pieces, wp_t, h, nw8, dh_res, tag):
    n = h.shape[0]
    tm = _pick(n, (256, 192, 128, 64))
    widths = [p.shape[1] for p in pieces]
    assert sum(widths) == PROJ_W

    def body(*refs):
        p_refs = refs[:len(pieces)]
        wt_ref, h_ref, nw_ref, dres_ref, dh_ref, cat_ref, dnw_ref = refs[len(pieces):]

        @pl.when(pl.program_id(0) == 0)
        def _():
            dnw_ref[...] = jnp.zeros_like(dnw_ref)

        off = 0
        for p_ref, w in zip(p_refs, widths):
            cat_ref[:, off:off + w] = p_ref[...]
            off += w
        dxn = jnp.dot(cat_ref[...], wt_ref[...], preferred_element_type=F32)
        _, vjp = jax.vjp(_rms, h_ref[...], nw_ref[0:1, :])
        dx, dnw = vjp(dxn)
        dh_ref[...] = dres_ref[...] + dx
        dnw_ref[0:1, :] += dnw

    r2 = lambda i: (i, 0)
    full = lambda i: (0, 0)
    return pl.pallas_call(
        body, name=f"proj_bwd_x_{tag}", grid=(n // tm,),
        in_specs=[BS((tm, w), r2) for w in widths] + [BS((PROJ_W, D_MODEL), full), BS((tm, D_MODEL), r2),
                                                      BS((SUBLANES, D_MODEL), full), BS((tm, D_MODEL), r2)],
        out_specs=[BS((tm, D_MODEL), r2), BS((tm, PROJ_W), r2), BS((SUBLANES, D_MODEL), full)],
        out_shape=[SDS((n, D_MODEL), F32), SDS((n, PROJ_W), BF16), SDS((SUBLANES, D_MODEL), F32)],
        compiler_params=_params(1),
    )(*pieces, wp_t, h, nw8, dh_res)


def _proj_bwd_w(xn, dproj, tag):
    n = xn.shape[0]
    tm = _pick(n, (768, 512, 384, 256, 192, 128, 64))
    tn = 896

    def body(x_ref, d_ref, o_ref):
        @pl.when(pl.program_id(1) == 0)
        def _():
            o_ref[...] = jnp.zeros_like(o_ref)

        o_ref[...] += _dg(x_ref[...], d_ref[...], ((0,), (0,)))

    return pl.pallas_call(
        body, name=f"proj_bwd_w_{tag}", grid=(PROJ_W // tn, n // tm),
        in_specs=[BS((tm, D_MODEL), lambda j, i: (i, 0)), BS((tm, tn), lambda j, i: (i, j))],
        out_specs=BS((D_MODEL, tn), lambda j, i: (0, j)), out_shape=SDS((D_MODEL, PROJ_W), F32),
        compiler_params=_params(2),
    )(xn, dproj)


def _adamw(w, g, m, v, name):
    rows, cols = w.shape
    tr = _pick(rows, (256, 128, 64, 32, 16, 8, 4, 2, 1)) if rows > 256 else rows

    def body(w_ref, g_ref, m_ref, v_ref, d_ref, nm_ref, nv_ref):
        gr = g_ref[...]
        m_new = ADAM_B1 * m_ref[...] + (1.0 - ADAM_B1) * gr
        v_new = ADAM_B2 * v_ref[...] + (1.0 - ADAM_B2) * jnp.square(gr)
        m_hat = m_new / (1.0 - ADAM_B1 ** ADAM_STEP)
        v_hat = v_new / (1.0 - ADAM_B2 ** ADAM_STEP)
        d_ref[...] = -ADAM_LR * (m_hat / (jnp.sqrt(v_hat) + ADAM_EPS) + ADAM_WD * w_ref[...])
        nm_ref[...] = m_new
        nv_ref[...] = v_new

    blk = BS((tr, cols), lambda i: (i, 0))
    return pl.pallas_call(
        body, name=name, grid=(rows // tr,), in_specs=[blk] * 4, out_specs=[blk] * 3,
        out_shape=[SDS((rows, cols), F32)] * 3, compiler_params=_params(1),
    )(w, g, m, v)


def _row8(v, width):
    v = jnp.atleast_2d(v).astype(F32)
    return jnp.pad(v, ((0, SUBLANES - v.shape[0]), (0, width - v.shape[1])))


REF_MISC = 1536
N_MISC = 2 * N_HEADS
LAYOUT_RUNS = ((0, REF_MISC, 0), (REF_MISC + N_MISC, REF_W, REF_MISC), (REF_MISC, REF_MISC + N_MISC, C_MISC))


def _to_layout(w_full):
    runs = [w_full[:, lo:hi] for lo, hi, _ in sorted(LAYOUT_RUNS, key=lambda run: run[2])]
    return jnp.concatenate(runs + [jnp.zeros((w_full.shape[0], PROJ_W - REF_W), w_full.dtype)], axis=1)


def _from_layout(dw, n_slabs):
    width = REF_W // n_slabs
    slabs = []
    for j in range(n_slabs):
        pieces = []
        for lo, hi, at in sorted(LAYOUT_RUNS):
            a, b = max(lo, j * width), min(hi, (j + 1) * width)
            if a < b:
                pieces.append(dw[:, at + a - lo:at + b - lo])
        slabs.append(jnp.concatenate(pieces, axis=1))
    return slabs


def _lower_bounds(lb):
    sm = jax.nn.softmax(lb.astype(F32), axis=0)
    return jnp.cumsum(sm, axis=0) - sm[0]


def kernel(x, meta_tokens, norm_w, w_in, conv_w, a_log, dt_bias, gnorm_a, gnorm_b, hgrn_lower_bounds, w_branch_a, w_branch_b, w_out, final_norm_w, loss_target, m_meta_tokens, m_norm_w, m_w_in, m_conv_w, m_a_log, m_dt_bias, m_gnorm_a, m_gnorm_b, m_hgrn_lower_bounds, m_w_branch_a, m_w_branch_b, m_w_out, m_final_norm_w, v_meta_tokens, v_norm_w, v_w_in, v_conv_w, v_a_log, v_dt_bias, v_gnorm_a, v_gnorm_b, v_hgrn_lower_bounds, v_w_branch_a, v_w_branch_b, v_w_out, v_final_norm_w):
    nseq, seq, _ = x.shape
    depth = norm_w.shape[0]
    t_len = N_PAD + N_META + seq
    n = nseq * t_len
    win_c, conv_c = w_in.shape[2], conv_w.shape[2]
    my = 4 * lax.axis_index("x") + 2 * lax.axis_index("y") + lax.axis_index("c")

    mats = (w_in.astype(BF16), w_branch_a.astype(BF16), w_branch_b.astype(BF16), w_out.astype(BF16))
    first = _all_gather_hbm([m[:1] for m in mats] + [conv_w, meta_tokens], "gather_first")
    later_flight, later_token = _send_all_start([m[1:] for m in mats], False, "gather_later_start")

    def full_weights(g_win_, g_wa_, g_wb_, g_wout_):
        layers = g_win_.shape[1]
        return (g_win_.transpose(1, 2, 0, 3).reshape(layers, D_MODEL, REF_W),
                g_wa_.transpose(1, 2, 0, 3).reshape(layers, HEADS_W, D_MODEL),
                g_wb_.transpose(1, 2, 0, 3).reshape(layers, HEADS_W, D_MODEL),
                g_wout_.transpose(1, 0, 2, 3).reshape(layers, D_MODEL, D_MODEL))

    w_in_full, wa_full, wb_full, wout_full = ([w] for w in full_weights(*first[:4]))
    conv_full = first[4].transpose(1, 2, 0, 3).reshape(depth, 4, 3 * HEADS_W)
    meta_full = first[5].transpose(1, 0, 2).reshape(N_META, D_MODEL)

    lb_all, lb_vjp = jax.vjp(_lower_bounds, hgrn_lower_bounds)

    h = jnp.concatenate([jnp.zeros((nseq, N_PAD, D_MODEL), F32),
                         jnp.broadcast_to(meta_full[None], (nseq, N_META, D_MODEL)), x], axis=1).reshape(n, D_MODEL)
    saved = []
    layer_of = lambda groups, l: groups[0][0] if l == 0 else groups[1][l - 1]
    for l in range(depth):
        if l == 1:
            sent, landed = _send_all_wait(later_flight, h, "gather_later_wait")
            landed = [lax.dynamic_update_slice(ld, own[None], (my,) + (0,) * own.ndim) for ld, own in zip(landed, sent)]
            for groups, w in zip((w_in_full, wa_full, wb_full, wout_full), full_weights(*landed)):
                groups.append(w)
        wp = _to_layout(layer_of(w_in_full, l))
        wa_l, wb_l, wout_l = layer_of(wa_full, l), layer_of(wb_full, l), layer_of(wout_full, l)
        nw8 = _row8(norm_w[l], D_MODEL)
        if l == 0:
            nw8 = nw8 + later_token[0:1, 0:1]
        cw8 = _row8(conv_full[l], 3 * HEADS_W)
        aux = _row8(jnp.stack([a_log[l], dt_bias[l]]), LANES)
        lb8 = _row8(lb_all[l], HEADS_W)
        gn8 = _row8(jnp.stack([gnorm_a[l], gnorm_b[l]]), LANES)
        proj, xn = _proj_fwd(h, nw8, wp, l)
        q, k, v, b, g, qb, kb, lf = _prep_fwd(proj, cw8, aux, lb8, nseq, t_len, l)
        oa, sck_a = _gdn_fwd(q, k, v, b, g, nseq, t_len, l)
        ob, sck_b = _hgrn_fwd(qb, kb, proj, C_BI // HEADS_W, lf, nseq, t_len, l)
        h_next = _post_fwd(oa, ob, proj, h, gn8, wa_l, wb_l, wout_l, l)
        saved.append(dict(h=h, wp=wp, nw8=nw8, cw8=cw8, aux=aux, lb8=lb8, gn8=gn8, proj=proj, xn=xn, q=q, k=k, v=v, b=b,
                          wa=wa_l, wb=wb_l, wout=wout_l,
                          g=g, qb=qb, kb=kb, lf=lf, oa=oa, ob=ob, sck_a=sck_a, sck_b=sck_b))
        h = h_next

    dh, acc = _loss_head(h, _row8(final_norm_w, D_MODEL), loss_target, nseq, t_len)

    g_win, g_wa, g_wb, g_wout, g_conv, small = [], [], [], [], [], []

    def grad_slabs(ls, dtype):
        at = lambda g, l: g[depth - 1 - l]
        rows = lambda a, w: a.reshape(len(ls) * a.shape[1], N_DEV, w).transpose(1, 0, 2)
        dwa_, dwb_, dwout_ = (jnp.stack([at(g, l) for l in ls]) for g in (g_wa, g_wb, g_wout))
        win = jnp.stack([jnp.concatenate([at(g_win, l)[j] for l in ls], axis=0) for j in range(N_DEV)])
        ab = jnp.concatenate([rows(dwa_, LANES), rows(dwb_, LANES)], axis=1)
        wout = dwout_.reshape(len(ls), N_DEV, LANES, D_MODEL).transpose(1, 0, 2, 3).reshape(N_DEV, len(ls) * LANES, D_MODEL)
        return [a.astype(dtype) for a in (win, ab, wout)]

    for l in reversed(range(depth)):
        s = saved[l]
        gn8 = s["gn8"]
        if l == 0:
            later_flight, later_token = _send_all_start(grad_slabs(range(1, depth), BF16), True, "scatter_later_start")
            gn8 = gn8 + later_token[0:1, 0:1]
        doa, dob, dz, dbg, dga, dgb, dwa, dwb, dwout, dgn = _post_bwd(
            dh, s["oa"], s["ob"], s["proj"], s["h"], gn8, s["wa"], s["wb"], s["wa"].T, s["wb"].T, s["wout"].T, l)
        dq, dk, dv, db, dg = _gdn_bwd(s["q"], s["k"], s["v"], s["b"], s["g"], s["sck_a"], doa, nseq, t_len, l)
        dqb, dkb, dbi, dlf = _hgrn_bwd(s["qb"], s["kb"], s["proj"], C_BI // HEADS_W, s["lf"], s["sck_b"], dob, nseq,
                                       t_len, l)
        dqkv, dmisc, dbq, dbf, dcw, daux, dlb = _prep_bwd(s["proj"], dq, dk, dv, db, dg, dqb, dkb, dlf, s["cw8"],
                                                          s["aux"], s["lb8"], nseq, t_len, l)
        dh, dproj, dnw = _proj_bwd_x([dqkv, dz, dbq, dbf, dbi, dbg, dga, dgb, dmisc], s["wp"].T, s["h"], s["nw8"], dh, l)
        dwp = _proj_bwd_w(s["xn"], dproj, l)
        g_win.append(_from_layout(dwp, N_DEV))
        g_wa.append(dwa)
        g_wb.append(dwb)
        g_wout.append(dwout)
        g_conv.append(dcw[:4])
        small.append((dnw[0], dgn[0], dgn[1], daux[0, :N_HEADS], daux[1, :N_HEADS], dlb[0]))
    first_slabs = grad_slabs([0], F32)
    g_conv.reverse()
    small.reverse()
    dh = dh.reshape(nseq, t_len, D_MODEL)
    grad_x = dh[:, N_PAD + N_META:]

    packed = jnp.concatenate([small[0][1], small[1][1], small[0][2], small[1][2], small[0][3], small[1][3],
                              small[0][4], small[1][4]])
    tile = jnp.concatenate([
        jnp.sum(dh[:, N_PAD:N_PAD + N_META], axis=0), _row8(jnp.stack([small[0][0], small[1][0], acc[0]]), D_MODEL),
        _row8(jnp.stack([small[0][5], small[1][5]]), D_MODEL), _row8(packed, D_MODEL), _row8(acc[1], D_MODEL)], axis=0)
    tile = _all_reduce_small(tile, "reduce_small")
    loss = jnp.sum(tile[40])
    g_meta = lax.dynamic_slice_in_dim(tile[0:N_META], my * LANES, LANES, axis=1)
    g_norm, g_final = tile[16:18], tile[18]
    (g_lb,) = lb_vjp(tile[24:26, :HEADS_W])
    r21 = tile[32]
    g_gna, g_gnb = r21[0:256].reshape(2, LANES), r21[256:512].reshape(2, LANES)
    g_alog, g_dtb = r21[512:520].reshape(2, N_HEADS), r21[520:528].reshape(2, N_HEADS)

    dconv = jnp.stack(g_conv)
    conv_slabs = dconv.reshape(depth * dconv.shape[1], N_DEV, conv_c).transpose(1, 0, 2)
    coords = jnp.stack([lax.axis_index("x"), lax.axis_index("y"), lax.axis_index("c")]).astype(jnp.int32)
    r_win, r_ab, r_wout, r_conv = _reduce_scatter(first_slabs + [conv_slabs], coords, "grads")
    sent, landed = _send_all_wait(later_flight, dwp, "scatter_later_wait")
    landed = [lax.dynamic_update_slice(ld, lax.dynamic_index_in_dim(src, my, 0, keepdims=True), (my, 0, 0))
              for ld, src in zip(landed, sent)]
    l_win, l_ab, l_wout = [_sum_slabs(ld, f"scatter_later_sum{i}") for i, ld in enumerate(landed)]
    both = lambda a, b, shape: jnp.concatenate([a.reshape(1, *shape[1:]), b.reshape(depth - 1, *shape[1:])])
    half, half_l = HEADS_W, (depth - 1) * HEADS_W
    mine = [both(r_win, l_win, w_in.shape), both(r_ab[:half], l_ab[:half_l], w_branch_a.shape),
            both(r_ab[half:], l_ab[half_l:], w_branch_b.shape), both(r_wout, l_wout, w_out.shape), r_conv]
    gseg = lambda i, shape: mine[i].reshape(shape)
    grads = {
        "meta_tokens": g_meta, "norm_w": g_norm, "w_in": gseg(0, w_in.shape), "conv_w": gseg(4, conv_w.shape),
        "a_log": g_alog, "dt_bias": g_dtb, "gnorm_a": g_gna, "gnorm_b": g_gnb, "hgrn_lower_bounds": g_lb,
        "w_branch_a": gseg(1, w_branch_a.shape), "w_branch_b": gseg(2, w_branch_b.shape), "w_out": gseg(3, w_out.shape),
        "final_norm_w": g_final}
    weights = {
        "meta_tokens": (meta_tokens, m_meta_tokens, v_meta_tokens), "norm_w": (norm_w, m_norm_w, v_norm_w),
        "w_in": (w_in, m_w_in, v_w_in), "conv_w": (conv_w, m_conv_w, v_conv_w), "a_log": (a_log, m_a_log, v_a_log),
        "dt_bias": (dt_bias, m_dt_bias, v_dt_bias), "gnorm_a": (gnorm_a, m_gnorm_a, v_gnorm_a),
        "gnorm_b": (gnorm_b, m_gnorm_b, v_gnorm_b),
        "hgrn_lower_bounds": (hgrn_lower_bounds, m_hgrn_lower_bounds, v_hgrn_lower_bounds),
        "w_branch_a": (w_branch_a, m_w_branch_a, v_w_branch_a), "w_branch_b": (w_branch_b, m_w_branch_b, v_w_branch_b),
        "w_out": (w_out, m_w_out, v_w_out), "final_norm_w": (final_norm_w, m_final_norm_w, v_final_norm_w)}
    names = list(weights)
    deltas, new_m, new_v = [], [], []
    for nm in names:
        w, m, v = weights[nm]
        view = (-1, w.shape[-1])
        d, m2, v2 = _adamw(w.reshape(view), grads[nm].reshape(view), m.reshape(view), v.reshape(view), f"adamw_{nm}")
        deltas.append(d.reshape(w.shape))
        new_m.append(m2.reshape(w.shape))
        new_v.append(v2.reshape(w.shape))
    return (loss, grad_x, *[grads[nm].reshape(weights[nm][0].shape) for nm in names], *deltas, *new_m, *new_v)
```

```python
import functools

import jax
import jax.numpy as jnp
import numpy as np
from jax import lax
from jax.experimental import pallas as pl
from jax.experimental.pallas import tpu as pltpu

F32 = jnp.float32
BF16 = jnp.bfloat16

D_MODEL = 1024
N_HEADS = 4
D_HEAD = 128
HEADS_W = N_HEADS * D_HEAD
N_META = 16
N_PAD = 48
GDN_CHUNK = 64
HGRN_CHUNK = 16
EPS = 1e-6
N_DEV = 8
LANES = 128
SUBLANES = 8
VMEM_LIMIT = 56 * 1024 * 1024

C_QKV, C_Z, C_BQ, C_BF, C_BI, C_BG, C_GA, C_GB, C_MISC = 0, 1536, 2048, 2560, 3072, 3584, 4096, 5120, 6144
PROJ_W = 6272
REF_W = 6152

ADAM_LR, ADAM_B1, ADAM_B2, ADAM_EPS, ADAM_WD, ADAM_STEP = 0.001, 0.9, 0.999, 1e-08, 0.01, 10

MESH = pl.DeviceIdType.MESH
SDS = jax.ShapeDtypeStruct
BS = pl.BlockSpec


def _params(n_axes):
    return pltpu.CompilerParams(dimension_semantics=("arbitrary",) * n_axes, vmem_limit_bytes=VMEM_LIMIT)


def _pick(n, cands):
    for c in cands:
        if n % c == 0:
            return c
    raise ValueError(f"no tile for {n} among {cands}")


def _iota2(shape, dim):
    return lax.broadcasted_iota(jnp.int32, shape, dim)


def _dg(a, b, dims):
    return lax.dot_general(a.astype(BF16), b.astype(BF16), (dims, ((), ())), preferred_element_type=F32)


def _bdg(a, b, ca, cb):
    return lax.dot_general(a.astype(BF16), b.astype(BF16), (((ca,), (cb,)), ((0,), (0,))), preferred_element_type=F32)


@jax.custom_vjp
def _bnn(a, b):
    return _bdg(a, b, 2, 1)


@jax.custom_vjp
def _bnt(a, b):
    return _bdg(a, b, 2, 2)


@jax.custom_vjp
def _btn(a, b):
    return _bdg(a, b, 1, 1)


_bnn.defvjp(lambda a, b: (_bnn(a, b), (a, b)), lambda r, g: (_bnt(g, r[1]), _btn(r[0], g)))
_bnt.defvjp(lambda a, b: (_bnt(a, b), (a, b)), lambda r, g: (_bnn(g, r[1]), _btn(g, r[0])))
_btn.defvjp(lambda a, b: (_btn(a, b), (a, b)), lambda r, g: (_bnt(r[1], g), _bnn(r[0], g)))


def _split2(x):
    hi = x.astype(BF16).astype(F32)
    return hi, x - hi


def _tri(bsz, n):
    return jnp.broadcast_to((_iota2((n, n), 0) >= _iota2((n, n), 1)).astype(F32), (bsz, n, n))


@jax.custom_vjp
def _cumsum_rows(x):
    tri = _tri(x.shape[0], x.shape[1])
    hi, lo = _split2(x)
    return _bdg(tri, hi, 2, 1) + _bdg(tri, lo, 2, 1)


def _cumsum_rows_bwd(_, g):
    tri = _tri(g.shape[0], g.shape[1])
    hi, lo = _split2(g)
    return (_bdg(tri, hi, 1, 1) + _bdg(tri, lo, 1, 1),)


_cumsum_rows.defvjp(lambda x: (_cumsum_rows(x), None), _cumsum_rows_bwd)


def _sigmoid(x):
    return jax.nn.sigmoid(x)


def _silu(x):
    return x * _sigmoid(x)


def _softplus(x):
    return jnp.maximum(x, 0.0) + jnp.log1p(jnp.exp(-jnp.abs(x)))


def _rms(x, w):
    return x * lax.rsqrt(jnp.mean(x * x, axis=-1, keepdims=True) + EPS) * w


def _inv_unit_lower(lm):
    n = lm.shape[1]
    a = (_iota2((n, n), 0) == _iota2((n, n), 1)).astype(F32)[None] - lm
    steps = max(1, (n - 1).bit_length()) - 1
    p = _bnn(lm, lm)
    for i in range(steps):
        if i == steps - 1:
            a = a + _bnn(a, p)
        else:
            both = _bnn(jnp.concatenate([a, p], axis=1), p)
            a, p = a + both[:, :n], both[:, n:]
    return a


def _gdn_chunk(q, k, v, b_b, g_b, s):
    n, dv = q.shape[1], v.shape[2]
    r, c = _iota2((n, n), 0), _iota2((n, n), 1)
    causal, strict, eye = (r >= c)[None], (r > c)[None], (r == c)[None]
    g_cum = _cumsum_rows(g_b)
    g_i = g_cum[:, :, :n]
    g_j = jnp.sum(jnp.where(eye, g_i, 0.0), axis=1, keepdims=True)
    decay = jnp.where(causal, jnp.exp(jnp.where(causal, g_i - g_j, 0.0)), 0.0)
    e_g = jnp.exp(g_cum)
    kb = k * b_b
    kk = _bnt(jnp.concatenate([kb, q], axis=1), k)
    a_inv = _inv_unit_lower(jnp.where(strict, kk[:, :n] * decay, 0.0))
    uw = _bnn(a_inv, jnp.concatenate([v * b_b, kb * e_g], axis=2))
    ws = _bnn(jnp.concatenate([uw[:, :, dv:], q * e_g], axis=1), s)
    v_new = uw[:, :, :dv] - ws[:, :n]
    o = ws[:, n:] + _bnn(kk[:, n:] * decay, v_new)
    g_last = g_cum[:, n - 1:n, :]
    s_new = s * jnp.exp(g_last) +_btn(k * jnp.exp(g_last - g_cum), v_new)
    return o, s_new


@functools.partial(jax.custom_vjp, nondiff_argnums=(1, 2))
def _row(x, j, n):
    return x[:, j:j + 1, :]


def _row_bwd(j, n, _, g):
    return (jnp.where(_iota2((1, n, 1), 1) == j, g, 0.0),)


_row.defvjp(lambda x, j, n: (_row(x, j, n), None), _row_bwd)


def _hgrn_chunk(q, k, v, lf, st):
    n = q.shape[1]
    b_cum = _cumsum_rows(lf)
    o = _bnt(q * jnp.exp(b_cum), st)
    half = n // 2
    parts = []
    for lo in (0, half):
        qs, bs = q[:, lo:], b_cum[:, lo:]
        rows = _iota2((1, n - lo, 1), 1) + lo
        acc = jnp.zeros_like(qs)
        for j in range(lo, n if lo else half):
            p = jnp.exp(jnp.where(rows >= j, bs - _row(b_cum, j, n), -1e30))
            acc = acc + jnp.sum(qs * _row(k, j, n) * p, axis=2, keepdims=True) * _row(v, j, n)
        parts.append(acc)
    o = o + parts[0] + jnp.concatenate([jnp.zeros_like(parts[1]), parts[1]], axis=1)
    b_last = _row(b_cum, n - 1, n)
    st_new = st * jnp.exp(b_last) + _btn(v, k * jnp.exp(b_last - b_cum))
    return o, st_new


def _hgrn_block(q, k, v, lf, st):
    n = HGRN_CHUNK
    outs = []
    for c in range(q.shape[1] // n):
        rs = slice(c * n, (c + 1) * n)
        o, st = _hgrn_chunk(q[:, rs], k[:, rs], v[:, rs], lf[:, rs], st)
        outs.append(o)
    return jnp.concatenate(outs, axis=1), st


def _l2n_act(y, scale):
    a = _silu(y)
    return a * lax.rsqrt(jnp.sum(a * a, axis=-1, keepdims=True) + EPS) * scale


def _col(x, lane):
    return jnp.sum(jnp.where(_iota2(x.shape, 1) == lane, x, 0.0), axis=1, keepdims=True)


def _elem(x, row, lane):
    m = (_iota2(x.shape, 0) == row) & (_iota2(x.shape, 1) == lane)
    return jnp.sum(jnp.sum(jnp.where(m, x, 0.0), axis=1, keepdims=True), axis=0, keepdims=True)


def _gdn_gates(misc, aux, real, head):
    beta = _sigmoid(_col(misc, head))
    g = -jnp.exp(_elem(aux, 0, head)) * _softplus(_col(misc, N_HEADS + head) + _elem(aux, 1, head))
    g = jnp.where(real, g, 0.0)
    shape = (misc.shape[0], D_HEAD)
    return jnp.broadcast_to(beta, shape), jnp.broadcast_to(g, shape)


def _hgrn_prep(bq, bf, lb, real):
    qb = _silu(bq) * (D_HEAD ** -0.5)
    log_sig = jnp.minimum(bf, 0.0) - jnp.log1p(jnp.exp(-jnp.abs(bf)))
    pos = lb > 0.0
    lbs = jnp.where(pos, lb, 0.5)
    a = jnp.log(lbs)
    b = jnp.log1p(-lbs) + log_sig
    lae = jnp.maximum(a, b) + jnp.log1p(jnp.exp(-jnp.abs(a - b)))
    lf = jnp.where(pos, lae, log_sig)
    kb = jnp.where(pos, 1.0 - lbs, 1.0) * _sigmoid(-bf)
    return qb, jnp.where(real, kb, 0.0), jnp.where(real, lf, 0.0)


def _gated_norm(o, z, gw):
    return o * lax.rsqrt(jnp.mean(o * o, axis=-1, keepdims=True) + EPS) * gw * _silu(z)


def _shift_down(x, j):
    return x if j == 0 else pltpu.roll(x, j, 0)


def _shift_up(x, j):
    return x if j == 0 else pltpu.roll(x, x.shape[0] - j, 0)


def _all_gather_hbm(blocks, name):
    na = len(blocks)

    def body(*refs):
        x_refs, out_refs = refs[:na], refs[na:2 * na]
        send_sems, recv_sems, local_sems = refs[2 * na:]
        mx, my, mc = lax.axis_index("x"), lax.axis_index("y"), lax.axis_index("c")
        me, sibling = (mx, my, mc), (mx, my, 1 - mc)
        chips = [(1 - mx, my), (mx, 1 - my), (1 - mx, 1 - my)]

        def slab(a, px, py, pc):
            return out_refs[a].at[4 * px + 2 * py + pc]

        def copy(a, k, blk, to, own=False):
            return pltpu.make_async_remote_copy(
                src_ref=x_refs[a] if own else slab(a, *blk), dst_ref=slab(a, *blk),
                send_sem=send_sems.at[7 * a + k], recv_sem=recv_sems.at[7 * a + k], device_id=to, device_id_type=MESH)

        mine = [pltpu.make_async_copy(x_refs[a], slab(a, *me), local_sems.at[a]) for a in range(na)]
        for cp in mine:
            cp.start()
        first = [copy(a, 0, me, sibling, own=True) for a in range(na)]
        first += [copy(a, 1 + j, me, (*chip, mc), own=True) for j, chip in enumerate(chips) for a in range(na)]
        for cp in first:
            cp.start()
        passed = []
        for j, chip in enumerate(chips):
            for a in range(na):
                copy(a, 1 + j, (*chip, mc), me).wait_recv()
                passed.append(copy(a, 4 + j, (*chip, mc), sibling))
                passed[-1].start()
        for a in range(na):
            copy(a, 0, sibling, me).wait_recv()
            for j, chip in enumerate(chips):
                copy(a, 4 + j, (*chip, 1 - mc), me).wait_recv()
        for cp in first + passed:
            cp.wait_send()
        for cp in mine:
            cp.wait()

    hbm = BS(memory_space=pl.ANY)
    return pl.pallas_call(
        body, name=name, out_shape=[SDS((N_DEV, *b.shape), b.dtype) for b in blocks],
        in_specs=[hbm] * na, out_specs=[hbm] * na,
        scratch_shapes=[pltpu.SemaphoreType.DMA((7 * na,)), pltpu.SemaphoreType.DMA((7 * na,)),
                        pltpu.SemaphoreType.DMA((na,))],
    )(*blocks)


def _all_reduce_small(block, name):
    r, c = block.shape

    def body(x_ref, out_ref, buf, send_sems, recv_sems):
        mx, my, mc = lax.axis_index("x"), lax.axis_index("y"), lax.axis_index("c")
        me, sibling = (mx, my, mc), (mx, my, 1 - mc)
        chips = [(1 - mx, my), (mx, 1 - my), (1 - mx, 1 - my)]

        def slab(px, py, pc):
            return buf.at[4 * px + 2 * py + pc]

        def copy(k, blk, to, src=None):
            return pltpu.make_async_remote_copy(
                src_ref=slab(*blk) if src is None else src, dst_ref=slab(*blk),
                send_sem=send_sems.at[k], recv_sem=recv_sems.at[k], device_id=to, device_id_type=MESH)

        first = [copy(0, me, sibling, src=x_ref)]
        first += [copy(1 + j, me, (*chip, mc), src=x_ref) for j, chip in enumerate(chips)]
        for cp in first:
            cp.start()
        passed = [copy(4 + j, (*chip, mc), sibling) for j, chip in enumerate(chips)]
        for j, chip in enumerate(chips):
            copy(1 + j, (*chip, mc), me).wait_recv()
            passed[j].start()
        copy(0, sibling, me).wait_recv()
        for j, chip in enumerate(chips):
            copy(4 + j, (*chip, 1 - mc), me).wait_recv()
        for cp in first + passed:
            cp.wait_send()
        buf[4 * mx + 2 * my + mc] = x_ref[...]
        acc = buf[0]
        for d in range(1, N_DEV):
            acc = acc + buf[d]
        out_ref[...] = acc

    return pl.pallas_call(
        body, name=name, out_shape=SDS((r, c), F32),
        in_specs=[BS(memory_space=pltpu.VMEM)], out_specs=BS(memory_space=pltpu.VMEM),
        scratch_shapes=[pltpu.VMEM((N_DEV, r, c), F32), pltpu.SemaphoreType.DMA((7,)), pltpu.SemaphoreType.DMA((7,))],
    )(block)


HBM_SPEC = BS(memory_space=pltpu.HBM)
SEM_SPEC = BS(memory_space=pltpu.SEMAPHORE)
SIDE_EFFECT = pltpu.SideEffectType.DATAFLOW_SIDE_EFFECTING


def _peer(rel):
    flip = lambda v, bit: 1 - v if bit else v
    return (flip(lax.axis_index("x"), rel >> 2 & 1), flip(lax.axis_index("y"), rel >> 1 & 1),
            flip(lax.axis_index("c"), rel & 1))


def _send_all_start(blocks, scatter, name):
    na = len(blocks)
    shapes = [b.shape[1:] if scatter else b.shape for b in blocks]

    def body(*refs):
        srcs, lands = refs[:na], refs[na:2 * na]
        send_sems, recv_sems, token = refs[2 * na], refs[2 * na + 1], refs[-1]
        me = 4 * lax.axis_index("x") + 2 * lax.axis_index("y") + lax.axis_index("c")
        for a in range(na):
            for rel in range(1, N_DEV):
                px, py, pc = _peer(rel)
                pltpu.make_async_remote_copy(
                    src_ref=srcs[a].at[4 * px + 2 * py + pc] if scatter else srcs[a], dst_ref=lands[a].at[me],
                    send_sem=send_sems.at[7 * a + rel - 1], recv_sem=recv_sems.at[7 * a + rel - 1],
                    device_id=(px, py, pc), device_id_type=MESH).start()
        token[...] = jnp.zeros_like(token)

    lands = [lax.empty((N_DEV, *s), b.dtype) for s, b in zip(shapes, blocks)]
    res = pl.pallas_call(
        body, name=name,
        out_shape=([pltpu.SemaphoreType.DMA((7 * na,)), pltpu.SemaphoreType.DMA((7 * na,))]
                   + [pltpu.HBM(b.shape, b.dtype) for b in blocks] + [pltpu.HBM(ld.shape, ld.dtype) for ld in lands]
                   + [SDS((SUBLANES, LANES), F32)]),
        in_specs=[HBM_SPEC] * (2 * na), out_specs=[SEM_SPEC, SEM_SPEC] + [HBM_SPEC] * (2 * na) + [BS(memory_space=pltpu.VMEM)],
        input_output_aliases={i: 2 + i for i in range(2 * na)},
        compiler_params=pltpu.CompilerParams(has_side_effects=SIDE_EFFECT),
    )(*[pltpu.with_memory_space_constraint(b, pltpu.HBM) for b in blocks],
      *[pltpu.with_memory_space_constraint(ld, pltpu.HBM) for ld in lands])
    return dict(send=res[0], recv=res[1], srcs=res[2:2 + na], lands=res[2 + na:2 + 2 * na], scatter=scatter), res[-1]


def _send_all_wait(flight, after, name):
    na = len(flight["srcs"])

    def body(*refs):
        srcs, lands = refs[:na], refs[na:2 * na]
        send_sems, recv_sems = refs[2 * na], refs[2 * na + 1]
        for a in range(na):
            for rel in range(1, N_DEV):
                cp = pltpu.make_async_remote_copy(
                    src_ref=srcs[a].at[0] if flight["scatter"] else srcs[a], dst_ref=lands[a].at[0],
                    send_sem=send_sems.at[7 * a + rel - 1], recv_sem=recv_sems.at[7 * a + rel - 1],
                    device_id=_peer(rel), device_id_type=MESH)
                cp.wait_send()
                cp.wait_recv()

    arrays = list(flight["srcs"]) + list(flight["lands"])
    res = pl.pallas_call(
        body, name=name, out_shape=[pltpu.HBM(a.shape, a.dtype) for a in arrays],
        in_specs=[HBM_SPEC] * (2 * na) + [SEM_SPEC, SEM_SPEC, BS(memory_space=pl.ANY)], out_specs=[HBM_SPEC] * (2 * na),
        input_output_aliases={i: i for i in range(2 * na)},
        compiler_params=pltpu.CompilerParams(has_side_effects=SIDE_EFFECT),
    )(*arrays, flight["send"], flight["recv"], after)
    return res[:na], res[na:]


def _sum_slabs(land, name):
    _, r, c = land.shape
    tr = _pick(r, (256, 128, 64, 32, 16, 8))

    def body(l_ref, o_ref):
        acc = l_ref[0].astype(F32)
        for d in range(1, N_DEV):
            acc = acc + l_ref[d].astype(F32)
        o_ref[...] = acc

    return pl.pallas_call(
        body, name=name, grid=(r // tr,), out_shape=SDS((r, c), F32),
        in_specs=[BS((N_DEV, tr, c), lambda j: (0, j, 0))], out_specs=BS((tr, c), lambda j: (j, 0)),
        compiler_params=_params(1),
    )(land)


def _exchange(bufs, flip, paired, name):
    na, n = len(bufs), bufs[0].shape[0]
    axis = ("x", "y", "c")[flip]

    def body(*refs):
        g_refs, out_refs = refs[:na], refs[na:2 * na]
        send_sems, recv_sems = refs[2 * na:]
        pos = [lax.axis_index("x"), lax.axis_index("y"), lax.axis_index("c")]
        pos[flip] = 1 - pos[flip]
        other = 1 - lax.axis_index(axis)
        copies = [pltpu.make_async_remote_copy(
            src_ref=g_refs[a].at[i, other] if paired else g_refs[a].at[i], dst_ref=out_refs[a].at[i],
            send_sem=send_sems.at[n * a + i], recv_sem=recv_sems.at[n * a + i], device_id=tuple(pos),
            device_id_type=MESH) for i in range(n) for a in range(na)]
        for cp in copies:
            cp.start()
        for cp in copies:
            cp.wait_recv()
        for cp in copies:
            cp.wait_send()

    hbm = BS(memory_space=pl.ANY)
    return pl.pallas_call(
        body, name=name, out_shape=[SDS((n, *b.shape[(2 if paired else 1):]), b.dtype) for b in bufs],
        in_specs=[hbm] * na, out_specs=[hbm] * na,
        scratch_shapes=[pltpu.SemaphoreType.DMA((n * na,)), pltpu.SemaphoreType.DMA((n * na,))],
    )(*bufs)


def _rs_tile(r):
    return _pick(r, (704, 512, 352, 256, 192, 128, 64, 32, 16, 8))


def _rs_add_c(g4, recv, coords, name):
    _, _, r, c = g4.shape
    tr = _rs_tile(r)

    def body(co_ref, a0_ref, a1_ref, b0_ref, b1_ref, keep_ref, send_ref):
        s0 = a0_ref[...] + b0_ref[...]
        s1 = a1_ref[...] + b1_ref[...]
        mine = co_ref[1] == 0
        keep_ref[...] = jnp.where(mine, s0, s1)
        send_ref[...] = jnp.where(mine, s1, s0).astype(BF16)

    blk = lambda yy: BS((None, None, tr, c), functools.partial(lambda i, j, co, yy: (2 * i + yy, co[2], j, 0), yy=yy))
    rblk = lambda yy: BS((None, tr, c), functools.partial(lambda i, j, co, yy: (2 * i + yy, j, 0), yy=yy))
    out = BS((None, tr, c), lambda i, j, co: (i, j, 0))
    return pl.pallas_call(
        body, name=name, out_shape=[SDS((2, r, c), F32), SDS((2, r, c), BF16)],
        grid_spec=pltpu.PrefetchScalarGridSpec(num_scalar_prefetch=1, grid=(2, r // tr),
                                               in_specs=[blk(0), blk(1), rblk(0), rblk(1)], out_specs=[out, out]),
        compiler_params=_params(2),
    )(coords, g4, g4, recv, recv)


def _rs_add_y(kept, recv, coords, name):
    _, r, c = kept.shape
    tr = _rs_tile(r)

    def body(co_ref, a_ref, b_ref, keep_ref, send_ref):
        s0 = a_ref[0] + b_ref[0].astype(F32)
        s1 = a_ref[1] + b_ref[1].astype(F32)
        mine = co_ref[0] == 0
        keep_ref[...] = jnp.where(mine, s0, s1)
        send_ref[0] = jnp.where(mine, s1, s0).astype(BF16)

    blk = BS((2, tr, c), lambda j, co: (0, j, 0))
    return pl.pallas_call(
        body, name=name, out_shape=[SDS((r, c), F32), SDS((1, r, c), BF16)],
        grid_spec=pltpu.PrefetchScalarGridSpec(num_scalar_prefetch=1, grid=(r // tr,), in_specs=[blk, blk],
                                               out_specs=[BS((tr, c), lambda j, co: (j, 0)),
                                                          BS((1, tr, c), lambda j, co: (0, j, 0))]),
        compiler_params=_params(1),
    )(coords, kept, recv)


def _rs_add_x(kept, recv, name):
    r, c = kept.shape
    tr = _rs_tile(r)

    def body(a_ref, b_ref, o_ref):
        o_ref[...] = a_ref[...] + b_ref[0].astype(F32)

    return pl.pallas_call(
        body, name=name, grid=(r // tr,), out_shape=SDS((r, c), F32),
        in_specs=[BS((tr, c), lambda j: (j, 0)), BS((1, tr, c), lambda j: (0, j, 0))],
        out_specs=BS((tr, c), lambda j: (j, 0)), compiler_params=_params(1),
    )(kept, recv)


def _reduce_scatter(arrays, coords, tag):
    ids = range(len(arrays))
    g4 = [a.reshape(4, 2, *a.shape[1:]) for a in arrays]
    got = _exchange(g4, 2, True, f"rs_c_{tag}")
    kept, send = zip(*[_rs_add_c(g4[i], got[i], coords, f"rs_c_add_{tag}{i}") for i in ids])
    got = _exchange(list(send), 1, False, f"rs_y_{tag}")
    kept, send = zip(*[_rs_add_y(kept[i], got[i], coords, f"rs_y_add_{tag}{i}") for i in ids])
    got = _exchange(list(send), 0, False, f"rs_x_{tag}")
    return [_rs_add_x(kept[i], got[i], f"rs_x_add_{tag}{i}") for i in ids]


def _proj_fwd(h, nw8, wp, tag):
    n = h.shape[0]
    tm = _pick(n, (1408, 768, 512, 384, 256, 192, 128, 64))
    tn = 896

    def body(h_ref, nw_ref, w_ref, proj_ref, xn_ref):
        @pl.when(pl.program_id(1) == 0)
        def _():
            xn_ref[...] = _rms(h_ref[...], nw_ref[0:1, :]).astype(BF16)

        proj_ref[...] = jnp.dot(xn_ref[...], w_ref[...], preferred_element_type=F32)

    return pl.pallas_call(
        body, name=f"proj_fwd_{tag}", grid=(n // tm, PROJ_W // tn),
        in_specs=[BS((tm, D_MODEL), lambda i, j: (i, 0)), BS((SUBLANES, D_MODEL), lambda i, j: (0, 0)),
                  BS((D_MODEL, tn), lambda i, j: (0, j))],
        out_specs=[BS((tm, tn), lambda i, j: (i, j)), BS((tm, D_MODEL), lambda i, j: (i, 0))],
        out_shape=[SDS((n, PROJ_W), F32), SDS((n, D_MODEL), BF16)], compiler_params=_params(2),
    )(h, nw8, wp)


def _conv_ext(x_ext, cw_ref):
    y = x_ext * cw_ref[3:4, :]
    for k in range(3):
        y = y + _shift_down(x_ext, 3 - k) * cw_ref[k:k + 1, :]
    return y[SUBLANES:]


def _prep_fwd(proj, cw8, aux, lb8, nseq, t_len, tag):
    n = proj.shape[0]
    tt = _pick(t_len, (192, 128, 64))
    nt_ = t_len // tt
    qkv_w = 3 * HEADS_W

    def body(cur_ref, prev_ref, misc_ref, bq_ref, bf_ref, cw_ref, aux_ref, lb_ref,
             q_ref, k_ref, v_ref, b_ref, g_ref, qb_ref, kb_ref, lf_ref):
        t = pl.program_id(1)
        prev = jnp.where(t == 0, 0.0, prev_ref[...])
        y = _conv_ext(jnp.concatenate([prev, cur_ref[...]], axis=0), cw_ref)
        real = (t * tt + _iota2((tt, 1), 0)) >= N_PAD
        misc = misc_ref[...]
        auxv = aux_ref[...]
        for hd in range(N_HEADS):
            sl = slice(hd * D_HEAD, (hd + 1) * D_HEAD)
            q_ref[:, sl] = _l2n_act(y[:, sl], D_HEAD ** -0.5)
            k_ref[:, sl] = _l2n_act(y[:, HEADS_W + hd * D_HEAD:HEADS_W + (hd + 1) * D_HEAD], 1.0)
            v_ref[:, sl] = _silu(y[:, 2 * HEADS_W + hd * D_HEAD:2 * HEADS_W + (hd + 1) * D_HEAD])
            b_ref[:, sl], g_ref[:, sl] = _gdn_gates(misc, auxv, real, hd)
        qb_ref[...], kb_ref[...], lf_ref[...] = _hgrn_prep(bq_ref[...], bf_ref[...], lb_ref[0:1, :], real)

    rb = tt // SUBLANES
    row = lambda s, t: s * nt_ + t
    wide = BS((tt, HEADS_W), lambda s, t: (row(s, t), 0))
    return pl.pallas_call(
        body, name=f"prep_fwd_{tag}", grid=(nseq, nt_),
        in_specs=[BS((tt, qkv_w), lambda s, t: (row(s, t), 0)),
                  BS((SUBLANES, qkv_w), lambda s, t: (jnp.maximum(row(s, t) * rb - 1, 0), 0)),
                  BS((tt, LANES), lambda s, t: (row(s, t), C_MISC // LANES)),
                  BS((tt, HEADS_W), lambda s, t: (row(s, t), C_BQ // HEADS_W)),
                  BS((tt, HEADS_W), lambda s, t: (row(s, t), C_BF // HEADS_W)),
                  BS((SUBLANES, qkv_w), lambda s, t: (0, 0)), BS((SUBLANES, LANES), lambda s, t: (0, 0)),
                  BS((SUBLANES, HEADS_W), lambda s, t: (0, 0))],
        out_specs=[wide] * 8, out_shape=[SDS((n, HEADS_W), F32)] * 8, compiler_params=_params(2),
    )(proj, proj, proj, proj, proj, cw8, aux, lb8)


GDN_SEQS = 4
HGRN_SEQS = 2


def _seq_block(nseq, most):
    return max(s for s in (1, 2, 4) if s <= most and nseq % s == 0)


def _to_chains(x):
    return jnp.concatenate([x[:, :, hd * D_HEAD:(hd + 1) * D_HEAD] for hd in range(N_HEADS)], axis=0)


def _from_chains(ref, rows, val):
    sb = val.shape[0] // N_HEADS
    for hd in range(N_HEADS):
        ref[:, rows, hd * D_HEAD:(hd + 1) * D_HEAD] = val[hd * sb:(hd + 1) * sb].astype(ref.dtype)


def _scan_call(body, name, arrays, col_blocks, reverse, nseq, t_len, most_seqs, extra_in, outs):
    sb = _seq_block(nseq, most_seqs)
    nc = t_len // GDN_CHUNK
    chains = N_HEADS * sb
    cidx = (lambda c: nc - 1 - c) if reverse else (lambda c: c)
    ck_shape = (nseq // sb, nc, chains, D_HEAD, D_HEAD)
    ck_spec = BS((None, None, chains, D_HEAD, D_HEAD), lambda p, c: (p, cidx(c), 0, 0, 0))
    in_specs = [BS((sb, GDN_CHUNK, HEADS_W), functools.partial(lambda p, c, cb: (p, cidx(c), cb), cb=cb))
                for cb in col_blocks]
    args = [a.reshape(nseq, t_len, a.shape[1]) for a in arrays]
    if extra_in is not None:
        in_specs.append(ck_spec)
        args.append(extra_in)
    out_specs, out_shape = [], []
    for o in outs:
        if o == "ckpt":
            out_specs.append(ck_spec)
            out_shape.append(SDS(ck_shape, F32))
        else:
            out_specs.append(BS((sb, GDN_CHUNK, HEADS_W), lambda p, c: (p, cidx(c), 0)))
            out_shape.append(SDS((nseq, t_len, HEADS_W), o))
    res = pl.pallas_call(
        body, name=name, grid=(nseq // sb, nc), in_specs=in_specs, out_specs=out_specs, out_shape=out_shape,
        scratch_shapes=[pltpu.VMEM((chains, D_HEAD, D_HEAD), F32)], compiler_params=_params(2),
    )(*args)
    return [r if o == "ckpt" else r.reshape(nseq * t_len, HEADS_W) for r, o in zip(res, outs)]


def _gdn_fwd(q, k, v, b, g, nseq, t_len, tag):
    def body(q_ref, k_ref, v_ref, b_ref, g_ref, o_ref, sck_ref, s_ref):
        @pl.when(pl.program_id(1) == 0)
        def _():
            s_ref[...] = jnp.zeros_like(s_ref)

        s = s_ref[...]
        sck_ref[...] = s
        o, s_new = _gdn_chunk(*[_to_chains(r[...]) for r in (q_ref, k_ref, v_ref, b_ref, g_ref)], s)
        _from_chains(o_ref, slice(None), o)
        s_ref[...] = s_new

    return _scan_call(body, f"gdn_fwd_{tag}", [q, k, v, b, g], [0] * 5, False, nseq, t_len, GDN_SEQS, None,
                      [F32, "ckpt"])


def _gdn_bwd(q, k, v, b, g, sck, do, nseq, t_len, tag):
    def body(q_ref, k_ref, v_ref, b_ref, g_ref, do_ref, sck_ref, dq_ref, dk_ref, dv_ref, db_ref, dg_ref, ds_ref):
        @pl.when(pl.program_id(1) == 0)
        def _():
            ds_ref[...] = jnp.zeros_like(ds_ref)

        _, vjp = jax.vjp(_gdn_chunk, *[_to_chains(r[...]) for r in (q_ref, k_ref, v_ref, b_ref, g_ref)], sck_ref[...])
        grads = vjp((_to_chains(do_ref[...]), ds_ref[...]))
        for ref, val in zip((dq_ref, dk_ref, dv_ref, db_ref, dg_ref), grads[:5]):
            _from_chains(ref, slice(None), val)
        ds_ref[...] = grads[5]

    return _scan_call(body, f"gdn_bwd_{tag}", [q, k, v, b, g, do], [0] * 6, True, nseq, t_len, GDN_SEQS, sck,
                      [F32] * 5)


def _hgrn_fwd(q, k, v, v_col, lf, nseq, t_len, tag):
    def body(q_ref, k_ref, v_ref, lf_ref, o_ref, sck_ref, s_ref):
        @pl.when(pl.program_id(1) == 0)
        def _():
            s_ref[...] = jnp.zeros_like(s_ref)

        s = s_ref[...]
        sck_ref[...] = s
        o, s_new = _hgrn_block(*[_to_chains(r[...]) for r in (q_ref, k_ref, v_ref, lf_ref)], s)
        _from_chains(o_ref, slice(None), o)
        s_ref[...] = s_new

    return _scan_call(body, f"hgrn_fwd_{tag}", [q, k, v, lf], [0, 0, v_col, 0], False, nseq, t_len, HGRN_SEQS, None,
                      [F32, "ckpt"])


def _hgrn_bwd(q, k, v, v_col, lf, sck, do, nseq, t_len, tag):
    def body(q_ref, k_ref, v_ref, lf_ref, do_ref, sck_ref, dq_ref, dk_ref, dv_ref, dlf_ref, ds_ref):
        @pl.when(pl.program_id(1) == 0)
        def _():
            ds_ref[...] = jnp.zeros_like(ds_ref)

        _, vjp = jax.vjp(_hgrn_block, *[_to_chains(r[...]) for r in (q_ref, k_ref, v_ref, lf_ref)], sck_ref[...])
        grads = vjp((_to_chains(do_ref[...]), ds_ref[...]))
        for ref, val in zip((dq_ref, dk_ref, dv_ref, dlf_ref), grads[:4]):
            _from_chains(ref, slice(None), val)
        ds_ref[...] = grads[4]

    return _scan_call(body, f"hgrn_bwd_{tag}", [q, k, v, lf, do], [0, 0, v_col, 0, 0], True, nseq, t_len, HGRN_SEQS, sck,
                      [F32, F32, BF16, F32])


def _post_values(oa_ref, ob_ref, z_ref, bg_ref, ga_ref, gb_ref, gn_ref, wa_ref, wb_ref, ya_ref, yb_ref):
    for hd in range(N_HEADS):
        sl = slice(hd * D_HEAD, (hd + 1) * D_HEAD)
        ya_ref[:, sl] = _gated_norm(oa_ref[:, sl], z_ref[:, sl], gn_ref[0:1, :]).astype(BF16)
        yb_ref[:, sl] = _gated_norm(ob_ref[:, sl], bg_ref[:, sl], gn_ref[1:2, :]).astype(BF16)
    pa = jnp.dot(ya_ref[...], wa_ref[...], preferred_element_type=F32)
    pb = jnp.dot(yb_ref[...], wb_ref[...], preferred_element_type=F32)
    return pa, pb, _sigmoid(ga_ref[...]), _sigmoid(gb_ref[...])


def _post_specs(tm):
    r2 = lambda i: (i, 0)
    return [BS((tm, HEADS_W), r2), BS((tm, HEADS_W), r2),
            BS((tm, HEADS_W), lambda i: (i, C_Z // HEADS_W)), BS((tm, HEADS_W), lambda i: (i, C_BG // HEADS_W)),
            BS((tm, D_MODEL), lambda i: (i, C_GA // D_MODEL)), BS((tm, D_MODEL), lambda i: (i, C_GB // D_MODEL)),
            BS((tm, D_MODEL), r2), BS((SUBLANES, LANES), lambda i: (0, 0))]


def _post_fwd(oa, ob, proj, h, gn8, wa, wb, wout, tag):
    n = h.shape[0]
    tm = _pick(n, (256, 192, 128, 64))

    def body(oa_ref, ob_ref, z_ref, bg_ref, ga_ref, gb_ref, h_ref, gn_ref, wa_ref, wb_ref, wout_ref, out_ref,
             ya_ref, yb_ref):
        pa, pb, sa, sb = _post_values(oa_ref, ob_ref, z_ref, bg_ref, ga_ref, gb_ref, gn_ref, wa_ref, wb_ref,
                                      ya_ref, yb_ref)
        mixed = (sa * pa + sb * pb).astype(BF16)
        out_ref[...] = h_ref[...] + jnp.dot(mixed, wout_ref[...], preferred_element_type=F32)

    full = lambda i: (0, 0)
    return pl.pallas_call(
        body, name=f"post_fwd_{tag}", grid=(n // tm,),
        in_specs=_post_specs(tm) + [BS((HEADS_W, D_MODEL), full), BS((HEADS_W, D_MODEL), full),
                                    BS((D_MODEL, D_MODEL), full)],
        out_specs=BS((tm, D_MODEL), lambda i: (i, 0)), out_shape=SDS((n, D_MODEL), F32),
        scratch_shapes=[pltpu.VMEM((tm, HEADS_W), BF16), pltpu.VMEM((tm, HEADS_W), BF16)], compiler_params=_params(1),
    )(oa, ob, proj, proj, proj, proj, h, gn8, wa, wb, wout)


def _post_bwd(dh, oa, ob, proj, h, gn8, wa, wb, wa_t, wb_t, wout_t, tag):
    n = h.shape[0]
    tm = _pick(n, (256, 192, 128, 64))

    def body(dh_ref, oa_ref, ob_ref, z_ref, bg_ref, ga_ref, gb_ref, h_ref, gn_ref, wa_ref, wb_ref, wat_ref, wbt_ref,
             woutt_ref, doa_ref, dob_ref, dz_ref, dbg_ref, dga_ref, dgb_ref, dwa_ref, dwb_ref, dwout_ref, dgn_ref,
             ya_ref, yb_ref):
        @pl.when(pl.program_id(0) == 0)
        def _():
            dwa_ref[...] = jnp.zeros_like(dwa_ref)
            dwb_ref[...] = jnp.zeros_like(dwb_ref)
            dwout_ref[...] = jnp.zeros_like(dwout_ref)
            dgn_ref[...] = jnp.zeros_like(dgn_ref)

        pa, pb, sa, sb = _post_values(oa_ref, ob_ref, z_ref, bg_ref, ga_ref, gb_ref, gn_ref, wa_ref, wb_ref,
                                      ya_ref, yb_ref)
        mixed = (sa * pa + sb * pb).astype(BF16)
        dout = dh_ref[...].astype(BF16)
        dwout_ref[...] += _dg(mixed, dout, ((0,), (0,)))
        dmixed = jnp.dot(dout, woutt_ref[...], preferred_element_type=F32)
        dga_ref[...] = (dmixed * pa * sa * (1.0 - sa)).astype(BF16)
        dgb_ref[...] = (dmixed * pb * sb * (1.0 - sb)).astype(BF16)
        dpa = (dmixed * sa).astype(BF16)
        dpb = (dmixed * sb).astype(BF16)
        dwa_ref[...] += _dg(ya_ref[...], dpa, ((0,), (0,)))
        dwb_ref[...] += _dg(yb_ref[...], dpb, ((0,), (0,)))
        dya = jnp.dot(dpa, wat_ref[...], preferred_element_type=F32)
        dyb = jnp.dot(dpb, wbt_ref[...], preferred_element_type=F32)
        dgn_a = jnp.zeros((1, D_HEAD), F32)
        dgn_b = jnp.zeros((1, D_HEAD), F32)
        for hd in range(N_HEADS):
            sl = slice(hd * D_HEAD, (hd + 1) * D_HEAD)
            _, vjp = jax.vjp(_gated_norm, oa_ref[:, sl], z_ref[:, sl], gn_ref[0:1, :])
            doa, dz, dgw = vjp(dya[:, sl])
            doa_ref[:, sl], dz_ref[:, sl], dgn_a = doa, dz.astype(BF16), dgn_a + dgw
            _, vjp = jax.vjp(_gated_norm, ob_ref[:, sl], bg_ref[:, sl], gn_ref[1:2, :])
            dob, dbg, dgw = vjp(dyb[:, sl])
            dob_ref[:, sl], dbg_ref[:, sl], dgn_b = dob, dbg.astype(BF16), dgn_b + dgw
        dgn_ref[0:1, :] += dgn_a
        dgn_ref[1:2, :] += dgn_b

    full = lambda i: (0, 0)
    r2 = lambda i: (i, 0)
    return pl.pallas_call(
        body, name=f"post_bwd_{tag}", grid=(n // tm,),
        in_specs=[BS((tm, D_MODEL), r2)] + _post_specs(tm) + [
            BS((HEADS_W, D_MODEL), full), BS((HEADS_W, D_MODEL), full), BS((D_MODEL, HEADS_W), full),
            BS((D_MODEL, HEADS_W), full), BS((D_MODEL, D_MODEL), full)],
        out_specs=[BS((tm, HEADS_W), r2)] * 4 + [BS((tm, D_MODEL), r2)] * 2 + [
            BS((HEADS_W, D_MODEL), full), BS((HEADS_W, D_MODEL), full), BS((D_MODEL, D_MODEL), full),
            BS((SUBLANES, LANES), full)],
        out_shape=[SDS((n, HEADS_W), F32), SDS((n, HEADS_W), F32), SDS((n, HEADS_W), BF16), SDS((n, HEADS_W), BF16),
                   SDS((n, D_MODEL), BF16), SDS((n, D_MODEL), BF16), SDS((HEADS_W, D_MODEL), F32),
                   SDS((HEADS_W, D_MODEL), F32), SDS((D_MODEL, D_MODEL), F32), SDS((SUBLANES, LANES), F32)],
        scratch_shapes=[pltpu.VMEM((tm, HEADS_W), BF16), pltpu.VMEM((tm, HEADS_W), BF16)], compiler_params=_params(1),
    )(dh, oa, ob, proj, proj, proj, proj, h, gn8, wa, wb, wa_t, wb_t, wout_t)


def _loss_head(h, fw8, target, nseq, t_len):
    n = h.shape[0]
    nc = t_len // GDN_CHUNK
    sub = 3 if nc % 3 == 0 else 1
    tl, nt = sub * GDN_CHUNK, nc // sub
    inv_d = 1.0 / D_MODEL

    def body(h_ref, fw_ref, *rest):
        tgt_refs, (dh_ref, acc_ref) = rest[:sub], rest[sub:]

        @pl.when((pl.program_id(0) == 0) & (pl.program_id(1) == 0))
        def _():
            acc_ref[...] = jnp.zeros_like(acc_ref)

        frames = ((pl.program_id(1) * tl + _iota2((tl, 1), 0)) >= N_PAD + N_META).astype(F32)
        y, vjp = jax.vjp(_rms, h_ref[...], fw_ref[0:1, :])
        err = (y - jnp.concatenate([r[...] for r in tgt_refs], axis=0)) * frames
        dx, dfw = vjp(err * inv_d)
        dh_ref[...] = dx
        acc_ref[0:1, :] += dfw
        acc_ref[1:2, :] += (0.5 * inv_d) * jnp.sum(err * err, axis=0, keepdims=True)

    tgt_spec = lambda u: BS((None, GDN_CHUNK, D_MODEL), lambda s, t: (s, jnp.maximum(t * sub + u - 1, 0), 0))
    return pl.pallas_call(
        body, name="loss_head", grid=(nseq, nt),
        in_specs=[BS((tl, D_MODEL), lambda s, t: (s * nt + t, 0)), BS((SUBLANES, D_MODEL), lambda s, t: (0, 0))]
        + [tgt_spec(u) for u in range(sub)],
        out_specs=[BS((tl, D_MODEL), lambda s, t: (s * nt + t, 0)), BS((SUBLANES, D_MODEL), lambda s, t: (0, 0))],
        out_shape=[SDS((n, D_MODEL), F32), SDS((SUBLANES, D_MODEL), F32)], compiler_params=_params(2),
    )(h, fw8, *[target] * sub)


def _prep_bwd(proj, dq, dk, dv, db, dg, dqb, dkb, dlf, cw8, aux, lb8, nseq, t_len, tag):
    n = proj.shape[0]
    tt = _pick(t_len, (192, 128, 64))
    nt_ = t_len // tt
    qkv_w = 3 * HEADS_W
    rb = tt // SUBLANES
    ext = tt + SUBLANES

    def body(cur_ref, prev_ref, next_ref, misc_ref, bq_ref, bf_ref, dq_ref, dqn_ref, dk_ref, dkn_ref, dv_ref, dvn_ref,
             db_ref, dg_ref, dqb_ref, dkb_ref, dlf_ref, cw_ref, aux_ref, lb_ref,
             dqkv_ref, dmisc_ref, dbq_ref, dbf_ref, dcw_ref, daux_ref, dlb_ref, dy_ref):
        s, t = pl.program_id(0), pl.program_id(1)

        @pl.when((s == 0) & (t == 0))
        def _():
            dcw_ref[...] = jnp.zeros_like(dcw_ref)
            daux_ref[...] = jnp.zeros_like(daux_ref)
            dlb_ref[...] = jnp.zeros_like(dlb_ref)

        prev = jnp.where(t == 0, 0.0, prev_ref[...])
        x_ext = jnp.concatenate([prev, cur_ref[...], next_ref[...]], axis=0)
        y = _conv_ext(x_ext, cw_ref)
        inside = (t < nt_ - 1) | (_iota2((ext, 1), 0) < tt)
        dy_ref[0:SUBLANES, :] = jnp.zeros((SUBLANES, qkv_w), F32)
        for hd in range(N_HEADS):
            for grp, (g_ref, gn_ref, scale) in enumerate(((dq_ref, dqn_ref, D_HEAD ** -0.5), (dk_ref, dkn_ref, 1.0),
                                                          (dv_ref, dvn_ref, None))):
                lo = grp * HEADS_W + hd * D_HEAD
                sl = slice(hd * D_HEAD, (hd + 1) * D_HEAD)
                cot = jnp.concatenate([g_ref[:, sl], gn_ref[:, sl]], axis=0)
                fn = _silu if scale is None else functools.partial(_l2n_act, scale=scale)
                _, vjp = jax.vjp(fn, y[:, lo:lo + D_HEAD])
                dy_ref[SUBLANES:, lo:lo + D_HEAD] = jnp.where(inside, vjp(cot)[0], 0.0)
        dy_ext = dy_ref[...]
        dx = dy_ext * cw_ref[3:4, :]
        for kk in range(3):
            dx = dx + _shift_up(dy_ext, 3 - kk) * cw_ref[kk:kk + 1, :]
        dqkv_ref[...] = dx[SUBLANES:SUBLANES + tt].astype(BF16)
        dy_cur = dy_ext[SUBLANES:SUBLANES + tt]
        for kk in range(4):
            xs = _shift_down(x_ext, 3 - kk)[SUBLANES:SUBLANES + tt]
            dcw_ref[kk:kk + 1, :] += jnp.sum(xs * dy_cur, axis=0, keepdims=True)

        real = (t * tt + _iota2((tt, 1), 0)) >= N_PAD
        dmisc = jnp.zeros((tt, LANES), F32)
        daux = jnp.zeros((SUBLANES, LANES), F32)
        for hd in range(N_HEADS):
            sl = slice(hd * D_HEAD, (hd + 1) * D_HEAD)
            _, vjp = jax.vjp(lambda m, a: _gdn_gates(m, a, real, hd), misc_ref[...], aux_ref[...])
            dm, da = vjp((db_ref[:, sl], dg_ref[:, sl]))
            dmisc, daux = dmisc + dm, daux + da
        dmisc_ref[...] = dmisc.astype(BF16)
        daux_ref[...] += daux
        _, vjp = jax.vjp(lambda a, b, c: _hgrn_prep(a, b, c, real), bq_ref[...], bf_ref[...], lb_ref[0:1, :])
        dbq, dbf, dlb = vjp((dqb_ref[...], dkb_ref[...], dlf_ref[...]))
        dbq_ref[...], dbf_ref[...] = dbq.astype(BF16), dbf.astype(BF16)
        dlb_ref[0:1, :] += dlb

    row = lambda s, t: s * nt_ + t
    cur = lambda s, t: (row(s, t), 0)
    nxt = lambda s, t: (jnp.minimum((row(s, t) + 1) * rb, n // SUBLANES - 1), 0)
    wide = BS((tt, HEADS_W), cur)
    halo = BS((SUBLANES, HEADS_W), nxt)
    full = lambda s, t: (0, 0)
    return pl.pallas_call(
        body, name=f"prep_bwd_{tag}", grid=(nseq, nt_),
        in_specs=[BS((tt, qkv_w), cur), BS((SUBLANES, qkv_w), lambda s, t: (jnp.maximum(row(s, t) * rb - 1, 0), 0)),
                  BS((SUBLANES, qkv_w), nxt), BS((tt, LANES), lambda s, t: (row(s, t), C_MISC // LANES)),
                  BS((tt, HEADS_W), lambda s, t: (row(s, t), C_BQ // HEADS_W)),
                  BS((tt, HEADS_W), lambda s, t: (row(s, t), C_BF // HEADS_W)),
                  wide, halo, wide, halo, wide, halo, wide, wide, wide, wide, wide,
                  BS((SUBLANES, qkv_w), full), BS((SUBLANES, LANES), full), BS((SUBLANES, HEADS_W), full)],
        out_specs=[BS((tt, qkv_w), cur), BS((tt, LANES), cur), wide, wide,
                   BS((SUBLANES, qkv_w), full), BS((SUBLANES, LANES), full), BS((SUBLANES, HEADS_W), full)],
        out_shape=[SDS((n, qkv_w), BF16), SDS((n, LANES), BF16), SDS((n, HEADS_W), BF16), SDS((n, HEADS_W), BF16),
                   SDS((SUBLANES, qkv_w), F32), SDS((SUBLANES, LANES), F32), SDS((SUBLANES, HEADS_W), F32)],
        scratch_shapes=[pltpu.VMEM((tt + 2 * SUBLANES, qkv_w), F32)], compiler_params=_params(2),
    )(proj, proj, proj, proj, proj, proj, dq, dq, dk, dk, dv, dv, db, dg, dqb, dkb, dlf, cw8, aux, lb8)


def _proj_bwd_x(pieces, wp_t, h, nw8, dh_res, tag):
    n = h.shape[0]
    tm = _pick(n, (256, 192, 128, 64))
    widths = [p.shape[1] for p in pieces]
    assert sum(widths) == PROJ_W

    def body(*refs):
        p_refs = refs[:len(pieces)]
        wt_ref, h_ref, nw_ref, dres_ref, dh_ref, dnw_ref = refs[len(pieces):]

        @pl.when(pl.program_id(0) == 0)
        def _():
            dnw_ref[...] = jnp.zeros_like(dnw_ref)

        dxn, off = None, 0
        for p_ref, w in zip(p_refs, widths):
            part = jnp.dot(p_ref[...], wt_ref[off:off + w, :], preferred_element_type=F32)
            dxn = part if dxn is None else dxn + part
            off += w
        _, vjp = jax.vjp(_rms, h_ref[...], nw_ref[0:1, :])
        dx, dnw = vjp(dxn)
        dh_ref[...] = dres_ref[...] + dx
        dnw_ref[0:1, :] += dnw

    r2 = lambda i: (i, 0)
    full = lambda i: (0, 0)
    return pl.pallas_call(
        body, name=f"proj_bwd_x_{tag}", grid=(n // tm,),
        in_specs=[BS((tm, w), r2) for w in widths] + [BS((PROJ_W, D_MODEL), full), BS((tm, D_MODEL), r2),
                                                      BS((SUBLANES, D_MODEL), full), BS((tm, D_MODEL), r2)],
        out_specs=[BS((tm, D_MODEL), r2), BS((SUBLANES, D_MODEL), full)],
        out_shape=[SDS((n, D_MODEL), F32), SDS((SUBLANES, D_MODEL), F32)], compiler_params=_params(1),
    )(*pieces, wp_t, h, nw8, dh_res)


def _proj_bwd_w(xn, pieces, tag):
    n = xn.shape[0]
    tm = _pick(n, (384, 256, 192, 128, 64))
    widths = [p.shape[1] for p in pieces]
    assert sum(widths) == PROJ_W

    def body(*refs):
        x_ref, p_refs = refs[0], refs[1:1 + len(pieces)]
        o_ref, acc_ref = refs[1 + len(pieces):]

        @pl.when(pl.program_id(0) == 0)
        def _():
            acc_ref[...] = jnp.zeros_like(acc_ref)

        off = 0
        for p_ref, w in zip(p_refs, widths):
            acc_ref[:, off:off + w] += _dg(x_ref[...], p_ref[...], ((0,), (0,)))
            off += w

        @pl.when(pl.program_id(0) == pl.num_programs(0) - 1)
        def _():
            pltpu.sync_copy(acc_ref, o_ref)

    r2 = lambda i: (i, 0)
    return pl.pallas_call(
        body, name=f"proj_bwd_w_{tag}", grid=(n // tm,),
        in_specs=[BS((tm, D_MODEL), r2)] + [BS((tm, w), r2) for w in widths], out_specs=BS(memory_space=pl.ANY),
        out_shape=SDS((D_MODEL, PROJ_W), F32), scratch_shapes=[pltpu.VMEM((D_MODEL, PROJ_W), F32)],
        compiler_params=_params(1),
    )(xn, *pieces)


def _adamw(w, g, m, v, name):
    rows, cols = w.shape
    tr = _pick(rows, (256, 128, 64, 32, 16, 8, 4, 2, 1)) if rows > 256 else rows

    def body(w_ref, g_ref, m_ref, v_ref, d_ref, nm_ref, nv_ref):
        gr = g_ref[...]
        m_new = ADAM_B1 * m_ref[...] + (1.0 - ADAM_B1) * gr
        v_new = ADAM_B2 * v_ref[...] + (1.0 - ADAM_B2) * jnp.square(gr)
        m_hat = m_new / (1.0 - ADAM_B1 ** ADAM_STEP)
        v_hat = v_new / (1.0 - ADAM_B2 ** ADAM_STEP)
        d_ref[...] = -ADAM_LR * (m_hat / (jnp.sqrt(v_hat) + ADAM_EPS) + ADAM_WD * w_ref[...])
        nm_ref[...] = m_new
        nv_ref[...] = v_new

    blk = BS((tr, cols), lambda i: (i, 0))
    return pl.pallas_call(
        body, name=name, grid=(rows // tr,), in_specs=[blk] * 4, out_specs=[blk] * 3,
        out_shape=[SDS((rows, cols), F32)] * 3, compiler_params=_params(1),
    )(w, g, m, v)


def _row8(v, width):
    v = jnp.atleast_2d(v).astype(F32)
    return jnp.pad(v, ((0, SUBLANES - v.shape[0]), (0, width - v.shape[1])))


REF_MISC = 1536
N_MISC = 2 * N_HEADS
LAYOUT_RUNS = ((0, REF_MISC, 0), (REF_MISC + N_MISC, REF_W, REF_MISC), (REF_MISC, REF_MISC + N_MISC, C_MISC))


def _to_layout(w_full):
    runs = [w_full[:, lo:hi] for lo, hi, _ in sorted(LAYOUT_RUNS, key=lambda run: run[2])]
    return jnp.concatenate(runs + [jnp.zeros((w_full.shape[0], PROJ_W - REF_W), w_full.dtype)], axis=1)


def _from_layout(dw, n_slabs):
    width = REF_W // n_slabs
    slabs = []
    for j in range(n_slabs):
        pieces = []
        for lo, hi, at in sorted(LAYOUT_RUNS):
            a, b = max(lo, j * width), min(hi, (j + 1) * width)
            if a < b:
                pieces.append(dw[:, at + a - lo:at + b - lo])
        slabs.append(jnp.concatenate(pieces, axis=1))
    return slabs


def _lower_bounds(lb):
    sm = jax.nn.softmax(lb.astype(F32), axis=0)
    return jnp.cumsum(sm, axis=0) - sm[0]


def kernel(x, meta_tokens, norm_w, w_in, conv_w, a_log, dt_bias, gnorm_a, gnorm_b, hgrn_lower_bounds, w_branch_a, w_branch_b, w_out, final_norm_w, loss_target, m_meta_tokens, m_norm_w, m_w_in, m_conv_w, m_a_log, m_dt_bias, m_gnorm_a, m_gnorm_b, m_hgrn_lower_bounds, m_w_branch_a, m_w_branch_b, m_w_out, m_final_norm_w, v_meta_tokens, v_norm_w, v_w_in, v_conv_w, v_a_log, v_dt_bias, v_gnorm_a, v_gnorm_b, v_hgrn_lower_bounds, v_w_branch_a, v_w_branch_b, v_w_out, v_final_norm_w):
    nseq, seq, _ = x.shape
    depth = norm_w.shape[0]
    t_len = N_PAD + N_META + seq
    n = nseq * t_len
    win_c, conv_c = w_in.shape[2], conv_w.shape[2]
    my = 4 * lax.axis_index("x") + 2 * lax.axis_index("y") + lax.axis_index("c")

    assert depth >= 2
    by_cols = lambda g: g.transpose(1, 2, 0, 3).reshape(g.shape[1], g.shape[2], N_DEV * g.shape[3])
    first = _all_gather_hbm([w_in[:1].astype(BF16), conv_w, meta_tokens], "gather_first")
    later_flight, later_token = _send_all_start(
        [w_in[1:].astype(BF16), w_branch_a.astype(BF16), w_branch_b.astype(BF16), w_out.astype(BF16)], False,
        "gather_later_start")
    w_in_full = [by_cols(first[0])]
    conv_full = by_cols(first[1])
    meta_full = first[2].transpose(1, 0, 2).reshape(N_META, D_MODEL)

    lb_all, lb_vjp = jax.vjp(_lower_bounds, hgrn_lower_bounds)

    h = jnp.concatenate([jnp.zeros((nseq, N_PAD, D_MODEL), F32),
                         jnp.broadcast_to(meta_full[None], (nseq, N_META, D_MODEL)), x], axis=1).reshape(n, D_MODEL)
    saved = []
    for l in range(depth):
        wp = _to_layout(w_in_full[0][0] if l == 0 else w_in_full[1][l - 1])
        nw8 = _row8(norm_w[l], D_MODEL)
        if l == 0:
            nw8 = nw8 + later_token[0:1, 0:1]
        cw8 = _row8(conv_full[l], 3 * HEADS_W)
        aux = _row8(jnp.stack([a_log[l], dt_bias[l]]), LANES)
        lb8 = _row8(lb_all[l], HEADS_W)
        gn8 = _row8(jnp.stack([gnorm_a[l], gnorm_b[l]]), LANES)
        proj, xn = _proj_fwd(h, nw8, wp, l)
        q, k, v, b, g, qb, kb, lf = _prep_fwd(proj, cw8, aux, lb8, nseq, t_len, l)
        oa, sck_a = _gdn_fwd(q, k, v, b, g, nseq, t_len, l)
        ob, sck_b = _hgrn_fwd(qb, kb, proj, C_BI // HEADS_W, lf, nseq, t_len, l)
        if l == 0:
            sent, landed = _send_all_wait(later_flight, ob, "gather_later_wait")
            landed = [lax.dynamic_update_slice(ld, own[None], (my,) + (0,) * own.ndim) for ld, own in zip(landed, sent)]
            w_in_full.append(by_cols(landed[0]))
            wa_full, wb_full = by_cols(landed[1]), by_cols(landed[2])
            wout_full = landed[3].transpose(1, 0, 2, 3).reshape(depth, D_MODEL, D_MODEL)
        wa_l, wb_l, wout_l = wa_full[l], wb_full[l], wout_full[l]
        h_next = _post_fwd(oa, ob, proj, h, gn8, wa_l, wb_l, wout_l, l)
        saved.append(dict(h=h, wp=wp, nw8=nw8, cw8=cw8, aux=aux, lb8=lb8, gn8=gn8, proj=proj, xn=xn, q=q, k=k, v=v, b=b,
                          wa=wa_l, wb=wb_l, wout=wout_l,
                          g=g, qb=qb, kb=kb, lf=lf, oa=oa, ob=ob, sck_a=sck_a, sck_b=sck_b))
        h = h_next

    dh, acc = _loss_head(h, _row8(final_norm_w, D_MODEL), loss_target, nseq, t_len)

    g_win, g_wa, g_wb, g_wout, g_conv, small = [], [], [], [], [], []

    def mixer_slabs(dwa_s, dwb_s, dwout_s):
        nl = len(dwa_s)
        rows = lambda a: jnp.stack(a).reshape(nl * HEADS_W, N_DEV, LANES).transpose(1, 0, 2)
        wout = jnp.stack(dwout_s).reshape(nl, N_DEV, LANES, D_MODEL).transpose(1, 0, 2, 3)
        return [jnp.concatenate([rows(dwa_s), rows(dwb_s)], axis=1).astype(BF16),
                wout.reshape(N_DEV, nl * LANES, D_MODEL).astype(BF16)]

    def win_slabs(per_layer, dtype):
        return jnp.stack([jnp.concatenate([sl[j] for sl in per_layer], axis=0) for j in range(N_DEV)]).astype(dtype)

    for l in reversed(range(depth)):
        s = saved[l]
        gn8, aux = s["gn8"], s["aux"]
        if l == 0:
            later_flight, later_token = _send_all_start(
                [win_slabs(g_win[::-1], BF16)] + mixer_slabs(g_wa[::-1], g_wb[::-1], g_wout[::-1]), True,
                "scatter_later_start")
            gn8 = gn8 + later_token[0:1, 0:1]
        doa, dob, dz, dbg, dga, dgb, dwa, dwb, dwout, dgn = _post_bwd(
            dh, s["oa"], s["ob"], s["proj"], s["h"], gn8, s["wa"], s["wb"], s["wa"].T, s["wb"].T, s["wout"].T, l)
        dq, dk, dv, db, dg = _gdn_bwd(s["q"], s["k"], s["v"], s["b"], s["g"], s["sck_a"], doa, nseq, t_len, l)
        dqb, dkb, dbi, dlf = _hgrn_bwd(s["qb"], s["kb"], s["proj"], C_BI // HEADS_W, s["lf"], s["sck_b"], dob, nseq,
                                       t_len, l)
        if l == 0:
            mixer_flight, mixer_token = _send_all_start(mixer_slabs([dwa], [dwb], [dwout]), True, "scatter_first_start")
            aux = aux + mixer_token[0:1, 0:1]
        dqkv, dmisc, dbq, dbf, dcw, daux, dlb = _prep_bwd(s["proj"], dq, dk, dv, db, dg, dqb, dkb, dlf, s["cw8"], aux,
                                                          s["lb8"], nseq, t_len, l)
        pieces = [dqkv, dz, dbq, dbf, dbi, dbg, dga, dgb, dmisc]
        dh, dnw = _proj_bwd_x(pieces, s["wp"].T, s["h"], s["nw8"], dh, l)
        dwp = _proj_bwd_w(s["xn"], pieces, l)
        g_win.append(_from_layout(dwp, N_DEV))
        g_wa.append(dwa)
        g_wb.append(dwb)
        g_wout.append(dwout)
        g_conv.append(dcw[:4])
        small.append((dnw[0], dgn[0], dgn[1], daux[0, :N_HEADS], daux[1, :N_HEADS], dlb[0]))
    g_conv.reverse()
    small.reverse()
    dh = dh.reshape(nseq, t_len, D_MODEL)
    grad_x = dh[:, N_PAD + N_META:]

    packed = jnp.concatenate([small[0][1], small[1][1], small[0][2], small[1][2], small[0][3], small[1][3],
                              small[0][4], small[1][4]])
    tile = jnp.concatenate([
        jnp.sum(dh[:, N_PAD:N_PAD + N_META], axis=0), _row8(jnp.stack([small[0][0], small[1][0], acc[0]]), D_MODEL),
        _row8(jnp.stack([small[0][5], small[1][5]]), D_MODEL), _row8(packed, D_MODEL), _row8(acc[1], D_MODEL)], axis=0)
    tile = _all_reduce_small(tile, "reduce_small")
    loss = jnp.sum(tile[40])
    g_meta = lax.dynamic_slice_in_dim(tile[0:N_META], my * LANES, LANES, axis=1)
    g_norm, g_final = tile[16:18], tile[18]
    (g_lb,) = lb_vjp(tile[24:26, :HEADS_W])
    r21 = tile[32]
    g_gna, g_gnb = r21[0:256].reshape(2, LANES), r21[256:512].reshape(2, LANES)
    g_alog, g_dtb = r21[512:520].reshape(2, N_HEADS), r21[520:528].reshape(2, N_HEADS)

    dconv = jnp.stack(g_conv)
    conv_slabs = dconv.reshape(depth * dconv.shape[1], N_DEV, conv_c).transpose(1, 0, 2)
    coords = jnp.stack([lax.axis_index("x"), lax.axis_index("y"), lax.axis_index("c")]).astype(jnp.int32)
    r_win, r_conv = _reduce_scatter([win_slabs(g_win[-1:], F32), conv_slabs], coords, "grads")

    def landed_sums(flight, tag):
        sent, landed = _send_all_wait(flight, dwp, f"{tag}_wait")
        landed = [lax.dynamic_update_slice(ld, lax.dynamic_index_in_dim(src, my, 0, keepdims=True), (my, 0, 0))
                  for ld, src in zip(landed, sent)]
        return [_sum_slabs(ld, f"{tag}_sum{i}") for i, ld in enumerate(landed)]

    l_win, l_ab, l_wout = landed_sums(later_flight, "scatter_later")
    r_ab, r_wout = landed_sums(mixer_flight, "scatter_first")
    both = lambda a, b, shape: jnp.concatenate([a.reshape(1, *shape[1:]), b.reshape(depth - 1, *shape[1:])])
    half, half_l = HEADS_W, (depth - 1) * HEADS_W
    mine = [both(r_win, l_win, w_in.shape), both(r_ab[:half], l_ab[:half_l], w_branch_a.shape),
            both(r_ab[half:], l_ab[half_l:], w_branch_b.shape), both(r_wout, l_wout, w_out.shape), r_conv]
    gseg = lambda i, shape: mine[i].reshape(shape)
    grads = {
        "meta_tokens": g_meta, "norm_w": g_norm, "w_in": gseg(0, w_in.shape), "conv_w": gseg(4, conv_w.shape),
        "a_log": g_alog, "dt_bias": g_dtb, "gnorm_a": g_gna, "gnorm_b": g_gnb, "hgrn_lower_bounds": g_lb,
        "w_branch_a": gseg(1, w_branch_a.shape), "w_branch_b": gseg(2, w_branch_b.shape), "w_out": gseg(3, w_out.shape),
        "final_norm_w": g_final}
    weights = {
        "meta_tokens": (meta_tokens, m_meta_tokens, v_meta_tokens), "norm_w": (norm_w, m_norm_w, v_norm_w),
        "w_in": (w_in, m_w_in, v_w_in), "conv_w": (conv_w, m_conv_w, v_conv_w), "a_log": (a_log, m_a_log, v_a_log),
        "dt_bias": (dt_bias, m_dt_bias, v_dt_bias), "gnorm_a": (gnorm_a, m_gnorm_a, v_gnorm_a),
        "gnorm_b": (gnorm_b, m_gnorm_b, v_gnorm_b),
        "hgrn_lower_bounds": (hgrn_lower_bounds, m_hgrn_lower_bounds, v_hgrn_lower_bounds),
        "w_branch_a": (w_branch_a, m_w_branch_a, v_w_branch_a), "w_branch_b": (w_branch_b, m_w_branch_b, v_w_branch_b),
        "w_out": (w_out, m_w_out, v_w_out), "final_norm_w": (final_norm_w, m_final_norm_w, v_final_norm_w)}
    names = list(weights)
    deltas, new_m, new_v = [], [], []
    for nm in names:
        w, m, v = weights[nm]
        view = (-1, w.shape[-1])
        d, m2, v2 = _adamw(w.reshape(view), grads[nm].reshape(view), m.reshape(view), v.reshape(view), f"adamw_{nm}")
        deltas.append(d.reshape(w.shape))
        new_m.append(m2.reshape(w.shape))
        new_v.append(v2.reshape(w.shape))
    return (loss, grad_x, *[grads[nm].reshape(weights[nm][0].shape) for nm in names], *deltas, *new_m, *new_v)
```

```python
import functools

import jax
import jax.numpy as jnp
import numpy as np
from jax import lax
from jax.experimental import pallas as pl
from jax.experimental.pallas import tpu as pltpu

F32 = jnp.float32
BF16 = jnp.bfloat16

D_MODEL = 1024
N_HEADS = 4
D_HEAD = 128
HEADS_W = N_HEADS * D_HEAD
N_META = 16
N_PAD = 48
GDN_CHUNK = 64
HGRN_CHUNK = 16
EPS = 1e-6
N_DEV = 8
LANES = 128
SUBLANES = 8
VMEM_LIMIT = 56 * 1024 * 1024

C_QKV, C_Z, C_BQ, C_BF, C_BI, C_BG, C_GA, C_GB, C_MISC = 0, 1536, 2048, 2560, 3072, 3584, 4096, 5120, 6144
PROJ_W = 6272
REF_W = 6152

ADAM_LR, ADAM_B1, ADAM_B2, ADAM_EPS, ADAM_WD, ADAM_STEP = 0.001, 0.9, 0.999, 1e-08, 0.01, 10

MESH = pl.DeviceIdType.MESH
SDS = jax.ShapeDtypeStruct
BS = pl.BlockSpec


def _params(n_axes):
    return pltpu.CompilerParams(dimension_semantics=("arbitrary",) * n_axes, vmem_limit_bytes=VMEM_LIMIT)


def _pick(n, cands):
    for c in cands:
        if n % c == 0:
            return c
    raise ValueError(f"no tile for {n} among {cands}")


def _iota2(shape, dim):
    return lax.broadcasted_iota(jnp.int32, shape, dim)


def _dg(a, b, dims):
    return lax.dot_general(a.astype(BF16), b.astype(BF16), (dims, ((), ())), preferred_element_type=F32)


def _bdg(a, b, ca, cb):
    return lax.dot_general(a.astype(BF16), b.astype(BF16), (((ca,), (cb,)), ((0,), (0,))), preferred_element_type=F32)


@jax.custom_vjp
def _bnn(a, b):
    return _bdg(a, b, 2, 1)


@jax.custom_vjp
def _bnt(a, b):
    return _bdg(a, b, 2, 2)


@jax.custom_vjp
def _btn(a, b):
    return _bdg(a, b, 1, 1)


_bnn.defvjp(lambda a, b: (_bnn(a, b), (a, b)), lambda r, g: (_bnt(g, r[1]), _btn(r[0], g)))
_bnt.defvjp(lambda a, b: (_bnt(a, b), (a, b)), lambda r, g: (_bnn(g, r[1]), _btn(g, r[0])))
_btn.defvjp(lambda a, b: (_btn(a, b), (a, b)), lambda r, g: (_bnt(r[1], g), _bnn(r[0], g)))


def _split2(x):
    hi = x.astype(BF16).astype(F32)
    return hi, x - hi


def _tri(bsz, n):
    return jnp.broadcast_to((_iota2((n, n), 0) >= _iota2((n, n), 1)).astype(F32), (bsz, n, n))


@jax.custom_vjp
def _cumsum_rows(x):
    tri = _tri(x.shape[0], x.shape[1])
    hi, lo = _split2(x)
    return _bdg(tri, hi, 2, 1) + _bdg(tri, lo, 2, 1)


def _cumsum_rows_bwd(_, g):
    tri = _tri(g.shape[0], g.shape[1])
    hi, lo = _split2(g)
    return (_bdg(tri, hi, 1, 1) + _bdg(tri, lo, 1, 1),)


_cumsum_rows.defvjp(lambda x: (_cumsum_rows(x), None), _cumsum_rows_bwd)


def _sigmoid(x):
    return jax.nn.sigmoid(x)


def _silu(x):
    return x * _sigmoid(x)


def _softplus(x):
    return jnp.maximum(x, 0.0) + jnp.log1p(jnp.exp(-jnp.abs(x)))


def _rms(x, w):
    return x * lax.rsqrt(jnp.mean(x * x, axis=-1, keepdims=True) + EPS) * w


@jax.custom_vjp
def _inv_unit_lower(lm):
    n = lm.shape[1]
    a = (_iota2((n, n), 0) == _iota2((n, n), 1)).astype(F32)[None] - lm
    steps = max(1, (n - 1).bit_length()) - 1
    p = _bnn(lm, lm)
    for i in range(steps):
        if i == steps - 1:
            a = a + _bnn(a, p)
        else:
            both = _bnn(jnp.concatenate([a, p], axis=1), p)
            a, p = a + both[:, :n], both[:, n:]
    return a


_inv_unit_lower.defvjp(lambda lm: (lambda a: (a, a))(_inv_unit_lower(lm)),
                       lambda a, g: (-_bnt(_btn(a, g), a),))


def _gdn_chunk(q, k, v, b_b, g_b, s):
    n, dv = q.shape[1], v.shape[2]
    r, c = _iota2((n, n), 0), _iota2((n, n), 1)
    causal, strict, eye = (r >= c)[None], (r > c)[None], (r == c)[None]
    g_cum = _cumsum_rows(g_b)
    g_i = g_cum[:, :, :n]
    g_j = jnp.sum(jnp.where(eye, g_i, 0.0), axis=1, keepdims=True)
    decay = jnp.where(causal, jnp.exp(jnp.where(causal, g_i - g_j, 0.0)), 0.0)
    e_g = jnp.exp(g_cum)
    kb = k * b_b
    kk = _bnt(jnp.concatenate([kb, q], axis=1), k)
    a_inv = _inv_unit_lower(jnp.where(strict, kk[:, :n] * decay, 0.0))
    uw = _bnn(a_inv, jnp.concatenate([v * b_b, kb * e_g], axis=2))
    ws = _bnn(jnp.concatenate([uw[:, :, dv:], q * e_g], axis=1), s)
    v_new = uw[:, :, :dv] - ws[:, :n]
    o = ws[:, n:] + _bnn(kk[:, n:] * decay, v_new)
    g_last = g_cum[:, n - 1:n, :]
    s_new = s * jnp.exp(g_last) +_btn(k * jnp.exp(g_last - g_cum), v_new)
    return o, s_new


@functools.partial(jax.custom_vjp, nondiff_argnums=(1, 2))
def _row(x, j, n):
    return x[:, j:j + 1, :]


def _row_bwd(j, n, _, g):
    return (jnp.where(_iota2((1, n, 1), 1) == j, g, 0.0),)


_row.defvjp(lambda x, j, n: (_row(x, j, n), None), _row_bwd)


def _hgrn_chunk(q, k, v, lf, st):
    n = q.shape[1]
    b_cum = _cumsum_rows(lf)
    o = _bnt(q * jnp.exp(b_cum), st)
    half = n // 2
    parts = []
    for lo in (0, half):
        qs, bs = q[:, lo:], b_cum[:, lo:]
        rows = _iota2((1, n - lo, 1), 1) + lo
        acc = jnp.zeros_like(qs)
        for j in range(lo, n if lo else half):
            p = jnp.exp(jnp.where(rows >= j, bs - _row(b_cum, j, n), -1e30))
            acc = acc + jnp.sum(qs * _row(k, j, n) * p, axis=2, keepdims=True) * _row(v, j, n)
        parts.append(acc)
    o = o + parts[0] + jnp.concatenate([jnp.zeros_like(parts[1]), parts[1]], axis=1)
    b_last = _row(b_cum, n - 1, n)
    st_new = st * jnp.exp(b_last) + _btn(v, k * jnp.exp(b_last - b_cum))
    return o, st_new


def _hgrn_block(q, k, v, lf, st):
    n = HGRN_CHUNK
    outs = []
    for c in range(q.shape[1] // n):
        rs = slice(c * n, (c + 1) * n)
        o, st = _hgrn_chunk(q[:, rs], k[:, rs], v[:, rs], lf[:, rs], st)
        outs.append(o)
    return jnp.concatenate(outs, axis=1), st


def _l2n_act(y, scale):
    a = _silu(y)
    return a * lax.rsqrt(jnp.sum(a * a, axis=-1, keepdims=True) + EPS) * scale


def _col(x, lane):
    return jnp.sum(jnp.where(_iota2(x.shape, 1) == lane, x, 0.0), axis=1, keepdims=True)


def _elem(x, row, lane):
    m = (_iota2(x.shape, 0) == row) & (_iota2(x.shape, 1) == lane)
    return jnp.sum(jnp.sum(jnp.where(m, x, 0.0), axis=1, keepdims=True), axis=0, keepdims=True)


def _gdn_gates(misc, aux, real, head):
    beta = _sigmoid(_col(misc, head))
    g = -jnp.exp(_elem(aux, 0, head)) * _softplus(_col(misc, N_HEADS + head) + _elem(aux, 1, head))
    g = jnp.where(real, g, 0.0)
    shape = (misc.shape[0], D_HEAD)
    return jnp.broadcast_to(beta, shape), jnp.broadcast_to(g, shape)


def _hgrn_prep(bq, bf, lb, real):
    qb = _silu(bq) * (D_HEAD ** -0.5)
    log_sig = jnp.minimum(bf, 0.0) - jnp.log1p(jnp.exp(-jnp.abs(bf)))
    pos = lb > 0.0
    lbs = jnp.where(pos, lb, 0.5)
    a = jnp.log(lbs)
    b = jnp.log1p(-lbs) + log_sig
    lae = jnp.maximum(a, b) + jnp.log1p(jnp.exp(-jnp.abs(a - b)))
    lf = jnp.where(pos, lae, log_sig)
    kb = jnp.where(pos, 1.0 - lbs, 1.0) * _sigmoid(-bf)
    return qb, jnp.where(real, kb, 0.0), jnp.where(real, lf, 0.0)


def _gated_norm(o, z, gw):
    return o * lax.rsqrt(jnp.mean(o * o, axis=-1, keepdims=True) + EPS) * gw * _silu(z)


def _shift_down(x, j):
    return x if j == 0 else pltpu.roll(x, j, 0)


def _shift_up(x, j):
    return x if j == 0 else pltpu.roll(x, x.shape[0] - j, 0)


def _all_gather_hbm(blocks, name):
    na = len(blocks)

    def body(*refs):
        x_refs, out_refs = refs[:na], refs[na:2 * na]
        send_sems, recv_sems, local_sems = refs[2 * na:]
        mx, my, mc = lax.axis_index("x"), lax.axis_index("y"), lax.axis_index("c")
        me, sibling = (mx, my, mc), (mx, my, 1 - mc)
        chips = [(1 - mx, my), (mx, 1 - my), (1 - mx, 1 - my)]

        def slab(a, px, py, pc):
            return out_refs[a].at[4 * px + 2 * py + pc]

        def copy(a, k, blk, to, own=False):
            return pltpu.make_async_remote_copy(
                src_ref=x_refs[a] if own else slab(a, *blk), dst_ref=slab(a, *blk),
                send_sem=send_sems.at[7 * a + k], recv_sem=recv_sems.at[7 * a + k], device_id=to, device_id_type=MESH)

        mine = [pltpu.make_async_copy(x_refs[a], slab(a, *me), local_sems.at[a]) for a in range(na)]
        for cp in mine:
            cp.start()
        first = [copy(a, 0, me, sibling, own=True) for a in range(na)]
        first += [copy(a, 1 + j, me, (*chip, mc), own=True) for j, chip in enumerate(chips) for a in range(na)]
        for cp in first:
            cp.start()
        passed = []
        for j, chip in enumerate(chips):
            for a in range(na):
                copy(a, 1 + j, (*chip, mc), me).wait_recv()
                passed.append(copy(a, 4 + j, (*chip, mc), sibling))
                passed[-1].start()
        for a in range(na):
            copy(a, 0, sibling, me).wait_recv()
            for j, chip in enumerate(chips):
                copy(a, 4 + j, (*chip, 1 - mc), me).wait_recv()
        for cp in first + passed:
            cp.wait_send()
        for cp in mine:
            cp.wait()

    hbm = BS(memory_space=pl.ANY)
    return pl.pallas_call(
        body, name=name, out_shape=[SDS((N_DEV, *b.shape), b.dtype) for b in blocks],
        in_specs=[hbm] * na, out_specs=[hbm] * na,
        scratch_shapes=[pltpu.SemaphoreType.DMA((7 * na,)), pltpu.SemaphoreType.DMA((7 * na,)),
                        pltpu.SemaphoreType.DMA((na,))],
    )(*blocks)


def _all_reduce_small(block, name):
    r, c = block.shape

    def body(x_ref, out_ref, buf, send_sems, recv_sems):
        mx, my, mc = lax.axis_index("x"), lax.axis_index("y"), lax.axis_index("c")
        me, sibling = (mx, my, mc), (mx, my, 1 - mc)
        chips = [(1 - mx, my), (mx, 1 - my), (1 - mx, 1 - my)]

        def slab(px, py, pc):
            return buf.at[4 * px + 2 * py + pc]

        def copy(k, blk, to, src=None):
            return pltpu.make_async_remote_copy(
                src_ref=slab(*blk) if src is None else src, dst_ref=slab(*blk),
                send_sem=send_sems.at[k], recv_sem=recv_sems.at[k], device_id=to, device_id_type=MESH)

        first = [copy(0, me, sibling, src=x_ref)]
        first += [copy(1 + j, me, (*chip, mc), src=x_ref) for j, chip in enumerate(chips)]
        for cp in first:
            cp.start()
        passed = [copy(4 + j, (*chip, mc), sibling) for j, chip in enumerate(chips)]
        for j, chip in enumerate(chips):
            copy(1 + j, (*chip, mc), me).wait_recv()
            passed[j].start()
        copy(0, sibling, me).wait_recv()
        for j, chip in enumerate(chips):
            copy(4 + j, (*chip, 1 - mc), me).wait_recv()
        for cp in first + passed:
            cp.wait_send()
        buf[4 * mx + 2 * my + mc] = x_ref[...]
        acc = buf[0]
        for d in range(1, N_DEV):
            acc = acc + buf[d]
        out_ref[...] = acc

    return pl.pallas_call(
        body, name=name, out_shape=SDS((r, c), F32),
        in_specs=[BS(memory_space=pltpu.VMEM)], out_specs=BS(memory_space=pltpu.VMEM),
        scratch_shapes=[pltpu.VMEM((N_DEV, r, c), F32), pltpu.SemaphoreType.DMA((7,)), pltpu.SemaphoreType.DMA((7,))],
    )(block)


HBM_SPEC = BS(memory_space=pltpu.HBM)
SEM_SPEC = BS(memory_space=pltpu.SEMAPHORE)
SIDE_EFFECT = pltpu.SideEffectType.DATAFLOW_SIDE_EFFECTING


def _peer(rel):
    flip = lambda v, bit: 1 - v if bit else v
    return (flip(lax.axis_index("x"), rel >> 2 & 1), flip(lax.axis_index("y"), rel >> 1 & 1),
            flip(lax.axis_index("c"), rel & 1))


def _send_all_start(blocks, scatter, name):
    na = len(blocks)
    shapes = [b.shape[1:] if scatter else b.shape for b in blocks]

    def body(*refs):
        srcs, lands = refs[:na], refs[na:2 * na]
        send_sems, recv_sems, token = refs[2 * na], refs[2 * na + 1], refs[-1]
        me = 4 * lax.axis_index("x") + 2 * lax.axis_index("y") + lax.axis_index("c")
        for a in range(na):
            for rel in range(1, N_DEV):
                px, py, pc = _peer(rel)
                pltpu.make_async_remote_copy(
                    src_ref=srcs[a].at[4 * px + 2 * py + pc] if scatter else srcs[a], dst_ref=lands[a].at[me],
                    send_sem=send_sems.at[7 * a + rel - 1], recv_sem=recv_sems.at[7 * a + rel - 1],
                    device_id=(px, py, pc), device_id_type=MESH).start()
        token[...] = jnp.zeros_like(token)

    lands = [lax.empty((N_DEV, *s), b.dtype) for s, b in zip(shapes, blocks)]
    res = pl.pallas_call(
        body, name=name,
        out_shape=([pltpu.SemaphoreType.DMA((7 * na,)), pltpu.SemaphoreType.DMA((7 * na,))]
                   + [pltpu.HBM(b.shape, b.dtype) for b in blocks] + [pltpu.HBM(ld.shape, ld.dtype) for ld in lands]
                   + [SDS((SUBLANES, LANES), F32)]),
        in_specs=[HBM_SPEC] * (2 * na), out_specs=[SEM_SPEC, SEM_SPEC] + [HBM_SPEC] * (2 * na) + [BS(memory_space=pltpu.VMEM)],
        input_output_aliases={i: 2 + i for i in range(2 * na)},
        compiler_params=pltpu.CompilerParams(has_side_effects=SIDE_EFFECT),
    )(*[pltpu.with_memory_space_constraint(b, pltpu.HBM) for b in blocks],
      *[pltpu.with_memory_space_constraint(ld, pltpu.HBM) for ld in lands])
    return dict(send=res[0], recv=res[1], srcs=res[2:2 + na], lands=res[2 + na:2 + 2 * na], scatter=scatter), res[-1]


def _send_all_wait(flight, after, name):
    na = len(flight["srcs"])

    def body(*refs):
        srcs, lands = refs[:na], refs[na:2 * na]
        send_sems, recv_sems = refs[2 * na], refs[2 * na + 1]
        for a in range(na):
            for rel in range(1, N_DEV):
                cp = pltpu.make_async_remote_copy(
                    src_ref=srcs[a].at[0] if flight["scatter"] else srcs[a], dst_ref=lands[a].at[0],
                    send_sem=send_sems.at[7 * a + rel - 1], recv_sem=recv_sems.at[7 * a + rel - 1],
                    device_id=_peer(rel), device_id_type=MESH)
                cp.wait_send()
                cp.wait_recv()

    arrays = list(flight["srcs"]) + list(flight["lands"])
    res = pl.pallas_call(
        body, name=name, out_shape=[pltpu.HBM(a.shape, a.dtype) for a in arrays],
        in_specs=[HBM_SPEC] * (2 * na) + [SEM_SPEC, SEM_SPEC, BS(memory_space=pl.ANY)], out_specs=[HBM_SPEC] * (2 * na),
        input_output_aliases={i: i for i in range(2 * na)},
        compiler_params=pltpu.CompilerParams(has_side_effects=SIDE_EFFECT),
    )(*arrays, flight["send"], flight["recv"], after)
    return res[:na], res[na:]


def _sum_slabs(land, name):
    _, r, c = land.shape
    tr = _pick(r, (256, 128, 64, 32, 16, 8))

    def body(l_ref, o_ref):
        acc = l_ref[0].astype(F32)
        for d in range(1, N_DEV):
            acc = acc + l_ref[d].astype(F32)
        o_ref[...] = acc

    return pl.pallas_call(
        body, name=name, grid=(r // tr,), out_shape=SDS((r, c), F32),
        in_specs=[BS((N_DEV, tr, c), lambda j: (0, j, 0))], out_specs=BS((tr, c), lambda j: (j, 0)),
        compiler_params=_params(1),
    )(land)


def _exchange(bufs, flip, paired, name):
    na, n = len(bufs), bufs[0].shape[0]
    axis = ("x", "y", "c")[flip]

    def body(*refs):
        g_refs, out_refs = refs[:na], refs[na:2 * na]
        send_sems, recv_sems = refs[2 * na:]
        pos = [lax.axis_index("x"), lax.axis_index("y"), lax.axis_index("c")]
        pos[flip] = 1 - pos[flip]
        other = 1 - lax.axis_index(axis)
        copies = [pltpu.make_async_remote_copy(
            src_ref=g_refs[a].at[i, other] if paired else g_refs[a].at[i], dst_ref=out_refs[a].at[i],
            send_sem=send_sems.at[n * a + i], recv_sem=recv_sems.at[n * a + i], device_id=tuple(pos),
            device_id_type=MESH) for i in range(n) for a in range(na)]
        for cp in copies:
            cp.start()
        for cp in copies:
            cp.wait_recv()
        for cp in copies:
            cp.wait_send()

    hbm = BS(memory_space=pl.ANY)
    return pl.pallas_call(
        body, name=name, out_shape=[SDS((n, *b.shape[(2 if paired else 1):]), b.dtype) for b in bufs],
        in_specs=[hbm] * na, out_specs=[hbm] * na,
        scratch_shapes=[pltpu.SemaphoreType.DMA((n * na,)), pltpu.SemaphoreType.DMA((n * na,))],
    )(*bufs)


def _rs_tile(r):
    return _pick(r, (704, 512, 352, 256, 192, 128, 64, 32, 16, 8))


def _rs_add_c(g4, recv, coords, name):
    _, _, r, c = g4.shape
    tr = _rs_tile(r)

    def body(co_ref, a0_ref, a1_ref, b0_ref, b1_ref, keep_ref, send_ref):
        s0 = a0_ref[...] + b0_ref[...]
        s1 = a1_ref[...] + b1_ref[...]
        mine = co_ref[1] == 0
        keep_ref[...] = jnp.where(mine, s0, s1)
        send_ref[...] = jnp.where(mine, s1, s0).astype(BF16)

    blk = lambda yy: BS((None, None, tr, c), functools.partial(lambda i, j, co, yy: (2 * i + yy, co[2], j, 0), yy=yy))
    rblk = lambda yy: BS((None, tr, c), functools.partial(lambda i, j, co, yy: (2 * i + yy, j, 0), yy=yy))
    out = BS((None, tr, c), lambda i, j, co: (i, j, 0))
    return pl.pallas_call(
        body, name=name, out_shape=[SDS((2, r, c), F32), SDS((2, r, c), BF16)],
        grid_spec=pltpu.PrefetchScalarGridSpec(num_scalar_prefetch=1, grid=(2, r // tr),
                                               in_specs=[blk(0), blk(1), rblk(0), rblk(1)], out_specs=[out, out]),
        compiler_params=_params(2),
    )(coords, g4, g4, recv, recv)


def _rs_add_y(kept, recv, coords, name):
    _, r, c = kept.shape
    tr = _rs_tile(r)

    def body(co_ref, a_ref, b_ref, keep_ref, send_ref):
        s0 = a_ref[0] + b_ref[0].astype(F32)
        s1 = a_ref[1] + b_ref[1].astype(F32)
        mine = co_ref[0] == 0
        keep_ref[...] = jnp.where(mine, s0, s1)
        send_ref[0] = jnp.where(mine, s1, s0).astype(BF16)

    blk = BS((2, tr, c), lambda j, co: (0, j, 0))
    return pl.pallas_call(
        body, name=name, out_shape=[SDS((r, c), F32), SDS((1, r, c), BF16)],
        grid_spec=pltpu.PrefetchScalarGridSpec(num_scalar_prefetch=1, grid=(r // tr,), in_specs=[blk, blk],
                                               out_specs=[BS((tr, c), lambda j, co: (j, 0)),
                                                          BS((1, tr, c), lambda j, co: (0, j, 0))]),
        compiler_params=_params(1),
    )(coords, kept, recv)


def _rs_add_x(kept, recv, name):
    r, c = kept.shape
    tr = _rs_tile(r)

    def body(a_ref, b_ref, o_ref):
        o_ref[...] = a_ref[...] + b_ref[0].astype(F32)

    return pl.pallas_call(
        body, name=name, grid=(r // tr,), out_shape=SDS((r, c), F32),
        in_specs=[BS((tr, c), lambda j: (j, 0)), BS((1, tr, c), lambda j: (0, j, 0))],
        out_specs=BS((tr, c), lambda j: (j, 0)), compiler_params=_params(1),
    )(kept, recv)


def _reduce_scatter(arrays, coords, tag):
    ids = range(len(arrays))
    g4 = [a.reshape(4, 2, *a.shape[1:]) for a in arrays]
    got = _exchange(g4, 2, True, f"rs_c_{tag}")
    kept, send = zip(*[_rs_add_c(g4[i], got[i], coords, f"rs_c_add_{tag}{i}") for i in ids])
    got = _exchange(list(send), 1, False, f"rs_y_{tag}")
    kept, send = zip(*[_rs_add_y(kept[i], got[i], coords, f"rs_y_add_{tag}{i}") for i in ids])
    got = _exchange(list(send), 0, False, f"rs_x_{tag}")
    return [_rs_add_x(kept[i], got[i], f"rs_x_add_{tag}{i}") for i in ids]


def _proj_fwd(h, nw8, wp, tag):
    n = h.shape[0]
    tm = _pick(n, (1408, 768, 512, 384, 256, 192, 128, 64))
    tn = 896

    def body(h_ref, nw_ref, w_ref, proj_ref, xn_ref):
        @pl.when(pl.program_id(1) == 0)
        def _():
            xn_ref[...] = _rms(h_ref[...], nw_ref[0:1, :]).astype(BF16)

        proj_ref[...] = jnp.dot(xn_ref[...], w_ref[...], preferred_element_type=F32)

    return pl.pallas_call(
        body, name=f"proj_fwd_{tag}", grid=(n // tm, PROJ_W // tn),
        in_specs=[BS((tm, D_MODEL), lambda i, j: (i, 0)), BS((SUBLANES, D_MODEL), lambda i, j: (0, 0)),
                  BS((D_MODEL, tn), lambda i, j: (0, j))],
        out_specs=[BS((tm, tn), lambda i, j: (i, j)), BS((tm, D_MODEL), lambda i, j: (i, 0))],
        out_shape=[SDS((n, PROJ_W), F32), SDS((n, D_MODEL), BF16)], compiler_params=_params(2),
    )(h, nw8, wp)


def _conv_ext(x_ext, cw_ref):
    y = x_ext * cw_ref[3:4, :]
    for k in range(3):
        y = y + _shift_down(x_ext, 3 - k) * cw_ref[k:k + 1, :]
    return y[SUBLANES:]


def _prep_fwd(proj, cw8, aux, lb8, nseq, t_len, tag):
    n = proj.shape[0]
    tt = _pick(t_len, (192, 128, 64))
    nt_ = t_len // tt
    qkv_w = 3 * HEADS_W

    def body(cur_ref, prev_ref, misc_ref, bq_ref, bf_ref, cw_ref, aux_ref, lb_ref,
             q_ref, k_ref, v_ref, b_ref, g_ref, qb_ref, kb_ref, lf_ref, ext_ref):
        t = pl.program_id(1)
        ext_ref[0:SUBLANES, :] = jnp.where(t == 0, 0.0, prev_ref[...])
        ext_ref[SUBLANES:, :] = cur_ref[...]
        y = ext_ref[SUBLANES:, :] * cw_ref[3:4, :]
        for kk in range(3):
            y = y + ext_ref[SUBLANES - 3 + kk:SUBLANES - 3 + kk + tt, :] * cw_ref[kk:kk + 1, :]
        real = (t * tt + _iota2((tt, 1), 0)) >= N_PAD
        misc = misc_ref[...]
        auxv = aux_ref[...]
        for hd in range(N_HEADS):
            sl = slice(hd * D_HEAD, (hd + 1) * D_HEAD)
            q_ref[:, sl] = _l2n_act(y[:, sl], D_HEAD ** -0.5)
            k_ref[:, sl] = _l2n_act(y[:, HEADS_W + hd * D_HEAD:HEADS_W + (hd + 1) * D_HEAD], 1.0)
            v_ref[:, sl] = _silu(y[:, 2 * HEADS_W + hd * D_HEAD:2 * HEADS_W + (hd + 1) * D_HEAD])
            b_ref[:, sl], g_ref[:, sl] = _gdn_gates(misc, auxv, real, hd)
        qb_ref[...], kb_ref[...], lf_ref[...] = _hgrn_prep(bq_ref[...], bf_ref[...], lb_ref[0:1, :], real)

    rb = tt // SUBLANES
    row = lambda s, t: s * nt_ + t
    wide = BS((tt, HEADS_W), lambda s, t: (row(s, t), 0))
    return pl.pallas_call(
        body, name=f"prep_fwd_{tag}", grid=(nseq, nt_),
        in_specs=[BS((tt, qkv_w), lambda s, t: (row(s, t), 0)),
                  BS((SUBLANES, qkv_w), lambda s, t: (jnp.maximum(row(s, t) * rb - 1, 0), 0)),
                  BS((tt, LANES), lambda s, t: (row(s, t), C_MISC // LANES)),
                  BS((tt, HEADS_W), lambda s, t: (row(s, t), C_BQ // HEADS_W)),
                  BS((tt, HEADS_W), lambda s, t: (row(s, t), C_BF // HEADS_W)),
                  BS((SUBLANES, qkv_w), lambda s, t: (0, 0)), BS((SUBLANES, LANES), lambda s, t: (0, 0)),
                  BS((SUBLANES, HEADS_W), lambda s, t: (0, 0))],
        out_specs=[wide] * 8, out_shape=[SDS((n, HEADS_W), F32)] * 8,
        scratch_shapes=[pltpu.VMEM((tt + SUBLANES, qkv_w), F32)], compiler_params=_params(2),
    )(proj, proj, proj, proj, proj, cw8, aux, lb8)


GDN_SEQS = 4
HGRN_SEQS = 2


def _seq_block(nseq, most):
    return max(s for s in (1, 2, 4) if s <= most and nseq % s == 0)


def _to_chains(x):
    return jnp.concatenate([x[:, :, hd * D_HEAD:(hd + 1) * D_HEAD] for hd in range(N_HEADS)], axis=0)


def _from_chains(ref, rows, val):
    sb = val.shape[0] // N_HEADS
    for hd in range(N_HEADS):
        ref[:, rows, hd * D_HEAD:(hd + 1) * D_HEAD] = val[hd * sb:(hd + 1) * sb].astype(ref.dtype)


def _scan_call(body, name, arrays, col_blocks, reverse, nseq, t_len, most_seqs, extra_in, outs):
    sb = _seq_block(nseq, most_seqs)
    nc = t_len // GDN_CHUNK
    chains = N_HEADS * sb
    cidx = (lambda c: nc - 1 - c) if reverse else (lambda c: c)
    ck_shape = (nseq // sb, nc, chains, D_HEAD, D_HEAD)
    ck_spec = BS((None, None, chains, D_HEAD, D_HEAD), lambda p, c: (p, cidx(c), 0, 0, 0))
    in_specs = [BS((sb, GDN_CHUNK, HEADS_W), functools.partial(lambda p, c, cb: (p, cidx(c), cb), cb=cb))
                for cb in col_blocks]
    args = [a.reshape(nseq, t_len, a.shape[1]) for a in arrays]
    if extra_in is not None:
        in_specs.append(ck_spec)
        args.append(extra_in)
    out_specs, out_shape = [], []
    for o in outs:
        if o == "ckpt":
            out_specs.append(ck_spec)
            out_shape.append(SDS(ck_shape, F32))
        else:
            out_specs.append(BS((sb, GDN_CHUNK, HEADS_W), lambda p, c: (p, cidx(c), 0)))
            out_shape.append(SDS((nseq, t_len, HEADS_W), o))
    res = pl.pallas_call(
        body, name=name, grid=(nseq // sb, nc), in_specs=in_specs, out_specs=out_specs, out_shape=out_shape,
        scratch_shapes=[pltpu.VMEM((chains, D_HEAD, D_HEAD), F32)], compiler_params=_params(2),
    )(*args)
    return [r if o == "ckpt" else r.reshape(nseq * t_len, HEADS_W) for r, o in zip(res, outs)]


def _gdn_fwd(q, k, v, b, g, nseq, t_len, tag):
    def body(q_ref, k_ref, v_ref, b_ref, g_ref, o_ref, sck_ref, s_ref):
        @pl.when(pl.program_id(1) == 0)
        def _():
            s_ref[...] = jnp.zeros_like(s_ref)

        s = s_ref[...]
        sck_ref[...] = s
        o, s_new = _gdn_chunk(*[_to_chains(r[...]) for r in (q_ref, k_ref, v_ref, b_ref, g_ref)], s)
        _from_chains(o_ref, slice(None), o)
        s_ref[...] = s_new

    return _scan_call(body, f"gdn_fwd_{tag}", [q, k, v, b, g], [0] * 5, False, nseq, t_len, GDN_SEQS, None,
                      [F32, "ckpt"])


def _gdn_bwd(q, k, v, b, g, sck, do, nseq, t_len, tag):
    def body(q_ref, k_ref, v_ref, b_ref, g_ref, do_ref, sck_ref, dq_ref, dk_ref, dv_ref, db_ref, dg_ref, ds_ref):
        @pl.when(pl.program_id(1) == 0)
        def _():
            ds_ref[...] = jnp.zeros_like(ds_ref)

        _, vjp = jax.vjp(_gdn_chunk, *[_to_chains(r[...]) for r in (q_ref, k_ref, v_ref, b_ref, g_ref)], sck_ref[...])
        grads = vjp((_to_chains(do_ref[...]), ds_ref[...]))
        for ref, val in zip((dq_ref, dk_ref, dv_ref, db_ref, dg_ref), grads[:5]):
            _from_chains(ref, slice(None), val)
        ds_ref[...] = grads[5]

    return _scan_call(body, f"gdn_bwd_{tag}", [q, k, v, b, g, do], [0] * 6, True, nseq, t_len, GDN_SEQS, sck,
                      [F32] * 5)


def _hgrn_fwd(q, k, v, v_col, lf, nseq, t_len, tag):
    def body(q_ref, k_ref, v_ref, lf_ref, o_ref, sck_ref, s_ref):
        @pl.when(pl.program_id(1) == 0)
        def _():
            s_ref[...] = jnp.zeros_like(s_ref)

        s = s_ref[...]
        sck_ref[...] = s
        o, s_new = _hgrn_block(*[_to_chains(r[...]) for r in (q_ref, k_ref, v_ref, lf_ref)], s)
        _from_chains(o_ref, slice(None), o)
        s_ref[...] = s_new

    return _scan_call(body, f"hgrn_fwd_{tag}", [q, k, v, lf], [0, 0, v_col, 0], False, nseq, t_len, HGRN_SEQS, None,
                      [F32, "ckpt"])


def _hgrn_bwd(q, k, v, v_col, lf, sck, do, nseq, t_len, tag):
    def body(q_ref, k_ref, v_ref, lf_ref, do_ref, sck_ref, dq_ref, dk_ref, dv_ref, dlf_ref, ds_ref):
        @pl.when(pl.program_id(1) == 0)
        def _():
            ds_ref[...] = jnp.zeros_like(ds_ref)

        _, vjp = jax.vjp(_hgrn_block, *[_to_chains(r[...]) for r in (q_ref, k_ref, v_ref, lf_ref)], sck_ref[...])
        grads = vjp((_to_chains(do_ref[...]), ds_ref[...]))
        for ref, val in zip((dq_ref, dk_ref, dv_ref, dlf_ref), grads[:4]):
            _from_chains(ref, slice(None), val)
        ds_ref[...] = grads[4]

    return _scan_call(body, f"hgrn_bwd_{tag}", [q, k, v, lf, do], [0, 0, v_col, 0, 0], True, nseq, t_len, HGRN_SEQS, sck,
                      [F32, F32, BF16, F32])


def _post_values(oa_ref, ob_ref, z_ref, bg_ref, ga_ref, gb_ref, gn_ref, wa_ref, wb_ref, ya_ref, yb_ref):
    for hd in range(N_HEADS):
        sl = slice(hd * D_HEAD, (hd + 1) * D_HEAD)
        ya_ref[:, sl] = _gated_norm(oa_ref[:, sl], z_ref[:, sl], gn_ref[0:1, :]).astype(BF16)
        yb_ref[:, sl] = _gated_norm(ob_ref[:, sl], bg_ref[:, sl], gn_ref[1:2, :]).astype(BF16)
    pa = jnp.dot(ya_ref[...], wa_ref[...], preferred_element_type=F32)
    pb = jnp.dot(yb_ref[...], wb_ref[...], preferred_element_type=F32)
    return pa, pb, _sigmoid(ga_ref[...]), _sigmoid(gb_ref[...])


def _post_specs(tm):
    r2 = lambda i: (i, 0)
    return [BS((tm, HEADS_W), r2), BS((tm, HEADS_W), r2),
            BS((tm, HEADS_W), lambda i: (i, C_Z // HEADS_W)), BS((tm, HEADS_W), lambda i: (i, C_BG // HEADS_W)),
            BS((tm, D_MODEL), lambda i: (i, C_GA // D_MODEL)), BS((tm, D_MODEL), lambda i: (i, C_GB // D_MODEL)),
            BS((tm, D_MODEL), r2), BS((SUBLANES, LANES), lambda i: (0, 0))]


def _post_fwd(oa, ob, proj, h, gn8, wa, wb, wout, tag):
    n = h.shape[0]
    tm = _pick(n, (256, 192, 128, 64))

    def body(oa_ref, ob_ref, z_ref, bg_ref, ga_ref, gb_ref, h_ref, gn_ref, wa_ref, wb_ref, wout_ref, out_ref,
             ya_ref, yb_ref):
        pa, pb, sa, sb = _post_values(oa_ref, ob_ref, z_ref, bg_ref, ga_ref, gb_ref, gn_ref, wa_ref, wb_ref,
                                      ya_ref, yb_ref)
        mixed = (sa * pa + sb * pb).astype(BF16)
        out_ref[...] = h_ref[...] + jnp.dot(mixed, wout_ref[...], preferred_element_type=F32)

    full = lambda i: (0, 0)
    return pl.pallas_call(
        body, name=f"post_fwd_{tag}", grid=(n // tm,),
        in_specs=_post_specs(tm) + [BS((HEADS_W, D_MODEL), full), BS((HEADS_W, D_MODEL), full),
                                    BS((D_MODEL, D_MODEL), full)],
        out_specs=BS((tm, D_MODEL), lambda i: (i, 0)), out_shape=SDS((n, D_MODEL), F32),
        scratch_shapes=[pltpu.VMEM((tm, HEADS_W), BF16), pltpu.VMEM((tm, HEADS_W), BF16)], compiler_params=_params(1),
    )(oa, ob, proj, proj, proj, proj, h, gn8, wa, wb, wout)


def _post_bwd(dh, oa, ob, proj, h, gn8, wa, wb, wa_t, wb_t, wout_t, tag):
    n = h.shape[0]
    tm = _pick(n, (256, 192, 128, 64))

    def body(dh_ref, oa_ref, ob_ref, z_ref, bg_ref, ga_ref, gb_ref, h_ref, gn_ref, wa_ref, wb_ref, wat_ref, wbt_ref,
             woutt_ref, doa_ref, dob_ref, dz_ref, dbg_ref, dga_ref, dgb_ref, dwa_ref, dwb_ref, dwout_ref, dgn_ref,
             ya_ref, yb_ref):
        @pl.when(pl.program_id(0) == 0)
        def _():
            dwa_ref[...] = jnp.zeros_like(dwa_ref)
            dwb_ref[...] = jnp.zeros_like(dwb_ref)
            dwout_ref[...] = jnp.zeros_like(dwout_ref)
            dgn_ref[...] = jnp.zeros_like(dgn_ref)

        pa, pb, sa, sb = _post_values(oa_ref, ob_ref, z_ref, bg_ref, ga_ref, gb_ref, gn_ref, wa_ref, wb_ref,
                                      ya_ref, yb_ref)
        mixed = (sa * pa + sb * pb).astype(BF16)
        dout = dh_ref[...].astype(BF16)
        dwout_ref[...] += _dg(mixed, dout, ((0,), (0,)))
        dmixed = jnp.dot(dout, woutt_ref[...], preferred_element_type=F32)
        dga_ref[...] = (dmixed * pa * sa * (1.0 - sa)).astype(BF16)
        dgb_ref[...] = (dmixed * pb * sb * (1.0 - sb)).astype(BF16)
        dpa = (dmixed * sa).astype(BF16)
        dpb = (dmixed * sb).astype(BF16)
        dwa_ref[...] += _dg(ya_ref[...], dpa, ((0,), (0,)))
        dwb_ref[...] += _dg(yb_ref[...], dpb, ((0,), (0,)))
        dya = jnp.dot(dpa, wat_ref[...], preferred_element_type=F32)
        dyb = jnp.dot(dpb, wbt_ref[...], preferred_element_type=F32)
        dgn_a = jnp.zeros((1, D_HEAD), F32)
        dgn_b = jnp.zeros((1, D_HEAD), F32)
        for hd in range(N_HEADS):
            sl = slice(hd * D_HEAD, (hd + 1) * D_HEAD)
            _, vjp = jax.vjp(_gated_norm, oa_ref[:, sl], z_ref[:, sl], gn_ref[0:1, :])
            doa, dz, dgw = vjp(dya[:, sl])
            doa_ref[:, sl], dz_ref[:, sl], dgn_a = doa, dz.astype(BF16), dgn_a + dgw
            _, vjp = jax.vjp(_gated_norm, ob_ref[:, sl], bg_ref[:, sl], gn_ref[1:2, :])
            dob, dbg, dgw = vjp(dyb[:, sl])
            dob_ref[:, sl], dbg_ref[:, sl], dgn_b = dob, dbg.astype(BF16), dgn_b + dgw
        dgn_ref[0:1, :] += dgn_a
        dgn_ref[1:2, :] += dgn_b

    full = lambda i: (0, 0)
    r2 = lambda i: (i, 0)
    return pl.pallas_call(
        body, name=f"post_bwd_{tag}", grid=(n // tm,),
        in_specs=[BS((tm, D_MODEL), r2)] + _post_specs(tm) + [
            BS((HEADS_W, D_MODEL), full), BS((HEADS_W, D_MODEL), full), BS((D_MODEL, HEADS_W), full),
            BS((D_MODEL, HEADS_W), full), BS((D_MODEL, D_MODEL), full)],
        out_specs=[BS((tm, HEADS_W), r2)] * 4 + [BS((tm, D_MODEL), r2)] * 2 + [
            BS((HEADS_W, D_MODEL), full), BS((HEADS_W, D_MODEL), full), BS((D_MODEL, D_MODEL), full),
            BS((SUBLANES, LANES), full)],
        out_shape=[SDS((n, HEADS_W), F32), SDS((n, HEADS_W), F32), SDS((n, HEADS_W), BF16), SDS((n, HEADS_W), BF16),
                   SDS((n, D_MODEL), BF16), SDS((n, D_MODEL), BF16), SDS((HEADS_W, D_MODEL), F32),
                   SDS((HEADS_W, D_MODEL), F32), SDS((D_MODEL, D_MODEL), F32), SDS((SUBLANES, LANES), F32)],
        scratch_shapes=[pltpu.VMEM((tm, HEADS_W), BF16), pltpu.VMEM((tm, HEADS_W), BF16)], compiler_params=_params(1),
    )(dh, oa, ob, proj, proj, proj, proj, h, gn8, wa, wb, wa_t, wb_t, wout_t)


def _loss_head(h, fw8, target, nseq, t_len):
    n = h.shape[0]
    nc = t_len // GDN_CHUNK
    sub = 3 if nc % 3 == 0 else 1
    tl, nt = sub * GDN_CHUNK, nc // sub
    inv_d = 1.0 / D_MODEL

    def body(h_ref, fw_ref, *rest):
        tgt_refs, (dh_ref, acc_ref) = rest[:sub], rest[sub:]

        @pl.when((pl.program_id(0) == 0) & (pl.program_id(1) == 0))
        def _():
            acc_ref[...] = jnp.zeros_like(acc_ref)

        frames = ((pl.program_id(1) * tl + _iota2((tl, 1), 0)) >= N_PAD + N_META).astype(F32)
        y, vjp = jax.vjp(_rms, h_ref[...], fw_ref[0:1, :])
        err = (y - jnp.concatenate([r[...] for r in tgt_refs], axis=0)) * frames
        dx, dfw = vjp(err * inv_d)
        dh_ref[...] = dx
        acc_ref[0:1, :] += dfw
        acc_ref[1:2, :] += (0.5 * inv_d) * jnp.sum(err * err, axis=0, keepdims=True)

    tgt_spec = lambda u: BS((None, GDN_CHUNK, D_MODEL), lambda s, t: (s, jnp.maximum(t * sub + u - 1, 0), 0))
    return pl.pallas_call(
        body, name="loss_head", grid=(nseq, nt),
        in_specs=[BS((tl, D_MODEL), lambda s, t: (s * nt + t, 0)), BS((SUBLANES, D_MODEL), lambda s, t: (0, 0))]
        + [tgt_spec(u) for u in range(sub)],
        out_specs=[BS((tl, D_MODEL), lambda s, t: (s * nt + t, 0)), BS((SUBLANES, D_MODEL), lambda s, t: (0, 0))],
        out_shape=[SDS((n, D_MODEL), F32), SDS((SUBLANES, D_MODEL), F32)], compiler_params=_params(2),
    )(h, fw8, *[target] * sub)


def _prep_bwd(proj, dq, dk, dv, db, dg, dqb, dkb, dlf, cw8, aux, lb8, nseq, t_len, tag):
    n = proj.shape[0]
    tt = _pick(t_len, (192, 128, 64))
    nt_ = t_len // tt
    qkv_w = 3 * HEADS_W
    rb = tt // SUBLANES
    ext = tt + SUBLANES

    def body(cur_ref, prev_ref, next_ref, misc_ref, bq_ref, bf_ref, dq_ref, dqn_ref, dk_ref, dkn_ref, dv_ref, dvn_ref,
             db_ref, dg_ref, dqb_ref, dkb_ref, dlf_ref, cw_ref, aux_ref, lb_ref,
             dqkv_ref, dmisc_ref, dbq_ref, dbf_ref, dcw_ref, daux_ref, dlb_ref, dy_ref):
        s, t = pl.program_id(0), pl.program_id(1)

        @pl.when((s == 0) & (t == 0))
        def _():
            dcw_ref[...] = jnp.zeros_like(dcw_ref)
            daux_ref[...] = jnp.zeros_like(daux_ref)
            dlb_ref[...] = jnp.zeros_like(dlb_ref)

        prev = jnp.where(t == 0, 0.0, prev_ref[...])
        x_ext = jnp.concatenate([prev, cur_ref[...], next_ref[...]], axis=0)
        y = _conv_ext(x_ext, cw_ref)
        inside = (t < nt_ - 1) | (_iota2((ext, 1), 0) < tt)
        dy_ref[0:SUBLANES, :] = jnp.zeros((SUBLANES, qkv_w), F32)
        for hd in range(N_HEADS):
            for grp, (g_ref, gn_ref, scale) in enumerate(((dq_ref, dqn_ref, D_HEAD ** -0.5), (dk_ref, dkn_ref, 1.0),
                                                          (dv_ref, dvn_ref, None))):
                lo = grp * HEADS_W + hd * D_HEAD
                sl = slice(hd * D_HEAD, (hd + 1) * D_HEAD)
                cot = jnp.concatenate([g_ref[:, sl], gn_ref[:, sl]], axis=0)
                fn = _silu if scale is None else functools.partial(_l2n_act, scale=scale)
                _, vjp = jax.vjp(fn, y[:, lo:lo + D_HEAD])
                dy_ref[SUBLANES:, lo:lo + D_HEAD] = jnp.where(inside, vjp(cot)[0], 0.0)
        dy_ext = dy_ref[...]
        dx = dy_ext * cw_ref[3:4, :]
        for kk in range(3):
            dx = dx + _shift_up(dy_ext, 3 - kk) * cw_ref[kk:kk + 1, :]
        dqkv_ref[...] = dx[SUBLANES:SUBLANES + tt].astype(BF16)
        dy_cur = dy_ext[SUBLANES:SUBLANES + tt]
        for kk in range(4):
            xs = _shift_down(x_ext, 3 - kk)[SUBLANES:SUBLANES + tt]
            dcw_ref[kk:kk + 1, :] += jnp.sum(xs * dy_cur, axis=0, keepdims=True)

        real = (t * tt + _iota2((tt, 1), 0)) >= N_PAD
        dmisc = jnp.zeros((tt, LANES), F32)
        daux = jnp.zeros((SUBLANES, LANES), F32)
        for hd in range(N_HEADS):
            sl = slice(hd * D_HEAD, (hd + 1) * D_HEAD)
            _, vjp = jax.vjp(lambda m, a: _gdn_gates(m, a, real, hd), misc_ref[...], aux_ref[...])
            dm, da = vjp((db_ref[:, sl], dg_ref[:, sl]))
            dmisc, daux = dmisc + dm, daux + da
        dmisc_ref[...] = dmisc.astype(BF16)
        daux_ref[...] += daux
        _, vjp = jax.vjp(lambda a, b, c: _hgrn_prep(a, b, c, real), bq_ref[...], bf_ref[...], lb_ref[0:1, :])
        dbq, dbf, dlb = vjp((dqb_ref[...], dkb_ref[...], dlf_ref[...]))
        dbq_ref[...], dbf_ref[...] = dbq.astype(BF16), dbf.astype(BF16)
        dlb_ref[0:1, :] += dlb

    row = lambda s, t: s * nt_ + t
    cur = lambda s, t: (row(s, t), 0)
    nxt = lambda s, t: (jnp.minimum((row(s, t) + 1) * rb, n // SUBLANES - 1), 0)
    wide = BS((tt, HEADS_W), cur)
    halo = BS((SUBLANES, HEADS_W), nxt)
    full = lambda s, t: (0, 0)
    return pl.pallas_call(
        body, name=f"prep_bwd_{tag}", grid=(nseq, nt_),
        in_specs=[BS((tt, qkv_w), cur), BS((SUBLANES, qkv_w), lambda s, t: (jnp.maximum(row(s, t) * rb - 1, 0), 0)),
                  BS((SUBLANES, qkv_w), nxt), BS((tt, LANES), lambda s, t: (row(s, t), C_MISC // LANES)),
                  BS((tt, HEADS_W), lambda s, t: (row(s, t), C_BQ // HEADS_W)),
                  BS((tt, HEADS_W), lambda s, t: (row(s, t), C_BF // HEADS_W)),
                  wide, halo, wide, halo, wide, halo, wide, wide, wide, wide, wide,
                  BS((SUBLANES, qkv_w), full), BS((SUBLANES, LANES), full), BS((SUBLANES, HEADS_W), full)],
        out_specs=[BS((tt, qkv_w), cur), BS((tt, LANES), cur), wide, wide,
                   BS((SUBLANES, qkv_w), full), BS((SUBLANES, LANES), full), BS((SUBLANES, HEADS_W), full)],
        out_shape=[SDS((n, qkv_w), BF16), SDS((n, LANES), BF16), SDS((n, HEADS_W), BF16), SDS((n, HEADS_W), BF16),
                   SDS((SUBLANES, qkv_w), F32), SDS((SUBLANES, LANES), F32), SDS((SUBLANES, HEADS_W), F32)],
        scratch_shapes=[pltpu.VMEM((tt + 2 * SUBLANES, qkv_w), F32)], compiler_params=_params(2),
    )(proj, proj, proj, proj, proj, proj, dq, dq, dk, dk, dv, dv, db, dg, dqb, dkb, dlf, cw8, aux, lb8)


def _proj_bwd_x(pieces, wp_t, h, nw8, dh_res, tag):
    n = h.shape[0]
    tm = _pick(n, (256, 192, 128, 64))
    widths = [p.shape[1] for p in pieces]
    assert sum(widths) == PROJ_W

    def body(*refs):
        p_refs = refs[:len(pieces)]
        wt_ref, h_ref, nw_ref, dres_ref, dh_ref, dnw_ref = refs[len(pieces):]

        @pl.when(pl.program_id(0) == 0)
        def _():
            dnw_ref[...] = jnp.zeros_like(dnw_ref)

        dxn, off = None, 0
        for p_ref, w in zip(p_refs, widths):
            part = jnp.dot(p_ref[...], wt_ref[off:off + w, :], preferred_element_type=F32)
            dxn = part if dxn is None else dxn + part
            off += w
        _, vjp = jax.vjp(_rms, h_ref[...], nw_ref[0:1, :])
        dx, dnw = vjp(dxn)
        dh_ref[...] = dres_ref[...] + dx
        dnw_ref[0:1, :] += dnw

    r2 = lambda i: (i, 0)
    full = lambda i: (0, 0)
    return pl.pallas_call(
        body, name=f"proj_bwd_x_{tag}", grid=(n // tm,),
        in_specs=[BS((tm, w), r2) for w in widths] + [BS((PROJ_W, D_MODEL), full), BS((tm, D_MODEL), r2),
                                                      BS((SUBLANES, D_MODEL), full), BS((tm, D_MODEL), r2)],
        out_specs=[BS((tm, D_MODEL), r2), BS((SUBLANES, D_MODEL), full)],
        out_shape=[SDS((n, D_MODEL), F32), SDS((SUBLANES, D_MODEL), F32)], compiler_params=_params(1),
    )(*pieces, wp_t, h, nw8, dh_res)


def _proj_bwd_w(xn, pieces, tag):
    n = xn.shape[0]
    tm = _pick(n, (384, 256, 192, 128, 64))
    widths = [p.shape[1] for p in pieces]
    assert sum(widths) == PROJ_W

    def body(*refs):
        x_ref, p_refs = refs[0], refs[1:1 + len(pieces)]
        o_ref, acc_ref = refs[1 + len(pieces):]

        @pl.when(pl.program_id(0) == 0)
        def _():
            acc_ref[...] = jnp.zeros_like(acc_ref)

        off = 0
        for p_ref, w in zip(p_refs, widths):
            acc_ref[:, off:off + w] += _dg(x_ref[...], p_ref[...], ((0,), (0,)))
            off += w

        @pl.when(pl.program_id(0) == pl.num_programs(0) - 1)
        def _():
            pltpu.sync_copy(acc_ref, o_ref)

    r2 = lambda i: (i, 0)
    return pl.pallas_call(
        body, name=f"proj_bwd_w_{tag}", grid=(n // tm,),
        in_specs=[BS((tm, D_MODEL), r2)] + [BS((tm, w), r2) for w in widths], out_specs=BS(memory_space=pl.ANY),
        out_shape=SDS((D_MODEL, PROJ_W), F32), scratch_shapes=[pltpu.VMEM((D_MODEL, PROJ_W), F32)],
        compiler_params=_params(1),
    )(xn, *pieces)


def _adamw(w, g, m, v, name):
    lead, rows, cols = w.shape
    tr = _pick(rows, (256, 128, 64, 32, 16, 8, 4, 2, 1)) if rows > 256 else rows

    def body(w_ref, g_ref, m_ref, v_ref, d_ref, nm_ref, nv_ref):
        gr = g_ref[...]
        m_new = ADAM_B1 * m_ref[...] + (1.0 - ADAM_B1) * gr
        v_new = ADAM_B2 * v_ref[...] + (1.0 - ADAM_B2) * jnp.square(gr)
        m_hat = m_new / (1.0 - ADAM_B1 ** ADAM_STEP)
        v_hat = v_new / (1.0 - ADAM_B2 ** ADAM_STEP)
        d_ref[...] = -ADAM_LR * (m_hat / (jnp.sqrt(v_hat) + ADAM_EPS) + ADAM_WD * w_ref[...])
        nm_ref[...] = m_new
        nv_ref[...] = v_new

    blk = BS((None, tr, cols), lambda a, i: (a, i, 0))
    return pl.pallas_call(
        body, name=name, grid=(lead, rows // tr), in_specs=[blk] * 4, out_specs=[blk] * 3,
        out_shape=[SDS((lead, rows, cols), F32)] * 3, compiler_params=_params(2),
    )(w, g, m, v)


def _row8(v, width):
    v = jnp.atleast_2d(v).astype(F32)
    return jnp.pad(v, ((0, SUBLANES - v.shape[0]), (0, width - v.shape[1])))


REF_MISC = 1536
N_MISC = 2 * N_HEADS
LAYOUT_RUNS = ((0, REF_MISC, 0), (REF_MISC + N_MISC, REF_W, REF_MISC), (REF_MISC, REF_MISC + N_MISC, C_MISC))


def _to_layout(w_full):
    runs = [w_full[:, lo:hi] for lo, hi, _ in sorted(LAYOUT_RUNS, key=lambda run: run[2])]
    return jnp.concatenate(runs + [jnp.zeros((w_full.shape[0], PROJ_W - REF_W), w_full.dtype)], axis=1)


def _from_layout(dw, n_slabs):
    width = REF_W // n_slabs
    slabs = []
    for j in range(n_slabs):
        pieces = []
        for lo, hi, at in sorted(LAYOUT_RUNS):
            a, b = max(lo, j * width), min(hi, (j + 1) * width)
            if a < b:
                pieces.append(dw[:, at + a - lo:at + b - lo])
        slabs.append(jnp.concatenate(pieces, axis=1))
    return slabs


def _lower_bounds(lb):
    sm = jax.nn.softmax(lb.astype(F32), axis=0)
    return jnp.cumsum(sm, axis=0) - sm[0]


def kernel(x, meta_tokens, norm_w, w_in, conv_w, a_log, dt_bias, gnorm_a, gnorm_b, hgrn_lower_bounds, w_branch_a, w_branch_b, w_out, final_norm_w, loss_target, m_meta_tokens, m_norm_w, m_w_in, m_conv_w, m_a_log, m_dt_bias, m_gnorm_a, m_gnorm_b, m_hgrn_lower_bounds, m_w_branch_a, m_w_branch_b, m_w_out, m_final_norm_w, v_meta_tokens, v_norm_w, v_w_in, v_conv_w, v_a_log, v_dt_bias, v_gnorm_a, v_gnorm_b, v_hgrn_lower_bounds, v_w_branch_a, v_w_branch_b, v_w_out, v_final_norm_w):
    nseq, seq, _ = x.shape
    depth = norm_w.shape[0]
    t_len = N_PAD + N_META + seq
    n = nseq * t_len
    win_c, conv_c = w_in.shape[2], conv_w.shape[2]
    my = 4 * lax.axis_index("x") + 2 * lax.axis_index("y") + lax.axis_index("c")

    assert depth >= 2
    by_cols = lambda g: g.transpose(1, 2, 0, 3).reshape(g.shape[1], g.shape[2], N_DEV * g.shape[3])
    first = _all_gather_hbm([w_in[:1].astype(BF16), conv_w, meta_tokens], "gather_first")
    later_flight, later_token = _send_all_start(
        [w_in[1:].astype(BF16), w_branch_a.astype(BF16), w_branch_b.astype(BF16), w_out.astype(BF16)], False,
        "gather_later_start")
    w_in_full = [by_cols(first[0])]
    conv_full = by_cols(first[1])
    meta_full = first[2].transpose(1, 0, 2).reshape(N_META, D_MODEL)

    lb_all, lb_vjp = jax.vjp(_lower_bounds, hgrn_lower_bounds)

    h = jnp.concatenate([jnp.zeros((nseq, N_PAD, D_MODEL), F32),
                         jnp.broadcast_to(meta_full[None], (nseq, N_META, D_MODEL)), x], axis=1).reshape(n, D_MODEL)
    saved = []
    for l in range(depth):
        wp = _to_layout(w_in_full[0][0] if l == 0 else w_in_full[1][l - 1])
        nw8 = _row8(norm_w[l], D_MODEL)
        if l == 0:
            nw8 = nw8 + later_token[0:1, 0:1]
        cw8 = _row8(conv_full[l], 3 * HEADS_W)
        aux = _row8(jnp.stack([a_log[l], dt_bias[l]]), LANES)
        lb8 = _row8(lb_all[l], HEADS_W)
        gn8 = _row8(jnp.stack([gnorm_a[l], gnorm_b[l]]), LANES)
        proj, xn = _proj_fwd(h, nw8, wp, l)
        q, k, v, b, g, qb, kb, lf = _prep_fwd(proj, cw8, aux, lb8, nseq, t_len, l)
        oa, sck_a = _gdn_fwd(q, k, v, b, g, nseq, t_len, l)
        ob, sck_b = _hgrn_fwd(qb, kb, proj, C_BI // HEADS_W, lf, nseq, t_len, l)
        if l == 0:
            sent, landed = _send_all_wait(later_flight, ob, "gather_later_wait")
            landed = [lax.dynamic_update_slice(ld, own[None], (my,) + (0,) * own.ndim) for ld, own in zip(landed, sent)]
            w_in_full.append(by_cols(landed[0]))
            wa_full, wb_full = by_cols(landed[1]), by_cols(landed[2])
            wout_full = landed[3].transpose(1, 0, 2, 3).reshape(depth, D_MODEL, D_MODEL)
        wa_l, wb_l, wout_l = wa_full[l], wb_full[l], wout_full[l]
        h_next = _post_fwd(oa, ob, proj, h, gn8, wa_l, wb_l, wout_l, l)
        saved.append(dict(h=h, wp=wp, nw8=nw8, cw8=cw8, aux=aux, lb8=lb8, gn8=gn8, proj=proj, xn=xn, q=q, k=k, v=v, b=b,
                          wa=wa_l, wb=wb_l, wout=wout_l,
                          g=g, qb=qb, kb=kb, lf=lf, oa=oa, ob=ob, sck_a=sck_a, sck_b=sck_b))
        h = h_next

    dh, acc = _loss_head(h, _row8(final_norm_w, D_MODEL), loss_target, nseq, t_len)

    g_win, g_wa, g_wb, g_wout, g_conv, small = [], [], [], [], [], []

    def mixer_slabs(dwa_s, dwb_s, dwout_s):
        nl = len(dwa_s)
        rows = lambda a: jnp.stack(a).reshape(nl * HEADS_W, N_DEV, LANES).transpose(1, 0, 2)
        wout = jnp.stack(dwout_s).reshape(nl, N_DEV, LANES, D_MODEL).transpose(1, 0, 2, 3)
        return [jnp.concatenate([rows(dwa_s), rows(dwb_s)], axis=1).astype(BF16),
                wout.reshape(N_DEV, nl * LANES, D_MODEL).astype(BF16)]

    def win_slabs(per_layer, dtype):
        return jnp.stack([jnp.concatenate([sl[j] for sl in per_layer], axis=0) for j in range(N_DEV)]).astype(dtype)

    for l in reversed(range(depth)):
        s = saved[l]
        gn8, aux = s["gn8"], s["aux"]
        if l == 0:
            later_flight, later_token = _send_all_start(
                [win_slabs(g_win[::-1], BF16)] + mixer_slabs(g_wa[::-1], g_wb[::-1], g_wout[::-1]), True,
                "scatter_later_start")
            gn8 = gn8 + later_token[0:1, 0:1]
        doa, dob, dz, dbg, dga, dgb, dwa, dwb, dwout, dgn = _post_bwd(
            dh, s["oa"], s["ob"], s["proj"], s["h"], gn8, s["wa"], s["wb"], s["wa"].T, s["wb"].T, s["wout"].T, l)
        dq, dk, dv, db, dg = _gdn_bwd(s["q"], s["k"], s["v"], s["b"], s["g"], s["sck_a"], doa, nseq, t_len, l)
        dqb, dkb, dbi, dlf = _hgrn_bwd(s["qb"], s["kb"], s["proj"], C_BI // HEADS_W, s["lf"], s["sck_b"], dob, nseq,
                                       t_len, l)
        if l == 0:
            mixer_flight, mixer_token = _send_all_start(mixer_slabs([dwa], [dwb], [dwout]), True, "scatter_first_start")
            aux = aux + mixer_token[0:1, 0:1]
        dqkv, dmisc, dbq, dbf, dcw, daux, dlb = _prep_bwd(s["proj"], dq, dk, dv, db, dg, dqb, dkb, dlf, s["cw8"], aux,
                                                          s["lb8"], nseq, t_len, l)
        pieces = [dqkv, dz, dbq, dbf, dbi, dbg, dga, dgb, dmisc]
        dh, dnw = _proj_bwd_x(pieces, s["wp"].T, s["h"], s["nw8"], dh, l)
        dwp = _proj_bwd_w(s["xn"], pieces, l)
        g_win.append(_from_layout(dwp, N_DEV))
        g_wa.append(dwa)
        g_wb.append(dwb)
        g_wout.append(dwout)
        g_conv.append(dcw[:4])
        small.append((dnw[0], dgn[0], dgn[1], daux[0, :N_HEADS], daux[1, :N_HEADS], dlb[0]))
    g_conv.reverse()
    small.reverse()
    dh = dh.reshape(nseq, t_len, D_MODEL)
    grad_x = dh[:, N_PAD + N_META:]

    packed = jnp.concatenate([small[0][1], small[1][1], small[0][2], small[1][2], small[0][3], small[1][3],
                              small[0][4], small[1][4]])
    tile = jnp.concatenate([
        jnp.sum(dh[:, N_PAD:N_PAD + N_META], axis=0), _row8(jnp.stack([small[0][0], small[1][0], acc[0]]), D_MODEL),
        _row8(jnp.stack([small[0][5], small[1][5]]), D_MODEL), _row8(packed, D_MODEL), _row8(acc[1], D_MODEL)], axis=0)
    tile = _all_reduce_small(tile, "reduce_small")
    loss = jnp.sum(tile[40])
    g_meta = lax.dynamic_slice_in_dim(tile[0:N_META], my * LANES, LANES, axis=1)
    g_norm, g_final = tile[16:18], tile[18]
    (g_lb,) = lb_vjp(tile[24:26, :HEADS_W])
    r21 = tile[32]
    g_gna, g_gnb = r21[0:256].reshape(2, LANES), r21[256:512].reshape(2, LANES)
    g_alog, g_dtb = r21[512:520].reshape(2, N_HEADS), r21[520:528].reshape(2, N_HEADS)

    dconv = jnp.stack(g_conv)
    conv_slabs = dconv.reshape(depth * dconv.shape[1], N_DEV, conv_c).transpose(1, 0, 2)
    coords = jnp.stack([lax.axis_index("x"), lax.axis_index("y"), lax.axis_index("c")]).astype(jnp.int32)
    r_win, r_conv = _reduce_scatter([win_slabs(g_win[-1:], F32), conv_slabs], coords, "grads")

    def landed_sums(flight, tag):
        sent, landed = _send_all_wait(flight, dwp, f"{tag}_wait")
        landed = [lax.dynamic_update_slice(ld, lax.dynamic_index_in_dim(src, my, 0, keepdims=True), (my, 0, 0))
                  for ld, src in zip(landed, sent)]
        return [_sum_slabs(ld, f"{tag}_sum{i}") for i, ld in enumerate(landed)]

    l_win, l_ab, l_wout = landed_sums(later_flight, "scatter_later")
    r_ab, r_wout = landed_sums(mixer_flight, "scatter_first")
    both = lambda a, b, shape: jnp.concatenate([a.reshape(1, *shape[1:]), b.reshape(depth - 1, *shape[1:])])
    half, half_l = HEADS_W, (depth - 1) * HEADS_W
    mine = [both(r_win, l_win, w_in.shape), both(r_ab[:half], l_ab[:half_l], w_branch_a.shape),
            both(r_ab[half:], l_ab[half_l:], w_branch_b.shape), both(r_wout, l_wout, w_out.shape), r_conv]
    gseg = lambda i, shape: mine[i].reshape(shape)
    grads = {
        "meta_tokens": g_meta, "norm_w": g_norm, "w_in": gseg(0, w_in.shape), "conv_w": gseg(4, conv_w.shape),
        "a_log": g_alog, "dt_bias": g_dtb, "gnorm_a": g_gna, "gnorm_b": g_gnb, "hgrn_lower_bounds": g_lb,
        "w_branch_a": gseg(1, w_branch_a.shape), "w_branch_b": gseg(2, w_branch_b.shape), "w_out": gseg(3, w_out.shape),
        "final_norm_w": g_final}
    weights = {
        "meta_tokens": (meta_tokens, m_meta_tokens, v_meta_tokens), "norm_w": (norm_w, m_norm_w, v_norm_w),
        "w_in": (w_in, m_w_in, v_w_in), "conv_w": (conv_w, m_conv_w, v_conv_w), "a_log": (a_log, m_a_log, v_a_log),
        "dt_bias": (dt_bias, m_dt_bias, v_dt_bias), "gnorm_a": (gnorm_a, m_gnorm_a, v_gnorm_a),
        "gnorm_b": (gnorm_b, m_gnorm_b, v_gnorm_b),
        "hgrn_lower_bounds": (hgrn_lower_bounds, m_hgrn_lower_bounds, v_hgrn_lower_bounds),
        "w_branch_a": (w_branch_a, m_w_branch_a, v_w_branch_a), "w_branch_b": (w_branch_b, m_w_branch_b, v_w_branch_b),
        "w_out": (w_out, m_w_out, v_w_out), "final_norm_w": (final_norm_w, m_final_norm_w, v_final_norm_w)}
    names = list(weights)
    deltas, new_m, new_v = [], [], []
    for nm in names:
        w, m, v = weights[nm]
        view = (1,) * (3 - w.ndim) + w.shape
        d, m2, v2 = _adamw(w.reshape(view), grads[nm].reshape(view), m.reshape(view), v.reshape(view), f"adamw_{nm}")
        deltas.append(d.reshape(w.shape))
        new_m.append(m2.reshape(w.shape))
        new_v.append(v2.reshape(w.shape))
    return (loss, grad_x, *[grads[nm].reshape(weights[nm][0].shape) for nm in names], *deltas, *new_m, *new_v)
```

```python
import functools

import jax
import jax.numpy as jnp
import numpy as np
from jax import lax
from jax.experimental import pallas as pl
from jax.experimental.pallas import tpu as pltpu

F32 = jnp.float32
BF16 = jnp.bfloat16

D_MODEL = 1024
N_HEADS = 4
D_HEAD = 128
HEADS_W = N_HEADS * D_HEAD
N_META = 16
N_PAD = 48
GDN_CHUNK = 64
HGRN_CHUNK = 16
EPS = 1e-6
N_DEV = 8
LANES = 128
SUBLANES = 8
VMEM_LIMIT = 56 * 1024 * 1024

C_QKV, C_Z, C_BQ, C_BF, C_BI, C_BG, C_GA, C_GB, C_MISC = 0, 1536, 2048, 2560, 3072, 3584, 4096, 5120, 6144
PROJ_W = 6272
REF_W = 6152

ADAM_LR, ADAM_B1, ADAM_B2, ADAM_EPS, ADAM_WD, ADAM_STEP = 0.001, 0.9, 0.999, 1e-08, 0.01, 10

MESH = pl.DeviceIdType.MESH
SDS = jax.ShapeDtypeStruct
BS = pl.BlockSpec


def _params(n_axes):
    return pltpu.CompilerParams(dimension_semantics=("arbitrary",) * n_axes, vmem_limit_bytes=VMEM_LIMIT)


def _pick(n, cands):
    for c in cands:
        if n % c == 0:
            return c
    raise ValueError(f"no tile for {n} among {cands}")


def _iota2(shape, dim):
    return lax.broadcasted_iota(jnp.int32, shape, dim)


def _dg(a, b, dims):
    return lax.dot_general(a.astype(BF16), b.astype(BF16), (dims, ((), ())), preferred_element_type=F32)


def _bdg(a, b, ca, cb):
    return lax.dot_general(a.astype(BF16), b.astype(BF16), (((ca,), (cb,)), ((0,), (0,))), preferred_element_type=F32)


@jax.custom_vjp
def _bnn(a, b):
    return _bdg(a, b, 2, 1)


@jax.custom_vjp
def _bnt(a, b):
    return _bdg(a, b, 2, 2)


@jax.custom_vjp
def _btn(a, b):
    return _bdg(a, b, 1, 1)


_bnn.defvjp(lambda a, b: (_bnn(a, b), (a, b)), lambda r, g: (_bnt(g, r[1]), _btn(r[0], g)))
_bnt.defvjp(lambda a, b: (_bnt(a, b), (a, b)), lambda r, g: (_bnn(g, r[1]), _btn(g, r[0])))
_btn.defvjp(lambda a, b: (_btn(a, b), (a, b)), lambda r, g: (_bnt(r[1], g), _bnn(r[0], g)))


def _split2(x):
    hi = x.astype(BF16).astype(F32)
    return hi, x - hi


def _tri(bsz, n):
    return jnp.broadcast_to((_iota2((n, n), 0) >= _iota2((n, n), 1)).astype(F32), (bsz, n, n))


@jax.custom_vjp
def _cumsum_rows(x):
    tri = _tri(x.shape[0], x.shape[1])
    hi, lo = _split2(x)
    return _bdg(tri, hi, 2, 1) + _bdg(tri, lo, 2, 1)


def _cumsum_rows_bwd(_, g):
    tri = _tri(g.shape[0], g.shape[1])
    hi, lo = _split2(g)
    return (_bdg(tri, hi, 1, 1) + _bdg(tri, lo, 1, 1),)


_cumsum_rows.defvjp(lambda x: (_cumsum_rows(x), None), _cumsum_rows_bwd)


def _sigmoid(x):
    return jax.nn.sigmoid(x)


def _silu(x):
    return x * _sigmoid(x)


def _softplus(x):
    return jnp.maximum(x, 0.0) + jnp.log1p(jnp.exp(-jnp.abs(x)))


def _rms(x, w):
    return x * lax.rsqrt(jnp.mean(x * x, axis=-1, keepdims=True) + EPS) * w


@jax.custom_vjp
def _inv_unit_lower(lm):
    n = lm.shape[1]
    a = (_iota2((n, n), 0) == _iota2((n, n), 1)).astype(F32)[None] - lm
    steps = max(1, (n - 1).bit_length()) - 1
    p = _bnn(lm, lm)
    for i in range(steps):
        if i == steps - 1:
            a = a + _bnn(a, p)
        else:
            both = _bnn(jnp.concatenate([a, p], axis=1), p)
            a, p = a + both[:, :n], both[:, n:]
    return a


_inv_unit_lower.defvjp(lambda lm: (lambda a: (a, a))(_inv_unit_lower(lm)),
                       lambda a, g: (-_bnt(_btn(a, g), a),))


def _gdn_chunk(q, k, v, b_b, g_b, s):
    n, dv = q.shape[1], v.shape[2]
    r, c = _iota2((n, n), 0), _iota2((n, n), 1)
    causal, strict, eye = (r >= c)[None], (r > c)[None], (r == c)[None]
    g_cum = _cumsum_rows(g_b)
    g_i = g_cum[:, :, :n]
    g_j = jnp.sum(jnp.where(eye, g_i, 0.0), axis=1, keepdims=True)
    decay = jnp.where(causal, jnp.exp(jnp.where(causal, g_i - g_j, 0.0)), 0.0)
    e_g = jnp.exp(g_cum)
    kb = k * b_b
    kk = _bnt(jnp.concatenate([kb, q], axis=1), k)
    a_inv = _inv_unit_lower(jnp.where(strict, kk[:, :n] * decay, 0.0))
    uw = _bnn(a_inv, jnp.concatenate([v * b_b, kb * e_g], axis=2))
    ws = _bnn(jnp.concatenate([uw[:, :, dv:], q * e_g], axis=1), s)
    v_new = uw[:, :, :dv] - ws[:, :n]
    o = ws[:, n:] + _bnn(kk[:, n:] * decay, v_new)
    g_last = g_cum[:, n - 1:n, :]
    s_new = s * jnp.exp(g_last) +_btn(k * jnp.exp(g_last - g_cum), v_new)
    return o, s_new


@functools.partial(jax.custom_vjp, nondiff_argnums=(1, 2))
def _row(x, j, n):
    return x[:, j:j + 1, :]


def _row_bwd(j, n, _, g):
    return (jnp.where(_iota2((1, n, 1), 1) == j, g, 0.0),)


_row.defvjp(lambda x, j, n: (_row(x, j, n), None), _row_bwd)


def _hgrn_pairs(q, k, v, b_cum):
    n = q.shape[1]
    half = n // 2
    parts = []
    for lo in (0, half):
        qs, bs = q[:, lo:], b_cum[:, lo:]
        rows = _iota2((1, n - lo, 1), 1) + lo
        acc = jnp.zeros_like(qs)
        for j in range(lo, n if lo else half):
            p = jnp.exp(jnp.where(rows >= j, bs - _row(b_cum, j, n), -1e30))
            acc = acc + jnp.sum(qs * _row(k, j, n) * p, axis=2, keepdims=True) * _row(v, j, n)
        parts.append(acc)
    return parts[0] + jnp.concatenate([jnp.zeros_like(parts[1]), parts[1]], axis=1)


def _hgrn_block(q, k, v, lf, st):
    n, rows = HGRN_CHUNK, q.shape[1]
    b_cum = _cumsum_rows(lf)
    outs = []
    for c in range(rows // n):
        rs = slice(c * n, (c + 1) * n)
        o = _hgrn_pairs(q[:, rs], k[:, rs], v[:, rs], b_cum[:, rs])
        if c:
            b_c = _row(b_cum, c * n - 1, rows)
            scores = _bnt(q[:, rs] * jnp.exp(b_cum[:, rs] - b_c), k[:, :c * n] * jnp.exp(b_c - b_cum[:, :c * n]))
            o = o + _bnn(scores, v[:, :c * n])
        outs.append(o)
    b_last = _row(b_cum, rows - 1, rows)
    o = _bnt(q * jnp.exp(b_cum), st) + jnp.concatenate(outs, axis=1)
    return o, st * jnp.exp(b_last) + _btn(v, k * jnp.exp(b_last - b_cum))


def _l2n_act(y, scale):
    a = _silu(y)
    return a * lax.rsqrt(jnp.sum(a * a, axis=-1, keepdims=True) + EPS) * scale


def _col(x, lane):
    return jnp.sum(jnp.where(_iota2(x.shape, 1) == lane, x, 0.0), axis=1, keepdims=True)


def _elem(x, row, lane):
    m = (_iota2(x.shape, 0) == row) & (_iota2(x.shape, 1) == lane)
    return jnp.sum(jnp.sum(jnp.where(m, x, 0.0), axis=1, keepdims=True), axis=0, keepdims=True)


def _gdn_gates(misc, aux, real, head):
    beta = _sigmoid(_col(misc, head))
    g = -jnp.exp(_elem(aux, 0, head)) * _softplus(_col(misc, N_HEADS + head) + _elem(aux, 1, head))
    g = jnp.where(real, g, 0.0)
    shape = (misc.shape[0], D_HEAD)
    return jnp.broadcast_to(beta, shape), jnp.broadcast_to(g, shape)


def _hgrn_prep(bq, bf, lb, real):
    qb = _silu(bq) * (D_HEAD ** -0.5)
    log_sig = jnp.minimum(bf, 0.0) - jnp.log1p(jnp.exp(-jnp.abs(bf)))
    pos = lb > 0.0
    lbs = jnp.where(pos, lb, 0.5)
    a = jnp.log(lbs)
    b = jnp.log1p(-lbs) + log_sig
    lae = jnp.maximum(a, b) + jnp.log1p(jnp.exp(-jnp.abs(a - b)))
    lf = jnp.where(pos, lae, log_sig)
    kb = jnp.where(pos, 1.0 - lbs, 1.0) * _sigmoid(-bf)
    return qb, jnp.where(real, kb, 0.0), jnp.where(real, lf, 0.0)


def _gated_norm(o, z, gw):
    return o * lax.rsqrt(jnp.mean(o * o, axis=-1, keepdims=True) + EPS) * gw * _silu(z)


def _shift_down(x, j):
    return x if j == 0 else pltpu.roll(x, j, 0)


def _shift_up(x, j):
    return x if j == 0 else pltpu.roll(x, x.shape[0] - j, 0)


def _all_gather_hbm(blocks, name):
    na = len(blocks)

    def body(*refs):
        x_refs, out_refs = refs[:na], refs[na:2 * na]
        send_sems, recv_sems, local_sems = refs[2 * na:]
        mx, my, mc = lax.axis_index("x"), lax.axis_index("y"), lax.axis_index("c")
        me, sibling = (mx, my, mc), (mx, my, 1 - mc)
        chips = [(1 - mx, my), (mx, 1 - my), (1 - mx, 1 - my)]

        def slab(a, px, py, pc):
            return out_refs[a].at[4 * px + 2 * py + pc]

        def copy(a, k, blk, to, own=False):
            return pltpu.make_async_remote_copy(
                src_ref=x_refs[a] if own else slab(a, *blk), dst_ref=slab(a, *blk),
                send_sem=send_sems.at[7 * a + k], recv_sem=recv_sems.at[7 * a + k], device_id=to, device_id_type=MESH)

        mine = [pltpu.make_async_copy(x_refs[a], slab(a, *me), local_sems.at[a]) for a in range(na)]
        for cp in mine:
            cp.start()
        first = [copy(a, 0, me, sibling, own=True) for a in range(na)]
        first += [copy(a, 1 + j, me, (*chip, mc), own=True) for j, chip in enumerate(chips) for a in range(na)]
        for cp in first:
            cp.start()
        passed = []
        for j, chip in enumerate(chips):
            for a in range(na):
                copy(a, 1 + j, (*chip, mc), me).wait_recv()
                passed.append(copy(a, 4 + j, (*chip, mc), sibling))
                passed[-1].start()
        for a in range(na):
            copy(a, 0, sibling, me).wait_recv()
            for j, chip in enumerate(chips):
                copy(a, 4 + j, (*chip, 1 - mc), me).wait_recv()
        for cp in first + passed:
            cp.wait_send()
        for cp in mine:
            cp.wait()

    hbm = BS(memory_space=pl.ANY)
    return pl.pallas_call(
        body, name=name, out_shape=[SDS((N_DEV, *b.shape), b.dtype) for b in blocks],
        in_specs=[hbm] * na, out_specs=[hbm] * na,
        scratch_shapes=[pltpu.SemaphoreType.DMA((7 * na,)), pltpu.SemaphoreType.DMA((7 * na,)),
                        pltpu.SemaphoreType.DMA((na,))],
    )(*blocks)


def _all_reduce_small(block, name):
    r, c = block.shape

    def body(x_ref, out_ref, buf, send_sems, recv_sems):
        mx, my, mc = lax.axis_index("x"), lax.axis_index("y"), lax.axis_index("c")
        me, sibling = (mx, my, mc), (mx, my, 1 - mc)
        chips = [(1 - mx, my), (mx, 1 - my), (1 - mx, 1 - my)]

        def slab(px, py, pc):
            return buf.at[4 * px + 2 * py + pc]

        def copy(k, blk, to, src=None):
            return pltpu.make_async_remote_copy(
                src_ref=slab(*blk) if src is None else src, dst_ref=slab(*blk),
                send_sem=send_sems.at[k], recv_sem=recv_sems.at[k], device_id=to, device_id_type=MESH)

        first = [copy(0, me, sibling, src=x_ref)]
        first += [copy(1 + j, me, (*chip, mc), src=x_ref) for j, chip in enumerate(chips)]
        for cp in first:
            cp.start()
        passed = [copy(4 + j, (*chip, mc), sibling) for j, chip in enumerate(chips)]
        for j, chip in enumerate(chips):
            copy(1 + j, (*chip, mc), me).wait_recv()
            passed[j].start()
        copy(0, sibling, me).wait_recv()
        for j, chip in enumerate(chips):
            copy(4 + j, (*chip, 1 - mc), me).wait_recv()
        for cp in first + passed:
            cp.wait_send()
        buf[4 * mx + 2 * my + mc] = x_ref[...]
        acc = buf[0]
        for d in range(1, N_DEV):
            acc = acc + buf[d]
        out_ref[...] = acc

    return pl.pallas_call(
        body, name=name, out_shape=SDS((r, c), F32),
        in_specs=[BS(memory_space=pltpu.VMEM)], out_specs=BS(memory_space=pltpu.VMEM),
        scratch_shapes=[pltpu.VMEM((N_DEV, r, c), F32), pltpu.SemaphoreType.DMA((7,)), pltpu.SemaphoreType.DMA((7,))],
    )(block)


HBM_SPEC = BS(memory_space=pltpu.HBM)
SEM_SPEC = BS(memory_space=pltpu.SEMAPHORE)
SIDE_EFFECT = pltpu.SideEffectType.DATAFLOW_SIDE_EFFECTING


def _peer(rel):
    flip = lambda v, bit: 1 - v if bit else v
    return (flip(lax.axis_index("x"), rel >> 2 & 1), flip(lax.axis_index("y"), rel >> 1 & 1),
            flip(lax.axis_index("c"), rel & 1))


def _send_all_start(blocks, scatter, name):
    na = len(blocks)
    shapes = [b.shape[1:] if scatter else b.shape for b in blocks]

    def body(*refs):
        srcs, lands = refs[:na], refs[na:2 * na]
        send_sems, recv_sems, token = refs[2 * na], refs[2 * na + 1], refs[-1]
        me = 4 * lax.axis_index("x") + 2 * lax.axis_index("y") + lax.axis_index("c")
        for a in range(na):
            for rel in range(1, N_DEV):
                px, py, pc = _peer(rel)
                pltpu.make_async_remote_copy(
                    src_ref=srcs[a].at[4 * px + 2 * py + pc] if scatter else srcs[a], dst_ref=lands[a].at[me],
                    send_sem=send_sems.at[7 * a + rel - 1], recv_sem=recv_sems.at[7 * a + rel - 1],
                    device_id=(px, py, pc), device_id_type=MESH).start()
        token[...] = jnp.zeros_like(token)

    lands = [lax.empty((N_DEV, *s), b.dtype) for s, b in zip(shapes, blocks)]
    res = pl.pallas_call(
        body, name=name,
        out_shape=([pltpu.SemaphoreType.DMA((7 * na,)), pltpu.SemaphoreType.DMA((7 * na,))]
                   + [pltpu.HBM(b.shape, b.dtype) for b in blocks] + [pltpu.HBM(ld.shape, ld.dtype) for ld in lands]
                   + [SDS((SUBLANES, LANES), F32)]),
        in_specs=[HBM_SPEC] * (2 * na), out_specs=[SEM_SPEC, SEM_SPEC] + [HBM_SPEC] * (2 * na) + [BS(memory_space=pltpu.VMEM)],
        input_output_aliases={i: 2 + i for i in range(2 * na)},
        compiler_params=pltpu.CompilerParams(has_side_effects=SIDE_EFFECT),
    )(*[pltpu.with_memory_space_constraint(b, pltpu.HBM) for b in blocks],
      *[pltpu.with_memory_space_constraint(ld, pltpu.HBM) for ld in lands])
    return dict(send=res[0], recv=res[1], srcs=res[2:2 + na], lands=res[2 + na:2 + 2 * na], scatter=scatter), res[-1]


def _send_all_wait(flight, after, name):
    na = len(flight["srcs"])

    def body(*refs):
        srcs, lands = refs[:na], refs[na:2 * na]
        send_sems, recv_sems = refs[2 * na], refs[2 * na + 1]
        for a in range(na):
            for rel in range(1, N_DEV):
                cp = pltpu.make_async_remote_copy(
                    src_ref=srcs[a].at[0] if flight["scatter"] else srcs[a], dst_ref=lands[a].at[0],
                    send_sem=send_sems.at[7 * a + rel - 1], recv_sem=recv_sems.at[7 * a + rel - 1],
                    device_id=_peer(rel), device_id_type=MESH)
                cp.wait_send()
                cp.wait_recv()

    arrays = list(flight["srcs"]) + list(flight["lands"])
    res = pl.pallas_call(
        body, name=name, out_shape=[pltpu.HBM(a.shape, a.dtype) for a in arrays],
        in_specs=[HBM_SPEC] * (2 * na) + [SEM_SPEC, SEM_SPEC, BS(memory_space=pl.ANY)], out_specs=[HBM_SPEC] * (2 * na),
        input_output_aliases={i: i for i in range(2 * na)},
        compiler_params=pltpu.CompilerParams(has_side_effects=SIDE_EFFECT),
    )(*arrays, flight["send"], flight["recv"], after)
    return res[:na], res[na:]


def _sum_slabs(land, name):
    _, r, c = land.shape
    tr = _pick(r, (256, 128, 64, 32, 16, 8))

    def body(l_ref, o_ref):
        acc = l_ref[0].astype(F32)
        for d in range(1, N_DEV):
            acc = acc + l_ref[d].astype(F32)
        o_ref[...] = acc

    return pl.pallas_call(
        body, name=name, grid=(r // tr,), out_shape=SDS((r, c), F32),
        in_specs=[BS((N_DEV, tr, c), lambda j: (0, j, 0))], out_specs=BS((tr, c), lambda j: (j, 0)),
        compiler_params=_params(1),
    )(land)


def _exchange(bufs, flip, paired, name):
    na, n = len(bufs), bufs[0].shape[0]
    axis = ("x", "y", "c")[flip]

    def body(*refs):
        g_refs, out_refs = refs[:na], refs[na:2 * na]
        send_sems, recv_sems = refs[2 * na:]
        pos = [lax.axis_index("x"), lax.axis_index("y"), lax.axis_index("c")]
        pos[flip] = 1 - pos[flip]
        other = 1 - lax.axis_index(axis)
        copies = [pltpu.make_async_remote_copy(
            src_ref=g_refs[a].at[i, other] if paired else g_refs[a].at[i], dst_ref=out_refs[a].at[i],
            send_sem=send_sems.at[n * a + i], recv_sem=recv_sems.at[n * a + i], device_id=tuple(pos),
            device_id_type=MESH) for i in range(n) for a in range(na)]
        for cp in copies:
            cp.start()
        for cp in copies:
            cp.wait_recv()
        for cp in copies:
            cp.wait_send()

    hbm = BS(memory_space=pl.ANY)
    return pl.pallas_call(
        body, name=name, out_shape=[SDS((n, *b.shape[(2 if paired else 1):]), b.dtype) for b in bufs],
        in_specs=[hbm] * na, out_specs=[hbm] * na,
        scratch_shapes=[pltpu.SemaphoreType.DMA((n * na,)), pltpu.SemaphoreType.DMA((n * na,))],
    )(*bufs)


def _rs_tile(r):
    return _pick(r, (704, 512, 352, 256, 192, 128, 64, 32, 16, 8))


def _rs_add_c(g4, recv, coords, name):
    _, _, r, c = g4.shape
    tr = _rs_tile(r)

    def body(co_ref, a0_ref, a1_ref, b0_ref, b1_ref, keep_ref, send_ref):
        s0 = a0_ref[...] + b0_ref[...]
        s1 = a1_ref[...] + b1_ref[...]
        mine = co_ref[1] == 0
        keep_ref[...] = jnp.where(mine, s0, s1)
        send_ref[...] = jnp.where(mine, s1, s0).astype(BF16)

    blk = lambda yy: BS((None, None, tr, c), functools.partial(lambda i, j, co, yy: (2 * i + yy, co[2], j, 0), yy=yy))
    rblk = lambda yy: BS((None, tr, c), functools.partial(lambda i, j, co, yy: (2 * i + yy, j, 0), yy=yy))
    out = BS((None, tr, c), lambda i, j, co: (i, j, 0))
    return pl.pallas_call(
        body, name=name, out_shape=[SDS((2, r, c), F32), SDS((2, r, c), BF16)],
        grid_spec=pltpu.PrefetchScalarGridSpec(num_scalar_prefetch=1, grid=(2, r // tr),
                                               in_specs=[blk(0), blk(1), rblk(0), rblk(1)], out_specs=[out, out]),
        compiler_params=_params(2),
    )(coords, g4, g4, recv, recv)


def _rs_add_y(kept, recv, coords, name):
    _, r, c = kept.shape
    tr = _rs_tile(r)

    def body(co_ref, a_ref, b_ref, keep_ref, send_ref):
        s0 = a_ref[0] + b_ref[0].astype(F32)
        s1 = a_ref[1] + b_ref[1].astype(F32)
        mine = co_ref[0] == 0
        keep_ref[...] = jnp.where(mine, s0, s1)
        send_ref[0] = jnp.where(mine, s1, s0).astype(BF16)

    blk = BS((2, tr, c), lambda j, co: (0, j, 0))
    return pl.pallas_call(
        body, name=name, out_shape=[SDS((r, c), F32), SDS((1, r, c), BF16)],
        grid_spec=pltpu.PrefetchScalarGridSpec(num_scalar_prefetch=1, grid=(r // tr,), in_specs=[blk, blk],
                                               out_specs=[BS((tr, c), lambda j, co: (j, 0)),
                                                          BS((1, tr, c), lambda j, co: (0, j, 0))]),
        compiler_params=_params(1),
    )(coords, kept, recv)


def _rs_add_x(kept, recv, name):
    r, c = kept.shape
    tr = _rs_tile(r)

    def body(a_ref, b_ref, o_ref):
        o_ref[...] = a_ref[...] + b_ref[0].astype(F32)

    return pl.pallas_call(
        body, name=name, grid=(r // tr,), out_shape=SDS((r, c), F32),
        in_specs=[BS((tr, c), lambda j: (j, 0)), BS((1, tr, c), lambda j: (0, j, 0))],
        out_specs=BS((tr, c), lambda j: (j, 0)), compiler_params=_params(1),
    )(kept, recv)


def _reduce_scatter(arrays, coords, tag):
    ids = range(len(arrays))
    g4 = [a.reshape(4, 2, *a.shape[1:]) for a in arrays]
    got = _exchange(g4, 2, True, f"rs_c_{tag}")
    kept, send = zip(*[_rs_add_c(g4[i], got[i], coords, f"rs_c_add_{tag}{i}") for i in ids])
    got = _exchange(list(send), 1, False, f"rs_y_{tag}")
    kept, send = zip(*[_rs_add_y(kept[i], got[i], coords, f"rs_y_add_{tag}{i}") for i in ids])
    got = _exchange(list(send), 0, False, f"rs_x_{tag}")
    return [_rs_add_x(kept[i], got[i], f"rs_x_add_{tag}{i}") for i in ids]


def _proj_fwd(h, nw8, wp, tag):
    n = h.shape[0]
    tm = _pick(n, (1408, 768, 512, 384, 256, 192, 128, 64))
    tn = 896

    def body(h_ref, nw_ref, w_ref, proj_ref, xn_ref):
        @pl.when(pl.program_id(1) == 0)
        def _():
            xn_ref[...] = _rms(h_ref[...], nw_ref[0:1, :]).astype(BF16)

        proj_ref[...] = jnp.dot(xn_ref[...], w_ref[...], preferred_element_type=F32)

    return pl.pallas_call(
        body, name=f"proj_fwd_{tag}", grid=(n // tm, PROJ_W // tn),
        in_specs=[BS((tm, D_MODEL), lambda i, j: (i, 0)), BS((SUBLANES, D_MODEL), lambda i, j: (0, 0)),
                  BS((D_MODEL, tn), lambda i, j: (0, j))],
        out_specs=[BS((tm, tn), lambda i, j: (i, j)), BS((tm, D_MODEL), lambda i, j: (i, 0))],
        out_shape=[SDS((n, PROJ_W), F32), SDS((n, D_MODEL), BF16)], compiler_params=_params(2),
    )(h, nw8, wp)


def _conv_ext(x_ext, cw_ref):
    y = x_ext * cw_ref[3:4, :]
    for k in range(3):
        y = y + _shift_down(x_ext, 3 - k) * cw_ref[k:k + 1, :]
    return y[SUBLANES:]


def _prep_fwd(proj, cw8, aux, lb8, nseq, t_len, tag):
    n = proj.shape[0]
    tt = _pick(t_len, (192, 128, 64))
    nt_ = t_len // tt
    qkv_w = 3 * HEADS_W

    def body(cur_ref, prev_ref, misc_ref, bq_ref, bf_ref, cw_ref, aux_ref, lb_ref,
             q_ref, k_ref, v_ref, b_ref, g_ref, qb_ref, kb_ref, lf_ref, ext_ref):
        t = pl.program_id(1)
        ext_ref[0:SUBLANES, :] = jnp.where(t == 0, 0.0, prev_ref[...])
        ext_ref[SUBLANES:, :] = cur_ref[...]
        y = ext_ref[SUBLANES:, :] * cw_ref[3:4, :]
        for kk in range(3):
            y = y + ext_ref[SUBLANES - 3 + kk:SUBLANES - 3 + kk + tt, :] * cw_ref[kk:kk + 1, :]
        real = (t * tt + _iota2((tt, 1), 0)) >= N_PAD
        misc = misc_ref[...]
        auxv = aux_ref[...]
        for hd in range(N_HEADS):
            sl = slice(hd * D_HEAD, (hd + 1) * D_HEAD)
            q_ref[:, sl] = _l2n_act(y[:, sl], D_HEAD ** -0.5)
            k_ref[:, sl] = _l2n_act(y[:, HEADS_W + hd * D_HEAD:HEADS_W + (hd + 1) * D_HEAD], 1.0)
            v_ref[:, sl] = _silu(y[:, 2 * HEADS_W + hd * D_HEAD:2 * HEADS_W + (hd + 1) * D_HEAD])
            b_ref[:, sl], g_ref[:, sl] = _gdn_gates(misc, auxv, real, hd)
        qb_ref[...], kb_ref[...], lf_ref[...] = _hgrn_prep(bq_ref[...], bf_ref[...], lb_ref[0:1, :], real)

    rb = tt // SUBLANES
    row = lambda s, t: s * nt_ + t
    wide = BS((tt, HEADS_W), lambda s, t: (row(s, t), 0))
    return pl.pallas_call(
        body, name=f"prep_fwd_{tag}", grid=(nseq, nt_),
        in_specs=[BS((tt, qkv_w), lambda s, t: (row(s, t), 0)),
                  BS((SUBLANES, qkv_w), lambda s, t: (jnp.maximum(row(s, t) * rb - 1, 0), 0)),
                  BS((tt, LANES), lambda s, t: (row(s, t), C_MISC // LANES)),
                  BS((tt, HEADS_W), lambda s, t: (row(s, t), C_BQ // HEADS_W)),
                  BS((tt, HEADS_W), lambda s, t: (row(s, t), C_BF // HEADS_W)),
                  BS((SUBLANES, qkv_w), lambda s, t: (0, 0)), BS((SUBLANES, LANES), lambda s, t: (0, 0)),
                  BS((SUBLANES, HEADS_W), lambda s, t: (0, 0))],
        out_specs=[wide] * 8, out_shape=[SDS((n, HEADS_W), F32)] * 8,
        scratch_shapes=[pltpu.VMEM((tt + SUBLANES, qkv_w), F32)], compiler_params=_params(2),
    )(proj, proj, proj, proj, proj, cw8, aux, lb8)


GDN_SEQS = 4
HGRN_SEQS = 2


def _seq_block(nseq, most):
    return max(s for s in (1, 2, 4) if s <= most and nseq % s == 0)


def _to_chains(x):
    return jnp.concatenate([x[:, :, hd * D_HEAD:(hd + 1) * D_HEAD] for hd in range(N_HEADS)], axis=0)


def _from_chains(ref, rows, val):
    sb = val.shape[0] // N_HEADS
    for hd in range(N_HEADS):
        ref[:, rows, hd * D_HEAD:(hd + 1) * D_HEAD] = val[hd * sb:(hd + 1) * sb].astype(ref.dtype)


def _mixers_fwd(q, k, v, b, g, qb, kb, vb, vb_col, lf, nseq, t_len, tag):
    sb, hs = _seq_block(nseq, GDN_SEQS), _seq_block(nseq, HGRN_SEQS)
    nc = t_len // GDN_CHUNK
    chains = N_HEADS * sb

    def body(q_ref, k_ref, v_ref, b_ref, g_ref, qb_ref, kb_ref, vb_ref, lf_ref, oa_ref, ob_ref, cka_ref, ckb_ref,
             sa_ref, sb_ref):
        @pl.when(pl.program_id(1) == 0)
        def _():
            sa_ref[...] = jnp.zeros_like(sa_ref)
            sb_ref[...] = jnp.zeros_like(sb_ref)

        s = sa_ref[...]
        cka_ref[...] = s
        o, s_new = _gdn_chunk(*[_to_chains(r[...]) for r in (q_ref, k_ref, v_ref, b_ref, g_ref)], s)
        _from_chains(oa_ref, slice(None), o)
        sa_ref[...] = s_new
        for part in range(sb // hs):
            seqs, ch = slice(part * hs, (part + 1) * hs), slice(part * N_HEADS * hs, (part + 1) * N_HEADS * hs)
            s = sb_ref[ch]
            ckb_ref[ch] = s
            o, s_new = _hgrn_block(*[_to_chains(r[seqs]) for r in (qb_ref, kb_ref, vb_ref, lf_ref)], s)
            for hd in range(N_HEADS):
                ob_ref[seqs, :, hd * D_HEAD:(hd + 1) * D_HEAD] = o[hd * hs:(hd + 1) * hs]
            sb_ref[ch] = s_new

    blk = lambda cb: BS((sb, GDN_CHUNK, HEADS_W), lambda p, c: (p, c, cb))
    ck_spec = BS((None, None, chains, D_HEAD, D_HEAD), lambda p, c: (p, c, 0, 0, 0))
    ck_shape = SDS((nseq // sb, nc, chains, D_HEAD, D_HEAD), F32)
    view = lambda a: a.reshape(nseq, t_len, a.shape[1])
    oa, ob, cka, ckb = pl.pallas_call(
        body, name=f"mixers_fwd_{tag}", grid=(nseq // sb, nc),
        in_specs=[blk(0)] * 7 + [blk(vb_col), blk(0)], out_specs=[blk(0), blk(0), ck_spec, ck_spec],
        out_shape=[SDS((nseq, t_len, HEADS_W), F32)] * 2 + [ck_shape] * 2,
        scratch_shapes=[pltpu.VMEM((chains, D_HEAD, D_HEAD), F32)] * 2, compiler_params=_params(2),
    )(*[view(a) for a in (q, k, v, b, g, qb, kb, vb, lf)])
    return oa.reshape(-1, HEADS_W), ob.reshape(-1, HEADS_W), cka, ckb


def _mixers_bwd(q, k, v, b, g, qb, kb, vb, vb_col, lf, cka, ckb, doa, dob, nseq, t_len, tag):
    sb, hs = _seq_block(nseq, GDN_SEQS), _seq_block(nseq, HGRN_SEQS)
    nc = t_len // GDN_CHUNK
    chains = N_HEADS * sb

    def body(q_ref, k_ref, v_ref, b_ref, g_ref, qb_ref, kb_ref, vb_ref, lf_ref, doa_ref, dob_ref, cka_ref, ckb_ref,
             dq_ref, dk_ref, dv_ref, db_ref, dg_ref, dqb_ref, dkb_ref, dvb_ref, dlf_ref, dsa_ref, dsb_ref):
        @pl.when(pl.program_id(1) == 0)
        def _():
            dsa_ref[...] = jnp.zeros_like(dsa_ref)
            dsb_ref[...] = jnp.zeros_like(dsb_ref)

        _, vjp = jax.vjp(_gdn_chunk, *[_to_chains(r[...]) for r in (q_ref, k_ref, v_ref, b_ref, g_ref)], cka_ref[...])
        grads = vjp((_to_chains(doa_ref[...]), dsa_ref[...]))
        for ref, val in zip((dq_ref, dk_ref, dv_ref, db_ref, dg_ref), grads[:5]):
            _from_chains(ref, slice(None), val)
        dsa_ref[...] = grads[5]
        for part in range(sb // hs):
            seqs, ch = slice(part * hs, (part + 1) * hs), slice(part * N_HEADS * hs, (part + 1) * N_HEADS * hs)
            _, vjp = jax.vjp(_hgrn_block, *[_to_chains(r[seqs]) for r in (qb_ref, kb_ref, vb_ref, lf_ref)], ckb_ref[ch])
            grads = vjp((_to_chains(dob_ref[seqs]), dsb_ref[ch]))
            for ref, val in zip((dqb_ref, dkb_ref, dvb_ref, dlf_ref), grads[:4]):
                for hd in range(N_HEADS):
                    ref[seqs, :, hd * D_HEAD:(hd + 1) * D_HEAD] = val[hd * hs:(hd + 1) * hs].astype(ref.dtype)
            dsb_ref[ch] = grads[4]

    blk = lambda cb: BS((sb, GDN_CHUNK, HEADS_W), lambda p, c: (p, nc - 1 - c, cb))
    ck_spec = BS((None, None, chains, D_HEAD, D_HEAD), lambda p, c: (p, nc - 1 - c, 0, 0, 0))
    view = lambda a: a.reshape(nseq, t_len, a.shape[1])
    dts = [F32] * 7 + [BF16, F32]
    res = pl.pallas_call(
        body, name=f"mixers_bwd_{tag}", grid=(nseq // sb, nc),
        in_specs=[blk(0)] * 7 + [blk(vb_col), blk(0), blk(0), blk(0), ck_spec, ck_spec], out_specs=[blk(0)] * 9,
        out_shape=[SDS((nseq, t_len, HEADS_W), dt) for dt in dts],
        scratch_shapes=[pltpu.VMEM((chains, D_HEAD, D_HEAD), F32)] * 2, compiler_params=_params(2),
    )(*[view(a) for a in (q, k, v, b, g, qb, kb, vb, lf, doa, dob)], cka, ckb)
    return [r.reshape(-1, HEADS_W) for r in res]


def _post_values(oa_ref, ob_ref, z_ref, bg_ref, ga_ref, gb_ref, gn_ref, wa_ref, wb_ref, ya_ref, yb_ref):
    for hd in range(N_HEADS):
        sl = slice(hd * D_HEAD, (hd + 1) * D_HEAD)
        ya_ref[:, sl] = _gated_norm(oa_ref[:, sl], z_ref[:, sl], gn_ref[0:1, :]).astype(BF16)
        yb_ref[:, sl] = _gated_norm(ob_ref[:, sl], bg_ref[:, sl], gn_ref[1:2, :]).astype(BF16)
    pa = jnp.dot(ya_ref[...], wa_ref[...], preferred_element_type=F32)
    pb = jnp.dot(yb_ref[...], wb_ref[...], preferred_element_type=F32)
    return pa, pb, _sigmoid(ga_ref[...]), _sigmoid(gb_ref[...])


def _post_specs(tm):
    r2 = lambda i: (i, 0)
    return [BS((tm, HEADS_W), r2), BS((tm, HEADS_W), r2),
            BS((tm, HEADS_W), lambda i: (i, C_Z // HEADS_W)), BS((tm, HEADS_W), lambda i: (i, C_BG // HEADS_W)),
            BS((tm, D_MODEL), lambda i: (i, C_GA // D_MODEL)), BS((tm, D_MODEL), lambda i: (i, C_GB // D_MODEL)),
            BS((tm, D_MODEL), r2), BS((SUBLANES, LANES), lambda i: (0, 0))]


def _post_fwd(oa, ob, proj, h, gn8, wa, wb, wout, tag):
    n = h.shape[0]
    tm = _pick(n, (256, 192, 128, 64))

    def body(oa_ref, ob_ref, z_ref, bg_ref, ga_ref, gb_ref, h_ref, gn_ref, wa_ref, wb_ref, wout_ref, out_ref,
             ya_ref, yb_ref):
        pa, pb, sa, sb = _post_values(oa_ref, ob_ref, z_ref, bg_ref, ga_ref, gb_ref, gn_ref, wa_ref, wb_ref,
                                      ya_ref, yb_ref)
        mixed = (sa * pa + sb * pb).astype(BF16)
        out_ref[...] = h_ref[...] + jnp.dot(mixed, wout_ref[...], preferred_element_type=F32)

    full = lambda i: (0, 0)
    return pl.pallas_call(
        body, name=f"post_fwd_{tag}", grid=(n // tm,),
        in_specs=_post_specs(tm) + [BS((HEADS_W, D_MODEL), full), BS((HEADS_W, D_MODEL), full),
                                    BS((D_MODEL, D_MODEL), full)],
        out_specs=BS((tm, D_MODEL), lambda i: (i, 0)), out_shape=SDS((n, D_MODEL), F32),
        scratch_shapes=[pltpu.VMEM((tm, HEADS_W), BF16), pltpu.VMEM((tm, HEADS_W), BF16)], compiler_params=_params(1),
    )(oa, ob, proj, proj, proj, proj, h, gn8, wa, wb, wout)


def _post_bwd(dh, oa, ob, proj, h, gn8, wa, wb, wa_t, wb_t, wout_t, tag):
    n = h.shape[0]
    tm = _pick(n, (256, 192, 128, 64))

    def body(dh_ref, oa_ref, ob_ref, z_ref, bg_ref, ga_ref, gb_ref, h_ref, gn_ref, wa_ref, wb_ref, wat_ref, wbt_ref,
             woutt_ref, doa_ref, dob_ref, dz_ref, dbg_ref, dga_ref, dgb_ref, dwa_ref, dwb_ref, dwout_ref, dgn_ref,
             ya_ref, yb_ref):
        @pl.when(pl.program_id(0) == 0)
        def _():
            dwa_ref[...] = jnp.zeros_like(dwa_ref)
            dwb_ref[...] = jnp.zeros_like(dwb_ref)
            dwout_ref[...] = jnp.zeros_like(dwout_ref)
            dgn_ref[...] = jnp.zeros_like(dgn_ref)

        pa, pb, sa, sb = _post_values(oa_ref, ob_ref, z_ref, bg_ref, ga_ref, gb_ref, gn_ref, wa_ref, wb_ref,
                                      ya_ref, yb_ref)
        mixed = (sa * pa + sb * pb).astype(BF16)
        dout = dh_ref[...].astype(BF16)
        dwout_ref[...] += _dg(mixed, dout, ((0,), (0,)))
        dmixed = jnp.dot(dout, woutt_ref[...], preferred_element_type=F32)
        dga_ref[...] = (dmixed * pa * sa * (1.0 - sa)).astype(BF16)
        dgb_ref[...] = (dmixed * pb * sb * (1.0 - sb)).astype(BF16)
        dpa = (dmixed * sa).astype(BF16)
        dpb = (dmixed * sb).astype(BF16)
        dwa_ref[...] += _dg(ya_ref[...], dpa, ((0,), (0,)))
        dwb_ref[...] += _dg(yb_ref[...], dpb, ((0,), (0,)))
        dya = jnp.dot(dpa, wat_ref[...], preferred_element_type=F32)
        dyb = jnp.dot(dpb, wbt_ref[...], preferred_element_type=F32)
        dgn_a = jnp.zeros((1, D_HEAD), F32)
        dgn_b = jnp.zeros((1, D_HEAD), F32)
        for hd in range(N_HEADS):
            sl = slice(hd * D_HEAD, (hd + 1) * D_HEAD)
            _, vjp = jax.vjp(_gated_norm, oa_ref[:, sl], z_ref[:, sl], gn_ref[0:1, :])
            doa, dz, dgw = vjp(dya[:, sl])
            doa_ref[:, sl], dz_ref[:, sl], dgn_a = doa, dz.astype(BF16), dgn_a + dgw
            _, vjp = jax.vjp(_gated_norm, ob_ref[:, sl], bg_ref[:, sl], gn_ref[1:2, :])
            dob, dbg, dgw = vjp(dyb[:, sl])
            dob_ref[:, sl], dbg_ref[:, sl], dgn_b = dob, dbg.astype(BF16), dgn_b + dgw
        dgn_ref[0:1, :] += dgn_a
        dgn_ref[1:2, :] += dgn_b

    full = lambda i: (0, 0)
    r2 = lambda i: (i, 0)
    return pl.pallas_call(
        body, name=f"post_bwd_{tag}", grid=(n // tm,),
        in_specs=[BS((tm, D_MODEL), r2)] + _post_specs(tm) + [
            BS((HEADS_W, D_MODEL), full), BS((HEADS_W, D_MODEL), full), BS((D_MODEL, HEADS_W), full),
            BS((D_MODEL, HEADS_W), full), BS((D_MODEL, D_MODEL), full)],
        out_specs=[BS((tm, HEADS_W), r2)] * 4 + [BS((tm, D_MODEL), r2)] * 2 + [
            BS((HEADS_W, D_MODEL), full), BS((HEADS_W, D_MODEL), full), BS((D_MODEL, D_MODEL), full),
            BS((SUBLANES, LANES), full)],
        out_shape=[SDS((n, HEADS_W), F32), SDS((n, HEADS_W), F32), SDS((n, HEADS_W), BF16), SDS((n, HEADS_W), BF16),
                   SDS((n, D_MODEL), BF16), SDS((n, D_MODEL), BF16), SDS((HEADS_W, D_MODEL), F32),
                   SDS((HEADS_W, D_MODEL), F32), SDS((D_MODEL, D_MODEL), F32), SDS((SUBLANES, LANES), F32)],
        scratch_shapes=[pltpu.VMEM((tm, HEADS_W), BF16), pltpu.VMEM((tm, HEADS_W), BF16)], compiler_params=_params(1),
    )(dh, oa, ob, proj, proj, proj, proj, h, gn8, wa, wb, wa_t, wb_t, wout_t)


def _loss_head(h, fw8, target, nseq, t_len):
    n = h.shape[0]
    nc = t_len // GDN_CHUNK
    sub = 3 if nc % 3 == 0 else 1
    tl, nt = sub * GDN_CHUNK, nc // sub
    inv_d = 1.0 / D_MODEL

    def body(h_ref, fw_ref, *rest):
        tgt_refs, (dh_ref, acc_ref) = rest[:sub], rest[sub:]

        @pl.when((pl.program_id(0) == 0) & (pl.program_id(1) == 0))
        def _():
            acc_ref[...] = jnp.zeros_like(acc_ref)

        frames = ((pl.program_id(1) * tl + _iota2((tl, 1), 0)) >= N_PAD + N_META).astype(F32)
        y, vjp = jax.vjp(_rms, h_ref[...], fw_ref[0:1, :])
        err = (y - jnp.concatenate([r[...] for r in tgt_refs], axis=0)) * frames
        dx, dfw = vjp(err * inv_d)
        dh_ref[...] = dx
        acc_ref[0:1, :] += dfw
        acc_ref[1:2, :] += (0.5 * inv_d) * jnp.sum(err * err, axis=0, keepdims=True)

    tgt_spec = lambda u: BS((None, GDN_CHUNK, D_MODEL), lambda s, t: (s, jnp.maximum(t * sub + u - 1, 0), 0))
    return pl.pallas_call(
        body, name="loss_head", grid=(nseq, nt),
        in_specs=[BS((tl, D_MODEL), lambda s, t: (s * nt + t, 0)), BS((SUBLANES, D_MODEL), lambda s, t: (0, 0))]
        + [tgt_spec(u) for u in range(sub)],
        out_specs=[BS((tl, D_MODEL), lambda s, t: (s * nt + t, 0)), BS((SUBLANES, D_MODEL), lambda s, t: (0, 0))],
        out_shape=[SDS((n, D_MODEL), F32), SDS((SUBLANES, D_MODEL), F32)], compiler_params=_params(2),
    )(h, fw8, *[target] * sub)


def _prep_bwd(proj, dq, dk, dv, db, dg, dqb, dkb, dlf, cw8, aux, lb8, nseq, t_len, tag):
    n = proj.shape[0]
    tt = _pick(t_len, (192, 128, 64))
    nt_ = t_len // tt
    qkv_w = 3 * HEADS_W
    rb = tt // SUBLANES
    ext = tt + SUBLANES

    def body(cur_ref, prev_ref, next_ref, misc_ref, bq_ref, bf_ref, dq_ref, dqn_ref, dk_ref, dkn_ref, dv_ref, dvn_ref,
             db_ref, dg_ref, dqb_ref, dkb_ref, dlf_ref, cw_ref, aux_ref, lb_ref,
             dqkv_ref, dmisc_ref, dbq_ref, dbf_ref, dcw_ref, daux_ref, dlb_ref, dy_ref):
        s, t = pl.program_id(0), pl.program_id(1)

        @pl.when((s == 0) & (t == 0))
        def _():
            dcw_ref[...] = jnp.zeros_like(dcw_ref)
            daux_ref[...] = jnp.zeros_like(daux_ref)
            dlb_ref[...] = jnp.zeros_like(dlb_ref)

        prev = jnp.where(t == 0, 0.0, prev_ref[...])
        x_ext = jnp.concatenate([prev, cur_ref[...], next_ref[...]], axis=0)
        y = _conv_ext(x_ext, cw_ref)
        inside = (t < nt_ - 1) | (_iota2((ext, 1), 0) < tt)
        dy_ref[0:SUBLANES, :] = jnp.zeros((SUBLANES, qkv_w), F32)
        for hd in range(N_HEADS):
            for grp, (g_ref, gn_ref, scale) in enumerate(((dq_ref, dqn_ref, D_HEAD ** -0.5), (dk_ref, dkn_ref, 1.0),
                                                          (dv_ref, dvn_ref, None))):
                lo = grp * HEADS_W + hd * D_HEAD
                sl = slice(hd * D_HEAD, (hd + 1) * D_HEAD)
                cot = jnp.concatenate([g_ref[:, sl], gn_ref[:, sl]], axis=0)
                fn = _silu if scale is None else functools.partial(_l2n_act, scale=scale)
                _, vjp = jax.vjp(fn, y[:, lo:lo + D_HEAD])
                dy_ref[SUBLANES:, lo:lo + D_HEAD] = jnp.where(inside, vjp(cot)[0], 0.0)
        dy_ext = dy_ref[...]
        dx = dy_ext * cw_ref[3:4, :]
        for kk in range(3):
            dx = dx + _shift_up(dy_ext, 3 - kk) * cw_ref[kk:kk + 1, :]
        dqkv_ref[...] = dx[SUBLANES:SUBLANES + tt].astype(BF16)
        dy_cur = dy_ext[SUBLANES:SUBLANES + tt]
        for kk in range(4):
            xs = _shift_down(x_ext, 3 - kk)[SUBLANES:SUBLANES + tt]
            dcw_ref[kk:kk + 1, :] += jnp.sum(xs * dy_cur, axis=0, keepdims=True)

        real = (t * tt + _iota2((tt, 1), 0)) >= N_PAD
        dmisc = jnp.zeros((tt, LANES), F32)
        daux = jnp.zeros((SUBLANES, LANES), F32)
        for hd in range(N_HEADS):
            sl = slice(hd * D_HEAD, (hd + 1) * D_HEAD)
            _, vjp = jax.vjp(lambda m, a: _gdn_gates(m, a, real, hd), misc_ref[...], aux_ref[...])
            dm, da = vjp((db_ref[:, sl], dg_ref[:, sl]))
            dmisc, daux = dmisc + dm, daux + da
        dmisc_ref[...] = dmisc.astype(BF16)
        daux_ref[...] += daux
        _, vjp = jax.vjp(lambda a, b, c: _hgrn_prep(a, b, c, real), bq_ref[...], bf_ref[...], lb_ref[0:1, :])
        dbq, dbf, dlb = vjp((dqb_ref[...], dkb_ref[...], dlf_ref[...]))
        dbq_ref[...], dbf_ref[...] = dbq.astype(BF16), dbf.astype(BF16)
        dlb_ref[0:1, :] += dlb

    row = lambda s, t: s * nt_ + t
    cur = lambda s, t: (row(s, t), 0)
    nxt = lambda s, t: (jnp.minimum((row(s, t) + 1) * rb, n // SUBLANES - 1), 0)
    wide = BS((tt, HEADS_W), cur)
    halo = BS((SUBLANES, HEADS_W), nxt)
    full = lambda s, t: (0, 0)
    return pl.pallas_call(
        body, name=f"prep_bwd_{tag}", grid=(nseq, nt_),
        in_specs=[BS((tt, qkv_w), cur), BS((SUBLANES, qkv_w), lambda s, t: (jnp.maximum(row(s, t) * rb - 1, 0), 0)),
                  BS((SUBLANES, qkv_w), nxt), BS((tt, LANES), lambda s, t: (row(s, t), C_MISC // LANES)),
                  BS((tt, HEADS_W), lambda s, t: (row(s, t), C_BQ // HEADS_W)),
                  BS((tt, HEADS_W), lambda s, t: (row(s, t), C_BF // HEADS_W)),
                  wide, halo, wide, halo, wide, halo, wide, wide, wide, wide, wide,
                  BS((SUBLANES, qkv_w), full), BS((SUBLANES, LANES), full), BS((SUBLANES, HEADS_W), full)],
        out_specs=[BS((tt, qkv_w), cur), BS((tt, LANES), cur), wide, wide,
                   BS((SUBLANES, qkv_w), full), BS((SUBLANES, LANES), full), BS((SUBLANES, HEADS_W), full)],
        out_shape=[SDS((n, qkv_w), BF16), SDS((n, LANES), BF16), SDS((n, HEADS_W), BF16), SDS((n, HEADS_W), BF16),
                   SDS((SUBLANES, qkv_w), F32), SDS((SUBLANES, LANES), F32), SDS((SUBLANES, HEADS_W), F32)],
        scratch_shapes=[pltpu.VMEM((tt + 2 * SUBLANES, qkv_w), F32)], compiler_params=_params(2),
    )(proj, proj, proj, proj, proj, proj, dq, dq, dk, dk, dv, dv, db, dg, dqb, dkb, dlf, cw8, aux, lb8)


def _proj_bwd_x(pieces, wp_t, h, nw8, dh_res, tag):
    n = h.shape[0]
    tm = _pick(n, (256, 192, 128, 64))
    widths = [p.shape[1] for p in pieces]
    assert sum(widths) == PROJ_W

    def body(*refs):
        p_refs = refs[:len(pieces)]
        wt_ref, h_ref, nw_ref, dres_ref, dh_ref, dnw_ref = refs[len(pieces):]

        @pl.when(pl.program_id(0) == 0)
        def _():
            dnw_ref[...] = jnp.zeros_like(dnw_ref)

        dxn, off = None, 0
        for p_ref, w in zip(p_refs, widths):
            part = jnp.dot(p_ref[...], wt_ref[off:off + w, :], preferred_element_type=F32)
            dxn = part if dxn is None else dxn + part
            off += w
        _, vjp = jax.vjp(_rms, h_ref[...], nw_ref[0:1, :])
        dx, dnw = vjp(dxn)
        dh_ref[...] = dres_ref[...] + dx
        dnw_ref[0:1, :] += dnw

    r2 = lambda i: (i, 0)
    full = lambda i: (0, 0)
    return pl.pallas_call(
        body, name=f"proj_bwd_x_{tag}", grid=(n // tm,),
        in_specs=[BS((tm, w), r2) for w in widths] + [BS((PROJ_W, D_MODEL), full), BS((tm, D_MODEL), r2),
                                                      BS((SUBLANES, D_MODEL), full), BS((tm, D_MODEL), r2)],
        out_specs=[BS((tm, D_MODEL), r2), BS((SUBLANES, D_MODEL), full)],
        out_shape=[SDS((n, D_MODEL), F32), SDS((SUBLANES, D_MODEL), F32)], compiler_params=_params(1),
    )(*pieces, wp_t, h, nw8, dh_res)


def _proj_bwd_w(xn, pieces, tag):
    n = xn.shape[0]
    tm = _pick(n, (384, 256, 192, 128, 64))
    widths = [p.shape[1] for p in pieces]
    assert sum(widths) == PROJ_W

    def body(*refs):
        x_ref, p_refs = refs[0], refs[1:1 + len(pieces)]
        o_ref, acc_ref = refs[1 + len(pieces):]

        @pl.when(pl.program_id(0) == 0)
        def _():
            acc_ref[...] = jnp.zeros_like(acc_ref)

        off = 0
        for p_ref, w in zip(p_refs, widths):
            acc_ref[:, off:off + w] += _dg(x_ref[...], p_ref[...], ((0,), (0,)))
            off += w

        @pl.when(pl.program_id(0) == pl.num_programs(0) - 1)
        def _():
            pltpu.sync_copy(acc_ref, o_ref)

    r2 = lambda i: (i, 0)
    return pl.pallas_call(
        body, name=f"proj_bwd_w_{tag}", grid=(n // tm,),
        in_specs=[BS((tm, D_MODEL), r2)] + [BS((tm, w), r2) for w in widths], out_specs=BS(memory_space=pl.ANY),
        out_shape=SDS((D_MODEL, PROJ_W), F32), scratch_shapes=[pltpu.VMEM((D_MODEL, PROJ_W), F32)],
        compiler_params=_params(1),
    )(xn, *pieces)


def _adamw(w, g, m, v, name):
    lead, rows, cols = w.shape
    tr = _pick(rows, (256, 128, 64, 32, 16, 8, 4, 2, 1)) if rows > 256 else rows

    def body(w_ref, g_ref, m_ref, v_ref, d_ref, nm_ref, nv_ref):
        gr = g_ref[...]
        m_new = ADAM_B1 * m_ref[...] + (1.0 - ADAM_B1) * gr
        v_new = ADAM_B2 * v_ref[...] + (1.0 - ADAM_B2) * jnp.square(gr)
        m_hat = m_new / (1.0 - ADAM_B1 ** ADAM_STEP)
        v_hat = v_new / (1.0 - ADAM_B2 ** ADAM_STEP)
        d_ref[...] = -ADAM_LR * (m_hat / (jnp.sqrt(v_hat) + ADAM_EPS) + ADAM_WD * w_ref[...])
        nm_ref[...] = m_new
        nv_ref[...] = v_new

    blk = BS((None, tr, cols), lambda a, i: (a, i, 0))
    return pl.pallas_call(
        body, name=name, grid=(lead, rows // tr), in_specs=[blk] * 4, out_specs=[blk] * 3,
        out_shape=[SDS((lead, rows, cols), F32)] * 3, compiler_params=_params(2),
    )(w, g, m, v)


def _row8(v, width):
    v = jnp.atleast_2d(v).astype(F32)
    return jnp.pad(v, ((0, SUBLANES - v.shape[0]), (0, width - v.shape[1])))


REF_MISC = 1536
N_MISC = 2 * N_HEADS
LAYOUT_RUNS = ((0, REF_MISC, 0), (REF_MISC + N_MISC, REF_W, REF_MISC), (REF_MISC, REF_MISC + N_MISC, C_MISC))


def _to_layout(w_full):
    runs = [w_full[:, lo:hi] for lo, hi, _ in sorted(LAYOUT_RUNS, key=lambda run: run[2])]
    return jnp.concatenate(runs + [jnp.zeros((w_full.shape[0], PROJ_W - REF_W), w_full.dtype)], axis=1)


def _from_layout(dw, n_slabs):
    width = REF_W // n_slabs
    slabs = []
    for j in range(n_slabs):
        pieces = []
        for lo, hi, at in sorted(LAYOUT_RUNS):
            a, b = max(lo, j * width), min(hi, (j + 1) * width)
            if a < b:
                pieces.append(dw[:, at + a - lo:at + b - lo])
        slabs.append(jnp.concatenate(pieces, axis=1))
    return slabs


def _lower_bounds(lb):
    sm = jax.nn.softmax(lb.astype(F32), axis=0)
    return jnp.cumsum(sm, axis=0) - sm[0]


def kernel(x, meta_tokens, norm_w, w_in, conv_w, a_log, dt_bias, gnorm_a, gnorm_b, hgrn_lower_bounds, w_branch_a, w_branch_b, w_out, final_norm_w, loss_target, m_meta_tokens, m_norm_w, m_w_in, m_conv_w, m_a_log, m_dt_bias, m_gnorm_a, m_gnorm_b, m_hgrn_lower_bounds, m_w_branch_a, m_w_branch_b, m_w_out, m_final_norm_w, v_meta_tokens, v_norm_w, v_w_in, v_conv_w, v_a_log, v_dt_bias, v_gnorm_a, v_gnorm_b, v_hgrn_lower_bounds, v_w_branch_a, v_w_branch_b, v_w_out, v_final_norm_w):
    nseq, seq, _ = x.shape
    depth = norm_w.shape[0]
    t_len = N_PAD + N_META + seq
    n = nseq * t_len
    win_c, conv_c = w_in.shape[2], conv_w.shape[2]
    my = 4 * lax.axis_index("x") + 2 * lax.axis_index("y") + lax.axis_index("c")

    assert depth >= 2
    by_cols = lambda g: g.transpose(1, 2, 0, 3).reshape(g.shape[1], g.shape[2], N_DEV * g.shape[3])
    first = _all_gather_hbm([w_in[:1].astype(BF16), conv_w, meta_tokens], "gather_first")
    later_flight, later_token = _send_all_start(
        [w_in[1:].astype(BF16), w_branch_a.astype(BF16), w_branch_b.astype(BF16), w_out.astype(BF16)], False,
        "gather_later_start")
    w_in_full = [by_cols(first[0])]
    conv_full = by_cols(first[1])
    meta_full = first[2].transpose(1, 0, 2).reshape(N_META, D_MODEL)

    lb_all, lb_vjp = jax.vjp(_lower_bounds, hgrn_lower_bounds)

    h = jnp.concatenate([jnp.zeros((nseq, N_PAD, D_MODEL), F32),
                         jnp.broadcast_to(meta_full[None], (nseq, N_META, D_MODEL)), x], axis=1).reshape(n, D_MODEL)
    saved = []
    for l in range(depth):
        wp = _to_layout(w_in_full[0][0] if l == 0 else w_in_full[1][l - 1])
        nw8 = _row8(norm_w[l], D_MODEL)
        if l == 0:
            nw8 = nw8 + later_token[0:1, 0:1]
        cw8 = _row8(conv_full[l], 3 * HEADS_W)
        aux = _row8(jnp.stack([a_log[l], dt_bias[l]]), LANES)
        lb8 = _row8(lb_all[l], HEADS_W)
        gn8 = _row8(jnp.stack([gnorm_a[l], gnorm_b[l]]), LANES)
        proj, xn = _proj_fwd(h, nw8, wp, l)
        q, k, v, b, g, qb, kb, lf = _prep_fwd(proj, cw8, aux, lb8, nseq, t_len, l)
        oa, ob, sck_a, sck_b = _mixers_fwd(q, k, v, b, g, qb, kb, proj, C_BI // HEADS_W, lf, nseq, t_len, l)
        if l == 0:
            sent, landed = _send_all_wait(later_flight, ob, "gather_later_wait")
            landed = [lax.dynamic_update_slice(ld, own[None], (my,) + (0,) * own.ndim) for ld, own in zip(landed, sent)]
            w_in_full.append(by_cols(landed[0]))
            wa_full, wb_full = by_cols(landed[1]), by_cols(landed[2])
            wout_full = landed[3].transpose(1, 0, 2, 3).reshape(depth, D_MODEL, D_MODEL)
        wa_l, wb_l, wout_l = wa_full[l], wb_full[l], wout_full[l]
        h_next = _post_fwd(oa, ob, proj, h, gn8, wa_l, wb_l, wout_l, l)
        saved.append(dict(h=h, wp=wp, nw8=nw8, cw8=cw8, aux=aux, lb8=lb8, gn8=gn8, proj=proj, xn=xn, q=q, k=k, v=v, b=b,
                          wa=wa_l, wb=wb_l, wout=wout_l,
                          g=g, qb=qb, kb=kb, lf=lf, oa=oa, ob=ob, sck_a=sck_a, sck_b=sck_b))
        h = h_next

    dh, acc = _loss_head(h, _row8(final_norm_w, D_MODEL), loss_target, nseq, t_len)

    g_win, g_wa, g_wb, g_wout, g_conv, small = [], [], [], [], [], []

    def mixer_slabs(dwa_s, dwb_s, dwout_s):
        nl = len(dwa_s)
        rows = lambda a: jnp.stack(a).reshape(nl * HEADS_W, N_DEV, LANES).transpose(1, 0, 2)
        wout = jnp.stack(dwout_s).reshape(nl, N_DEV, LANES, D_MODEL).transpose(1, 0, 2, 3)
        return [jnp.concatenate([rows(dwa_s), rows(dwb_s)], axis=1).astype(BF16),
                wout.reshape(N_DEV, nl * LANES, D_MODEL).astype(BF16)]

    def win_slabs(per_layer, dtype):
        return jnp.stack([jnp.concatenate([sl[j] for sl in per_layer], axis=0) for j in range(N_DEV)]).astype(dtype)

    for l in reversed(range(depth)):
        s = saved[l]
        gn8, aux = s["gn8"], s["aux"]
        if l == 0:
            later_flight, later_token = _send_all_start(
                [win_slabs(g_win[::-1], BF16)] + mixer_slabs(g_wa[::-1], g_wb[::-1], g_wout[::-1]), True,
                "scatter_later_start")
            gn8 = gn8 + later_token[0:1, 0:1]
        doa, dob, dz, dbg, dga, dgb, dwa, dwb, dwout, dgn = _post_bwd(
            dh, s["oa"], s["ob"], s["proj"], s["h"], gn8, s["wa"], s["wb"], s["wa"].T, s["wb"].T, s["wout"].T, l)
        dq, dk, dv, db, dg, dqb, dkb, dbi, dlf = _mixers_bwd(
            s["q"], s["k"], s["v"], s["b"], s["g"], s["qb"], s["kb"], s["proj"], C_BI // HEADS_W, s["lf"], s["sck_a"],
            s["sck_b"], doa, dob, nseq, t_len, l)
        if l == 0:
            mixer_flight, mixer_token = _send_all_start(mixer_slabs([dwa], [dwb], [dwout]), True, "scatter_first_start")
            aux = aux + mixer_token[0:1, 0:1]
        dqkv, dmisc, dbq, dbf, dcw, daux, dlb = _prep_bwd(s["proj"], dq, dk, dv, db, dg, dqb, dkb, dlf, s["cw8"], aux,
                                                          s["lb8"], nseq, t_len, l)
        pieces = [dqkv, dz, dbq, dbf, dbi, dbg, dga, dgb, dmisc]
        dh, dnw = _proj_bwd_x(pieces, s["wp"].T, s["h"], s["nw8"], dh, l)
        dwp = _proj_bwd_w(s["xn"], pieces, l)
        g_win.append(_from_layout(dwp, N_DEV))
        g_wa.append(dwa)
        g_wb.append(dwb)
        g_wout.append(dwout)
        g_conv.append(dcw[:4])
        small.append((dnw[0], dgn[0], dgn[1], daux[0, :N_HEADS], daux[1, :N_HEADS], dlb[0]))
    g_conv.reverse()
    small.reverse()
    dh = dh.reshape(nseq, t_len, D_MODEL)
    grad_x = dh[:, N_PAD + N_META:]

    packed = jnp.concatenate([small[0][1], small[1][1], small[0][2], small[1][2], small[0][3], small[1][3],
                              small[0][4], small[1][4]])
    tile = jnp.concatenate([
        jnp.sum(dh[:, N_PAD:N_PAD + N_META], axis=0), _row8(jnp.stack([small[0][0], small[1][0], acc[0]]), D_MODEL),
        _row8(jnp.stack([small[0][5], small[1][5]]), D_MODEL), _row8(packed, D_MODEL), _row8(acc[1], D_MODEL)], axis=0)
    tile = _all_reduce_small(tile, "reduce_small")
    loss = jnp.sum(tile[40])
    g_meta = lax.dynamic_slice_in_dim(tile[0:N_META], my * LANES, LANES, axis=1)
    g_norm, g_final = tile[16:18], tile[18]
    (g_lb,) = lb_vjp(tile[24:26, :HEADS_W])
    r21 = tile[32]
    g_gna, g_gnb = r21[0:256].reshape(2, LANES), r21[256:512].reshape(2, LANES)
    g_alog, g_dtb = r21[512:520].reshape(2, N_HEADS), r21[520:528].reshape(2, N_HEADS)

    dconv = jnp.stack(g_conv)
    conv_slabs = dconv.reshape(depth * dconv.shape[1], N_DEV, conv_c).transpose(1, 0, 2)
    coords = jnp.stack([lax.axis_index("x"), lax.axis_index("y"), lax.axis_index("c")]).astype(jnp.int32)
    r_win, r_conv = _reduce_scatter([win_slabs(g_win[-1:], F32), conv_slabs], coords, "grads")

    def landed_sums(flight, tag):
        sent, landed = _send_all_wait(flight, dwp, f"{tag}_wait")
        landed = [lax.dynamic_update_slice(ld, lax.dynamic_index_in_dim(src, my, 0, keepdims=True), (my, 0, 0))
                  for ld, src in zip(landed, sent)]
        return [_sum_slabs(ld, f"{tag}_sum{i}") for i, ld in enumerate(landed)]

    l_win, l_ab, l_wout = landed_sums(later_flight, "scatter_later")
    r_ab, r_wout = landed_sums(mixer_flight, "scatter_first")
    both = lambda a, b, shape: jnp.concatenate([a.reshape(1, *shape[1:]), b.reshape(depth - 1, *shape[1:])])
    half, half_l = HEADS_W, (depth - 1) * HEADS_W
    mine = [both(r_win, l_win, w_in.shape), both(r_ab[:half], l_ab[:half_l], w_branch_a.shape),
            both(r_ab[half:], l_ab[half_l:], w_branch_b.shape), both(r_wout, l_wout, w_out.shape), r_conv]
    gseg = lambda i, shape: mine[i].reshape(shape)
    grads = {
        "meta_tokens": g_meta, "norm_w": g_norm, "w_in": gseg(0, w_in.shape), "conv_w": gseg(4, conv_w.shape),
        "a_log": g_alog, "dt_bias": g_dtb, "gnorm_a": g_gna, "gnorm_b": g_gnb, "hgrn_lower_bounds": g_lb,
        "w_branch_a": gseg(1, w_branch_a.shape), "w_branch_b": gseg(2, w_branch_b.shape), "w_out": gseg(3, w_out.shape),
        "final_norm_w": g_final}
    weights = {
        "meta_tokens": (meta_tokens, m_meta_tokens, v_meta_tokens), "norm_w": (norm_w, m_norm_w, v_norm_w),
        "w_in": (w_in, m_w_in, v_w_in), "conv_w": (conv_w, m_conv_w, v_conv_w), "a_log": (a_log, m_a_log, v_a_log),
        "dt_bias": (dt_bias, m_dt_bias, v_dt_bias), "gnorm_a": (gnorm_a, m_gnorm_a, v_gnorm_a),
        "gnorm_b": (gnorm_b, m_gnorm_b, v_gnorm_b),
        "hgrn_lower_bounds": (hgrn_lower_bounds, m_hgrn_lower_bounds, v_hgrn_lower_bounds),
        "w_branch_a": (w_branch_a, m_w_branch_a, v_w_branch_a), "w_branch_b": (w_branch_b, m_w_branch_b, v_w_branch_b),
        "w_out": (w_out, m_w_out, v_w_out), "final_norm_w": (final_norm_w, m_final_norm_w, v_final_norm_w)}
    names = list(weights)
    deltas, new_m, new_v = [], [], []
    for nm in names:
        w, m, v = weights[nm]
        view = (1,) * (3 - w.ndim) + w.shape
        d, m2, v2 = _adamw(w.reshape(view), grads[nm].reshape(view), m.reshape(view), v.reshape(view), f"adamw_{nm}")
        deltas.append(d.reshape(w.shape))
        new_m.append(m2.reshape(w.shape))
        new_v.append(v2.reshape(w.shape))
    return (loss, grad_x, *[grads[nm].reshape(weights[nm][0].shape) for nm in names], *deltas, *new_m, *new_v)
```

```python
import functools

import jax
import jax.numpy as jnp
import numpy as np
from jax import lax
from jax.experimental import pallas as pl
from jax.experimental.pallas import tpu as pltpu

F32 = jnp.float32
BF16 = jnp.bfloat16

D_MODEL = 1024
N_HEADS = 4
D_HEAD = 128
HEADS_W = N_HEADS * D_HEAD
N_META = 16
N_PAD = 48
GDN_CHUNK = 64
HGRN_CHUNK = 16
EPS = 1e-6
N_DEV = 8
LANES = 128
SUBLANES = 8
VMEM_LIMIT = 56 * 1024 * 1024

C_QKV, C_Z, C_BQ, C_BF, C_BI, C_BG, C_GA, C_GB, C_MISC = 0, 1536, 2048, 2560, 3072, 3584, 4096, 5120, 6144
PROJ_W = 6272
REF_W = 6152

ADAM_LR, ADAM_B1, ADAM_B2, ADAM_EPS, ADAM_WD, ADAM_STEP = 0.001, 0.9, 0.999, 1e-08, 0.01, 10

MESH = pl.DeviceIdType.MESH
SDS = jax.ShapeDtypeStruct
BS = pl.BlockSpec


def _params(n_axes):
    return pltpu.CompilerParams(dimension_semantics=("arbitrary",) * n_axes, vmem_limit_bytes=VMEM_LIMIT)


def _pick(n, cands):
    for c in cands:
        if n % c == 0:
            return c
    raise ValueError(f"no tile for {n} among {cands}")


def _iota2(shape, dim):
    return lax.broadcasted_iota(jnp.int32, shape, dim)


def _dg(a, b, dims):
    return lax.dot_general(a.astype(BF16), b.astype(BF16), (dims, ((), ())), preferred_element_type=F32)


def _bdg(a, b, ca, cb):
    return lax.dot_general(a.astype(BF16), b.astype(BF16), (((ca,), (cb,)), ((0,), (0,))), preferred_element_type=F32)


@jax.custom_vjp
def _bnn(a, b):
    return _bdg(a, b, 2, 1)


@jax.custom_vjp
def _bnt(a, b):
    return _bdg(a, b, 2, 2)


@jax.custom_vjp
def _btn(a, b):
    return _bdg(a, b, 1, 1)


_bnn.defvjp(lambda a, b: (_bnn(a, b), (a, b)), lambda r, g: (_bnt(g, r[1]), _btn(r[0], g)))
_bnt.defvjp(lambda a, b: (_bnt(a, b), (a, b)), lambda r, g: (_bnn(g, r[1]), _btn(g, r[0])))
_btn.defvjp(lambda a, b: (_btn(a, b), (a, b)), lambda r, g: (_bnt(r[1], g), _bnn(r[0], g)))


def _split2(x):
    hi = x.astype(BF16).astype(F32)
    return hi, x - hi


def _tri(bsz, n):
    return jnp.broadcast_to((_iota2((n, n), 0) >= _iota2((n, n), 1)).astype(F32), (bsz, n, n))


@jax.custom_vjp
def _cumsum_rows(x):
    tri = _tri(x.shape[0], x.shape[1])
    hi, lo = _split2(x)
    return _bdg(tri, hi, 2, 1) + _bdg(tri, lo, 2, 1)


def _cumsum_rows_bwd(_, g):
    tri = _tri(g.shape[0], g.shape[1])
    hi, lo = _split2(g)
    return (_bdg(tri, hi, 1, 1) + _bdg(tri, lo, 1, 1),)


_cumsum_rows.defvjp(lambda x: (_cumsum_rows(x), None), _cumsum_rows_bwd)


def _sigmoid(x):
    return jax.nn.sigmoid(x)


def _silu(x):
    return x * _sigmoid(x)


def _softplus(x):
    return jnp.maximum(x, 0.0) + jnp.log1p(jnp.exp(-jnp.abs(x)))


def _rms(x, w):
    return x * lax.rsqrt(jnp.mean(x * x, axis=-1, keepdims=True) + EPS) * w


@jax.custom_vjp
def _inv_unit_lower(lm):
    n = lm.shape[1]
    a = (_iota2((n, n), 0) == _iota2((n, n), 1)).astype(F32)[None] - lm
    steps = max(1, (n - 1).bit_length()) - 1
    p = _bnn(lm, lm)
    for i in range(steps):
        if i == steps - 1:
            a = a + _bnn(a, p)
        else:
            both = _bnn(jnp.concatenate([a, p], axis=1), p)
            a, p = a + both[:, :n], both[:, n:]
    return a


_inv_unit_lower.defvjp(lambda lm: (lambda a: (a, a))(_inv_unit_lower(lm)),
                       lambda a, g: (-_bnt(_btn(a, g), a),))


def _gdn_chunk(q, k, v, b_b, g_b, s):
    n, dv = q.shape[1], v.shape[2]
    r, c = _iota2((n, n), 0), _iota2((n, n), 1)
    causal, strict, eye = (r >= c)[None], (r > c)[None], (r == c)[None]
    g_cum = _cumsum_rows(g_b)
    g_i = g_cum[:, :, :n]
    g_j = jnp.sum(jnp.where(eye, g_i, 0.0), axis=1, keepdims=True)
    decay = jnp.where(causal, jnp.exp(jnp.where(causal, g_i - g_j, 0.0)), 0.0)
    e_g = jnp.exp(g_cum)
    kb = k * b_b
    kk = _bnt(jnp.concatenate([kb, q], axis=1), k)
    a_inv = _inv_unit_lower(jnp.where(strict, kk[:, :n] * decay, 0.0))
    uw = _bnn(a_inv, jnp.concatenate([v * b_b, kb * e_g], axis=2))
    ws = _bnn(jnp.concatenate([uw[:, :, dv:], q * e_g], axis=1), s)
    v_new = uw[:, :, :dv] - ws[:, :n]
    o = ws[:, n:] + _bnn(kk[:, n:] * decay, v_new)
    g_last = g_cum[:, n - 1:n, :]
    s_new = s * jnp.exp(g_last) +_btn(k * jnp.exp(g_last - g_cum), v_new)
    return o, s_new


@functools.partial(jax.custom_vjp, nondiff_argnums=(1, 2))
def _row(x, j, n):
    return x[:, j:j + 1, :]


def _row_bwd(j, n, _, g):
    return (jnp.where(_iota2((1, n, 1), 1) == j, g, 0.0),)


_row.defvjp(lambda x, j, n: (_row(x, j, n), None), _row_bwd)


def _hgrn_pairs(q, k, v, b_cum):
    n = q.shape[1]
    half = n // 2
    parts = []
    for lo in (0, half):
        qs, bs = q[:, lo:], b_cum[:, lo:]
        rows = _iota2((1, n - lo, 1), 1) + lo
        acc = jnp.zeros_like(qs)
        for j in range(lo, n if lo else half):
            p = jnp.exp(jnp.where(rows >= j, bs - _row(b_cum, j, n), -1e30))
            acc = acc + jnp.sum(qs * _row(k, j, n) * p, axis=2, keepdims=True) * _row(v, j, n)
        parts.append(acc)
    return parts[0] + jnp.concatenate([jnp.zeros_like(parts[1]), parts[1]], axis=1)


def _hgrn_block(q, k, v, lf, st):
    n, rows = HGRN_CHUNK, q.shape[1]
    b_cum = _cumsum_rows(lf)
    outs = []
    for c in range(rows // n):
        rs = slice(c * n, (c + 1) * n)
        o = _hgrn_pairs(q[:, rs], k[:, rs], v[:, rs], b_cum[:, rs])
        if c:
            b_c = _row(b_cum, c * n - 1, rows)
            scores = _bnt(q[:, rs] * jnp.exp(b_cum[:, rs] - b_c), k[:, :c * n] * jnp.exp(b_c - b_cum[:, :c * n]))
            o = o + _bnn(scores, v[:, :c * n])
        outs.append(o)
    b_last = _row(b_cum, rows - 1, rows)
    o = _bnt(q * jnp.exp(b_cum), st) + jnp.concatenate(outs, axis=1)
    return o, st * jnp.exp(b_last) + _btn(v, k * jnp.exp(b_last - b_cum))


def _l2n_act(y, scale):
    a = _silu(y)
    return a * lax.rsqrt(jnp.sum(a * a, axis=-1, keepdims=True) + EPS) * scale


def _col(x, lane):
    return jnp.sum(jnp.where(_iota2(x.shape, 1) == lane, x, 0.0), axis=1, keepdims=True)


def _elem(x, row, lane):
    m = (_iota2(x.shape, 0) == row) & (_iota2(x.shape, 1) == lane)
    return jnp.sum(jnp.sum(jnp.where(m, x, 0.0), axis=1, keepdims=True), axis=0, keepdims=True)


def _gdn_gates(misc, aux, real, head):
    beta = _sigmoid(_col(misc, head))
    g = -jnp.exp(_elem(aux, 0, head)) * _softplus(_col(misc, N_HEADS + head) + _elem(aux, 1, head))
    g = jnp.where(real, g, 0.0)
    shape = (misc.shape[0], D_HEAD)
    return jnp.broadcast_to(beta, shape), jnp.broadcast_to(g, shape)


def _hgrn_prep(bq, bf, lb, real):
    qb = _silu(bq) * (D_HEAD ** -0.5)
    log_sig = jnp.minimum(bf, 0.0) - jnp.log1p(jnp.exp(-jnp.abs(bf)))
    pos = lb > 0.0
    lbs = jnp.where(pos, lb, 0.5)
    a = jnp.log(lbs)
    b = jnp.log1p(-lbs) + log_sig
    lae = jnp.maximum(a, b) + jnp.log1p(jnp.exp(-jnp.abs(a - b)))
    lf = jnp.where(pos, lae, log_sig)
    kb = jnp.where(pos, 1.0 - lbs, 1.0) * _sigmoid(-bf)
    return qb, jnp.where(real, kb, 0.0), jnp.where(real, lf, 0.0)


def _gated_norm(o, z, gw):
    return o * lax.rsqrt(jnp.mean(o * o, axis=-1, keepdims=True) + EPS) * gw * _silu(z)


def _shift_down(x, j):
    return x if j == 0 else pltpu.roll(x, j, 0)


def _shift_up(x, j):
    return x if j == 0 else pltpu.roll(x, x.shape[0] - j, 0)


def _all_gather_hbm(blocks, name):
    na = len(blocks)

    def body(*refs):
        x_refs, out_refs = refs[:na], refs[na:2 * na]
        send_sems, recv_sems, local_sems = refs[2 * na:]
        mx, my, mc = lax.axis_index("x"), lax.axis_index("y"), lax.axis_index("c")
        me, sibling = (mx, my, mc), (mx, my, 1 - mc)
        chips = [(1 - mx, my), (mx, 1 - my), (1 - mx, 1 - my)]

        def slab(a, px, py, pc):
            return out_refs[a].at[4 * px + 2 * py + pc]

        def copy(a, k, blk, to, own=False):
            return pltpu.make_async_remote_copy(
                src_ref=x_refs[a] if own else slab(a, *blk), dst_ref=slab(a, *blk),
                send_sem=send_sems.at[7 * a + k], recv_sem=recv_sems.at[7 * a + k], device_id=to, device_id_type=MESH)

        mine = [pltpu.make_async_copy(x_refs[a], slab(a, *me), local_sems.at[a]) for a in range(na)]
        for cp in mine:
            cp.start()
        first = [copy(a, 0, me, sibling, own=True) for a in range(na)]
        first += [copy(a, 1 + j, me, (*chip, mc), own=True) for j, chip in enumerate(chips) for a in range(na)]
        for cp in first:
            cp.start()
        passed = []
        for j, chip in enumerate(chips):
            for a in range(na):
                copy(a, 1 + j, (*chip, mc), me).wait_recv()
                passed.append(copy(a, 4 + j, (*chip, mc), sibling))
                passed[-1].start()
        for a in range(na):
            copy(a, 0, sibling, me).wait_recv()
            for j, chip in enumerate(chips):
                copy(a, 4 + j, (*chip, 1 - mc), me).wait_recv()
        for cp in first + passed:
            cp.wait_send()
        for cp in mine:
            cp.wait()

    hbm = BS(memory_space=pl.ANY)
    return pl.pallas_call(
        body, name=name, out_shape=[SDS((N_DEV, *b.shape), b.dtype) for b in blocks],
        in_specs=[hbm] * na, out_specs=[hbm] * na,
        scratch_shapes=[pltpu.SemaphoreType.DMA((7 * na,)), pltpu.SemaphoreType.DMA((7 * na,)),
                        pltpu.SemaphoreType.DMA((na,))],
    )(*blocks)


def _all_reduce_small(block, name):
    r, c = block.shape

    def body(x_ref, out_ref, buf, send_sems, recv_sems):
        mx, my, mc = lax.axis_index("x"), lax.axis_index("y"), lax.axis_index("c")
        me, sibling = (mx, my, mc), (mx, my, 1 - mc)
        chips = [(1 - mx, my), (mx, 1 - my), (1 - mx, 1 - my)]

        def slab(px, py, pc):
            return buf.at[4 * px + 2 * py + pc]

        def copy(k, blk, to, src=None):
            return pltpu.make_async_remote_copy(
                src_ref=slab(*blk) if src is None else src, dst_ref=slab(*blk),
                send_sem=send_sems.at[k], recv_sem=recv_sems.at[k], device_id=to, device_id_type=MESH)

        first = [copy(0, me, sibling, src=x_ref)]
        first += [copy(1 + j, me, (*chip, mc), src=x_ref) for j, chip in enumerate(chips)]
        for cp in first:
            cp.start()
        passed = [copy(4 + j, (*chip, mc), sibling) for j, chip in enumerate(chips)]
        for j, chip in enumerate(chips):
            copy(1 + j, (*chip, mc), me).wait_recv()
            passed[j].start()
        copy(0, sibling, me).wait_recv()
        for j, chip in enumerate(chips):
            copy(4 + j, (*chip, 1 - mc), me).wait_recv()
        for cp in first + passed:
            cp.wait_send()
        buf[4 * mx + 2 * my + mc] = x_ref[...]
        acc = buf[0]
        for d in range(1, N_DEV):
            acc = acc + buf[d]
        out_ref[...] = acc

    return pl.pallas_call(
        body, name=name, out_shape=SDS((r, c), F32),
        in_specs=[BS(memory_space=pltpu.VMEM)], out_specs=BS(memory_space=pltpu.VMEM),
        scratch_shapes=[pltpu.VMEM((N_DEV, r, c), F32), pltpu.SemaphoreType.DMA((7,)), pltpu.SemaphoreType.DMA((7,))],
    )(block)


HBM_SPEC = BS(memory_space=pltpu.HBM)
SEM_SPEC = BS(memory_space=pltpu.SEMAPHORE)
SIDE_EFFECT = pltpu.SideEffectType.DATAFLOW_SIDE_EFFECTING


def _peer(rel):
    flip = lambda v, bit: 1 - v if bit else v
    return (flip(lax.axis_index("x"), rel >> 2 & 1), flip(lax.axis_index("y"), rel >> 1 & 1),
            flip(lax.axis_index("c"), rel & 1))


def _send_all_start(blocks, scatter, name):
    na = len(blocks)
    shapes = [b.shape[1:] if scatter else b.shape for b in blocks]

    def body(*refs):
        srcs, lands = refs[:na], refs[na:2 * na]
        send_sems, recv_sems, token = refs[2 * na], refs[2 * na + 1], refs[-1]
        me = 4 * lax.axis_index("x") + 2 * lax.axis_index("y") + lax.axis_index("c")
        for a in range(na):
            for rel in range(1, N_DEV):
                px, py, pc = _peer(rel)
                pltpu.make_async_remote_copy(
                    src_ref=srcs[a].at[4 * px + 2 * py + pc] if scatter else srcs[a], dst_ref=lands[a].at[me],
                    send_sem=send_sems.at[7 * a + rel - 1], recv_sem=recv_sems.at[7 * a + rel - 1],
                    device_id=(px, py, pc), device_id_type=MESH).start()
        token[...] = jnp.zeros_like(token)

    lands = [lax.empty((N_DEV, *s), b.dtype) for s, b in zip(shapes, blocks)]
    res = pl.pallas_call(
        body, name=name,
        out_shape=([pltpu.SemaphoreType.DMA((7 * na,)), pltpu.SemaphoreType.DMA((7 * na,))]
                   + [pltpu.HBM(b.shape, b.dtype) for b in blocks] + [pltpu.HBM(ld.shape, ld.dtype) for ld in lands]
                   + [SDS((SUBLANES, LANES), F32)]),
        in_specs=[HBM_SPEC] * (2 * na), out_specs=[SEM_SPEC, SEM_SPEC] + [HBM_SPEC] * (2 * na) + [BS(memory_space=pltpu.VMEM)],
        input_output_aliases={i: 2 + i for i in range(2 * na)},
        compiler_params=pltpu.CompilerParams(has_side_effects=SIDE_EFFECT),
    )(*[pltpu.with_memory_space_constraint(b, pltpu.HBM) for b in blocks],
      *[pltpu.with_memory_space_constraint(ld, pltpu.HBM) for ld in lands])
    return dict(send=res[0], recv=res[1], srcs=res[2:2 + na], lands=res[2 + na:2 + 2 * na], scatter=scatter), res[-1]


def _send_all_wait(flight, after, name):
    na = len(flight["srcs"])

    def body(*refs):
        srcs, lands = refs[:na], refs[na:2 * na]
        send_sems, recv_sems = refs[2 * na], refs[2 * na + 1]
        for a in range(na):
            for rel in range(1, N_DEV):
                cp = pltpu.make_async_remote_copy(
                    src_ref=srcs[a].at[0] if flight["scatter"] else srcs[a], dst_ref=lands[a].at[0],
                    send_sem=send_sems.at[7 * a + rel - 1], recv_sem=recv_sems.at[7 * a + rel - 1],
                    device_id=_peer(rel), device_id_type=MESH)
                cp.wait_send()
                cp.wait_recv()

    arrays = list(flight["srcs"]) + list(flight["lands"])
    res = pl.pallas_call(
        body, name=name, out_shape=[pltpu.HBM(a.shape, a.dtype) for a in arrays],
        in_specs=[HBM_SPEC] * (2 * na) + [SEM_SPEC, SEM_SPEC, BS(memory_space=pl.ANY)], out_specs=[HBM_SPEC] * (2 * na),
        input_output_aliases={i: i for i in range(2 * na)},
        compiler_params=pltpu.CompilerParams(has_side_effects=SIDE_EFFECT),
    )(*arrays, flight["send"], flight["recv"], after)
    return res[:na], res[na:]


def _sum_slabs(land, name):
    _, r, c = land.shape
    tr = _pick(r, (256, 128, 64, 32, 16, 8))

    def body(l_ref, o_ref):
        acc = l_ref[0].astype(F32)
        for d in range(1, N_DEV):
            acc = acc + l_ref[d].astype(F32)
        o_ref[...] = acc

    return pl.pallas_call(
        body, name=name, grid=(r // tr,), out_shape=SDS((r, c), F32),
        in_specs=[BS((N_DEV, tr, c), lambda j: (0, j, 0))], out_specs=BS((tr, c), lambda j: (j, 0)),
        compiler_params=_params(1),
    )(land)


def _proj_fwd(h, nw8, wp, tag):
    n = h.shape[0]
    tm = _pick(n, (1408, 768, 512, 384, 256, 192, 128, 64))
    tn = 896

    def body(h_ref, nw_ref, w_ref, proj_ref, xn_ref):
        @pl.when(pl.program_id(1) == 0)
        def _():
            xn_ref[...] = _rms(h_ref[...], nw_ref[0:1, :]).astype(BF16)

        proj_ref[...] = jnp.dot(xn_ref[...], w_ref[...], preferred_element_type=F32)

    return pl.pallas_call(
        body, name=f"proj_fwd_{tag}", grid=(n // tm, PROJ_W // tn),
        in_specs=[BS((tm, D_MODEL), lambda i, j: (i, 0)), BS((SUBLANES, D_MODEL), lambda i, j: (0, 0)),
                  BS((D_MODEL, tn), lambda i, j: (0, j))],
        out_specs=[BS((tm, tn), lambda i, j: (i, j)), BS((tm, D_MODEL), lambda i, j: (i, 0))],
        out_shape=[SDS((n, PROJ_W), F32), SDS((n, D_MODEL), BF16)], compiler_params=_params(2),
    )(h, nw8, wp)


def _conv_ext(x_ext, cw_ref):
    y = x_ext * cw_ref[3:4, :]
    for k in range(3):
        y = y + _shift_down(x_ext, 3 - k) * cw_ref[k:k + 1, :]
    return y[SUBLANES:]


def _prep_fwd(proj, cw8, aux, lb8, nseq, t_len, tag):
    n = proj.shape[0]
    tt = _pick(t_len, (192, 128, 64))
    nt_ = t_len // tt
    qkv_w = 3 * HEADS_W

    def body(cur_ref, prev_ref, misc_ref, bq_ref, bf_ref, cw_ref, aux_ref, lb_ref,
             q_ref, k_ref, v_ref, b_ref, g_ref, qb_ref, kb_ref, lf_ref, ext_ref):
        t = pl.program_id(1)
        ext_ref[0:SUBLANES, :] = jnp.where(t == 0, 0.0, prev_ref[...])
        ext_ref[SUBLANES:, :] = cur_ref[...]
        y = ext_ref[SUBLANES:, :] * cw_ref[3:4, :]
        for kk in range(3):
            y = y + ext_ref[SUBLANES - 3 + kk:SUBLANES - 3 + kk + tt, :] * cw_ref[kk:kk + 1, :]
        real = (t * tt + _iota2((tt, 1), 0)) >= N_PAD
        misc = misc_ref[...]
        auxv = aux_ref[...]
        for hd in range(N_HEADS):
            sl = slice(hd * D_HEAD, (hd + 1) * D_HEAD)
            q_ref[:, sl] = _l2n_act(y[:, sl], D_HEAD ** -0.5)
            k_ref[:, sl] = _l2n_act(y[:, HEADS_W + hd * D_HEAD:HEADS_W + (hd + 1) * D_HEAD], 1.0)
            v_ref[:, sl] = _silu(y[:, 2 * HEADS_W + hd * D_HEAD:2 * HEADS_W + (hd + 1) * D_HEAD])
            b_ref[:, sl], g_ref[:, sl] = _gdn_gates(misc, auxv, real, hd)
        qb_ref[...], kb_ref[...], lf_ref[...] = _hgrn_prep(bq_ref[...], bf_ref[...], lb_ref[0:1, :], real)

    rb = tt // SUBLANES
    row = lambda s, t: s * nt_ + t
    wide = BS((tt, HEADS_W), lambda s, t: (row(s, t), 0))
    return pl.pallas_call(
        body, name=f"prep_fwd_{tag}", grid=(nseq, nt_),
        in_specs=[BS((tt, qkv_w), lambda s, t: (row(s, t), 0)),
                  BS((SUBLANES, qkv_w), lambda s, t: (jnp.maximum(row(s, t) * rb - 1, 0), 0)),
                  BS((tt, LANES), lambda s, t: (row(s, t), C_MISC // LANES)),
                  BS((tt, HEADS_W), lambda s, t: (row(s, t), C_BQ // HEADS_W)),
                  BS((tt, HEADS_W), lambda s, t: (row(s, t), C_BF // HEADS_W)),
                  BS((SUBLANES, qkv_w), lambda s, t: (0, 0)), BS((SUBLANES, LANES), lambda s, t: (0, 0)),
                  BS((SUBLANES, HEADS_W), lambda s, t: (0, 0))],
        out_specs=[wide] * 8, out_shape=[SDS((n, HEADS_W), F32)] * 8,
        scratch_shapes=[pltpu.VMEM((tt + SUBLANES, qkv_w), F32)], compiler_params=_params(2),
    )(proj, proj, proj, proj, proj, cw8, aux, lb8)


GDN_SEQS = 4
HGRN_SEQS = 2


def _seq_block(nseq, most):
    return max(s for s in (1, 2, 4) if s <= most and nseq % s == 0)


def _to_chains(x):
    return jnp.concatenate([x[:, :, hd * D_HEAD:(hd + 1) * D_HEAD] for hd in range(N_HEADS)], axis=0)


def _from_chains(ref, rows, val):
    sb = val.shape[0] // N_HEADS
    for hd in range(N_HEADS):
        ref[:, rows, hd * D_HEAD:(hd + 1) * D_HEAD] = val[hd * sb:(hd + 1) * sb].astype(ref.dtype)


def _mixers_fwd(q, k, v, b, g, qb, kb, vb, vb_col, lf, nseq, t_len, tag):
    sb, hs = _seq_block(nseq, GDN_SEQS), _seq_block(nseq, HGRN_SEQS)
    nc = t_len // GDN_CHUNK
    chains = N_HEADS * sb

    def body(q_ref, k_ref, v_ref, b_ref, g_ref, qb_ref, kb_ref, vb_ref, lf_ref, oa_ref, ob_ref, cka_ref, ckb_ref,
             sa_ref, sb_ref):
        @pl.when(pl.program_id(1) == 0)
        def _():
            sa_ref[...] = jnp.zeros_like(sa_ref)
            sb_ref[...] = jnp.zeros_like(sb_ref)

        s = sa_ref[...]
        cka_ref[...] = s
        o, s_new = _gdn_chunk(*[_to_chains(r[...]) for r in (q_ref, k_ref, v_ref, b_ref, g_ref)], s)
        _from_chains(oa_ref, slice(None), o)
        sa_ref[...] = s_new
        for part in range(sb // hs):
            seqs, ch = slice(part * hs, (part + 1) * hs), slice(part * N_HEADS * hs, (part + 1) * N_HEADS * hs)
            s = sb_ref[ch]
            ckb_ref[ch] = s
            o, s_new = _hgrn_block(*[_to_chains(r[seqs]) for r in (qb_ref, kb_ref, vb_ref, lf_ref)], s)
            for hd in range(N_HEADS):
                ob_ref[seqs, :, hd * D_HEAD:(hd + 1) * D_HEAD] = o[hd * hs:(hd + 1) * hs]
            sb_ref[ch] = s_new

    blk = lambda cb: BS((sb, GDN_CHUNK, HEADS_W), lambda p, c: (p, c, cb))
    ck_spec = BS((None, None, chains, D_HEAD, D_HEAD), lambda p, c: (p, c, 0, 0, 0))
    ck_shape = SDS((nseq // sb, nc, chains, D_HEAD, D_HEAD), F32)
    view = lambda a: a.reshape(nseq, t_len, a.shape[1])
    oa, ob, cka, ckb = pl.pallas_call(
        body, name=f"mixers_fwd_{tag}", grid=(nseq // sb, nc),
        in_specs=[blk(0)] * 7 + [blk(vb_col), blk(0)], out_specs=[blk(0), blk(0), ck_spec, ck_spec],
        out_shape=[SDS((nseq, t_len, HEADS_W), F32)] * 2 + [ck_shape] * 2,
        scratch_shapes=[pltpu.VMEM((chains, D_HEAD, D_HEAD), F32)] * 2, compiler_params=_params(2),
    )(*[view(a) for a in (q, k, v, b, g, qb, kb, vb, lf)])
    return oa.reshape(-1, HEADS_W), ob.reshape(-1, HEADS_W), cka, ckb


def _mixers_bwd(q, k, v, b, g, qb, kb, vb, vb_col, lf, cka, ckb, doa, dob, nseq, t_len, tag):
    sb, hs = _seq_block(nseq, GDN_SEQS), _seq_block(nseq, HGRN_SEQS)
    nc = t_len // GDN_CHUNK
    chains = N_HEADS * sb

    def body(q_ref, k_ref, v_ref, b_ref, g_ref, qb_ref, kb_ref, vb_ref, lf_ref, doa_ref, dob_ref, cka_ref, ckb_ref,
             dq_ref, dk_ref, dv_ref, db_ref, dg_ref, dqb_ref, dkb_ref, dvb_ref, dlf_ref, dsa_ref, dsb_ref):
        @pl.when(pl.program_id(1) == 0)
        def _():
            dsa_ref[...] = jnp.zeros_like(dsa_ref)
            dsb_ref[...] = jnp.zeros_like(dsb_ref)

        _, vjp = jax.vjp(_gdn_chunk, *[_to_chains(r[...]) for r in (q_ref, k_ref, v_ref, b_ref, g_ref)], cka_ref[...])
        grads = vjp((_to_chains(doa_ref[...]), dsa_ref[...]))
        for ref, val in zip((dq_ref, dk_ref, dv_ref, db_ref, dg_ref), grads[:5]):
            _from_chains(ref, slice(None), val)
        dsa_ref[...] = grads[5]
        for part in range(sb // hs):
            seqs, ch = slice(part * hs, (part + 1) * hs), slice(part * N_HEADS * hs, (part + 1) * N_HEADS * hs)
            _, vjp = jax.vjp(_hgrn_block, *[_to_chains(r[seqs]) for r in (qb_ref, kb_ref, vb_ref, lf_ref)], ckb_ref[ch])
            grads = vjp((_to_chains(dob_ref[seqs]), dsb_ref[ch]))
            for ref, val in zip((dqb_ref, dkb_ref, dvb_ref, dlf_ref), grads[:4]):
                for hd in range(N_HEADS):
                    ref[seqs, :, hd * D_HEAD:(hd + 1) * D_HEAD] = val[hd * hs:(hd + 1) * hs].astype(ref.dtype)
            dsb_ref[ch] = grads[4]

    blk = lambda cb: BS((sb, GDN_CHUNK, HEADS_W), lambda p, c: (p, nc - 1 - c, cb))
    ck_spec = BS((None, None, chains, D_HEAD, D_HEAD), lambda p, c: (p, nc - 1 - c, 0, 0, 0))
    view = lambda a: a.reshape(nseq, t_len, a.shape[1])
    dts = [F32] * 7 + [BF16, F32]
    res = pl.pallas_call(
        body, name=f"mixers_bwd_{tag}", grid=(nseq // sb, nc),
        in_specs=[blk(0)] * 7 + [blk(vb_col), blk(0), blk(0), blk(0), ck_spec, ck_spec], out_specs=[blk(0)] * 9,
        out_shape=[SDS((nseq, t_len, HEADS_W), dt) for dt in dts],
        scratch_shapes=[pltpu.VMEM((chains, D_HEAD, D_HEAD), F32)] * 2, compiler_params=_params(2),
    )(*[view(a) for a in (q, k, v, b, g, qb, kb, vb, lf, doa, dob)], cka, ckb)
    return [r.reshape(-1, HEADS_W) for r in res]


def _post_values(oa_ref, ob_ref, z_ref, bg_ref, ga_ref, gb_ref, gn_ref, wa_ref, wb_ref, ya_ref, yb_ref):
    for hd in range(N_HEADS):
        sl = slice(hd * D_HEAD, (hd + 1) * D_HEAD)
        ya_ref[:, sl] = _gated_norm(oa_ref[:, sl], z_ref[:, sl], gn_ref[0:1, :]).astype(BF16)
        yb_ref[:, sl] = _gated_norm(ob_ref[:, sl], bg_ref[:, sl], gn_ref[1:2, :]).astype(BF16)
    pa = jnp.dot(ya_ref[...], wa_ref[...], preferred_element_type=F32)
    pb = jnp.dot(yb_ref[...], wb_ref[...], preferred_element_type=F32)
    return pa, pb, _sigmoid(ga_ref[...]), _sigmoid(gb_ref[...])


def _post_specs(tm):
    r2 = lambda i: (i, 0)
    return [BS((tm, HEADS_W), r2), BS((tm, HEADS_W), r2),
            BS((tm, HEADS_W), lambda i: (i, C_Z // HEADS_W)), BS((tm, HEADS_W), lambda i: (i, C_BG // HEADS_W)),
            BS((tm, D_MODEL), lambda i: (i, C_GA // D_MODEL)), BS((tm, D_MODEL), lambda i: (i, C_GB // D_MODEL)),
            BS((tm, D_MODEL), r2), BS((SUBLANES, LANES), lambda i: (0, 0))]


def _post_fwd(oa, ob, proj, h, gn8, wa, wb, wout, tag):
    n = h.shape[0]
    tm = _pick(n, (256, 192, 128, 64))

    def body(oa_ref, ob_ref, z_ref, bg_ref, ga_ref, gb_ref, h_ref, gn_ref, wa_ref, wb_ref, wout_ref, out_ref,
             ya_ref, yb_ref):
        pa, pb, sa, sb = _post_values(oa_ref, ob_ref, z_ref, bg_ref, ga_ref, gb_ref, gn_ref, wa_ref, wb_ref,
                                      ya_ref, yb_ref)
        mixed = (sa * pa + sb * pb).astype(BF16)
        out_ref[...] = h_ref[...] + jnp.dot(mixed, wout_ref[...], preferred_element_type=F32)

    full = lambda i: (0, 0)
    return pl.pallas_call(
        body, name=f"post_fwd_{tag}", grid=(n // tm,),
        in_specs=_post_specs(tm) + [BS((HEADS_W, D_MODEL), full), BS((HEADS_W, D_MODEL), full),
                                    BS((D_MODEL, D_MODEL), full)],
        out_specs=BS((tm, D_MODEL), lambda i: (i, 0)), out_shape=SDS((n, D_MODEL), F32),
        scratch_shapes=[pltpu.VMEM((tm, HEADS_W), BF16), pltpu.VMEM((tm, HEADS_W), BF16)], compiler_params=_params(1),
    )(oa, ob, proj, proj, proj, proj, h, gn8, wa, wb, wout)


def _post_bwd(dh, oa, ob, proj, h, gn8, wa, wb, wa_t, wb_t, wout_t, tag):
    n = h.shape[0]
    tm = _pick(n, (256, 192, 128, 64))

    def body(dh_ref, oa_ref, ob_ref, z_ref, bg_ref, ga_ref, gb_ref, h_ref, gn_ref, wa_ref, wb_ref, wat_ref, wbt_ref,
             woutt_ref, doa_ref, dob_ref, dz_ref, dbg_ref, dga_ref, dgb_ref, dwa_ref, dwb_ref, dwout_ref, dgn_ref,
             ya_ref, yb_ref):
        @pl.when(pl.program_id(0) == 0)
        def _():
            dwa_ref[...] = jnp.zeros_like(dwa_ref)
            dwb_ref[...] = jnp.zeros_like(dwb_ref)
            dwout_ref[...] = jnp.zeros_like(dwout_ref)
            dgn_ref[...] = jnp.zeros_like(dgn_ref)

        pa, pb, sa, sb = _post_values(oa_ref, ob_ref, z_ref, bg_ref, ga_ref, gb_ref, gn_ref, wa_ref, wb_ref,
                                      ya_ref, yb_ref)
        mixed = (sa * pa + sb * pb).astype(BF16)
        dout = dh_ref[...].astype(BF16)
        dwout_ref[...] += _dg(mixed, dout, ((0,), (0,)))
        dmixed = jnp.dot(dout, woutt_ref[...], preferred_element_type=F32)
        dga_ref[...] = (dmixed * pa * sa * (1.0 - sa)).astype(BF16)
        dgb_ref[...] = (dmixed * pb * sb * (1.0 - sb)).astype(BF16)
        dpa = (dmixed * sa).astype(BF16)
        dpb = (dmixed * sb).astype(BF16)
        dwa_ref[...] += _dg(ya_ref[...], dpa, ((0,), (0,)))
        dwb_ref[...] += _dg(yb_ref[...], dpb, ((0,), (0,)))
        dya = jnp.dot(dpa, wat_ref[...], preferred_element_type=F32)
        dyb = jnp.dot(dpb, wbt_ref[...], preferred_element_type=F32)
        dgn_a = jnp.zeros((1, D_HEAD), F32)
        dgn_b = jnp.zeros((1, D_HEAD), F32)
        for hd in range(N_HEADS):
            sl = slice(hd * D_HEAD, (hd + 1) * D_HEAD)
            _, vjp = jax.vjp(_gated_norm, oa_ref[:, sl], z_ref[:, sl], gn_ref[0:1, :])
            doa, dz, dgw = vjp(dya[:, sl])
            doa_ref[:, sl], dz_ref[:, sl], dgn_a = doa, dz.astype(BF16), dgn_a + dgw
            _, vjp = jax.vjp(_gated_norm, ob_ref[:, sl], bg_ref[:, sl], gn_ref[1:2, :])
            dob, dbg, dgw = vjp(dyb[:, sl])
            dob_ref[:, sl], dbg_ref[:, sl], dgn_b = dob, dbg.astype(BF16), dgn_b + dgw
        dgn_ref[0:1, :] += dgn_a
        dgn_ref[1:2, :] += dgn_b

    full = lambda i: (0, 0)
    r2 = lambda i: (i, 0)
    return pl.pallas_call(
        body, name=f"post_bwd_{tag}", grid=(n // tm,),
        in_specs=[BS((tm, D_MODEL), r2)] + _post_specs(tm) + [
            BS((HEADS_W, D_MODEL), full), BS((HEADS_W, D_MODEL), full), BS((D_MODEL, HEADS_W), full),
            BS((D_MODEL, HEADS_W), full), BS((D_MODEL, D_MODEL), full)],
        out_specs=[BS((tm, HEADS_W), r2)] * 4 + [BS((tm, D_MODEL), r2)] * 2 + [
            BS((HEADS_W, D_MODEL), full), BS((HEADS_W, D_MODEL), full), BS((D_MODEL, D_MODEL), full),
            BS((SUBLANES, LANES), full)],
        out_shape=[SDS((n, HEADS_W), F32), SDS((n, HEADS_W), F32), SDS((n, HEADS_W), BF16), SDS((n, HEADS_W), BF16),
                   SDS((n, D_MODEL), BF16), SDS((n, D_MODEL), BF16), SDS((HEADS_W, D_MODEL), F32),
                   SDS((HEADS_W, D_MODEL), F32), SDS((D_MODEL, D_MODEL), F32), SDS((SUBLANES, LANES), F32)],
        scratch_shapes=[pltpu.VMEM((tm, HEADS_W), BF16), pltpu.VMEM((tm, HEADS_W), BF16)], compiler_params=_params(1),
    )(dh, oa, ob, proj, proj, proj, proj, h, gn8, wa, wb, wa_t, wb_t, wout_t)


def _loss_head(h, fw8, target, nseq, t_len):
    n = h.shape[0]
    nc = t_len // GDN_CHUNK
    sub = 3 if nc % 3 == 0 else 1
    tl, nt = sub * GDN_CHUNK, nc // sub
    inv_d = 1.0 / D_MODEL

    def body(h_ref, fw_ref, *rest):
        tgt_refs, (dh_ref, acc_ref) = rest[:sub], rest[sub:]

        @pl.when((pl.program_id(0) == 0) & (pl.program_id(1) == 0))
        def _():
            acc_ref[...] = jnp.zeros_like(acc_ref)

        frames = ((pl.program_id(1) * tl + _iota2((tl, 1), 0)) >= N_PAD + N_META).astype(F32)
        y, vjp = jax.vjp(_rms, h_ref[...], fw_ref[0:1, :])
        err = (y - jnp.concatenate([r[...] for r in tgt_refs], axis=0)) * frames
        dx, dfw = vjp(err * inv_d)
        dh_ref[...] = dx
        acc_ref[0:1, :] += dfw
        acc_ref[1:2, :] += (0.5 * inv_d) * jnp.sum(err * err, axis=0, keepdims=True)

    tgt_spec = lambda u: BS((None, GDN_CHUNK, D_MODEL), lambda s, t: (s, jnp.maximum(t * sub + u - 1, 0), 0))
    return pl.pallas_call(
        body, name="loss_head", grid=(nseq, nt),
        in_specs=[BS((tl, D_MODEL), lambda s, t: (s * nt + t, 0)), BS((SUBLANES, D_MODEL), lambda s, t: (0, 0))]
        + [tgt_spec(u) for u in range(sub)],
        out_specs=[BS((tl, D_MODEL), lambda s, t: (s * nt + t, 0)), BS((SUBLANES, D_MODEL), lambda s, t: (0, 0))],
        out_shape=[SDS((n, D_MODEL), F32), SDS((SUBLANES, D_MODEL), F32)], compiler_params=_params(2),
    )(h, fw8, *[target] * sub)


def _prep_bwd(proj, dq, dk, dv, db, dg, dqb, dkb, dlf, cw8, aux, lb8, nseq, t_len, tag):
    n = proj.shape[0]
    tt = _pick(t_len, (192, 128, 64))
    nt_ = t_len // tt
    qkv_w = 3 * HEADS_W
    rb = tt // SUBLANES
    ext = tt + SUBLANES

    def body(cur_ref, prev_ref, next_ref, misc_ref, bq_ref, bf_ref, dq_ref, dqn_ref, dk_ref, dkn_ref, dv_ref, dvn_ref,
             db_ref, dg_ref, dqb_ref, dkb_ref, dlf_ref, cw_ref, aux_ref, lb_ref,
             dqkv_ref, dmisc_ref, dbq_ref, dbf_ref, dcw_ref, daux_ref, dlb_ref, dy_ref):
        s, t = pl.program_id(0), pl.program_id(1)

        @pl.when((s == 0) & (t == 0))
        def _():
            dcw_ref[...] = jnp.zeros_like(dcw_ref)
            daux_ref[...] = jnp.zeros_like(daux_ref)
            dlb_ref[...] = jnp.zeros_like(dlb_ref)

        prev = jnp.where(t == 0, 0.0, prev_ref[...])
        x_ext = jnp.concatenate([prev, cur_ref[...], next_ref[...]], axis=0)
        y = _conv_ext(x_ext, cw_ref)
        inside = (t < nt_ - 1) | (_iota2((ext, 1), 0) < tt)
        dy_ref[0:SUBLANES, :] = jnp.zeros((SUBLANES, qkv_w), F32)
        for hd in range(N_HEADS):
            for grp, (g_ref, gn_ref, scale) in enumerate(((dq_ref, dqn_ref, D_HEAD ** -0.5), (dk_ref, dkn_ref, 1.0),
                                                          (dv_ref, dvn_ref, None))):
                lo = grp * HEADS_W + hd * D_HEAD
                sl = slice(hd * D_HEAD, (hd + 1) * D_HEAD)
                cot = jnp.concatenate([g_ref[:, sl], gn_ref[:, sl]], axis=0)
                fn = _silu if scale is None else functools.partial(_l2n_act, scale=scale)
                _, vjp = jax.vjp(fn, y[:, lo:lo + D_HEAD])
                dy_ref[SUBLANES:, lo:lo + D_HEAD] = jnp.where(inside, vjp(cot)[0], 0.0)
        dy_ext = dy_ref[...]
        dx = dy_ext * cw_ref[3:4, :]
        for kk in range(3):
            dx = dx + _shift_up(dy_ext, 3 - kk) * cw_ref[kk:kk + 1, :]
        dqkv_ref[...] = dx[SUBLANES:SUBLANES + tt].astype(BF16)
        dy_cur = dy_ext[SUBLANES:SUBLANES + tt]
        for kk in range(4):
            xs = _shift_down(x_ext, 3 - kk)[SUBLANES:SUBLANES + tt]
            dcw_ref[kk:kk + 1, :] += jnp.sum(xs * dy_cur, axis=0, keepdims=True)

        real = (t * tt + _iota2((tt, 1), 0)) >= N_PAD
        dmisc = jnp.zeros((tt, LANES), F32)
        daux = jnp.zeros((SUBLANES, LANES), F32)
        for hd in range(N_HEADS):
            sl = slice(hd * D_HEAD, (hd + 1) * D_HEAD)
            _, vjp = jax.vjp(lambda m, a: _gdn_gates(m, a, real, hd), misc_ref[...], aux_ref[...])
            dm, da = vjp((db_ref[:, sl], dg_ref[:, sl]))
            dmisc, daux = dmisc + dm, daux + da
        dmisc_ref[...] = dmisc.astype(BF16)
        daux_ref[...] += daux
        _, vjp = jax.vjp(lambda a, b, c: _hgrn_prep(a, b, c, real), bq_ref[...], bf_ref[...], lb_ref[0:1, :])
        dbq, dbf, dlb = vjp((dqb_ref[...], dkb_ref[...], dlf_ref[...]))
        dbq_ref[...], dbf_ref[...] = dbq.astype(BF16), dbf.astype(BF16)
        dlb_ref[0:1, :] += dlb

    row = lambda s, t: s * nt_ + t
    cur = lambda s, t: (row(s, t), 0)
    nxt = lambda s, t: (jnp.minimum((row(s, t) + 1) * rb, n // SUBLANES - 1), 0)
    wide = BS((tt, HEADS_W), cur)
    halo = BS((SUBLANES, HEADS_W), nxt)
    full = lambda s, t: (0, 0)
    return pl.pallas_call(
        body, name=f"prep_bwd_{tag}", grid=(nseq, nt_),
        in_specs=[BS((tt, qkv_w), cur), BS((SUBLANES, qkv_w), lambda s, t: (jnp.maximum(row(s, t) * rb - 1, 0), 0)),
                  BS((SUBLANES, qkv_w), nxt), BS((tt, LANES), lambda s, t: (row(s, t), C_MISC // LANES)),
                  BS((tt, HEADS_W), lambda s, t: (row(s, t), C_BQ // HEADS_W)),
                  BS((tt, HEADS_W), lambda s, t: (row(s, t), C_BF // HEADS_W)),
                  wide, halo, wide, halo, wide, halo, wide, wide, wide, wide, wide,
                  BS((SUBLANES, qkv_w), full), BS((SUBLANES, LANES), full), BS((SUBLANES, HEADS_W), full)],
        out_specs=[BS((tt, qkv_w), cur), BS((tt, LANES), cur), wide, wide,
                   BS((SUBLANES, qkv_w), full), BS((SUBLANES, LANES), full), BS((SUBLANES, HEADS_W), full)],
        out_shape=[SDS((n, qkv_w), BF16), SDS((n, LANES), BF16), SDS((n, HEADS_W), BF16), SDS((n, HEADS_W), BF16),
                   SDS((SUBLANES, qkv_w), F32), SDS((SUBLANES, LANES), F32), SDS((SUBLANES, HEADS_W), F32)],
        scratch_shapes=[pltpu.VMEM((tt + 2 * SUBLANES, qkv_w), F32)], compiler_params=_params(2),
    )(proj, proj, proj, proj, proj, proj, dq, dq, dk, dk, dv, dv, db, dg, dqb, dkb, dlf, cw8, aux, lb8)


def _proj_bwd_x(pieces, wp_t, h, nw8, dh_res, tag):
    n = h.shape[0]
    tm = _pick(n, (256, 192, 128, 64))
    widths = [p.shape[1] for p in pieces]
    assert sum(widths) == PROJ_W

    def body(*refs):
        p_refs = refs[:len(pieces)]
        wt_ref, h_ref, nw_ref, dres_ref, dh_ref, dnw_ref = refs[len(pieces):]

        @pl.when(pl.program_id(0) == 0)
        def _():
            dnw_ref[...] = jnp.zeros_like(dnw_ref)

        dxn, off = None, 0
        for p_ref, w in zip(p_refs, widths):
            part = jnp.dot(p_ref[...], wt_ref[off:off + w, :], preferred_element_type=F32)
            dxn = part if dxn is None else dxn + part
            off += w
        _, vjp = jax.vjp(_rms, h_ref[...], nw_ref[0:1, :])
        dx, dnw = vjp(dxn)
        dh_ref[...] = dres_ref[...] + dx
        dnw_ref[0:1, :] += dnw

    r2 = lambda i: (i, 0)
    full = lambda i: (0, 0)
    return pl.pallas_call(
        body, name=f"proj_bwd_x_{tag}", grid=(n // tm,),
        in_specs=[BS((tm, w), r2) for w in widths] + [BS((PROJ_W, D_MODEL), full), BS((tm, D_MODEL), r2),
                                                      BS((SUBLANES, D_MODEL), full), BS((tm, D_MODEL), r2)],
        out_specs=[BS((tm, D_MODEL), r2), BS((SUBLANES, D_MODEL), full)],
        out_shape=[SDS((n, D_MODEL), F32), SDS((SUBLANES, D_MODEL), F32)], compiler_params=_params(1),
    )(*pieces, wp_t, h, nw8, dh_res)


def _proj_bwd_w(xn, pieces, tag):
    n = xn.shape[0]
    tm = _pick(n, (384, 256, 192, 128, 64))
    widths = [p.shape[1] for p in pieces]
    assert sum(widths) == PROJ_W

    def body(*refs):
        x_ref, p_refs = refs[0], refs[1:1 + len(pieces)]
        o_ref, acc_ref = refs[1 + len(pieces):]

        @pl.when(pl.program_id(0) == 0)
        def _():
            acc_ref[...] = jnp.zeros_like(acc_ref)

        off = 0
        for p_ref, w in zip(p_refs, widths):
            acc_ref[:, off:off + w] += _dg(x_ref[...], p_ref[...], ((0,), (0,)))
            off += w

        @pl.when(pl.program_id(0) == pl.num_programs(0) - 1)
        def _():
            pltpu.sync_copy(acc_ref, o_ref)

    r2 = lambda i: (i, 0)
    return pl.pallas_call(
        body, name=f"proj_bwd_w_{tag}", grid=(n // tm,),
        in_specs=[BS((tm, D_MODEL), r2)] + [BS((tm, w), r2) for w in widths], out_specs=BS(memory_space=pl.ANY),
        out_shape=SDS((D_MODEL, PROJ_W), F32), scratch_shapes=[pltpu.VMEM((D_MODEL, PROJ_W), F32)],
        compiler_params=_params(1),
    )(xn, *pieces)


def _adamw(w, g, m, v, name):
    lead, rows, cols = w.shape
    tr = _pick(rows, (256, 128, 64, 32, 16, 8, 4, 2, 1)) if rows > 256 else rows

    def body(w_ref, g_ref, m_ref, v_ref, d_ref, nm_ref, nv_ref):
        gr = g_ref[...]
        m_new = ADAM_B1 * m_ref[...] + (1.0 - ADAM_B1) * gr
        v_new = ADAM_B2 * v_ref[...] + (1.0 - ADAM_B2) * jnp.square(gr)
        m_hat = m_new / (1.0 - ADAM_B1 ** ADAM_STEP)
        v_hat = v_new / (1.0 - ADAM_B2 ** ADAM_STEP)
        d_ref[...] = -ADAM_LR * (m_hat / (jnp.sqrt(v_hat) + ADAM_EPS) + ADAM_WD * w_ref[...])
        nm_ref[...] = m_new
        nv_ref[...] = v_new

    blk = BS((None, tr, cols), lambda a, i: (a, i, 0))
    return pl.pallas_call(
        body, name=name, grid=(lead, rows // tr), in_specs=[blk] * 4, out_specs=[blk] * 3,
        out_shape=[SDS((lead, rows, cols), F32)] * 3, compiler_params=_params(2),
    )(w, g, m, v)


def _row8(v, width):
    v = jnp.atleast_2d(v).astype(F32)
    return jnp.pad(v, ((0, SUBLANES - v.shape[0]), (0, width - v.shape[1])))


REF_MISC = 1536
N_MISC = 2 * N_HEADS
LAYOUT_RUNS = ((0, REF_MISC, 0), (REF_MISC + N_MISC, REF_W, REF_MISC), (REF_MISC, REF_MISC + N_MISC, C_MISC))


def _to_layout(w_full):
    runs = [w_full[:, lo:hi] for lo, hi, _ in sorted(LAYOUT_RUNS, key=lambda run: run[2])]
    return jnp.concatenate(runs + [jnp.zeros((w_full.shape[0], PROJ_W - REF_W), w_full.dtype)], axis=1)


def _from_layout(dw, n_slabs):
    width = REF_W // n_slabs
    slabs = []
    for j in range(n_slabs):
        pieces = []
        for lo, hi, at in sorted(LAYOUT_RUNS):
            a, b = max(lo, j * width), min(hi, (j + 1) * width)
            if a < b:
                pieces.append(dw[:, at + a - lo:at + b - lo])
        slabs.append(jnp.concatenate(pieces, axis=1))
    return slabs


def _lower_bounds(lb):
    sm = jax.nn.softmax(lb.astype(F32), axis=0)
    return jnp.cumsum(sm, axis=0) - sm[0]


def kernel(x, meta_tokens, norm_w, w_in, conv_w, a_log, dt_bias, gnorm_a, gnorm_b, hgrn_lower_bounds, w_branch_a, w_branch_b, w_out, final_norm_w, loss_target, m_meta_tokens, m_norm_w, m_w_in, m_conv_w, m_a_log, m_dt_bias, m_gnorm_a, m_gnorm_b, m_hgrn_lower_bounds, m_w_branch_a, m_w_branch_b, m_w_out, m_final_norm_w, v_meta_tokens, v_norm_w, v_w_in, v_conv_w, v_a_log, v_dt_bias, v_gnorm_a, v_gnorm_b, v_hgrn_lower_bounds, v_w_branch_a, v_w_branch_b, v_w_out, v_final_norm_w):
    nseq, seq, _ = x.shape
    depth = norm_w.shape[0]
    t_len = N_PAD + N_META + seq
    n = nseq * t_len
    win_c, conv_c = w_in.shape[2], conv_w.shape[2]
    my = 4 * lax.axis_index("x") + 2 * lax.axis_index("y") + lax.axis_index("c")

    assert depth >= 2
    by_cols = lambda g: g.transpose(1, 2, 0, 3).reshape(g.shape[1], g.shape[2], N_DEV * g.shape[3])
    first = _all_gather_hbm([w_in[:1].astype(BF16), conv_w, meta_tokens], "gather_first")
    later_flight, later_token = _send_all_start(
        [w_in[1:].astype(BF16), w_branch_a.astype(BF16), w_branch_b.astype(BF16), w_out.astype(BF16)], False,
        "gather_later_start")
    w_in_full = [by_cols(first[0])]
    conv_full = by_cols(first[1])
    meta_full = first[2].transpose(1, 0, 2).reshape(N_META, D_MODEL)

    lb_all, lb_vjp = jax.vjp(_lower_bounds, hgrn_lower_bounds)

    h = jnp.concatenate([jnp.zeros((nseq, N_PAD, D_MODEL), F32),
                         jnp.broadcast_to(meta_full[None], (nseq, N_META, D_MODEL)), x], axis=1).reshape(n, D_MODEL)
    saved = []
    for l in range(depth):
        wp = _to_layout(w_in_full[0][0] if l == 0 else w_in_full[1][l - 1])
        nw8 = _row8(norm_w[l], D_MODEL)
        if l == 0:
            nw8 = nw8 + later_token[0:1, 0:1]
        cw8 = _row8(conv_full[l], 3 * HEADS_W)
        aux = _row8(jnp.stack([a_log[l], dt_bias[l]]), LANES)
        lb8 = _row8(lb_all[l], HEADS_W)
        gn8 = _row8(jnp.stack([gnorm_a[l], gnorm_b[l]]), LANES)
        proj, xn = _proj_fwd(h, nw8, wp, l)
        q, k, v, b, g, qb, kb, lf = _prep_fwd(proj, cw8, aux, lb8, nseq, t_len, l)
        oa, ob, sck_a, sck_b = _mixers_fwd(q, k, v, b, g, qb, kb, proj, C_BI // HEADS_W, lf, nseq, t_len, l)
        if l == 0:
            sent, landed = _send_all_wait(later_flight, ob, "gather_later_wait")
            landed = [lax.dynamic_update_slice(ld, own[None], (my,) + (0,) * own.ndim) for ld, own in zip(landed, sent)]
            w_in_full.append(by_cols(landed[0]))
            wa_full, wb_full = by_cols(landed[1]), by_cols(landed[2])
            wout_full = landed[3].transpose(1, 0, 2, 3).reshape(depth, D_MODEL, D_MODEL)
        wa_l, wb_l, wout_l = wa_full[l], wb_full[l], wout_full[l]
        h_next = _post_fwd(oa, ob, proj, h, gn8, wa_l, wb_l, wout_l, l)
        saved.append(dict(h=h, wp=wp, nw8=nw8, cw8=cw8, aux=aux, lb8=lb8, gn8=gn8, proj=proj, xn=xn, q=q, k=k, v=v, b=b,
                          wa=wa_l, wb=wb_l, wout=wout_l,
                          g=g, qb=qb, kb=kb, lf=lf, oa=oa, ob=ob, sck_a=sck_a, sck_b=sck_b))
        h = h_next

    dh, acc = _loss_head(h, _row8(final_norm_w, D_MODEL), loss_target, nseq, t_len)

    g_win, g_wa, g_wb, g_wout, g_conv, small = [], [], [], [], [], []

    def mixer_slabs(dwa_s, dwb_s, dwout_s):
        nl = len(dwa_s)
        rows = lambda a: jnp.stack(a).reshape(nl * HEADS_W, N_DEV, LANES).transpose(1, 0, 2)
        wout = jnp.stack(dwout_s).reshape(nl, N_DEV, LANES, D_MODEL).transpose(1, 0, 2, 3)
        return [jnp.concatenate([rows(dwa_s), rows(dwb_s)], axis=1).astype(BF16),
                wout.reshape(N_DEV, nl * LANES, D_MODEL).astype(BF16)]

    def win_slabs(per_layer, dtype):
        return jnp.stack([jnp.concatenate([sl[j] for sl in per_layer], axis=0) for j in range(N_DEV)]).astype(dtype)

    for l in reversed(range(depth)):
        s = saved[l]
        gn8, aux = s["gn8"], s["aux"]
        if l == 0:
            later_flight, later_token = _send_all_start(
                [win_slabs(g_win[::-1], BF16)] + mixer_slabs(g_wa[::-1], g_wb[::-1], g_wout[::-1]), True,
                "scatter_later_start")
            gn8 = gn8 + later_token[0:1, 0:1]
        doa, dob, dz, dbg, dga, dgb, dwa, dwb, dwout, dgn = _post_bwd(
            dh, s["oa"], s["ob"], s["proj"], s["h"], gn8, s["wa"], s["wb"], s["wa"].T, s["wb"].T, s["wout"].T, l)
        dq, dk, dv, db, dg, dqb, dkb, dbi, dlf = _mixers_bwd(
            s["q"], s["k"], s["v"], s["b"], s["g"], s["qb"], s["kb"], s["proj"], C_BI // HEADS_W, s["lf"], s["sck_a"],
            s["sck_b"], doa, dob, nseq, t_len, l)
        if l == 0:
            mixer_flight, mixer_token = _send_all_start(mixer_slabs([dwa], [dwb], [dwout]), True, "scatter_first_start")
            aux = aux + mixer_token[0:1, 0:1]
        dqkv, dmisc, dbq, dbf, dcw, daux, dlb = _prep_bwd(s["proj"], dq, dk, dv, db, dg, dqb, dkb, dlf, s["cw8"], aux,
                                                          s["lb8"], nseq, t_len, l)
        pieces = [dqkv, dz, dbq, dbf, dbi, dbg, dga, dgb, dmisc]
        g_win.append(_from_layout(_proj_bwd_w(s["xn"], pieces, l), N_DEV))
        g_conv.append(dcw[:4])
        nw8 = s["nw8"]
        if l == 0:
            dconv = jnp.stack(g_conv[::-1])
            conv_slabs = dconv.reshape(depth * dconv.shape[1], N_DEV, conv_c).transpose(1, 0, 2)
            win_flight, win_token = _send_all_start([win_slabs(g_win[-1:], BF16), conv_slabs], True, "scatter_win_start")
            nw8 = nw8 + win_token[0:1, 0:1]
        dh, dnw = _proj_bwd_x(pieces, s["wp"].T, s["h"], nw8, dh, l)
        g_wa.append(dwa)
        g_wb.append(dwb)
        g_wout.append(dwout)
        small.append((dnw[0], dgn[0], dgn[1], daux[0, :N_HEADS], daux[1, :N_HEADS], dlb[0]))
    small.reverse()
    dh = dh.reshape(nseq, t_len, D_MODEL)
    grad_x = dh[:, N_PAD + N_META:]

    packed = jnp.concatenate([small[0][1], small[1][1], small[0][2], small[1][2], small[0][3], small[1][3],
                              small[0][4], small[1][4]])
    tile = jnp.concatenate([
        jnp.sum(dh[:, N_PAD:N_PAD + N_META], axis=0), _row8(jnp.stack([small[0][0], small[1][0], acc[0]]), D_MODEL),
        _row8(jnp.stack([small[0][5], small[1][5]]), D_MODEL), _row8(packed, D_MODEL), _row8(acc[1], D_MODEL)], axis=0)
    tile = _all_reduce_small(tile, "reduce_small")
    loss = jnp.sum(tile[40])
    g_meta = lax.dynamic_slice_in_dim(tile[0:N_META], my * LANES, LANES, axis=1)
    g_norm, g_final = tile[16:18], tile[18]
    (g_lb,) = lb_vjp(tile[24:26, :HEADS_W])
    r21 = tile[32]
    g_gna, g_gnb = r21[0:256].reshape(2, LANES), r21[256:512].reshape(2, LANES)
    g_alog, g_dtb = r21[512:520].reshape(2, N_HEADS), r21[520:528].reshape(2, N_HEADS)

    def landed_sums(flight, tag):
        sent, landed = _send_all_wait(flight, dh, f"{tag}_wait")
        landed = [lax.dynamic_update_slice(ld, lax.dynamic_index_in_dim(src, my, 0, keepdims=True), (my, 0, 0))
                  for ld, src in zip(landed, sent)]
        return [_sum_slabs(ld, f"{tag}_sum{i}") for i, ld in enumerate(landed)]

    l_win, l_ab, l_wout = landed_sums(later_flight, "scatter_later")
    r_ab, r_wout = landed_sums(mixer_flight, "scatter_first")
    r_win, r_conv = landed_sums(win_flight, "scatter_win")
    both = lambda a, b, shape: jnp.concatenate([a.reshape(1, *shape[1:]), b.reshape(depth - 1, *shape[1:])])
    half, half_l = HEADS_W, (depth - 1) * HEADS_W
    mine = [both(r_win, l_win, w_in.shape), both(r_ab[:half], l_ab[:half_l], w_branch_a.shape),
            both(r_ab[half:], l_ab[half_l:], w_branch_b.shape), both(r_wout, l_wout, w_out.shape), r_conv]
    gseg = lambda i, shape: mine[i].reshape(shape)
    grads = {
        "meta_tokens": g_meta, "norm_w": g_norm, "w_in": gseg(0, w_in.shape), "conv_w": gseg(4, conv_w.shape),
        "a_log": g_alog, "dt_bias": g_dtb, "gnorm_a": g_gna, "gnorm_b": g_gnb, "hgrn_lower_bounds": g_lb,
        "w_branch_a": gseg(1, w_branch_a.shape), "w_branch_b": gseg(2, w_branch_b.shape), "w_out": gseg(3, w_out.shape),
        "final_norm_w": g_final}
    weights = {
        "meta_tokens": (meta_tokens, m_meta_tokens, v_meta_tokens), "norm_w": (norm_w, m_norm_w, v_norm_w),
        "w_in": (w_in, m_w_in, v_w_in), "conv_w": (conv_w, m_conv_w, v_conv_w), "a_log": (a_log, m_a_log, v_a_log),
        "dt_bias": (dt_bias, m_dt_bias, v_dt_bias), "gnorm_a": (gnorm_a, m_gnorm_a, v_gnorm_a),
        "gnorm_b": (gnorm_b, m_gnorm_b, v_gnorm_b),
        "hgrn_lower_bounds": (hgrn_lower_bounds, m_hgrn_lower_bounds, v_hgrn_lower_bounds),
        "w_branch_a": (w_branch_a, m_w_branch_a, v_w_branch_a), "w_branch_b": (w_branch_b, m_w_branch_b, v_w_branch_b),
        "w_out": (w_out, m_w_out, v_w_out), "final_norm_w": (final_norm_w, m_final_norm_w, v_final_norm_w)}
    names = list(weights)
    deltas, new_m, new_v = [], [], []
    for nm in names:
        w, m, v = weights[nm]
        view = (1,) * (3 - w.ndim) + w.shape
        d, m2, v2 = _adamw(w.reshape(view), grads[nm].reshape(view), m.reshape(view), v.reshape(view), f"adamw_{nm}")
        deltas.append(d.reshape(w.shape))
        new_m.append(m2.reshape(w.shape))
        new_v.append(v2.reshape(w.shape))
    return (loss, grad_x, *[grads[nm].reshape(weights[nm][0].shape) for nm in names], *deltas, *new_m, *new_v)
```

```python
import functools

import jax
import jax.numpy as jnp
import numpy as np
from jax import lax
from jax.experimental import pallas as pl
from jax.experimental.pallas import tpu as pltpu

F32 = jnp.float32
BF16 = jnp.bfloat16

D_MODEL = 1024
N_HEADS = 4
D_HEAD = 128
HEADS_W = N_HEADS * D_HEAD
N_META = 16
N_PAD = 48
GDN_CHUNK = 64
HGRN_CHUNK = 16
EPS = 1e-6
N_DEV = 8
LANES = 128
SUBLANES = 8
VMEM_LIMIT = 56 * 1024 * 1024

C_QKV, C_Z, C_BQ, C_BF, C_BI, C_BG, C_GA, C_GB, C_MISC = 0, 1536, 2048, 2560, 3072, 3584, 4096, 5120, 6144
PROJ_W = 6272
REF_W = 6152

ADAM_LR, ADAM_B1, ADAM_B2, ADAM_EPS, ADAM_WD, ADAM_STEP = 0.001, 0.9, 0.999, 1e-08, 0.01, 10

MESH = pl.DeviceIdType.MESH
SDS = jax.ShapeDtypeStruct
BS = pl.BlockSpec


def _params(n_axes):
    return pltpu.CompilerParams(dimension_semantics=("arbitrary",) * n_axes, vmem_limit_bytes=VMEM_LIMIT)


def _pick(n, cands):
    for c in cands:
        if n % c == 0:
            return c
    raise ValueError(f"no tile for {n} among {cands}")


def _iota2(shape, dim):
    return lax.broadcasted_iota(jnp.int32, shape, dim)


def _dg(a, b, dims):
    return lax.dot_general(a.astype(BF16), b.astype(BF16), (dims, ((), ())), preferred_element_type=F32)


def _bdg(a, b, ca, cb):
    return lax.dot_general(a.astype(BF16), b.astype(BF16), (((ca,), (cb,)), ((0,), (0,))), preferred_element_type=F32)


@jax.custom_vjp
def _bnn(a, b):
    return _bdg(a, b, 2, 1)


@jax.custom_vjp
def _bnt(a, b):
    return _bdg(a, b, 2, 2)


@jax.custom_vjp
def _btn(a, b):
    return _bdg(a, b, 1, 1)


_bnn.defvjp(lambda a, b: (_bnn(a, b), (a, b)), lambda r, g: (_bnt(g, r[1]), _btn(r[0], g)))
_bnt.defvjp(lambda a, b: (_bnt(a, b), (a, b)), lambda r, g: (_bnn(g, r[1]), _btn(g, r[0])))
_btn.defvjp(lambda a, b: (_btn(a, b), (a, b)), lambda r, g: (_bnt(r[1], g), _bnn(r[0], g)))


def _split2(x):
    hi = x.astype(BF16).astype(F32)
    return hi, x - hi


def _tri(bsz, n):
    return jnp.broadcast_to((_iota2((n, n), 0) >= _iota2((n, n), 1)).astype(F32), (bsz, n, n))


@jax.custom_vjp
def _cumsum_rows(x):
    tri = _tri(x.shape[0], x.shape[1])
    hi, lo = _split2(x)
    return _bdg(tri, hi, 2, 1) + _bdg(tri, lo, 2, 1)


def _cumsum_rows_bwd(_, g):
    tri = _tri(g.shape[0], g.shape[1])
    hi, lo = _split2(g)
    return (_bdg(tri, hi, 1, 1) + _bdg(tri, lo, 1, 1),)


_cumsum_rows.defvjp(lambda x: (_cumsum_rows(x), None), _cumsum_rows_bwd)


def _sigmoid(x):
    return jax.nn.sigmoid(x)


def _silu(x):
    return x * _sigmoid(x)


def _softplus(x):
    return jnp.maximum(x, 0.0) + jnp.log1p(jnp.exp(-jnp.abs(x)))


def _rms(x, w):
    return x * lax.rsqrt(jnp.mean(x * x, axis=-1, keepdims=True) + EPS) * w


@jax.custom_vjp
def _inv_unit_lower(lm):
    n = lm.shape[1]
    a = (_iota2((n, n), 0) == _iota2((n, n), 1)).astype(F32)[None] - lm
    steps = max(1, (n - 1).bit_length()) - 1
    p = _bnn(lm, lm)
    for i in range(steps):
        if i == steps - 1:
            a = a + _bnn(a, p)
        else:
            both = _bnn(jnp.concatenate([a, p], axis=1), p)
            a, p = a + both[:, :n], both[:, n:]
    return a


_inv_unit_lower.defvjp(lambda lm: (lambda a: (a, a))(_inv_unit_lower(lm)),
                       lambda a, g: (-_bnt(_btn(a, g), a),))


def _gdn_chunk(q, k, v, b_b, g_b, s):
    n, dv = q.shape[1], v.shape[2]
    r, c = _iota2((n, n), 0), _iota2((n, n), 1)
    causal, strict, eye = (r >= c)[None], (r > c)[None], (r == c)[None]
    g_cum = _cumsum_rows(g_b)
    g_i = g_cum[:, :, :n]
    g_j = jnp.sum(jnp.where(eye, g_i, 0.0), axis=1, keepdims=True)
    decay = jnp.where(causal, jnp.exp(jnp.where(causal, g_i - g_j, 0.0)), 0.0)
    e_g = jnp.exp(g_cum)
    kb = k * b_b
    kk = _bnt(jnp.concatenate([kb, q], axis=1), k)
    a_inv = _inv_unit_lower(jnp.where(strict, kk[:, :n] * decay, 0.0))
    uw = _bnn(a_inv, jnp.concatenate([v * b_b, kb * e_g], axis=2))
    ws = _bnn(jnp.concatenate([uw[:, :, dv:], q * e_g], axis=1), s)
    v_new = uw[:, :, :dv] - ws[:, :n]
    o = ws[:, n:] + _bnn(kk[:, n:] * decay, v_new)
    g_last = g_cum[:, n - 1:n, :]
    s_new = s * jnp.exp(g_last) +_btn(k * jnp.exp(g_last - g_cum), v_new)
    return o, s_new


@functools.partial(jax.custom_vjp, nondiff_argnums=(1, 2))
def _row(x, j, n):
    return x[:, j:j + 1, :]


def _row_bwd(j, n, _, g):
    return (jnp.where(_iota2((1, n, 1), 1) == j, g, 0.0),)


_row.defvjp(lambda x, j, n: (_row(x, j, n), None), _row_bwd)


def _hgrn_pairs(q, k, v, b_cum):
    n = q.shape[1]
    half = n // 2 if n > SUBLANES else n
    parts = []
    for lo in range(0, n, half):
        qs, bs = q[:, lo:], b_cum[:, lo:]
        rows = _iota2((1, n - lo, 1), 1) + lo
        acc = jnp.zeros_like(qs)
        for j in range(lo, lo + half):
            p = jnp.exp(jnp.where(rows >= j, bs - _row(b_cum, j, n), -1e30))
            acc = acc + jnp.sum(qs * _row(k, j, n) * p, axis=2, keepdims=True) * _row(v, j, n)
        parts.append(acc)
    if len(parts) == 1:
        return parts[0]
    return parts[0] + jnp.concatenate([jnp.zeros_like(parts[1]), parts[1]], axis=1)


def _hgrn_block(q, k, v, lf, st, group=HGRN_CHUNK):
    n, rows = group, q.shape[1]
    b_cum = _cumsum_rows(lf)
    outs = []
    for c in range(rows // n):
        rs = slice(c * n, (c + 1) * n)
        o = _hgrn_pairs(q[:, rs], k[:, rs], v[:, rs], b_cum[:, rs])
        if c:
            b_c = _row(b_cum, c * n - 1, rows)
            scores = _bnt(q[:, rs] * jnp.exp(b_cum[:, rs] - b_c), k[:, :c * n] * jnp.exp(b_c - b_cum[:, :c * n]))
            o = o + _bnn(scores, v[:, :c * n])
        outs.append(o)
    b_last = _row(b_cum, rows - 1, rows)
    o = _bnt(q * jnp.exp(b_cum), st) + jnp.concatenate(outs, axis=1)
    return o, st * jnp.exp(b_last) + _btn(v, k * jnp.exp(b_last - b_cum))


def _l2n_act(y, scale):
    a = _silu(y)
    return a * lax.rsqrt(jnp.sum(a * a, axis=-1, keepdims=True) + EPS) * scale


def _col(x, lane):
    return jnp.sum(jnp.where(_iota2(x.shape, 1) == lane, x, 0.0), axis=1, keepdims=True)


def _elem(x, row, lane):
    m = (_iota2(x.shape, 0) == row) & (_iota2(x.shape, 1) == lane)
    return jnp.sum(jnp.sum(jnp.where(m, x, 0.0), axis=1, keepdims=True), axis=0, keepdims=True)


def _gdn_gates(misc, aux, real, head):
    beta = _sigmoid(_col(misc, head))
    g = -jnp.exp(_elem(aux, 0, head)) * _softplus(_col(misc, N_HEADS + head) + _elem(aux, 1, head))
    g = jnp.where(real, g, 0.0)
    shape = (misc.shape[0], D_HEAD)
    return jnp.broadcast_to(beta, shape), jnp.broadcast_to(g, shape)


def _hgrn_prep(bq, bf, lb, real):
    qb = _silu(bq) * (D_HEAD ** -0.5)
    log_sig = jnp.minimum(bf, 0.0) - jnp.log1p(jnp.exp(-jnp.abs(bf)))
    pos = lb > 0.0
    lbs = jnp.where(pos, lb, 0.5)
    a = jnp.log(lbs)
    b = jnp.log1p(-lbs) + log_sig
    lae = jnp.maximum(a, b) + jnp.log1p(jnp.exp(-jnp.abs(a - b)))
    lf = jnp.where(pos, lae, log_sig)
    kb = jnp.where(pos, 1.0 - lbs, 1.0) * _sigmoid(-bf)
    return qb, jnp.where(real, kb, 0.0), jnp.where(real, lf, 0.0)


def _gated_norm(o, z, gw):
    return o * lax.rsqrt(jnp.mean(o * o, axis=-1, keepdims=True) + EPS) * gw * _silu(z)


def _shift_down(x, j):
    return x if j == 0 else pltpu.roll(x, j, 0)


def _shift_up(x, j):
    return x if j == 0 else pltpu.roll(x, x.shape[0] - j, 0)


def _all_gather_hbm(blocks, name):
    na = len(blocks)

    def body(*refs):
        x_refs, out_refs = refs[:na], refs[na:2 * na]
        send_sems, recv_sems, local_sems = refs[2 * na:]
        mx, my, mc = lax.axis_index("x"), lax.axis_index("y"), lax.axis_index("c")
        me, sibling = (mx, my, mc), (mx, my, 1 - mc)
        chips = [(1 - mx, my), (mx, 1 - my), (1 - mx, 1 - my)]

        def slab(a, px, py, pc):
            return out_refs[a].at[4 * px + 2 * py + pc]

        def copy(a, k, blk, to, own=False):
            return pltpu.make_async_remote_copy(
                src_ref=x_refs[a] if own else slab(a, *blk), dst_ref=slab(a, *blk),
                send_sem=send_sems.at[7 * a + k], recv_sem=recv_sems.at[7 * a + k], device_id=to, device_id_type=MESH)

        mine = [pltpu.make_async_copy(x_refs[a], slab(a, *me), local_sems.at[a]) for a in range(na)]
        for cp in mine:
            cp.start()
        first = [copy(a, 0, me, sibling, own=True) for a in range(na)]
        first += [copy(a, 1 + j, me, (*chip, mc), own=True) for j, chip in enumerate(chips) for a in range(na)]
        for cp in first:
            cp.start()
        passed = []
        for j, chip in enumerate(chips):
            for a in range(na):
                copy(a, 1 + j, (*chip, mc), me).wait_recv()
                passed.append(copy(a, 4 + j, (*chip, mc), sibling))
                passed[-1].start()
        for a in range(na):
            copy(a, 0, sibling, me).wait_recv()
            for j, chip in enumerate(chips):
                copy(a, 4 + j, (*chip, 1 - mc), me).wait_recv()
        for cp in first + passed:
            cp.wait_send()
        for cp in mine:
            cp.wait()

    hbm = BS(memory_space=pl.ANY)
    return pl.pallas_call(
        body, name=name, out_shape=[SDS((N_DEV, *b.shape), b.dtype) for b in blocks],
        in_specs=[hbm] * na, out_specs=[hbm] * na,
        scratch_shapes=[pltpu.SemaphoreType.DMA((7 * na,)), pltpu.SemaphoreType.DMA((7 * na,)),
                        pltpu.SemaphoreType.DMA((na,))],
    )(*blocks)


def _all_reduce_small(block, name):
    r, c = block.shape

    def body(x_ref, out_ref, buf, send_sems, recv_sems):
        mx, my, mc = lax.axis_index("x"), lax.axis_index("y"), lax.axis_index("c")
        me, sibling = (mx, my, mc), (mx, my, 1 - mc)
        chips = [(1 - mx, my), (mx, 1 - my), (1 - mx, 1 - my)]

        def slab(px, py, pc):
            return buf.at[4 * px + 2 * py + pc]

        def copy(k, blk, to, src=None):
            return pltpu.make_async_remote_copy(
                src_ref=slab(*blk) if src is None else src, dst_ref=slab(*blk),
                send_sem=send_sems.at[k], recv_sem=recv_sems.at[k], device_id=to, device_id_type=MESH)

        first = [copy(0, me, sibling, src=x_ref)]
        first += [copy(1 + j, me, (*chip, mc), src=x_ref) for j, chip in enumerate(chips)]
        for cp in first:
            cp.start()
        passed = [copy(4 + j, (*chip, mc), sibling) for j, chip in enumerate(chips)]
        for j, chip in enumerate(chips):
            copy(1 + j, (*chip, mc), me).wait_recv()
            passed[j].start()
        copy(0, sibling, me).wait_recv()
        for j, chip in enumerate(chips):
            copy(4 + j, (*chip, 1 - mc), me).wait_recv()
        for cp in first + passed:
            cp.wait_send()
        buf[4 * mx + 2 * my + mc] = x_ref[...]
        acc = buf[0]
        for d in range(1, N_DEV):
            acc = acc + buf[d]
        out_ref[...] = acc

    return pl.pallas_call(
        body, name=name, out_shape=SDS((r, c), F32),
        in_specs=[BS(memory_space=pltpu.VMEM)], out_specs=BS(memory_space=pltpu.VMEM),
        scratch_shapes=[pltpu.VMEM((N_DEV, r, c), F32), pltpu.SemaphoreType.DMA((7,)), pltpu.SemaphoreType.DMA((7,))],
    )(block)


HBM_SPEC = BS(memory_space=pltpu.HBM)
SEM_SPEC = BS(memory_space=pltpu.SEMAPHORE)
SIDE_EFFECT = pltpu.SideEffectType.DATAFLOW_SIDE_EFFECTING


def _peer(rel):
    flip = lambda v, bit: 1 - v if bit else v
    return (flip(lax.axis_index("x"), rel >> 2 & 1), flip(lax.axis_index("y"), rel >> 1 & 1),
            flip(lax.axis_index("c"), rel & 1))


def _send_all_start(blocks, scatter, name):
    na = len(blocks)
    shapes = [b.shape[1:] if scatter else b.shape for b in blocks]

    def body(*refs):
        srcs, lands = refs[:na], refs[na:2 * na]
        send_sems, recv_sems, token = refs[2 * na], refs[2 * na + 1], refs[-1]
        me = 4 * lax.axis_index("x") + 2 * lax.axis_index("y") + lax.axis_index("c")
        for a in range(na):
            for rel in range(1, N_DEV):
                px, py, pc = _peer(rel)
                pltpu.make_async_remote_copy(
                    src_ref=srcs[a].at[4 * px + 2 * py + pc] if scatter else srcs[a], dst_ref=lands[a].at[me],
                    send_sem=send_sems.at[7 * a + rel - 1], recv_sem=recv_sems.at[7 * a + rel - 1],
                    device_id=(px, py, pc), device_id_type=MESH).start()
        token[...] = jnp.zeros_like(token)

    lands = [lax.empty((N_DEV, *s), b.dtype) for s, b in zip(shapes, blocks)]
    res = pl.pallas_call(
        body, name=name,
        out_shape=([pltpu.SemaphoreType.DMA((7 * na,)), pltpu.SemaphoreType.DMA((7 * na,))]
                   + [pltpu.HBM(b.shape, b.dtype) for b in blocks] + [pltpu.HBM(ld.shape, ld.dtype) for ld in lands]
                   + [SDS((SUBLANES, LANES), F32)]),
        in_specs=[HBM_SPEC] * (2 * na), out_specs=[SEM_SPEC, SEM_SPEC] + [HBM_SPEC] * (2 * na) + [BS(memory_space=pltpu.VMEM)],
        input_output_aliases={i: 2 + i for i in range(2 * na)},
        compiler_params=pltpu.CompilerParams(has_side_effects=SIDE_EFFECT),
    )(*[pltpu.with_memory_space_constraint(b, pltpu.HBM) for b in blocks],
      *[pltpu.with_memory_space_constraint(ld, pltpu.HBM) for ld in lands])
    return dict(send=res[0], recv=res[1], srcs=res[2:2 + na], lands=res[2 + na:2 + 2 * na], scatter=scatter), res[-1]


def _send_all_wait(flight, after, name):
    na = len(flight["srcs"])

    def body(*refs):
        srcs, lands = refs[:na], refs[na:2 * na]
        send_sems, recv_sems = refs[2 * na], refs[2 * na + 1]
        for a in range(na):
            for rel in range(1, N_DEV):
                cp = pltpu.make_async_remote_copy(
                    src_ref=srcs[a].at[0] if flight["scatter"] else srcs[a], dst_ref=lands[a].at[0],
                    send_sem=send_sems.at[7 * a + rel - 1], recv_sem=recv_sems.at[7 * a + rel - 1],
                    device_id=_peer(rel), device_id_type=MESH)
                cp.wait_send()
                cp.wait_recv()

    arrays = list(flight["srcs"]) + list(flight["lands"])
    res = pl.pallas_call(
        body, name=name, out_shape=[pltpu.HBM(a.shape, a.dtype) for a in arrays],
        in_specs=[HBM_SPEC] * (2 * na) + [SEM_SPEC, SEM_SPEC, BS(memory_space=pl.ANY)], out_specs=[HBM_SPEC] * (2 * na),
        input_output_aliases={i: i for i in range(2 * na)},
        compiler_params=pltpu.CompilerParams(has_side_effects=SIDE_EFFECT),
    )(*arrays, flight["send"], flight["recv"], after)
    return res[:na], res[na:]


def _sum_slabs(land, name):
    _, r, c = land.shape
    tr = _pick(r, (256, 128, 64, 32, 16, 8))

    def body(l_ref, o_ref):
        acc = l_ref[0].astype(F32)
        for d in range(1, N_DEV):
            acc = acc + l_ref[d].astype(F32)
        o_ref[...] = acc

    return pl.pallas_call(
        body, name=name, grid=(r // tr,), out_shape=SDS((r, c), F32),
        in_specs=[BS((N_DEV, tr, c), lambda j: (0, j, 0))], out_specs=BS((tr, c), lambda j: (j, 0)),
        compiler_params=_params(1),
    )(land)


def _proj_fwd(h, nw8, wp, tag):
    n = h.shape[0]
    tm = _pick(n, (1408, 768, 512, 384, 256, 192, 128, 64))
    tn = 896

    def body(h_ref, nw_ref, w_ref, proj_ref, xn_ref):
        @pl.when(pl.program_id(1) == 0)
        def _():
            xn_ref[...] = _rms(h_ref[...], nw_ref[0:1, :]).astype(BF16)

        proj_ref[...] = jnp.dot(xn_ref[...], w_ref[...], preferred_element_type=F32)

    return pl.pallas_call(
        body, name=f"proj_fwd_{tag}", grid=(n // tm, PROJ_W // tn),
        in_specs=[BS((tm, D_MODEL), lambda i, j: (i, 0)), BS((SUBLANES, D_MODEL), lambda i, j: (0, 0)),
                  BS((D_MODEL, tn), lambda i, j: (0, j))],
        out_specs=[BS((tm, tn), lambda i, j: (i, j)), BS((tm, D_MODEL), lambda i, j: (i, 0))],
        out_shape=[SDS((n, PROJ_W), F32), SDS((n, D_MODEL), BF16)], compiler_params=_params(2),
    )(h, nw8, wp)


def _conv_ext(x_ext, cw_ref):
    y = x_ext * cw_ref[3:4, :]
    for k in range(3):
        y = y + _shift_down(x_ext, 3 - k) * cw_ref[k:k + 1, :]
    return y[SUBLANES:]


def _prep_fwd(proj, cw8, aux, lb8, nseq, t_len, tag):
    n = proj.shape[0]
    tt = _pick(t_len, (192, 128, 64))
    nt_ = t_len // tt
    qkv_w = 3 * HEADS_W

    def body(cur_ref, prev_ref, misc_ref, bq_ref, bf_ref, cw_ref, aux_ref, lb_ref,
             q_ref, k_ref, v_ref, b_ref, g_ref, qb_ref, kb_ref, lf_ref, ext_ref):
        t = pl.program_id(1)
        ext_ref[0:SUBLANES, :] = jnp.where(t == 0, 0.0, prev_ref[...])
        ext_ref[SUBLANES:, :] = cur_ref[...]
        y = ext_ref[SUBLANES:, :] * cw_ref[3:4, :]
        for kk in range(3):
            y = y + ext_ref[SUBLANES - 3 + kk:SUBLANES - 3 + kk + tt, :] * cw_ref[kk:kk + 1, :]
        real = (t * tt + _iota2((tt, 1), 0)) >= N_PAD
        misc = misc_ref[...]
        auxv = aux_ref[...]
        for hd in range(N_HEADS):
            sl = slice(hd * D_HEAD, (hd + 1) * D_HEAD)
            q_ref[:, sl] = _l2n_act(y[:, sl], D_HEAD ** -0.5)
            k_ref[:, sl] = _l2n_act(y[:, HEADS_W + hd * D_HEAD:HEADS_W + (hd + 1) * D_HEAD], 1.0)
            v_ref[:, sl] = _silu(y[:, 2 * HEADS_W + hd * D_HEAD:2 * HEADS_W + (hd + 1) * D_HEAD])
            b_ref[:, sl], g_ref[:, sl] = _gdn_gates(misc, auxv, real, hd)
        qb_ref[...], kb_ref[...], lf_ref[...] = _hgrn_prep(bq_ref[...], bf_ref[...], lb_ref[0:1, :], real)

    rb = tt // SUBLANES
    row = lambda s, t: s * nt_ + t
    wide = BS((tt, HEADS_W), lambda s, t: (row(s, t), 0))
    return pl.pallas_call(
        body, name=f"prep_fwd_{tag}", grid=(nseq, nt_),
        in_specs=[BS((tt, qkv_w), lambda s, t: (row(s, t), 0)),
                  BS((SUBLANES, qkv_w), lambda s, t: (jnp.maximum(row(s, t) * rb - 1, 0), 0)),
                  BS((tt, LANES), lambda s, t: (row(s, t), C_MISC // LANES)),
                  BS((tt, HEADS_W), lambda s, t: (row(s, t), C_BQ // HEADS_W)),
                  BS((tt, HEADS_W), lambda s, t: (row(s, t), C_BF // HEADS_W)),
                  BS((SUBLANES, qkv_w), lambda s, t: (0, 0)), BS((SUBLANES, LANES), lambda s, t: (0, 0)),
                  BS((SUBLANES, HEADS_W), lambda s, t: (0, 0))],
        out_specs=[wide] * 8, out_shape=[SDS((n, HEADS_W), F32)] * 8,
        scratch_shapes=[pltpu.VMEM((tt + SUBLANES, qkv_w), F32)], compiler_params=_params(2),
    )(proj, proj, proj, proj, proj, cw8, aux, lb8)


GDN_SEQS = 4
HGRN_SEQS = 2


def _seq_block(nseq, most):
    return max(s for s in (1, 2, 4) if s <= most and nseq % s == 0)


def _to_chains(x):
    return jnp.concatenate([x[:, :, hd * D_HEAD:(hd + 1) * D_HEAD] for hd in range(N_HEADS)], axis=0)


def _from_chains(ref, rows, val):
    sb = val.shape[0] // N_HEADS
    for hd in range(N_HEADS):
        ref[:, rows, hd * D_HEAD:(hd + 1) * D_HEAD] = val[hd * sb:(hd + 1) * sb].astype(ref.dtype)


def _mixers_fwd(q, k, v, b, g, qb, kb, vb, vb_col, lf, nseq, t_len, tag):
    sb, hs = _seq_block(nseq, GDN_SEQS), _seq_block(nseq, HGRN_SEQS)
    nc = t_len // GDN_CHUNK
    chains = N_HEADS * sb

    def body(q_ref, k_ref, v_ref, b_ref, g_ref, qb_ref, kb_ref, vb_ref, lf_ref, oa_ref, ob_ref, cka_ref, ckb_ref,
             sa_ref, sb_ref):
        @pl.when(pl.program_id(1) == 0)
        def _():
            sa_ref[...] = jnp.zeros_like(sa_ref)
            sb_ref[...] = jnp.zeros_like(sb_ref)

        s = sa_ref[...]
        cka_ref[...] = s
        o, s_new = _gdn_chunk(*[_to_chains(r[...]) for r in (q_ref, k_ref, v_ref, b_ref, g_ref)], s)
        _from_chains(oa_ref, slice(None), o)
        sa_ref[...] = s_new
        for part in range(sb // hs):
            seqs, ch = slice(part * hs, (part + 1) * hs), slice(part * N_HEADS * hs, (part + 1) * N_HEADS * hs)
            s = sb_ref[ch]
            ckb_ref[ch] = s
            o, s_new = _hgrn_block(*[_to_chains(r[seqs]) for r in (qb_ref, kb_ref, vb_ref, lf_ref)], s)
            for hd in range(N_HEADS):
                ob_ref[seqs, :, hd * D_HEAD:(hd + 1) * D_HEAD] = o[hd * hs:(hd + 1) * hs]
            sb_ref[ch] = s_new

    blk = lambda cb: BS((sb, GDN_CHUNK, HEADS_W), lambda p, c: (p, c, cb))
    ck_spec = BS((None, None, chains, D_HEAD, D_HEAD), lambda p, c: (p, c, 0, 0, 0))
    ck_shape = SDS((nseq // sb, nc, chains, D_HEAD, D_HEAD), F32)
    view = lambda a: a.reshape(nseq, t_len, a.shape[1])
    oa, ob, cka, ckb = pl.pallas_call(
        body, name=f"mixers_fwd_{tag}", grid=(nseq // sb, nc),
        in_specs=[blk(0)] * 7 + [blk(vb_col), blk(0)], out_specs=[blk(0), blk(0), ck_spec, ck_spec],
        out_shape=[SDS((nseq, t_len, HEADS_W), F32)] * 2 + [ck_shape] * 2,
        scratch_shapes=[pltpu.VMEM((chains, D_HEAD, D_HEAD), F32)] * 2, compiler_params=_params(2),
    )(*[view(a) for a in (q, k, v, b, g, qb, kb, vb, lf)])
    return oa.reshape(-1, HEADS_W), ob.reshape(-1, HEADS_W), cka, ckb


def _mixers_bwd(q, k, v, b, g, qb, kb, vb, vb_col, lf, cka, ckb, doa, dob, nseq, t_len, tag):
    sb, hs = _seq_block(nseq, GDN_SEQS), _seq_block(nseq, HGRN_SEQS)
    nc = t_len // GDN_CHUNK
    chains = N_HEADS * sb

    def body(q_ref, k_ref, v_ref, b_ref, g_ref, qb_ref, kb_ref, vb_ref, lf_ref, doa_ref, dob_ref, cka_ref, ckb_ref,
             dq_ref, dk_ref, dv_ref, db_ref, dg_ref, dqb_ref, dkb_ref, dvb_ref, dlf_ref, dsa_ref, dsb_ref):
        @pl.when(pl.program_id(1) == 0)
        def _():
            dsa_ref[...] = jnp.zeros_like(dsa_ref)
            dsb_ref[...] = jnp.zeros_like(dsb_ref)

        _, vjp = jax.vjp(_gdn_chunk, *[_to_chains(r[...]) for r in (q_ref, k_ref, v_ref, b_ref, g_ref)], cka_ref[...])
        grads = vjp((_to_chains(doa_ref[...]), dsa_ref[...]))
        for ref, val in zip((dq_ref, dk_ref, dv_ref, db_ref, dg_ref), grads[:5]):
            _from_chains(ref, slice(None), val)
        dsa_ref[...] = grads[5]
        for part in range(sb // hs):
            seqs, ch = slice(part * hs, (part + 1) * hs), slice(part * N_HEADS * hs, (part + 1) * N_HEADS * hs)
            _, vjp = jax.vjp(functools.partial(_hgrn_block, group=SUBLANES),
                             *[_to_chains(r[seqs]) for r in (qb_ref, kb_ref, vb_ref, lf_ref)], ckb_ref[ch])
            grads = vjp((_to_chains(dob_ref[seqs]), dsb_ref[ch]))
            for ref, val in zip((dqb_ref, dkb_ref, dvb_ref, dlf_ref), grads[:4]):
                for hd in range(N_HEADS):
                    ref[seqs, :, hd * D_HEAD:(hd + 1) * D_HEAD] = val[hd * hs:(hd + 1) * hs].astype(ref.dtype)
            dsb_ref[ch] = grads[4]

    blk = lambda cb: BS((sb, GDN_CHUNK, HEADS_W), lambda p, c: (p, nc - 1 - c, cb))
    ck_spec = BS((None, None, chains, D_HEAD, D_HEAD), lambda p, c: (p, nc - 1 - c, 0, 0, 0))
    view = lambda a: a.reshape(nseq, t_len, a.shape[1])
    dts = [F32] * 7 + [BF16, F32]
    res = pl.pallas_call(
        body, name=f"mixers_bwd_{tag}", grid=(nseq // sb, nc),
        in_specs=[blk(0)] * 7 + [blk(vb_col), blk(0), blk(0), blk(0), ck_spec, ck_spec], out_specs=[blk(0)] * 9,
        out_shape=[SDS((nseq, t_len, HEADS_W), dt) for dt in dts],
        scratch_shapes=[pltpu.VMEM((chains, D_HEAD, D_HEAD), F32)] * 2, compiler_params=_params(2),
    )(*[view(a) for a in (q, k, v, b, g, qb, kb, vb, lf, doa, dob)], cka, ckb)
    return [r.reshape(-1, HEADS_W) for r in res]


def _post_values(oa_ref, ob_ref, z_ref, bg_ref, ga_ref, gb_ref, gn_ref, wa_ref, wb_ref, ya_ref, yb_ref):
    for hd in range(N_HEADS):
        sl = slice(hd * D_HEAD, (hd + 1) * D_HEAD)
        ya_ref[:, sl] = _gated_norm(oa_ref[:, sl], z_ref[:, sl], gn_ref[0:1, :]).astype(BF16)
        yb_ref[:, sl] = _gated_norm(ob_ref[:, sl], bg_ref[:, sl], gn_ref[1:2, :]).astype(BF16)
    pa = jnp.dot(ya_ref[...], wa_ref[...], preferred_element_type=F32)
    pb = jnp.dot(yb_ref[...], wb_ref[...], preferred_element_type=F32)
    return pa, pb, _sigmoid(ga_ref[...]), _sigmoid(gb_ref[...])


def _post_specs(tm):
    r2 = lambda i: (i, 0)
    return [BS((tm, HEADS_W), r2), BS((tm, HEADS_W), r2),
            BS((tm, HEADS_W), lambda i: (i, C_Z // HEADS_W)), BS((tm, HEADS_W), lambda i: (i, C_BG // HEADS_W)),
            BS((tm, D_MODEL), lambda i: (i, C_GA // D_MODEL)), BS((tm, D_MODEL), lambda i: (i, C_GB // D_MODEL)),
            BS((tm, D_MODEL), r2), BS((SUBLANES, LANES), lambda i: (0, 0))]


def _post_fwd(oa, ob, proj, h, gn8, wa, wb, wout, tag):
    n = h.shape[0]
    tm = _pick(n, (256, 192, 128, 64))

    def body(oa_ref, ob_ref, z_ref, bg_ref, ga_ref, gb_ref, h_ref, gn_ref, wa_ref, wb_ref, wout_ref, out_ref,
             ya_ref, yb_ref):
        pa, pb, sa, sb = _post_values(oa_ref, ob_ref, z_ref, bg_ref, ga_ref, gb_ref, gn_ref, wa_ref, wb_ref,
                                      ya_ref, yb_ref)
        mixed = (sa * pa + sb * pb).astype(BF16)
        out_ref[...] = h_ref[...] + jnp.dot(mixed, wout_ref[...], preferred_element_type=F32)

    full = lambda i: (0, 0)
    return pl.pallas_call(
        body, name=f"post_fwd_{tag}", grid=(n // tm,),
        in_specs=_post_specs(tm) + [BS((HEADS_W, D_MODEL), full), BS((HEADS_W, D_MODEL), full),
                                    BS((D_MODEL, D_MODEL), full)],
        out_specs=BS((tm, D_MODEL), lambda i: (i, 0)), out_shape=SDS((n, D_MODEL), F32),
        scratch_shapes=[pltpu.VMEM((tm, HEADS_W), BF16), pltpu.VMEM((tm, HEADS_W), BF16)], compiler_params=_params(1),
    )(oa, ob, proj, proj, proj, proj, h, gn8, wa, wb, wout)


def _post_bwd(dh, oa, ob, proj, h, gn8, wa, wb, wout, tag):
    n = h.shape[0]
    tm = _pick(n, (256, 192, 128, 64))

    def body(dh_ref, oa_ref, ob_ref, z_ref, bg_ref, ga_ref, gb_ref, h_ref, gn_ref, wa_ref, wb_ref, wout_ref,
             doa_ref, dob_ref, dz_ref, dbg_ref, dga_ref, dgb_ref, dwa_ref, dwb_ref, dwout_ref, dgn_ref,
             ya_ref, yb_ref):
        @pl.when(pl.program_id(0) == 0)
        def _():
            dwa_ref[...] = jnp.zeros_like(dwa_ref)
            dwb_ref[...] = jnp.zeros_like(dwb_ref)
            dwout_ref[...] = jnp.zeros_like(dwout_ref)
            dgn_ref[...] = jnp.zeros_like(dgn_ref)

        pa, pb, sa, sb = _post_values(oa_ref, ob_ref, z_ref, bg_ref, ga_ref, gb_ref, gn_ref, wa_ref, wb_ref,
                                      ya_ref, yb_ref)
        mixed = (sa * pa + sb * pb).astype(BF16)
        dout = dh_ref[...].astype(BF16)
        dwout_ref[...] += _dg(mixed, dout, ((0,), (0,)))
        dmixed = _dg(dout, wout_ref[...], ((1,), (1,)))
        dga_ref[...] = (dmixed * pa * sa * (1.0 - sa)).astype(BF16)
        dgb_ref[...] = (dmixed * pb * sb * (1.0 - sb)).astype(BF16)
        dpa = (dmixed * sa).astype(BF16)
        dpb = (dmixed * sb).astype(BF16)
        dwa_ref[...] += _dg(ya_ref[...], dpa, ((0,), (0,)))
        dwb_ref[...] += _dg(yb_ref[...], dpb, ((0,), (0,)))
        dya = _dg(dpa, wa_ref[...], ((1,), (1,)))
        dyb = _dg(dpb, wb_ref[...], ((1,), (1,)))
        dgn_a = jnp.zeros((1, D_HEAD), F32)
        dgn_b = jnp.zeros((1, D_HEAD), F32)
        for hd in range(N_HEADS):
            sl = slice(hd * D_HEAD, (hd + 1) * D_HEAD)
            _, vjp = jax.vjp(_gated_norm, oa_ref[:, sl], z_ref[:, sl], gn_ref[0:1, :])
            doa, dz, dgw = vjp(dya[:, sl])
            doa_ref[:, sl], dz_ref[:, sl], dgn_a = doa, dz.astype(BF16), dgn_a + dgw
            _, vjp = jax.vjp(_gated_norm, ob_ref[:, sl], bg_ref[:, sl], gn_ref[1:2, :])
            dob, dbg, dgw = vjp(dyb[:, sl])
            dob_ref[:, sl], dbg_ref[:, sl], dgn_b = dob, dbg.astype(BF16), dgn_b + dgw
        dgn_ref[0:1, :] += dgn_a
        dgn_ref[1:2, :] += dgn_b

    full = lambda i: (0, 0)
    r2 = lambda i: (i, 0)
    return pl.pallas_call(
        body, name=f"post_bwd_{tag}", grid=(n // tm,),
        in_specs=[BS((tm, D_MODEL), r2)] + _post_specs(tm) + [
            BS((HEADS_W, D_MODEL), full), BS((HEADS_W, D_MODEL), full), BS((D_MODEL, D_MODEL), full)],
        out_specs=[BS((tm, HEADS_W), r2)] * 4 + [BS((tm, D_MODEL), r2)] * 2 + [
            BS((HEADS_W, D_MODEL), full), BS((HEADS_W, D_MODEL), full), BS((D_MODEL, D_MODEL), full),
            BS((SUBLANES, LANES), full)],
        out_shape=[SDS((n, HEADS_W), F32), SDS((n, HEADS_W), F32), SDS((n, HEADS_W), BF16), SDS((n, HEADS_W), BF16),
                   SDS((n, D_MODEL), BF16), SDS((n, D_MODEL), BF16), SDS((HEADS_W, D_MODEL), F32),
                   SDS((HEADS_W, D_MODEL), F32), SDS((D_MODEL, D_MODEL), F32), SDS((SUBLANES, LANES), F32)],
        scratch_shapes=[pltpu.VMEM((tm, HEADS_W), BF16), pltpu.VMEM((tm, HEADS_W), BF16)], compiler_params=_params(1),
    )(dh, oa, ob, proj, proj, proj, proj, h, gn8, wa, wb, wout)


def _loss_head(h, fw8, target, nseq, t_len):
    n = h.shape[0]
    nc = t_len // GDN_CHUNK
    sub = 3 if nc % 3 == 0 else 1
    tl, nt = sub * GDN_CHUNK, nc // sub
    inv_d = 1.0 / D_MODEL

    def body(h_ref, fw_ref, *rest):
        tgt_refs, (dh_ref, acc_ref) = rest[:sub], rest[sub:]

        @pl.when((pl.program_id(0) == 0) & (pl.program_id(1) == 0))
        def _():
            acc_ref[...] = jnp.zeros_like(acc_ref)

        frames = ((pl.program_id(1) * tl + _iota2((tl, 1), 0)) >= N_PAD + N_META).astype(F32)
        y, vjp = jax.vjp(_rms, h_ref[...], fw_ref[0:1, :])
        err = (y - jnp.concatenate([r[...] for r in tgt_refs], axis=0)) * frames
        dx, dfw = vjp(err * inv_d)
        dh_ref[...] = dx
        acc_ref[0:1, :] += dfw
        acc_ref[1:2, :] += (0.5 * inv_d) * jnp.sum(err * err, axis=0, keepdims=True)

    tgt_spec = lambda u: BS((None, GDN_CHUNK, D_MODEL), lambda s, t: (s, jnp.maximum(t * sub + u - 1, 0), 0))
    return pl.pallas_call(
        body, name="loss_head", grid=(nseq, nt),
        in_specs=[BS((tl, D_MODEL), lambda s, t: (s * nt + t, 0)), BS((SUBLANES, D_MODEL), lambda s, t: (0, 0))]
        + [tgt_spec(u) for u in range(sub)],
        out_specs=[BS((tl, D_MODEL), lambda s, t: (s * nt + t, 0)), BS((SUBLANES, D_MODEL), lambda s, t: (0, 0))],
        out_shape=[SDS((n, D_MODEL), F32), SDS((SUBLANES, D_MODEL), F32)], compiler_params=_params(2),
    )(h, fw8, *[target] * sub)


def _prep_bwd(proj, dq, dk, dv, db, dg, dqb, dkb, dlf, cw8, aux, lb8, nseq, t_len, tag):
    n = proj.shape[0]
    tt = _pick(t_len, (192, 128, 64))
    nt_ = t_len // tt
    qkv_w = 3 * HEADS_W
    rb = tt // SUBLANES
    ext = tt + SUBLANES

    def body(cur_ref, prev_ref, next_ref, misc_ref, bq_ref, bf_ref, dq_ref, dqn_ref, dk_ref, dkn_ref, dv_ref, dvn_ref,
             db_ref, dg_ref, dqb_ref, dkb_ref, dlf_ref, cw_ref, aux_ref, lb_ref,
             dqkv_ref, dmisc_ref, dbq_ref, dbf_ref, dcw_ref, daux_ref, dlb_ref, dy_ref):
        s, t = pl.program_id(0), pl.program_id(1)

        @pl.when((s == 0) & (t == 0))
        def _():
            dcw_ref[...] = jnp.zeros_like(dcw_ref)
            daux_ref[...] = jnp.zeros_like(daux_ref)
            dlb_ref[...] = jnp.zeros_like(dlb_ref)

        prev = jnp.where(t == 0, 0.0, prev_ref[...])
        x_ext = jnp.concatenate([prev, cur_ref[...], next_ref[...]], axis=0)
        y = _conv_ext(x_ext, cw_ref)
        inside = (t < nt_ - 1) | (_iota2((ext, 1), 0) < tt)
        dy_ref[0:SUBLANES, :] = jnp.zeros((SUBLANES, qkv_w), F32)
        for hd in range(N_HEADS):
            for grp, (g_ref, gn_ref, scale) in enumerate(((dq_ref, dqn_ref, D_HEAD ** -0.5), (dk_ref, dkn_ref, 1.0),
                                                          (dv_ref, dvn_ref, None))):
                lo = grp * HEADS_W + hd * D_HEAD
                sl = slice(hd * D_HEAD, (hd + 1) * D_HEAD)
                cot = jnp.concatenate([g_ref[:, sl], gn_ref[:, sl]], axis=0)
                fn = _silu if scale is None else functools.partial(_l2n_act, scale=scale)
                _, vjp = jax.vjp(fn, y[:, lo:lo + D_HEAD])
                dy_ref[SUBLANES:, lo:lo + D_HEAD] = jnp.where(inside, vjp(cot)[0], 0.0)
        dy_ext = dy_ref[...]
        dx = dy_ext * cw_ref[3:4, :]
        for kk in range(3):
            dx = dx + _shift_up(dy_ext, 3 - kk) * cw_ref[kk:kk + 1, :]
        dqkv_ref[...] = dx[SUBLANES:SUBLANES + tt].astype(BF16)
        dy_cur = dy_ext[SUBLANES:SUBLANES + tt]
        for kk in range(4):
            xs = _shift_down(x_ext, 3 - kk)[SUBLANES:SUBLANES + tt]
            dcw_ref[kk:kk + 1, :] += jnp.sum(xs * dy_cur, axis=0, keepdims=True)

        real = (t * tt + _iota2((tt, 1), 0)) >= N_PAD
        dmisc = jnp.zeros((tt, LANES), F32)
        daux = jnp.zeros((SUBLANES, LANES), F32)
        for hd in range(N_HEADS):
            sl = slice(hd * D_HEAD, (hd + 1) * D_HEAD)
            _, vjp = jax.vjp(lambda m, a: _gdn_gates(m, a, real, hd), misc_ref[...], aux_ref[...])
            dm, da = vjp((db_ref[:, sl], dg_ref[:, sl]))
            dmisc, daux = dmisc + dm, daux + da
        dmisc_ref[...] = dmisc.astype(BF16)
        daux_ref[...] += daux
        _, vjp = jax.vjp(lambda a, b, c: _hgrn_prep(a, b, c, real), bq_ref[...], bf_ref[...], lb_ref[0:1, :])
        dbq, dbf, dlb = vjp((dqb_ref[...], dkb_ref[...], dlf_ref[...]))
        dbq_ref[...], dbf_ref[...] = dbq.astype(BF16), dbf.astype(BF16)
        dlb_ref[0:1, :] += dlb

    row = lambda s, t: s * nt_ + t
    cur = lambda s, t: (row(s, t), 0)
    nxt = lambda s, t: (jnp.minimum((row(s, t) + 1) * rb, n // SUBLANES - 1), 0)
    wide = BS((tt, HEADS_W), cur)
    halo = BS((SUBLANES, HEADS_W), nxt)
    full = lambda s, t: (0, 0)
    return pl.pallas_call(
        body, name=f"prep_bwd_{tag}", grid=(nseq, nt_),
        in_specs=[BS((tt, qkv_w), cur), BS((SUBLANES, qkv_w), lambda s, t: (jnp.maximum(row(s, t) * rb - 1, 0), 0)),
                  BS((SUBLANES, qkv_w), nxt), BS((tt, LANES), lambda s, t: (row(s, t), C_MISC // LANES)),
                  BS((tt, HEADS_W), lambda s, t: (row(s, t), C_BQ // HEADS_W)),
                  BS((tt, HEADS_W), lambda s, t: (row(s, t), C_BF // HEADS_W)),
                  wide, halo, wide, halo, wide, halo, wide, wide, wide, wide, wide,
                  BS((SUBLANES, qkv_w), full), BS((SUBLANES, LANES), full), BS((SUBLANES, HEADS_W), full)],
        out_specs=[BS((tt, qkv_w), cur), BS((tt, LANES), cur), wide, wide,
                   BS((SUBLANES, qkv_w), full), BS((SUBLANES, LANES), full), BS((SUBLANES, HEADS_W), full)],
        out_shape=[SDS((n, qkv_w), BF16), SDS((n, LANES), BF16), SDS((n, HEADS_W), BF16), SDS((n, HEADS_W), BF16),
                   SDS((SUBLANES, qkv_w), F32), SDS((SUBLANES, LANES), F32), SDS((SUBLANES, HEADS_W), F32)],
        scratch_shapes=[pltpu.VMEM((tt + 2 * SUBLANES, qkv_w), F32)], compiler_params=_params(2),
    )(proj, proj, proj, proj, proj, proj, dq, dq, dk, dk, dv, dv, db, dg, dqb, dkb, dlf, cw8, aux, lb8)


def _proj_bwd_x(pieces, wp, h, nw8, dh_res, tag):
    n = h.shape[0]
    tm = _pick(n, (256, 192, 128, 64))
    widths = [p.shape[1] for p in pieces]
    assert sum(widths) == PROJ_W

    def body(*refs):
        p_refs = refs[:len(pieces)]
        w_ref, h_ref, nw_ref, dres_ref, dh_ref, dnw_ref = refs[len(pieces):]

        @pl.when(pl.program_id(0) == 0)
        def _():
            dnw_ref[...] = jnp.zeros_like(dnw_ref)

        dxn, off = None, 0
        for p_ref, w in zip(p_refs, widths):
            part = _dg(p_ref[...], w_ref[:, off:off + w], ((1,), (1,)))
            dxn = part if dxn is None else dxn + part
            off += w
        _, vjp = jax.vjp(_rms, h_ref[...], nw_ref[0:1, :])
        dx, dnw = vjp(dxn)
        dh_ref[...] = dres_ref[...] + dx
        dnw_ref[0:1, :] += dnw

    r2 = lambda i: (i, 0)
    full = lambda i: (0, 0)
    return pl.pallas_call(
        body, name=f"proj_bwd_x_{tag}", grid=(n // tm,),
        in_specs=[BS((tm, w), r2) for w in widths] + [BS((D_MODEL, PROJ_W), full), BS((tm, D_MODEL), r2),
                                                      BS((SUBLANES, D_MODEL), full), BS((tm, D_MODEL), r2)],
        out_specs=[BS((tm, D_MODEL), r2), BS((SUBLANES, D_MODEL), full)],
        out_shape=[SDS((n, D_MODEL), F32), SDS((SUBLANES, D_MODEL), F32)], compiler_params=_params(1),
    )(*pieces, wp, h, nw8, dh_res)


def _proj_bwd_w(xn, pieces, tag):
    n = xn.shape[0]
    tm = _pick(n, (384, 256, 192, 128, 64))
    widths = [p.shape[1] for p in pieces]
    assert sum(widths) == PROJ_W

    def body(*refs):
        x_ref, p_refs = refs[0], refs[1:1 + len(pieces)]
        o_ref, acc_ref = refs[1 + len(pieces):]

        @pl.when(pl.program_id(0) == 0)
        def _():
            acc_ref[...] = jnp.zeros_like(acc_ref)

        off = 0
        for p_ref, w in zip(p_refs, widths):
            acc_ref[:, off:off + w] += _dg(x_ref[...], p_ref[...], ((0,), (0,)))
            off += w

        @pl.when(pl.program_id(0) == pl.num_programs(0) - 1)
        def _():
            pltpu.sync_copy(acc_ref, o_ref)

    r2 = lambda i: (i, 0)
    return pl.pallas_call(
        body, name=f"proj_bwd_w_{tag}", grid=(n // tm,),
        in_specs=[BS((tm, D_MODEL), r2)] + [BS((tm, w), r2) for w in widths], out_specs=BS(memory_space=pl.ANY),
        out_shape=SDS((D_MODEL, PROJ_W), F32), scratch_shapes=[pltpu.VMEM((D_MODEL, PROJ_W), F32)],
        compiler_params=_params(1),
    )(xn, *pieces)


def _adamw(w, g, m, v, name):
    lead, rows, cols = w.shape
    tr = _pick(rows, (256, 128, 64, 32, 16, 8, 4, 2, 1)) if rows > 256 else rows

    def body(w_ref, g_ref, m_ref, v_ref, d_ref, nm_ref, nv_ref):
        gr = g_ref[...]
        m_new = ADAM_B1 * m_ref[...] + (1.0 - ADAM_B1) * gr
        v_new = ADAM_B2 * v_ref[...] + (1.0 - ADAM_B2) * jnp.square(gr)
        m_hat = m_new / (1.0 - ADAM_B1 ** ADAM_STEP)
        v_hat = v_new / (1.0 - ADAM_B2 ** ADAM_STEP)
        d_ref[...] = -ADAM_LR * (m_hat / (jnp.sqrt(v_hat) + ADAM_EPS) + ADAM_WD * w_ref[...])
        nm_ref[...] = m_new
        nv_ref[...] = v_new

    blk = BS((None, tr, cols), lambda a, i: (a, i, 0))
    return pl.pallas_call(
        body, name=name, grid=(lead, rows // tr), in_specs=[blk] * 4, out_specs=[blk] * 3,
        out_shape=[SDS((lead, rows, cols), F32)] * 3, compiler_params=_params(2),
    )(w, g, m, v)


def _row8(v, width):
    v = jnp.atleast_2d(v).astype(F32)
    return jnp.pad(v, ((0, SUBLANES - v.shape[0]), (0, width - v.shape[1])))


REF_MISC = 1536
N_MISC = 2 * N_HEADS
LAYOUT_RUNS = ((0, REF_MISC, 0), (REF_MISC + N_MISC, REF_W, REF_MISC), (REF_MISC, REF_MISC + N_MISC, C_MISC))


def _to_layout(w_full):
    runs = [w_full[:, lo:hi] for lo, hi, _ in sorted(LAYOUT_RUNS, key=lambda run: run[2])]
    return jnp.concatenate(runs + [jnp.zeros((w_full.shape[0], PROJ_W - REF_W), w_full.dtype)], axis=1)


def _from_layout(dw, n_slabs):
    width = REF_W // n_slabs
    slabs = []
    for j in range(n_slabs):
        pieces = []
        for lo, hi, at in sorted(LAYOUT_RUNS):
            a, b = max(lo, j * width), min(hi, (j + 1) * width)
            if a < b:
                pieces.append(dw[:, at + a - lo:at + b - lo])
        slabs.append(jnp.concatenate(pieces, axis=1))
    return slabs


def _lower_bounds(lb):
    sm = jax.nn.softmax(lb.astype(F32), axis=0)
    return jnp.cumsum(sm, axis=0) - sm[0]


def kernel(x, meta_tokens, norm_w, w_in, conv_w, a_log, dt_bias, gnorm_a, gnorm_b, hgrn_lower_bounds, w_branch_a, w_branch_b, w_out, final_norm_w, loss_target, m_meta_tokens, m_norm_w, m_w_in, m_conv_w, m_a_log, m_dt_bias, m_gnorm_a, m_gnorm_b, m_hgrn_lower_bounds, m_w_branch_a, m_w_branch_b, m_w_out, m_final_norm_w, v_meta_tokens, v_norm_w, v_w_in, v_conv_w, v_a_log, v_dt_bias, v_gnorm_a, v_gnorm_b, v_hgrn_lower_bounds, v_w_branch_a, v_w_branch_b, v_w_out, v_final_norm_w):
    nseq, seq, _ = x.shape
    depth = norm_w.shape[0]
    t_len = N_PAD + N_META + seq
    n = nseq * t_len
    win_c, conv_c = w_in.shape[2], conv_w.shape[2]
    my = 4 * lax.axis_index("x") + 2 * lax.axis_index("y") + lax.axis_index("c")

    assert depth >= 2
    by_cols = lambda g: g.transpose(1, 2, 0, 3).reshape(g.shape[1], g.shape[2], N_DEV * g.shape[3])
    first = _all_gather_hbm([w_in[:1].astype(BF16), conv_w, meta_tokens], "gather_first")
    later_flight, later_token = _send_all_start(
        [w_in[1:].astype(BF16), w_branch_a.astype(BF16), w_branch_b.astype(BF16), w_out.astype(BF16)], False,
        "gather_later_start")
    w_in_full = [by_cols(first[0])]
    conv_full = by_cols(first[1])
    meta_full = first[2].transpose(1, 0, 2).reshape(N_META, D_MODEL)

    lb_all, lb_vjp = jax.vjp(_lower_bounds, hgrn_lower_bounds)

    h = jnp.concatenate([jnp.zeros((nseq, N_PAD, D_MODEL), F32),
                         jnp.broadcast_to(meta_full[None], (nseq, N_META, D_MODEL)), x], axis=1).reshape(n, D_MODEL)
    saved = []
    for l in range(depth):
        wp = _to_layout(w_in_full[0][0] if l == 0 else w_in_full[1][l - 1])
        nw8 = _row8(norm_w[l], D_MODEL)
        if l == 0:
            nw8 = nw8 + later_token[0:1, 0:1]
        cw8 = _row8(conv_full[l], 3 * HEADS_W)
        aux = _row8(jnp.stack([a_log[l], dt_bias[l]]), LANES)
        lb8 = _row8(lb_all[l], HEADS_W)
        gn8 = _row8(jnp.stack([gnorm_a[l], gnorm_b[l]]), LANES)
        proj, xn = _proj_fwd(h, nw8, wp, l)
        q, k, v, b, g, qb, kb, lf = _prep_fwd(proj, cw8, aux, lb8, nseq, t_len, l)
        oa, ob, sck_a, sck_b = _mixers_fwd(q, k, v, b, g, qb, kb, proj, C_BI // HEADS_W, lf, nseq, t_len, l)
        if l == 0:
            sent, landed = _send_all_wait(later_flight, ob, "gather_later_wait")
            landed = [lax.dynamic_update_slice(ld, own[None], (my,) + (0,) * own.ndim) for ld, own in zip(landed, sent)]
            w_in_full.append(by_cols(landed[0]))
            wa_full, wb_full = by_cols(landed[1]), by_cols(landed[2])
            wout_full = landed[3].transpose(1, 0, 2, 3).reshape(depth, D_MODEL, D_MODEL)
        wa_l, wb_l, wout_l = wa_full[l], wb_full[l], wout_full[l]
        h_next = _post_fwd(oa, ob, proj, h, gn8, wa_l, wb_l, wout_l, l)
        saved.append(dict(h=h, wp=wp, nw8=nw8, cw8=cw8, aux=aux, lb8=lb8, gn8=gn8, proj=proj, xn=xn, q=q, k=k, v=v, b=b,
                          wa=wa_l, wb=wb_l, wout=wout_l,
                          g=g, qb=qb, kb=kb, lf=lf, oa=oa, ob=ob, sck_a=sck_a, sck_b=sck_b))
        h = h_next

    dh, acc = _loss_head(h, _row8(final_norm_w, D_MODEL), loss_target, nseq, t_len)

    g_win, g_wa, g_wb, g_wout, g_conv, small = [], [], [], [], [], []

    def mixer_slabs(dwa_s, dwb_s, dwout_s):
        nl = len(dwa_s)
        rows = lambda a: jnp.stack(a).reshape(nl * HEADS_W, N_DEV, LANES).transpose(1, 0, 2)
        wout = jnp.stack(dwout_s).reshape(nl, N_DEV, LANES, D_MODEL).transpose(1, 0, 2, 3)
        return [jnp.concatenate([rows(dwa_s), rows(dwb_s)], axis=1).astype(BF16),
                wout.reshape(N_DEV, nl * LANES, D_MODEL).astype(BF16)]

    def win_slabs(per_layer, dtype):
        return jnp.stack([jnp.concatenate([sl[j] for sl in per_layer], axis=0) for j in range(N_DEV)]).astype(dtype)

    for l in reversed(range(depth)):
        s = saved[l]
        gn8, aux = s["gn8"], s["aux"]
        if l == 0:
            later_flight, later_token = _send_all_start(
                [win_slabs(g_win[::-1], BF16)] + mixer_slabs(g_wa[::-1], g_wb[::-1], g_wout[::-1]), True,
                "scatter_later_start")
            gn8 = gn8 + later_token[0:1, 0:1]
        doa, dob, dz, dbg, dga, dgb, dwa, dwb, dwout, dgn = _post_bwd(
            dh, s["oa"], s["ob"], s["proj"], s["h"], gn8, s["wa"], s["wb"], s["wout"], l)
        dq, dk, dv, db, dg, dqb, dkb, dbi, dlf = _mixers_bwd(
            s["q"], s["k"], s["v"], s["b"], s["g"], s["qb"], s["kb"], s["proj"], C_BI // HEADS_W, s["lf"], s["sck_a"],
            s["sck_b"], doa, dob, nseq, t_len, l)
        if l == 0:
            mixer_flight, mixer_token = _send_all_start(mixer_slabs([dwa], [dwb], [dwout]), True, "scatter_first_start")
            aux = aux + mixer_token[0:1, 0:1]
        dqkv, dmisc, dbq, dbf, dcw, daux, dlb = _prep_bwd(s["proj"], dq, dk, dv, db, dg, dqb, dkb, dlf, s["cw8"], aux,
                                                          s["lb8"], nseq, t_len, l)
        pieces = [dqkv, dz, dbq, dbf, dbi, dbg, dga, dgb, dmisc]
        g_win.append(_from_layout(_proj_bwd_w(s["xn"], pieces, l), N_DEV))
        g_conv.append(dcw[:4])
        nw8 = s["nw8"]
        if l == 0:
            dconv = jnp.stack(g_conv[::-1])
            conv_slabs = dconv.reshape(depth * dconv.shape[1], N_DEV, conv_c).transpose(1, 0, 2)
            win_flight, win_token = _send_all_start([win_slabs(g_win[-1:], BF16), conv_slabs], True, "scatter_win_start")
            nw8 = nw8 + win_token[0:1, 0:1]
        dh, dnw = _proj_bwd_x(pieces, s["wp"], s["h"], nw8, dh, l)
        g_wa.append(dwa)
        g_wb.append(dwb)
        g_wout.append(dwout)
        small.append((dnw[0], dgn[0], dgn[1], daux[0, :N_HEADS], daux[1, :N_HEADS], dlb[0]))
    small.reverse()
    dh = dh.reshape(nseq, t_len, D_MODEL)
    grad_x = dh[:, N_PAD + N_META:]

    packed = jnp.concatenate([small[0][1], small[1][1], small[0][2], small[1][2], small[0][3], small[1][3],
                              small[0][4], small[1][4]])
    tile = jnp.concatenate([
        jnp.sum(dh[:, N_PAD:N_PAD + N_META], axis=0), _row8(jnp.stack([small[0][0], small[1][0], acc[0]]), D_MODEL),
        _row8(jnp.stack([small[0][5], small[1][5]]), D_MODEL), _row8(packed, D_MODEL), _row8(acc[1], D_MODEL)], axis=0)
    tile = _all_reduce_small(tile, "reduce_small")
    loss = jnp.sum(tile[40])
    g_meta = lax.dynamic_slice_in_dim(tile[0:N_META], my * LANES, LANES, axis=1)
    g_norm, g_final = tile[16:18], tile[18]
    (g_lb,) = lb_vjp(tile[24:26, :HEADS_W])
    r21 = tile[32]
    g_gna, g_gnb = r21[0:256].reshape(2, LANES), r21[256:512].reshape(2, LANES)
    g_alog, g_dtb = r21[512:520].reshape(2, N_HEADS), r21[520:528].reshape(2, N_HEADS)

    def landed_sums(flight, tag):
        sent, landed = _send_all_wait(flight, dh, f"{tag}_wait")
        landed = [lax.dynamic_update_slice(ld, lax.dynamic_index_in_dim(src, my, 0, keepdims=True), (my, 0, 0))
                  for ld, src in zip(landed, sent)]
        return [_sum_slabs(ld, f"{tag}_sum{i}") for i, ld in enumerate(landed)]

    l_win, l_ab, l_wout = landed_sums(later_flight, "scatter_later")
    r_ab, r_wout = landed_sums(mixer_flight, "scatter_first")
    r_win, r_conv = landed_sums(win_flight, "scatter_win")
    both = lambda a, b, shape: jnp.concatenate([a.reshape(1, *shape[1:]), b.reshape(depth - 1, *shape[1:])])
    half, half_l = HEADS_W, (depth - 1) * HEADS_W
    mine = [both(r_win, l_win, w_in.shape), both(r_ab[:half], l_ab[:half_l], w_branch_a.shape),
            both(r_ab[half:], l_ab[half_l:], w_branch_b.shape), both(r_wout, l_wout, w_out.shape), r_conv]
    gseg = lambda i, shape: mine[i].reshape(shape)
    grads = {
        "meta_tokens": g_meta, "norm_w": g_norm, "w_in": gseg(0, w_in.shape), "conv_w": gseg(4, conv_w.shape),
        "a_log": g_alog, "dt_bias": g_dtb, "gnorm_a": g_gna, "gnorm_b": g_gnb, "hgrn_lower_bounds": g_lb,
        "w_branch_a": gseg(1, w_branch_a.shape), "w_branch_b": gseg(2, w_branch_b.shape), "w_out": gseg(3, w_out.shape),
        "final_norm_w": g_final}
    weights = {
        "meta_tokens": (meta_tokens, m_meta_tokens, v_meta_tokens), "norm_w": (norm_w, m_norm_w, v_norm_w),
        "w_in": (w_in, m_w_in, v_w_in), "conv_w": (conv_w, m_conv_w, v_conv_w), "a_log": (a_log, m_a_log, v_a_log),
        "dt_bias": (dt_bias, m_dt_bias, v_dt_bias), "gnorm_a": (gnorm_a, m_gnorm_a, v_gnorm_a),
        "gnorm_b": (gnorm_b, m_gnorm_b, v_gnorm_b),
        "hgrn_lower_bounds": (hgrn_lower_bounds, m_hgrn_lower_bounds, v_hgrn_lower_bounds),
        "w_branch_a": (w_branch_a, m_w_branch_a, v_w_branch_a), "w_branch_b": (w_branch_b, m_w_branch_b, v_w_branch_b),
        "w_out": (w_out, m_w_out, v_w_out), "final_norm_w": (final_norm_w, m_final_norm_w, v_final_norm_w)}
    names = list(weights)
    deltas, new_m, new_v = [], [], []
    for nm in names:
        w, m, v = weights[nm]
        view = (1,) * (3 - w.ndim) + w.shape
        d, m2, v2 = _adamw(w.reshape(view), grads[nm].reshape(view), m.reshape(view), v.reshape(view), f"adamw_{nm}")
        deltas.append(d.reshape(w.shape))
        new_m.append(m2.reshape(w.shape))
        new_v.append(v2.reshape(w.shape))
    return (loss, grad_x, *[grads[nm].reshape(weights[nm][0].shape) for nm in names], *deltas, *new_m, *new_v)
```

```python
import functools

import jax
import jax.numpy as jnp
from jax import lax
from jax.experimental import pallas as pl
from jax.experimental.pallas import tpu as pltpu

F32 = jnp.float32
BF16 = jnp.bfloat16

D_MODEL = 1024
N_HEADS = 4
D_HEAD = 128
HEADS_W = N_HEADS * D_HEAD
N_META = 16
N_PAD = 48
GDN_CHUNK = 64
HGRN_CHUNK = 16
EPS = 1e-6
N_DEV = 8
LANES = 128
SUBLANES = 8
VMEM_LIMIT = 56 * 1024 * 1024

C_QKV, C_Z, C_BQ, C_BF, C_BI, C_BG, C_GA, C_GB, C_MISC = 0, 1536, 2048, 2560, 3072, 3584, 4096, 5120, 6144
PROJ_W = 6272
REF_W = 6152

ADAM_LR, ADAM_B1, ADAM_B2, ADAM_EPS, ADAM_WD, ADAM_STEP = 0.001, 0.9, 0.999, 1e-08, 0.01, 10

MESH = pl.DeviceIdType.MESH
SDS = jax.ShapeDtypeStruct
BS = pl.BlockSpec


def _params(n_axes):
    return pltpu.CompilerParams(dimension_semantics=("arbitrary",) * n_axes, vmem_limit_bytes=VMEM_LIMIT)


def _pick(n, cands):
    for c in cands:
        if n % c == 0:
            return c
    raise ValueError(f"no tile for {n} among {cands}")


def _iota2(shape, dim):
    return lax.broadcasted_iota(jnp.int32, shape, dim)


def _dg(a, b, dims):
    return lax.dot_general(a.astype(BF16), b.astype(BF16), (dims, ((), ())), preferred_element_type=F32)


def _bdg(a, b, ca, cb):
    return lax.dot_general(a.astype(BF16), b.astype(BF16), (((ca,), (cb,)), ((0,), (0,))), preferred_element_type=F32)


@jax.custom_vjp
def _bnn(a, b):
    return _bdg(a, b, 2, 1)


@jax.custom_vjp
def _bnt(a, b):
    return _bdg(a, b, 2, 2)


@jax.custom_vjp
def _btn(a, b):
    return _bdg(a, b, 1, 1)


_bnn.defvjp(lambda a, b: (_bnn(a, b), (a, b)), lambda r, g: (_bnt(g, r[1]), _btn(r[0], g)))
_bnt.defvjp(lambda a, b: (_bnt(a, b), (a, b)), lambda r, g: (_bnn(g, r[1]), _btn(g, r[0])))
_btn.defvjp(lambda a, b: (_btn(a, b), (a, b)), lambda r, g: (_bnt(r[1], g), _bnn(r[0], g)))


def _split2(x):
    hi = x.astype(BF16).astype(F32)
    return hi, x - hi


def _tri(bsz, n):
    return jnp.broadcast_to((_iota2((n, n), 0) >= _iota2((n, n), 1)).astype(F32), (bsz, n, n))


@jax.custom_vjp
def _cumsum_rows(x):
    tri = _tri(x.shape[0], x.shape[1])
    hi, lo = _split2(x)
    return _bdg(tri, hi, 2, 1) + _bdg(tri, lo, 2, 1)


def _cumsum_rows_bwd(_, g):
    tri = _tri(g.shape[0], g.shape[1])
    hi, lo = _split2(g)
    return (_bdg(tri, hi, 1, 1) + _bdg(tri, lo, 1, 1),)


_cumsum_rows.defvjp(lambda x: (_cumsum_rows(x), None), _cumsum_rows_bwd)


def _sigmoid(x):
    return jax.nn.sigmoid(x)


def _silu(x):
    return x * _sigmoid(x)


def _softplus(x):
    return jnp.maximum(x, 0.0) + jnp.log1p(jnp.exp(-jnp.abs(x)))


def _rms(x, w):
    return x * lax.rsqrt(jnp.mean(x * x, axis=-1, keepdims=True) + EPS) * w


@jax.custom_vjp
def _inv_unit_lower(lm):
    n = lm.shape[1]
    a = (_iota2((n, n), 0) == _iota2((n, n), 1)).astype(F32)[None] - lm
    steps = max(1, (n - 1).bit_length()) - 1
    p = _bnn(lm, lm)
    for i in range(steps):
        if i == steps - 1:
            a = a + _bnn(a, p)
        else:
            both = _bnn(jnp.concatenate([a, p], axis=1), p)
            a, p = a + both[:, :n], both[:, n:]
    return a


_inv_unit_lower.defvjp(lambda lm: (lambda a: (a, a))(_inv_unit_lower(lm)),
                       lambda a, g: (-_bnt(_btn(a, g), a),))


def _gdn_chunk(q, k, v, b_b, g_b, s):
    n, dv = q.shape[1], v.shape[2]
    r, c = _iota2((n, n), 0), _iota2((n, n), 1)
    causal, strict, eye = (r >= c)[None], (r > c)[None], (r == c)[None]
    g_cum = _cumsum_rows(g_b)
    g_i = g_cum[:, :, :n]
    g_j = jnp.sum(jnp.where(eye, g_i, 0.0), axis=1, keepdims=True)
    decay = jnp.where(causal, jnp.exp(jnp.where(causal, g_i - g_j, 0.0)), 0.0)
    e_g = jnp.exp(g_cum)
    kb = k * b_b
    kk = _bnt(jnp.concatenate([kb, q], axis=1), k)
    a_inv = _inv_unit_lower(jnp.where(strict, kk[:, :n] * decay, 0.0))
    uw = _bnn(a_inv, jnp.concatenate([v * b_b, kb * e_g], axis=2))
    ws = _bnn(jnp.concatenate([uw[:, :, dv:], q * e_g], axis=1), s)
    v_new = uw[:, :, :dv] - ws[:, :n]
    o = ws[:, n:] + _bnn(kk[:, n:] * decay, v_new)
    g_last = g_cum[:, n - 1:n, :]
    s_new = s * jnp.exp(g_last) +_btn(k * jnp.exp(g_last - g_cum), v_new)
    return o, s_new


@functools.partial(jax.custom_vjp, nondiff_argnums=(1, 2))
def _row(x, j, n):
    return x[:, j:j + 1, :]


def _row_bwd(j, n, _, g):
    return (jnp.where(_iota2((1, n, 1), 1) == j, g, 0.0),)


_row.defvjp(lambda x, j, n: (_row(x, j, n), None), _row_bwd)


def _hgrn_pairs(q, k, v, b_cum):
    n = q.shape[1]
    half = n // 2 if n > SUBLANES else n
    parts = []
    for lo in range(0, n, half):
        qs, bs = q[:, lo:], b_cum[:, lo:]
        rows = _iota2((1, n - lo, 1), 1) + lo
        acc = jnp.zeros_like(qs)
        for j in range(lo, lo + half):
            p = jnp.exp(jnp.where(rows >= j, bs - _row(b_cum, j, n), -1e30))
            acc = acc + jnp.sum(qs * _row(k, j, n) * p, axis=2, keepdims=True) * _row(v, j, n)
        parts.append(acc)
    if len(parts) == 1:
        return parts[0]
    return parts[0] + jnp.concatenate([jnp.zeros_like(parts[1]), parts[1]], axis=1)


def _hgrn_block(q, k, v, lf, st, group=HGRN_CHUNK):
    n, rows = group, q.shape[1]
    b_cum = _cumsum_rows(lf)
    outs = []
    for c in range(rows // n):
        rs = slice(c * n, (c + 1) * n)
        o = _hgrn_pairs(q[:, rs], k[:, rs], v[:, rs], b_cum[:, rs])
        if c:
            b_c = _row(b_cum, c * n - 1, rows)
            scores = _bnt(q[:, rs] * jnp.exp(b_cum[:, rs] - b_c), k[:, :c * n] * jnp.exp(b_c - b_cum[:, :c * n]))
            o = o + _bnn(scores, v[:, :c * n])
        outs.append(o)
    b_last = _row(b_cum, rows - 1, rows)
    o = _bnt(q * jnp.exp(b_cum), st) + jnp.concatenate(outs, axis=1)
    return o, st * jnp.exp(b_last) + _btn(v, k * jnp.exp(b_last - b_cum))


def _l2n_act(y, scale):
    a = _silu(y)
    return a * lax.rsqrt(jnp.sum(a * a, axis=-1, keepdims=True) + EPS) * scale


def _col(x, lane):
    return jnp.sum(jnp.where(_iota2(x.shape, 1) == lane, x, 0.0), axis=1, keepdims=True)


def _elem(x, row, lane):
    m = (_iota2(x.shape, 0) == row) & (_iota2(x.shape, 1) == lane)
    return jnp.sum(jnp.sum(jnp.where(m, x, 0.0), axis=1, keepdims=True), axis=0, keepdims=True)


def _gdn_gates(misc, aux, real, head):
    beta = _sigmoid(_col(misc, head))
    g = -jnp.exp(_elem(aux, 0, head)) * _softplus(_col(misc, N_HEADS + head) + _elem(aux, 1, head))
    g = jnp.where(real, g, 0.0)
    shape = (misc.shape[0], D_HEAD)
    return jnp.broadcast_to(beta, shape), jnp.broadcast_to(g, shape)


def _hgrn_prep(bq, bf, lb, real):
    qb = _silu(bq) * (D_HEAD ** -0.5)
    log_sig = jnp.minimum(bf, 0.0) - jnp.log1p(jnp.exp(-jnp.abs(bf)))
    pos = lb > 0.0
    lbs = jnp.where(pos, lb, 0.5)
    a = jnp.log(lbs)
    b = jnp.log1p(-lbs) + log_sig
    lae = jnp.maximum(a, b) + jnp.log1p(jnp.exp(-jnp.abs(a - b)))
    lf = jnp.where(pos, lae, log_sig)
    kb = jnp.where(pos, 1.0 - lbs, 1.0) * _sigmoid(-bf)
    return qb, jnp.where(real, kb, 0.0), jnp.where(real, lf, 0.0)


def _gated_norm(o, z, gw):
    return o * lax.rsqrt(jnp.mean(o * o, axis=-1, keepdims=True) + EPS) * gw * _silu(z)


def _shift_down(x, j):
    return x if j == 0 else pltpu.roll(x, j, 0)


def _shift_up(x, j):
    return x if j == 0 else pltpu.roll(x, x.shape[0] - j, 0)


def _all_gather_hbm(blocks, name):
    na = len(blocks)

    def body(*refs):
        x_refs, out_refs = refs[:na], refs[na:2 * na]
        send_sems, recv_sems, local_sems = refs[2 * na:]
        mx, my, mc = lax.axis_index("x"), lax.axis_index("y"), lax.axis_index("c")
        me, sibling = (mx, my, mc), (mx, my, 1 - mc)
        chips = [(1 - mx, my), (mx, 1 - my), (1 - mx, 1 - my)]

        def slab(a, px, py, pc):
            return out_refs[a].at[4 * px + 2 * py + pc]

        def copy(a, k, blk, to, own=False):
            return pltpu.make_async_remote_copy(
                src_ref=x_refs[a] if own else slab(a, *blk), dst_ref=slab(a, *blk),
                send_sem=send_sems.at[7 * a + k], recv_sem=recv_sems.at[7 * a + k], device_id=to, device_id_type=MESH)

        mine = [pltpu.make_async_copy(x_refs[a], slab(a, *me), local_sems.at[a]) for a in range(na)]
        for cp in mine:
            cp.start()
        first = [copy(a, 0, me, sibling, own=True) for a in range(na)]
        first += [copy(a, 1 + j, me, (*chip, mc), own=True) for j, chip in enumerate(chips) for a in range(na)]
        for cp in first:
            cp.start()
        passed = []
        for j, chip in enumerate(chips):
            for a in range(na):
                copy(a, 1 + j, (*chip, mc), me).wait_recv()
                passed.append(copy(a, 4 + j, (*chip, mc), sibling))
                passed[-1].start()
        for a in range(na):
            copy(a, 0, sibling, me).wait_recv()
            for j, chip in enumerate(chips):
                copy(a, 4 + j, (*chip, 1 - mc), me).wait_recv()
        for cp in first + passed:
            cp.wait_send()
        for cp in mine:
            cp.wait()

    hbm = BS(memory_space=pl.ANY)
    return pl.pallas_call(
        body, name=name, out_shape=[SDS((N_DEV, *b.shape), b.dtype) for b in blocks],
        in_specs=[hbm] * na, out_specs=[hbm] * na,
        scratch_shapes=[pltpu.SemaphoreType.DMA((7 * na,)), pltpu.SemaphoreType.DMA((7 * na,)),
                        pltpu.SemaphoreType.DMA((na,))],
    )(*blocks)


def _all_reduce_small(block, name):
    r, c = block.shape

    def body(x_ref, out_ref, buf, send_sems, recv_sems):
        mx, my, mc = lax.axis_index("x"), lax.axis_index("y"), lax.axis_index("c")
        me, sibling = (mx, my, mc), (mx, my, 1 - mc)
        chips = [(1 - mx, my), (mx, 1 - my), (1 - mx, 1 - my)]

        def slab(px, py, pc):
            return buf.at[4 * px + 2 * py + pc]

        def copy(k, blk, to, src=None):
            return pltpu.make_async_remote_copy(
                src_ref=slab(*blk) if src is None else src, dst_ref=slab(*blk),
                send_sem=send_sems.at[k], recv_sem=recv_sems.at[k], device_id=to, device_id_type=MESH)

        first = [copy(0, me, sibling, src=x_ref)]
        first += [copy(1 + j, me, (*chip, mc), src=x_ref) for j, chip in enumerate(chips)]
        for cp in first:
            cp.start()
        passed = [copy(4 + j, (*chip, mc), sibling) for j, chip in enumerate(chips)]
        for j, chip in enumerate(chips):
            copy(1 + j, (*chip, mc), me).wait_recv()
            passed[j].start()
        copy(0, sibling, me).wait_recv()
        for j, chip in enumerate(chips):
            copy(4 + j, (*chip, 1 - mc), me).wait_recv()
        for cp in first + passed:
            cp.wait_send()
        buf[4 * mx + 2 * my + mc] = x_ref[...]
        acc = buf[0]
        for d in range(1, N_DEV):
            acc = acc + buf[d]
        out_ref[...] = acc

    return pl.pallas_call(
        body, name=name, out_shape=SDS((r, c), F32),
        in_specs=[BS(memory_space=pltpu.VMEM)], out_specs=BS(memory_space=pltpu.VMEM),
        scratch_shapes=[pltpu.VMEM((N_DEV, r, c), F32), pltpu.SemaphoreType.DMA((7,)), pltpu.SemaphoreType.DMA((7,))],
    )(block)


HBM_SPEC = BS(memory_space=pltpu.HBM)
SEM_SPEC = BS(memory_space=pltpu.SEMAPHORE)
SIDE_EFFECT = pltpu.SideEffectType.DATAFLOW_SIDE_EFFECTING


def _peer(rel):
    flip = lambda v, bit: 1 - v if bit else v
    return (flip(lax.axis_index("x"), rel >> 2 & 1), flip(lax.axis_index("y"), rel >> 1 & 1),
            flip(lax.axis_index("c"), rel & 1))


def _send_all_start(blocks, scatter, name):
    na = len(blocks)
    shapes = [b.shape[1:] if scatter else b.shape for b in blocks]

    def body(*refs):
        srcs, lands = refs[:na], refs[na:2 * na]
        send_sems, recv_sems, token = refs[2 * na], refs[2 * na + 1], refs[-1]
        me = 4 * lax.axis_index("x") + 2 * lax.axis_index("y") + lax.axis_index("c")
        for a in range(na):
            for rel in range(1, N_DEV):
                px, py, pc = _peer(rel)
                pltpu.make_async_remote_copy(
                    src_ref=srcs[a].at[4 * px + 2 * py + pc] if scatter else srcs[a], dst_ref=lands[a].at[me],
                    send_sem=send_sems.at[7 * a + rel - 1], recv_sem=recv_sems.at[7 * a + rel - 1],
                    device_id=(px, py, pc), device_id_type=MESH).start()
        token[...] = jnp.zeros_like(token)

    lands = [lax.empty((N_DEV, *s), b.dtype) for s, b in zip(shapes, blocks)]
    res = pl.pallas_call(
        body, name=name,
        out_shape=([pltpu.SemaphoreType.DMA((7 * na,)), pltpu.SemaphoreType.DMA((7 * na,))]
                   + [pltpu.HBM(b.shape, b.dtype) for b in blocks] + [pltpu.HBM(ld.shape, ld.dtype) for ld in lands]
                   + [SDS((SUBLANES, LANES), F32)]),
        in_specs=[HBM_SPEC] * (2 * na), out_specs=[SEM_SPEC, SEM_SPEC] + [HBM_SPEC] * (2 * na) + [BS(memory_space=pltpu.VMEM)],
        input_output_aliases={i: 2 + i for i in range(2 * na)},
        compiler_params=pltpu.CompilerParams(has_side_effects=SIDE_EFFECT),
    )(*[pltpu.with_memory_space_constraint(b, pltpu.HBM) for b in blocks],
      *[pltpu.with_memory_space_constraint(ld, pltpu.HBM) for ld in lands])
    return dict(send=res[0], recv=res[1], srcs=res[2:2 + na], lands=res[2 + na:2 + 2 * na], scatter=scatter), res[-1]


def _send_all_wait(flight, after, name):
    na = len(flight["srcs"])

    def body(*refs):
        srcs, lands = refs[:na], refs[na:2 * na]
        send_sems, recv_sems = refs[2 * na], refs[2 * na + 1]
        for a in range(na):
            for rel in range(1, N_DEV):
                cp = pltpu.make_async_remote_copy(
                    src_ref=srcs[a].at[0] if flight["scatter"] else srcs[a], dst_ref=lands[a].at[0],
                    send_sem=send_sems.at[7 * a + rel - 1], recv_sem=recv_sems.at[7 * a + rel - 1],
                    device_id=_peer(rel), device_id_type=MESH)
                cp.wait_send()
                cp.wait_recv()

    arrays = list(flight["srcs"]) + list(flight["lands"])
    res = pl.pallas_call(
        body, name=name, out_shape=[pltpu.HBM(a.shape, a.dtype) for a in arrays],
        in_specs=[HBM_SPEC] * (2 * na) + [SEM_SPEC, SEM_SPEC, BS(memory_space=pl.ANY)], out_specs=[HBM_SPEC] * (2 * na),
        input_output_aliases={i: i for i in range(2 * na)},
        compiler_params=pltpu.CompilerParams(has_side_effects=SIDE_EFFECT),
    )(*arrays, flight["send"], flight["recv"], after)
    return res[:na], res[na:]


def _sum_slabs(land, name):
    _, r, c = land.shape
    tr = _pick(r, (256, 128, 64, 32, 16, 8))

    def body(l_ref, o_ref):
        acc = l_ref[0].astype(F32)
        for d in range(1, N_DEV):
            acc = acc + l_ref[d].astype(F32)
        o_ref[...] = acc

    return pl.pallas_call(
        body, name=name, grid=(r // tr,), out_shape=SDS((r, c), F32),
        in_specs=[BS((N_DEV, tr, c), lambda j: (0, j, 0))], out_specs=BS((tr, c), lambda j: (j, 0)),
        compiler_params=_params(1),
    )(land)


def _proj_fwd(h, nw8, wp, tag):
    n = h.shape[0]
    tm = _pick(n, (1408, 768, 512, 384, 256, 192, 128, 64))
    tn = 896

    def body(h_ref, nw_ref, w_ref, proj_ref, xn_ref):
        @pl.when(pl.program_id(1) == 0)
        def _():
            xn_ref[...] = _rms(h_ref[...], nw_ref[0:1, :]).astype(BF16)

        proj_ref[...] = jnp.dot(xn_ref[...], w_ref[...], preferred_element_type=F32)

    return pl.pallas_call(
        body, name=f"proj_fwd_{tag}", grid=(n // tm, PROJ_W // tn),
        in_specs=[BS((tm, D_MODEL), lambda i, j: (i, 0)), BS((SUBLANES, D_MODEL), lambda i, j: (0, 0)),
                  BS((D_MODEL, tn), lambda i, j: (0, j))],
        out_specs=[BS((tm, tn), lambda i, j: (i, j)), BS((tm, D_MODEL), lambda i, j: (i, 0))],
        out_shape=[SDS((n, PROJ_W), F32), SDS((n, D_MODEL), BF16)], compiler_params=_params(2),
    )(h, nw8, wp)


def _conv_ext(x_ext, cw_ref):
    y = x_ext * cw_ref[3:4, :]
    for k in range(3):
        y = y + _shift_down(x_ext, 3 - k) * cw_ref[k:k + 1, :]
    return y[SUBLANES:]


def _prep_fwd(proj, cw8, aux, lb8, nseq, t_len, tag):
    n = proj.shape[0]
    tt = _pick(t_len, (192, 128, 64))
    nt_ = t_len // tt
    qkv_w = 3 * HEADS_W

    def body(cur_ref, prev_ref, misc_ref, bq_ref, bf_ref, cw_ref, aux_ref, lb_ref,
             q_ref, k_ref, v_ref, b_ref, g_ref, qb_ref, kb_ref, lf_ref, ext_ref):
        t = pl.program_id(1)
        ext_ref[0:SUBLANES, :] = jnp.where(t == 0, 0.0, prev_ref[...])
        ext_ref[SUBLANES:, :] = cur_ref[...]
        y = ext_ref[SUBLANES:, :] * cw_ref[3:4, :]
        for kk in range(3):
            y = y + ext_ref[SUBLANES - 3 + kk:SUBLANES - 3 + kk + tt, :] * cw_ref[kk:kk + 1, :]
        real = (t * tt + _iota2((tt, 1), 0)) >= N_PAD
        misc = misc_ref[...]
        auxv = aux_ref[...]
        for hd in range(N_HEADS):
            sl = slice(hd * D_HEAD, (hd + 1) * D_HEAD)
            q_ref[:, sl] = _l2n_act(y[:, sl], D_HEAD ** -0.5)
            k_ref[:, sl] = _l2n_act(y[:, HEADS_W + hd * D_HEAD:HEADS_W + (hd + 1) * D_HEAD], 1.0)
            v_ref[:, sl] = _silu(y[:, 2 * HEADS_W + hd * D_HEAD:2 * HEADS_W + (hd + 1) * D_HEAD])
            b_ref[:, sl], g_ref[:, sl] = _gdn_gates(misc, auxv, real, hd)
        qb_ref[...], kb_ref[...], lf_ref[...] = _hgrn_prep(bq_ref[...], bf_ref[...], lb_ref[0:1, :], real)

    rb = tt // SUBLANES
    row = lambda s, t: s * nt_ + t
    wide = BS((tt, HEADS_W), lambda s, t: (row(s, t), 0))
    return pl.pallas_call(
        body, name=f"prep_fwd_{tag}", grid=(nseq, nt_),
        in_specs=[BS((tt, qkv_w), lambda s, t: (row(s, t), 0)),
                  BS((SUBLANES, qkv_w), lambda s, t: (jnp.maximum(row(s, t) * rb - 1, 0), 0)),
                  BS((tt, LANES), lambda s, t: (row(s, t), C_MISC // LANES)),
                  BS((tt, HEADS_W), lambda s, t: (row(s, t), C_BQ // HEADS_W)),
                  BS((tt, HEADS_W), lambda s, t: (row(s, t), C_BF // HEADS_W)),
                  BS((SUBLANES, qkv_w), lambda s, t: (0, 0)), BS((SUBLANES, LANES), lambda s, t: (0, 0)),
                  BS((SUBLANES, HEADS_W), lambda s, t: (0, 0))],
        out_specs=[wide] * 8, out_shape=[SDS((n, HEADS_W), F32)] * 8,
        scratch_shapes=[pltpu.VMEM((tt + SUBLANES, qkv_w), F32)], compiler_params=_params(2),
    )(proj, proj, proj, proj, proj, cw8, aux, lb8)


GDN_SEQS = 4
HGRN_SEQS = 2


def _seq_block(nseq, most):
    return max(s for s in (1, 2, 4) if s <= most and nseq % s == 0)


def _to_chains(x):
    return jnp.concatenate([x[:, :, hd * D_HEAD:(hd + 1) * D_HEAD] for hd in range(N_HEADS)], axis=0)


def _from_chains(ref, rows, val):
    sb = val.shape[0] // N_HEADS
    for hd in range(N_HEADS):
        ref[:, rows, hd * D_HEAD:(hd + 1) * D_HEAD] = val[hd * sb:(hd + 1) * sb].astype(ref.dtype)


def _mixers_fwd(q, k, v, b, g, qb, kb, vb, vb_col, lf, nseq, t_len, tag):
    sb, hs = _seq_block(nseq, GDN_SEQS), _seq_block(nseq, HGRN_SEQS)
    nc = t_len // GDN_CHUNK
    chains = N_HEADS * sb

    def body(q_ref, k_ref, v_ref, b_ref, g_ref, qb_ref, kb_ref, vb_ref, lf_ref, oa_ref, ob_ref, cka_ref, ckb_ref,
             sa_ref, sb_ref):
        @pl.when(pl.program_id(1) == 0)
        def _():
            sa_ref[...] = jnp.zeros_like(sa_ref)
            sb_ref[...] = jnp.zeros_like(sb_ref)

        s = sa_ref[...]
        cka_ref[...] = s
        o, s_new = _gdn_chunk(*[_to_chains(r[...]) for r in (q_ref, k_ref, v_ref, b_ref, g_ref)], s)
        _from_chains(oa_ref, slice(None), o)
        sa_ref[...] = s_new
        for part in range(sb // hs):
            seqs, ch = slice(part * hs, (part + 1) * hs), slice(part * N_HEADS * hs, (part + 1) * N_HEADS * hs)
            s = sb_ref[ch]
            ckb_ref[ch] = s
            o, s_new = _hgrn_block(*[_to_chains(r[seqs]) for r in (qb_ref, kb_ref, vb_ref, lf_ref)], s)
            for hd in range(N_HEADS):
                ob_ref[seqs, :, hd * D_HEAD:(hd + 1) * D_HEAD] = o[hd * hs:(hd + 1) * hs]
            sb_ref[ch] = s_new

    blk = lambda cb: BS((sb, GDN_CHUNK, HEADS_W), lambda p, c: (p, c, cb))
    ck_spec = BS((None, None, chains, D_HEAD, D_HEAD), lambda p, c: (p, c, 0, 0, 0))
    ck_shape = SDS((nseq // sb, nc, chains, D_HEAD, D_HEAD), F32)
    view = lambda a: a.reshape(nseq, t_len, a.shape[1])
    oa, ob, cka, ckb = pl.pallas_call(
        body, name=f"mixers_fwd_{tag}", grid=(nseq // sb, nc),
        in_specs=[blk(0)] * 7 + [blk(vb_col), blk(0)], out_specs=[blk(0), blk(0), ck_spec, ck_spec],
        out_shape=[SDS((nseq, t_len, HEADS_W), F32)] * 2 + [ck_shape] * 2,
        scratch_shapes=[pltpu.VMEM((chains, D_HEAD, D_HEAD), F32)] * 2, compiler_params=_params(2),
    )(*[view(a) for a in (q, k, v, b, g, qb, kb, vb, lf)])
    return oa.reshape(-1, HEADS_W), ob.reshape(-1, HEADS_W), cka, ckb


def _mixers_bwd(q, k, v, b, g, qb, kb, vb, vb_col, lf, cka, ckb, doa, dob, nseq, t_len, tag):
    sb, hs = _seq_block(nseq, GDN_SEQS), _seq_block(nseq, HGRN_SEQS)
    nc = t_len // GDN_CHUNK
    chains = N_HEADS * sb

    def body(q_ref, k_ref, v_ref, b_ref, g_ref, qb_ref, kb_ref, vb_ref, lf_ref, doa_ref, dob_ref, cka_ref, ckb_ref,
             dq_ref, dk_ref, dv_ref, db_ref, dg_ref, dqb_ref, dkb_ref, dvb_ref, dlf_ref, dsa_ref, dsb_ref):
        @pl.when(pl.program_id(1) == 0)
        def _():
            dsa_ref[...] = jnp.zeros_like(dsa_ref)
            dsb_ref[...] = jnp.zeros_like(dsb_ref)

        _, vjp = jax.vjp(_gdn_chunk, *[_to_chains(r[...]) for r in (q_ref, k_ref, v_ref, b_ref, g_ref)], cka_ref[...])
        grads = vjp((_to_chains(doa_ref[...]), dsa_ref[...]))
        for ref, val in zip((dq_ref, dk_ref, dv_ref, db_ref, dg_ref), grads[:5]):
            _from_chains(ref, slice(None), val)
        dsa_ref[...] = grads[5]
        for part in range(sb // hs):
            seqs, ch = slice(part * hs, (part + 1) * hs), slice(part * N_HEADS * hs, (part + 1) * N_HEADS * hs)
            _, vjp = jax.vjp(functools.partial(_hgrn_block, group=SUBLANES),
                             *[_to_chains(r[seqs]) for r in (qb_ref, kb_ref, vb_ref, lf_ref)], ckb_ref[ch])
            grads = vjp((_to_chains(dob_ref[seqs]), dsb_ref[ch]))
            for ref, val in zip((dqb_ref, dkb_ref, dvb_ref, dlf_ref), grads[:4]):
                for hd in range(N_HEADS):
                    ref[seqs, :, hd * D_HEAD:(hd + 1) * D_HEAD] = val[hd * hs:(hd + 1) * hs].astype(ref.dtype)
            dsb_ref[ch] = grads[4]

    blk = lambda cb: BS((sb, GDN_CHUNK, HEADS_W), lambda p, c: (p, nc - 1 - c, cb))
    ck_spec = BS((None, None, chains, D_HEAD, D_HEAD), lambda p, c: (p, nc - 1 - c, 0, 0, 0))
    view = lambda a: a.reshape(nseq, t_len, a.shape[1])
    dts = [F32] * 7 + [BF16, F32]
    res = pl.pallas_call(
        body, name=f"mixers_bwd_{tag}", grid=(nseq // sb, nc),
        in_specs=[blk(0)] * 7 + [blk(vb_col), blk(0), blk(0), blk(0), ck_spec, ck_spec], out_specs=[blk(0)] * 9,
        out_shape=[SDS((nseq, t_len, HEADS_W), dt) for dt in dts],
        scratch_shapes=[pltpu.VMEM((chains, D_HEAD, D_HEAD), F32)] * 2, compiler_params=_params(2),
    )(*[view(a) for a in (q, k, v, b, g, qb, kb, vb, lf, doa, dob)], cka, ckb)
    return [r.reshape(-1, HEADS_W) for r in res]


def _post_values(oa_ref, ob_ref, z_ref, bg_ref, ga_ref, gb_ref, gn_ref, wa_ref, wb_ref, ya_ref, yb_ref):
    for hd in range(N_HEADS):
        sl = slice(hd * D_HEAD, (hd + 1) * D_HEAD)
        ya_ref[:, sl] = _gated_norm(oa_ref[:, sl], z_ref[:, sl], gn_ref[0:1, :]).astype(BF16)
        yb_ref[:, sl] = _gated_norm(ob_ref[:, sl], bg_ref[:, sl], gn_ref[1:2, :]).astype(BF16)
    pa = jnp.dot(ya_ref[...], wa_ref[...], preferred_element_type=F32)
    pb = jnp.dot(yb_ref[...], wb_ref[...], preferred_element_type=F32)
    return pa, pb, _sigmoid(ga_ref[...]), _sigmoid(gb_ref[...])


def _post_specs(tm):
    r2 = lambda i: (i, 0)
    return [BS((tm, HEADS_W), r2), BS((tm, HEADS_W), r2),
            BS((tm, HEADS_W), lambda i: (i, C_Z // HEADS_W)), BS((tm, HEADS_W), lambda i: (i, C_BG // HEADS_W)),
            BS((tm, D_MODEL), lambda i: (i, C_GA // D_MODEL)), BS((tm, D_MODEL), lambda i: (i, C_GB // D_MODEL)),
            BS((tm, D_MODEL), r2), BS((SUBLANES, LANES), lambda i: (0, 0))]


def _post_fwd(oa, ob, proj, h, gn8, wa, wb, wout, tag):
    n = h.shape[0]
    tm = _pick(n, (256, 192, 128, 64))

    def body(oa_ref, ob_ref, z_ref, bg_ref, ga_ref, gb_ref, h_ref, gn_ref, wa_ref, wb_ref, wout_ref, out_ref,
             ya_ref, yb_ref):
        pa, pb, sa, sb = _post_values(oa_ref, ob_ref, z_ref, bg_ref, ga_ref, gb_ref, gn_ref, wa_ref, wb_ref,
                                      ya_ref, yb_ref)
        mixed = (sa * pa + sb * pb).astype(BF16)
        out_ref[...] = h_ref[...] + jnp.dot(mixed, wout_ref[...], preferred_element_type=F32)

    full = lambda i: (0, 0)
    return pl.pallas_call(
        body, name=f"post_fwd_{tag}", grid=(n // tm,),
        in_specs=_post_specs(tm) + [BS((HEADS_W, D_MODEL), full), BS((HEADS_W, D_MODEL), full),
                                    BS((D_MODEL, D_MODEL), full)],
        out_specs=BS((tm, D_MODEL), lambda i: (i, 0)), out_shape=SDS((n, D_MODEL), F32),
        scratch_shapes=[pltpu.VMEM((tm, HEADS_W), BF16), pltpu.VMEM((tm, HEADS_W), BF16)], compiler_params=_params(1),
    )(oa, ob, proj, proj, proj, proj, h, gn8, wa, wb, wout)


def _post_bwd(dh, oa, ob, proj, h, gn8, wa, wb, wout, tag):
    n = h.shape[0]
    tm = _pick(n, (256, 192, 128, 64))

    def body(dh_ref, oa_ref, ob_ref, z_ref, bg_ref, ga_ref, gb_ref, h_ref, gn_ref, wa_ref, wb_ref, wout_ref,
             doa_ref, dob_ref, dz_ref, dbg_ref, dga_ref, dgb_ref, dwa_ref, dwb_ref, dwout_ref, dgn_ref,
             ya_ref, yb_ref):
        @pl.when(pl.program_id(0) == 0)
        def _():
            dwa_ref[...] = jnp.zeros_like(dwa_ref)
            dwb_ref[...] = jnp.zeros_like(dwb_ref)
            dwout_ref[...] = jnp.zeros_like(dwout_ref)
            dgn_ref[...] = jnp.zeros_like(dgn_ref)

        pa, pb, sa, sb = _post_values(oa_ref, ob_ref, z_ref, bg_ref, ga_ref, gb_ref, gn_ref, wa_ref, wb_ref,
                                      ya_ref, yb_ref)
        mixed = (sa * pa + sb * pb).astype(BF16)
        dout = dh_ref[...].astype(BF16)
        dwout_ref[...] += _dg(mixed, dout, ((0,), (0,)))
        dmixed = _dg(dout, wout_ref[...], ((1,), (1,)))
        dga_ref[...] = (dmixed * pa * sa * (1.0 - sa)).astype(BF16)
        dgb_ref[...] = (dmixed * pb * sb * (1.0 - sb)).astype(BF16)
        dpa = (dmixed * sa).astype(BF16)
        dpb = (dmixed * sb).astype(BF16)
        dwa_ref[...] += _dg(ya_ref[...], dpa, ((0,), (0,)))
        dwb_ref[...] += _dg(yb_ref[...], dpb, ((0,), (0,)))
        dya = _dg(dpa, wa_ref[...], ((1,), (1,)))
        dyb = _dg(dpb, wb_ref[...], ((1,), (1,)))
        dgn_a = jnp.zeros((1, D_HEAD), F32)
        dgn_b = jnp.zeros((1, D_HEAD), F32)
        for hd in range(N_HEADS):
            sl = slice(hd * D_HEAD, (hd + 1) * D_HEAD)
            _, vjp = jax.vjp(_gated_norm, oa_ref[:, sl], z_ref[:, sl], gn_ref[0:1, :])
            doa, dz, dgw = vjp(dya[:, sl])
            doa_ref[:, sl], dz_ref[:, sl], dgn_a = doa, dz.astype(BF16), dgn_a + dgw
            _, vjp = jax.vjp(_gated_norm, ob_ref[:, sl], bg_ref[:, sl], gn_ref[1:2, :])
            dob, dbg, dgw = vjp(dyb[:, sl])
            dob_ref[:, sl], dbg_ref[:, sl], dgn_b = dob, dbg.astype(BF16), dgn_b + dgw
        dgn_ref[0:1, :] += dgn_a
        dgn_ref[1:2, :] += dgn_b

    full = lambda i: (0, 0)
    r2 = lambda i: (i, 0)
    return pl.pallas_call(
        body, name=f"post_bwd_{tag}", grid=(n // tm,),
        in_specs=[BS((tm, D_MODEL), r2)] + _post_specs(tm) + [
            BS((HEADS_W, D_MODEL), full), BS((HEADS_W, D_MODEL), full), BS((D_MODEL, D_MODEL), full)],
        out_specs=[BS((tm, HEADS_W), r2)] * 4 + [BS((tm, D_MODEL), r2)] * 2 + [
            BS((HEADS_W, D_MODEL), full), BS((HEADS_W, D_MODEL), full), BS((D_MODEL, D_MODEL), full),
            BS((SUBLANES, LANES), full)],
        out_shape=[SDS((n, HEADS_W), F32), SDS((n, HEADS_W), F32), SDS((n, HEADS_W), BF16), SDS((n, HEADS_W), BF16),
                   SDS((n, D_MODEL), BF16), SDS((n, D_MODEL), BF16), SDS((HEADS_W, D_MODEL), F32),
                   SDS((HEADS_W, D_MODEL), F32), SDS((D_MODEL, D_MODEL), F32), SDS((SUBLANES, LANES), F32)],
        scratch_shapes=[pltpu.VMEM((tm, HEADS_W), BF16), pltpu.VMEM((tm, HEADS_W), BF16)], compiler_params=_params(1),
    )(dh, oa, ob, proj, proj, proj, proj, h, gn8, wa, wb, wout)


def _loss_head(h, fw8, target, nseq, t_len):
    n = h.shape[0]
    nc = t_len // GDN_CHUNK
    sub = 3 if nc % 3 == 0 else 1
    tl, nt = sub * GDN_CHUNK, nc // sub
    inv_d = 1.0 / D_MODEL

    def body(h_ref, fw_ref, *rest):
        tgt_refs, (dh_ref, acc_ref) = rest[:sub], rest[sub:]

        @pl.when((pl.program_id(0) == 0) & (pl.program_id(1) == 0))
        def _():
            acc_ref[...] = jnp.zeros_like(acc_ref)

        frames = ((pl.program_id(1) * tl + _iota2((tl, 1), 0)) >= N_PAD + N_META).astype(F32)
        y, vjp = jax.vjp(_rms, h_ref[...], fw_ref[0:1, :])
        err = (y - jnp.concatenate([r[...] for r in tgt_refs], axis=0)) * frames
        dx, dfw = vjp(err * inv_d)
        dh_ref[...] = dx
        acc_ref[0:1, :] += dfw
        acc_ref[1:2, :] += (0.5 * inv_d) * jnp.sum(err * err, axis=0, keepdims=True)

    tgt_spec = lambda u: BS((None, GDN_CHUNK, D_MODEL), lambda s, t: (s, jnp.maximum(t * sub + u - 1, 0), 0))
    return pl.pallas_call(
        body, name="loss_head", grid=(nseq, nt),
        in_specs=[BS((tl, D_MODEL), lambda s, t: (s * nt + t, 0)), BS((SUBLANES, D_MODEL), lambda s, t: (0, 0))]
        + [tgt_spec(u) for u in range(sub)],
        out_specs=[BS((tl, D_MODEL), lambda s, t: (s * nt + t, 0)), BS((SUBLANES, D_MODEL), lambda s, t: (0, 0))],
        out_shape=[SDS((n, D_MODEL), F32), SDS((SUBLANES, D_MODEL), F32)], compiler_params=_params(2),
    )(h, fw8, *[target] * sub)


def _prep_bwd(proj, dq, dk, dv, db, dg, dqb, dkb, dlf, cw8, aux, lb8, nseq, t_len, tag):
    n = proj.shape[0]
    tt = _pick(t_len, (192, 128, 64))
    nt_ = t_len // tt
    qkv_w = 3 * HEADS_W
    rb = tt // SUBLANES
    ext = tt + SUBLANES

    def body(cur_ref, prev_ref, next_ref, misc_ref, bq_ref, bf_ref, dq_ref, dqn_ref, dk_ref, dkn_ref, dv_ref, dvn_ref,
             db_ref, dg_ref, dqb_ref, dkb_ref, dlf_ref, cw_ref, aux_ref, lb_ref,
             dqkv_ref, dmisc_ref, dbq_ref, dbf_ref, dcw_ref, daux_ref, dlb_ref, dy_ref):
        s, t = pl.program_id(0), pl.program_id(1)

        @pl.when((s == 0) & (t == 0))
        def _():
            dcw_ref[...] = jnp.zeros_like(dcw_ref)
            daux_ref[...] = jnp.zeros_like(daux_ref)
            dlb_ref[...] = jnp.zeros_like(dlb_ref)

        prev = jnp.where(t == 0, 0.0, prev_ref[...])
        x_ext = jnp.concatenate([prev, cur_ref[...], next_ref[...]], axis=0)
        y = _conv_ext(x_ext, cw_ref)
        inside = (t < nt_ - 1) | (_iota2((ext, 1), 0) < tt)
        dy_ref[0:SUBLANES, :] = jnp.zeros((SUBLANES, qkv_w), F32)
        for hd in range(N_HEADS):
            for grp, (g_ref, gn_ref, scale) in enumerate(((dq_ref, dqn_ref, D_HEAD ** -0.5), (dk_ref, dkn_ref, 1.0),
                                                          (dv_ref, dvn_ref, None))):
                lo = grp * HEADS_W + hd * D_HEAD
                sl = slice(hd * D_HEAD, (hd + 1) * D_HEAD)
                cot = jnp.concatenate([g_ref[:, sl], gn_ref[:, sl]], axis=0)
                fn = _silu if scale is None else functools.partial(_l2n_act, scale=scale)
                _, vjp = jax.vjp(fn, y[:, lo:lo + D_HEAD])
                dy_ref[SUBLANES:, lo:lo + D_HEAD] = jnp.where(inside, vjp(cot)[0], 0.0)
        dy_ext = dy_ref[...]
        dx = dy_ext * cw_ref[3:4, :]
        for kk in range(3):
            dx = dx + _shift_up(dy_ext, 3 - kk) * cw_ref[kk:kk + 1, :]
        dqkv_ref[...] = dx[SUBLANES:SUBLANES + tt].astype(BF16)
        dy_cur = dy_ext[SUBLANES:SUBLANES + tt]
        for kk in range(4):
            xs = _shift_down(x_ext, 3 - kk)[SUBLANES:SUBLANES + tt]
            dcw_ref[kk:kk + 1, :] += jnp.sum(xs * dy_cur, axis=0, keepdims=True)

        real = (t * tt + _iota2((tt, 1), 0)) >= N_PAD
        dmisc = jnp.zeros((tt, LANES), F32)
        daux = jnp.zeros((SUBLANES, LANES), F32)
        for hd in range(N_HEADS):
            sl = slice(hd * D_HEAD, (hd + 1) * D_HEAD)
            _, vjp = jax.vjp(lambda m, a: _gdn_gates(m, a, real, hd), misc_ref[...], aux_ref[...])
            dm, da = vjp((db_ref[:, sl], dg_ref[:, sl]))
            dmisc, daux = dmisc + dm, daux + da
        dmisc_ref[...] = dmisc.astype(BF16)
        daux_ref[...] += daux
        _, vjp = jax.vjp(lambda a, b, c: _hgrn_prep(a, b, c, real), bq_ref[...], bf_ref[...], lb_ref[0:1, :])
        dbq, dbf, dlb = vjp((dqb_ref[...], dkb_ref[...], dlf_ref[...]))
        dbq_ref[...], dbf_ref[...] = dbq.astype(BF16), dbf.astype(BF16)
        dlb_ref[0:1, :] += dlb

    row = lambda s, t: s * nt_ + t
    cur = lambda s, t: (row(s, t), 0)
    nxt = lambda s, t: (jnp.minimum((row(s, t) + 1) * rb, n // SUBLANES - 1), 0)
    wide = BS((tt, HEADS_W), cur)
    halo = BS((SUBLANES, HEADS_W), nxt)
    full = lambda s, t: (0, 0)
    return pl.pallas_call(
        body, name=f"prep_bwd_{tag}", grid=(nseq, nt_),
        in_specs=[BS((tt, qkv_w), cur), BS((SUBLANES, qkv_w), lambda s, t: (jnp.maximum(row(s, t) * rb - 1, 0), 0)),
                  BS((SUBLANES, qkv_w), nxt), BS((tt, LANES), lambda s, t: (row(s, t), C_MISC // LANES)),
                  BS((tt, HEADS_W), lambda s, t: (row(s, t), C_BQ // HEADS_W)),
                  BS((tt, HEADS_W), lambda s, t: (row(s, t), C_BF // HEADS_W)),
                  wide, halo, wide, halo, wide, halo, wide, wide, wide, wide, wide,
                  BS((SUBLANES, qkv_w), full), BS((SUBLANES, LANES), full), BS((SUBLANES, HEADS_W), full)],
        out_specs=[BS((tt, qkv_w), cur), BS((tt, LANES), cur), wide, wide,
                   BS((SUBLANES, qkv_w), full), BS((SUBLANES, LANES), full), BS((SUBLANES, HEADS_W), full)],
        out_shape=[SDS((n, qkv_w), BF16), SDS((n, LANES), BF16), SDS((n, HEADS_W), BF16), SDS((n, HEADS_W), BF16),
                   SDS((SUBLANES, qkv_w), F32), SDS((SUBLANES, LANES), F32), SDS((SUBLANES, HEADS_W), F32)],
        scratch_shapes=[pltpu.VMEM((tt + 2 * SUBLANES, qkv_w), F32)], compiler_params=_params(2),
    )(proj, proj, proj, proj, proj, proj, dq, dq, dk, dk, dv, dv, db, dg, dqb, dkb, dlf, cw8, aux, lb8)


def _proj_bwd_x(pieces, wp, h, nw8, dh_res, tag):
    n = h.shape[0]
    tm = _pick(n, (256, 192, 128, 64))
    widths = [p.shape[1] for p in pieces]
    assert sum(widths) == PROJ_W

    def body(*refs):
        p_refs = refs[:len(pieces)]
        w_ref, h_ref, nw_ref, dres_ref, dh_ref, dnw_ref = refs[len(pieces):]

        @pl.when(pl.program_id(0) == 0)
        def _():
            dnw_ref[...] = jnp.zeros_like(dnw_ref)

        dxn, off = None, 0
        for p_ref, w in zip(p_refs, widths):
            part = _dg(p_ref[...], w_ref[:, off:off + w], ((1,), (1,)))
            dxn = part if dxn is None else dxn + part
            off += w
        _, vjp = jax.vjp(_rms, h_ref[...], nw_ref[0:1, :])
        dx, dnw = vjp(dxn)
        dh_ref[...] = dres_ref[...] + dx
        dnw_ref[0:1, :] += dnw

    r2 = lambda i: (i, 0)
    full = lambda i: (0, 0)
    return pl.pallas_call(
        body, name=f"proj_bwd_x_{tag}", grid=(n // tm,),
        in_specs=[BS((tm, w), r2) for w in widths] + [BS((D_MODEL, PROJ_W), full), BS((tm, D_MODEL), r2),
                                                      BS((SUBLANES, D_MODEL), full), BS((tm, D_MODEL), r2)],
        out_specs=[BS((tm, D_MODEL), r2), BS((SUBLANES, D_MODEL), full)],
        out_shape=[SDS((n, D_MODEL), F32), SDS((SUBLANES, D_MODEL), F32)], compiler_params=_params(1),
    )(*pieces, wp, h, nw8, dh_res)


def _proj_bwd_w(xn, pieces, tag):
    n = xn.shape[0]
    tm = _pick(n, (384, 256, 192, 128, 64))
    widths = [p.shape[1] for p in pieces]
    assert sum(widths) == PROJ_W

    def body(*refs):
        x_ref, p_refs = refs[0], refs[1:1 + len(pieces)]
        o_ref, acc_ref = refs[1 + len(pieces):]

        @pl.when(pl.program_id(0) == 0)
        def _():
            acc_ref[...] = jnp.zeros_like(acc_ref)

        off = 0
        for p_ref, w in zip(p_refs, widths):
            acc_ref[:, off:off + w] += _dg(x_ref[...], p_ref[...], ((0,), (0,)))
            off += w

        @pl.when(pl.program_id(0) == pl.num_programs(0) - 1)
        def _():
            pltpu.sync_copy(acc_ref, o_ref)

    r2 = lambda i: (i, 0)
    return pl.pallas_call(
        body, name=f"proj_bwd_w_{tag}", grid=(n // tm,),
        in_specs=[BS((tm, D_MODEL), r2)] + [BS((tm, w), r2) for w in widths], out_specs=BS(memory_space=pl.ANY),
        out_shape=SDS((D_MODEL, PROJ_W), F32), scratch_shapes=[pltpu.VMEM((D_MODEL, PROJ_W), F32)],
        compiler_params=_params(1),
    )(xn, *pieces)


def _adamw(w, g, m, v, name):
    lead, rows, cols = w.shape
    tr = _pick(rows, (256, 128, 64, 32, 16, 8, 4, 2, 1)) if rows > 256 else rows

    def body(w_ref, g_ref, m_ref, v_ref, d_ref, nm_ref, nv_ref):
        gr = g_ref[...]
        m_new = ADAM_B1 * m_ref[...] + (1.0 - ADAM_B1) * gr
        v_new = ADAM_B2 * v_ref[...] + (1.0 - ADAM_B2) * jnp.square(gr)
        m_hat = m_new / (1.0 - ADAM_B1 ** ADAM_STEP)
        v_hat = v_new / (1.0 - ADAM_B2 ** ADAM_STEP)
        d_ref[...] = -ADAM_LR * (m_hat / (jnp.sqrt(v_hat) + ADAM_EPS) + ADAM_WD * w_ref[...])
        nm_ref[...] = m_new
        nv_ref[...] = v_new

    blk = BS((None, tr, cols), lambda a, i: (a, i, 0))
    return pl.pallas_call(
        body, name=name, grid=(lead, rows // tr), in_specs=[blk] * 4, out_specs=[blk] * 3,
        out_shape=[SDS((lead, rows, cols), F32)] * 3, compiler_params=_params(2),
    )(w, g, m, v)


def _row8(v, width):
    v = jnp.atleast_2d(v).astype(F32)
    return jnp.pad(v, ((0, SUBLANES - v.shape[0]), (0, width - v.shape[1])))


REF_MISC = 1536
N_MISC = 2 * N_HEADS
LAYOUT_RUNS = ((0, REF_MISC, 0), (REF_MISC + N_MISC, REF_W, REF_MISC), (REF_MISC, REF_MISC + N_MISC, C_MISC))


def _to_layout(slabs, tag):
    n_slabs, rows, width = slabs.shape
    tr = _pick(rows, (256, 128, 64, 32, 16))

    def body(x_ref, o_ref):
        off = 0
        for lo, hi, _ in sorted(LAYOUT_RUNS, key=lambda run: run[2]):
            for j in range(n_slabs):
                a, b = max(lo, j * width), min(hi, (j + 1) * width)
                if a < b:
                    o_ref[:, off:off + b - a] = x_ref[j, :, a - j * width:b - j * width]
                    off += b - a
        o_ref[:, off:] = jnp.zeros((tr, PROJ_W - off), o_ref.dtype)

    return pl.pallas_call(
        body, name=f"weights_layout_{tag}", grid=(rows // tr,), out_shape=SDS((rows, PROJ_W), slabs.dtype),
        in_specs=[BS((n_slabs, tr, width), lambda i: (0, i, 0))], out_specs=BS((tr, PROJ_W), lambda i: (i, 0)),
        compiler_params=_params(1),
    )(slabs)


def _from_layout(dw, n_slabs):
    width = REF_W // n_slabs
    slabs = []
    for j in range(n_slabs):
        pieces = []
        for lo, hi, at in sorted(LAYOUT_RUNS):
            a, b = max(lo, j * width), min(hi, (j + 1) * width)
            if a < b:
                pieces.append(dw[:, at + a - lo:at + b - lo])
        slabs.append(jnp.concatenate(pieces, axis=1))
    return slabs


def _lower_bounds(lb):
    sm = jax.nn.softmax(lb.astype(F32), axis=0)
    return jnp.cumsum(sm, axis=0) - sm[0]


def kernel(x, meta_tokens, norm_w, w_in, conv_w, a_log, dt_bias, gnorm_a, gnorm_b, hgrn_lower_bounds, w_branch_a, w_branch_b, w_out, final_norm_w, loss_target, m_meta_tokens, m_norm_w, m_w_in, m_conv_w, m_a_log, m_dt_bias, m_gnorm_a, m_gnorm_b, m_hgrn_lower_bounds, m_w_branch_a, m_w_branch_b, m_w_out, m_final_norm_w, v_meta_tokens, v_norm_w, v_w_in, v_conv_w, v_a_log, v_dt_bias, v_gnorm_a, v_gnorm_b, v_hgrn_lower_bounds, v_w_branch_a, v_w_branch_b, v_w_out, v_final_norm_w):
    nseq, seq, _ = x.shape
    depth = norm_w.shape[0]
    t_len = N_PAD + N_META + seq
    n = nseq * t_len
    conv_c = conv_w.shape[2]
    my = 4 * lax.axis_index("x") + 2 * lax.axis_index("y") + lax.axis_index("c")

    assert depth >= 2
    by_cols = lambda g: g.transpose(1, 2, 0, 3).reshape(g.shape[1], g.shape[2], N_DEV * g.shape[3])
    first = _all_gather_hbm([w_in[:1].astype(BF16), conv_w, meta_tokens], "gather_first")
    later_flight, later_token = _send_all_start(
        [w_in[1:].astype(BF16), w_branch_a.astype(BF16), w_branch_b.astype(BF16), w_out.astype(BF16)], False,
        "gather_later_start")
    w_in_slabs = [first[0]]
    conv_full = by_cols(first[1])
    meta_full = first[2].transpose(1, 0, 2).reshape(N_META, D_MODEL)

    lb_all, lb_vjp = jax.vjp(_lower_bounds, hgrn_lower_bounds)

    h = jnp.concatenate([jnp.zeros((nseq, N_PAD, D_MODEL), F32),
                         jnp.broadcast_to(meta_full[None], (nseq, N_META, D_MODEL)), x], axis=1).reshape(n, D_MODEL)
    saved = []
    for l in range(depth):
        wp = _to_layout(w_in_slabs[0][:, 0] if l == 0 else w_in_slabs[1][:, l - 1], l)
        nw8 = _row8(norm_w[l], D_MODEL)
        if l == 0:
            nw8 = nw8 + later_token[0:1, 0:1]
        cw8 = _row8(conv_full[l], 3 * HEADS_W)
        aux = _row8(jnp.stack([a_log[l], dt_bias[l]]), LANES)
        lb8 = _row8(lb_all[l], HEADS_W)
        gn8 = _row8(jnp.stack([gnorm_a[l], gnorm_b[l]]), LANES)
        proj, xn = _proj_fwd(h, nw8, wp, l)
        q, k, v, b, g, qb, kb, lf = _prep_fwd(proj, cw8, aux, lb8, nseq, t_len, l)
        oa, ob, sck_a, sck_b = _mixers_fwd(q, k, v, b, g, qb, kb, proj, C_BI // HEADS_W, lf, nseq, t_len, l)
        if l == 0:
            sent, landed = _send_all_wait(later_flight, ob, "gather_later_wait")
            landed = [lax.dynamic_update_slice(ld, own[None], (my,) + (0,) * own.ndim) for ld, own in zip(landed, sent)]
            w_in_slabs.append(landed[0])
            wa_full, wb_full = by_cols(landed[1]), by_cols(landed[2])
            wout_full = landed[3].transpose(1, 0, 2, 3).reshape(depth, D_MODEL, D_MODEL)
        wa_l, wb_l, wout_l = wa_full[l], wb_full[l], wout_full[l]
        h_next = _post_fwd(oa, ob, proj, h, gn8, wa_l, wb_l, wout_l, l)
        saved.append(dict(h=h, wp=wp, nw8=nw8, cw8=cw8, aux=aux, lb8=lb8, gn8=gn8, proj=proj, xn=xn, q=q, k=k, v=v, b=b,
                          wa=wa_l, wb=wb_l, wout=wout_l,
                          g=g, qb=qb, kb=kb, lf=lf, oa=oa, ob=ob, sck_a=sck_a, sck_b=sck_b))
        h = h_next

    dh, acc = _loss_head(h, _row8(final_norm_w, D_MODEL), loss_target, nseq, t_len)

    g_win, g_wa, g_wb, g_wout, g_conv, small = [], [], [], [], [], []

    def mixer_slabs(dwa_s, dwb_s, dwout_s):
        nl = len(dwa_s)
        rows = lambda a: jnp.stack(a).reshape(nl * HEADS_W, N_DEV, LANES).transpose(1, 0, 2)
        wout = jnp.stack(dwout_s).reshape(nl, N_DEV, LANES, D_MODEL).transpose(1, 0, 2, 3)
        return [jnp.concatenate([rows(dwa_s), rows(dwb_s)], axis=1).astype(BF16),
                wout.reshape(N_DEV, nl * LANES, D_MODEL).astype(BF16)]

    def win_slabs(per_layer, dtype):
        return jnp.stack([jnp.concatenate([sl[j] for sl in per_layer], axis=0) for j in range(N_DEV)]).astype(dtype)

    for l in reversed(range(depth)):
        s = saved[l]
        gn8, aux = s["gn8"], s["aux"]
        if l == 0:
            later_flight, later_token = _send_all_start(
                [win_slabs(g_win[::-1], BF16)] + mixer_slabs(g_wa[::-1], g_wb[::-1], g_wout[::-1]), True,
                "scatter_later_start")
            gn8 = gn8 + later_token[0:1, 0:1]
        doa, dob, dz, dbg, dga, dgb, dwa, dwb, dwout, dgn = _post_bwd(
            dh, s["oa"], s["ob"], s["proj"], s["h"], gn8, s["wa"], s["wb"], s["wout"], l)
        dq, dk, dv, db, dg, dqb, dkb, dbi, dlf = _mixers_bwd(
            s["q"], s["k"], s["v"], s["b"], s["g"], s["qb"], s["kb"], s["proj"], C_BI // HEADS_W, s["lf"], s["sck_a"],
            s["sck_b"], doa, dob, nseq, t_len, l)
        if l == 0:
            mixer_flight, mixer_token = _send_all_start(mixer_slabs([dwa], [dwb], [dwout]), True, "scatter_first_start")
            aux = aux + mixer_token[0:1, 0:1]
        dqkv, dmisc, dbq, dbf, dcw, daux, dlb = _prep_bwd(s["proj"], dq, dk, dv, db, dg, dqb, dkb, dlf, s["cw8"], aux,
                                                          s["lb8"], nseq, t_len, l)
        pieces = [dqkv, dz, dbq, dbf, dbi, dbg, dga, dgb, dmisc]
        g_win.append(_from_layout(_proj_bwd_w(s["xn"], pieces, l), N_DEV))
        g_conv.append(dcw[:4])
        nw8 = s["nw8"]
        if l == 0:
            dconv = jnp.stack(g_conv[::-1])
            conv_slabs = dconv.reshape(depth * dconv.shape[1], N_DEV, conv_c).transpose(1, 0, 2)
            win_flight, win_token = _send_all_start([win_slabs(g_win[-1:], BF16), conv_slabs], True, "scatter_win_start")
            nw8 = nw8 + win_token[0:1, 0:1]
        dh, dnw = _proj_bwd_x(pieces, s["wp"], s["h"], nw8, dh, l)
        g_wa.append(dwa)
        g_wb.append(dwb)
        g_wout.append(dwout)
        small.append((dnw[0], dgn[0], dgn[1], daux[0, :N_HEADS], daux[1, :N_HEADS], dlb[0]))
    small.reverse()
    dh = dh.reshape(nseq, t_len, D_MODEL)
    grad_x = dh[:, N_PAD + N_META:]

    packed = jnp.concatenate([small[0][1], small[1][1], small[0][2], small[1][2], small[0][3], small[1][3],
                              small[0][4], small[1][4]])
    tile = jnp.concatenate([
        jnp.sum(dh[:, N_PAD:N_PAD + N_META], axis=0), _row8(jnp.stack([small[0][0], small[1][0], acc[0]]), D_MODEL),
        _row8(jnp.stack([small[0][5], small[1][5]]), D_MODEL), _row8(packed, D_MODEL), _row8(acc[1], D_MODEL)], axis=0)
    tile = _all_reduce_small(tile, "reduce_small")
    loss = jnp.sum(tile[40])
    g_meta = lax.dynamic_slice_in_dim(tile[0:N_META], my * LANES, LANES, axis=1)
    g_norm, g_final = tile[16:18], tile[18]
    (g_lb,) = lb_vjp(tile[24:26, :HEADS_W])
    r21 = tile[32]
    g_gna, g_gnb = r21[0:256].reshape(2, LANES), r21[256:512].reshape(2, LANES)
    g_alog, g_dtb = r21[512:520].reshape(2, N_HEADS), r21[520:528].reshape(2, N_HEADS)

    def landed_sums(flight, tag):
        sent, landed = _send_all_wait(flight, dh, f"{tag}_wait")
        landed = [lax.dynamic_update_slice(ld, lax.dynamic_index_in_dim(src, my, 0, keepdims=True), (my, 0, 0))
                  for ld, src in zip(landed, sent)]
        return [_sum_slabs(ld, f"{tag}_sum{i}") for i, ld in enumerate(landed)]

    l_win, l_ab, l_wout = landed_sums(later_flight, "scatter_later")
    r_ab, r_wout = landed_sums(mixer_flight, "scatter_first")
    r_win, r_conv = landed_sums(win_flight, "scatter_win")
    both = lambda a, b, shape: jnp.concatenate([a.reshape(1, *shape[1:]), b.reshape(depth - 1, *shape[1:])])
    half, half_l = HEADS_W, (depth - 1) * HEADS_W
    mine = [both(r_win, l_win, w_in.shape), both(r_ab[:half], l_ab[:half_l], w_branch_a.shape),
            both(r_ab[half:], l_ab[half_l:], w_branch_b.shape), both(r_wout, l_wout, w_out.shape), r_conv]
    gseg = lambda i, shape: mine[i].reshape(shape)
    grads = {
        "meta_tokens": g_meta, "norm_w": g_norm, "w_in": gseg(0, w_in.shape), "conv_w": gseg(4, conv_w.shape),
        "a_log": g_alog, "dt_bias": g_dtb, "gnorm_a": g_gna, "gnorm_b": g_gnb, "hgrn_lower_bounds": g_lb,
        "w_branch_a": gseg(1, w_branch_a.shape), "w_branch_b": gseg(2, w_branch_b.shape), "w_out": gseg(3, w_out.shape),
        "final_norm_w": g_final}
    weights = {
        "meta_tokens": (meta_tokens, m_meta_tokens, v_meta_tokens), "norm_w": (norm_w, m_norm_w, v_norm_w),
        "w_in": (w_in, m_w_in, v_w_in), "conv_w": (conv_w, m_conv_w, v_conv_w), "a_log": (a_log, m_a_log, v_a_log),
        "dt_bias": (dt_bias, m_dt_bias, v_dt_bias), "gnorm_a": (gnorm_a, m_gnorm_a, v_gnorm_a),
        "gnorm_b": (gnorm_b, m_gnorm_b, v_gnorm_b),
        "hgrn_lower_bounds": (hgrn_lower_bounds, m_hgrn_lower_bounds, v_hgrn_lower_bounds),
        "w_branch_a": (w_branch_a, m_w_branch_a, v_w_branch_a), "w_branch_b": (w_branch_b, m_w_branch_b, v_w_branch_b),
        "w_out": (w_out, m_w_out, v_w_out), "final_norm_w": (final_norm_w, m_final_norm_w, v_final_norm_w)}
    names = list(weights)
    deltas, new_m, new_v = [], [], []
    for nm in names:
        w, m, v = weights[nm]
        view = (1,) * (3 - w.ndim) + w.shape
        d, m2, v2 = _adamw(w.reshape(view), grads[nm].reshape(view), m.reshape(view), v.reshape(view), f"adamw_{nm}")
        deltas.append(d.reshape(w.shape))
        new_m.append(m2.reshape(w.shape))
        new_v.append(v2.reshape(w.shape))
    return (loss, grad_x, *[grads[nm].reshape(weights[nm][0].shape) for nm in names], *deltas, *new_m, *new_v)
```

```python
import functools

import jax
import jax.numpy as jnp
from jax import lax
from jax.experimental import pallas as pl
from jax.experimental.pallas import tpu as pltpu

F32 = jnp.float32
BF16 = jnp.bfloat16

D_MODEL = 1024
N_HEADS = 4
D_HEAD = 128
HEADS_W = N_HEADS * D_HEAD
N_META = 16
N_PAD = 48
GDN_CHUNK = 64
HGRN_CHUNK = 16
EPS = 1e-6
N_DEV = 8
LANES = 128
SUBLANES = 8
VMEM_LIMIT = 56 * 1024 * 1024

C_QKV, C_Z, C_BQ, C_BF, C_BI, C_BG, C_GA, C_GB, C_MISC = 0, 1536, 2048, 2560, 3072, 3584, 4096, 5120, 6144
PROJ_W = 6272
REF_W = 6152

ADAM_LR, ADAM_B1, ADAM_B2, ADAM_EPS, ADAM_WD, ADAM_STEP = 0.001, 0.9, 0.999, 1e-08, 0.01, 10

MESH = pl.DeviceIdType.MESH
SDS = jax.ShapeDtypeStruct
BS = pl.BlockSpec


def _params(n_axes):
    return pltpu.CompilerParams(dimension_semantics=("arbitrary",) * n_axes, vmem_limit_bytes=VMEM_LIMIT)


def _pick(n, cands):
    for c in cands:
        if n % c == 0:
            return c
    raise ValueError(f"no tile for {n} among {cands}")


def _iota2(shape, dim):
    return lax.broadcasted_iota(jnp.int32, shape, dim)


def _dg(a, b, dims):
    return lax.dot_general(a.astype(BF16), b.astype(BF16), (dims, ((), ())), preferred_element_type=F32)


def _bdg(a, b, ca, cb):
    return lax.dot_general(a.astype(BF16), b.astype(BF16), (((ca,), (cb,)), ((0,), (0,))), preferred_element_type=F32)


@jax.custom_vjp
def _bnn(a, b):
    return _bdg(a, b, 2, 1)


@jax.custom_vjp
def _bnt(a, b):
    return _bdg(a, b, 2, 2)


@jax.custom_vjp
def _btn(a, b):
    return _bdg(a, b, 1, 1)


_bnn.defvjp(lambda a, b: (_bnn(a, b), (a, b)), lambda r, g: (_bnt(g, r[1]), _btn(r[0], g)))
_bnt.defvjp(lambda a, b: (_bnt(a, b), (a, b)), lambda r, g: (_bnn(g, r[1]), _btn(g, r[0])))
_btn.defvjp(lambda a, b: (_btn(a, b), (a, b)), lambda r, g: (_bnt(r[1], g), _bnn(r[0], g)))


def _split2(x):
    hi = x.astype(BF16).astype(F32)
    return hi, x - hi


def _tri(bsz, n):
    return jnp.broadcast_to((_iota2((n, n), 0) >= _iota2((n, n), 1)).astype(F32), (bsz, n, n))


@jax.custom_vjp
def _cumsum_rows(x):
    tri = _tri(x.shape[0], x.shape[1])
    hi, lo = _split2(x)
    return _bdg(tri, hi, 2, 1) + _bdg(tri, lo, 2, 1)


def _cumsum_rows_bwd(_, g):
    tri = _tri(g.shape[0], g.shape[1])
    hi, lo = _split2(g)
    return (_bdg(tri, hi, 1, 1) + _bdg(tri, lo, 1, 1),)


_cumsum_rows.defvjp(lambda x: (_cumsum_rows(x), None), _cumsum_rows_bwd)


def _sigmoid(x):
    return jax.nn.sigmoid(x)


def _silu(x):
    return x * _sigmoid(x)


def _softplus(x):
    return jnp.maximum(x, 0.0) + jnp.log1p(jnp.exp(-jnp.abs(x)))


def _rms(x, w):
    return x * lax.rsqrt(jnp.mean(x * x, axis=-1, keepdims=True) + EPS) * w


@jax.custom_vjp
def _inv_unit_lower(lm):
    n = lm.shape[1]
    a = (_iota2((n, n), 0) == _iota2((n, n), 1)).astype(F32)[None] - lm
    steps = max(1, (n - 1).bit_length()) - 1
    p = _bnn(lm, lm)
    for i in range(steps):
        if i == steps - 1:
            a = a + _bnn(a, p)
        else:
            both = _bnn(jnp.concatenate([a, p], axis=1), p)
            a, p = a + both[:, :n], both[:, n:]
    return a


_inv_unit_lower.defvjp(lambda lm: (lambda a: (a, a))(_inv_unit_lower(lm)),
                       lambda a, g: (-_bnt(_btn(a, g), a),))


def _gdn_chunk(q, k, v, b_b, g_b, s):
    n, dv = q.shape[1], v.shape[2]
    r, c = _iota2((n, n), 0), _iota2((n, n), 1)
    causal, strict, eye = (r >= c)[None], (r > c)[None], (r == c)[None]
    g_cum = _cumsum_rows(g_b)
    g_i = g_cum[:, :, :n]
    g_j = jnp.sum(jnp.where(eye, g_i, 0.0), axis=1, keepdims=True)
    decay = jnp.where(causal, jnp.exp(jnp.where(causal, g_i - g_j, 0.0)), 0.0)
    e_g = jnp.exp(g_cum)
    kb = k * b_b
    kk = _bnt(jnp.concatenate([kb, q], axis=1), k)
    a_inv = _inv_unit_lower(jnp.where(strict, kk[:, :n] * decay, 0.0))
    uw = _bnn(a_inv, jnp.concatenate([v * b_b, kb * e_g], axis=2))
    ws = _bnn(jnp.concatenate([uw[:, :, dv:], q * e_g], axis=1), s)
    v_new = uw[:, :, :dv] - ws[:, :n]
    o = ws[:, n:] + _bnn(kk[:, n:] * decay, v_new)
    g_last = g_cum[:, n - 1:n, :]
    s_new = s * jnp.exp(g_last) +_btn(k * jnp.exp(g_last - g_cum), v_new)
    return o, s_new


@functools.partial(jax.custom_vjp, nondiff_argnums=(1, 2))
def _row(x, j, n):
    return x[:, j:j + 1, :]


def _row_bwd(j, n, _, g):
    return (jnp.where(_iota2((1, n, 1), 1) == j, g, 0.0),)


_row.defvjp(lambda x, j, n: (_row(x, j, n), None), _row_bwd)


def _hgrn_pairs(q, k, v, b_cum):
    n = q.shape[1]
    half = n // 2 if n > SUBLANES else n
    parts = []
    for lo in range(0, n, half):
        qs, bs = q[:, lo:], b_cum[:, lo:]
        rows = _iota2((1, n - lo, 1), 1) + lo
        acc = jnp.zeros_like(qs)
        for j in range(lo, lo + half):
            p = jnp.exp(jnp.where(rows >= j, bs - _row(b_cum, j, n), -1e30))
            acc = acc + jnp.sum(qs * _row(k, j, n) * p, axis=2, keepdims=True) * _row(v, j, n)
        parts.append(acc)
    if len(parts) == 1:
        return parts[0]
    return parts[0] + jnp.concatenate([jnp.zeros_like(parts[1]), parts[1]], axis=1)


def _hgrn_block(q, k, v, lf, st, group=HGRN_CHUNK):
    n, rows = group, q.shape[1]
    b_cum = _cumsum_rows(lf)
    outs = []
    for c in range(rows // n):
        rs = slice(c * n, (c + 1) * n)
        o = _hgrn_pairs(q[:, rs], k[:, rs], v[:, rs], b_cum[:, rs])
        if c:
            b_c = _row(b_cum, c * n - 1, rows)
            scores = _bnt(q[:, rs] * jnp.exp(b_cum[:, rs] - b_c), k[:, :c * n] * jnp.exp(b_c - b_cum[:, :c * n]))
            o = o + _bnn(scores, v[:, :c * n])
        outs.append(o)
    b_last = _row(b_cum, rows - 1, rows)
    o = _bnt(q * jnp.exp(b_cum), st) + jnp.concatenate(outs, axis=1)
    return o, st * jnp.exp(b_last) + _btn(v, k * jnp.exp(b_last - b_cum))


def _l2n_act(y, scale):
    a = _silu(y)
    return a * lax.rsqrt(jnp.sum(a * a, axis=-1, keepdims=True) + EPS) * scale


def _col(x, lane):
    return jnp.sum(jnp.where(_iota2(x.shape, 1) == lane, x, 0.0), axis=1, keepdims=True)


def _elem(x, row, lane):
    m = (_iota2(x.shape, 0) == row) & (_iota2(x.shape, 1) == lane)
    return jnp.sum(jnp.sum(jnp.where(m, x, 0.0), axis=1, keepdims=True), axis=0, keepdims=True)


def _gdn_gates(misc, aux, real, head):
    beta = _sigmoid(_col(misc, head))
    g = -jnp.exp(_elem(aux, 0, head)) * _softplus(_col(misc, N_HEADS + head) + _elem(aux, 1, head))
    g = jnp.where(real, g, 0.0)
    shape = (misc.shape[0], D_HEAD)
    return jnp.broadcast_to(beta, shape), jnp.broadcast_to(g, shape)


def _hgrn_prep(bq, bf, lb, real):
    qb = _silu(bq) * (D_HEAD ** -0.5)
    log_sig = jnp.minimum(bf, 0.0) - jnp.log1p(jnp.exp(-jnp.abs(bf)))
    pos = lb > 0.0
    lbs = jnp.where(pos, lb, 0.5)
    a = jnp.log(lbs)
    b = jnp.log1p(-lbs) + log_sig
    lae = jnp.maximum(a, b) + jnp.log1p(jnp.exp(-jnp.abs(a - b)))
    lf = jnp.where(pos, lae, log_sig)
    kb = jnp.where(pos, 1.0 - lbs, 1.0) * _sigmoid(-bf)
    return qb, jnp.where(real, kb, 0.0), jnp.where(real, lf, 0.0)


def _gated_norm(o, z, gw):
    return o * lax.rsqrt(jnp.mean(o * o, axis=-1, keepdims=True) + EPS) * gw * _silu(z)


def _shift_down(x, j):
    return x if j == 0 else pltpu.roll(x, j, 0)


def _shift_up(x, j):
    return x if j == 0 else pltpu.roll(x, x.shape[0] - j, 0)


def _all_gather_hbm(blocks, name):
    na = len(blocks)

    def body(*refs):
        x_refs, out_refs = refs[:na], refs[na:2 * na]
        send_sems, recv_sems, local_sems = refs[2 * na:]
        mx, my, mc = lax.axis_index("x"), lax.axis_index("y"), lax.axis_index("c")
        me, sibling = (mx, my, mc), (mx, my, 1 - mc)
        chips = [(1 - mx, my), (mx, 1 - my), (1 - mx, 1 - my)]

        def slab(a, px, py, pc):
            return out_refs[a].at[4 * px + 2 * py + pc]

        def copy(a, k, blk, to, own=False):
            return pltpu.make_async_remote_copy(
                src_ref=x_refs[a] if own else slab(a, *blk), dst_ref=slab(a, *blk),
                send_sem=send_sems.at[7 * a + k], recv_sem=recv_sems.at[7 * a + k], device_id=to, device_id_type=MESH)

        mine = [pltpu.make_async_copy(x_refs[a], slab(a, *me), local_sems.at[a]) for a in range(na)]
        for cp in mine:
            cp.start()
        first = [copy(a, 0, me, sibling, own=True) for a in range(na)]
        first += [copy(a, 1 + j, me, (*chip, mc), own=True) for j, chip in enumerate(chips) for a in range(na)]
        for cp in first:
            cp.start()
        passed = []
        for j, chip in enumerate(chips):
            for a in range(na):
                copy(a, 1 + j, (*chip, mc), me).wait_recv()
                passed.append(copy(a, 4 + j, (*chip, mc), sibling))
                passed[-1].start()
        for a in range(na):
            copy(a, 0, sibling, me).wait_recv()
            for j, chip in enumerate(chips):
                copy(a, 4 + j, (*chip, 1 - mc), me).wait_recv()
        for cp in first + passed:
            cp.wait_send()
        for cp in mine:
            cp.wait()

    hbm = BS(memory_space=pl.ANY)
    return pl.pallas_call(
        body, name=name, out_shape=[SDS((N_DEV, *b.shape), b.dtype) for b in blocks],
        in_specs=[hbm] * na, out_specs=[hbm] * na,
        scratch_shapes=[pltpu.SemaphoreType.DMA((7 * na,)), pltpu.SemaphoreType.DMA((7 * na,)),
                        pltpu.SemaphoreType.DMA((na,))],
    )(*blocks)


def _all_reduce_small(block, name):
    r, c = block.shape

    def body(x_ref, out_ref, buf, send_sems, recv_sems):
        mx, my, mc = lax.axis_index("x"), lax.axis_index("y"), lax.axis_index("c")
        me, sibling = (mx, my, mc), (mx, my, 1 - mc)
        chips = [(1 - mx, my), (mx, 1 - my), (1 - mx, 1 - my)]

        def slab(px, py, pc):
            return buf.at[4 * px + 2 * py + pc]

        def copy(k, blk, to, src=None):
            return pltpu.make_async_remote_copy(
                src_ref=slab(*blk) if src is None else src, dst_ref=slab(*blk),
                send_sem=send_sems.at[k], recv_sem=recv_sems.at[k], device_id=to, device_id_type=MESH)

        first = [copy(0, me, sibling, src=x_ref)]
        first += [copy(1 + j, me, (*chip, mc), src=x_ref) for j, chip in enumerate(chips)]
        for cp in first:
            cp.start()
        passed = [copy(4 + j, (*chip, mc), sibling) for j, chip in enumerate(chips)]
        for j, chip in enumerate(chips):
            copy(1 + j, (*chip, mc), me).wait_recv()
            passed[j].start()
        copy(0, sibling, me).wait_recv()
        for j, chip in enumerate(chips):
            copy(4 + j, (*chip, 1 - mc), me).wait_recv()
        for cp in first + passed:
            cp.wait_send()
        buf[4 * mx + 2 * my + mc] = x_ref[...]
        acc = buf[0]
        for d in range(1, N_DEV):
            acc = acc + buf[d]
        out_ref[...] = acc

    return pl.pallas_call(
        body, name=name, out_shape=SDS((r, c), F32),
        in_specs=[BS(memory_space=pltpu.VMEM)], out_specs=BS(memory_space=pltpu.VMEM),
        scratch_shapes=[pltpu.VMEM((N_DEV, r, c), F32), pltpu.SemaphoreType.DMA((7,)), pltpu.SemaphoreType.DMA((7,))],
    )(block)


HBM_SPEC = BS(memory_space=pltpu.HBM)
SEM_SPEC = BS(memory_space=pltpu.SEMAPHORE)
SIDE_EFFECT = pltpu.SideEffectType.DATAFLOW_SIDE_EFFECTING


def _peer(rel):
    flip = lambda v, bit: 1 - v if bit else v
    return (flip(lax.axis_index("x"), rel >> 2 & 1), flip(lax.axis_index("y"), rel >> 1 & 1),
            flip(lax.axis_index("c"), rel & 1))


def _send_all_start(blocks, scatter, name, after=None):
    na = len(blocks)
    shapes = [b.shape[1:] if scatter else b.shape for b in blocks]
    n_in = 2 * na + (after is not None)

    def body(*refs):
        srcs, lands = refs[:na], refs[na:2 * na]
        send_sems, recv_sems, token = refs[n_in], refs[n_in + 1], refs[-1]
        me = 4 * lax.axis_index("x") + 2 * lax.axis_index("y") + lax.axis_index("c")
        for a in range(na):
            for rel in range(1, N_DEV):
                px, py, pc = _peer(rel)
                pltpu.make_async_remote_copy(
                    src_ref=srcs[a].at[4 * px + 2 * py + pc] if scatter else srcs[a], dst_ref=lands[a].at[me],
                    send_sem=send_sems.at[7 * a + rel - 1], recv_sem=recv_sems.at[7 * a + rel - 1],
                    device_id=(px, py, pc), device_id_type=MESH).start()
        token[...] = jnp.zeros_like(token)

    lands = [lax.empty((N_DEV, *s), b.dtype) for s, b in zip(shapes, blocks)]
    res = pl.pallas_call(
        body, name=name,
        out_shape=([pltpu.SemaphoreType.DMA((7 * na,)), pltpu.SemaphoreType.DMA((7 * na,))]
                   + [pltpu.HBM(b.shape, b.dtype) for b in blocks] + [pltpu.HBM(ld.shape, ld.dtype) for ld in lands]
                   + [SDS((SUBLANES, LANES), F32)]),
        in_specs=[HBM_SPEC] * (2 * na) + [BS(memory_space=pl.ANY)] * (after is not None),
        out_specs=[SEM_SPEC, SEM_SPEC] + [HBM_SPEC] * (2 * na) + [BS(memory_space=pltpu.VMEM)],
        input_output_aliases={i: 2 + i for i in range(2 * na)},
        compiler_params=pltpu.CompilerParams(has_side_effects=SIDE_EFFECT),
    )(*[pltpu.with_memory_space_constraint(b, pltpu.HBM) for b in blocks],
      *[pltpu.with_memory_space_constraint(ld, pltpu.HBM) for ld in lands], *([] if after is None else [after]))
    return dict(send=res[0], recv=res[1], srcs=res[2:2 + na], lands=res[2 + na:2 + 2 * na], scatter=scatter), res[-1]


def _send_all_wait(flight, after, name):
    na = len(flight["srcs"])

    def body(*refs):
        srcs, lands = refs[:na], refs[na:2 * na]
        send_sems, recv_sems = refs[2 * na], refs[2 * na + 1]
        for a in range(na):
            for rel in range(1, N_DEV):
                cp = pltpu.make_async_remote_copy(
                    src_ref=srcs[a].at[0] if flight["scatter"] else srcs[a], dst_ref=lands[a].at[0],
                    send_sem=send_sems.at[7 * a + rel - 1], recv_sem=recv_sems.at[7 * a + rel - 1],
                    device_id=_peer(rel), device_id_type=MESH)
                cp.wait_send()
                cp.wait_recv()

    arrays = list(flight["srcs"]) + list(flight["lands"])
    res = pl.pallas_call(
        body, name=name, out_shape=[pltpu.HBM(a.shape, a.dtype) for a in arrays],
        in_specs=[HBM_SPEC] * (2 * na) + [SEM_SPEC, SEM_SPEC, BS(memory_space=pl.ANY)], out_specs=[HBM_SPEC] * (2 * na),
        input_output_aliases={i: i for i in range(2 * na)},
        compiler_params=pltpu.CompilerParams(has_side_effects=SIDE_EFFECT),
    )(*arrays, flight["send"], flight["recv"], after)
    return res[:na], res[na:]


def _sum_slabs(land, name):
    _, r, c = land.shape
    tr = _pick(r, (256, 128, 64, 32, 16, 8))

    def body(l_ref, o_ref):
        acc = l_ref[0].astype(F32)
        for d in range(1, N_DEV):
            acc = acc + l_ref[d].astype(F32)
        o_ref[...] = acc

    return pl.pallas_call(
        body, name=name, grid=(r // tr,), out_shape=SDS((r, c), F32),
        in_specs=[BS((N_DEV, tr, c), lambda j: (0, j, 0))], out_specs=BS((tr, c), lambda j: (j, 0)),
        compiler_params=_params(1),
    )(land)


def _proj_fwd(h, nw8, wp, tag):
    n = h.shape[0]
    tm = _pick(n, (1408, 768, 512, 384, 256, 192, 128, 64))
    tn = 896

    def body(h_ref, nw_ref, w_ref, proj_ref, xn_ref):
        @pl.when(pl.program_id(1) == 0)
        def _():
            xn_ref[...] = _rms(h_ref[...], nw_ref[0:1, :]).astype(BF16)

        proj_ref[...] = jnp.dot(xn_ref[...], w_ref[...], preferred_element_type=F32)

    return pl.pallas_call(
        body, name=f"proj_fwd_{tag}", grid=(n // tm, PROJ_W // tn),
        in_specs=[BS((tm, D_MODEL), lambda i, j: (i, 0)), BS((SUBLANES, D_MODEL), lambda i, j: (0, 0)),
                  BS((D_MODEL, tn), lambda i, j: (0, j))],
        out_specs=[BS((tm, tn), lambda i, j: (i, j)), BS((tm, D_MODEL), lambda i, j: (i, 0))],
        out_shape=[SDS((n, PROJ_W), F32), SDS((n, D_MODEL), BF16)], compiler_params=_params(2),
    )(h, nw8, wp)


def _conv_ext(x_ext, cw_ref):
    y = x_ext * cw_ref[3:4, :]
    for k in range(3):
        y = y + _shift_down(x_ext, 3 - k) * cw_ref[k:k + 1, :]
    return y[SUBLANES:]


def _prep_fwd(proj, cw8, aux, lb8, nseq, t_len, tag):
    n = proj.shape[0]
    tt = _pick(t_len, (192, 128, 64))
    nt_ = t_len // tt
    qkv_w = 3 * HEADS_W

    def body(cur_ref, prev_ref, misc_ref, bq_ref, bf_ref, cw_ref, aux_ref, lb_ref,
             q_ref, k_ref, v_ref, b_ref, g_ref, qb_ref, kb_ref, lf_ref, ext_ref):
        t = pl.program_id(1)
        ext_ref[0:SUBLANES, :] = jnp.where(t == 0, 0.0, prev_ref[...])
        ext_ref[SUBLANES:, :] = cur_ref[...]
        y = ext_ref[SUBLANES:, :] * cw_ref[3:4, :]
        for kk in range(3):
            y = y + ext_ref[SUBLANES - 3 + kk:SUBLANES - 3 + kk + tt, :] * cw_ref[kk:kk + 1, :]
        real = (t * tt + _iota2((tt, 1), 0)) >= N_PAD
        misc = misc_ref[...]
        auxv = aux_ref[...]
        for hd in range(N_HEADS):
            sl = slice(hd * D_HEAD, (hd + 1) * D_HEAD)
            q_ref[:, sl] = _l2n_act(y[:, sl], D_HEAD ** -0.5)
            k_ref[:, sl] = _l2n_act(y[:, HEADS_W + hd * D_HEAD:HEADS_W + (hd + 1) * D_HEAD], 1.0)
            v_ref[:, sl] = _silu(y[:, 2 * HEADS_W + hd * D_HEAD:2 * HEADS_W + (hd + 1) * D_HEAD])
            b_ref[:, sl], g_ref[:, sl] = _gdn_gates(misc, auxv, real, hd)
        qb_ref[...], kb_ref[...], lf_ref[...] = _hgrn_prep(bq_ref[...], bf_ref[...], lb_ref[0:1, :], real)

    rb = tt // SUBLANES
    row = lambda s, t: s * nt_ + t
    wide = BS((tt, HEADS_W), lambda s, t: (row(s, t), 0))
    return pl.pallas_call(
        body, name=f"prep_fwd_{tag}", grid=(nseq, nt_),
        in_specs=[BS((tt, qkv_w), lambda s, t: (row(s, t), 0)),
                  BS((SUBLANES, qkv_w), lambda s, t: (jnp.maximum(row(s, t) * rb - 1, 0), 0)),
                  BS((tt, LANES), lambda s, t: (row(s, t), C_MISC // LANES)),
                  BS((tt, HEADS_W), lambda s, t: (row(s, t), C_BQ // HEADS_W)),
                  BS((tt, HEADS_W), lambda s, t: (row(s, t), C_BF // HEADS_W)),
                  BS((SUBLANES, qkv_w), lambda s, t: (0, 0)), BS((SUBLANES, LANES), lambda s, t: (0, 0)),
                  BS((SUBLANES, HEADS_W), lambda s, t: (0, 0))],
        out_specs=[wide] * 8, out_shape=[SDS((n, HEADS_W), F32)] * 8,
        scratch_shapes=[pltpu.VMEM((tt + SUBLANES, qkv_w), F32)], compiler_params=_params(2),
    )(proj, proj, proj, proj, proj, cw8, aux, lb8)


GDN_SEQS = 4
HGRN_SEQS = 2


def _seq_block(nseq, most):
    return max(s for s in (1, 2, 4) if s <= most and nseq % s == 0)


def _to_chains(x):
    return jnp.concatenate([x[:, :, hd * D_HEAD:(hd + 1) * D_HEAD] for hd in range(N_HEADS)], axis=0)


def _from_chains(ref, rows, val):
    sb = val.shape[0] // N_HEADS
    for hd in range(N_HEADS):
        ref[:, rows, hd * D_HEAD:(hd + 1) * D_HEAD] = val[hd * sb:(hd + 1) * sb].astype(ref.dtype)


def _mixers_fwd(q, k, v, b, g, qb, kb, vb, vb_col, lf, nseq, t_len, tag):
    sb, hs = _seq_block(nseq, GDN_SEQS), _seq_block(nseq, HGRN_SEQS)
    nc = t_len // GDN_CHUNK
    chains = N_HEADS * sb

    def body(q_ref, k_ref, v_ref, b_ref, g_ref, qb_ref, kb_ref, vb_ref, lf_ref, oa_ref, ob_ref, cka_ref, ckb_ref,
             sa_ref, sb_ref):
        @pl.when(pl.program_id(1) == 0)
        def _():
            sa_ref[...] = jnp.zeros_like(sa_ref)
            sb_ref[...] = jnp.zeros_like(sb_ref)

        s = sa_ref[...]
        cka_ref[...] = s
        o, s_new = _gdn_chunk(*[_to_chains(r[...]) for r in (q_ref, k_ref, v_ref, b_ref, g_ref)], s)
        _from_chains(oa_ref, slice(None), o)
        sa_ref[...] = s_new
        for part in range(sb // hs):
            seqs, ch = slice(part * hs, (part + 1) * hs), slice(part * N_HEADS * hs, (part + 1) * N_HEADS * hs)
            s = sb_ref[ch]
            ckb_ref[ch] = s
            o, s_new = _hgrn_block(*[_to_chains(r[seqs]) for r in (qb_ref, kb_ref, vb_ref, lf_ref)], s)
            for hd in range(N_HEADS):
                ob_ref[seqs, :, hd * D_HEAD:(hd + 1) * D_HEAD] = o[hd * hs:(hd + 1) * hs]
            sb_ref[ch] = s_new

    blk = lambda cb: BS((sb, GDN_CHUNK, HEADS_W), lambda p, c: (p, c, cb))
    ck_spec = BS((None, None, chains, D_HEAD, D_HEAD), lambda p, c: (p, c, 0, 0, 0))
    ck_shape = SDS((nseq // sb, nc, chains, D_HEAD, D_HEAD), F32)
    view = lambda a: a.reshape(nseq, t_len, a.shape[1])
    oa, ob, cka, ckb = pl.pallas_call(
        body, name=f"mixers_fwd_{tag}", grid=(nseq // sb, nc),
        in_specs=[blk(0)] * 7 + [blk(vb_col), blk(0)], out_specs=[blk(0), blk(0), ck_spec, ck_spec],
        out_shape=[SDS((nseq, t_len, HEADS_W), F32)] * 2 + [ck_shape] * 2,
        scratch_shapes=[pltpu.VMEM((chains, D_HEAD, D_HEAD), F32)] * 2, compiler_params=_params(2),
    )(*[view(a) for a in (q, k, v, b, g, qb, kb, vb, lf)])
    return oa.reshape(-1, HEADS_W), ob.reshape(-1, HEADS_W), cka, ckb


def _mixers_bwd(q, k, v, b, g, qb, kb, vb, vb_col, lf, cka, ckb, doa, dob, nseq, t_len, tag):
    sb, hs = _seq_block(nseq, GDN_SEQS), _seq_block(nseq, HGRN_SEQS)
    nc = t_len // GDN_CHUNK
    chains = N_HEADS * sb

    def body(q_ref, k_ref, v_ref, b_ref, g_ref, qb_ref, kb_ref, vb_ref, lf_ref, doa_ref, dob_ref, cka_ref, ckb_ref,
             dq_ref, dk_ref, dv_ref, db_ref, dg_ref, dqb_ref, dkb_ref, dvb_ref, dlf_ref, dsa_ref, dsb_ref):
        @pl.when(pl.program_id(1) == 0)
        def _():
            dsa_ref[...] = jnp.zeros_like(dsa_ref)
            dsb_ref[...] = jnp.zeros_like(dsb_ref)

        _, vjp = jax.vjp(_gdn_chunk, *[_to_chains(r[...]) for r in (q_ref, k_ref, v_ref, b_ref, g_ref)], cka_ref[...])
        grads = vjp((_to_chains(doa_ref[...]), dsa_ref[...]))
        for ref, val in zip((dq_ref, dk_ref, dv_ref, db_ref, dg_ref), grads[:5]):
            _from_chains(ref, slice(None), val)
        dsa_ref[...] = grads[5]
        for part in range(sb // hs):
            seqs, ch = slice(part * hs, (part + 1) * hs), slice(part * N_HEADS * hs, (part + 1) * N_HEADS * hs)
            _, vjp = jax.vjp(functools.partial(_hgrn_block, group=SUBLANES),
                             *[_to_chains(r[seqs]) for r in (qb_ref, kb_ref, vb_ref, lf_ref)], ckb_ref[ch])
            grads = vjp((_to_chains(dob_ref[seqs]), dsb_ref[ch]))
            for ref, val in zip((dqb_ref, dkb_ref, dvb_ref, dlf_ref), grads[:4]):
                for hd in range(N_HEADS):
                    ref[seqs, :, hd * D_HEAD:(hd + 1) * D_HEAD] = val[hd * hs:(hd + 1) * hs].astype(ref.dtype)
            dsb_ref[ch] = grads[4]

    blk = lambda cb: BS((sb, GDN_CHUNK, HEADS_W), lambda p, c: (p, nc - 1 - c, cb))
    ck_spec = BS((None, None, chains, D_HEAD, D_HEAD), lambda p, c: (p, nc - 1 - c, 0, 0, 0))
    view = lambda a: a.reshape(nseq, t_len, a.shape[1])
    dts = [F32] * 7 + [BF16, F32]
    res = pl.pallas_call(
        body, name=f"mixers_bwd_{tag}", grid=(nseq // sb, nc),
        in_specs=[blk(0)] * 7 + [blk(vb_col), blk(0), blk(0), blk(0), ck_spec, ck_spec], out_specs=[blk(0)] * 9,
        out_shape=[SDS((nseq, t_len, HEADS_W), dt) for dt in dts],
        scratch_shapes=[pltpu.VMEM((chains, D_HEAD, D_HEAD), F32)] * 2, compiler_params=_params(2),
    )(*[view(a) for a in (q, k, v, b, g, qb, kb, vb, lf, doa, dob)], cka, ckb)
    return [r.reshape(-1, HEADS_W) for r in res]


def _post_values(oa_ref, ob_ref, z_ref, bg_ref, ga_ref, gb_ref, gn_ref, wa_ref, wb_ref, ya_ref, yb_ref):
    for hd in range(N_HEADS):
        sl = slice(hd * D_HEAD, (hd + 1) * D_HEAD)
        ya_ref[:, sl] = _gated_norm(oa_ref[:, sl], z_ref[:, sl], gn_ref[0:1, :]).astype(BF16)
        yb_ref[:, sl] = _gated_norm(ob_ref[:, sl], bg_ref[:, sl], gn_ref[1:2, :]).astype(BF16)
    pa = jnp.dot(ya_ref[...], wa_ref[...], preferred_element_type=F32)
    pb = jnp.dot(yb_ref[...], wb_ref[...], preferred_element_type=F32)
    return pa, pb, _sigmoid(ga_ref[...]), _sigmoid(gb_ref[...])


def _post_specs(tm):
    r2 = lambda i: (i, 0)
    return [BS((tm, HEADS_W), r2), BS((tm, HEADS_W), r2),
            BS((tm, HEADS_W), lambda i: (i, C_Z // HEADS_W)), BS((tm, HEADS_W), lambda i: (i, C_BG // HEADS_W)),
            BS((tm, D_MODEL), lambda i: (i, C_GA // D_MODEL)), BS((tm, D_MODEL), lambda i: (i, C_GB // D_MODEL)),
            BS((tm, D_MODEL), r2), BS((SUBLANES, LANES), lambda i: (0, 0))]


def _post_fwd(oa, ob, proj, h, gn8, wa, wb, wout, tag):
    n = h.shape[0]
    tm = _pick(n, (256, 192, 128, 64))

    def body(oa_ref, ob_ref, z_ref, bg_ref, ga_ref, gb_ref, h_ref, gn_ref, wa_ref, wb_ref, wout_ref, out_ref,
             ya_ref, yb_ref):
        pa, pb, sa, sb = _post_values(oa_ref, ob_ref, z_ref, bg_ref, ga_ref, gb_ref, gn_ref, wa_ref, wb_ref,
                                      ya_ref, yb_ref)
        mixed = (sa * pa + sb * pb).astype(BF16)
        out_ref[...] = h_ref[...] + jnp.dot(mixed, wout_ref[...], preferred_element_type=F32)

    full = lambda i: (0, 0)
    return pl.pallas_call(
        body, name=f"post_fwd_{tag}", grid=(n // tm,),
        in_specs=_post_specs(tm) + [BS((HEADS_W, D_MODEL), full), BS((HEADS_W, D_MODEL), full),
                                    BS((D_MODEL, D_MODEL), full)],
        out_specs=BS((tm, D_MODEL), lambda i: (i, 0)), out_shape=SDS((n, D_MODEL), F32),
        scratch_shapes=[pltpu.VMEM((tm, HEADS_W), BF16), pltpu.VMEM((tm, HEADS_W), BF16)], compiler_params=_params(1),
    )(oa, ob, proj, proj, proj, proj, h, gn8, wa, wb, wout)


def _post_bwd(dh, oa, ob, proj, h, gn8, wa, wb, wout, tag):
    n = h.shape[0]
    tm = _pick(n, (256, 192, 128, 64))

    def body(dh_ref, oa_ref, ob_ref, z_ref, bg_ref, ga_ref, gb_ref, h_ref, gn_ref, wa_ref, wb_ref, wout_ref,
             doa_ref, dob_ref, dz_ref, dbg_ref, dga_ref, dgb_ref, dwa_ref, dwb_ref, dwout_ref, dgn_ref,
             ya_ref, yb_ref):
        @pl.when(pl.program_id(0) == 0)
        def _():
            dwa_ref[...] = jnp.zeros_like(dwa_ref)
            dwb_ref[...] = jnp.zeros_like(dwb_ref)
            dwout_ref[...] = jnp.zeros_like(dwout_ref)
            dgn_ref[...] = jnp.zeros_like(dgn_ref)

        pa, pb, sa, sb = _post_values(oa_ref, ob_ref, z_ref, bg_ref, ga_ref, gb_ref, gn_ref, wa_ref, wb_ref,
                                      ya_ref, yb_ref)
        mixed = (sa * pa + sb * pb).astype(BF16)
        dout = dh_ref[...].astype(BF16)
        dwout_ref[...] += _dg(mixed, dout, ((0,), (0,)))
        dmixed = _dg(dout, wout_ref[...], ((1,), (1,)))
        dga_ref[...] = (dmixed * pa * sa * (1.0 - sa)).astype(BF16)
        dgb_ref[...] = (dmixed * pb * sb * (1.0 - sb)).astype(BF16)
        dpa = (dmixed * sa).astype(BF16)
        dpb = (dmixed * sb).astype(BF16)
        dwa_ref[...] += _dg(ya_ref[...], dpa, ((0,), (0,)))
        dwb_ref[...] += _dg(yb_ref[...], dpb, ((0,), (0,)))
        dya = _dg(dpa, wa_ref[...], ((1,), (1,)))
        dyb = _dg(dpb, wb_ref[...], ((1,), (1,)))
        dgn_a = jnp.zeros((1, D_HEAD), F32)
        dgn_b = jnp.zeros((1, D_HEAD), F32)
        for hd in range(N_HEADS):
            sl = slice(hd * D_HEAD, (hd + 1) * D_HEAD)
            _, vjp = jax.vjp(_gated_norm, oa_ref[:, sl], z_ref[:, sl], gn_ref[0:1, :])
            doa, dz, dgw = vjp(dya[:, sl])
            doa_ref[:, sl], dz_ref[:, sl], dgn_a = doa, dz.astype(BF16), dgn_a + dgw
            _, vjp = jax.vjp(_gated_norm, ob_ref[:, sl], bg_ref[:, sl], gn_ref[1:2, :])
            dob, dbg, dgw = vjp(dyb[:, sl])
            dob_ref[:, sl], dbg_ref[:, sl], dgn_b = dob, dbg.astype(BF16), dgn_b + dgw
        dgn_ref[0:1, :] += dgn_a
        dgn_ref[1:2, :] += dgn_b

    full = lambda i: (0, 0)
    r2 = lambda i: (i, 0)
    return pl.pallas_call(
        body, name=f"post_bwd_{tag}", grid=(n // tm,),
        in_specs=[BS((tm, D_MODEL), r2)] + _post_specs(tm) + [
            BS((HEADS_W, D_MODEL), full), BS((HEADS_W, D_MODEL), full), BS((D_MODEL, D_MODEL), full)],
        out_specs=[BS((tm, HEADS_W), r2)] * 4 + [BS((tm, D_MODEL), r2)] * 2 + [
            BS((HEADS_W, D_MODEL), full), BS((HEADS_W, D_MODEL), full), BS((D_MODEL, D_MODEL), full),
            BS((SUBLANES, LANES), full)],
        out_shape=[SDS((n, HEADS_W), F32), SDS((n, HEADS_W), F32), SDS((n, HEADS_W), BF16), SDS((n, HEADS_W), BF16),
                   SDS((n, D_MODEL), BF16), SDS((n, D_MODEL), BF16), SDS((HEADS_W, D_MODEL), F32),
                   SDS((HEADS_W, D_MODEL), F32), SDS((D_MODEL, D_MODEL), F32), SDS((SUBLANES, LANES), F32)],
        scratch_shapes=[pltpu.VMEM((tm, HEADS_W), BF16), pltpu.VMEM((tm, HEADS_W), BF16)], compiler_params=_params(1),
    )(dh, oa, ob, proj, proj, proj, proj, h, gn8, wa, wb, wout)


def _loss_head(h, fw8, target, nseq, t_len):
    n = h.shape[0]
    nc = t_len // GDN_CHUNK
    sub = 3 if nc % 3 == 0 else 1
    tl, nt = sub * GDN_CHUNK, nc // sub
    inv_d = 1.0 / D_MODEL

    def body(h_ref, fw_ref, *rest):
        tgt_refs, (dh_ref, acc_ref) = rest[:sub], rest[sub:]

        @pl.when((pl.program_id(0) == 0) & (pl.program_id(1) == 0))
        def _():
            acc_ref[...] = jnp.zeros_like(acc_ref)

        frames = ((pl.program_id(1) * tl + _iota2((tl, 1), 0)) >= N_PAD + N_META).astype(F32)
        y, vjp = jax.vjp(_rms, h_ref[...], fw_ref[0:1, :])
        err = (y - jnp.concatenate([r[...] for r in tgt_refs], axis=0)) * frames
        dx, dfw = vjp(err * inv_d)
        dh_ref[...] = dx
        acc_ref[0:1, :] += dfw
        acc_ref[1:2, :] += (0.5 * inv_d) * jnp.sum(err * err, axis=0, keepdims=True)

    tgt_spec = lambda u: BS((None, GDN_CHUNK, D_MODEL), lambda s, t: (s, jnp.maximum(t * sub + u - 1, 0), 0))
    return pl.pallas_call(
        body, name="loss_head", grid=(nseq, nt),
        in_specs=[BS((tl, D_MODEL), lambda s, t: (s * nt + t, 0)), BS((SUBLANES, D_MODEL), lambda s, t: (0, 0))]
        + [tgt_spec(u) for u in range(sub)],
        out_specs=[BS((tl, D_MODEL), lambda s, t: (s * nt + t, 0)), BS((SUBLANES, D_MODEL), lambda s, t: (0, 0))],
        out_shape=[SDS((n, D_MODEL), F32), SDS((SUBLANES, D_MODEL), F32)], compiler_params=_params(2),
    )(h, fw8, *[target] * sub)


def _prep_bwd(proj, dq, dk, dv, db, dg, dqb, dkb, dlf, cw8, aux, lb8, nseq, t_len, tag):
    n = proj.shape[0]
    tt = _pick(t_len, (192, 128, 64))
    nt_ = t_len // tt
    qkv_w = 3 * HEADS_W
    rb = tt // SUBLANES
    ext = tt + SUBLANES

    def body(cur_ref, prev_ref, next_ref, misc_ref, bq_ref, bf_ref, dq_ref, dqn_ref, dk_ref, dkn_ref, dv_ref, dvn_ref,
             db_ref, dg_ref, dqb_ref, dkb_ref, dlf_ref, cw_ref, aux_ref, lb_ref,
             dqkv_ref, dmisc_ref, dbq_ref, dbf_ref, dcw_ref, daux_ref, dlb_ref, dy_ref):
        s, t = pl.program_id(0), pl.program_id(1)

        @pl.when((s == 0) & (t == 0))
        def _():
            dcw_ref[...] = jnp.zeros_like(dcw_ref)
            daux_ref[...] = jnp.zeros_like(daux_ref)
            dlb_ref[...] = jnp.zeros_like(dlb_ref)

        prev = jnp.where(t == 0, 0.0, prev_ref[...])
        x_ext = jnp.concatenate([prev, cur_ref[...], next_ref[...]], axis=0)
        y = _conv_ext(x_ext, cw_ref)
        inside = (t < nt_ - 1) | (_iota2((ext, 1), 0) < tt)
        dy_ref[0:SUBLANES, :] = jnp.zeros((SUBLANES, qkv_w), F32)
        for hd in range(N_HEADS):
            for grp, (g_ref, gn_ref, scale) in enumerate(((dq_ref, dqn_ref, D_HEAD ** -0.5), (dk_ref, dkn_ref, 1.0),
                                                          (dv_ref, dvn_ref, None))):
                lo = grp * HEADS_W + hd * D_HEAD
                sl = slice(hd * D_HEAD, (hd + 1) * D_HEAD)
                cot = jnp.concatenate([g_ref[:, sl], gn_ref[:, sl]], axis=0)
                fn = _silu if scale is None else functools.partial(_l2n_act, scale=scale)
                _, vjp = jax.vjp(fn, y[:, lo:lo + D_HEAD])
                dy_ref[SUBLANES:, lo:lo + D_HEAD] = jnp.where(inside, vjp(cot)[0], 0.0)
        dy_ext = dy_ref[...]
        dx = dy_ext * cw_ref[3:4, :]
        for kk in range(3):
            dx = dx + _shift_up(dy_ext, 3 - kk) * cw_ref[kk:kk + 1, :]
        dqkv_ref[...] = dx[SUBLANES:SUBLANES + tt].astype(BF16)
        dy_cur = dy_ext[SUBLANES:SUBLANES + tt]
        for kk in range(4):
            xs = _shift_down(x_ext, 3 - kk)[SUBLANES:SUBLANES + tt]
            dcw_ref[kk:kk + 1, :] += jnp.sum(xs * dy_cur, axis=0, keepdims=True)

        real = (t * tt + _iota2((tt, 1), 0)) >= N_PAD
        dmisc = jnp.zeros((tt, LANES), F32)
        daux = jnp.zeros((SUBLANES, LANES), F32)
        for hd in range(N_HEADS):
            sl = slice(hd * D_HEAD, (hd + 1) * D_HEAD)
            _, vjp = jax.vjp(lambda m, a: _gdn_gates(m, a, real, hd), misc_ref[...], aux_ref[...])
            dm, da = vjp((db_ref[:, sl], dg_ref[:, sl]))
            dmisc, daux = dmisc + dm, daux + da
        dmisc_ref[...] = dmisc.astype(BF16)
        daux_ref[...] += daux
        _, vjp = jax.vjp(lambda a, b, c: _hgrn_prep(a, b, c, real), bq_ref[...], bf_ref[...], lb_ref[0:1, :])
        dbq, dbf, dlb = vjp((dqb_ref[...], dkb_ref[...], dlf_ref[...]))
        dbq_ref[...], dbf_ref[...] = dbq.astype(BF16), dbf.astype(BF16)
        dlb_ref[0:1, :] += dlb

    row = lambda s, t: s * nt_ + t
    cur = lambda s, t: (row(s, t), 0)
    nxt = lambda s, t: (jnp.minimum((row(s, t) + 1) * rb, n // SUBLANES - 1), 0)
    wide = BS((tt, HEADS_W), cur)
    halo = BS((SUBLANES, HEADS_W), nxt)
    full = lambda s, t: (0, 0)
    return pl.pallas_call(
        body, name=f"prep_bwd_{tag}", grid=(nseq, nt_),
        in_specs=[BS((tt, qkv_w), cur), BS((SUBLANES, qkv_w), lambda s, t: (jnp.maximum(row(s, t) * rb - 1, 0), 0)),
                  BS((SUBLANES, qkv_w), nxt), BS((tt, LANES), lambda s, t: (row(s, t), C_MISC // LANES)),
                  BS((tt, HEADS_W), lambda s, t: (row(s, t), C_BQ // HEADS_W)),
                  BS((tt, HEADS_W), lambda s, t: (row(s, t), C_BF // HEADS_W)),
                  wide, halo, wide, halo, wide, halo, wide, wide, wide, wide, wide,
                  BS((SUBLANES, qkv_w), full), BS((SUBLANES, LANES), full), BS((SUBLANES, HEADS_W), full)],
        out_specs=[BS((tt, qkv_w), cur), BS((tt, LANES), cur), wide, wide,
                   BS((SUBLANES, qkv_w), full), BS((SUBLANES, LANES), full), BS((SUBLANES, HEADS_W), full)],
        out_shape=[SDS((n, qkv_w), BF16), SDS((n, LANES), BF16), SDS((n, HEADS_W), BF16), SDS((n, HEADS_W), BF16),
                   SDS((SUBLANES, qkv_w), F32), SDS((SUBLANES, LANES), F32), SDS((SUBLANES, HEADS_W), F32)],
        scratch_shapes=[pltpu.VMEM((tt + 2 * SUBLANES, qkv_w), F32)], compiler_params=_params(2),
    )(proj, proj, proj, proj, proj, proj, dq, dq, dk, dk, dv, dv, db, dg, dqb, dkb, dlf, cw8, aux, lb8)


def _proj_bwd_x(pieces, wp, h, nw8, dh_res, tag):
    n = h.shape[0]
    tm = _pick(n, (256, 192, 128, 64))
    widths = [p.shape[1] for p in pieces]
    assert sum(widths) == PROJ_W

    def body(*refs):
        p_refs = refs[:len(pieces)]
        w_ref, h_ref, nw_ref, dres_ref, dh_ref, dnw_ref = refs[len(pieces):]

        @pl.when(pl.program_id(0) == 0)
        def _():
            dnw_ref[...] = jnp.zeros_like(dnw_ref)

        dxn, off = None, 0
        for p_ref, w in zip(p_refs, widths):
            part = _dg(p_ref[...], w_ref[:, off:off + w], ((1,), (1,)))
            dxn = part if dxn is None else dxn + part
            off += w
        _, vjp = jax.vjp(_rms, h_ref[...], nw_ref[0:1, :])
        dx, dnw = vjp(dxn)
        dh_ref[...] = dres_ref[...] + dx
        dnw_ref[0:1, :] += dnw

    r2 = lambda i: (i, 0)
    full = lambda i: (0, 0)
    return pl.pallas_call(
        body, name=f"proj_bwd_x_{tag}", grid=(n // tm,),
        in_specs=[BS((tm, w), r2) for w in widths] + [BS((D_MODEL, PROJ_W), full), BS((tm, D_MODEL), r2),
                                                      BS((SUBLANES, D_MODEL), full), BS((tm, D_MODEL), r2)],
        out_specs=[BS((tm, D_MODEL), r2), BS((SUBLANES, D_MODEL), full)],
        out_shape=[SDS((n, D_MODEL), F32), SDS((SUBLANES, D_MODEL), F32)], compiler_params=_params(1),
    )(*pieces, wp, h, nw8, dh_res)


def _proj_bwd_w(xn, pieces, tag):
    n = xn.shape[0]
    tm = _pick(n, (384, 256, 192, 128, 64))
    widths = [p.shape[1] for p in pieces]
    assert sum(widths) == PROJ_W

    def body(*refs):
        x_ref, p_refs = refs[0], refs[1:1 + len(pieces)]
        o_ref, acc_ref = refs[1 + len(pieces):]

        @pl.when(pl.program_id(0) == 0)
        def _():
            acc_ref[...] = jnp.zeros_like(acc_ref)

        off = 0
        for p_ref, w in zip(p_refs, widths):
            acc_ref[:, off:off + w] += _dg(x_ref[...], p_ref[...], ((0,), (0,)))
            off += w

        @pl.when(pl.program_id(0) == pl.num_programs(0) - 1)
        def _():
            pltpu.sync_copy(acc_ref, o_ref)

    r2 = lambda i: (i, 0)
    return pl.pallas_call(
        body, name=f"proj_bwd_w_{tag}", grid=(n // tm,),
        in_specs=[BS((tm, D_MODEL), r2)] + [BS((tm, w), r2) for w in widths], out_specs=BS(memory_space=pl.ANY),
        out_shape=SDS((D_MODEL, PROJ_W), F32), scratch_shapes=[pltpu.VMEM((D_MODEL, PROJ_W), F32)],
        compiler_params=_params(1),
    )(xn, *pieces)


def _adamw(w, g, m, v, name):
    lead, rows, cols = w.shape
    tr = _pick(rows, (256, 128, 64, 32, 16, 8, 4, 2, 1)) if rows > 256 else rows

    def body(w_ref, g_ref, m_ref, v_ref, d_ref, nm_ref, nv_ref):
        gr = g_ref[...]
        m_new = ADAM_B1 * m_ref[...] + (1.0 - ADAM_B1) * gr
        v_new = ADAM_B2 * v_ref[...] + (1.0 - ADAM_B2) * jnp.square(gr)
        m_hat = m_new / (1.0 - ADAM_B1 ** ADAM_STEP)
        v_hat = v_new / (1.0 - ADAM_B2 ** ADAM_STEP)
        d_ref[...] = -ADAM_LR * (m_hat / (jnp.sqrt(v_hat) + ADAM_EPS) + ADAM_WD * w_ref[...])
        nm_ref[...] = m_new
        nv_ref[...] = v_new

    blk = BS((None, tr, cols), lambda a, i: (a, i, 0))
    return pl.pallas_call(
        body, name=name, grid=(lead, rows // tr), in_specs=[blk] * 4, out_specs=[blk] * 3,
        out_shape=[SDS((lead, rows, cols), F32)] * 3, compiler_params=_params(2),
    )(w, g, m, v)


def _row8(v, width):
    v = jnp.atleast_2d(v).astype(F32)
    return jnp.pad(v, ((0, SUBLANES - v.shape[0]), (0, width - v.shape[1])))


REF_MISC = 1536
N_MISC = 2 * N_HEADS
LAYOUT_RUNS = ((0, REF_MISC, 0), (REF_MISC + N_MISC, REF_W, REF_MISC), (REF_MISC, REF_MISC + N_MISC, C_MISC))


def _to_layout(slabs, tag):
    n_slabs, rows, width = slabs.shape
    tr = _pick(rows, (256, 128, 64, 32, 16))

    def body(x_ref, o_ref):
        off = 0
        for lo, hi, _ in sorted(LAYOUT_RUNS, key=lambda run: run[2]):
            for j in range(n_slabs):
                a, b = max(lo, j * width), min(hi, (j + 1) * width)
                if a < b:
                    o_ref[:, off:off + b - a] = x_ref[j, :, a - j * width:b - j * width]
                    off += b - a
        o_ref[:, off:] = jnp.zeros((tr, PROJ_W - off), o_ref.dtype)

    return pl.pallas_call(
        body, name=f"weights_layout_{tag}", grid=(rows // tr,), out_shape=SDS((rows, PROJ_W), slabs.dtype),
        in_specs=[BS((n_slabs, tr, width), lambda i: (0, i, 0))], out_specs=BS((tr, PROJ_W), lambda i: (i, 0)),
        compiler_params=_params(1),
    )(slabs)


def _from_layout(dw, n_slabs):
    width = REF_W // n_slabs
    slabs = []
    for j in range(n_slabs):
        pieces = []
        for lo, hi, at in sorted(LAYOUT_RUNS):
            a, b = max(lo, j * width), min(hi, (j + 1) * width)
            if a < b:
                pieces.append(dw[:, at + a - lo:at + b - lo])
        slabs.append(jnp.concatenate(pieces, axis=1))
    return slabs


def _lower_bounds(lb):
    sm = jax.nn.softmax(lb.astype(F32), axis=0)
    return jnp.cumsum(sm, axis=0) - sm[0]


def kernel(x, meta_tokens, norm_w, w_in, conv_w, a_log, dt_bias, gnorm_a, gnorm_b, hgrn_lower_bounds, w_branch_a, w_branch_b, w_out, final_norm_w, loss_target, m_meta_tokens, m_norm_w, m_w_in, m_conv_w, m_a_log, m_dt_bias, m_gnorm_a, m_gnorm_b, m_hgrn_lower_bounds, m_w_branch_a, m_w_branch_b, m_w_out, m_final_norm_w, v_meta_tokens, v_norm_w, v_w_in, v_conv_w, v_a_log, v_dt_bias, v_gnorm_a, v_gnorm_b, v_hgrn_lower_bounds, v_w_branch_a, v_w_branch_b, v_w_out, v_final_norm_w):
    nseq, seq, _ = x.shape
    depth = norm_w.shape[0]
    t_len = N_PAD + N_META + seq
    n = nseq * t_len
    conv_c = conv_w.shape[2]
    my = 4 * lax.axis_index("x") + 2 * lax.axis_index("y") + lax.axis_index("c")

    assert depth >= 2
    by_cols = lambda g: g.transpose(1, 2, 0, 3).reshape(g.shape[1], g.shape[2], N_DEV * g.shape[3])
    first = _all_gather_hbm([w_in[:1].astype(BF16), conv_w, meta_tokens], "gather_first")
    later_flight, later_token = _send_all_start(
        [w_in[1:].astype(BF16), w_branch_a.astype(BF16), w_branch_b.astype(BF16), w_out.astype(BF16)], False,
        "gather_later_start", after=first[0])
    w_in_slabs = [first[0]]
    conv_full = by_cols(first[1])
    meta_full = first[2].transpose(1, 0, 2).reshape(N_META, D_MODEL)

    lb_all, lb_vjp = jax.vjp(_lower_bounds, hgrn_lower_bounds)

    h = jnp.concatenate([jnp.zeros((nseq, N_PAD, D_MODEL), F32),
                         jnp.broadcast_to(meta_full[None], (nseq, N_META, D_MODEL)), x], axis=1).reshape(n, D_MODEL)
    saved = []
    for l in range(depth):
        wp = _to_layout(w_in_slabs[0][:, 0] if l == 0 else w_in_slabs[1][:, l - 1], l)
        nw8 = _row8(norm_w[l], D_MODEL)
        if l == 0:
            nw8 = nw8 + later_token[0:1, 0:1]
        cw8 = _row8(conv_full[l], 3 * HEADS_W)
        aux = _row8(jnp.stack([a_log[l], dt_bias[l]]), LANES)
        lb8 = _row8(lb_all[l], HEADS_W)
        gn8 = _row8(jnp.stack([gnorm_a[l], gnorm_b[l]]), LANES)
        proj, xn = _proj_fwd(h, nw8, wp, l)
        q, k, v, b, g, qb, kb, lf = _prep_fwd(proj, cw8, aux, lb8, nseq, t_len, l)
        oa, ob, sck_a, sck_b = _mixers_fwd(q, k, v, b, g, qb, kb, proj, C_BI // HEADS_W, lf, nseq, t_len, l)
        if l == 0:
            sent, landed = _send_all_wait(later_flight, ob, "gather_later_wait")
            landed = [lax.dynamic_update_slice(ld, own[None], (my,) + (0,) * own.ndim) for ld, own in zip(landed, sent)]
            w_in_slabs.append(landed[0])
            wa_full, wb_full = by_cols(landed[1]), by_cols(landed[2])
            wout_full = landed[3].transpose(1, 0, 2, 3).reshape(depth, D_MODEL, D_MODEL)
        wa_l, wb_l, wout_l = wa_full[l], wb_full[l], wout_full[l]
        h_next = _post_fwd(oa, ob, proj, h, gn8, wa_l, wb_l, wout_l, l)
        saved.append(dict(h=h, wp=wp, nw8=nw8, cw8=cw8, aux=aux, lb8=lb8, gn8=gn8, proj=proj, xn=xn, q=q, k=k, v=v, b=b,
                          wa=wa_l, wb=wb_l, wout=wout_l,
                          g=g, qb=qb, kb=kb, lf=lf, oa=oa, ob=ob, sck_a=sck_a, sck_b=sck_b))
        h = h_next

    dh, acc = _loss_head(h, _row8(final_norm_w, D_MODEL), loss_target, nseq, t_len)

    g_win, g_wa, g_wb, g_wout, g_conv, small = [], [], [], [], [], []

    def mixer_slabs(dwa_s, dwb_s, dwout_s):
        nl = len(dwa_s)
        rows = lambda a: jnp.stack(a).reshape(nl * HEADS_W, N_DEV, LANES).transpose(1, 0, 2)
        wout = jnp.stack(dwout_s).reshape(nl, N_DEV, LANES, D_MODEL).transpose(1, 0, 2, 3)
        return [jnp.concatenate([rows(dwa_s), rows(dwb_s)], axis=1).astype(BF16),
                wout.reshape(N_DEV, nl * LANES, D_MODEL).astype(BF16)]

    def win_slabs(per_layer, dtype):
        return jnp.stack([jnp.concatenate([sl[j] for sl in per_layer], axis=0) for j in range(N_DEV)]).astype(dtype)

    for l in reversed(range(depth)):
        s = saved[l]
        gn8, aux = s["gn8"], s["aux"]
        if l == 0:
            later_flight, later_token = _send_all_start(
                [win_slabs(g_win[::-1], BF16)] + mixer_slabs(g_wa[::-1], g_wb[::-1], g_wout[::-1]), True,
                "scatter_later_start")
            gn8 = gn8 + later_token[0:1, 0:1]
        doa, dob, dz, dbg, dga, dgb, dwa, dwb, dwout, dgn = _post_bwd(
            dh, s["oa"], s["ob"], s["proj"], s["h"], gn8, s["wa"], s["wb"], s["wout"], l)
        dq, dk, dv, db, dg, dqb, dkb, dbi, dlf = _mixers_bwd(
            s["q"], s["k"], s["v"], s["b"], s["g"], s["qb"], s["kb"], s["proj"], C_BI // HEADS_W, s["lf"], s["sck_a"],
            s["sck_b"], doa, dob, nseq, t_len, l)
        if l == 0:
            mixer_flight, mixer_token = _send_all_start(mixer_slabs([dwa], [dwb], [dwout]), True, "scatter_first_start")
            aux = aux + mixer_token[0:1, 0:1]
        dqkv, dmisc, dbq, dbf, dcw, daux, dlb = _prep_bwd(s["proj"], dq, dk, dv, db, dg, dqb, dkb, dlf, s["cw8"], aux,
                                                          s["lb8"], nseq, t_len, l)
        pieces = [dqkv, dz, dbq, dbf, dbi, dbg, dga, dgb, dmisc]
        g_win.append(_from_layout(_proj_bwd_w(s["xn"], pieces, l), N_DEV))
        g_conv.append(dcw[:4])
        nw8 = s["nw8"]
        if l == 0:
            dconv = jnp.stack(g_conv[::-1])
            conv_slabs = dconv.reshape(depth * dconv.shape[1], N_DEV, conv_c).transpose(1, 0, 2)
            win_flight, win_token = _send_all_start([win_slabs(g_win[-1:], BF16), conv_slabs], True, "scatter_win_start")
            nw8 = nw8 + win_token[0:1, 0:1]
        dh, dnw = _proj_bwd_x(pieces, s["wp"], s["h"], nw8, dh, l)
        g_wa.append(dwa)
        g_wb.append(dwb)
        g_wout.append(dwout)
        small.append((dnw[0], dgn[0], dgn[1], daux[0, :N_HEADS], daux[1, :N_HEADS], dlb[0]))
    small.reverse()
    dh = dh.reshape(nseq, t_len, D_MODEL)
    grad_x = dh[:, N_PAD + N_META:]

    packed = jnp.concatenate([small[0][1], small[1][1], small[0][2], small[1][2], small[0][3], small[1][3],
                              small[0][4], small[1][4]])
    tile = jnp.concatenate([
        jnp.sum(dh[:, N_PAD:N_PAD + N_META], axis=0), _row8(jnp.stack([small[0][0], small[1][0], acc[0]]), D_MODEL),
        _row8(jnp.stack([small[0][5], small[1][5]]), D_MODEL), _row8(packed, D_MODEL), _row8(acc[1], D_MODEL)], axis=0)
    tile = _all_reduce_small(tile, "reduce_small")
    loss = jnp.sum(tile[40])
    g_meta = lax.dynamic_slice_in_dim(tile[0:N_META], my * LANES, LANES, axis=1)
    g_norm, g_final = tile[16:18], tile[18]
    (g_lb,) = lb_vjp(tile[24:26, :HEADS_W])
    r21 = tile[32]
    g_gna, g_gnb = r21[0:256].reshape(2, LANES), r21[256:512].reshape(2, LANES)
    g_alog, g_dtb = r21[512:520].reshape(2, N_HEADS), r21[520:528].reshape(2, N_HEADS)

    def landed_sums(flight, tag):
        sent, landed = _send_all_wait(flight, dh, f"{tag}_wait")
        landed = [lax.dynamic_update_slice(ld, lax.dynamic_index_in_dim(src, my, 0, keepdims=True), (my, 0, 0))
                  for ld, src in zip(landed, sent)]
        return [_sum_slabs(ld, f"{tag}_sum{i}") for i, ld in enumerate(landed)]

    l_win, l_ab, l_wout = landed_sums(later_flight, "scatter_later")
    r_ab, r_wout = landed_sums(mixer_flight, "scatter_first")
    r_win, r_conv = landed_sums(win_flight, "scatter_win")
    both = lambda a, b, shape: jnp.concatenate([a.reshape(1, *shape[1:]), b.reshape(depth - 1, *shape[1:])])
    half, half_l = HEADS_W, (depth - 1) * HEADS_W
    mine = [both(r_win, l_win, w_in.shape), both(r_ab[:half], l_ab[:half_l], w_branch_a.shape),
            both(r_ab[half:], l_ab[half_l:], w_branch_b.shape), both(r_wout, l_wout, w_out.shape), r_conv]
    gseg = lambda i, shape: mine[i].reshape(shape)
    grads = {
        "meta_tokens": g_meta, "norm_w": g_norm, "w_in": gseg(0, w_in.shape), "conv_w": gseg(4, conv_w.shape),
        "a_log": g_alog, "dt_bias": g_dtb, "gnorm_a": g_gna, "gnorm_b": g_gnb, "hgrn_lower_bounds": g_lb,
        "w_branch_a": gseg(1, w_branch_a.shape), "w_branch_b": gseg(2, w_branch_b.shape), "w_out": gseg(3, w_out.shape),
        "final_norm_w": g_final}
    weights = {
        "meta_tokens": (meta_tokens, m_meta_tokens, v_meta_tokens), "norm_w": (norm_w, m_norm_w, v_norm_w),
        "w_in": (w_in, m_w_in, v_w_in), "conv_w": (conv_w, m_conv_w, v_conv_w), "a_log": (a_log, m_a_log, v_a_log),
        "dt_bias": (dt_bias, m_dt_bias, v_dt_bias), "gnorm_a": (gnorm_a, m_gnorm_a, v_gnorm_a),
        "gnorm_b": (gnorm_b, m_gnorm_b, v_gnorm_b),
        "hgrn_lower_bounds": (hgrn_lower_bounds, m_hgrn_lower_bounds, v_hgrn_lower_bounds),
        "w_branch_a": (w_branch_a, m_w_branch_a, v_w_branch_a), "w_branch_b": (w_branch_b, m_w_branch_b, v_w_branch_b),
        "w_out": (w_out, m_w_out, v_w_out), "final_norm_w": (final_norm_w, m_final_norm_w, v_final_norm_w)}
    names = list(weights)
    deltas, new_m, new_v = [], [], []
    for nm in names:
        w, m, v = weights[nm]
        view = (1,) * (3 - w.ndim) + w.shape
        d, m2, v2 = _adamw(w.reshape(view), grads[nm].reshape(view), m.reshape(view), v.reshape(view), f"adamw_{nm}")
        deltas.append(d.reshape(w.shape))
        new_m.append(m2.reshape(w.shape))
        new_v.append(v2.reshape(w.shape))
    return (loss, grad_x, *[grads[nm].reshape(weights[nm][0].shape) for nm in names], *deltas, *new_m, *new_v)
```

```python
import functools

import jax
import jax.numpy as jnp
from jax import lax
from jax.experimental import pallas as pl
from jax.experimental.pallas import tpu as pltpu

F32 = jnp.float32
BF16 = jnp.bfloat16

D_MODEL = 1024
N_HEADS = 4
D_HEAD = 128
HEADS_W = N_HEADS * D_HEAD
N_META = 16
N_PAD = 48
GDN_CHUNK = 64
HGRN_CHUNK = 16
EPS = 1e-6
N_DEV = 8
LANES = 128
SUBLANES = 8
VMEM_LIMIT = 56 * 1024 * 1024

C_QKV, C_Z, C_BQ, C_BF, C_BI, C_BG, C_GA, C_GB, C_MISC = 0, 1536, 2048, 2560, 3072, 3584, 4096, 5120, 6144
PROJ_W = 6272
REF_W = 6152

ADAM_LR, ADAM_B1, ADAM_B2, ADAM_EPS, ADAM_WD, ADAM_STEP = 0.001, 0.9, 0.999, 1e-08, 0.01, 10

MESH = pl.DeviceIdType.MESH
SDS = jax.ShapeDtypeStruct
BS = pl.BlockSpec


def _params(n_axes):
    return pltpu.CompilerParams(dimension_semantics=("arbitrary",) * n_axes, vmem_limit_bytes=VMEM_LIMIT)


def _pick(n, cands):
    for c in cands:
        if n % c == 0:
            return c
    raise ValueError(f"no tile for {n} among {cands}")


def _iota2(shape, dim):
    return lax.broadcasted_iota(jnp.int32, shape, dim)


def _dg(a, b, dims):
    return lax.dot_general(a.astype(BF16), b.astype(BF16), (dims, ((), ())), preferred_element_type=F32)


def _bdg(a, b, ca, cb):
    return lax.dot_general(a.astype(BF16), b.astype(BF16), (((ca,), (cb,)), ((0,), (0,))), preferred_element_type=F32)


@jax.custom_vjp
def _bnn(a, b):
    return _bdg(a, b, 2, 1)


@jax.custom_vjp
def _bnt(a, b):
    return _bdg(a, b, 2, 2)


@jax.custom_vjp
def _btn(a, b):
    return _bdg(a, b, 1, 1)


_bnn.defvjp(lambda a, b: (_bnn(a, b), (a, b)), lambda r, g: (_bnt(g, r[1]), _btn(r[0], g)))
_bnt.defvjp(lambda a, b: (_bnt(a, b), (a, b)), lambda r, g: (_bnn(g, r[1]), _btn(g, r[0])))
_btn.defvjp(lambda a, b: (_btn(a, b), (a, b)), lambda r, g: (_bnt(r[1], g), _bnn(r[0], g)))


def _split2(x):
    hi = x.astype(BF16).astype(F32)
    return hi, x - hi


def _tri(bsz, n):
    return jnp.broadcast_to((_iota2((n, n), 0) >= _iota2((n, n), 1)).astype(F32), (bsz, n, n))


@jax.custom_vjp
def _cumsum_rows(x):
    tri = _tri(x.shape[0], x.shape[1])
    hi, lo = _split2(x)
    return _bdg(tri, hi, 2, 1) + _bdg(tri, lo, 2, 1)


def _cumsum_rows_bwd(_, g):
    tri = _tri(g.shape[0], g.shape[1])
    hi, lo = _split2(g)
    return (_bdg(tri, hi, 1, 1) + _bdg(tri, lo, 1, 1),)


_cumsum_rows.defvjp(lambda x: (_cumsum_rows(x), None), _cumsum_rows_bwd)


def _sigmoid(x):
    return jax.nn.sigmoid(x)


def _silu(x):
    return x * _sigmoid(x)


def _softplus(x):
    return jnp.maximum(x, 0.0) + jnp.log1p(jnp.exp(-jnp.abs(x)))


def _rms(x, w):
    return x * lax.rsqrt(jnp.mean(x * x, axis=-1, keepdims=True) + EPS) * w


@jax.custom_vjp
def _inv_unit_lower(lm):
    n = lm.shape[1]
    a = (_iota2((n, n), 0) == _iota2((n, n), 1)).astype(F32)[None] - lm
    steps = max(1, (n - 1).bit_length()) - 1
    p = _bnn(lm, lm)
    for i in range(steps):
        if i == steps - 1:
            a = a + _bnn(a, p)
        else:
            both = _bnn(jnp.concatenate([a, p], axis=1), p)
            a, p = a + both[:, :n], both[:, n:]
    return a


_inv_unit_lower.defvjp(lambda lm: (lambda a: (a, a))(_inv_unit_lower(lm)),
                       lambda a, g: (-_bnt(_btn(a, g), a),))


def _gdn_chunk(q, k, v, b_b, g_b, s):
    n, dv = q.shape[1], v.shape[2]
    r, c = _iota2((n, n), 0), _iota2((n, n), 1)
    causal, strict, eye = (r >= c)[None], (r > c)[None], (r == c)[None]
    g_cum = _cumsum_rows(g_b)
    g_i = g_cum[:, :, :n]
    g_j = jnp.sum(jnp.where(eye, g_i, 0.0), axis=1, keepdims=True)
    decay = jnp.where(causal, jnp.exp(jnp.where(causal, g_i - g_j, 0.0)), 0.0)
    e_g = jnp.exp(g_cum)
    kb = k * b_b
    kk = _bnt(jnp.concatenate([kb, q], axis=1), k)
    a_inv = _inv_unit_lower(jnp.where(strict, kk[:, :n] * decay, 0.0))
    uw = _bnn(a_inv, jnp.concatenate([v * b_b, kb * e_g], axis=2))
    ws = _bnn(jnp.concatenate([uw[:, :, dv:], q * e_g], axis=1), s)
    v_new = uw[:, :, :dv] - ws[:, :n]
    o = ws[:, n:] + _bnn(kk[:, n:] * decay, v_new)
    g_last = g_cum[:, n - 1:n, :]
    s_new = s * jnp.exp(g_last) +_btn(k * jnp.exp(g_last - g_cum), v_new)
    return o, s_new


@functools.partial(jax.custom_vjp, nondiff_argnums=(1, 2))
def _row(x, j, n):
    return x[:, j:j + 1, :]


def _row_bwd(j, n, _, g):
    return (jnp.where(_iota2((1, n, 1), 1) == j, g, 0.0),)


_row.defvjp(lambda x, j, n: (_row(x, j, n), None), _row_bwd)


def _hgrn_pairs(q, k, v, b_cum):
    n = q.shape[1]
    half = n // 2 if n > SUBLANES else n
    parts = []
    for lo in range(0, n, half):
        qs, bs = q[:, lo:], b_cum[:, lo:]
        rows = _iota2((1, n - lo, 1), 1) + lo
        acc = jnp.zeros_like(qs)
        for j in range(lo, lo + half):
            p = jnp.exp(jnp.where(rows >= j, bs - _row(b_cum, j, n), -1e30))
            acc = acc + jnp.sum(qs * _row(k, j, n) * p, axis=2, keepdims=True) * _row(v, j, n)
        parts.append(acc)
    if len(parts) == 1:
        return parts[0]
    return parts[0] + jnp.concatenate([jnp.zeros_like(parts[1]), parts[1]], axis=1)


def _hgrn_block(q, k, v, lf, st, group=HGRN_CHUNK):
    n, rows = group, q.shape[1]
    b_cum = _cumsum_rows(lf)
    outs = []
    for c in range(rows // n):
        rs = slice(c * n, (c + 1) * n)
        o = _hgrn_pairs(q[:, rs], k[:, rs], v[:, rs], b_cum[:, rs])
        if c:
            b_c = _row(b_cum, c * n - 1, rows)
            scores = _bnt(q[:, rs] * jnp.exp(b_cum[:, rs] - b_c), k[:, :c * n] * jnp.exp(b_c - b_cum[:, :c * n]))
            o = o + _bnn(scores, v[:, :c * n])
        outs.append(o)
    b_last = _row(b_cum, rows - 1, rows)
    o = _bnt(q * jnp.exp(b_cum), st) + jnp.concatenate(outs, axis=1)
    return o, st * jnp.exp(b_last) + _btn(v, k * jnp.exp(b_last - b_cum))


def _l2n_act(y, scale):
    a = _silu(y)
    return a * lax.rsqrt(jnp.sum(a * a, axis=-1, keepdims=True) + EPS) * scale


def _col(x, lane):
    return jnp.sum(jnp.where(_iota2(x.shape, 1) == lane, x, 0.0), axis=1, keepdims=True)


def _elem(x, row, lane):
    m = (_iota2(x.shape, 0) == row) & (_iota2(x.shape, 1) == lane)
    return jnp.sum(jnp.sum(jnp.where(m, x, 0.0), axis=1, keepdims=True), axis=0, keepdims=True)


def _gdn_gates(misc, aux, real, head):
    beta = _sigmoid(_col(misc, head))
    g = -jnp.exp(_elem(aux, 0, head)) * _softplus(_col(misc, N_HEADS + head) + _elem(aux, 1, head))
    g = jnp.where(real, g, 0.0)
    shape = (misc.shape[0], D_HEAD)
    return jnp.broadcast_to(beta, shape), jnp.broadcast_to(g, shape)


def _hgrn_prep(bq, bf, lb, real):
    qb = _silu(bq) * (D_HEAD ** -0.5)
    log_sig = jnp.minimum(bf, 0.0) - jnp.log1p(jnp.exp(-jnp.abs(bf)))
    pos = lb > 0.0
    lbs = jnp.where(pos, lb, 0.5)
    a = jnp.log(lbs)
    b = jnp.log1p(-lbs) + log_sig
    lae = jnp.maximum(a, b) + jnp.log1p(jnp.exp(-jnp.abs(a - b)))
    lf = jnp.where(pos, lae, log_sig)
    kb = jnp.where(pos, 1.0 - lbs, 1.0) * _sigmoid(-bf)
    return qb, jnp.where(real, kb, 0.0), jnp.where(real, lf, 0.0)


def _gated_norm(o, z, gw):
    return o * lax.rsqrt(jnp.mean(o * o, axis=-1, keepdims=True) + EPS) * gw * _silu(z)


def _shift_down(x, j):
    return x if j == 0 else pltpu.roll(x, j, 0)


def _shift_up(x, j):
    return x if j == 0 else pltpu.roll(x, x.shape[0] - j, 0)


def _all_gather_hbm(blocks, name):
    na = len(blocks)

    def body(*refs):
        x_refs, out_refs = refs[:na], refs[na:2 * na]
        send_sems, recv_sems, local_sems = refs[2 * na:]
        mx, my, mc = lax.axis_index("x"), lax.axis_index("y"), lax.axis_index("c")
        me, sibling = (mx, my, mc), (mx, my, 1 - mc)
        chips = [(1 - mx, my), (mx, 1 - my), (1 - mx, 1 - my)]

        def slab(a, px, py, pc):
            return out_refs[a].at[4 * px + 2 * py + pc]

        def copy(a, k, blk, to, own=False):
            return pltpu.make_async_remote_copy(
                src_ref=x_refs[a] if own else slab(a, *blk), dst_ref=slab(a, *blk),
                send_sem=send_sems.at[7 * a + k], recv_sem=recv_sems.at[7 * a + k], device_id=to, device_id_type=MESH)

        mine = [pltpu.make_async_copy(x_refs[a], slab(a, *me), local_sems.at[a]) for a in range(na)]
        for cp in mine:
            cp.start()
        first = [copy(a, 0, me, sibling, own=True) for a in range(na)]
        first += [copy(a, 1 + j, me, (*chip, mc), own=True) for j, chip in enumerate(chips) for a in range(na)]
        for cp in first:
            cp.start()
        passed = []
        for j, chip in enumerate(chips):
            for a in range(na):
                copy(a, 1 + j, (*chip, mc), me).wait_recv()
                passed.append(copy(a, 4 + j, (*chip, mc), sibling))
                passed[-1].start()
        for a in range(na):
            copy(a, 0, sibling, me).wait_recv()
            for j, chip in enumerate(chips):
                copy(a, 4 + j, (*chip, 1 - mc), me).wait_recv()
        for cp in first + passed:
            cp.wait_send()
        for cp in mine:
            cp.wait()

    hbm = BS(memory_space=pl.ANY)
    return pl.pallas_call(
        body, name=name, out_shape=[SDS((N_DEV, *b.shape), b.dtype) for b in blocks],
        in_specs=[hbm] * na, out_specs=[hbm] * na,
        scratch_shapes=[pltpu.SemaphoreType.DMA((7 * na,)), pltpu.SemaphoreType.DMA((7 * na,)),
                        pltpu.SemaphoreType.DMA((na,))],
    )(*blocks)


def _all_reduce_small(block, name):
    r, c = block.shape

    def body(x_ref, out_ref, buf, send_sems, recv_sems):
        mx, my, mc = lax.axis_index("x"), lax.axis_index("y"), lax.axis_index("c")
        me, sibling = (mx, my, mc), (mx, my, 1 - mc)
        chips = [(1 - mx, my), (mx, 1 - my), (1 - mx, 1 - my)]

        def slab(px, py, pc):
            return buf.at[4 * px + 2 * py + pc]

        def copy(k, blk, to, src=None):
            return pltpu.make_async_remote_copy(
                src_ref=slab(*blk) if src is None else src, dst_ref=slab(*blk),
                send_sem=send_sems.at[k], recv_sem=recv_sems.at[k], device_id=to, device_id_type=MESH)

        first = [copy(0, me, sibling, src=x_ref)]
        first += [copy(1 + j, me, (*chip, mc), src=x_ref) for j, chip in enumerate(chips)]
        for cp in first:
            cp.start()
        passed = [copy(4 + j, (*chip, mc), sibling) for j, chip in enumerate(chips)]
        for j, chip in enumerate(chips):
            copy(1 + j, (*chip, mc), me).wait_recv()
            passed[j].start()
        copy(0, sibling, me).wait_recv()
        for j, chip in enumerate(chips):
            copy(4 + j, (*chip, 1 - mc), me).wait_recv()
        for cp in first + passed:
            cp.wait_send()
        buf[4 * mx + 2 * my + mc] = x_ref[...]
        acc = buf[0]
        for d in range(1, N_DEV):
            acc = acc + buf[d]
        out_ref[...] = acc

    return pl.pallas_call(
        body, name=name, out_shape=SDS((r, c), F32),
        in_specs=[BS(memory_space=pltpu.VMEM)], out_specs=BS(memory_space=pltpu.VMEM),
        scratch_shapes=[pltpu.VMEM((N_DEV, r, c), F32), pltpu.SemaphoreType.DMA((7,)), pltpu.SemaphoreType.DMA((7,))],
    )(block)


HBM_SPEC = BS(memory_space=pltpu.HBM)
SEM_SPEC = BS(memory_space=pltpu.SEMAPHORE)
SIDE_EFFECT = pltpu.SideEffectType.DATAFLOW_SIDE_EFFECTING


def _peer(rel):
    flip = lambda v, bit: 1 - v if bit else v
    return (flip(lax.axis_index("x"), rel >> 2 & 1), flip(lax.axis_index("y"), rel >> 1 & 1),
            flip(lax.axis_index("c"), rel & 1))


def _send_all_start(blocks, scatter, name, after=None):
    na = len(blocks)
    shapes = [b.shape[1:] if scatter else b.shape for b in blocks]
    n_in = 2 * na + (after is not None)

    def body(*refs):
        srcs, lands = refs[:na], refs[na:2 * na]
        send_sems, recv_sems, token = refs[n_in], refs[n_in + 1], refs[-1]
        me = 4 * lax.axis_index("x") + 2 * lax.axis_index("y") + lax.axis_index("c")
        for a in range(na):
            for rel in range(1, N_DEV):
                px, py, pc = _peer(rel)
                pltpu.make_async_remote_copy(
                    src_ref=srcs[a].at[4 * px + 2 * py + pc] if scatter else srcs[a], dst_ref=lands[a].at[me],
                    send_sem=send_sems.at[7 * a + rel - 1], recv_sem=recv_sems.at[7 * a + rel - 1],
                    device_id=(px, py, pc), device_id_type=MESH).start()
        token[...] = jnp.zeros_like(token)

    lands = [lax.empty((N_DEV, *s), b.dtype) for s, b in zip(shapes, blocks)]
    res = pl.pallas_call(
        body, name=name,
        out_shape=([pltpu.SemaphoreType.DMA((7 * na,)), pltpu.SemaphoreType.DMA((7 * na,))]
                   + [pltpu.HBM(b.shape, b.dtype) for b in blocks] + [pltpu.HBM(ld.shape, ld.dtype) for ld in lands]
                   + [SDS((SUBLANES, LANES), F32)]),
        in_specs=[HBM_SPEC] * (2 * na) + [BS(memory_space=pl.ANY)] * (after is not None),
        out_specs=[SEM_SPEC, SEM_SPEC] + [HBM_SPEC] * (2 * na) + [BS(memory_space=pltpu.VMEM)],
        input_output_aliases={i: 2 + i for i in range(2 * na)},
        compiler_params=pltpu.CompilerParams(has_side_effects=SIDE_EFFECT),
    )(*[pltpu.with_memory_space_constraint(b, pltpu.HBM) for b in blocks],
      *[pltpu.with_memory_space_constraint(ld, pltpu.HBM) for ld in lands], *([] if after is None else [after]))
    return dict(send=res[0], recv=res[1], srcs=res[2:2 + na], lands=res[2 + na:2 + 2 * na], scatter=scatter), res[-1]


def _send_all_wait(flight, after, name):
    na = len(flight["srcs"])

    def body(*refs):
        srcs, lands = refs[:na], refs[na:2 * na]
        send_sems, recv_sems = refs[2 * na], refs[2 * na + 1]
        for a in range(na):
            for rel in range(1, N_DEV):
                cp = pltpu.make_async_remote_copy(
                    src_ref=srcs[a].at[0] if flight["scatter"] else srcs[a], dst_ref=lands[a].at[0],
                    send_sem=send_sems.at[7 * a + rel - 1], recv_sem=recv_sems.at[7 * a + rel - 1],
                    device_id=_peer(rel), device_id_type=MESH)
                cp.wait_send()
                cp.wait_recv()

    arrays = list(flight["srcs"]) + list(flight["lands"])
    res = pl.pallas_call(
        body, name=name, out_shape=[pltpu.HBM(a.shape, a.dtype) for a in arrays],
        in_specs=[HBM_SPEC] * (2 * na) + [SEM_SPEC, SEM_SPEC, BS(memory_space=pl.ANY)], out_specs=[HBM_SPEC] * (2 * na),
        input_output_aliases={i: i for i in range(2 * na)},
        compiler_params=pltpu.CompilerParams(has_side_effects=SIDE_EFFECT),
    )(*arrays, flight["send"], flight["recv"], after)
    return res[:na], res[na:]


def _sum_slabs(land, name):
    _, r, c = land.shape
    tr = _pick(r, (256, 128, 64, 32, 16, 8))

    def body(l_ref, o_ref):
        acc = l_ref[0].astype(F32)
        for d in range(1, N_DEV):
            acc = acc + l_ref[d].astype(F32)
        o_ref[...] = acc

    return pl.pallas_call(
        body, name=name, grid=(r // tr,), out_shape=SDS((r, c), F32),
        in_specs=[BS((N_DEV, tr, c), lambda j: (0, j, 0))], out_specs=BS((tr, c), lambda j: (j, 0)),
        compiler_params=_params(1),
    )(land)


def _proj_fwd(h, nw8, wp, tag):
    n = h.shape[0]
    tm = _pick(n, (1408, 768, 512, 384, 256, 192, 128, 64))
    tn = 896

    def body(h_ref, nw_ref, w_ref, proj_ref, xn_ref):
        @pl.when(pl.program_id(1) == 0)
        def _():
            xn_ref[...] = _rms(h_ref[...], nw_ref[0:1, :]).astype(BF16)

        proj_ref[...] = jnp.dot(xn_ref[...], w_ref[...], preferred_element_type=F32)

    return pl.pallas_call(
        body, name=f"proj_fwd_{tag}", grid=(n // tm, PROJ_W // tn),
        in_specs=[BS((tm, D_MODEL), lambda i, j: (i, 0)), BS((SUBLANES, D_MODEL), lambda i, j: (0, 0)),
                  BS((D_MODEL, tn), lambda i, j: (0, j))],
        out_specs=[BS((tm, tn), lambda i, j: (i, j)), BS((tm, D_MODEL), lambda i, j: (i, 0))],
        out_shape=[SDS((n, PROJ_W), F32), SDS((n, D_MODEL), BF16)], compiler_params=_params(2),
    )(h, nw8, wp)


def _conv_ext(x_ext, cw_ref):
    y = x_ext * cw_ref[3:4, :]
    for k in range(3):
        y = y + _shift_down(x_ext, 3 - k) * cw_ref[k:k + 1, :]
    return y[SUBLANES:]


def _prep_fwd(proj, cw8, aux, lb8, nseq, t_len, tag):
    n = proj.shape[0]
    tt = _pick(t_len, (192, 128, 64))
    nt_ = t_len // tt
    qkv_w = 3 * HEADS_W

    def body(cur_ref, prev_ref, misc_ref, bq_ref, bf_ref, cw_ref, aux_ref, lb_ref,
             q_ref, k_ref, v_ref, b_ref, g_ref, qb_ref, kb_ref, lf_ref, ext_ref):
        t = pl.program_id(1)
        ext_ref[0:SUBLANES, :] = jnp.where(t == 0, 0.0, prev_ref[...])
        ext_ref[SUBLANES:, :] = cur_ref[...]
        y = ext_ref[SUBLANES:, :] * cw_ref[3:4, :]
        for kk in range(3):
            y = y + ext_ref[SUBLANES - 3 + kk:SUBLANES - 3 + kk + tt, :] * cw_ref[kk:kk + 1, :]
        real = (t * tt + _iota2((tt, 1), 0)) >= N_PAD
        misc = misc_ref[...]
        auxv = aux_ref[...]
        for hd in range(N_HEADS):
            sl = slice(hd * D_HEAD, (hd + 1) * D_HEAD)
            q_ref[:, sl] = _l2n_act(y[:, sl], D_HEAD ** -0.5)
            k_ref[:, sl] = _l2n_act(y[:, HEADS_W + hd * D_HEAD:HEADS_W + (hd + 1) * D_HEAD], 1.0)
            v_ref[:, sl] = _silu(y[:, 2 * HEADS_W + hd * D_HEAD:2 * HEADS_W + (hd + 1) * D_HEAD])
            b_ref[:, sl], g_ref[:, sl] = _gdn_gates(misc, auxv, real, hd)
        qb_ref[...], kb_ref[...], lf_ref[...] = _hgrn_prep(bq_ref[...], bf_ref[...], lb_ref[0:1, :], real)

    rb = tt // SUBLANES
    row = lambda s, t: s * nt_ + t
    wide = BS((tt, HEADS_W), lambda s, t: (row(s, t), 0))
    return pl.pallas_call(
        body, name=f"prep_fwd_{tag}", grid=(nseq, nt_),
        in_specs=[BS((tt, qkv_w), lambda s, t: (row(s, t), 0)),
                  BS((SUBLANES, qkv_w), lambda s, t: (jnp.maximum(row(s, t) * rb - 1, 0), 0)),
                  BS((tt, LANES), lambda s, t: (row(s, t), C_MISC // LANES)),
                  BS((tt, HEADS_W), lambda s, t: (row(s, t), C_BQ // HEADS_W)),
                  BS((tt, HEADS_W), lambda s, t: (row(s, t), C_BF // HEADS_W)),
                  BS((SUBLANES, qkv_w), lambda s, t: (0, 0)), BS((SUBLANES, LANES), lambda s, t: (0, 0)),
                  BS((SUBLANES, HEADS_W), lambda s, t: (0, 0))],
        out_specs=[wide] * 8, out_shape=[SDS((n, HEADS_W), F32)] * 8,
        scratch_shapes=[pltpu.VMEM((tt + SUBLANES, qkv_w), F32)], compiler_params=_params(2),
    )(proj, proj, proj, proj, proj, cw8, aux, lb8)


GDN_SEQS = 4
HGRN_SEQS = 2


def _seq_block(nseq, most):
    return max(s for s in (1, 2, 4) if s <= most and nseq % s == 0)


def _to_chains(x):
    return jnp.concatenate([x[:, :, hd * D_HEAD:(hd + 1) * D_HEAD] for hd in range(N_HEADS)], axis=0)


def _from_chains(ref, rows, val):
    sb = val.shape[0] // N_HEADS
    for hd in range(N_HEADS):
        ref[:, rows, hd * D_HEAD:(hd + 1) * D_HEAD] = val[hd * sb:(hd + 1) * sb].astype(ref.dtype)


def _mixers_fwd(q, k, v, b, g, qb, kb, vb, vb_col, lf, nseq, t_len, tag):
    sb, hs = _seq_block(nseq, GDN_SEQS), _seq_block(nseq, HGRN_SEQS)
    nc = t_len // GDN_CHUNK
    chains = N_HEADS * sb

    def body(q_ref, k_ref, v_ref, b_ref, g_ref, qb_ref, kb_ref, vb_ref, lf_ref, oa_ref, ob_ref, cka_ref, ckb_ref,
             sa_ref, sb_ref):
        @pl.when(pl.program_id(1) == 0)
        def _():
            sa_ref[...] = jnp.zeros_like(sa_ref)
            sb_ref[...] = jnp.zeros_like(sb_ref)

        s = sa_ref[...]
        cka_ref[...] = s
        o, s_new = _gdn_chunk(*[_to_chains(r[...]) for r in (q_ref, k_ref, v_ref, b_ref, g_ref)], s)
        _from_chains(oa_ref, slice(None), o)
        sa_ref[...] = s_new
        for part in range(sb // hs):
            seqs, ch = slice(part * hs, (part + 1) * hs), slice(part * N_HEADS * hs, (part + 1) * N_HEADS * hs)
            s = sb_ref[ch]
            ckb_ref[ch] = s
            o, s_new = _hgrn_block(*[_to_chains(r[seqs]) for r in (qb_ref, kb_ref, vb_ref, lf_ref)], s)
            for hd in range(N_HEADS):
                ob_ref[seqs, :, hd * D_HEAD:(hd + 1) * D_HEAD] = o[hd * hs:(hd + 1) * hs]
            sb_ref[ch] = s_new

    blk = lambda cb: BS((sb, GDN_CHUNK, HEADS_W), lambda p, c: (p, c, cb))
    ck_spec = BS((None, None, chains, D_HEAD, D_HEAD), lambda p, c: (p, c, 0, 0, 0))
    ck_shape = SDS((nseq // sb, nc, chains, D_HEAD, D_HEAD), F32)
    view = lambda a: a.reshape(nseq, t_len, a.shape[1])
    oa, ob, cka, ckb = pl.pallas_call(
        body, name=f"mixers_fwd_{tag}", grid=(nseq // sb, nc),
        in_specs=[blk(0)] * 7 + [blk(vb_col), blk(0)], out_specs=[blk(0), blk(0), ck_spec, ck_spec],
        out_shape=[SDS((nseq, t_len, HEADS_W), F32)] * 2 + [ck_shape] * 2,
        scratch_shapes=[pltpu.VMEM((chains, D_HEAD, D_HEAD), F32)] * 2, compiler_params=_params(2),
    )(*[view(a) for a in (q, k, v, b, g, qb, kb, vb, lf)])
    return oa.reshape(-1, HEADS_W), ob.reshape(-1, HEADS_W), cka, ckb


def _mixers_bwd(q, k, v, b, g, qb, kb, vb, vb_col, lf, cka, ckb, doa, dob, nseq, t_len, tag):
    sb, hs = _seq_block(nseq, GDN_SEQS), _seq_block(nseq, HGRN_SEQS)
    nc = t_len // GDN_CHUNK
    chains = N_HEADS * sb

    def body(q_ref, k_ref, v_ref, b_ref, g_ref, qb_ref, kb_ref, vb_ref, lf_ref, doa_ref, dob_ref, cka_ref, ckb_ref,
             dq_ref, dk_ref, dv_ref, db_ref, dg_ref, dqb_ref, dkb_ref, dvb_ref, dlf_ref, dsa_ref, dsb_ref):
        @pl.when(pl.program_id(1) == 0)
        def _():
            dsa_ref[...] = jnp.zeros_like(dsa_ref)
            dsb_ref[...] = jnp.zeros_like(dsb_ref)

        _, vjp = jax.vjp(_gdn_chunk, *[_to_chains(r[...]) for r in (q_ref, k_ref, v_ref, b_ref, g_ref)], cka_ref[...])
        grads = vjp((_to_chains(doa_ref[...]), dsa_ref[...]))
        for ref, val in zip((dq_ref, dk_ref, dv_ref, db_ref, dg_ref), grads[:5]):
            _from_chains(ref, slice(None), val)
        dsa_ref[...] = grads[5]
        for part in range(sb // hs):
            seqs, ch = slice(part * hs, (part + 1) * hs), slice(part * N_HEADS * hs, (part + 1) * N_HEADS * hs)
            _, vjp = jax.vjp(functools.partial(_hgrn_block, group=SUBLANES),
                             *[_to_chains(r[seqs]) for r in (qb_ref, kb_ref, vb_ref, lf_ref)], ckb_ref[ch])
            grads = vjp((_to_chains(dob_ref[seqs]), dsb_ref[ch]))
            for ref, val in zip((dqb_ref, dkb_ref, dvb_ref, dlf_ref), grads[:4]):
                for hd in range(N_HEADS):
                    ref[seqs, :, hd * D_HEAD:(hd + 1) * D_HEAD] = val[hd * hs:(hd + 1) * hs].astype(ref.dtype)
            dsb_ref[ch] = grads[4]

    blk = lambda cb: BS((sb, GDN_CHUNK, HEADS_W), lambda p, c: (p, nc - 1 - c, cb))
    ck_spec = BS((None, None, chains, D_HEAD, D_HEAD), lambda p, c: (p, nc - 1 - c, 0, 0, 0))
    view = lambda a: a.reshape(nseq, t_len, a.shape[1])
    dts = [F32] * 7 + [BF16, F32]
    res = pl.pallas_call(
        body, name=f"mixers_bwd_{tag}", grid=(nseq // sb, nc),
        in_specs=[blk(0)] * 7 + [blk(vb_col), blk(0), blk(0), blk(0), ck_spec, ck_spec], out_specs=[blk(0)] * 9,
        out_shape=[SDS((nseq, t_len, HEADS_W), dt) for dt in dts],
        scratch_shapes=[pltpu.VMEM((chains, D_HEAD, D_HEAD), F32)] * 2, compiler_params=_params(2),
    )(*[view(a) for a in (q, k, v, b, g, qb, kb, vb, lf, doa, dob)], cka, ckb)
    return [r.reshape(-1, HEADS_W) for r in res]


def _post_values(oa_ref, ob_ref, z_ref, bg_ref, ga_ref, gb_ref, gn_ref, wa_ref, wb_ref, ya_ref, yb_ref):
    for hd in range(N_HEADS):
        sl = slice(hd * D_HEAD, (hd + 1) * D_HEAD)
        ya_ref[:, sl] = _gated_norm(oa_ref[:, sl], z_ref[:, sl], gn_ref[0:1, :]).astype(BF16)
        yb_ref[:, sl] = _gated_norm(ob_ref[:, sl], bg_ref[:, sl], gn_ref[1:2, :]).astype(BF16)
    pa = jnp.dot(ya_ref[...], wa_ref[...], preferred_element_type=F32)
    pb = jnp.dot(yb_ref[...], wb_ref[...], preferred_element_type=F32)
    return pa, pb, _sigmoid(ga_ref[...]), _sigmoid(gb_ref[...])


def _post_specs(tm):
    r2 = lambda i: (i, 0)
    return [BS((tm, HEADS_W), r2), BS((tm, HEADS_W), r2),
            BS((tm, HEADS_W), lambda i: (i, C_Z // HEADS_W)), BS((tm, HEADS_W), lambda i: (i, C_BG // HEADS_W)),
            BS((tm, D_MODEL), lambda i: (i, C_GA // D_MODEL)), BS((tm, D_MODEL), lambda i: (i, C_GB // D_MODEL)),
            BS((tm, D_MODEL), r2), BS((SUBLANES, LANES), lambda i: (0, 0))]


def _post_fwd(oa, ob, proj, h, gn8, wa, wb, wout, tag):
    n = h.shape[0]
    tm = _pick(n, (768, 384, 256, 192, 128, 64))

    def body(oa_ref, ob_ref, z_ref, bg_ref, ga_ref, gb_ref, h_ref, gn_ref, wa_ref, wb_ref, wout_ref, out_ref,
             ya_ref, yb_ref):
        pa, pb, sa, sb = _post_values(oa_ref, ob_ref, z_ref, bg_ref, ga_ref, gb_ref, gn_ref, wa_ref, wb_ref,
                                      ya_ref, yb_ref)
        mixed = (sa * pa + sb * pb).astype(BF16)
        out_ref[...] = h_ref[...] + jnp.dot(mixed, wout_ref[...], preferred_element_type=F32)

    full = lambda i: (0, 0)
    return pl.pallas_call(
        body, name=f"post_fwd_{tag}", grid=(n // tm,),
        in_specs=_post_specs(tm) + [BS((HEADS_W, D_MODEL), full), BS((HEADS_W, D_MODEL), full),
                                    BS((D_MODEL, D_MODEL), full)],
        out_specs=BS((tm, D_MODEL), lambda i: (i, 0)), out_shape=SDS((n, D_MODEL), F32),
        scratch_shapes=[pltpu.VMEM((tm, HEADS_W), BF16), pltpu.VMEM((tm, HEADS_W), BF16)], compiler_params=_params(1),
    )(oa, ob, proj, proj, proj, proj, h, gn8, wa, wb, wout)


def _post_bwd(dh, oa, ob, proj, h, gn8, wa, wb, wout, tag):
    n = h.shape[0]
    tm = _pick(n, (256, 192, 128, 64))

    def body(dh_ref, oa_ref, ob_ref, z_ref, bg_ref, ga_ref, gb_ref, h_ref, gn_ref, wa_ref, wb_ref, wout_ref,
             doa_ref, dob_ref, dz_ref, dbg_ref, dga_ref, dgb_ref, dwa_ref, dwb_ref, dwout_ref, dgn_ref,
             ya_ref, yb_ref):
        @pl.when(pl.program_id(0) == 0)
        def _():
            dwa_ref[...] = jnp.zeros_like(dwa_ref)
            dwb_ref[...] = jnp.zeros_like(dwb_ref)
            dwout_ref[...] = jnp.zeros_like(dwout_ref)
            dgn_ref[...] = jnp.zeros_like(dgn_ref)

        pa, pb, sa, sb = _post_values(oa_ref, ob_ref, z_ref, bg_ref, ga_ref, gb_ref, gn_ref, wa_ref, wb_ref,
                                      ya_ref, yb_ref)
        mixed = (sa * pa + sb * pb).astype(BF16)
        dout = dh_ref[...].astype(BF16)
        dwout_ref[...] += _dg(mixed, dout, ((0,), (0,)))
        dmixed = _dg(dout, wout_ref[...], ((1,), (1,)))
        dga_ref[...] = (dmixed * pa * sa * (1.0 - sa)).astype(BF16)
        dgb_ref[...] = (dmixed * pb * sb * (1.0 - sb)).astype(BF16)
        dpa = (dmixed * sa).astype(BF16)
        dpb = (dmixed * sb).astype(BF16)
        dwa_ref[...] += _dg(ya_ref[...], dpa, ((0,), (0,)))
        dwb_ref[...] += _dg(yb_ref[...], dpb, ((0,), (0,)))
        dya = _dg(dpa, wa_ref[...], ((1,), (1,)))
        dyb = _dg(dpb, wb_ref[...], ((1,), (1,)))
        dgn_a = jnp.zeros((1, D_HEAD), F32)
        dgn_b = jnp.zeros((1, D_HEAD), F32)
        for hd in range(N_HEADS):
            sl = slice(hd * D_HEAD, (hd + 1) * D_HEAD)
            _, vjp = jax.vjp(_gated_norm, oa_ref[:, sl], z_ref[:, sl], gn_ref[0:1, :])
            doa, dz, dgw = vjp(dya[:, sl])
            doa_ref[:, sl], dz_ref[:, sl], dgn_a = doa, dz.astype(BF16), dgn_a + dgw
            _, vjp = jax.vjp(_gated_norm, ob_ref[:, sl], bg_ref[:, sl], gn_ref[1:2, :])
            dob, dbg, dgw = vjp(dyb[:, sl])
            dob_ref[:, sl], dbg_ref[:, sl], dgn_b = dob, dbg.astype(BF16), dgn_b + dgw
        dgn_ref[0:1, :] += dgn_a
        dgn_ref[1:2, :] += dgn_b

    full = lambda i: (0, 0)
    r2 = lambda i: (i, 0)
    return pl.pallas_call(
        body, name=f"post_bwd_{tag}", grid=(n // tm,),
        in_specs=[BS((tm, D_MODEL), r2)] + _post_specs(tm) + [
            BS((HEADS_W, D_MODEL), full), BS((HEADS_W, D_MODEL), full), BS((D_MODEL, D_MODEL), full)],
        out_specs=[BS((tm, HEADS_W), r2)] * 4 + [BS((tm, D_MODEL), r2)] * 2 + [
            BS((HEADS_W, D_MODEL), full), BS((HEADS_W, D_MODEL), full), BS((D_MODEL, D_MODEL), full),
            BS((SUBLANES, LANES), full)],
        out_shape=[SDS((n, HEADS_W), F32), SDS((n, HEADS_W), F32), SDS((n, HEADS_W), BF16), SDS((n, HEADS_W), BF16),
                   SDS((n, D_MODEL), BF16), SDS((n, D_MODEL), BF16), SDS((HEADS_W, D_MODEL), F32),
                   SDS((HEADS_W, D_MODEL), F32), SDS((D_MODEL, D_MODEL), F32), SDS((SUBLANES, LANES), F32)],
        scratch_shapes=[pltpu.VMEM((tm, HEADS_W), BF16), pltpu.VMEM((tm, HEADS_W), BF16)], compiler_params=_params(1),
    )(dh, oa, ob, proj, proj, proj, proj, h, gn8, wa, wb, wout)


def _loss_head(h, fw8, target, nseq, t_len):
    n = h.shape[0]
    nc = t_len // GDN_CHUNK
    sub = 3 if nc % 3 == 0 else 1
    tl, nt = sub * GDN_CHUNK, nc // sub
    inv_d = 1.0 / D_MODEL

    def body(h_ref, fw_ref, *rest):
        tgt_refs, (dh_ref, acc_ref) = rest[:sub], rest[sub:]

        @pl.when((pl.program_id(0) == 0) & (pl.program_id(1) == 0))
        def _():
            acc_ref[...] = jnp.zeros_like(acc_ref)

        frames = ((pl.program_id(1) * tl + _iota2((tl, 1), 0)) >= N_PAD + N_META).astype(F32)
        y, vjp = jax.vjp(_rms, h_ref[...], fw_ref[0:1, :])
        err = (y - jnp.concatenate([r[...] for r in tgt_refs], axis=0)) * frames
        dx, dfw = vjp(err * inv_d)
        dh_ref[...] = dx
        acc_ref[0:1, :] += dfw
        acc_ref[1:2, :] += (0.5 * inv_d) * jnp.sum(err * err, axis=0, keepdims=True)

    tgt_spec = lambda u: BS((None, GDN_CHUNK, D_MODEL), lambda s, t: (s, jnp.maximum(t * sub + u - 1, 0), 0))
    return pl.pallas_call(
        body, name="loss_head", grid=(nseq, nt),
        in_specs=[BS((tl, D_MODEL), lambda s, t: (s * nt + t, 0)), BS((SUBLANES, D_MODEL), lambda s, t: (0, 0))]
        + [tgt_spec(u) for u in range(sub)],
        out_specs=[BS((tl, D_MODEL), lambda s, t: (s * nt + t, 0)), BS((SUBLANES, D_MODEL), lambda s, t: (0, 0))],
        out_shape=[SDS((n, D_MODEL), F32), SDS((SUBLANES, D_MODEL), F32)], compiler_params=_params(2),
    )(h, fw8, *[target] * sub)


def _prep_bwd(proj, dq, dk, dv, db, dg, dqb, dkb, dlf, cw8, aux, lb8, nseq, t_len, tag):
    n = proj.shape[0]
    tt = _pick(t_len, (192, 128, 64))
    nt_ = t_len // tt
    qkv_w = 3 * HEADS_W
    rb = tt // SUBLANES
    ext = tt + SUBLANES

    def body(cur_ref, prev_ref, next_ref, misc_ref, bq_ref, bf_ref, dq_ref, dqn_ref, dk_ref, dkn_ref, dv_ref, dvn_ref,
             db_ref, dg_ref, dqb_ref, dkb_ref, dlf_ref, cw_ref, aux_ref, lb_ref,
             dqkv_ref, dmisc_ref, dbq_ref, dbf_ref, dcw_ref, daux_ref, dlb_ref, dy_ref):
        s, t = pl.program_id(0), pl.program_id(1)

        @pl.when((s == 0) & (t == 0))
        def _():
            dcw_ref[...] = jnp.zeros_like(dcw_ref)
            daux_ref[...] = jnp.zeros_like(daux_ref)
            dlb_ref[...] = jnp.zeros_like(dlb_ref)

        prev = jnp.where(t == 0, 0.0, prev_ref[...])
        x_ext = jnp.concatenate([prev, cur_ref[...], next_ref[...]], axis=0)
        y = _conv_ext(x_ext, cw_ref)
        inside = (t < nt_ - 1) | (_iota2((ext, 1), 0) < tt)
        dy_ref[0:SUBLANES, :] = jnp.zeros((SUBLANES, qkv_w), F32)
        for hd in range(N_HEADS):
            for grp, (g_ref, gn_ref, scale) in enumerate(((dq_ref, dqn_ref, D_HEAD ** -0.5), (dk_ref, dkn_ref, 1.0),
                                                          (dv_ref, dvn_ref, None))):
                lo = grp * HEADS_W + hd * D_HEAD
                sl = slice(hd * D_HEAD, (hd + 1) * D_HEAD)
                cot = jnp.concatenate([g_ref[:, sl], gn_ref[:, sl]], axis=0)
                fn = _silu if scale is None else functools.partial(_l2n_act, scale=scale)
                _, vjp = jax.vjp(fn, y[:, lo:lo + D_HEAD])
                dy_ref[SUBLANES:, lo:lo + D_HEAD] = jnp.where(inside, vjp(cot)[0], 0.0)
        dy_ext = dy_ref[...]
        dx = dy_ext * cw_ref[3:4, :]
        for kk in range(3):
            dx = dx + _shift_up(dy_ext, 3 - kk) * cw_ref[kk:kk + 1, :]
        dqkv_ref[...] = dx[SUBLANES:SUBLANES + tt].astype(BF16)
        dy_cur = dy_ext[SUBLANES:SUBLANES + tt]
        for kk in range(4):
            xs = _shift_down(x_ext, 3 - kk)[SUBLANES:SUBLANES + tt]
            dcw_ref[kk:kk + 1, :] += jnp.sum(xs * dy_cur, axis=0, keepdims=True)

        real = (t * tt + _iota2((tt, 1), 0)) >= N_PAD
        dmisc = jnp.zeros((tt, LANES), F32)
        daux = jnp.zeros((SUBLANES, LANES), F32)
        for hd in range(N_HEADS):
            sl = slice(hd * D_HEAD, (hd + 1) * D_HEAD)
            _, vjp = jax.vjp(lambda m, a: _gdn_gates(m, a, real, hd), misc_ref[...], aux_ref[...])
            dm, da = vjp((db_ref[:, sl], dg_ref[:, sl]))
            dmisc, daux = dmisc + dm, daux + da
        dmisc_ref[...] = dmisc.astype(BF16)
        daux_ref[...] += daux
        _, vjp = jax.vjp(lambda a, b, c: _hgrn_prep(a, b, c, real), bq_ref[...], bf_ref[...], lb_ref[0:1, :])
        dbq, dbf, dlb = vjp((dqb_ref[...], dkb_ref[...], dlf_ref[...]))
        dbq_ref[...], dbf_ref[...] = dbq.astype(BF16), dbf.astype(BF16)
        dlb_ref[0:1, :] += dlb

    row = lambda s, t: s * nt_ + t
    cur = lambda s, t: (row(s, t), 0)
    nxt = lambda s, t: (jnp.minimum((row(s, t) + 1) * rb, n // SUBLANES - 1), 0)
    wide = BS((tt, HEADS_W), cur)
    halo = BS((SUBLANES, HEADS_W), nxt)
    full = lambda s, t: (0, 0)
    return pl.pallas_call(
        body, name=f"prep_bwd_{tag}", grid=(nseq, nt_),
        in_specs=[BS((tt, qkv_w), cur), BS((SUBLANES, qkv_w), lambda s, t: (jnp.maximum(row(s, t) * rb - 1, 0), 0)),
                  BS((SUBLANES, qkv_w), nxt), BS((tt, LANES), lambda s, t: (row(s, t), C_MISC // LANES)),
                  BS((tt, HEADS_W), lambda s, t: (row(s, t), C_BQ // HEADS_W)),
                  BS((tt, HEADS_W), lambda s, t: (row(s, t), C_BF // HEADS_W)),
                  wide, halo, wide, halo, wide, halo, wide, wide, wide, wide, wide,
                  BS((SUBLANES, qkv_w), full), BS((SUBLANES, LANES), full), BS((SUBLANES, HEADS_W), full)],
        out_specs=[BS((tt, qkv_w), cur), BS((tt, LANES), cur), wide, wide,
                   BS((SUBLANES, qkv_w), full), BS((SUBLANES, LANES), full), BS((SUBLANES, HEADS_W), full)],
        out_shape=[SDS((n, qkv_w), BF16), SDS((n, LANES), BF16), SDS((n, HEADS_W), BF16), SDS((n, HEADS_W), BF16),
                   SDS((SUBLANES, qkv_w), F32), SDS((SUBLANES, LANES), F32), SDS((SUBLANES, HEADS_W), F32)],
        scratch_shapes=[pltpu.VMEM((tt + 2 * SUBLANES, qkv_w), F32)], compiler_params=_params(2),
    )(proj, proj, proj, proj, proj, proj, dq, dq, dk, dk, dv, dv, db, dg, dqb, dkb, dlf, cw8, aux, lb8)


def _proj_bwd_x(pieces, wp, h, nw8, dh_res, tag):
    n = h.shape[0]
    tm = _pick(n, (384, 256, 192, 128, 64))
    widths = [p.shape[1] for p in pieces]
    assert sum(widths) == PROJ_W

    def body(*refs):
        p_refs = refs[:len(pieces)]
        w_ref, h_ref, nw_ref, dres_ref, dh_ref, dnw_ref = refs[len(pieces):]

        @pl.when(pl.program_id(0) == 0)
        def _():
            dnw_ref[...] = jnp.zeros_like(dnw_ref)

        dxn, off = None, 0
        for p_ref, w in zip(p_refs, widths):
            part = _dg(p_ref[...], w_ref[:, off:off + w], ((1,), (1,)))
            dxn = part if dxn is None else dxn + part
            off += w
        _, vjp = jax.vjp(_rms, h_ref[...], nw_ref[0:1, :])
        dx, dnw = vjp(dxn)
        dh_ref[...] = dres_ref[...] + dx
        dnw_ref[0:1, :] += dnw

    r2 = lambda i: (i, 0)
    full = lambda i: (0, 0)
    return pl.pallas_call(
        body, name=f"proj_bwd_x_{tag}", grid=(n // tm,),
        in_specs=[BS((tm, w), r2) for w in widths] + [BS((D_MODEL, PROJ_W), full), BS((tm, D_MODEL), r2),
                                                      BS((SUBLANES, D_MODEL), full), BS((tm, D_MODEL), r2)],
        out_specs=[BS((tm, D_MODEL), r2), BS((SUBLANES, D_MODEL), full)],
        out_shape=[SDS((n, D_MODEL), F32), SDS((SUBLANES, D_MODEL), F32)], compiler_params=_params(1),
    )(*pieces, wp, h, nw8, dh_res)


def _proj_bwd_w(xn, pieces, tag):
    n = xn.shape[0]
    tm = _pick(n, (384, 256, 192, 128, 64))
    widths = [p.shape[1] for p in pieces]
    assert sum(widths) == PROJ_W

    def body(*refs):
        x_ref, p_refs = refs[0], refs[1:1 + len(pieces)]
        o_ref, acc_ref = refs[1 + len(pieces):]

        @pl.when(pl.program_id(0) == 0)
        def _():
            acc_ref[...] = jnp.zeros_like(acc_ref)

        off = 0
        for p_ref, w in zip(p_refs, widths):
            acc_ref[:, off:off + w] += _dg(x_ref[...], p_ref[...], ((0,), (0,)))
            off += w

        @pl.when(pl.program_id(0) == pl.num_programs(0) - 1)
        def _():
            pltpu.sync_copy(acc_ref, o_ref)

    r2 = lambda i: (i, 0)
    return pl.pallas_call(
        body, name=f"proj_bwd_w_{tag}", grid=(n // tm,),
        in_specs=[BS((tm, D_MODEL), r2)] + [BS((tm, w), r2) for w in widths], out_specs=BS(memory_space=pl.ANY),
        out_shape=SDS((D_MODEL, PROJ_W), F32), scratch_shapes=[pltpu.VMEM((D_MODEL, PROJ_W), F32)],
        compiler_params=_params(1),
    )(xn, *pieces)


def _adamw(w, g, m, v, name):
    lead, rows, cols = w.shape
    tr = _pick(rows, (256, 128, 64, 32, 16, 8, 4, 2, 1)) if rows > 256 else rows

    def body(w_ref, g_ref, m_ref, v_ref, d_ref, nm_ref, nv_ref):
        gr = g_ref[...]
        m_new = ADAM_B1 * m_ref[...] + (1.0 - ADAM_B1) * gr
        v_new = ADAM_B2 * v_ref[...] + (1.0 - ADAM_B2) * jnp.square(gr)
        m_hat = m_new / (1.0 - ADAM_B1 ** ADAM_STEP)
        v_hat = v_new / (1.0 - ADAM_B2 ** ADAM_STEP)
        d_ref[...] = -ADAM_LR * (m_hat / (jnp.sqrt(v_hat) + ADAM_EPS) + ADAM_WD * w_ref[...])
        nm_ref[...] = m_new
        nv_ref[...] = v_new

    blk = BS((None, tr, cols), lambda a, i: (a, i, 0))
    return pl.pallas_call(
        body, name=name, grid=(lead, rows // tr), in_specs=[blk] * 4, out_specs=[blk] * 3,
        out_shape=[SDS((lead, rows, cols), F32)] * 3, compiler_params=_params(2),
    )(w, g, m, v)


def _row8(v, width):
    v = jnp.atleast_2d(v).astype(F32)
    return jnp.pad(v, ((0, SUBLANES - v.shape[0]), (0, width - v.shape[1])))


REF_MISC = 1536
N_MISC = 2 * N_HEADS
LAYOUT_RUNS = ((0, REF_MISC, 0), (REF_MISC + N_MISC, REF_W, REF_MISC), (REF_MISC, REF_MISC + N_MISC, C_MISC))


def _to_layout(slabs, tag):
    n_slabs, rows, width = slabs.shape
    tr = _pick(rows, (256, 128, 64, 32, 16))

    def body(x_ref, o_ref):
        off = 0
        for lo, hi, _ in sorted(LAYOUT_RUNS, key=lambda run: run[2]):
            for j in range(n_slabs):
                a, b = max(lo, j * width), min(hi, (j + 1) * width)
                if a < b:
                    o_ref[:, off:off + b - a] = x_ref[j, :, a - j * width:b - j * width]
                    off += b - a
        o_ref[:, off:] = jnp.zeros((tr, PROJ_W - off), o_ref.dtype)

    return pl.pallas_call(
        body, name=f"weights_layout_{tag}", grid=(rows // tr,), out_shape=SDS((rows, PROJ_W), slabs.dtype),
        in_specs=[BS((n_slabs, tr, width), lambda i: (0, i, 0))], out_specs=BS((tr, PROJ_W), lambda i: (i, 0)),
        compiler_params=_params(1),
    )(slabs)


def _from_layout(dw, n_slabs):
    width = REF_W // n_slabs
    slabs = []
    for j in range(n_slabs):
        pieces = []
        for lo, hi, at in sorted(LAYOUT_RUNS):
            a, b = max(lo, j * width), min(hi, (j + 1) * width)
            if a < b:
                pieces.append(dw[:, at + a - lo:at + b - lo])
        slabs.append(jnp.concatenate(pieces, axis=1))
    return slabs


def _lower_bounds(lb):
    sm = jax.nn.softmax(lb.astype(F32), axis=0)
    return jnp.cumsum(sm, axis=0) - sm[0]


def kernel(x, meta_tokens, norm_w, w_in, conv_w, a_log, dt_bias, gnorm_a, gnorm_b, hgrn_lower_bounds, w_branch_a, w_branch_b, w_out, final_norm_w, loss_target, m_meta_tokens, m_norm_w, m_w_in, m_conv_w, m_a_log, m_dt_bias, m_gnorm_a, m_gnorm_b, m_hgrn_lower_bounds, m_w_branch_a, m_w_branch_b, m_w_out, m_final_norm_w, v_meta_tokens, v_norm_w, v_w_in, v_conv_w, v_a_log, v_dt_bias, v_gnorm_a, v_gnorm_b, v_hgrn_lower_bounds, v_w_branch_a, v_w_branch_b, v_w_out, v_final_norm_w):
    nseq, seq, _ = x.shape
    depth = norm_w.shape[0]
    t_len = N_PAD + N_META + seq
    n = nseq * t_len
    conv_c = conv_w.shape[2]
    my = 4 * lax.axis_index("x") + 2 * lax.axis_index("y") + lax.axis_index("c")

    assert depth >= 2
    by_cols = lambda g: g.transpose(1, 2, 0, 3).reshape(g.shape[1], g.shape[2], N_DEV * g.shape[3])
    first = _all_gather_hbm([w_in[:1].astype(BF16), conv_w, meta_tokens], "gather_first")
    later_flight, later_token = _send_all_start(
        [w_in[1:].astype(BF16), w_branch_a.astype(BF16), w_branch_b.astype(BF16), w_out.astype(BF16)], False,
        "gather_later_start", after=first[0])
    w_in_slabs = [first[0]]
    conv_full = by_cols(first[1])
    meta_full = first[2].transpose(1, 0, 2).reshape(N_META, D_MODEL)

    lb_all, lb_vjp = jax.vjp(_lower_bounds, hgrn_lower_bounds)

    h = jnp.concatenate([jnp.zeros((nseq, N_PAD, D_MODEL), F32),
                         jnp.broadcast_to(meta_full[None], (nseq, N_META, D_MODEL)), x], axis=1).reshape(n, D_MODEL)
    saved = []
    for l in range(depth):
        wp = _to_layout(w_in_slabs[0][:, 0] if l == 0 else w_in_slabs[1][:, l - 1], l)
        nw8 = _row8(norm_w[l], D_MODEL)
        if l == 0:
            nw8 = nw8 + later_token[0:1, 0:1]
        cw8 = _row8(conv_full[l], 3 * HEADS_W)
        aux = _row8(jnp.stack([a_log[l], dt_bias[l]]), LANES)
        lb8 = _row8(lb_all[l], HEADS_W)
        gn8 = _row8(jnp.stack([gnorm_a[l], gnorm_b[l]]), LANES)
        proj, xn = _proj_fwd(h, nw8, wp, l)
        q, k, v, b, g, qb, kb, lf = _prep_fwd(proj, cw8, aux, lb8, nseq, t_len, l)
        oa, ob, sck_a, sck_b = _mixers_fwd(q, k, v, b, g, qb, kb, proj, C_BI // HEADS_W, lf, nseq, t_len, l)
        if l == 0:
            sent, landed = _send_all_wait(later_flight, ob, "gather_later_wait")
            landed = [lax.dynamic_update_slice(ld, own[None], (my,) + (0,) * own.ndim) for ld, own in zip(landed, sent)]
            w_in_slabs.append(landed[0])
            wa_full, wb_full = by_cols(landed[1]), by_cols(landed[2])
            wout_full = landed[3].transpose(1, 0, 2, 3).reshape(depth, D_MODEL, D_MODEL)
        wa_l, wb_l, wout_l = wa_full[l], wb_full[l], wout_full[l]
        h_next = _post_fwd(oa, ob, proj, h, gn8, wa_l, wb_l, wout_l, l)
        saved.append(dict(h=h, wp=wp, nw8=nw8, cw8=cw8, aux=aux, lb8=lb8, gn8=gn8, proj=proj, xn=xn, q=q, k=k, v=v, b=b,
                          wa=wa_l, wb=wb_l, wout=wout_l,
                          g=g, qb=qb, kb=kb, lf=lf, oa=oa, ob=ob, sck_a=sck_a, sck_b=sck_b))
        h = h_next

    dh, acc = _loss_head(h, _row8(final_norm_w, D_MODEL), loss_target, nseq, t_len)

    g_win, g_wa, g_wb, g_wout, g_conv, small = [], [], [], [], [], []

    def mixer_slabs(dwa_s, dwb_s, dwout_s):
        nl = len(dwa_s)
        rows = lambda a: jnp.stack(a).reshape(nl * HEADS_W, N_DEV, LANES).transpose(1, 0, 2)
        wout = jnp.stack(dwout_s).reshape(nl, N_DEV, LANES, D_MODEL).transpose(1, 0, 2, 3)
        return [jnp.concatenate([rows(dwa_s), rows(dwb_s)], axis=1).astype(BF16),
                wout.reshape(N_DEV, nl * LANES, D_MODEL).astype(BF16)]

    def win_slabs(per_layer, dtype):
        return jnp.stack([jnp.concatenate([sl[j] for sl in per_layer], axis=0) for j in range(N_DEV)]).astype(dtype)

    for l in reversed(range(depth)):
        s = saved[l]
        gn8, aux = s["gn8"], s["aux"]
        if l == 0:
            later_flight, later_token = _send_all_start(
                [win_slabs(g_win[::-1], BF16)] + mixer_slabs(g_wa[::-1], g_wb[::-1], g_wout[::-1]), True,
                "scatter_later_start")
            gn8 = gn8 + later_token[0:1, 0:1]
        doa, dob, dz, dbg, dga, dgb, dwa, dwb, dwout, dgn = _post_bwd(
            dh, s["oa"], s["ob"], s["proj"], s["h"], gn8, s["wa"], s["wb"], s["wout"], l)
        dq, dk, dv, db, dg, dqb, dkb, dbi, dlf = _mixers_bwd(
            s["q"], s["k"], s["v"], s["b"], s["g"], s["qb"], s["kb"], s["proj"], C_BI // HEADS_W, s["lf"], s["sck_a"],
            s["sck_b"], doa, dob, nseq, t_len, l)
        if l == 0:
            mixer_flight, mixer_token = _send_all_start(mixer_slabs([dwa], [dwb], [dwout]), True, "scatter_first_start")
            aux = aux + mixer_token[0:1, 0:1]
        dqkv, dmisc, dbq, dbf, dcw, daux, dlb = _prep_bwd(s["proj"], dq, dk, dv, db, dg, dqb, dkb, dlf, s["cw8"], aux,
                                                          s["lb8"], nseq, t_len, l)
        pieces = [dqkv, dz, dbq, dbf, dbi, dbg, dga, dgb, dmisc]
        g_win.append(_from_layout(_proj_bwd_w(s["xn"], pieces, l), N_DEV))
        g_conv.append(dcw[:4])
        nw8 = s["nw8"]
        if l == 0:
            dconv = jnp.stack(g_conv[::-1])
            conv_slabs = dconv.reshape(depth * dconv.shape[1], N_DEV, conv_c).transpose(1, 0, 2)
            win_flight, win_token = _send_all_start([win_slabs(g_win[-1:], BF16), conv_slabs], True, "scatter_win_start")
            nw8 = nw8 + win_token[0:1, 0:1]
        dh, dnw = _proj_bwd_x(pieces, s["wp"], s["h"], nw8, dh, l)
        g_wa.append(dwa)
        g_wb.append(dwb)
        g_wout.append(dwout)
        small.append((dnw[0], dgn[0], dgn[1], daux[0, :N_HEADS], daux[1, :N_HEADS], dlb[0]))
    small.reverse()
    dh = dh.reshape(nseq, t_len, D_MODEL)
    grad_x = dh[:, N_PAD + N_META:]

    packed = jnp.concatenate([small[0][1], small[1][1], small[0][2], small[1][2], small[0][3], small[1][3],
                              small[0][4], small[1][4]])
    tile = jnp.concatenate([
        jnp.sum(dh[:, N_PAD:N_PAD + N_META], axis=0), _row8(jnp.stack([small[0][0], small[1][0], acc[0]]), D_MODEL),
        _row8(jnp.stack([small[0][5], small[1][5]]), D_MODEL), _row8(packed, D_MODEL), _row8(acc[1], D_MODEL)], axis=0)
    tile = _all_reduce_small(tile, "reduce_small")
    loss = jnp.sum(tile[40])
    g_meta = lax.dynamic_slice_in_dim(tile[0:N_META], my * LANES, LANES, axis=1)
    g_norm, g_final = tile[16:18], tile[18]
    (g_lb,) = lb_vjp(tile[24:26, :HEADS_W])
    r21 = tile[32]
    g_gna, g_gnb = r21[0:256].reshape(2, LANES), r21[256:512].reshape(2, LANES)
    g_alog, g_dtb = r21[512:520].reshape(2, N_HEADS), r21[520:528].reshape(2, N_HEADS)

    def landed_sums(flight, tag):
        sent, landed = _send_all_wait(flight, dh, f"{tag}_wait")
        landed = [lax.dynamic_update_slice(ld, lax.dynamic_index_in_dim(src, my, 0, keepdims=True), (my, 0, 0))
                  for ld, src in zip(landed, sent)]
        return [_sum_slabs(ld, f"{tag}_sum{i}") for i, ld in enumerate(landed)]

    l_win, l_ab, l_wout = landed_sums(later_flight, "scatter_later")
    r_ab, r_wout = landed_sums(mixer_flight, "scatter_first")
    r_win, r_conv = landed_sums(win_flight, "scatter_win")
    both = lambda a, b, shape: jnp.concatenate([a.reshape(1, *shape[1:]), b.reshape(depth - 1, *shape[1:])])
    half, half_l = HEADS_W, (depth - 1) * HEADS_W
    mine = [both(r_win, l_win, w_in.shape), both(r_ab[:half], l_ab[:half_l], w_branch_a.shape),
            both(r_ab[half:], l_ab[half_l:], w_branch_b.shape), both(r_wout, l_wout, w_out.shape), r_conv]
    gseg = lambda i, shape: mine[i].reshape(shape)
    grads = {
        "meta_tokens": g_meta, "norm_w": g_norm, "w_in": gseg(0, w_in.shape), "conv_w": gseg(4, conv_w.shape),
        "a_log": g_alog, "dt_bias": g_dtb, "gnorm_a": g_gna, "gnorm_b": g_gnb, "hgrn_lower_bounds": g_lb,
        "w_branch_a": gseg(1, w_branch_a.shape), "w_branch_b": gseg(2, w_branch_b.shape), "w_out": gseg(3, w_out.shape),
        "final_norm_w": g_final}
    weights = {
        "meta_tokens": (meta_tokens, m_meta_tokens, v_meta_tokens), "norm_w": (norm_w, m_norm_w, v_norm_w),
        "w_in": (w_in, m_w_in, v_w_in), "conv_w": (conv_w, m_conv_w, v_conv_w), "a_log": (a_log, m_a_log, v_a_log),
        "dt_bias": (dt_bias, m_dt_bias, v_dt_bias), "gnorm_a": (gnorm_a, m_gnorm_a, v_gnorm_a),
        "gnorm_b": (gnorm_b, m_gnorm_b, v_gnorm_b),
        "hgrn_lower_bounds": (hgrn_lower_bounds, m_hgrn_lower_bounds, v_hgrn_lower_bounds),
        "w_branch_a": (w_branch_a, m_w_branch_a, v_w_branch_a), "w_branch_b": (w_branch_b, m_w_branch_b, v_w_branch_b),
        "w_out": (w_out, m_w_out, v_w_out), "final_norm_w": (final_norm_w, m_final_norm_w, v_final_norm_w)}
    names = list(weights)
    deltas, new_m, new_v = [], [], []
    for nm in names:
        w, m, v = weights[nm]
        view = (1,) * (3 - w.ndim) + w.shape
        d, m2, v2 = _adamw(w.reshape(view), grads[nm].reshape(view), m.reshape(view), v.reshape(view), f"adamw_{nm}")
        deltas.append(d.reshape(w.shape))
        new_m.append(m2.reshape(w.shape))
        new_v.append(v2.reshape(w.shape))
    return (loss, grad_x, *[grads[nm].reshape(weights[nm][0].shape) for nm in names], *deltas, *new_m, *new_v)
```

```python
import functools

import jax
import jax.numpy as jnp
from jax import lax
from jax.experimental import pallas as pl
from jax.experimental.pallas import tpu as pltpu

F32 = jnp.float32
BF16 = jnp.bfloat16

D_MODEL = 1024
N_HEADS = 4
D_HEAD = 128
HEADS_W = N_HEADS * D_HEAD
N_META = 16
N_PAD = 48
GDN_CHUNK = 64
HGRN_CHUNK = 16
EPS = 1e-6
N_DEV = 8
LANES = 128
SUBLANES = 8
VMEM_LIMIT = 56 * 1024 * 1024

C_QKV, C_Z, C_BQ, C_BF, C_BI, C_BG, C_GA, C_GB, C_MISC = 0, 1536, 2048, 2560, 3072, 3584, 4096, 5120, 6144
PROJ_W = 6272
REF_W = 6152

ADAM_LR, ADAM_B1, ADAM_B2, ADAM_EPS, ADAM_WD, ADAM_STEP = 0.001, 0.9, 0.999, 1e-08, 0.01, 10

MESH = pl.DeviceIdType.MESH
SDS = jax.ShapeDtypeStruct
BS = pl.BlockSpec


def _params(n_axes):
    return pltpu.CompilerParams(dimension_semantics=("arbitrary",) * n_axes, vmem_limit_bytes=VMEM_LIMIT)


def _pick(n, cands):
    for c in cands:
        if n % c == 0:
            return c
    raise ValueError(f"no tile for {n} among {cands}")


def _iota2(shape, dim):
    return lax.broadcasted_iota(jnp.int32, shape, dim)


def _dg(a, b, dims):
    return lax.dot_general(a.astype(BF16), b.astype(BF16), (dims, ((), ())), preferred_element_type=F32)


def _bdg(a, b, ca, cb):
    return lax.dot_general(a.astype(BF16), b.astype(BF16), (((ca,), (cb,)), ((0,), (0,))), preferred_element_type=F32)


@jax.custom_vjp
def _bnn(a, b):
    return _bdg(a, b, 2, 1)


@jax.custom_vjp
def _bnt(a, b):
    return _bdg(a, b, 2, 2)


@jax.custom_vjp
def _btn(a, b):
    return _bdg(a, b, 1, 1)


_bnn.defvjp(lambda a, b: (_bnn(a, b), (a, b)), lambda r, g: (_bnt(g, r[1]), _btn(r[0], g)))
_bnt.defvjp(lambda a, b: (_bnt(a, b), (a, b)), lambda r, g: (_bnn(g, r[1]), _btn(g, r[0])))
_btn.defvjp(lambda a, b: (_btn(a, b), (a, b)), lambda r, g: (_bnt(r[1], g), _bnn(r[0], g)))


def _split2(x):
    hi = x.astype(BF16).astype(F32)
    return hi, x - hi


def _tri(bsz, n):
    return jnp.broadcast_to((_iota2((n, n), 0) >= _iota2((n, n), 1)).astype(F32), (bsz, n, n))


@jax.custom_vjp
def _cumsum_rows(x):
    tri = _tri(x.shape[0], x.shape[1])
    hi, lo = _split2(x)
    return _bdg(tri, hi, 2, 1) + _bdg(tri, lo, 2, 1)


def _cumsum_rows_bwd(_, g):
    tri = _tri(g.shape[0], g.shape[1])
    hi, lo = _split2(g)
    return (_bdg(tri, hi, 1, 1) + _bdg(tri, lo, 1, 1),)


_cumsum_rows.defvjp(lambda x: (_cumsum_rows(x), None), _cumsum_rows_bwd)


def _sigmoid(x):
    return jax.nn.sigmoid(x)


def _silu(x):
    return x * _sigmoid(x)


def _softplus(x):
    return jnp.maximum(x, 0.0) + jnp.log1p(jnp.exp(-jnp.abs(x)))


def _rms(x, w):
    return x * lax.rsqrt(jnp.mean(x * x, axis=-1, keepdims=True) + EPS) * w


@jax.custom_vjp
def _inv_unit_lower(lm):
    n = lm.shape[1]
    a = (_iota2((n, n), 0) == _iota2((n, n), 1)).astype(F32)[None] - lm
    steps = max(1, (n - 1).bit_length()) - 1
    p = _bnn(lm, lm)
    for i in range(steps):
        if i == steps - 1:
            a = a + _bnn(a, p)
        else:
            both = _bnn(jnp.concatenate([a, p], axis=1), p)
            a, p = a + both[:, :n], both[:, n:]
    return a


_inv_unit_lower.defvjp(lambda lm: (lambda a: (a, a))(_inv_unit_lower(lm)),
                       lambda a, g: (-_bnt(_btn(a, g), a),))


def _gdn_chunk(q, k, v, b_b, g_b, s):
    n, dv = q.shape[1], v.shape[2]
    r, c = _iota2((n, n), 0), _iota2((n, n), 1)
    causal, strict, eye = (r >= c)[None], (r > c)[None], (r == c)[None]
    g_cum = _cumsum_rows(g_b)
    g_i = g_cum[:, :, :n]
    g_j = jnp.sum(jnp.where(eye, g_i, 0.0), axis=1, keepdims=True)
    decay = jnp.where(causal, jnp.exp(jnp.where(causal, g_i - g_j, 0.0)), 0.0)
    e_g = jnp.exp(g_cum)
    kb = k * b_b
    kk = _bnt(jnp.concatenate([kb, q], axis=1), k)
    a_inv = _inv_unit_lower(jnp.where(strict, kk[:, :n] * decay, 0.0))
    uw = _bnn(a_inv, jnp.concatenate([v * b_b, kb * e_g], axis=2))
    ws = _bnn(jnp.concatenate([uw[:, :, dv:], q * e_g], axis=1), s)
    v_new = uw[:, :, :dv] - ws[:, :n]
    o = ws[:, n:] + _bnn(kk[:, n:] * decay, v_new)
    g_last = g_cum[:, n - 1:n, :]
    s_new = s * jnp.exp(g_last) +_btn(k * jnp.exp(g_last - g_cum), v_new)
    return o, s_new


@functools.partial(jax.custom_vjp, nondiff_argnums=(1, 2))
def _row(x, j, n):
    return x[:, j:j + 1, :]


def _row_bwd(j, n, _, g):
    return (jnp.where(_iota2((1, n, 1), 1) == j, g, 0.0),)


_row.defvjp(lambda x, j, n: (_row(x, j, n), None), _row_bwd)


def _hgrn_pairs(q, k, v, b_cum):
    n = q.shape[1]
    half = n // 2 if n > SUBLANES else n
    parts = []
    for lo in range(0, n, half):
        qs, bs = q[:, lo:], b_cum[:, lo:]
        rows = _iota2((1, n - lo, 1), 1) + lo
        acc = jnp.zeros_like(qs)
        for j in range(lo, lo + half):
            p = jnp.exp(jnp.where(rows >= j, bs - _row(b_cum, j, n), -1e30))
            acc = acc + jnp.sum(qs * _row(k, j, n) * p, axis=2, keepdims=True) * _row(v, j, n)
        parts.append(acc)
    if len(parts) == 1:
        return parts[0]
    return parts[0] + jnp.concatenate([jnp.zeros_like(parts[1]), parts[1]], axis=1)


def _hgrn_block(q, k, v, lf, st, group=HGRN_CHUNK):
    n, rows = group, q.shape[1]
    b_cum = _cumsum_rows(lf)
    outs = []
    for c in range(rows // n):
        rs = slice(c * n, (c + 1) * n)
        o = _hgrn_pairs(q[:, rs], k[:, rs], v[:, rs], b_cum[:, rs])
        if c:
            b_c = _row(b_cum, c * n - 1, rows)
            scores = _bnt(q[:, rs] * jnp.exp(b_cum[:, rs] - b_c), k[:, :c * n] * jnp.exp(b_c - b_cum[:, :c * n]))
            o = o + _bnn(scores, v[:, :c * n])
        outs.append(o)
    b_last = _row(b_cum, rows - 1, rows)
    o = _bnt(q * jnp.exp(b_cum), st) + jnp.concatenate(outs, axis=1)
    return o, st * jnp.exp(b_last) + _btn(v, k * jnp.exp(b_last - b_cum))


def _l2n_act(y, scale):
    a = _silu(y)
    return a * lax.rsqrt(jnp.sum(a * a, axis=-1, keepdims=True) + EPS) * scale


def _col(x, lane):
    return jnp.sum(jnp.where(_iota2(x.shape, 1) == lane, x, 0.0), axis=1, keepdims=True)


def _elem(x, row, lane):
    m = (_iota2(x.shape, 0) == row) & (_iota2(x.shape, 1) == lane)
    return jnp.sum(jnp.sum(jnp.where(m, x, 0.0), axis=1, keepdims=True), axis=0, keepdims=True)


def _gdn_gates(misc, aux, real, head):
    beta = _sigmoid(_col(misc, head))
    g = -jnp.exp(_elem(aux, 0, head)) * _softplus(_col(misc, N_HEADS + head) + _elem(aux, 1, head))
    g = jnp.where(real, g, 0.0)
    shape = (misc.shape[0], D_HEAD)
    return jnp.broadcast_to(beta, shape), jnp.broadcast_to(g, shape)


def _hgrn_prep(bq, bf, lb, real):
    qb = _silu(bq) * (D_HEAD ** -0.5)
    log_sig = jnp.minimum(bf, 0.0) - jnp.log1p(jnp.exp(-jnp.abs(bf)))
    pos = lb > 0.0
    lbs = jnp.where(pos, lb, 0.5)
    a = jnp.log(lbs)
    b = jnp.log1p(-lbs) + log_sig
    lae = jnp.maximum(a, b) + jnp.log1p(jnp.exp(-jnp.abs(a - b)))
    lf = jnp.where(pos, lae, log_sig)
    kb = jnp.where(pos, 1.0 - lbs, 1.0) * _sigmoid(-bf)
    return qb, jnp.where(real, kb, 0.0), jnp.where(real, lf, 0.0)


def _gated_norm(o, z, gw):
    return o * lax.rsqrt(jnp.mean(o * o, axis=-1, keepdims=True) + EPS) * gw * _silu(z)


def _shift_down(x, j):
    return x if j == 0 else pltpu.roll(x, j, 0)


def _shift_up(x, j):
    return x if j == 0 else pltpu.roll(x, x.shape[0] - j, 0)


def _all_gather_hbm(blocks, name):
    na = len(blocks)

    def body(*refs):
        x_refs, out_refs = refs[:na], refs[na:2 * na]
        send_sems, recv_sems, local_sems = refs[2 * na:]
        mx, my, mc = lax.axis_index("x"), lax.axis_index("y"), lax.axis_index("c")
        me, sibling = (mx, my, mc), (mx, my, 1 - mc)
        chips = [(1 - mx, my), (mx, 1 - my), (1 - mx, 1 - my)]

        def slab(a, px, py, pc):
            return out_refs[a].at[4 * px + 2 * py + pc]

        def copy(a, k, blk, to, own=False):
            return pltpu.make_async_remote_copy(
                src_ref=x_refs[a] if own else slab(a, *blk), dst_ref=slab(a, *blk),
                send_sem=send_sems.at[7 * a + k], recv_sem=recv_sems.at[7 * a + k], device_id=to, device_id_type=MESH)

        mine = [pltpu.make_async_copy(x_refs[a], slab(a, *me), local_sems.at[a]) for a in range(na)]
        for cp in mine:
            cp.start()
        first = [copy(a, 0, me, sibling, own=True) for a in range(na)]
        first += [copy(a, 1 + j, me, (*chip, mc), own=True) for j, chip in enumerate(chips) for a in range(na)]
        for cp in first:
            cp.start()
        passed = []
        for j, chip in enumerate(chips):
            for a in range(na):
                copy(a, 1 + j, (*chip, mc), me).wait_recv()
                passed.append(copy(a, 4 + j, (*chip, mc), sibling))
                passed[-1].start()
        for a in range(na):
            copy(a, 0, sibling, me).wait_recv()
            for j, chip in enumerate(chips):
                copy(a, 4 + j, (*chip, 1 - mc), me).wait_recv()
        for cp in first + passed:
            cp.wait_send()
        for cp in mine:
            cp.wait()

    hbm = BS(memory_space=pl.ANY)
    return pl.pallas_call(
        body, name=name, out_shape=[SDS((N_DEV, *b.shape), b.dtype) for b in blocks],
        in_specs=[hbm] * na, out_specs=[hbm] * na,
        scratch_shapes=[pltpu.SemaphoreType.DMA((7 * na,)), pltpu.SemaphoreType.DMA((7 * na,)),
                        pltpu.SemaphoreType.DMA((na,))],
    )(*blocks)


def _all_reduce_small(block, name):
    r, c = block.shape

    def body(x_ref, out_ref, buf, send_sems, recv_sems):
        mx, my, mc = lax.axis_index("x"), lax.axis_index("y"), lax.axis_index("c")
        me, sibling = (mx, my, mc), (mx, my, 1 - mc)
        chips = [(1 - mx, my), (mx, 1 - my), (1 - mx, 1 - my)]

        def slab(px, py, pc):
            return buf.at[4 * px + 2 * py + pc]

        def copy(k, blk, to, src=None):
            return pltpu.make_async_remote_copy(
                src_ref=slab(*blk) if src is None else src, dst_ref=slab(*blk),
                send_sem=send_sems.at[k], recv_sem=recv_sems.at[k], device_id=to, device_id_type=MESH)

        first = [copy(0, me, sibling, src=x_ref)]
        first += [copy(1 + j, me, (*chip, mc), src=x_ref) for j, chip in enumerate(chips)]
        for cp in first:
            cp.start()
        passed = [copy(4 + j, (*chip, mc), sibling) for j, chip in enumerate(chips)]
        for j, chip in enumerate(chips):
            copy(1 + j, (*chip, mc), me).wait_recv()
            passed[j].start()
        copy(0, sibling, me).wait_recv()
        for j, chip in enumerate(chips):
            copy(4 + j, (*chip, 1 - mc), me).wait_recv()
        for cp in first + passed:
            cp.wait_send()
        buf[4 * mx + 2 * my + mc] = x_ref[...]
        acc = buf[0]
        for d in range(1, N_DEV):
            acc = acc + buf[d]
        out_ref[...] = acc

    return pl.pallas_call(
        body, name=name, out_shape=SDS((r, c), F32),
        in_specs=[BS(memory_space=pltpu.VMEM)], out_specs=BS(memory_space=pltpu.VMEM),
        scratch_shapes=[pltpu.VMEM((N_DEV, r, c), F32), pltpu.SemaphoreType.DMA((7,)), pltpu.SemaphoreType.DMA((7,))],
    )(block)


HBM_SPEC = BS(memory_space=pltpu.HBM)
SEM_SPEC = BS(memory_space=pltpu.SEMAPHORE)
SIDE_EFFECT = pltpu.SideEffectType.DATAFLOW_SIDE_EFFECTING


def _peer(rel):
    flip = lambda v, bit: 1 - v if bit else v
    return (flip(lax.axis_index("x"), rel >> 2 & 1), flip(lax.axis_index("y"), rel >> 1 & 1),
            flip(lax.axis_index("c"), rel & 1))


def _send_all_start(blocks, scatter, name, after=None):
    na = len(blocks)
    shapes = [b.shape[1:] if scatter else b.shape for b in blocks]
    n_in = 2 * na + (after is not None)

    def body(*refs):
        srcs, lands = refs[:na], refs[na:2 * na]
        send_sems, recv_sems, token = refs[n_in], refs[n_in + 1], refs[-1]
        me = 4 * lax.axis_index("x") + 2 * lax.axis_index("y") + lax.axis_index("c")
        for a in range(na):
            for rel in range(1, N_DEV):
                px, py, pc = _peer(rel)
                pltpu.make_async_remote_copy(
                    src_ref=srcs[a].at[4 * px + 2 * py + pc] if scatter else srcs[a], dst_ref=lands[a].at[me],
                    send_sem=send_sems.at[7 * a + rel - 1], recv_sem=recv_sems.at[7 * a + rel - 1],
                    device_id=(px, py, pc), device_id_type=MESH).start()
        token[...] = jnp.zeros_like(token)

    lands = [lax.empty((N_DEV, *s), b.dtype) for s, b in zip(shapes, blocks)]
    res = pl.pallas_call(
        body, name=name,
        out_shape=([pltpu.SemaphoreType.DMA((7 * na,)), pltpu.SemaphoreType.DMA((7 * na,))]
                   + [pltpu.HBM(b.shape, b.dtype) for b in blocks] + [pltpu.HBM(ld.shape, ld.dtype) for ld in lands]
                   + [SDS((SUBLANES, LANES), F32)]),
        in_specs=[HBM_SPEC] * (2 * na) + [BS(memory_space=pl.ANY)] * (after is not None),
        out_specs=[SEM_SPEC, SEM_SPEC] + [HBM_SPEC] * (2 * na) + [BS(memory_space=pltpu.VMEM)],
        input_output_aliases={i: 2 + i for i in range(2 * na)},
        compiler_params=pltpu.CompilerParams(has_side_effects=SIDE_EFFECT),
    )(*[pltpu.with_memory_space_constraint(b, pltpu.HBM) for b in blocks],
      *[pltpu.with_memory_space_constraint(ld, pltpu.HBM) for ld in lands], *([] if after is None else [after]))
    return dict(send=res[0], recv=res[1], srcs=res[2:2 + na], lands=res[2 + na:2 + 2 * na], scatter=scatter), res[-1]


def _send_all_wait(flight, after, name):
    na = len(flight["srcs"])

    def body(*refs):
        srcs, lands = refs[:na], refs[na:2 * na]
        send_sems, recv_sems = refs[2 * na], refs[2 * na + 1]
        for a in range(na):
            for rel in range(1, N_DEV):
                cp = pltpu.make_async_remote_copy(
                    src_ref=srcs[a].at[0] if flight["scatter"] else srcs[a], dst_ref=lands[a].at[0],
                    send_sem=send_sems.at[7 * a + rel - 1], recv_sem=recv_sems.at[7 * a + rel - 1],
                    device_id=_peer(rel), device_id_type=MESH)
                cp.wait_send()
                cp.wait_recv()

    arrays = list(flight["srcs"]) + list(flight["lands"])
    res = pl.pallas_call(
        body, name=name, out_shape=[pltpu.HBM(a.shape, a.dtype) for a in arrays],
        in_specs=[HBM_SPEC] * (2 * na) + [SEM_SPEC, SEM_SPEC, BS(memory_space=pl.ANY)], out_specs=[HBM_SPEC] * (2 * na),
        input_output_aliases={i: i for i in range(2 * na)},
        compiler_params=pltpu.CompilerParams(has_side_effects=SIDE_EFFECT),
    )(*arrays, flight["send"], flight["recv"], after)
    return res[:na], res[na:]


def _sum_slabs(land, name):
    _, r, c = land.shape
    tr = _pick(r, (256, 128, 64, 32, 16, 8))

    def body(l_ref, o_ref):
        acc = l_ref[0].astype(F32)
        for d in range(1, N_DEV):
            acc = acc + l_ref[d].astype(F32)
        o_ref[...] = acc

    return pl.pallas_call(
        body, name=name, grid=(r // tr,), out_shape=SDS((r, c), F32),
        in_specs=[BS((N_DEV, tr, c), lambda j: (0, j, 0))], out_specs=BS((tr, c), lambda j: (j, 0)),
        compiler_params=_params(1),
    )(land)


def _proj_fwd(h, nw8, wp, tag):
    n = h.shape[0]
    tm = _pick(n, (2112, 1408, 768, 512, 384, 256, 192, 128, 64))
    tn = 896

    def body(h_ref, nw_ref, w_ref, proj_ref, xn_ref):
        @pl.when(pl.program_id(1) == 0)
        def _():
            xn_ref[...] = _rms(h_ref[...], nw_ref[0:1, :]).astype(BF16)

        proj_ref[...] = jnp.dot(xn_ref[...], w_ref[...], preferred_element_type=F32)

    return pl.pallas_call(
        body, name=f"proj_fwd_{tag}", grid=(n // tm, PROJ_W // tn),
        in_specs=[BS((tm, D_MODEL), lambda i, j: (i, 0)), BS((SUBLANES, D_MODEL), lambda i, j: (0, 0)),
                  BS((D_MODEL, tn), lambda i, j: (0, j))],
        out_specs=[BS((tm, tn), lambda i, j: (i, j)), BS((tm, D_MODEL), lambda i, j: (i, 0))],
        out_shape=[SDS((n, PROJ_W), F32), SDS((n, D_MODEL), BF16)], compiler_params=_params(2),
    )(h, nw8, wp)


def _conv_ext(x_ext, cw_ref):
    y = x_ext * cw_ref[3:4, :]
    for k in range(3):
        y = y + _shift_down(x_ext, 3 - k) * cw_ref[k:k + 1, :]
    return y[SUBLANES:]


def _prep_fwd(proj, cw8, aux, lb8, nseq, t_len, tag):
    n = proj.shape[0]
    tt = _pick(t_len, (192, 128, 64))
    nt_ = t_len // tt
    qkv_w = 3 * HEADS_W

    def body(cur_ref, prev_ref, misc_ref, bq_ref, bf_ref, cw_ref, aux_ref, lb_ref,
             q_ref, k_ref, v_ref, b_ref, g_ref, qb_ref, kb_ref, lf_ref, ext_ref):
        t = pl.program_id(1)
        ext_ref[0:SUBLANES, :] = jnp.where(t == 0, 0.0, prev_ref[...])
        ext_ref[SUBLANES:, :] = cur_ref[...]
        y = ext_ref[SUBLANES:, :] * cw_ref[3:4, :]
        for kk in range(3):
            y = y + ext_ref[SUBLANES - 3 + kk:SUBLANES - 3 + kk + tt, :] * cw_ref[kk:kk + 1, :]
        real = (t * tt + _iota2((tt, 1), 0)) >= N_PAD
        misc = misc_ref[...]
        auxv = aux_ref[...]
        for hd in range(N_HEADS):
            sl = slice(hd * D_HEAD, (hd + 1) * D_HEAD)
            q_ref[:, sl] = _l2n_act(y[:, sl], D_HEAD ** -0.5)
            k_ref[:, sl] = _l2n_act(y[:, HEADS_W + hd * D_HEAD:HEADS_W + (hd + 1) * D_HEAD], 1.0)
            v_ref[:, sl] = _silu(y[:, 2 * HEADS_W + hd * D_HEAD:2 * HEADS_W + (hd + 1) * D_HEAD])
            b_ref[:, sl], g_ref[:, sl] = _gdn_gates(misc, auxv, real, hd)
        qb_ref[...], kb_ref[...], lf_ref[...] = _hgrn_prep(bq_ref[...], bf_ref[...], lb_ref[0:1, :], real)

    rb = tt // SUBLANES
    row = lambda s, t: s * nt_ + t
    wide = BS((tt, HEADS_W), lambda s, t: (row(s, t), 0))
    return pl.pallas_call(
        body, name=f"prep_fwd_{tag}", grid=(nseq, nt_),
        in_specs=[BS((tt, qkv_w), lambda s, t: (row(s, t), 0)),
                  BS((SUBLANES, qkv_w), lambda s, t: (jnp.maximum(row(s, t) * rb - 1, 0), 0)),
                  BS((tt, LANES), lambda s, t: (row(s, t), C_MISC // LANES)),
                  BS((tt, HEADS_W), lambda s, t: (row(s, t), C_BQ // HEADS_W)),
                  BS((tt, HEADS_W), lambda s, t: (row(s, t), C_BF // HEADS_W)),
                  BS((SUBLANES, qkv_w), lambda s, t: (0, 0)), BS((SUBLANES, LANES), lambda s, t: (0, 0)),
                  BS((SUBLANES, HEADS_W), lambda s, t: (0, 0))],
        out_specs=[wide] * 8, out_shape=[SDS((n, HEADS_W), F32)] * 8,
        scratch_shapes=[pltpu.VMEM((tt + SUBLANES, qkv_w), F32)], compiler_params=_params(2),
    )(proj, proj, proj, proj, proj, cw8, aux, lb8)


GDN_SEQS = 4
HGRN_SEQS = 2


def _seq_block(nseq, most):
    return max(s for s in (1, 2, 4) if s <= most and nseq % s == 0)


def _to_chains(x):
    return jnp.concatenate([x[:, :, hd * D_HEAD:(hd + 1) * D_HEAD] for hd in range(N_HEADS)], axis=0)


def _from_chains(ref, rows, val):
    sb = val.shape[0] // N_HEADS
    for hd in range(N_HEADS):
        ref[:, rows, hd * D_HEAD:(hd + 1) * D_HEAD] = val[hd * sb:(hd + 1) * sb].astype(ref.dtype)


def _mixers_fwd(q, k, v, b, g, qb, kb, vb, vb_col, lf, nseq, t_len, tag):
    sb, hs = _seq_block(nseq, GDN_SEQS), _seq_block(nseq, HGRN_SEQS)
    nc = t_len // GDN_CHUNK
    chains = N_HEADS * sb

    def body(q_ref, k_ref, v_ref, b_ref, g_ref, qb_ref, kb_ref, vb_ref, lf_ref, oa_ref, ob_ref, cka_ref, ckb_ref,
             sa_ref, sb_ref):
        @pl.when(pl.program_id(1) == 0)
        def _():
            sa_ref[...] = jnp.zeros_like(sa_ref)
            sb_ref[...] = jnp.zeros_like(sb_ref)

        s = sa_ref[...]
        cka_ref[...] = s
        o, s_new = _gdn_chunk(*[_to_chains(r[...]) for r in (q_ref, k_ref, v_ref, b_ref, g_ref)], s)
        _from_chains(oa_ref, slice(None), o)
        sa_ref[...] = s_new
        for part in range(sb // hs):
            seqs, ch = slice(part * hs, (part + 1) * hs), slice(part * N_HEADS * hs, (part + 1) * N_HEADS * hs)
            s = sb_ref[ch]
            ckb_ref[ch] = s
            o, s_new = _hgrn_block(*[_to_chains(r[seqs]) for r in (qb_ref, kb_ref, vb_ref, lf_ref)], s)
            for hd in range(N_HEADS):
                ob_ref[seqs, :, hd * D_HEAD:(hd + 1) * D_HEAD] = o[hd * hs:(hd + 1) * hs]
            sb_ref[ch] = s_new

    blk = lambda cb: BS((sb, GDN_CHUNK, HEADS_W), lambda p, c: (p, c, cb))
    ck_spec = BS((None, None, chains, D_HEAD, D_HEAD), lambda p, c: (p, c, 0, 0, 0))
    ck_shape = SDS((nseq // sb, nc, chains, D_HEAD, D_HEAD), F32)
    view = lambda a: a.reshape(nseq, t_len, a.shape[1])
    oa, ob, cka, ckb = pl.pallas_call(
        body, name=f"mixers_fwd_{tag}", grid=(nseq // sb, nc),
        in_specs=[blk(0)] * 7 + [blk(vb_col), blk(0)], out_specs=[blk(0), blk(0), ck_spec, ck_spec],
        out_shape=[SDS((nseq, t_len, HEADS_W), F32)] * 2 + [ck_shape] * 2,
        scratch_shapes=[pltpu.VMEM((chains, D_HEAD, D_HEAD), F32)] * 2, compiler_params=_params(2),
    )(*[view(a) for a in (q, k, v, b, g, qb, kb, vb, lf)])
    return oa.reshape(-1, HEADS_W), ob.reshape(-1, HEADS_W), cka, ckb


def _mixers_bwd(q, k, v, b, g, qb, kb, vb, vb_col, lf, cka, ckb, doa, dob, nseq, t_len, tag):
    sb, hs = _seq_block(nseq, GDN_SEQS), _seq_block(nseq, HGRN_SEQS)
    nc = t_len // GDN_CHUNK
    chains = N_HEADS * sb

    def body(q_ref, k_ref, v_ref, b_ref, g_ref, qb_ref, kb_ref, vb_ref, lf_ref, doa_ref, dob_ref, cka_ref, ckb_ref,
             dq_ref, dk_ref, dv_ref, db_ref, dg_ref, dqb_ref, dkb_ref, dvb_ref, dlf_ref, dsa_ref, dsb_ref):
        @pl.when(pl.program_id(1) == 0)
        def _():
            dsa_ref[...] = jnp.zeros_like(dsa_ref)
            dsb_ref[...] = jnp.zeros_like(dsb_ref)

        _, vjp = jax.vjp(_gdn_chunk, *[_to_chains(r[...]) for r in (q_ref, k_ref, v_ref, b_ref, g_ref)], cka_ref[...])
        grads = vjp((_to_chains(doa_ref[...]), dsa_ref[...]))
        for ref, val in zip((dq_ref, dk_ref, dv_ref, db_ref, dg_ref), grads[:5]):
            _from_chains(ref, slice(None), val)
        dsa_ref[...] = grads[5]
        for part in range(sb // hs):
            seqs, ch = slice(part * hs, (part + 1) * hs), slice(part * N_HEADS * hs, (part + 1) * N_HEADS * hs)
            _, vjp = jax.vjp(functools.partial(_hgrn_block, group=SUBLANES),
                             *[_to_chains(r[seqs]) for r in (qb_ref, kb_ref, vb_ref, lf_ref)], ckb_ref[ch])
            grads = vjp((_to_chains(dob_ref[seqs]), dsb_ref[ch]))
            for ref, val in zip((dqb_ref, dkb_ref, dvb_ref, dlf_ref), grads[:4]):
                for hd in range(N_HEADS):
                    ref[seqs, :, hd * D_HEAD:(hd + 1) * D_HEAD] = val[hd * hs:(hd + 1) * hs].astype(ref.dtype)
            dsb_ref[ch] = grads[4]

    blk = lambda cb: BS((sb, GDN_CHUNK, HEADS_W), lambda p, c: (p, nc - 1 - c, cb))
    ck_spec = BS((None, None, chains, D_HEAD, D_HEAD), lambda p, c: (p, nc - 1 - c, 0, 0, 0))
    view = lambda a: a.reshape(nseq, t_len, a.shape[1])
    dts = [F32] * 7 + [BF16, F32]
    res = pl.pallas_call(
        body, name=f"mixers_bwd_{tag}", grid=(nseq // sb, nc),
        in_specs=[blk(0)] * 7 + [blk(vb_col), blk(0), blk(0), blk(0), ck_spec, ck_spec], out_specs=[blk(0)] * 9,
        out_shape=[SDS((nseq, t_len, HEADS_W), dt) for dt in dts],
        scratch_shapes=[pltpu.VMEM((chains, D_HEAD, D_HEAD), F32)] * 2, compiler_params=_params(2),
    )(*[view(a) for a in (q, k, v, b, g, qb, kb, vb, lf, doa, dob)], cka, ckb)
    return [r.reshape(-1, HEADS_W) for r in res]


def _post_values(oa_ref, ob_ref, z_ref, bg_ref, ga_ref, gb_ref, gn_ref, wa_ref, wb_ref, ya_ref, yb_ref):
    for hd in range(N_HEADS):
        sl = slice(hd * D_HEAD, (hd + 1) * D_HEAD)
        ya_ref[:, sl] = _gated_norm(oa_ref[:, sl], z_ref[:, sl], gn_ref[0:1, :]).astype(BF16)
        yb_ref[:, sl] = _gated_norm(ob_ref[:, sl], bg_ref[:, sl], gn_ref[1:2, :]).astype(BF16)
    pa = jnp.dot(ya_ref[...], wa_ref[...], preferred_element_type=F32)
    pb = jnp.dot(yb_ref[...], wb_ref[...], preferred_element_type=F32)
    return pa, pb, _sigmoid(ga_ref[...]), _sigmoid(gb_ref[...])


def _post_specs(tm):
    r2 = lambda i: (i, 0)
    return [BS((tm, HEADS_W), r2), BS((tm, HEADS_W), r2),
            BS((tm, HEADS_W), lambda i: (i, C_Z // HEADS_W)), BS((tm, HEADS_W), lambda i: (i, C_BG // HEADS_W)),
            BS((tm, D_MODEL), lambda i: (i, C_GA // D_MODEL)), BS((tm, D_MODEL), lambda i: (i, C_GB // D_MODEL)),
            BS((tm, D_MODEL), r2), BS((SUBLANES, LANES), lambda i: (0, 0))]


def _post_fwd(oa, ob, proj, h, gn8, wa, wb, wout, tag):
    n = h.shape[0]
    tm = _pick(n, (768, 384, 256, 192, 128, 64))

    def body(oa_ref, ob_ref, z_ref, bg_ref, ga_ref, gb_ref, h_ref, gn_ref, wa_ref, wb_ref, wout_ref, out_ref,
             ya_ref, yb_ref):
        pa, pb, sa, sb = _post_values(oa_ref, ob_ref, z_ref, bg_ref, ga_ref, gb_ref, gn_ref, wa_ref, wb_ref,
                                      ya_ref, yb_ref)
        mixed = (sa * pa + sb * pb).astype(BF16)
        out_ref[...] = h_ref[...] + jnp.dot(mixed, wout_ref[...], preferred_element_type=F32)

    full = lambda i: (0, 0)
    return pl.pallas_call(
        body, name=f"post_fwd_{tag}", grid=(n // tm,),
        in_specs=_post_specs(tm) + [BS((HEADS_W, D_MODEL), full), BS((HEADS_W, D_MODEL), full),
                                    BS((D_MODEL, D_MODEL), full)],
        out_specs=BS((tm, D_MODEL), lambda i: (i, 0)), out_shape=SDS((n, D_MODEL), F32),
        scratch_shapes=[pltpu.VMEM((tm, HEADS_W), BF16), pltpu.VMEM((tm, HEADS_W), BF16)], compiler_params=_params(1),
    )(oa, ob, proj, proj, proj, proj, h, gn8, wa, wb, wout)


def _post_bwd(dh, oa, ob, proj, h, gn8, wa, wb, wout, tag):
    n = h.shape[0]
    tm = _pick(n, (256, 192, 128, 64))

    def body(dh_ref, oa_ref, ob_ref, z_ref, bg_ref, ga_ref, gb_ref, h_ref, gn_ref, wa_ref, wb_ref, wout_ref,
             doa_ref, dob_ref, dz_ref, dbg_ref, dga_ref, dgb_ref, dwa_ref, dwb_ref, dwout_ref, dgn_ref,
             ya_ref, yb_ref):
        @pl.when(pl.program_id(0) == 0)
        def _():
            dwa_ref[...] = jnp.zeros_like(dwa_ref)
            dwb_ref[...] = jnp.zeros_like(dwb_ref)
            dwout_ref[...] = jnp.zeros_like(dwout_ref)
            dgn_ref[...] = jnp.zeros_like(dgn_ref)

        pa, pb, sa, sb = _post_values(oa_ref, ob_ref, z_ref, bg_ref, ga_ref, gb_ref, gn_ref, wa_ref, wb_ref,
                                      ya_ref, yb_ref)
        mixed = (sa * pa + sb * pb).astype(BF16)
        dout = dh_ref[...].astype(BF16)
        dwout_ref[...] += _dg(mixed, dout, ((0,), (0,)))
        dmixed = _dg(dout, wout_ref[...], ((1,), (1,)))
        dga_ref[...] = (dmixed * pa * sa * (1.0 - sa)).astype(BF16)
        dgb_ref[...] = (dmixed * pb * sb * (1.0 - sb)).astype(BF16)
        dpa = (dmixed * sa).astype(BF16)
        dpb = (dmixed * sb).astype(BF16)
        dwa_ref[...] += _dg(ya_ref[...], dpa, ((0,), (0,)))
        dwb_ref[...] += _dg(yb_ref[...], dpb, ((0,), (0,)))
        dya = _dg(dpa, wa_ref[...], ((1,), (1,)))
        dyb = _dg(dpb, wb_ref[...], ((1,), (1,)))
        dgn_a = jnp.zeros((1, D_HEAD), F32)
        dgn_b = jnp.zeros((1, D_HEAD), F32)
        for hd in range(N_HEADS):
            sl = slice(hd * D_HEAD, (hd + 1) * D_HEAD)
            _, vjp = jax.vjp(_gated_norm, oa_ref[:, sl], z_ref[:, sl], gn_ref[0:1, :])
            doa, dz, dgw = vjp(dya[:, sl])
            doa_ref[:, sl], dz_ref[:, sl], dgn_a = doa, dz.astype(BF16), dgn_a + dgw
            _, vjp = jax.vjp(_gated_norm, ob_ref[:, sl], bg_ref[:, sl], gn_ref[1:2, :])
            dob, dbg, dgw = vjp(dyb[:, sl])
            dob_ref[:, sl], dbg_ref[:, sl], dgn_b = dob, dbg.astype(BF16), dgn_b + dgw
        dgn_ref[0:1, :] += dgn_a
        dgn_ref[1:2, :] += dgn_b

    full = lambda i: (0, 0)
    r2 = lambda i: (i, 0)
    return pl.pallas_call(
        body, name=f"post_bwd_{tag}", grid=(n // tm,),
        in_specs=[BS((tm, D_MODEL), r2)] + _post_specs(tm) + [
            BS((HEADS_W, D_MODEL), full), BS((HEADS_W, D_MODEL), full), BS((D_MODEL, D_MODEL), full)],
        out_specs=[BS((tm, HEADS_W), r2)] * 4 + [BS((tm, D_MODEL), r2)] * 2 + [
            BS((HEADS_W, D_MODEL), full), BS((HEADS_W, D_MODEL), full), BS((D_MODEL, D_MODEL), full),
            BS((SUBLANES, LANES), full)],
        out_shape=[SDS((n, HEADS_W), F32), SDS((n, HEADS_W), F32), SDS((n, HEADS_W), BF16), SDS((n, HEADS_W), BF16),
                   SDS((n, D_MODEL), BF16), SDS((n, D_MODEL), BF16), SDS((HEADS_W, D_MODEL), F32),
                   SDS((HEADS_W, D_MODEL), F32), SDS((D_MODEL, D_MODEL), F32), SDS((SUBLANES, LANES), F32)],
        scratch_shapes=[pltpu.VMEM((tm, HEADS_W), BF16), pltpu.VMEM((tm, HEADS_W), BF16)], compiler_params=_params(1),
    )(dh, oa, ob, proj, proj, proj, proj, h, gn8, wa, wb, wout)


def _loss_head(h, fw8, target, nseq, t_len):
    n = h.shape[0]
    nc = t_len // GDN_CHUNK
    sub = _pick(nc, (11, 3, 1))
    tl, nt = sub * GDN_CHUNK, nc // sub
    inv_d = 1.0 / D_MODEL

    def body(h_ref, fw_ref, *rest):
        tgt_refs, (dh_ref, acc_ref) = rest[:sub], rest[sub:]

        @pl.when((pl.program_id(0) == 0) & (pl.program_id(1) == 0))
        def _():
            acc_ref[...] = jnp.zeros_like(acc_ref)

        frames = ((pl.program_id(1) * tl + _iota2((tl, 1), 0)) >= N_PAD + N_META).astype(F32)
        y, vjp = jax.vjp(_rms, h_ref[...], fw_ref[0:1, :])
        err = (y - jnp.concatenate([r[...] for r in tgt_refs], axis=0)) * frames
        dx, dfw = vjp(err * inv_d)
        dh_ref[...] = dx
        acc_ref[0:1, :] += dfw
        acc_ref[1:2, :] += (0.5 * inv_d) * jnp.sum(err * err, axis=0, keepdims=True)

    tgt_spec = lambda u: BS((None, GDN_CHUNK, D_MODEL), lambda s, t: (s, jnp.maximum(t * sub + u - 1, 0), 0))
    return pl.pallas_call(
        body, name="loss_head", grid=(nseq, nt),
        in_specs=[BS((tl, D_MODEL), lambda s, t: (s * nt + t, 0)), BS((SUBLANES, D_MODEL), lambda s, t: (0, 0))]
        + [tgt_spec(u) for u in range(sub)],
        out_specs=[BS((tl, D_MODEL), lambda s, t: (s * nt + t, 0)), BS((SUBLANES, D_MODEL), lambda s, t: (0, 0))],
        out_shape=[SDS((n, D_MODEL), F32), SDS((SUBLANES, D_MODEL), F32)], compiler_params=_params(2),
    )(h, fw8, *[target] * sub)


def _prep_bwd(proj, dq, dk, dv, db, dg, dqb, dkb, dlf, cw8, aux, lb8, nseq, t_len, tag):
    n = proj.shape[0]
    tt = _pick(t_len, (192, 128, 64))
    nt_ = t_len // tt
    qkv_w = 3 * HEADS_W
    rb = tt // SUBLANES
    ext = tt + SUBLANES

    def body(cur_ref, prev_ref, next_ref, misc_ref, bq_ref, bf_ref, dq_ref, dqn_ref, dk_ref, dkn_ref, dv_ref, dvn_ref,
             db_ref, dg_ref, dqb_ref, dkb_ref, dlf_ref, cw_ref, aux_ref, lb_ref,
             dqkv_ref, dmisc_ref, dbq_ref, dbf_ref, dcw_ref, daux_ref, dlb_ref, dy_ref):
        s, t = pl.program_id(0), pl.program_id(1)

        @pl.when((s == 0) & (t == 0))
        def _():
            dcw_ref[...] = jnp.zeros_like(dcw_ref)
            daux_ref[...] = jnp.zeros_like(daux_ref)
            dlb_ref[...] = jnp.zeros_like(dlb_ref)

        prev = jnp.where(t == 0, 0.0, prev_ref[...])
        x_ext = jnp.concatenate([prev, cur_ref[...], next_ref[...]], axis=0)
        y = _conv_ext(x_ext, cw_ref)
        inside = (t < nt_ - 1) | (_iota2((ext, 1), 0) < tt)
        dy_ref[0:SUBLANES, :] = jnp.zeros((SUBLANES, qkv_w), F32)
        for hd in range(N_HEADS):
            for grp, (g_ref, gn_ref, scale) in enumerate(((dq_ref, dqn_ref, D_HEAD ** -0.5), (dk_ref, dkn_ref, 1.0),
                                                          (dv_ref, dvn_ref, None))):
                lo = grp * HEADS_W + hd * D_HEAD
                sl = slice(hd * D_HEAD, (hd + 1) * D_HEAD)
                cot = jnp.concatenate([g_ref[:, sl], gn_ref[:, sl]], axis=0)
                fn = _silu if scale is None else functools.partial(_l2n_act, scale=scale)
                _, vjp = jax.vjp(fn, y[:, lo:lo + D_HEAD])
                dy_ref[SUBLANES:, lo:lo + D_HEAD] = jnp.where(inside, vjp(cot)[0], 0.0)
        dy_ext = dy_ref[...]
        dx = dy_ext * cw_ref[3:4, :]
        for kk in range(3):
            dx = dx + _shift_up(dy_ext, 3 - kk) * cw_ref[kk:kk + 1, :]
        dqkv_ref[...] = dx[SUBLANES:SUBLANES + tt].astype(BF16)
        dy_cur = dy_ext[SUBLANES:SUBLANES + tt]
        for kk in range(4):
            xs = _shift_down(x_ext, 3 - kk)[SUBLANES:SUBLANES + tt]
            dcw_ref[kk:kk + 1, :] += jnp.sum(xs * dy_cur, axis=0, keepdims=True)

        real = (t * tt + _iota2((tt, 1), 0)) >= N_PAD
        dmisc = jnp.zeros((tt, LANES), F32)
        daux = jnp.zeros((SUBLANES, LANES), F32)
        for hd in range(N_HEADS):
            sl = slice(hd * D_HEAD, (hd + 1) * D_HEAD)
            _, vjp = jax.vjp(lambda m, a: _gdn_gates(m, a, real, hd), misc_ref[...], aux_ref[...])
            dm, da = vjp((db_ref[:, sl], dg_ref[:, sl]))
            dmisc, daux = dmisc + dm, daux + da
        dmisc_ref[...] = dmisc.astype(BF16)
        daux_ref[...] += daux
        _, vjp = jax.vjp(lambda a, b, c: _hgrn_prep(a, b, c, real), bq_ref[...], bf_ref[...], lb_ref[0:1, :])
        dbq, dbf, dlb = vjp((dqb_ref[...], dkb_ref[...], dlf_ref[...]))
        dbq_ref[...], dbf_ref[...] = dbq.astype(BF16), dbf.astype(BF16)
        dlb_ref[0:1, :] += dlb

    row = lambda s, t: s * nt_ + t
    cur = lambda s, t: (row(s, t), 0)
    nxt = lambda s, t: (jnp.minimum((row(s, t) + 1) * rb, n // SUBLANES - 1), 0)
    wide = BS((tt, HEADS_W), cur)
    halo = BS((SUBLANES, HEADS_W), nxt)
    full = lambda s, t: (0, 0)
    return pl.pallas_call(
        body, name=f"prep_bwd_{tag}", grid=(nseq, nt_),
        in_specs=[BS((tt, qkv_w), cur), BS((SUBLANES, qkv_w), lambda s, t: (jnp.maximum(row(s, t) * rb - 1, 0), 0)),
                  BS((SUBLANES, qkv_w), nxt), BS((tt, LANES), lambda s, t: (row(s, t), C_MISC // LANES)),
                  BS((tt, HEADS_W), lambda s, t: (row(s, t), C_BQ // HEADS_W)),
                  BS((tt, HEADS_W), lambda s, t: (row(s, t), C_BF // HEADS_W)),
                  wide, halo, wide, halo, wide, halo, wide, wide, wide, wide, wide,
                  BS((SUBLANES, qkv_w), full), BS((SUBLANES, LANES), full), BS((SUBLANES, HEADS_W), full)],
        out_specs=[BS((tt, qkv_w), cur), BS((tt, LANES), cur), wide, wide,
                   BS((SUBLANES, qkv_w), full), BS((SUBLANES, LANES), full), BS((SUBLANES, HEADS_W), full)],
        out_shape=[SDS((n, qkv_w), BF16), SDS((n, LANES), BF16), SDS((n, HEADS_W), BF16), SDS((n, HEADS_W), BF16),
                   SDS((SUBLANES, qkv_w), F32), SDS((SUBLANES, LANES), F32), SDS((SUBLANES, HEADS_W), F32)],
        scratch_shapes=[pltpu.VMEM((tt + 2 * SUBLANES, qkv_w), F32)], compiler_params=_params(2),
    )(proj, proj, proj, proj, proj, proj, dq, dq, dk, dk, dv, dv, db, dg, dqb, dkb, dlf, cw8, aux, lb8)


def _proj_bwd_x(pieces, wp, h, nw8, dh_res, tag):
    n = h.shape[0]
    tm = _pick(n, (384, 256, 192, 128, 64))
    widths = [p.shape[1] for p in pieces]
    assert sum(widths) == PROJ_W

    def body(*refs):
        p_refs = refs[:len(pieces)]
        w_ref, h_ref, nw_ref, dres_ref, dh_ref, dnw_ref = refs[len(pieces):]

        @pl.when(pl.program_id(0) == 0)
        def _():
            dnw_ref[...] = jnp.zeros_like(dnw_ref)

        dxn, off = None, 0
        for p_ref, w in zip(p_refs, widths):
            part = _dg(p_ref[...], w_ref[:, off:off + w], ((1,), (1,)))
            dxn = part if dxn is None else dxn + part
            off += w
        _, vjp = jax.vjp(_rms, h_ref[...], nw_ref[0:1, :])
        dx, dnw = vjp(dxn)
        dh_ref[...] = dres_ref[...] + dx
        dnw_ref[0:1, :] += dnw

    r2 = lambda i: (i, 0)
    full = lambda i: (0, 0)
    return pl.pallas_call(
        body, name=f"proj_bwd_x_{tag}", grid=(n // tm,),
        in_specs=[BS((tm, w), r2) for w in widths] + [BS((D_MODEL, PROJ_W), full), BS((tm, D_MODEL), r2),
                                                      BS((SUBLANES, D_MODEL), full), BS((tm, D_MODEL), r2)],
        out_specs=[BS((tm, D_MODEL), r2), BS((SUBLANES, D_MODEL), full)],
        out_shape=[SDS((n, D_MODEL), F32), SDS((SUBLANES, D_MODEL), F32)], compiler_params=_params(1),
    )(*pieces, wp, h, nw8, dh_res)


def _proj_bwd_w(xn, pieces, tag):
    n = xn.shape[0]
    tm = _pick(n, (384, 256, 192, 128, 64))
    widths = [p.shape[1] for p in pieces]
    assert sum(widths) == PROJ_W

    def body(*refs):
        x_ref, p_refs = refs[0], refs[1:1 + len(pieces)]
        o_ref, acc_ref = refs[1 + len(pieces):]

        @pl.when(pl.program_id(0) == 0)
        def _():
            acc_ref[...] = jnp.zeros_like(acc_ref)

        off = 0
        for p_ref, w in zip(p_refs, widths):
            acc_ref[:, off:off + w] += _dg(x_ref[...], p_ref[...], ((0,), (0,)))
            off += w

        @pl.when(pl.program_id(0) == pl.num_programs(0) - 1)
        def _():
            pltpu.sync_copy(acc_ref, o_ref)

    r2 = lambda i: (i, 0)
    return pl.pallas_call(
        body, name=f"proj_bwd_w_{tag}", grid=(n // tm,),
        in_specs=[BS((tm, D_MODEL), r2)] + [BS((tm, w), r2) for w in widths], out_specs=BS(memory_space=pl.ANY),
        out_shape=SDS((D_MODEL, PROJ_W), F32), scratch_shapes=[pltpu.VMEM((D_MODEL, PROJ_W), F32)],
        compiler_params=_params(1),
    )(xn, *pieces)


def _adamw(w, g, m, v, name):
    lead, rows, cols = w.shape
    tr = _pick(rows, (256, 128, 64, 32, 16, 8, 4, 2, 1)) if rows > 256 else rows

    def body(w_ref, g_ref, m_ref, v_ref, d_ref, nm_ref, nv_ref):
        gr = g_ref[...]
        m_new = ADAM_B1 * m_ref[...] + (1.0 - ADAM_B1) * gr
        v_new = ADAM_B2 * v_ref[...] + (1.0 - ADAM_B2) * jnp.square(gr)
        m_hat = m_new / (1.0 - ADAM_B1 ** ADAM_STEP)
        v_hat = v_new / (1.0 - ADAM_B2 ** ADAM_STEP)
        d_ref[...] = -ADAM_LR * (m_hat / (jnp.sqrt(v_hat) + ADAM_EPS) + ADAM_WD * w_ref[...])
        nm_ref[...] = m_new
        nv_ref[...] = v_new

    blk = BS((None, tr, cols), lambda a, i: (a, i, 0))
    return pl.pallas_call(
        body, name=name, grid=(lead, rows // tr), in_specs=[blk] * 4, out_specs=[blk] * 3,
        out_shape=[SDS((lead, rows, cols), F32)] * 3, compiler_params=_params(2),
    )(w, g, m, v)


def _row8(v, width):
    v = jnp.atleast_2d(v).astype(F32)
    return jnp.pad(v, ((0, SUBLANES - v.shape[0]), (0, width - v.shape[1])))


REF_MISC = 1536
N_MISC = 2 * N_HEADS
LAYOUT_RUNS = ((0, REF_MISC, 0), (REF_MISC + N_MISC, REF_W, REF_MISC), (REF_MISC, REF_MISC + N_MISC, C_MISC))


def _to_layout(slabs, tag):
    n_slabs, rows, width = slabs.shape
    tr = _pick(rows, (256, 128, 64, 32, 16))

    def body(x_ref, o_ref):
        off = 0
        for lo, hi, _ in sorted(LAYOUT_RUNS, key=lambda run: run[2]):
            for j in range(n_slabs):
                a, b = max(lo, j * width), min(hi, (j + 1) * width)
                if a < b:
                    o_ref[:, off:off + b - a] = x_ref[j, :, a - j * width:b - j * width]
                    off += b - a
        o_ref[:, off:] = jnp.zeros((tr, PROJ_W - off), o_ref.dtype)

    return pl.pallas_call(
        body, name=f"weights_layout_{tag}", grid=(rows // tr,), out_shape=SDS((rows, PROJ_W), slabs.dtype),
        in_specs=[BS((n_slabs, tr, width), lambda i: (0, i, 0))], out_specs=BS((tr, PROJ_W), lambda i: (i, 0)),
        compiler_params=_params(1),
    )(slabs)


def _from_layout(dw, n_slabs):
    width = REF_W // n_slabs
    slabs = []
    for j in range(n_slabs):
        pieces = []
        for lo, hi, at in sorted(LAYOUT_RUNS):
            a, b = max(lo, j * width), min(hi, (j + 1) * width)
            if a < b:
                pieces.append(dw[:, at + a - lo:at + b - lo])
        slabs.append(jnp.concatenate(pieces, axis=1))
    return slabs


def _lower_bounds(lb):
    sm = jax.nn.softmax(lb.astype(F32), axis=0)
    return jnp.cumsum(sm, axis=0) - sm[0]


def kernel(x, meta_tokens, norm_w, w_in, conv_w, a_log, dt_bias, gnorm_a, gnorm_b, hgrn_lower_bounds, w_branch_a, w_branch_b, w_out, final_norm_w, loss_target, m_meta_tokens, m_norm_w, m_w_in, m_conv_w, m_a_log, m_dt_bias, m_gnorm_a, m_gnorm_b, m_hgrn_lower_bounds, m_w_branch_a, m_w_branch_b, m_w_out, m_final_norm_w, v_meta_tokens, v_norm_w, v_w_in, v_conv_w, v_a_log, v_dt_bias, v_gnorm_a, v_gnorm_b, v_hgrn_lower_bounds, v_w_branch_a, v_w_branch_b, v_w_out, v_final_norm_w):
    nseq, seq, _ = x.shape
    depth = norm_w.shape[0]
    t_len = N_PAD + N_META + seq
    n = nseq * t_len
    conv_c = conv_w.shape[2]
    my = 4 * lax.axis_index("x") + 2 * lax.axis_index("y") + lax.axis_index("c")

    assert depth >= 2
    by_cols = lambda g: g.transpose(1, 2, 0, 3).reshape(g.shape[1], g.shape[2], N_DEV * g.shape[3])
    first = _all_gather_hbm([w_in[:1].astype(BF16), conv_w, meta_tokens], "gather_first")
    later_flight, later_token = _send_all_start(
        [w_in[1:].astype(BF16), w_branch_a.astype(BF16), w_branch_b.astype(BF16), w_out.astype(BF16)], False,
        "gather_later_start", after=first[0])
    w_in_slabs = [first[0]]
    conv_full = by_cols(first[1])
    meta_full = first[2].transpose(1, 0, 2).reshape(N_META, D_MODEL)

    lb_all, lb_vjp = jax.vjp(_lower_bounds, hgrn_lower_bounds)

    h = jnp.concatenate([jnp.zeros((nseq, N_PAD, D_MODEL), F32),
                         jnp.broadcast_to(meta_full[None], (nseq, N_META, D_MODEL)), x], axis=1).reshape(n, D_MODEL)
    saved = []
    for l in range(depth):
        wp = _to_layout(w_in_slabs[0][:, 0] if l == 0 else w_in_slabs[1][:, l - 1], l)
        nw8 = _row8(norm_w[l], D_MODEL)
        if l == 0:
            nw8 = nw8 + later_token[0:1, 0:1]
        cw8 = _row8(conv_full[l], 3 * HEADS_W)
        aux = _row8(jnp.stack([a_log[l], dt_bias[l]]), LANES)
        lb8 = _row8(lb_all[l], HEADS_W)
        gn8 = _row8(jnp.stack([gnorm_a[l], gnorm_b[l]]), LANES)
        proj, xn = _proj_fwd(h, nw8, wp, l)
        q, k, v, b, g, qb, kb, lf = _prep_fwd(proj, cw8, aux, lb8, nseq, t_len, l)
        oa, ob, sck_a, sck_b = _mixers_fwd(q, k, v, b, g, qb, kb, proj, C_BI // HEADS_W, lf, nseq, t_len, l)
        if l == 0:
            sent, landed = _send_all_wait(later_flight, ob, "gather_later_wait")
            landed = [lax.dynamic_update_slice(ld, own[None], (my,) + (0,) * own.ndim) for ld, own in zip(landed, sent)]
            w_in_slabs.append(landed[0])
            wa_full, wb_full = by_cols(landed[1]), by_cols(landed[2])
            wout_full = landed[3].transpose(1, 0, 2, 3).reshape(depth, D_MODEL, D_MODEL)
        wa_l, wb_l, wout_l = wa_full[l], wb_full[l], wout_full[l]
        h_next = _post_fwd(oa, ob, proj, h, gn8, wa_l, wb_l, wout_l, l)
        saved.append(dict(h=h, wp=wp, nw8=nw8, cw8=cw8, aux=aux, lb8=lb8, gn8=gn8, proj=proj, xn=xn, q=q, k=k, v=v, b=b,
                          wa=wa_l, wb=wb_l, wout=wout_l,
                          g=g, qb=qb, kb=kb, lf=lf, oa=oa, ob=ob, sck_a=sck_a, sck_b=sck_b))
        h = h_next

    dh, acc = _loss_head(h, _row8(final_norm_w, D_MODEL), loss_target, nseq, t_len)

    g_win, g_wa, g_wb, g_wout, g_conv, small = [], [], [], [], [], []

    def mixer_slabs(dwa_s, dwb_s, dwout_s):
        nl = len(dwa_s)
        rows = lambda a: jnp.stack(a).reshape(nl * HEADS_W, N_DEV, LANES).transpose(1, 0, 2)
        wout = jnp.stack(dwout_s).reshape(nl, N_DEV, LANES, D_MODEL).transpose(1, 0, 2, 3)
        return [jnp.concatenate([rows(dwa_s), rows(dwb_s)], axis=1).astype(BF16),
                wout.reshape(N_DEV, nl * LANES, D_MODEL).astype(BF16)]

    def win_slabs(per_layer, dtype):
        return jnp.stack([jnp.concatenate([sl[j] for sl in per_layer], axis=0) for j in range(N_DEV)]).astype(dtype)

    for l in reversed(range(depth)):
        s = saved[l]
        gn8, aux = s["gn8"], s["aux"]
        if l == 0:
            later_flight, later_token = _send_all_start(
                [win_slabs(g_win[::-1], BF16)] + mixer_slabs(g_wa[::-1], g_wb[::-1], g_wout[::-1]), True,
                "scatter_later_start")
            gn8 = gn8 + later_token[0:1, 0:1]
        doa, dob, dz, dbg, dga, dgb, dwa, dwb, dwout, dgn = _post_bwd(
            dh, s["oa"], s["ob"], s["proj"], s["h"], gn8, s["wa"], s["wb"], s["wout"], l)
        dq, dk, dv, db, dg, dqb, dkb, dbi, dlf = _mixers_bwd(
            s["q"], s["k"], s["v"], s["b"], s["g"], s["qb"], s["kb"], s["proj"], C_BI // HEADS_W, s["lf"], s["sck_a"],
            s["sck_b"], doa, dob, nseq, t_len, l)
        if l == 0:
            mixer_flight, mixer_token = _send_all_start(mixer_slabs([dwa], [dwb], [dwout]), True, "scatter_first_start")
            aux = aux + mixer_token[0:1, 0:1]
        dqkv, dmisc, dbq, dbf, dcw, daux, dlb = _prep_bwd(s["proj"], dq, dk, dv, db, dg, dqb, dkb, dlf, s["cw8"], aux,
                                                          s["lb8"], nseq, t_len, l)
        pieces = [dqkv, dz, dbq, dbf, dbi, dbg, dga, dgb, dmisc]
        g_win.append(_from_layout(_proj_bwd_w(s["xn"], pieces, l), N_DEV))
        g_conv.append(dcw[:4])
        nw8 = s["nw8"]
        if l == 0:
            dconv = jnp.stack(g_conv[::-1])
            conv_slabs = dconv.reshape(depth * dconv.shape[1], N_DEV, conv_c).transpose(1, 0, 2)
            win_flight, win_token = _send_all_start([win_slabs(g_win[-1:], BF16), conv_slabs], True, "scatter_win_start")
            nw8 = nw8 + win_token[0:1, 0:1]
        dh, dnw = _proj_bwd_x(pieces, s["wp"], s["h"], nw8, dh, l)
        g_wa.append(dwa)
        g_wb.append(dwb)
        g_wout.append(dwout)
        small.append((dnw[0], dgn[0], dgn[1], daux[0, :N_HEADS], daux[1, :N_HEADS], dlb[0]))
    small.reverse()
    dh = dh.reshape(nseq, t_len, D_MODEL)
    grad_x = dh[:, N_PAD + N_META:]

    packed = jnp.concatenate([small[0][1], small[1][1], small[0][2], small[1][2], small[0][3], small[1][3],
                              small[0][4], small[1][4]])
    tile = jnp.concatenate([
        jnp.sum(dh[:, N_PAD:N_PAD + N_META], axis=0), _row8(jnp.stack([small[0][0], small[1][0], acc[0]]), D_MODEL),
        _row8(jnp.stack([small[0][5], small[1][5]]), D_MODEL), _row8(packed, D_MODEL), _row8(acc[1], D_MODEL)], axis=0)
    tile = _all_reduce_small(tile, "reduce_small")
    loss = jnp.sum(tile[40])
    g_meta = lax.dynamic_slice_in_dim(tile[0:N_META], my * LANES, LANES, axis=1)
    g_norm, g_final = tile[16:18], tile[18]
    (g_lb,) = lb_vjp(tile[24:26, :HEADS_W])
    r21 = tile[32]
    g_gna, g_gnb = r21[0:256].reshape(2, LANES), r21[256:512].reshape(2, LANES)
    g_alog, g_dtb = r21[512:520].reshape(2, N_HEADS), r21[520:528].reshape(2, N_HEADS)

    def landed_sums(flight, tag):
        sent, landed = _send_all_wait(flight, dh, f"{tag}_wait")
        landed = [lax.dynamic_update_slice(ld, lax.dynamic_index_in_dim(src, my, 0, keepdims=True), (my, 0, 0))
                  for ld, src in zip(landed, sent)]
        return [_sum_slabs(ld, f"{tag}_sum{i}") for i, ld in enumerate(landed)]

    l_win, l_ab, l_wout = landed_sums(later_flight, "scatter_later")
    r_ab, r_wout = landed_sums(mixer_flight, "scatter_first")
    r_win, r_conv = landed_sums(win_flight, "scatter_win")
    both = lambda a, b, shape: jnp.concatenate([a.reshape(1, *shape[1:]), b.reshape(depth - 1, *shape[1:])])
    half, half_l = HEADS_W, (depth - 1) * HEADS_W
    mine = [both(r_win, l_win, w_in.shape), both(r_ab[:half], l_ab[:half_l], w_branch_a.shape),
            both(r_ab[half:], l_ab[half_l:], w_branch_b.shape), both(r_wout, l_wout, w_out.shape), r_conv]
    gseg = lambda i, shape: mine[i].reshape(shape)
    grads = {
        "meta_tokens": g_meta, "norm_w": g_norm, "w_in": gseg(0, w_in.shape), "conv_w": gseg(4, conv_w.shape),
        "a_log": g_alog, "dt_bias": g_dtb, "gnorm_a": g_gna, "gnorm_b": g_gnb, "hgrn_lower_bounds": g_lb,
        "w_branch_a": gseg(1, w_branch_a.shape), "w_branch_b": gseg(2, w_branch_b.shape), "w_out": gseg(3, w_out.shape),
        "final_norm_w": g_final}
    weights = {
        "meta_tokens": (meta_tokens, m_meta_tokens, v_meta_tokens), "norm_w": (norm_w, m_norm_w, v_norm_w),
        "w_in": (w_in, m_w_in, v_w_in), "conv_w": (conv_w, m_conv_w, v_conv_w), "a_log": (a_log, m_a_log, v_a_log),
        "dt_bias": (dt_bias, m_dt_bias, v_dt_bias), "gnorm_a": (gnorm_a, m_gnorm_a, v_gnorm_a),
        "gnorm_b": (gnorm_b, m_gnorm_b, v_gnorm_b),
        "hgrn_lower_bounds": (hgrn_lower_bounds, m_hgrn_lower_bounds, v_hgrn_lower_bounds),
        "w_branch_a": (w_branch_a, m_w_branch_a, v_w_branch_a), "w_branch_b": (w_branch_b, m_w_branch_b, v_w_branch_b),
        "w_out": (w_out, m_w_out, v_w_out), "final_norm_w": (final_norm_w, m_final_norm_w, v_final_norm_w)}
    names = list(weights)
    deltas, new_m, new_v = [], [], []
    for nm in names:
        w, m, v = weights[nm]
        view = (1,) * (3 - w.ndim) + w.shape
        d, m2, v2 = _adamw(w.reshape(view), grads[nm].reshape(view), m.reshape(view), v.reshape(view), f"adamw_{nm}")
        deltas.append(d.reshape(w.shape))
        new_m.append(m2.reshape(w.shape))
        new_v.append(v2.reshape(w.shape))
    return (loss, grad_x, *[grads[nm].reshape(weights[nm][0].shape) for nm in names], *deltas, *new_m, *new_v)
```

```python
import functools

import jax
import jax.numpy as jnp
from jax import lax
from jax.experimental import pallas as pl
from jax.experimental.pallas import tpu as pltpu

F32 = jnp.float32
BF16 = jnp.bfloat16

D_MODEL = 1024
N_HEADS = 4
D_HEAD = 128
HEADS_W = N_HEADS * D_HEAD
N_META = 16
N_PAD = 48
GDN_CHUNK = 64
HGRN_CHUNK = 16
EPS = 1e-6
N_DEV = 8
LANES = 128
SUBLANES = 8
VMEM_LIMIT = 56 * 1024 * 1024

C_QKV, C_Z, C_BQ, C_BF, C_BI, C_BG, C_GA, C_GB, C_MISC = 0, 1536, 2048, 2560, 3072, 3584, 4096, 5120, 6144
PROJ_W = 6272
REF_W = 6152

ADAM_LR, ADAM_B1, ADAM_B2, ADAM_EPS, ADAM_WD, ADAM_STEP = 0.001, 0.9, 0.999, 1e-08, 0.01, 10

MESH = pl.DeviceIdType.MESH
SDS = jax.ShapeDtypeStruct
BS = pl.BlockSpec


def _params(n_axes):
    return pltpu.CompilerParams(dimension_semantics=("arbitrary",) * n_axes, vmem_limit_bytes=VMEM_LIMIT)


def _pick(n, cands):
    for c in cands:
        if n % c == 0:
            return c
    raise ValueError(f"no tile for {n} among {cands}")


def _iota2(shape, dim):
    return lax.broadcasted_iota(jnp.int32, shape, dim)


def _dg(a, b, dims):
    return lax.dot_general(a.astype(BF16), b.astype(BF16), (dims, ((), ())), preferred_element_type=F32)


def _bdg(a, b, ca, cb):
    return lax.dot_general(a.astype(BF16), b.astype(BF16), (((ca,), (cb,)), ((0,), (0,))), preferred_element_type=F32)


@jax.custom_vjp
def _bnn(a, b):
    return _bdg(a, b, 2, 1)


@jax.custom_vjp
def _bnt(a, b):
    return _bdg(a, b, 2, 2)


@jax.custom_vjp
def _btn(a, b):
    return _bdg(a, b, 1, 1)


_bnn.defvjp(lambda a, b: (_bnn(a, b), (a, b)), lambda r, g: (_bnt(g, r[1]), _btn(r[0], g)))
_bnt.defvjp(lambda a, b: (_bnt(a, b), (a, b)), lambda r, g: (_bnn(g, r[1]), _btn(g, r[0])))
_btn.defvjp(lambda a, b: (_btn(a, b), (a, b)), lambda r, g: (_bnt(r[1], g), _bnn(r[0], g)))


def _split2(x):
    hi = x.astype(BF16).astype(F32)
    return hi, x - hi


def _tri(bsz, n):
    return jnp.broadcast_to((_iota2((n, n), 0) >= _iota2((n, n), 1)).astype(F32), (bsz, n, n))


@jax.custom_vjp
def _cumsum_rows(x):
    tri = _tri(x.shape[0], x.shape[1])
    hi, lo = _split2(x)
    return _bdg(tri, hi, 2, 1) + _bdg(tri, lo, 2, 1)


def _cumsum_rows_bwd(_, g):
    tri = _tri(g.shape[0], g.shape[1])
    hi, lo = _split2(g)
    return (_bdg(tri, hi, 1, 1) + _bdg(tri, lo, 1, 1),)


_cumsum_rows.defvjp(lambda x: (_cumsum_rows(x), None), _cumsum_rows_bwd)


def _sigmoid(x):
    return jax.nn.sigmoid(x)


def _silu(x):
    return x * _sigmoid(x)


def _softplus(x):
    return jnp.maximum(x, 0.0) + jnp.log1p(jnp.exp(-jnp.abs(x)))


def _rms(x, w):
    return x * lax.rsqrt(jnp.mean(x * x, axis=-1, keepdims=True) + EPS) * w


@jax.custom_vjp
def _inv_unit_lower(lm):
    n = lm.shape[1]
    a = (_iota2((n, n), 0) == _iota2((n, n), 1)).astype(F32)[None] - lm
    steps = max(1, (n - 1).bit_length()) - 1
    p = _bnn(lm, lm)
    for i in range(steps):
        if i == steps - 1:
            a = a + _bnn(a, p)
        else:
            both = _bnn(jnp.concatenate([a, p], axis=1), p)
            a, p = a + both[:, :n], both[:, n:]
    return a


_inv_unit_lower.defvjp(lambda lm: (lambda a: (a, a))(_inv_unit_lower(lm)),
                       lambda a, g: (-_bnt(_btn(a, g), a),))


def _gdn_chunk(q, k, v, b_b, g_b, s):
    n, dv = q.shape[1], v.shape[2]
    r, c = _iota2((n, n), 0), _iota2((n, n), 1)
    causal, strict, eye = (r >= c)[None], (r > c)[None], (r == c)[None]
    g_cum = _cumsum_rows(g_b)
    g_i = g_cum[:, :, :n]
    g_j = jnp.sum(jnp.where(eye, g_i, 0.0), axis=1, keepdims=True)
    decay = jnp.where(causal, jnp.exp(jnp.where(causal, g_i - g_j, 0.0)), 0.0)
    e_g = jnp.exp(g_cum)
    kb = k * b_b
    kk = _bnt(jnp.concatenate([kb, q], axis=1), k)
    a_inv = _inv_unit_lower(jnp.where(strict, kk[:, :n] * decay, 0.0))
    uw = _bnn(a_inv, jnp.concatenate([v * b_b, kb * e_g], axis=2))
    ws = _bnn(jnp.concatenate([uw[:, :, dv:], q * e_g], axis=1), s)
    v_new = uw[:, :, :dv] - ws[:, :n]
    o = ws[:, n:] + _bnn(kk[:, n:] * decay, v_new)
    g_last = g_cum[:, n - 1:n, :]
    s_new = s * jnp.exp(g_last) +_btn(k * jnp.exp(g_last - g_cum), v_new)
    return o, s_new


@functools.partial(jax.custom_vjp, nondiff_argnums=(1, 2))
def _row(x, j, n):
    return x[:, j:j + 1, :]


def _row_bwd(j, n, _, g):
    return (jnp.where(_iota2((1, n, 1), 1) == j, g, 0.0),)


_row.defvjp(lambda x, j, n: (_row(x, j, n), None), _row_bwd)


def _hgrn_pairs(q, k, v, b_cum):
    n = q.shape[1]
    half = n // 2 if n > SUBLANES else n
    parts = []
    for lo in range(0, n, half):
        qs, bs = q[:, lo:], b_cum[:, lo:]
        rows = _iota2((1, n - lo, 1), 1) + lo
        acc = jnp.zeros_like(qs)
        for j in range(lo, lo + half):
            p = jnp.exp(jnp.where(rows >= j, bs - _row(b_cum, j, n), -1e30))
            acc = acc + jnp.sum(qs * _row(k, j, n) * p, axis=2, keepdims=True) * _row(v, j, n)
        parts.append(acc)
    if len(parts) == 1:
        return parts[0]
    return parts[0] + jnp.concatenate([jnp.zeros_like(parts[1]), parts[1]], axis=1)


def _hgrn_block(q, k, v, lf, st, group=HGRN_CHUNK):
    n, rows = group, q.shape[1]
    b_cum = _cumsum_rows(lf)
    outs = []
    for c in range(rows // n):
        rs = slice(c * n, (c + 1) * n)
        o = _hgrn_pairs(q[:, rs], k[:, rs], v[:, rs], b_cum[:, rs])
        if c:
            b_c = _row(b_cum, c * n - 1, rows)
            scores = _bnt(q[:, rs] * jnp.exp(b_cum[:, rs] - b_c), k[:, :c * n] * jnp.exp(b_c - b_cum[:, :c * n]))
            o = o + _bnn(scores, v[:, :c * n])
        outs.append(o)
    b_last = _row(b_cum, rows - 1, rows)
    o = _bnt(q * jnp.exp(b_cum), st) + jnp.concatenate(outs, axis=1)
    return o, st * jnp.exp(b_last) + _btn(v, k * jnp.exp(b_last - b_cum))


def _l2n_act(y, scale):
    a = _silu(y)
    return a * lax.rsqrt(jnp.sum(a * a, axis=-1, keepdims=True) + EPS) * scale


def _col(x, lane):
    return jnp.sum(jnp.where(_iota2(x.shape, 1) == lane, x, 0.0), axis=1, keepdims=True)


def _elem(x, row, lane):
    m = (_iota2(x.shape, 0) == row) & (_iota2(x.shape, 1) == lane)
    return jnp.sum(jnp.sum(jnp.where(m, x, 0.0), axis=1, keepdims=True), axis=0, keepdims=True)


def _gdn_gates(misc, aux, real, head):
    beta = _sigmoid(_col(misc, head))
    g = -jnp.exp(_elem(aux, 0, head)) * _softplus(_col(misc, N_HEADS + head) + _elem(aux, 1, head))
    g = jnp.where(real, g, 0.0)
    shape = (misc.shape[0], D_HEAD)
    return jnp.broadcast_to(beta, shape), jnp.broadcast_to(g, shape)


def _hgrn_prep(bq, bf, lb, real):
    qb = _silu(bq) * (D_HEAD ** -0.5)
    log_sig = jnp.minimum(bf, 0.0) - jnp.log1p(jnp.exp(-jnp.abs(bf)))
    pos = lb > 0.0
    lbs = jnp.where(pos, lb, 0.5)
    a = jnp.log(lbs)
    b = jnp.log1p(-lbs) + log_sig
    lae = jnp.maximum(a, b) + jnp.log1p(jnp.exp(-jnp.abs(a - b)))
    lf = jnp.where(pos, lae, log_sig)
    kb = jnp.where(pos, 1.0 - lbs, 1.0) * _sigmoid(-bf)
    return qb, jnp.where(real, kb, 0.0), jnp.where(real, lf, 0.0)


def _gated_norm(o, z, gw):
    return o * lax.rsqrt(jnp.mean(o * o, axis=-1, keepdims=True) + EPS) * gw * _silu(z)


def _shift_down(x, j):
    return x if j == 0 else pltpu.roll(x, j, 0)


def _shift_up(x, j):
    return x if j == 0 else pltpu.roll(x, x.shape[0] - j, 0)


def _all_gather_hbm(blocks, name):
    na = len(blocks)

    def body(*refs):
        x_refs, out_refs = refs[:na], refs[na:2 * na]
        send_sems, recv_sems, local_sems = refs[2 * na:]
        mx, my, mc = lax.axis_index("x"), lax.axis_index("y"), lax.axis_index("c")
        me, sibling = (mx, my, mc), (mx, my, 1 - mc)
        chips = [(1 - mx, my), (mx, 1 - my), (1 - mx, 1 - my)]

        def slab(a, px, py, pc):
            return out_refs[a].at[4 * px + 2 * py + pc]

        def copy(a, k, blk, to, own=False):
            return pltpu.make_async_remote_copy(
                src_ref=x_refs[a] if own else slab(a, *blk), dst_ref=slab(a, *blk),
                send_sem=send_sems.at[7 * a + k], recv_sem=recv_sems.at[7 * a + k], device_id=to, device_id_type=MESH)

        mine = [pltpu.make_async_copy(x_refs[a], slab(a, *me), local_sems.at[a]) for a in range(na)]
        for cp in mine:
            cp.start()
        first = [copy(a, 0, me, sibling, own=True) for a in range(na)]
        first += [copy(a, 1 + j, me, (*chip, mc), own=True) for j, chip in enumerate(chips) for a in range(na)]
        for cp in first:
            cp.start()
        passed = []
        for j, chip in enumerate(chips):
            for a in range(na):
                copy(a, 1 + j, (*chip, mc), me).wait_recv()
                passed.append(copy(a, 4 + j, (*chip, mc), sibling))
                passed[-1].start()
        for a in range(na):
            copy(a, 0, sibling, me).wait_recv()
            for j, chip in enumerate(chips):
                copy(a, 4 + j, (*chip, 1 - mc), me).wait_recv()
        for cp in first + passed:
            cp.wait_send()
        for cp in mine:
            cp.wait()

    hbm = BS(memory_space=pl.ANY)
    return pl.pallas_call(
        body, name=name, out_shape=[SDS((N_DEV, *b.shape), b.dtype) for b in blocks],
        in_specs=[hbm] * na, out_specs=[hbm] * na,
        scratch_shapes=[pltpu.SemaphoreType.DMA((7 * na,)), pltpu.SemaphoreType.DMA((7 * na,)),
                        pltpu.SemaphoreType.DMA((na,))],
    )(*blocks)


def _all_reduce_small(block, name):
    r, c = block.shape

    def body(x_ref, out_ref, buf, send_sems, recv_sems):
        mx, my, mc = lax.axis_index("x"), lax.axis_index("y"), lax.axis_index("c")
        me, sibling = (mx, my, mc), (mx, my, 1 - mc)
        chips = [(1 - mx, my), (mx, 1 - my), (1 - mx, 1 - my)]

        def slab(px, py, pc):
            return buf.at[4 * px + 2 * py + pc]

        def copy(k, blk, to, src=None):
            return pltpu.make_async_remote_copy(
                src_ref=slab(*blk) if src is None else src, dst_ref=slab(*blk),
                send_sem=send_sems.at[k], recv_sem=recv_sems.at[k], device_id=to, device_id_type=MESH)

        first = [copy(0, me, sibling, src=x_ref)]
        first += [copy(1 + j, me, (*chip, mc), src=x_ref) for j, chip in enumerate(chips)]
        for cp in first:
            cp.start()
        passed = [copy(4 + j, (*chip, mc), sibling) for j, chip in enumerate(chips)]
        for j, chip in enumerate(chips):
            copy(1 + j, (*chip, mc), me).wait_recv()
            passed[j].start()
        copy(0, sibling, me).wait_recv()
        for j, chip in enumerate(chips):
            copy(4 + j, (*chip, 1 - mc), me).wait_recv()
        for cp in first + passed:
            cp.wait_send()
        buf[4 * mx + 2 * my + mc] = x_ref[...]
        acc = buf[0]
        for d in range(1, N_DEV):
            acc = acc + buf[d]
        out_ref[...] = acc

    return pl.pallas_call(
        body, name=name, out_shape=SDS((r, c), F32),
        in_specs=[BS(memory_space=pltpu.VMEM)], out_specs=BS(memory_space=pltpu.VMEM),
        scratch_shapes=[pltpu.VMEM((N_DEV, r, c), F32), pltpu.SemaphoreType.DMA((7,)), pltpu.SemaphoreType.DMA((7,))],
    )(block)


HBM_SPEC = BS(memory_space=pltpu.HBM)
SEM_SPEC = BS(memory_space=pltpu.SEMAPHORE)
SIDE_EFFECT = pltpu.SideEffectType.DATAFLOW_SIDE_EFFECTING


def _peer(rel):
    flip = lambda v, bit: 1 - v if bit else v
    return (flip(lax.axis_index("x"), rel >> 2 & 1), flip(lax.axis_index("y"), rel >> 1 & 1),
            flip(lax.axis_index("c"), rel & 1))


def _send_all_start(blocks, scatter, name, after=None):
    na = len(blocks)
    shapes = [b.shape[1:] if scatter else b.shape for b in blocks]
    n_in = 2 * na + (after is not None)

    def body(*refs):
        srcs, lands = refs[:na], refs[na:2 * na]
        send_sems, recv_sems, token = refs[n_in], refs[n_in + 1], refs[-1]
        me = 4 * lax.axis_index("x") + 2 * lax.axis_index("y") + lax.axis_index("c")
        for a in range(na):
            for rel in range(1, N_DEV):
                px, py, pc = _peer(rel)
                pltpu.make_async_remote_copy(
                    src_ref=srcs[a].at[4 * px + 2 * py + pc] if scatter else srcs[a], dst_ref=lands[a].at[me],
                    send_sem=send_sems.at[7 * a + rel - 1], recv_sem=recv_sems.at[7 * a + rel - 1],
                    device_id=(px, py, pc), device_id_type=MESH).start()
        token[...] = jnp.zeros_like(token)

    lands = [lax.empty((N_DEV, *s), b.dtype) for s, b in zip(shapes, blocks)]
    res = pl.pallas_call(
        body, name=name,
        out_shape=([pltpu.SemaphoreType.DMA((7 * na,)), pltpu.SemaphoreType.DMA((7 * na,))]
                   + [pltpu.HBM(b.shape, b.dtype) for b in blocks] + [pltpu.HBM(ld.shape, ld.dtype) for ld in lands]
                   + [SDS((SUBLANES, LANES), F32)]),
        in_specs=[HBM_SPEC] * (2 * na) + [BS(memory_space=pl.ANY)] * (after is not None),
        out_specs=[SEM_SPEC, SEM_SPEC] + [HBM_SPEC] * (2 * na) + [BS(memory_space=pltpu.VMEM)],
        input_output_aliases={i: 2 + i for i in range(2 * na)},
        compiler_params=pltpu.CompilerParams(has_side_effects=SIDE_EFFECT),
    )(*[pltpu.with_memory_space_constraint(b, pltpu.HBM) for b in blocks],
      *[pltpu.with_memory_space_constraint(ld, pltpu.HBM) for ld in lands], *([] if after is None else [after]))
    return dict(send=res[0], recv=res[1], srcs=res[2:2 + na], lands=res[2 + na:2 + 2 * na], scatter=scatter), res[-1]


def _send_all_wait(flight, after, name):
    na = len(flight["srcs"])

    def body(*refs):
        srcs, lands = refs[:na], refs[na:2 * na]
        send_sems, recv_sems = refs[2 * na], refs[2 * na + 1]
        for a in range(na):
            for rel in range(1, N_DEV):
                cp = pltpu.make_async_remote_copy(
                    src_ref=srcs[a].at[0] if flight["scatter"] else srcs[a], dst_ref=lands[a].at[0],
                    send_sem=send_sems.at[7 * a + rel - 1], recv_sem=recv_sems.at[7 * a + rel - 1],
                    device_id=_peer(rel), device_id_type=MESH)
                cp.wait_send()
                cp.wait_recv()

    arrays = list(flight["srcs"]) + list(flight["lands"])
    res = pl.pallas_call(
        body, name=name, out_shape=[pltpu.HBM(a.shape, a.dtype) for a in arrays],
        in_specs=[HBM_SPEC] * (2 * na) + [SEM_SPEC, SEM_SPEC, BS(memory_space=pl.ANY)], out_specs=[HBM_SPEC] * (2 * na),
        input_output_aliases={i: i for i in range(2 * na)},
        compiler_params=pltpu.CompilerParams(has_side_effects=SIDE_EFFECT),
    )(*arrays, flight["send"], flight["recv"], after)
    return res[:na], res[na:]


def _sum_slabs(land, name):
    _, r, c = land.shape
    tr = _pick(r, (256, 128, 64, 32, 16, 8))

    def body(l_ref, o_ref):
        acc = l_ref[0].astype(F32)
        for d in range(1, N_DEV):
            acc = acc + l_ref[d].astype(F32)
        o_ref[...] = acc

    return pl.pallas_call(
        body, name=name, grid=(r // tr,), out_shape=SDS((r, c), F32),
        in_specs=[BS((N_DEV, tr, c), lambda j: (0, j, 0))], out_specs=BS((tr, c), lambda j: (j, 0)),
        compiler_params=_params(1),
    )(land)


def _proj_fwd(h, nw8, wp, tag):
    n = h.shape[0]
    tm = _pick(n, (2112, 1408, 768, 512, 384, 256, 192, 128, 64))
    tn = 896

    def body(h_ref, nw_ref, w_ref, proj_ref, xn_ref):
        @pl.when(pl.program_id(1) == 0)
        def _():
            xn_ref[...] = _rms(h_ref[...], nw_ref[0:1, :]).astype(BF16)

        proj_ref[...] = jnp.dot(xn_ref[...], w_ref[...], preferred_element_type=F32)

    return pl.pallas_call(
        body, name=f"proj_fwd_{tag}", grid=(n // tm, PROJ_W // tn),
        in_specs=[BS((tm, D_MODEL), lambda i, j: (i, 0)), BS((SUBLANES, D_MODEL), lambda i, j: (0, 0)),
                  BS((D_MODEL, tn), lambda i, j: (0, j))],
        out_specs=[BS((tm, tn), lambda i, j: (i, j)), BS((tm, D_MODEL), lambda i, j: (i, 0))],
        out_shape=[SDS((n, PROJ_W), F32), SDS((n, D_MODEL), BF16)], compiler_params=_params(2),
    )(h, nw8, wp)


def _conv_ext(x_ext, cw_ref):
    y = x_ext * cw_ref[3:4, :]
    for k in range(3):
        y = y + _shift_down(x_ext, 3 - k) * cw_ref[k:k + 1, :]
    return y[SUBLANES:]


def _prep_fwd(proj, cw8, aux, lb8, nseq, t_len, tag):
    n = proj.shape[0]
    tt = _pick(t_len, (192, 128, 64))
    nt_ = t_len // tt
    qkv_w = 3 * HEADS_W

    def body(cur_ref, prev_ref, misc_ref, bq_ref, bf_ref, cw_ref, aux_ref, lb_ref,
             q_ref, k_ref, v_ref, b_ref, g_ref, qb_ref, kb_ref, lf_ref, ext_ref):
        t = pl.program_id(1)
        ext_ref[0:SUBLANES, :] = jnp.where(t == 0, 0.0, prev_ref[...])
        ext_ref[SUBLANES:, :] = cur_ref[...]
        y = ext_ref[SUBLANES:, :] * cw_ref[3:4, :]
        for kk in range(3):
            y = y + ext_ref[SUBLANES - 3 + kk:SUBLANES - 3 + kk + tt, :] * cw_ref[kk:kk + 1, :]
        real = (t * tt + _iota2((tt, 1), 0)) >= N_PAD
        misc = misc_ref[...]
        auxv = aux_ref[...]
        for hd in range(N_HEADS):
            sl = slice(hd * D_HEAD, (hd + 1) * D_HEAD)
            q_ref[:, sl] = _l2n_act(y[:, sl], D_HEAD ** -0.5)
            k_ref[:, sl] = _l2n_act(y[:, HEADS_W + hd * D_HEAD:HEADS_W + (hd + 1) * D_HEAD], 1.0)
            v_ref[:, sl] = _silu(y[:, 2 * HEADS_W + hd * D_HEAD:2 * HEADS_W + (hd + 1) * D_HEAD])
            b_ref[:, sl], g_ref[:, sl] = _gdn_gates(misc, auxv, real, hd)
        qb_ref[...], kb_ref[...], lf_ref[...] = _hgrn_prep(bq_ref[...], bf_ref[...], lb_ref[0:1, :], real)

    rb = tt // SUBLANES
    row = lambda s, t: s * nt_ + t
    wide = BS((tt, HEADS_W), lambda s, t: (row(s, t), 0))
    return pl.pallas_call(
        body, name=f"prep_fwd_{tag}", grid=(nseq, nt_),
        in_specs=[BS((tt, qkv_w), lambda s, t: (row(s, t), 0)),
                  BS((SUBLANES, qkv_w), lambda s, t: (jnp.maximum(row(s, t) * rb - 1, 0), 0)),
                  BS((tt, LANES), lambda s, t: (row(s, t), C_MISC // LANES)),
                  BS((tt, HEADS_W), lambda s, t: (row(s, t), C_BQ // HEADS_W)),
                  BS((tt, HEADS_W), lambda s, t: (row(s, t), C_BF // HEADS_W)),
                  BS((SUBLANES, qkv_w), lambda s, t: (0, 0)), BS((SUBLANES, LANES), lambda s, t: (0, 0)),
                  BS((SUBLANES, HEADS_W), lambda s, t: (0, 0))],
        out_specs=[wide] * 8, out_shape=[SDS((n, HEADS_W), F32)] * 8,
        scratch_shapes=[pltpu.VMEM((tt + SUBLANES, qkv_w), F32)], compiler_params=_params(2),
    )(proj, proj, proj, proj, proj, cw8, aux, lb8)


GDN_SEQS = 4
HGRN_SEQS = 2


def _seq_block(nseq, most):
    return max(s for s in (1, 2, 4) if s <= most and nseq % s == 0)


def _to_chains(x):
    return jnp.concatenate([x[:, :, hd * D_HEAD:(hd + 1) * D_HEAD] for hd in range(N_HEADS)], axis=0)


def _from_chains(ref, rows, val):
    sb = val.shape[0] // N_HEADS
    for hd in range(N_HEADS):
        ref[:, rows, hd * D_HEAD:(hd + 1) * D_HEAD] = val[hd * sb:(hd + 1) * sb].astype(ref.dtype)


def _mixers_fwd(q, k, v, b, g, qb, kb, vb, vb_col, lf, nseq, t_len, tag):
    sb, hs = _seq_block(nseq, GDN_SEQS), _seq_block(nseq, HGRN_SEQS)
    nc = t_len // GDN_CHUNK
    chains = N_HEADS * sb

    def body(q_ref, k_ref, v_ref, b_ref, g_ref, qb_ref, kb_ref, vb_ref, lf_ref, oa_ref, ob_ref, cka_ref, ckb_ref,
             sa_ref, sb_ref):
        @pl.when(pl.program_id(1) == 0)
        def _():
            sa_ref[...] = jnp.zeros_like(sa_ref)
            sb_ref[...] = jnp.zeros_like(sb_ref)

        s = sa_ref[...]
        cka_ref[...] = s
        o, s_new = _gdn_chunk(*[_to_chains(r[...]) for r in (q_ref, k_ref, v_ref, b_ref, g_ref)], s)
        _from_chains(oa_ref, slice(None), o)
        sa_ref[...] = s_new
        for part in range(sb // hs):
            seqs, ch = slice(part * hs, (part + 1) * hs), slice(part * N_HEADS * hs, (part + 1) * N_HEADS * hs)
            s = sb_ref[ch]
            ckb_ref[ch] = s
            o, s_new = _hgrn_block(*[_to_chains(r[seqs]) for r in (qb_ref, kb_ref, vb_ref, lf_ref)], s)
            for hd in range(N_HEADS):
                ob_ref[seqs, :, hd * D_HEAD:(hd + 1) * D_HEAD] = o[hd * hs:(hd + 1) * hs]
            sb_ref[ch] = s_new

    blk = lambda cb: BS((sb, GDN_CHUNK, HEADS_W), lambda p, c: (p, c, cb))
    ck_spec = BS((None, None, chains, D_HEAD, D_HEAD), lambda p, c: (p, c, 0, 0, 0))
    ck_shape = SDS((nseq // sb, nc, chains, D_HEAD, D_HEAD), F32)
    view = lambda a: a.reshape(nseq, t_len, a.shape[1])
    oa, ob, cka, ckb = pl.pallas_call(
        body, name=f"mixers_fwd_{tag}", grid=(nseq // sb, nc),
        in_specs=[blk(0)] * 7 + [blk(vb_col), blk(0)], out_specs=[blk(0), blk(0), ck_spec, ck_spec],
        out_shape=[SDS((nseq, t_len, HEADS_W), F32)] * 2 + [ck_shape] * 2,
        scratch_shapes=[pltpu.VMEM((chains, D_HEAD, D_HEAD), F32)] * 2, compiler_params=_params(2),
    )(*[view(a) for a in (q, k, v, b, g, qb, kb, vb, lf)])
    return oa.reshape(-1, HEADS_W), ob.reshape(-1, HEADS_W), cka, ckb


def _mixers_bwd(q, k, v, b, g, qb, kb, vb, vb_col, lf, cka, ckb, doa, dob, nseq, t_len, tag):
    sb, hs = _seq_block(nseq, GDN_SEQS), _seq_block(nseq, HGRN_SEQS)
    nc = t_len // GDN_CHUNK
    chains = N_HEADS * sb

    def body(q_ref, k_ref, v_ref, b_ref, g_ref, qb_ref, kb_ref, vb_ref, lf_ref, doa_ref, dob_ref, cka_ref, ckb_ref,
             dq_ref, dk_ref, dv_ref, db_ref, dg_ref, dqb_ref, dkb_ref, dvb_ref, dlf_ref, dsa_ref, dsb_ref):
        @pl.when(pl.program_id(1) == 0)
        def _():
            dsa_ref[...] = jnp.zeros_like(dsa_ref)
            dsb_ref[...] = jnp.zeros_like(dsb_ref)

        _, vjp = jax.vjp(_gdn_chunk, *[_to_chains(r[...]) for r in (q_ref, k_ref, v_ref, b_ref, g_ref)], cka_ref[...])
        grads = vjp((_to_chains(doa_ref[...]), dsa_ref[...]))
        for ref, val in zip((dq_ref, dk_ref, dv_ref, db_ref, dg_ref), grads[:5]):
            _from_chains(ref, slice(None), val)
        dsa_ref[...] = grads[5]
        for part in range(sb // hs):
            seqs, ch = slice(part * hs, (part + 1) * hs), slice(part * N_HEADS * hs, (part + 1) * N_HEADS * hs)
            _, vjp = jax.vjp(functools.partial(_hgrn_block, group=SUBLANES),
                             *[_to_chains(r[seqs]) for r in (qb_ref, kb_ref, vb_ref, lf_ref)], ckb_ref[ch])
            grads = vjp((_to_chains(dob_ref[seqs]), dsb_ref[ch]))
            for ref, val in zip((dqb_ref, dkb_ref, dvb_ref, dlf_ref), grads[:4]):
                for hd in range(N_HEADS):
                    ref[seqs, :, hd * D_HEAD:(hd + 1) * D_HEAD] = val[hd * hs:(hd + 1) * hs].astype(ref.dtype)
            dsb_ref[ch] = grads[4]

    blk = lambda cb: BS((sb, GDN_CHUNK, HEADS_W), lambda p, c: (p, nc - 1 - c, cb))
    ck_spec = BS((None, None, chains, D_HEAD, D_HEAD), lambda p, c: (p, nc - 1 - c, 0, 0, 0))
    view = lambda a: a.reshape(nseq, t_len, a.shape[1])
    dts = [F32] * 7 + [BF16, F32]
    res = pl.pallas_call(
        body, name=f"mixers_bwd_{tag}", grid=(nseq // sb, nc),
        in_specs=[blk(0)] * 7 + [blk(vb_col), blk(0), blk(0), blk(0), ck_spec, ck_spec], out_specs=[blk(0)] * 9,
        out_shape=[SDS((nseq, t_len, HEADS_W), dt) for dt in dts],
        scratch_shapes=[pltpu.VMEM((chains, D_HEAD, D_HEAD), F32)] * 2, compiler_params=_params(2),
    )(*[view(a) for a in (q, k, v, b, g, qb, kb, vb, lf, doa, dob)], cka, ckb)
    return [r.reshape(-1, HEADS_W) for r in res]


def _post_values(oa_ref, ob_ref, z_ref, bg_ref, ga_ref, gb_ref, gn_ref, wa_ref, wb_ref, ya_ref, yb_ref):
    for hd in range(N_HEADS):
        sl = slice(hd * D_HEAD, (hd + 1) * D_HEAD)
        ya_ref[:, sl] = _gated_norm(oa_ref[:, sl], z_ref[:, sl], gn_ref[0:1, :]).astype(BF16)
        yb_ref[:, sl] = _gated_norm(ob_ref[:, sl], bg_ref[:, sl], gn_ref[1:2, :]).astype(BF16)
    pa = jnp.dot(ya_ref[...], wa_ref[...], preferred_element_type=F32)
    pb = jnp.dot(yb_ref[...], wb_ref[...], preferred_element_type=F32)
    return pa, pb, _sigmoid(ga_ref[...]), _sigmoid(gb_ref[...])


def _post_specs(tm):
    r2 = lambda i: (i, 0)
    return [BS((tm, HEADS_W), r2), BS((tm, HEADS_W), r2),
            BS((tm, HEADS_W), lambda i: (i, C_Z // HEADS_W)), BS((tm, HEADS_W), lambda i: (i, C_BG // HEADS_W)),
            BS((tm, D_MODEL), lambda i: (i, C_GA // D_MODEL)), BS((tm, D_MODEL), lambda i: (i, C_GB // D_MODEL)),
            BS((tm, D_MODEL), r2), BS((SUBLANES, LANES), lambda i: (0, 0))]


def _post_fwd(oa, ob, proj, h, gn8, wa, wb, wout, tag):
    n = h.shape[0]
    tm = _pick(n, (768, 384, 256, 192, 128, 64))

    def body(oa_ref, ob_ref, z_ref, bg_ref, ga_ref, gb_ref, h_ref, gn_ref, wa_ref, wb_ref, wout_ref, out_ref,
             ya_ref, yb_ref):
        pa, pb, sa, sb = _post_values(oa_ref, ob_ref, z_ref, bg_ref, ga_ref, gb_ref, gn_ref, wa_ref, wb_ref,
                                      ya_ref, yb_ref)
        mixed = (sa * pa + sb * pb).astype(BF16)
        out_ref[...] = h_ref[...] + jnp.dot(mixed, wout_ref[...], preferred_element_type=F32)

    full = lambda i: (0, 0)
    return pl.pallas_call(
        body, name=f"post_fwd_{tag}", grid=(n // tm,),
        in_specs=_post_specs(tm) + [BS((HEADS_W, D_MODEL), full), BS((HEADS_W, D_MODEL), full),
                                    BS((D_MODEL, D_MODEL), full)],
        out_specs=BS((tm, D_MODEL), lambda i: (i, 0)), out_shape=SDS((n, D_MODEL), F32),
        scratch_shapes=[pltpu.VMEM((tm, HEADS_W), BF16), pltpu.VMEM((tm, HEADS_W), BF16)], compiler_params=_params(1),
    )(oa, ob, proj, proj, proj, proj, h, gn8, wa, wb, wout)


def _post_bwd(dh, oa, ob, proj, h, gn8, wa, wb, wout, tag):
    n = h.shape[0]
    tm = _pick(n, (256, 192, 128, 64))

    def body(dh_ref, oa_ref, ob_ref, z_ref, bg_ref, ga_ref, gb_ref, h_ref, gn_ref, wa_ref, wb_ref, wout_ref,
             doa_ref, dob_ref, dz_ref, dbg_ref, dga_ref, dgb_ref, dwa_ref, dwb_ref, dwout_ref, dgn_ref,
             ya_ref, yb_ref):
        @pl.when(pl.program_id(0) == 0)
        def _():
            dwa_ref[...] = jnp.zeros_like(dwa_ref)
            dwb_ref[...] = jnp.zeros_like(dwb_ref)
            dwout_ref[...] = jnp.zeros_like(dwout_ref)
            dgn_ref[...] = jnp.zeros_like(dgn_ref)

        pa, pb, sa, sb = _post_values(oa_ref, ob_ref, z_ref, bg_ref, ga_ref, gb_ref, gn_ref, wa_ref, wb_ref,
                                      ya_ref, yb_ref)
        mixed = (sa * pa + sb * pb).astype(BF16)
        dout = dh_ref[...].astype(BF16)
        dwout_ref[...] += _dg(mixed, dout, ((0,), (0,)))
        dmixed = _dg(dout, wout_ref[...], ((1,), (1,)))
        dga_ref[...] = (dmixed * pa * sa * (1.0 - sa)).astype(BF16)
        dgb_ref[...] = (dmixed * pb * sb * (1.0 - sb)).astype(BF16)
        dpa = (dmixed * sa).astype(BF16)
        dpb = (dmixed * sb).astype(BF16)
        dwa_ref[...] += _dg(ya_ref[...], dpa, ((0,), (0,)))
        dwb_ref[...] += _dg(yb_ref[...], dpb, ((0,), (0,)))
        dya = _dg(dpa, wa_ref[...], ((1,), (1,)))
        dyb = _dg(dpb, wb_ref[...], ((1,), (1,)))
        dgn_a = jnp.zeros((1, D_HEAD), F32)
        dgn_b = jnp.zeros((1, D_HEAD), F32)
        for hd in range(N_HEADS):
            sl = slice(hd * D_HEAD, (hd + 1) * D_HEAD)
            _, vjp = jax.vjp(_gated_norm, oa_ref[:, sl], z_ref[:, sl], gn_ref[0:1, :])
            doa, dz, dgw = vjp(dya[:, sl])
            doa_ref[:, sl], dz_ref[:, sl], dgn_a = doa, dz.astype(BF16), dgn_a + dgw
            _, vjp = jax.vjp(_gated_norm, ob_ref[:, sl], bg_ref[:, sl], gn_ref[1:2, :])
            dob, dbg, dgw = vjp(dyb[:, sl])
            dob_ref[:, sl], dbg_ref[:, sl], dgn_b = dob, dbg.astype(BF16), dgn_b + dgw
        dgn_ref[0:1, :] += dgn_a
        dgn_ref[1:2, :] += dgn_b

    full = lambda i: (0, 0)
    r2 = lambda i: (i, 0)
    return pl.pallas_call(
        body, name=f"post_bwd_{tag}", grid=(n // tm,),
        in_specs=[BS((tm, D_MODEL), r2)] + _post_specs(tm) + [
            BS((HEADS_W, D_MODEL), full), BS((HEADS_W, D_MODEL), full), BS((D_MODEL, D_MODEL), full)],
        out_specs=[BS((tm, HEADS_W), r2)] * 4 + [BS((tm, D_MODEL), r2)] * 2 + [
            BS((HEADS_W, D_MODEL), full), BS((HEADS_W, D_MODEL), full), BS((D_MODEL, D_MODEL), full),
            BS((SUBLANES, LANES), full)],
        out_shape=[SDS((n, HEADS_W), F32), SDS((n, HEADS_W), F32), SDS((n, HEADS_W), BF16), SDS((n, HEADS_W), BF16),
                   SDS((n, D_MODEL), BF16), SDS((n, D_MODEL), BF16), SDS((HEADS_W, D_MODEL), F32),
                   SDS((HEADS_W, D_MODEL), F32), SDS((D_MODEL, D_MODEL), F32), SDS((SUBLANES, LANES), F32)],
        scratch_shapes=[pltpu.VMEM((tm, HEADS_W), BF16), pltpu.VMEM((tm, HEADS_W), BF16)], compiler_params=_params(1),
    )(dh, oa, ob, proj, proj, proj, proj, h, gn8, wa, wb, wout)


def _loss_head(h, fw8, target, nseq, t_len):
    n = h.shape[0]
    nc = t_len // GDN_CHUNK
    sub = _pick(nc, (11, 3, 1))
    tl, nt = sub * GDN_CHUNK, nc // sub
    inv_d = 1.0 / D_MODEL

    def body(h_ref, fw_ref, *rest):
        tgt_refs, (dh_ref, acc_ref) = rest[:sub], rest[sub:]

        @pl.when((pl.program_id(0) == 0) & (pl.program_id(1) == 0))
        def _():
            acc_ref[...] = jnp.zeros_like(acc_ref)

        frames = ((pl.program_id(1) * tl + _iota2((tl, 1), 0)) >= N_PAD + N_META).astype(F32)
        y, vjp = jax.vjp(_rms, h_ref[...], fw_ref[0:1, :])
        err = (y - jnp.concatenate([r[...] for r in tgt_refs], axis=0)) * frames
        dx, dfw = vjp(err * inv_d)
        dh_ref[...] = dx
        acc_ref[0:1, :] += dfw
        acc_ref[1:2, :] += (0.5 * inv_d) * jnp.sum(err * err, axis=0, keepdims=True)

    tgt_spec = lambda u: BS((None, GDN_CHUNK, D_MODEL), lambda s, t: (s, jnp.maximum(t * sub + u - 1, 0), 0))
    return pl.pallas_call(
        body, name="loss_head", grid=(nseq, nt),
        in_specs=[BS((tl, D_MODEL), lambda s, t: (s * nt + t, 0)), BS((SUBLANES, D_MODEL), lambda s, t: (0, 0))]
        + [tgt_spec(u) for u in range(sub)],
        out_specs=[BS((tl, D_MODEL), lambda s, t: (s * nt + t, 0)), BS((SUBLANES, D_MODEL), lambda s, t: (0, 0))],
        out_shape=[SDS((n, D_MODEL), F32), SDS((SUBLANES, D_MODEL), F32)], compiler_params=_params(2),
    )(h, fw8, *[target] * sub)


def _prep_bwd(proj, dq, dk, dv, db, dg, dqb, dkb, dlf, cw8, aux, lb8, nseq, t_len, tag):
    n = proj.shape[0]
    tt = _pick(t_len, (192, 128, 64))
    nt_ = t_len // tt
    qkv_w = 3 * HEADS_W
    rb = tt // SUBLANES
    ext = tt + SUBLANES

    def body(cur_ref, prev_ref, next_ref, misc_ref, bq_ref, bf_ref, dq_ref, dqn_ref, dk_ref, dkn_ref, dv_ref, dvn_ref,
             db_ref, dg_ref, dqb_ref, dkb_ref, dlf_ref, cw_ref, aux_ref, lb_ref,
             dqkv_ref, dmisc_ref, dbq_ref, dbf_ref, dcw_ref, daux_ref, dlb_ref, dy_ref):
        s, t = pl.program_id(0), pl.program_id(1)

        @pl.when((s == 0) & (t == 0))
        def _():
            dcw_ref[...] = jnp.zeros_like(dcw_ref)
            daux_ref[...] = jnp.zeros_like(daux_ref)
            dlb_ref[...] = jnp.zeros_like(dlb_ref)

        prev = jnp.where(t == 0, 0.0, prev_ref[...])
        x_ext = jnp.concatenate([prev, cur_ref[...], next_ref[...]], axis=0)
        y = _conv_ext(x_ext, cw_ref)
        inside = (t < nt_ - 1) | (_iota2((ext, 1), 0) < tt)
        dy_ref[0:SUBLANES, :] = jnp.zeros((SUBLANES, qkv_w), F32)
        for hd in range(N_HEADS):
            for grp, (g_ref, gn_ref, scale) in enumerate(((dq_ref, dqn_ref, D_HEAD ** -0.5), (dk_ref, dkn_ref, 1.0),
                                                          (dv_ref, dvn_ref, None))):
                lo = grp * HEADS_W + hd * D_HEAD
                sl = slice(hd * D_HEAD, (hd + 1) * D_HEAD)
                cot = jnp.concatenate([g_ref[:, sl], gn_ref[:, sl]], axis=0)
                fn = _silu if scale is None else functools.partial(_l2n_act, scale=scale)
                _, vjp = jax.vjp(fn, y[:, lo:lo + D_HEAD])
                dy_ref[SUBLANES:, lo:lo + D_HEAD] = jnp.where(inside, vjp(cot)[0], 0.0)
        dy_ext = dy_ref[...]
        dx = dy_ext * cw_ref[3:4, :]
        for kk in range(3):
            dx = dx + _shift_up(dy_ext, 3 - kk) * cw_ref[kk:kk + 1, :]
        dqkv_ref[...] = dx[SUBLANES:SUBLANES + tt].astype(BF16)
        dy_cur = dy_ext[SUBLANES:SUBLANES + tt]
        for kk in range(4):
            xs = _shift_down(x_ext, 3 - kk)[SUBLANES:SUBLANES + tt]
            dcw_ref[kk:kk + 1, :] += jnp.sum(xs * dy_cur, axis=0, keepdims=True)

        real = (t * tt + _iota2((tt, 1), 0)) >= N_PAD
        dmisc = jnp.zeros((tt, LANES), F32)
        daux = jnp.zeros((SUBLANES, LANES), F32)
        for hd in range(N_HEADS):
            sl = slice(hd * D_HEAD, (hd + 1) * D_HEAD)
            _, vjp = jax.vjp(lambda m, a: _gdn_gates(m, a, real, hd), misc_ref[...], aux_ref[...])
            dm, da = vjp((db_ref[:, sl], dg_ref[:, sl]))
            dmisc, daux = dmisc + dm, daux + da
        dmisc_ref[...] = dmisc.astype(BF16)
        daux_ref[...] += daux
        _, vjp = jax.vjp(lambda a, b, c: _hgrn_prep(a, b, c, real), bq_ref[...], bf_ref[...], lb_ref[0:1, :])
        dbq, dbf, dlb = vjp((dqb_ref[...], dkb_ref[...], dlf_ref[...]))
        dbq_ref[...], dbf_ref[...] = dbq.astype(BF16), dbf.astype(BF16)
        dlb_ref[0:1, :] += dlb

    row = lambda s, t: s * nt_ + t
    cur = lambda s, t: (row(s, t), 0)
    nxt = lambda s, t: (jnp.minimum((row(s, t) + 1) * rb, n // SUBLANES - 1), 0)
    wide = BS((tt, HEADS_W), cur)
    halo = BS((SUBLANES, HEADS_W), nxt)
    full = lambda s, t: (0, 0)
    return pl.pallas_call(
        body, name=f"prep_bwd_{tag}", grid=(nseq, nt_),
        in_specs=[BS((tt, qkv_w), cur), BS((SUBLANES, qkv_w), lambda s, t: (jnp.maximum(row(s, t) * rb - 1, 0), 0)),
                  BS((SUBLANES, qkv_w), nxt), BS((tt, LANES), lambda s, t: (row(s, t), C_MISC // LANES)),
                  BS((tt, HEADS_W), lambda s, t: (row(s, t), C_BQ // HEADS_W)),
                  BS((tt, HEADS_W), lambda s, t: (row(s, t), C_BF // HEADS_W)),
                  wide, halo, wide, halo, wide, halo, wide, wide, wide, wide, wide,
                  BS((SUBLANES, qkv_w), full), BS((SUBLANES, LANES), full), BS((SUBLANES, HEADS_W), full)],
        out_specs=[BS((tt, qkv_w), cur), BS((tt, LANES), cur), wide, wide,
                   BS((SUBLANES, qkv_w), full), BS((SUBLANES, LANES), full), BS((SUBLANES, HEADS_W), full)],
        out_shape=[SDS((n, qkv_w), BF16), SDS((n, LANES), BF16), SDS((n, HEADS_W), BF16), SDS((n, HEADS_W), BF16),
                   SDS((SUBLANES, qkv_w), F32), SDS((SUBLANES, LANES), F32), SDS((SUBLANES, HEADS_W), F32)],
        scratch_shapes=[pltpu.VMEM((tt + 2 * SUBLANES, qkv_w), F32)], compiler_params=_params(2),
    )(proj, proj, proj, proj, proj, proj, dq, dq, dk, dk, dv, dv, db, dg, dqb, dkb, dlf, cw8, aux, lb8)


def _proj_bwd_x(pieces, wp, h, nw8, dh_res, tag):
    n = h.shape[0]
    tm = _pick(n, (384, 256, 192, 128, 64))
    widths = [p.shape[1] for p in pieces]
    assert sum(widths) == PROJ_W

    def body(*refs):
        p_refs = refs[:len(pieces)]
        w_ref, h_ref, nw_ref, dres_ref, dh_ref, dnw_ref = refs[len(pieces):]

        @pl.when(pl.program_id(0) == 0)
        def _():
            dnw_ref[...] = jnp.zeros_like(dnw_ref)

        dxn, off = None, 0
        for p_ref, w in zip(p_refs, widths):
            part = _dg(p_ref[...], w_ref[:, off:off + w], ((1,), (1,)))
            dxn = part if dxn is None else dxn + part
            off += w
        _, vjp = jax.vjp(_rms, h_ref[...], nw_ref[0:1, :])
        dx, dnw = vjp(dxn)
        dh_ref[...] = dres_ref[...] + dx
        dnw_ref[0:1, :] += dnw

    r2 = lambda i: (i, 0)
    full = lambda i: (0, 0)
    return pl.pallas_call(
        body, name=f"proj_bwd_x_{tag}", grid=(n // tm,),
        in_specs=[BS((tm, w), r2) for w in widths] + [BS((D_MODEL, PROJ_W), full), BS((tm, D_MODEL), r2),
                                                      BS((SUBLANES, D_MODEL), full), BS((tm, D_MODEL), r2)],
        out_specs=[BS((tm, D_MODEL), r2), BS((SUBLANES, D_MODEL), full)],
        out_shape=[SDS((n, D_MODEL), F32), SDS((SUBLANES, D_MODEL), F32)], compiler_params=_params(1),
    )(*pieces, wp, h, nw8, dh_res)


def _proj_bwd_w(xn, pieces, tag):
    n = xn.shape[0]
    tm = _pick(n, (768, 384, 256, 192, 128, 64))
    widths = [p.shape[1] for p in pieces]
    assert sum(widths) == PROJ_W

    def body(*refs):
        x_ref, p_refs = refs[0], refs[1:1 + len(pieces)]
        o_ref, acc_ref = refs[1 + len(pieces):]

        @pl.when(pl.program_id(0) == 0)
        def _():
            acc_ref[...] = jnp.zeros_like(acc_ref)

        off = 0
        for p_ref, w in zip(p_refs, widths):
            acc_ref[:, off:off + w] += _dg(x_ref[...], p_ref[...], ((0,), (0,)))
            off += w

        @pl.when(pl.program_id(0) == pl.num_programs(0) - 1)
        def _():
            pltpu.sync_copy(acc_ref, o_ref)

    r2 = lambda i: (i, 0)
    return pl.pallas_call(
        body, name=f"proj_bwd_w_{tag}", grid=(n // tm,),
        in_specs=[BS((tm, D_MODEL), r2)] + [BS((tm, w), r2) for w in widths], out_specs=BS(memory_space=pl.ANY),
        out_shape=SDS((D_MODEL, PROJ_W), F32), scratch_shapes=[pltpu.VMEM((D_MODEL, PROJ_W), F32)],
        compiler_params=_params(1),
    )(xn, *pieces)


def _adamw(w, g, m, v, name):
    lead, rows, cols = w.shape
    tr = _pick(rows, (256, 128, 64, 32, 16, 8, 4, 2, 1)) if rows > 256 else rows

    def body(w_ref, g_ref, m_ref, v_ref, d_ref, nm_ref, nv_ref):
        gr = g_ref[...]
        m_new = ADAM_B1 * m_ref[...] + (1.0 - ADAM_B1) * gr
        v_new = ADAM_B2 * v_ref[...] + (1.0 - ADAM_B2) * jnp.square(gr)
        m_hat = m_new / (1.0 - ADAM_B1 ** ADAM_STEP)
        v_hat = v_new / (1.0 - ADAM_B2 ** ADAM_STEP)
        d_ref[...] = -ADAM_LR * (m_hat / (jnp.sqrt(v_hat) + ADAM_EPS) + ADAM_WD * w_ref[...])
        nm_ref[...] = m_new
        nv_ref[...] = v_new

    blk = BS((None, tr, cols), lambda a, i: (a, i, 0))
    return pl.pallas_call(
        body, name=name, grid=(lead, rows // tr), in_specs=[blk] * 4, out_specs=[blk] * 3,
        out_shape=[SDS((lead, rows, cols), F32)] * 3, compiler_params=_params(2),
    )(w, g, m, v)


def _row8(v, width):
    v = jnp.atleast_2d(v).astype(F32)
    return jnp.pad(v, ((0, SUBLANES - v.shape[0]), (0, width - v.shape[1])))


REF_MISC = 1536
N_MISC = 2 * N_HEADS
LAYOUT_RUNS = ((0, REF_MISC, 0), (REF_MISC + N_MISC, REF_W, REF_MISC), (REF_MISC, REF_MISC + N_MISC, C_MISC))


def _to_layout(slabs, tag):
    n_slabs, rows, width = slabs.shape
    tr = _pick(rows, (256, 128, 64, 32, 16))

    def body(x_ref, o_ref):
        off = 0
        for lo, hi, _ in sorted(LAYOUT_RUNS, key=lambda run: run[2]):
            for j in range(n_slabs):
                a, b = max(lo, j * width), min(hi, (j + 1) * width)
                if a < b:
                    o_ref[:, off:off + b - a] = x_ref[j, :, a - j * width:b - j * width]
                    off += b - a
        o_ref[:, off:] = jnp.zeros((tr, PROJ_W - off), o_ref.dtype)

    return pl.pallas_call(
        body, name=f"weights_layout_{tag}", grid=(rows // tr,), out_shape=SDS((rows, PROJ_W), slabs.dtype),
        in_specs=[BS((n_slabs, tr, width), lambda i: (0, i, 0))], out_specs=BS((tr, PROJ_W), lambda i: (i, 0)),
        compiler_params=_params(1),
    )(slabs)


def _from_layout(dw, n_slabs):
    width = REF_W // n_slabs
    slabs = []
    for j in range(n_slabs):
        pieces = []
        for lo, hi, at in sorted(LAYOUT_RUNS):
            a, b = max(lo, j * width), min(hi, (j + 1) * width)
            if a < b:
                pieces.append(dw[:, at + a - lo:at + b - lo])
        slabs.append(jnp.concatenate(pieces, axis=1))
    return slabs


def _lower_bounds(lb):
    sm = jax.nn.softmax(lb.astype(F32), axis=0)
    return jnp.cumsum(sm, axis=0) - sm[0]


def kernel(x, meta_tokens, norm_w, w_in, conv_w, a_log, dt_bias, gnorm_a, gnorm_b, hgrn_lower_bounds, w_branch_a, w_branch_b, w_out, final_norm_w, loss_target, m_meta_tokens, m_norm_w, m_w_in, m_conv_w, m_a_log, m_dt_bias, m_gnorm_a, m_gnorm_b, m_hgrn_lower_bounds, m_w_branch_a, m_w_branch_b, m_w_out, m_final_norm_w, v_meta_tokens, v_norm_w, v_w_in, v_conv_w, v_a_log, v_dt_bias, v_gnorm_a, v_gnorm_b, v_hgrn_lower_bounds, v_w_branch_a, v_w_branch_b, v_w_out, v_final_norm_w):
    nseq, seq, _ = x.shape
    depth = norm_w.shape[0]
    t_len = N_PAD + N_META + seq
    n = nseq * t_len
    conv_c = conv_w.shape[2]
    my = 4 * lax.axis_index("x") + 2 * lax.axis_index("y") + lax.axis_index("c")

    assert depth >= 2
    by_cols = lambda g: g.transpose(1, 2, 0, 3).reshape(g.shape[1], g.shape[2], N_DEV * g.shape[3])
    first = _all_gather_hbm([w_in[:1].astype(BF16), conv_w, meta_tokens], "gather_first")
    later_flight, later_token = _send_all_start(
        [w_in[1:].astype(BF16), w_branch_a.astype(BF16), w_branch_b.astype(BF16), w_out.astype(BF16)], False,
        "gather_later_start", after=first[0])
    w_in_slabs = [first[0]]
    conv_full = by_cols(first[1])
    meta_full = first[2].transpose(1, 0, 2).reshape(N_META, D_MODEL)

    lb_all, lb_vjp = jax.vjp(_lower_bounds, hgrn_lower_bounds)

    h = jnp.concatenate([jnp.zeros((nseq, N_PAD, D_MODEL), F32),
                         jnp.broadcast_to(meta_full[None], (nseq, N_META, D_MODEL)), x], axis=1).reshape(n, D_MODEL)
    saved = []
    for l in range(depth):
        wp = _to_layout(w_in_slabs[0][:, 0] if l == 0 else w_in_slabs[1][:, l - 1], l)
        nw8 = _row8(norm_w[l], D_MODEL)
        if l == 0:
            nw8 = nw8 + later_token[0:1, 0:1]
        cw8 = _row8(conv_full[l], 3 * HEADS_W)
        aux = _row8(jnp.stack([a_log[l], dt_bias[l]]), LANES)
        lb8 = _row8(lb_all[l], HEADS_W)
        gn8 = _row8(jnp.stack([gnorm_a[l], gnorm_b[l]]), LANES)
        proj, xn = _proj_fwd(h, nw8, wp, l)
        q, k, v, b, g, qb, kb, lf = _prep_fwd(proj, cw8, aux, lb8, nseq, t_len, l)
        oa, ob, sck_a, sck_b = _mixers_fwd(q, k, v, b, g, qb, kb, proj, C_BI // HEADS_W, lf, nseq, t_len, l)
        if l == 0:
            sent, landed = _send_all_wait(later_flight, ob, "gather_later_wait")
            landed = [lax.dynamic_update_slice(ld, own[None], (my,) + (0,) * own.ndim) for ld, own in zip(landed, sent)]
            w_in_slabs.append(landed[0])
            wa_full, wb_full = by_cols(landed[1]), by_cols(landed[2])
            wout_full = landed[3].transpose(1, 0, 2, 3).reshape(depth, D_MODEL, D_MODEL)
        wa_l, wb_l, wout_l = wa_full[l], wb_full[l], wout_full[l]
        h_next = _post_fwd(oa, ob, proj, h, gn8, wa_l, wb_l, wout_l, l)
        saved.append(dict(h=h, wp=wp, nw8=nw8, cw8=cw8, aux=aux, lb8=lb8, gn8=gn8, proj=proj, xn=xn, q=q, k=k, v=v, b=b,
                          wa=wa_l, wb=wb_l, wout=wout_l,
                          g=g, qb=qb, kb=kb, lf=lf, oa=oa, ob=ob, sck_a=sck_a, sck_b=sck_b))
        h = h_next

    dh, acc = _loss_head(h, _row8(final_norm_w, D_MODEL), loss_target, nseq, t_len)

    g_win, g_wa, g_wb, g_wout, g_conv, small = [], [], [], [], [], []

    def mixer_slabs(dwa_s, dwb_s, dwout_s):
        nl = len(dwa_s)
        rows = lambda a: jnp.stack(a).reshape(nl * HEADS_W, N_DEV, LANES).transpose(1, 0, 2)
        wout = jnp.stack(dwout_s).reshape(nl, N_DEV, LANES, D_MODEL).transpose(1, 0, 2, 3)
        return [jnp.concatenate([rows(dwa_s), rows(dwb_s)], axis=1).astype(BF16),
                wout.reshape(N_DEV, nl * LANES, D_MODEL).astype(BF16)]

    def win_slabs(per_layer, dtype):
        return jnp.stack([jnp.concatenate([sl[j] for sl in per_layer], axis=0) for j in range(N_DEV)]).astype(dtype)

    for l in reversed(range(depth)):
        s = saved[l]
        gn8, aux = s["gn8"], s["aux"]
        if l == 0:
            later_flight, later_token = _send_all_start(
                [win_slabs(g_win[::-1], BF16)] + mixer_slabs(g_wa[::-1], g_wb[::-1], g_wout[::-1]), True,
                "scatter_later_start")
            gn8 = gn8 + later_token[0:1, 0:1]
        doa, dob, dz, dbg, dga, dgb, dwa, dwb, dwout, dgn = _post_bwd(
            dh, s["oa"], s["ob"], s["proj"], s["h"], gn8, s["wa"], s["wb"], s["wout"], l)
        dq, dk, dv, db, dg, dqb, dkb, dbi, dlf = _mixers_bwd(
            s["q"], s["k"], s["v"], s["b"], s["g"], s["qb"], s["kb"], s["proj"], C_BI // HEADS_W, s["lf"], s["sck_a"],
            s["sck_b"], doa, dob, nseq, t_len, l)
        if l == 0:
            mixer_flight, mixer_token = _send_all_start(mixer_slabs([dwa], [dwb], [dwout]), True, "scatter_first_start")
            aux = aux + mixer_token[0:1, 0:1]
        dqkv, dmisc, dbq, dbf, dcw, daux, dlb = _prep_bwd(s["proj"], dq, dk, dv, db, dg, dqb, dkb, dlf, s["cw8"], aux,
                                                          s["lb8"], nseq, t_len, l)
        pieces = [dqkv, dz, dbq, dbf, dbi, dbg, dga, dgb, dmisc]
        g_win.append(_from_layout(_proj_bwd_w(s["xn"], pieces, l), N_DEV))
        g_conv.append(dcw[:4])
        nw8 = s["nw8"]
        if l == 0:
            dconv = jnp.stack(g_conv[::-1])
            conv_slabs = dconv.reshape(depth * dconv.shape[1], N_DEV, conv_c).transpose(1, 0, 2)
            win_flight, win_token = _send_all_start([win_slabs(g_win[-1:], BF16), conv_slabs], True, "scatter_win_start")
            nw8 = nw8 + win_token[0:1, 0:1]
        dh, dnw = _proj_bwd_x(pieces, s["wp"], s["h"], nw8, dh, l)
        g_wa.append(dwa)
        g_wb.append(dwb)
        g_wout.append(dwout)
        small.append((dnw[0], dgn[0], dgn[1], daux[0, :N_HEADS], daux[1, :N_HEADS], dlb[0]))
    small.reverse()
    dh = dh.reshape(nseq, t_len, D_MODEL)
    grad_x = dh[:, N_PAD + N_META:]

    packed = jnp.concatenate([small[0][1], small[1][1], small[0][2], small[1][2], small[0][3], small[1][3],
                              small[0][4], small[1][4]])
    tile = jnp.concatenate([
        jnp.sum(dh[:, N_PAD:N_PAD + N_META], axis=0), _row8(jnp.stack([small[0][0], small[1][0], acc[0]]), D_MODEL),
        _row8(jnp.stack([small[0][5], small[1][5]]), D_MODEL), _row8(packed, D_MODEL), _row8(acc[1], D_MODEL)], axis=0)
    tile = _all_reduce_small(tile, "reduce_small")
    loss = jnp.sum(tile[40])
    g_meta = lax.dynamic_slice_in_dim(tile[0:N_META], my * LANES, LANES, axis=1)
    g_norm, g_final = tile[16:18], tile[18]
    (g_lb,) = lb_vjp(tile[24:26, :HEADS_W])
    r21 = tile[32]
    g_gna, g_gnb = r21[0:256].reshape(2, LANES), r21[256:512].reshape(2, LANES)
    g_alog, g_dtb = r21[512:520].reshape(2, N_HEADS), r21[520:528].reshape(2, N_HEADS)

    def landed_sums(flight, tag):
        sent, landed = _send_all_wait(flight, dh, f"{tag}_wait")
        landed = [lax.dynamic_update_slice(ld, lax.dynamic_index_in_dim(src, my, 0, keepdims=True), (my, 0, 0))
                  for ld, src in zip(landed, sent)]
        return [_sum_slabs(ld, f"{tag}_sum{i}") for i, ld in enumerate(landed)]

    l_win, l_ab, l_wout = landed_sums(later_flight, "scatter_later")
    r_ab, r_wout = landed_sums(mixer_flight, "scatter_first")
    r_win, r_conv = landed_sums(win_flight, "scatter_win")
    both = lambda a, b, shape: jnp.concatenate([a.reshape(1, *shape[1:]), b.reshape(depth - 1, *shape[1:])])
    half, half_l = HEADS_W, (depth - 1) * HEADS_W
    mine = [both(r_win, l_win, w_in.shape), both(r_ab[:half], l_ab[:half_l], w_branch_a.shape),
            both(r_ab[half:], l_ab[half_l:], w_branch_b.shape), both(r_wout, l_wout, w_out.shape), r_conv]
    gseg = lambda i, shape: mine[i].reshape(shape)
    grads = {
        "meta_tokens": g_meta, "norm_w": g_norm, "w_in": gseg(0, w_in.shape), "conv_w": gseg(4, conv_w.shape),
        "a_log": g_alog, "dt_bias": g_dtb, "gnorm_a": g_gna, "gnorm_b": g_gnb, "hgrn_lower_bounds": g_lb,
        "w_branch_a": gseg(1, w_branch_a.shape), "w_branch_b": gseg(2, w_branch_b.shape), "w_out": gseg(3, w_out.shape),
        "final_norm_w": g_final}
    weights = {
        "meta_tokens": (meta_tokens, m_meta_tokens, v_meta_tokens), "norm_w": (norm_w, m_norm_w, v_norm_w),
        "w_in": (w_in, m_w_in, v_w_in), "conv_w": (conv_w, m_conv_w, v_conv_w), "a_log": (a_log, m_a_log, v_a_log),
        "dt_bias": (dt_bias, m_dt_bias, v_dt_bias), "gnorm_a": (gnorm_a, m_gnorm_a, v_gnorm_a),
        "gnorm_b": (gnorm_b, m_gnorm_b, v_gnorm_b),
        "hgrn_lower_bounds": (hgrn_lower_bounds, m_hgrn_lower_bounds, v_hgrn_lower_bounds),
        "w_branch_a": (w_branch_a, m_w_branch_a, v_w_branch_a), "w_branch_b": (w_branch_b, m_w_branch_b, v_w_branch_b),
        "w_out": (w_out, m_w_out, v_w_out), "final_norm_w": (final_norm_w, m_final_norm_w, v_final_norm_w)}
    names = list(weights)
    deltas, new_m, new_v = [], [], []
    for nm in names:
        w, m, v = weights[nm]
        view = (1,) * (3 - w.ndim) + w.shape
        d, m2, v2 = _adamw(w.reshape(view), grads[nm].reshape(view), m.reshape(view), v.reshape(view), f"adamw_{nm}")
        deltas.append(d.reshape(w.shape))
        new_m.append(m2.reshape(w.shape))
        new_v.append(v2.reshape(w.shape))
    return (loss, grad_x, *[grads[nm].reshape(weights[nm][0].shape) for nm in names], *deltas, *new_m, *new_v)
```

```python
import functools

import jax
import jax.numpy as jnp
from jax import lax
from jax.experimental import pallas as pl
from jax.experimental.pallas import tpu as pltpu

F32 = jnp.float32
BF16 = jnp.bfloat16

D_MODEL = 1024
N_HEADS = 4
D_HEAD = 128
HEADS_W = N_HEADS * D_HEAD
N_META = 16
N_PAD = 48
GDN_CHUNK = 64
HGRN_CHUNK = 16
EPS = 1e-6
N_DEV = 8
LANES = 128
SUBLANES = 8
VMEM_LIMIT = 56 * 1024 * 1024

C_QKV, C_Z, C_BQ, C_BF, C_BI, C_BG, C_GA, C_GB, C_MISC = 0, 1536, 2048, 2560, 3072, 3584, 4096, 5120, 6144
PROJ_W = 6272
REF_W = 6152

ADAM_LR, ADAM_B1, ADAM_B2, ADAM_EPS, ADAM_WD, ADAM_STEP = 0.001, 0.9, 0.999, 1e-08, 0.01, 10

MESH = pl.DeviceIdType.MESH
SDS = jax.ShapeDtypeStruct
BS = pl.BlockSpec


def _params(n_axes):
    return pltpu.CompilerParams(dimension_semantics=("arbitrary",) * n_axes, vmem_limit_bytes=VMEM_LIMIT)


def _pick(n, cands):
    for c in cands:
        if n % c == 0:
            return c
    raise ValueError(f"no tile for {n} among {cands}")


def _iota2(shape, dim):
    return lax.broadcasted_iota(jnp.int32, shape, dim)


def _dg(a, b, dims):
    return lax.dot_general(a.astype(BF16), b.astype(BF16), (dims, ((), ())), preferred_element_type=F32)


def _bdg(a, b, ca, cb):
    return lax.dot_general(a.astype(BF16), b.astype(BF16), (((ca,), (cb,)), ((0,), (0,))), preferred_element_type=F32)


@jax.custom_vjp
def _bnn(a, b):
    return _bdg(a, b, 2, 1)


@jax.custom_vjp
def _bnt(a, b):
    return _bdg(a, b, 2, 2)


@jax.custom_vjp
def _btn(a, b):
    return _bdg(a, b, 1, 1)


_bnn.defvjp(lambda a, b: (_bnn(a, b), (a, b)), lambda r, g: (_bnt(g, r[1]), _btn(r[0], g)))
_bnt.defvjp(lambda a, b: (_bnt(a, b), (a, b)), lambda r, g: (_bnn(g, r[1]), _btn(g, r[0])))
_btn.defvjp(lambda a, b: (_btn(a, b), (a, b)), lambda r, g: (_bnt(r[1], g), _bnn(r[0], g)))


def _split2(x):
    hi = x.astype(BF16).astype(F32)
    return hi, x - hi


def _tri(bsz, n):
    return jnp.broadcast_to((_iota2((n, n), 0) >= _iota2((n, n), 1)).astype(F32), (bsz, n, n))


@jax.custom_vjp
def _cumsum_rows(x):
    tri = _tri(x.shape[0], x.shape[1])
    hi, lo = _split2(x)
    return _bdg(tri, hi, 2, 1) + _bdg(tri, lo, 2, 1)


def _cumsum_rows_bwd(_, g):
    tri = _tri(g.shape[0], g.shape[1])
    hi, lo = _split2(g)
    return (_bdg(tri, hi, 1, 1) + _bdg(tri, lo, 1, 1),)


_cumsum_rows.defvjp(lambda x: (_cumsum_rows(x), None), _cumsum_rows_bwd)


def _sigmoid(x):
    return jax.nn.sigmoid(x)


def _silu(x):
    return x * _sigmoid(x)


def _softplus(x):
    return jnp.maximum(x, 0.0) + jnp.log1p(jnp.exp(-jnp.abs(x)))


def _rms(x, w):
    return x * lax.rsqrt(jnp.mean(x * x, axis=-1, keepdims=True) + EPS) * w


@jax.custom_vjp
def _inv_unit_lower(lm):
    n = lm.shape[1]
    a = (_iota2((n, n), 0) == _iota2((n, n), 1)).astype(F32)[None] - lm
    steps = max(1, (n - 1).bit_length()) - 1
    p = _bnn(lm, lm)
    for i in range(steps):
        if i == steps - 1:
            a = a + _bnn(a, p)
        else:
            both = _bnn(jnp.concatenate([a, p], axis=1), p)
            a, p = a + both[:, :n], both[:, n:]
    return a


_inv_unit_lower.defvjp(lambda lm: (lambda a: (a, a))(_inv_unit_lower(lm)),
                       lambda a, g: (-_bnt(_btn(a, g), a),))


def _gdn_chunk(q, k, v, b_b, g_b, s):
    n, dv = q.shape[1], v.shape[2]
    r, c = _iota2((n, n), 0), _iota2((n, n), 1)
    causal, strict, eye = (r >= c)[None], (r > c)[None], (r == c)[None]
    g_cum = _cumsum_rows(g_b)
    g_i = g_cum[:, :, :n]
    g_j = jnp.sum(jnp.where(eye, g_i, 0.0), axis=1, keepdims=True)
    decay = jnp.where(causal, jnp.exp(jnp.where(causal, g_i - g_j, 0.0)), 0.0)
    e_g = jnp.exp(g_cum)
    kb = k * b_b
    kk = _bnt(jnp.concatenate([kb, q], axis=1), k)
    a_inv = _inv_unit_lower(jnp.where(strict, kk[:, :n] * decay, 0.0))
    uw = _bnn(a_inv, jnp.concatenate([v * b_b, kb * e_g], axis=2))
    ws = _bnn(jnp.concatenate([uw[:, :, dv:], q * e_g], axis=1), s)
    v_new = uw[:, :, :dv] - ws[:, :n]
    o = ws[:, n:] + _bnn(kk[:, n:] * decay, v_new)
    g_last = g_cum[:, n - 1:n, :]
    s_new = s * jnp.exp(g_last) +_btn(k * jnp.exp(g_last - g_cum), v_new)
    return o, s_new


@functools.partial(jax.custom_vjp, nondiff_argnums=(1, 2))
def _row(x, j, n):
    return x[:, j:j + 1, :]


def _row_bwd(j, n, _, g):
    return (jnp.where(_iota2((1, n, 1), 1) == j, g, 0.0),)


_row.defvjp(lambda x, j, n: (_row(x, j, n), None), _row_bwd)


def _hgrn_pairs(q, k, v, b_cum):
    n = q.shape[1]
    half = n // 2 if n > SUBLANES else n
    parts = []
    for lo in range(0, n, half):
        qs, bs = q[:, lo:], b_cum[:, lo:]
        rows = _iota2((1, n - lo, 1), 1) + lo
        acc = jnp.zeros_like(qs)
        for j in range(lo, lo + half):
            p = jnp.exp(jnp.where(rows >= j, bs - _row(b_cum, j, n), -1e30))
            acc = acc + jnp.sum(qs * _row(k, j, n) * p, axis=2, keepdims=True) * _row(v, j, n)
        parts.append(acc)
    if len(parts) == 1:
        return parts[0]
    return parts[0] + jnp.concatenate([jnp.zeros_like(parts[1]), parts[1]], axis=1)


def _hgrn_block(q, k, v, lf, st, group=HGRN_CHUNK):
    n, rows = group, q.shape[1]
    b_cum = _cumsum_rows(lf)
    outs = []
    for c in range(rows // n):
        rs = slice(c * n, (c + 1) * n)
        o = _hgrn_pairs(q[:, rs], k[:, rs], v[:, rs], b_cum[:, rs])
        if c:
            b_c = _row(b_cum, c * n - 1, rows)
            scores = _bnt(q[:, rs] * jnp.exp(b_cum[:, rs] - b_c), k[:, :c * n] * jnp.exp(b_c - b_cum[:, :c * n]))
            o = o + _bnn(scores, v[:, :c * n])
        outs.append(o)
    b_last = _row(b_cum, rows - 1, rows)
    o = _bnt(q * jnp.exp(b_cum), st) + jnp.concatenate(outs, axis=1)
    return o, st * jnp.exp(b_last) + _btn(v, k * jnp.exp(b_last - b_cum))


def _l2n_act(y, scale):
    a = _silu(y)
    return a * lax.rsqrt(jnp.sum(a * a, axis=-1, keepdims=True) + EPS) * scale


def _col(x, lane):
    return jnp.sum(jnp.where(_iota2(x.shape, 1) == lane, x, 0.0), axis=1, keepdims=True)


def _elem(x, row, lane):
    m = (_iota2(x.shape, 0) == row) & (_iota2(x.shape, 1) == lane)
    return jnp.sum(jnp.sum(jnp.where(m, x, 0.0), axis=1, keepdims=True), axis=0, keepdims=True)


def _gdn_gates(misc, aux, real, head):
    beta = _sigmoid(_col(misc, head))
    g = -jnp.exp(_elem(aux, 0, head)) * _softplus(_col(misc, N_HEADS + head) + _elem(aux, 1, head))
    g = jnp.where(real, g, 0.0)
    shape = (misc.shape[0], D_HEAD)
    return jnp.broadcast_to(beta, shape), jnp.broadcast_to(g, shape)


def _hgrn_prep(bq, bf, lb, real):
    qb = _silu(bq) * (D_HEAD ** -0.5)
    log_sig = jnp.minimum(bf, 0.0) - jnp.log1p(jnp.exp(-jnp.abs(bf)))
    pos = lb > 0.0
    lbs = jnp.where(pos, lb, 0.5)
    a = jnp.log(lbs)
    b = jnp.log1p(-lbs) + log_sig
    lae = jnp.maximum(a, b) + jnp.log1p(jnp.exp(-jnp.abs(a - b)))
    lf = jnp.where(pos, lae, log_sig)
    kb = jnp.where(pos, 1.0 - lbs, 1.0) * _sigmoid(-bf)
    return qb, jnp.where(real, kb, 0.0), jnp.where(real, lf, 0.0)


def _gated_norm(o, z, gw):
    return o * lax.rsqrt(jnp.mean(o * o, axis=-1, keepdims=True) + EPS) * gw * _silu(z)


def _shift_down(x, j):
    return x if j == 0 else pltpu.roll(x, j, 0)


def _shift_up(x, j):
    return x if j == 0 else pltpu.roll(x, x.shape[0] - j, 0)


def _all_gather_hbm(blocks, name):
    na = len(blocks)

    def body(*refs):
        x_refs, out_refs = refs[:na], refs[na:2 * na]
        send_sems, recv_sems, local_sems = refs[2 * na:]
        mx, my, mc = lax.axis_index("x"), lax.axis_index("y"), lax.axis_index("c")
        me, sibling = (mx, my, mc), (mx, my, 1 - mc)
        chips = [(1 - mx, my), (mx, 1 - my), (1 - mx, 1 - my)]

        def slab(a, px, py, pc):
            return out_refs[a].at[4 * px + 2 * py + pc]

        def copy(a, k, blk, to, own=False):
            return pltpu.make_async_remote_copy(
                src_ref=x_refs[a] if own else slab(a, *blk), dst_ref=slab(a, *blk),
                send_sem=send_sems.at[7 * a + k], recv_sem=recv_sems.at[7 * a + k], device_id=to, device_id_type=MESH)

        mine = [pltpu.make_async_copy(x_refs[a], slab(a, *me), local_sems.at[a]) for a in range(na)]
        for cp in mine:
            cp.start()
        first = [copy(a, 0, me, sibling, own=True) for a in range(na)]
        first += [copy(a, 1 + j, me, (*chip, mc), own=True) for j, chip in enumerate(chips) for a in range(na)]
        for cp in first:
            cp.start()
        passed = []
        for j, chip in enumerate(chips):
            for a in range(na):
                copy(a, 1 + j, (*chip, mc), me).wait_recv()
                passed.append(copy(a, 4 + j, (*chip, mc), sibling))
                passed[-1].start()
        for a in range(na):
            copy(a, 0, sibling, me).wait_recv()
            for j, chip in enumerate(chips):
                copy(a, 4 + j, (*chip, 1 - mc), me).wait_recv()
        for cp in first + passed:
            cp.wait_send()
        for cp in mine:
            cp.wait()

    hbm = BS(memory_space=pl.ANY)
    return pl.pallas_call(
        body, name=name, out_shape=[SDS((N_DEV, *b.shape), b.dtype) for b in blocks],
        in_specs=[hbm] * na, out_specs=[hbm] * na,
        scratch_shapes=[pltpu.SemaphoreType.DMA((7 * na,)), pltpu.SemaphoreType.DMA((7 * na,)),
                        pltpu.SemaphoreType.DMA((na,))],
    )(*blocks)


def _all_reduce_small(block, name):
    r, c = block.shape

    def body(x_ref, out_ref, buf, send_sems, recv_sems):
        mx, my, mc = lax.axis_index("x"), lax.axis_index("y"), lax.axis_index("c")
        me, sibling = (mx, my, mc), (mx, my, 1 - mc)
        chips = [(1 - mx, my), (mx, 1 - my), (1 - mx, 1 - my)]

        def slab(px, py, pc):
            return buf.at[4 * px + 2 * py + pc]

        def copy(k, blk, to, src=None):
            return pltpu.make_async_remote_copy(
                src_ref=slab(*blk) if src is None else src, dst_ref=slab(*blk),
                send_sem=send_sems.at[k], recv_sem=recv_sems.at[k], device_id=to, device_id_type=MESH)

        first = [copy(0, me, sibling, src=x_ref)]
        first += [copy(1 + j, me, (*chip, mc), src=x_ref) for j, chip in enumerate(chips)]
        for cp in first:
            cp.start()
        passed = [copy(4 + j, (*chip, mc), sibling) for j, chip in enumerate(chips)]
        for j, chip in enumerate(chips):
            copy(1 + j, (*chip, mc), me).wait_recv()
            passed[j].start()
        copy(0, sibling, me).wait_recv()
        for j, chip in enumerate(chips):
            copy(4 + j, (*chip, 1 - mc), me).wait_recv()
        for cp in first + passed:
            cp.wait_send()
        buf[4 * mx + 2 * my + mc] = x_ref[...]
        acc = buf[0]
        for d in range(1, N_DEV):
            acc = acc + buf[d]
        out_ref[...] = acc

    return pl.pallas_call(
        body, name=name, out_shape=SDS((r, c), F32),
        in_specs=[BS(memory_space=pltpu.VMEM)], out_specs=BS(memory_space=pltpu.VMEM),
        scratch_shapes=[pltpu.VMEM((N_DEV, r, c), F32), pltpu.SemaphoreType.DMA((7,)), pltpu.SemaphoreType.DMA((7,))],
    )(block)


HBM_SPEC = BS(memory_space=pltpu.HBM)
SEM_SPEC = BS(memory_space=pltpu.SEMAPHORE)
SIDE_EFFECT = pltpu.SideEffectType.DATAFLOW_SIDE_EFFECTING


def _peer(rel):
    flip = lambda v, bit: 1 - v if bit else v
    return (flip(lax.axis_index("x"), rel >> 2 & 1), flip(lax.axis_index("y"), rel >> 1 & 1),
            flip(lax.axis_index("c"), rel & 1))


def _send_all_start(blocks, scatter, name, after=None):
    na = len(blocks)
    shapes = [b.shape[1:] if scatter else b.shape for b in blocks]
    n_in = 2 * na + (after is not None)

    def body(*refs):
        srcs, lands = refs[:na], refs[na:2 * na]
        send_sems, recv_sems, token = refs[n_in], refs[n_in + 1], refs[-1]
        me = 4 * lax.axis_index("x") + 2 * lax.axis_index("y") + lax.axis_index("c")
        for a in range(na):
            for rel in range(1, N_DEV):
                px, py, pc = _peer(rel)
                pltpu.make_async_remote_copy(
                    src_ref=srcs[a].at[4 * px + 2 * py + pc] if scatter else srcs[a], dst_ref=lands[a].at[me],
                    send_sem=send_sems.at[7 * a + rel - 1], recv_sem=recv_sems.at[7 * a + rel - 1],
                    device_id=(px, py, pc), device_id_type=MESH).start()
        token[...] = jnp.zeros_like(token)

    lands = [lax.empty((N_DEV, *s), b.dtype) for s, b in zip(shapes, blocks)]
    res = pl.pallas_call(
        body, name=name,
        out_shape=([pltpu.SemaphoreType.DMA((7 * na,)), pltpu.SemaphoreType.DMA((7 * na,))]
                   + [pltpu.HBM(b.shape, b.dtype) for b in blocks] + [pltpu.HBM(ld.shape, ld.dtype) for ld in lands]
                   + [SDS((SUBLANES, LANES), F32)]),
        in_specs=[HBM_SPEC] * (2 * na) + [BS(memory_space=pl.ANY)] * (after is not None),
        out_specs=[SEM_SPEC, SEM_SPEC] + [HBM_SPEC] * (2 * na) + [BS(memory_space=pltpu.VMEM)],
        input_output_aliases={i: 2 + i for i in range(2 * na)},
        compiler_params=pltpu.CompilerParams(has_side_effects=SIDE_EFFECT),
    )(*[pltpu.with_memory_space_constraint(b, pltpu.HBM) for b in blocks],
      *[pltpu.with_memory_space_constraint(ld, pltpu.HBM) for ld in lands], *([] if after is None else [after]))
    return dict(send=res[0], recv=res[1], srcs=res[2:2 + na], lands=res[2 + na:2 + 2 * na], scatter=scatter), res[-1]


def _send_all_wait(flight, after, name):
    na = len(flight["srcs"])

    def body(*refs):
        srcs, lands = refs[:na], refs[na:2 * na]
        send_sems, recv_sems = refs[2 * na], refs[2 * na + 1]
        for a in range(na):
            for rel in range(1, N_DEV):
                cp = pltpu.make_async_remote_copy(
                    src_ref=srcs[a].at[0] if flight["scatter"] else srcs[a], dst_ref=lands[a].at[0],
                    send_sem=send_sems.at[7 * a + rel - 1], recv_sem=recv_sems.at[7 * a + rel - 1],
                    device_id=_peer(rel), device_id_type=MESH)
                cp.wait_send()
                cp.wait_recv()

    arrays = list(flight["srcs"]) + list(flight["lands"])
    res = pl.pallas_call(
        body, name=name, out_shape=[pltpu.HBM(a.shape, a.dtype) for a in arrays],
        in_specs=[HBM_SPEC] * (2 * na) + [SEM_SPEC, SEM_SPEC, BS(memory_space=pl.ANY)], out_specs=[HBM_SPEC] * (2 * na),
        input_output_aliases={i: i for i in range(2 * na)},
        compiler_params=pltpu.CompilerParams(has_side_effects=SIDE_EFFECT),
    )(*arrays, flight["send"], flight["recv"], after)
    return res[:na], res[na:]


def _sum_slabs(land, name):
    _, r, c = land.shape
    tr = _pick(r, (256, 128, 64, 32, 16, 8))

    def body(l_ref, o_ref):
        acc = l_ref[0].astype(F32)
        for d in range(1, N_DEV):
            acc = acc + l_ref[d].astype(F32)
        o_ref[...] = acc

    return pl.pallas_call(
        body, name=name, grid=(r // tr,), out_shape=SDS((r, c), F32),
        in_specs=[BS((N_DEV, tr, c), lambda j: (0, j, 0))], out_specs=BS((tr, c), lambda j: (j, 0)),
        compiler_params=_params(1),
    )(land)


def _proj_fwd(h, nw8, wp, tag):
    n = h.shape[0]
    tm = _pick(n, (2112, 1408, 768, 512, 384, 256, 192, 128, 64))
    tn = 896

    def body(h_ref, nw_ref, w_ref, proj_ref, xn_ref):
        @pl.when(pl.program_id(1) == 0)
        def _():
            xn_ref[...] = _rms(h_ref[...], nw_ref[0:1, :]).astype(BF16)

        proj_ref[...] = jnp.dot(xn_ref[...], w_ref[...], preferred_element_type=F32)

    return pl.pallas_call(
        body, name=f"proj_fwd_{tag}", grid=(n // tm, PROJ_W // tn),
        in_specs=[BS((tm, D_MODEL), lambda i, j: (i, 0)), BS((SUBLANES, D_MODEL), lambda i, j: (0, 0)),
                  BS((D_MODEL, tn), lambda i, j: (0, j))],
        out_specs=[BS((tm, tn), lambda i, j: (i, j)), BS((tm, D_MODEL), lambda i, j: (i, 0))],
        out_shape=[SDS((n, PROJ_W), F32), SDS((n, D_MODEL), BF16)], compiler_params=_params(2),
    )(h, nw8, wp)


def _conv_ext(x_ext, cw_ref):
    y = x_ext * cw_ref[3:4, :]
    for k in range(3):
        y = y + _shift_down(x_ext, 3 - k) * cw_ref[k:k + 1, :]
    return y[SUBLANES:]


def _prep_fwd(proj, cw8, aux, lb8, nseq, t_len, tag):
    n = proj.shape[0]
    tt = _pick(t_len, (704, 192, 128, 64))
    nt_ = t_len // tt
    qkv_w = 3 * HEADS_W

    def body(cur_ref, prev_ref, misc_ref, bq_ref, bf_ref, cw_ref, aux_ref, lb_ref,
             q_ref, k_ref, v_ref, b_ref, g_ref, qb_ref, kb_ref, lf_ref, ext_ref):
        t = pl.program_id(1)
        ext_ref[0:SUBLANES, :] = jnp.where(t == 0, 0.0, prev_ref[...])
        ext_ref[SUBLANES:, :] = cur_ref[...]
        y = ext_ref[SUBLANES:, :] * cw_ref[3:4, :]
        for kk in range(3):
            y = y + ext_ref[SUBLANES - 3 + kk:SUBLANES - 3 + kk + tt, :] * cw_ref[kk:kk + 1, :]
        real = (t * tt + _iota2((tt, 1), 0)) >= N_PAD
        misc = misc_ref[...]
        auxv = aux_ref[...]
        for hd in range(N_HEADS):
            sl = slice(hd * D_HEAD, (hd + 1) * D_HEAD)
            q_ref[:, sl] = _l2n_act(y[:, sl], D_HEAD ** -0.5)
            k_ref[:, sl] = _l2n_act(y[:, HEADS_W + hd * D_HEAD:HEADS_W + (hd + 1) * D_HEAD], 1.0)
            v_ref[:, sl] = _silu(y[:, 2 * HEADS_W + hd * D_HEAD:2 * HEADS_W + (hd + 1) * D_HEAD])
            b_ref[:, sl], g_ref[:, sl] = _gdn_gates(misc, auxv, real, hd)
        qb_ref[...], kb_ref[...], lf_ref[...] = _hgrn_prep(bq_ref[...], bf_ref[...], lb_ref[0:1, :], real)

    rb = tt // SUBLANES
    row = lambda s, t: s * nt_ + t
    wide = BS((tt, HEADS_W), lambda s, t: (row(s, t), 0))
    return pl.pallas_call(
        body, name=f"prep_fwd_{tag}", grid=(nseq, nt_),
        in_specs=[BS((tt, qkv_w), lambda s, t: (row(s, t), 0)),
                  BS((SUBLANES, qkv_w), lambda s, t: (jnp.maximum(row(s, t) * rb - 1, 0), 0)),
                  BS((tt, LANES), lambda s, t: (row(s, t), C_MISC // LANES)),
                  BS((tt, HEADS_W), lambda s, t: (row(s, t), C_BQ // HEADS_W)),
                  BS((tt, HEADS_W), lambda s, t: (row(s, t), C_BF // HEADS_W)),
                  BS((SUBLANES, qkv_w), lambda s, t: (0, 0)), BS((SUBLANES, LANES), lambda s, t: (0, 0)),
                  BS((SUBLANES, HEADS_W), lambda s, t: (0, 0))],
        out_specs=[wide] * 8, out_shape=[SDS((n, HEADS_W), F32)] * 8,
        scratch_shapes=[pltpu.VMEM((tt + SUBLANES, qkv_w), F32)], compiler_params=_params(2),
    )(proj, proj, proj, proj, proj, cw8, aux, lb8)


GDN_SEQS = 4
HGRN_SEQS = 2


def _seq_block(nseq, most):
    return max(s for s in (1, 2, 4) if s <= most and nseq % s == 0)


def _to_chains(x):
    return jnp.concatenate([x[:, :, hd * D_HEAD:(hd + 1) * D_HEAD] for hd in range(N_HEADS)], axis=0)


def _from_chains(ref, rows, val):
    sb = val.shape[0] // N_HEADS
    for hd in range(N_HEADS):
        ref[:, rows, hd * D_HEAD:(hd + 1) * D_HEAD] = val[hd * sb:(hd + 1) * sb].astype(ref.dtype)


def _mixers_fwd(q, k, v, b, g, qb, kb, vb, vb_col, lf, nseq, t_len, tag):
    sb, hs = _seq_block(nseq, GDN_SEQS), _seq_block(nseq, HGRN_SEQS)
    nc = t_len // GDN_CHUNK
    chains = N_HEADS * sb

    def body(q_ref, k_ref, v_ref, b_ref, g_ref, qb_ref, kb_ref, vb_ref, lf_ref, oa_ref, ob_ref, cka_ref, ckb_ref,
             sa_ref, sb_ref):
        @pl.when(pl.program_id(1) == 0)
        def _():
            sa_ref[...] = jnp.zeros_like(sa_ref)
            sb_ref[...] = jnp.zeros_like(sb_ref)

        s = sa_ref[...]
        cka_ref[...] = s
        o, s_new = _gdn_chunk(*[_to_chains(r[...]) for r in (q_ref, k_ref, v_ref, b_ref, g_ref)], s)
        _from_chains(oa_ref, slice(None), o)
        sa_ref[...] = s_new
        for part in range(sb // hs):
            seqs, ch = slice(part * hs, (part + 1) * hs), slice(part * N_HEADS * hs, (part + 1) * N_HEADS * hs)
            s = sb_ref[ch]
            ckb_ref[ch] = s
            o, s_new = _hgrn_block(*[_to_chains(r[seqs]) for r in (qb_ref, kb_ref, vb_ref, lf_ref)], s)
            for hd in range(N_HEADS):
                ob_ref[seqs, :, hd * D_HEAD:(hd + 1) * D_HEAD] = o[hd * hs:(hd + 1) * hs]
            sb_ref[ch] = s_new

    blk = lambda cb: BS((sb, GDN_CHUNK, HEADS_W), lambda p, c: (p, c, cb))
    ck_spec = BS((None, None, chains, D_HEAD, D_HEAD), lambda p, c: (p, c, 0, 0, 0))
    ck_shape = SDS((nseq // sb, nc, chains, D_HEAD, D_HEAD), F32)
    view = lambda a: a.reshape(nseq, t_len, a.shape[1])
    oa, ob, cka, ckb = pl.pallas_call(
        body, name=f"mixers_fwd_{tag}", grid=(nseq // sb, nc),
        in_specs=[blk(0)] * 7 + [blk(vb_col), blk(0)], out_specs=[blk(0), blk(0), ck_spec, ck_spec],
        out_shape=[SDS((nseq, t_len, HEADS_W), F32)] * 2 + [ck_shape] * 2,
        scratch_shapes=[pltpu.VMEM((chains, D_HEAD, D_HEAD), F32)] * 2, compiler_params=_params(2),
    )(*[view(a) for a in (q, k, v, b, g, qb, kb, vb, lf)])
    return oa.reshape(-1, HEADS_W), ob.reshape(-1, HEADS_W), cka, ckb


def _mixers_bwd(q, k, v, b, g, qb, kb, vb, vb_col, lf, cka, ckb, doa, dob, nseq, t_len, tag):
    sb, hs = _seq_block(nseq, GDN_SEQS), _seq_block(nseq, HGRN_SEQS)
    nc = t_len // GDN_CHUNK
    chains = N_HEADS * sb

    def body(q_ref, k_ref, v_ref, b_ref, g_ref, qb_ref, kb_ref, vb_ref, lf_ref, doa_ref, dob_ref, cka_ref, ckb_ref,
             dq_ref, dk_ref, dv_ref, db_ref, dg_ref, dqb_ref, dkb_ref, dvb_ref, dlf_ref, dsa_ref, dsb_ref):
        @pl.when(pl.program_id(1) == 0)
        def _():
            dsa_ref[...] = jnp.zeros_like(dsa_ref)
            dsb_ref[...] = jnp.zeros_like(dsb_ref)

        _, vjp = jax.vjp(_gdn_chunk, *[_to_chains(r[...]) for r in (q_ref, k_ref, v_ref, b_ref, g_ref)], cka_ref[...])
        grads = vjp((_to_chains(doa_ref[...]), dsa_ref[...]))
        for ref, val in zip((dq_ref, dk_ref, dv_ref, db_ref, dg_ref), grads[:5]):
            _from_chains(ref, slice(None), val)
        dsa_ref[...] = grads[5]
        for part in range(sb // hs):
            seqs, ch = slice(part * hs, (part + 1) * hs), slice(part * N_HEADS * hs, (part + 1) * N_HEADS * hs)
            _, vjp = jax.vjp(functools.partial(_hgrn_block, group=SUBLANES),
                             *[_to_chains(r[seqs]) for r in (qb_ref, kb_ref, vb_ref, lf_ref)], ckb_ref[ch])
            grads = vjp((_to_chains(dob_ref[seqs]), dsb_ref[ch]))
            for ref, val in zip((dqb_ref, dkb_ref, dvb_ref, dlf_ref), grads[:4]):
                for hd in range(N_HEADS):
                    ref[seqs, :, hd * D_HEAD:(hd + 1) * D_HEAD] = val[hd * hs:(hd + 1) * hs].astype(ref.dtype)
            dsb_ref[ch] = grads[4]

    blk = lambda cb: BS((sb, GDN_CHUNK, HEADS_W), lambda p, c: (p, nc - 1 - c, cb))
    ck_spec = BS((None, None, chains, D_HEAD, D_HEAD), lambda p, c: (p, nc - 1 - c, 0, 0, 0))
    view = lambda a: a.reshape(nseq, t_len, a.shape[1])
    dts = [F32] * 7 + [BF16, F32]
    res = pl.pallas_call(
        body, name=f"mixers_bwd_{tag}", grid=(nseq // sb, nc),
        in_specs=[blk(0)] * 7 + [blk(vb_col), blk(0), blk(0), blk(0), ck_spec, ck_spec], out_specs=[blk(0)] * 9,
        out_shape=[SDS((nseq, t_len, HEADS_W), dt) for dt in dts],
        scratch_shapes=[pltpu.VMEM((chains, D_HEAD, D_HEAD), F32)] * 2, compiler_params=_params(2),
    )(*[view(a) for a in (q, k, v, b, g, qb, kb, vb, lf, doa, dob)], cka, ckb)
    return [r.reshape(-1, HEADS_W) for r in res]


def _post_values(oa_ref, ob_ref, z_ref, bg_ref, ga_ref, gb_ref, gn_ref, wa_ref, wb_ref, ya_ref, yb_ref):
    for hd in range(N_HEADS):
        sl = slice(hd * D_HEAD, (hd + 1) * D_HEAD)
        ya_ref[:, sl] = _gated_norm(oa_ref[:, sl], z_ref[:, sl], gn_ref[0:1, :]).astype(BF16)
        yb_ref[:, sl] = _gated_norm(ob_ref[:, sl], bg_ref[:, sl], gn_ref[1:2, :]).astype(BF16)
    pa = jnp.dot(ya_ref[...], wa_ref[...], preferred_element_type=F32)
    pb = jnp.dot(yb_ref[...], wb_ref[...], preferred_element_type=F32)
    return pa, pb, _sigmoid(ga_ref[...]), _sigmoid(gb_ref[...])


def _post_specs(tm):
    r2 = lambda i: (i, 0)
    return [BS((tm, HEADS_W), r2), BS((tm, HEADS_W), r2),
            BS((tm, HEADS_W), lambda i: (i, C_Z // HEADS_W)), BS((tm, HEADS_W), lambda i: (i, C_BG // HEADS_W)),
            BS((tm, D_MODEL), lambda i: (i, C_GA // D_MODEL)), BS((tm, D_MODEL), lambda i: (i, C_GB // D_MODEL)),
            BS((tm, D_MODEL), r2), BS((SUBLANES, LANES), lambda i: (0, 0))]


def _post_fwd(oa, ob, proj, h, gn8, wa, wb, wout, tag):
    n = h.shape[0]
    tm = _pick(n, (768, 384, 256, 192, 128, 64))

    def body(oa_ref, ob_ref, z_ref, bg_ref, ga_ref, gb_ref, h_ref, gn_ref, wa_ref, wb_ref, wout_ref, out_ref,
             ya_ref, yb_ref):
        pa, pb, sa, sb = _post_values(oa_ref, ob_ref, z_ref, bg_ref, ga_ref, gb_ref, gn_ref, wa_ref, wb_ref,
                                      ya_ref, yb_ref)
        mixed = (sa * pa + sb * pb).astype(BF16)
        out_ref[...] = h_ref[...] + jnp.dot(mixed, wout_ref[...], preferred_element_type=F32)

    full = lambda i: (0, 0)
    return pl.pallas_call(
        body, name=f"post_fwd_{tag}", grid=(n // tm,),
        in_specs=_post_specs(tm) + [BS((HEADS_W, D_MODEL), full), BS((HEADS_W, D_MODEL), full),
                                    BS((D_MODEL, D_MODEL), full)],
        out_specs=BS((tm, D_MODEL), lambda i: (i, 0)), out_shape=SDS((n, D_MODEL), F32),
        scratch_shapes=[pltpu.VMEM((tm, HEADS_W), BF16), pltpu.VMEM((tm, HEADS_W), BF16)], compiler_params=_params(1),
    )(oa, ob, proj, proj, proj, proj, h, gn8, wa, wb, wout)


def _post_bwd(dh, oa, ob, proj, h, gn8, wa, wb, wout, tag):
    n = h.shape[0]
    tm = _pick(n, (256, 192, 128, 64))

    def body(dh_ref, oa_ref, ob_ref, z_ref, bg_ref, ga_ref, gb_ref, h_ref, gn_ref, wa_ref, wb_ref, wout_ref,
             doa_ref, dob_ref, dz_ref, dbg_ref, dga_ref, dgb_ref, dwa_ref, dwb_ref, dwout_ref, dgn_ref,
             ya_ref, yb_ref):
        @pl.when(pl.program_id(0) == 0)
        def _():
            dwa_ref[...] = jnp.zeros_like(dwa_ref)
            dwb_ref[...] = jnp.zeros_like(dwb_ref)
            dwout_ref[...] = jnp.zeros_like(dwout_ref)
            dgn_ref[...] = jnp.zeros_like(dgn_ref)

        pa, pb, sa, sb = _post_values(oa_ref, ob_ref, z_ref, bg_ref, ga_ref, gb_ref, gn_ref, wa_ref, wb_ref,
                                      ya_ref, yb_ref)
        mixed = (sa * pa + sb * pb).astype(BF16)
        dout = dh_ref[...].astype(BF16)
        dwout_ref[...] += _dg(mixed, dout, ((0,), (0,)))
        dmixed = _dg(dout, wout_ref[...], ((1,), (1,)))
        dga_ref[...] = (dmixed * pa * sa * (1.0 - sa)).astype(BF16)
        dgb_ref[...] = (dmixed * pb * sb * (1.0 - sb)).astype(BF16)
        dpa = (dmixed * sa).astype(BF16)
        dpb = (dmixed * sb).astype(BF16)
        dwa_ref[...] += _dg(ya_ref[...], dpa, ((0,), (0,)))
        dwb_ref[...] += _dg(yb_ref[...], dpb, ((0,), (0,)))
        dya = _dg(dpa, wa_ref[...], ((1,), (1,)))
        dyb = _dg(dpb, wb_ref[...], ((1,), (1,)))
        dgn_a = jnp.zeros((1, D_HEAD), F32)
        dgn_b = jnp.zeros((1, D_HEAD), F32)
        for hd in range(N_HEADS):
            sl = slice(hd * D_HEAD, (hd + 1) * D_HEAD)
            _, vjp = jax.vjp(_gated_norm, oa_ref[:, sl], z_ref[:, sl], gn_ref[0:1, :])
            doa, dz, dgw = vjp(dya[:, sl])
            doa_ref[:, sl], dz_ref[:, sl], dgn_a = doa, dz.astype(BF16), dgn_a + dgw
            _, vjp = jax.vjp(_gated_norm, ob_ref[:, sl], bg_ref[:, sl], gn_ref[1:2, :])
            dob, dbg, dgw = vjp(dyb[:, sl])
            dob_ref[:, sl], dbg_ref[:, sl], dgn_b = dob, dbg.astype(BF16), dgn_b + dgw
        dgn_ref[0:1, :] += dgn_a
        dgn_ref[1:2, :] += dgn_b

    full = lambda i: (0, 0)
    r2 = lambda i: (i, 0)
    return pl.pallas_call(
        body, name=f"post_bwd_{tag}", grid=(n // tm,),
        in_specs=[BS((tm, D_MODEL), r2)] + _post_specs(tm) + [
            BS((HEADS_W, D_MODEL), full), BS((HEADS_W, D_MODEL), full), BS((D_MODEL, D_MODEL), full)],
        out_specs=[BS((tm, HEADS_W), r2)] * 4 + [BS((tm, D_MODEL), r2)] * 2 + [
            BS((HEADS_W, D_MODEL), full), BS((HEADS_W, D_MODEL), full), BS((D_MODEL, D_MODEL), full),
            BS((SUBLANES, LANES), full)],
        out_shape=[SDS((n, HEADS_W), F32), SDS((n, HEADS_W), F32), SDS((n, HEADS_W), BF16), SDS((n, HEADS_W), BF16),
                   SDS((n, D_MODEL), BF16), SDS((n, D_MODEL), BF16), SDS((HEADS_W, D_MODEL), F32),
                   SDS((HEADS_W, D_MODEL), F32), SDS((D_MODEL, D_MODEL), F32), SDS((SUBLANES, LANES), F32)],
        scratch_shapes=[pltpu.VMEM((tm, HEADS_W), BF16), pltpu.VMEM((tm, HEADS_W), BF16)], compiler_params=_params(1),
    )(dh, oa, ob, proj, proj, proj, proj, h, gn8, wa, wb, wout)


def _loss_head(h, fw8, target, nseq, t_len):
    n = h.shape[0]
    nc = t_len // GDN_CHUNK
    sub = _pick(nc, (11, 3, 1))
    tl, nt = sub * GDN_CHUNK, nc // sub
    inv_d = 1.0 / D_MODEL

    def body(h_ref, fw_ref, *rest):
        tgt_refs, (dh_ref, acc_ref) = rest[:sub], rest[sub:]

        @pl.when((pl.program_id(0) == 0) & (pl.program_id(1) == 0))
        def _():
            acc_ref[...] = jnp.zeros_like(acc_ref)

        frames = ((pl.program_id(1) * tl + _iota2((tl, 1), 0)) >= N_PAD + N_META).astype(F32)
        y, vjp = jax.vjp(_rms, h_ref[...], fw_ref[0:1, :])
        err = (y - jnp.concatenate([r[...] for r in tgt_refs], axis=0)) * frames
        dx, dfw = vjp(err * inv_d)
        dh_ref[...] = dx
        acc_ref[0:1, :] += dfw
        acc_ref[1:2, :] += (0.5 * inv_d) * jnp.sum(err * err, axis=0, keepdims=True)

    tgt_spec = lambda u: BS((None, GDN_CHUNK, D_MODEL), lambda s, t: (s, jnp.maximum(t * sub + u - 1, 0), 0))
    return pl.pallas_call(
        body, name="loss_head", grid=(nseq, nt),
        in_specs=[BS((tl, D_MODEL), lambda s, t: (s * nt + t, 0)), BS((SUBLANES, D_MODEL), lambda s, t: (0, 0))]
        + [tgt_spec(u) for u in range(sub)],
        out_specs=[BS((tl, D_MODEL), lambda s, t: (s * nt + t, 0)), BS((SUBLANES, D_MODEL), lambda s, t: (0, 0))],
        out_shape=[SDS((n, D_MODEL), F32), SDS((SUBLANES, D_MODEL), F32)], compiler_params=_params(2),
    )(h, fw8, *[target] * sub)


def _prep_bwd(proj, dq, dk, dv, db, dg, dqb, dkb, dlf, cw8, aux, lb8, nseq, t_len, tag):
    n = proj.shape[0]
    tt = _pick(t_len, (192, 128, 64))
    nt_ = t_len // tt
    qkv_w = 3 * HEADS_W
    rb = tt // SUBLANES
    ext = tt + SUBLANES

    def body(cur_ref, prev_ref, next_ref, misc_ref, bq_ref, bf_ref, dq_ref, dqn_ref, dk_ref, dkn_ref, dv_ref, dvn_ref,
             db_ref, dg_ref, dqb_ref, dkb_ref, dlf_ref, cw_ref, aux_ref, lb_ref,
             dqkv_ref, dmisc_ref, dbq_ref, dbf_ref, dcw_ref, daux_ref, dlb_ref, dy_ref):
        s, t = pl.program_id(0), pl.program_id(1)

        @pl.when((s == 0) & (t == 0))
        def _():
            dcw_ref[...] = jnp.zeros_like(dcw_ref)
            daux_ref[...] = jnp.zeros_like(daux_ref)
            dlb_ref[...] = jnp.zeros_like(dlb_ref)

        prev = jnp.where(t == 0, 0.0, prev_ref[...])
        x_ext = jnp.concatenate([prev, cur_ref[...], next_ref[...]], axis=0)
        y = _conv_ext(x_ext, cw_ref)
        inside = (t < nt_ - 1) | (_iota2((ext, 1), 0) < tt)
        dy_ref[0:SUBLANES, :] = jnp.zeros((SUBLANES, qkv_w), F32)
        for hd in range(N_HEADS):
            for grp, (g_ref, gn_ref, scale) in enumerate(((dq_ref, dqn_ref, D_HEAD ** -0.5), (dk_ref, dkn_ref, 1.0),
                                                          (dv_ref, dvn_ref, None))):
                lo = grp * HEADS_W + hd * D_HEAD
                sl = slice(hd * D_HEAD, (hd + 1) * D_HEAD)
                cot = jnp.concatenate([g_ref[:, sl], gn_ref[:, sl]], axis=0)
                fn = _silu if scale is None else functools.partial(_l2n_act, scale=scale)
                _, vjp = jax.vjp(fn, y[:, lo:lo + D_HEAD])
                dy_ref[SUBLANES:, lo:lo + D_HEAD] = jnp.where(inside, vjp(cot)[0], 0.0)
        dy_ext = dy_ref[...]
        dx = dy_ext * cw_ref[3:4, :]
        for kk in range(3):
            dx = dx + _shift_up(dy_ext, 3 - kk) * cw_ref[kk:kk + 1, :]
        dqkv_ref[...] = dx[SUBLANES:SUBLANES + tt].astype(BF16)
        dy_cur = dy_ext[SUBLANES:SUBLANES + tt]
        for kk in range(4):
            xs = _shift_down(x_ext, 3 - kk)[SUBLANES:SUBLANES + tt]
            dcw_ref[kk:kk + 1, :] += jnp.sum(xs * dy_cur, axis=0, keepdims=True)

        real = (t * tt + _iota2((tt, 1), 0)) >= N_PAD
        dmisc = jnp.zeros((tt, LANES), F32)
        daux = jnp.zeros((SUBLANES, LANES), F32)
        for hd in range(N_HEADS):
            sl = slice(hd * D_HEAD, (hd + 1) * D_HEAD)
            _, vjp = jax.vjp(lambda m, a: _gdn_gates(m, a, real, hd), misc_ref[...], aux_ref[...])
            dm, da = vjp((db_ref[:, sl], dg_ref[:, sl]))
            dmisc, daux = dmisc + dm, daux + da
        dmisc_ref[...] = dmisc.astype(BF16)
        daux_ref[...] += daux
        _, vjp = jax.vjp(lambda a, b, c: _hgrn_prep(a, b, c, real), bq_ref[...], bf_ref[...], lb_ref[0:1, :])
        dbq, dbf, dlb = vjp((dqb_ref[...], dkb_ref[...], dlf_ref[...]))
        dbq_ref[...], dbf_ref[...] = dbq.astype(BF16), dbf.astype(BF16)
        dlb_ref[0:1, :] += dlb

    row = lambda s, t: s * nt_ + t
    cur = lambda s, t: (row(s, t), 0)
    nxt = lambda s, t: (jnp.minimum((row(s, t) + 1) * rb, n // SUBLANES - 1), 0)
    wide = BS((tt, HEADS_W), cur)
    halo = BS((SUBLANES, HEADS_W), nxt)
    full = lambda s, t: (0, 0)
    return pl.pallas_call(
        body, name=f"prep_bwd_{tag}", grid=(nseq, nt_),
        in_specs=[BS((tt, qkv_w), cur), BS((SUBLANES, qkv_w), lambda s, t: (jnp.maximum(row(s, t) * rb - 1, 0), 0)),
                  BS((SUBLANES, qkv_w), nxt), BS((tt, LANES), lambda s, t: (row(s, t), C_MISC // LANES)),
                  BS((tt, HEADS_W), lambda s, t: (row(s, t), C_BQ // HEADS_W)),
                  BS((tt, HEADS_W), lambda s, t: (row(s, t), C_BF // HEADS_W)),
                  wide, halo, wide, halo, wide, halo, wide, wide, wide, wide, wide,
                  BS((SUBLANES, qkv_w), full), BS((SUBLANES, LANES), full), BS((SUBLANES, HEADS_W), full)],
        out_specs=[BS((tt, qkv_w), cur), BS((tt, LANES), cur), wide, wide,
                   BS((SUBLANES, qkv_w), full), BS((SUBLANES, LANES), full), BS((SUBLANES, HEADS_W), full)],
        out_shape=[SDS((n, qkv_w), BF16), SDS((n, LANES), BF16), SDS((n, HEADS_W), BF16), SDS((n, HEADS_W), BF16),
                   SDS((SUBLANES, qkv_w), F32), SDS((SUBLANES, LANES), F32), SDS((SUBLANES, HEADS_W), F32)],
        scratch_shapes=[pltpu.VMEM((tt + 2 * SUBLANES, qkv_w), F32)], compiler_params=_params(2),
    )(proj, proj, proj, proj, proj, proj, dq, dq, dk, dk, dv, dv, db, dg, dqb, dkb, dlf, cw8, aux, lb8)


def _proj_bwd_x(pieces, wp, h, nw8, dh_res, tag):
    n = h.shape[0]
    tm = _pick(n, (384, 256, 192, 128, 64))
    widths = [p.shape[1] for p in pieces]
    assert sum(widths) == PROJ_W

    def body(*refs):
        p_refs = refs[:len(pieces)]
        w_ref, h_ref, nw_ref, dres_ref, dh_ref, dnw_ref = refs[len(pieces):]

        @pl.when(pl.program_id(0) == 0)
        def _():
            dnw_ref[...] = jnp.zeros_like(dnw_ref)

        dxn, off = None, 0
        for p_ref, w in zip(p_refs, widths):
            part = _dg(p_ref[...], w_ref[:, off:off + w], ((1,), (1,)))
            dxn = part if dxn is None else dxn + part
            off += w
        _, vjp = jax.vjp(_rms, h_ref[...], nw_ref[0:1, :])
        dx, dnw = vjp(dxn)
        dh_ref[...] = dres_ref[...] + dx
        dnw_ref[0:1, :] += dnw

    r2 = lambda i: (i, 0)
    full = lambda i: (0, 0)
    return pl.pallas_call(
        body, name=f"proj_bwd_x_{tag}", grid=(n // tm,),
        in_specs=[BS((tm, w), r2) for w in widths] + [BS((D_MODEL, PROJ_W), full), BS((tm, D_MODEL), r2),
                                                      BS((SUBLANES, D_MODEL), full), BS((tm, D_MODEL), r2)],
        out_specs=[BS((tm, D_MODEL), r2), BS((SUBLANES, D_MODEL), full)],
        out_shape=[SDS((n, D_MODEL), F32), SDS((SUBLANES, D_MODEL), F32)], compiler_params=_params(1),
    )(*pieces, wp, h, nw8, dh_res)


def _proj_bwd_w(xn, pieces, tag):
    n = xn.shape[0]
    tm = _pick(n, (768, 384, 256, 192, 128, 64))
    widths = [p.shape[1] for p in pieces]
    assert sum(widths) == PROJ_W

    def body(*refs):
        x_ref, p_refs = refs[0], refs[1:1 + len(pieces)]
        o_ref, acc_ref = refs[1 + len(pieces):]

        @pl.when(pl.program_id(0) == 0)
        def _():
            acc_ref[...] = jnp.zeros_like(acc_ref)

        off = 0
        for p_ref, w in zip(p_refs, widths):
            acc_ref[:, off:off + w] += _dg(x_ref[...], p_ref[...], ((0,), (0,)))
            off += w

        @pl.when(pl.program_id(0) == pl.num_programs(0) - 1)
        def _():
            pltpu.sync_copy(acc_ref, o_ref)

    r2 = lambda i: (i, 0)
    return pl.pallas_call(
        body, name=f"proj_bwd_w_{tag}", grid=(n // tm,),
        in_specs=[BS((tm, D_MODEL), r2)] + [BS((tm, w), r2) for w in widths], out_specs=BS(memory_space=pl.ANY),
        out_shape=SDS((D_MODEL, PROJ_W), F32), scratch_shapes=[pltpu.VMEM((D_MODEL, PROJ_W), F32)],
        compiler_params=_params(1),
    )(xn, *pieces)


def _adamw(w, g, m, v, name):
    lead, rows, cols = w.shape
    tr = _pick(rows, (256, 128, 64, 32, 16, 8, 4, 2, 1)) if rows > 256 else rows

    def body(w_ref, g_ref, m_ref, v_ref, d_ref, nm_ref, nv_ref):
        gr = g_ref[...]
        m_new = ADAM_B1 * m_ref[...] + (1.0 - ADAM_B1) * gr
        v_new = ADAM_B2 * v_ref[...] + (1.0 - ADAM_B2) * jnp.square(gr)
        m_hat = m_new / (1.0 - ADAM_B1 ** ADAM_STEP)
        v_hat = v_new / (1.0 - ADAM_B2 ** ADAM_STEP)
        d_ref[...] = -ADAM_LR * (m_hat / (jnp.sqrt(v_hat) + ADAM_EPS) + ADAM_WD * w_ref[...])
        nm_ref[...] = m_new
        nv_ref[...] = v_new

    blk = BS((None, tr, cols), lambda a, i: (a, i, 0))
    return pl.pallas_call(
        body, name=name, grid=(lead, rows // tr), in_specs=[blk] * 4, out_specs=[blk] * 3,
        out_shape=[SDS((lead, rows, cols), F32)] * 3, compiler_params=_params(2),
    )(w, g, m, v)


def _row8(v, width):
    v = jnp.atleast_2d(v).astype(F32)
    return jnp.pad(v, ((0, SUBLANES - v.shape[0]), (0, width - v.shape[1])))


REF_MISC = 1536
N_MISC = 2 * N_HEADS
LAYOUT_RUNS = ((0, REF_MISC, 0), (REF_MISC + N_MISC, REF_W, REF_MISC), (REF_MISC, REF_MISC + N_MISC, C_MISC))


def _to_layout(slabs, tag):
    n_slabs, rows, width = slabs.shape
    tr = _pick(rows, (256, 128, 64, 32, 16))

    def body(x_ref, o_ref):
        off = 0
        for lo, hi, _ in sorted(LAYOUT_RUNS, key=lambda run: run[2]):
            for j in range(n_slabs):
                a, b = max(lo, j * width), min(hi, (j + 1) * width)
                if a < b:
                    o_ref[:, off:off + b - a] = x_ref[j, :, a - j * width:b - j * width]
                    off += b - a
        o_ref[:, off:] = jnp.zeros((tr, PROJ_W - off), o_ref.dtype)

    return pl.pallas_call(
        body, name=f"weights_layout_{tag}", grid=(rows // tr,), out_shape=SDS((rows, PROJ_W), slabs.dtype),
        in_specs=[BS((n_slabs, tr, width), lambda i: (0, i, 0))], out_specs=BS((tr, PROJ_W), lambda i: (i, 0)),
        compiler_params=_params(1),
    )(slabs)


def _from_layout(dw, n_slabs):
    width = REF_W // n_slabs
    slabs = []
    for j in range(n_slabs):
        pieces = []
        for lo, hi, at in sorted(LAYOUT_RUNS):
            a, b = max(lo, j * width), min(hi, (j + 1) * width)
            if a < b:
                pieces.append(dw[:, at + a - lo:at + b - lo])
        slabs.append(jnp.concatenate(pieces, axis=1))
    return slabs


def _lower_bounds(lb):
    sm = jax.nn.softmax(lb.astype(F32), axis=0)
    return jnp.cumsum(sm, axis=0) - sm[0]


def kernel(x, meta_tokens, norm_w, w_in, conv_w, a_log, dt_bias, gnorm_a, gnorm_b, hgrn_lower_bounds, w_branch_a, w_branch_b, w_out, final_norm_w, loss_target, m_meta_tokens, m_norm_w, m_w_in, m_conv_w, m_a_log, m_dt_bias, m_gnorm_a, m_gnorm_b, m_hgrn_lower_bounds, m_w_branch_a, m_w_branch_b, m_w_out, m_final_norm_w, v_meta_tokens, v_norm_w, v_w_in, v_conv_w, v_a_log, v_dt_bias, v_gnorm_a, v_gnorm_b, v_hgrn_lower_bounds, v_w_branch_a, v_w_branch_b, v_w_out, v_final_norm_w):
    nseq, seq, _ = x.shape
    depth = norm_w.shape[0]
    t_len = N_PAD + N_META + seq
    n = nseq * t_len
    conv_c = conv_w.shape[2]
    my = 4 * lax.axis_index("x") + 2 * lax.axis_index("y") + lax.axis_index("c")

    assert depth >= 2
    by_cols = lambda g: g.transpose(1, 2, 0, 3).reshape(g.shape[1], g.shape[2], N_DEV * g.shape[3])
    first = _all_gather_hbm([w_in[:1].astype(BF16), conv_w, meta_tokens], "gather_first")
    later_flight, later_token = _send_all_start(
        [w_in[1:].astype(BF16), w_branch_a.astype(BF16), w_branch_b.astype(BF16), w_out.astype(BF16)], False,
        "gather_later_start", after=first[0])
    w_in_slabs = [first[0]]
    conv_full = by_cols(first[1])
    meta_full = first[2].transpose(1, 0, 2).reshape(N_META, D_MODEL)

    lb_all, lb_vjp = jax.vjp(_lower_bounds, hgrn_lower_bounds)

    h = jnp.concatenate([jnp.zeros((nseq, N_PAD, D_MODEL), F32),
                         jnp.broadcast_to(meta_full[None], (nseq, N_META, D_MODEL)), x], axis=1).reshape(n, D_MODEL)
    saved = []
    for l in range(depth):
        wp = _to_layout(w_in_slabs[0][:, 0] if l == 0 else w_in_slabs[1][:, l - 1], l)
        nw8 = _row8(norm_w[l], D_MODEL)
        if l == 0:
            nw8 = nw8 + later_token[0:1, 0:1]
        cw8 = _row8(conv_full[l], 3 * HEADS_W)
        aux = _row8(jnp.stack([a_log[l], dt_bias[l]]), LANES)
        lb8 = _row8(lb_all[l], HEADS_W)
        gn8 = _row8(jnp.stack([gnorm_a[l], gnorm_b[l]]), LANES)
        proj, xn = _proj_fwd(h, nw8, wp, l)
        q, k, v, b, g, qb, kb, lf = _prep_fwd(proj, cw8, aux, lb8, nseq, t_len, l)
        oa, ob, sck_a, sck_b = _mixers_fwd(q, k, v, b, g, qb, kb, proj, C_BI // HEADS_W, lf, nseq, t_len, l)
        if l == 0:
            sent, landed = _send_all_wait(later_flight, ob, "gather_later_wait")
            landed = [lax.dynamic_update_slice(ld, own[None], (my,) + (0,) * own.ndim) for ld, own in zip(landed, sent)]
            w_in_slabs.append(landed[0])
            wa_full, wb_full = by_cols(landed[1]), by_cols(landed[2])
            wout_full = landed[3].transpose(1, 0, 2, 3).reshape(depth, D_MODEL, D_MODEL)
        wa_l, wb_l, wout_l = wa_full[l], wb_full[l], wout_full[l]
        h_next = _post_fwd(oa, ob, proj, h, gn8, wa_l, wb_l, wout_l, l)
        saved.append(dict(h=h, wp=wp, nw8=nw8, cw8=cw8, aux=aux, lb8=lb8, gn8=gn8, proj=proj, xn=xn, q=q, k=k, v=v, b=b,
                          wa=wa_l, wb=wb_l, wout=wout_l,
                          g=g, qb=qb, kb=kb, lf=lf, oa=oa, ob=ob, sck_a=sck_a, sck_b=sck_b))
        h = h_next

    dh, acc = _loss_head(h, _row8(final_norm_w, D_MODEL), loss_target, nseq, t_len)

    g_win, g_wa, g_wb, g_wout, g_conv, small = [], [], [], [], [], []

    def mixer_slabs(dwa_s, dwb_s, dwout_s):
        nl = len(dwa_s)
        rows = lambda a: jnp.stack(a).reshape(nl * HEADS_W, N_DEV, LANES).transpose(1, 0, 2)
        wout = jnp.stack(dwout_s).reshape(nl, N_DEV, LANES, D_MODEL).transpose(1, 0, 2, 3)
        return [jnp.concatenate([rows(dwa_s), rows(dwb_s)], axis=1).astype(BF16),
                wout.reshape(N_DEV, nl * LANES, D_MODEL).astype(BF16)]

    def win_slabs(per_layer, dtype):
        return jnp.stack([jnp.concatenate([sl[j] for sl in per_layer], axis=0) for j in range(N_DEV)]).astype(dtype)

    for l in reversed(range(depth)):
        s = saved[l]
        gn8, aux = s["gn8"], s["aux"]
        if l == 0:
            later_flight, later_token = _send_all_start(
                [win_slabs(g_win[::-1], BF16)] + mixer_slabs(g_wa[::-1], g_wb[::-1], g_wout[::-1]), True,
                "scatter_later_start")
            gn8 = gn8 + later_token[0:1, 0:1]
        doa, dob, dz, dbg, dga, dgb, dwa, dwb, dwout, dgn = _post_bwd(
            dh, s["oa"], s["ob"], s["proj"], s["h"], gn8, s["wa"], s["wb"], s["wout"], l)
        dq, dk, dv, db, dg, dqb, dkb, dbi, dlf = _mixers_bwd(
            s["q"], s["k"], s["v"], s["b"], s["g"], s["qb"], s["kb"], s["proj"], C_BI // HEADS_W, s["lf"], s["sck_a"],
            s["sck_b"], doa, dob, nseq, t_len, l)
        if l == 0:
            mixer_flight, mixer_token = _send_all_start(mixer_slabs([dwa], [dwb], [dwout]), True, "scatter_first_start")
            aux = aux + mixer_token[0:1, 0:1]
        dqkv, dmisc, dbq, dbf, dcw, daux, dlb = _prep_bwd(s["proj"], dq, dk, dv, db, dg, dqb, dkb, dlf, s["cw8"], aux,
                                                          s["lb8"], nseq, t_len, l)
        pieces = [dqkv, dz, dbq, dbf, dbi, dbg, dga, dgb, dmisc]
        g_win.append(_from_layout(_proj_bwd_w(s["xn"], pieces, l), N_DEV))
        g_conv.append(dcw[:4])
        nw8 = s["nw8"]
        if l == 0:
            dconv = jnp.stack(g_conv[::-1])
            conv_slabs = dconv.reshape(depth * dconv.shape[1], N_DEV, conv_c).transpose(1, 0, 2)
            win_flight, win_token = _send_all_start([win_slabs(g_win[-1:], BF16), conv_slabs], True, "scatter_win_start")
            nw8 = nw8 + win_token[0:1, 0:1]
        dh, dnw = _proj_bwd_x(pieces, s["wp"], s["h"], nw8, dh, l)
        g_wa.append(dwa)
        g_wb.append(dwb)
        g_wout.append(dwout)
        small.append((dnw[0], dgn[0], dgn[1], daux[0, :N_HEADS], daux[1, :N_HEADS], dlb[0]))
    small.reverse()
    dh = dh.reshape(nseq, t_len, D_MODEL)
    grad_x = dh[:, N_PAD + N_META:]

    packed = jnp.concatenate([small[0][1], small[1][1], small[0][2], small[1][2], small[0][3], small[1][3],
                              small[0][4], small[1][4]])
    tile = jnp.concatenate([
        jnp.sum(dh[:, N_PAD:N_PAD + N_META], axis=0), _row8(jnp.stack([small[0][0], small[1][0], acc[0]]), D_MODEL),
        _row8(jnp.stack([small[0][5], small[1][5]]), D_MODEL), _row8(packed, D_MODEL), _row8(acc[1], D_MODEL)], axis=0)
    tile = _all_reduce_small(tile, "reduce_small")
    loss = jnp.sum(tile[40])
    g_meta = lax.dynamic_slice_in_dim(tile[0:N_META], my * LANES, LANES, axis=1)
    g_norm, g_final = tile[16:18], tile[18]
    (g_lb,) = lb_vjp(tile[24:26, :HEADS_W])
    r21 = tile[32]
    g_gna, g_gnb = r21[0:256].reshape(2, LANES), r21[256:512].reshape(2, LANES)
    g_alog, g_dtb = r21[512:520].reshape(2, N_HEADS), r21[520:528].reshape(2, N_HEADS)

    def landed_sums(flight, tag):
        sent, landed = _send_all_wait(flight, dh, f"{tag}_wait")
        landed = [lax.dynamic_update_slice(ld, lax.dynamic_index_in_dim(src, my, 0, keepdims=True), (my, 0, 0))
                  for ld, src in zip(landed, sent)]
        return [_sum_slabs(ld, f"{tag}_sum{i}") for i, ld in enumerate(landed)]

    l_win, l_ab, l_wout = landed_sums(later_flight, "scatter_later")
    r_ab, r_wout = landed_sums(mixer_flight, "scatter_first")
    r_win, r_conv = landed_sums(win_flight, "scatter_win")
    both = lambda a, b, shape: jnp.concatenate([a.reshape(1, *shape[1:]), b.reshape(depth - 1, *shape[1:])])
    half, half_l = HEADS_W, (depth - 1) * HEADS_W
    mine = [both(r_win, l_win, w_in.shape), both(r_ab[:half], l_ab[:half_l], w_branch_a.shape),
            both(r_ab[half:], l_ab[half_l:], w_branch_b.shape), both(r_wout, l_wout, w_out.shape), r_conv]
    gseg = lambda i, shape: mine[i].reshape(shape)
    grads = {
        "meta_tokens": g_meta, "norm_w": g_norm, "w_in": gseg(0, w_in.shape), "conv_w": gseg(4, conv_w.shape),
        "a_log": g_alog, "dt_bias": g_dtb, "gnorm_a": g_gna, "gnorm_b": g_gnb, "hgrn_lower_bounds": g_lb,
        "w_branch_a": gseg(1, w_branch_a.shape), "w_branch_b": gseg(2, w_branch_b.shape), "w_out": gseg(3, w_out.shape),
        "final_norm_w": g_final}
    weights = {
        "meta_tokens": (meta_tokens, m_meta_tokens, v_meta_tokens), "norm_w": (norm_w, m_norm_w, v_norm_w),
        "w_in": (w_in, m_w_in, v_w_in), "conv_w": (conv_w, m_conv_w, v_conv_w), "a_log": (a_log, m_a_log, v_a_log),
        "dt_bias": (dt_bias, m_dt_bias, v_dt_bias), "gnorm_a": (gnorm_a, m_gnorm_a, v_gnorm_a),
        "gnorm_b": (gnorm_b, m_gnorm_b, v_gnorm_b),
        "hgrn_lower_bounds": (hgrn_lower_bounds, m_hgrn_lower_bounds, v_hgrn_lower_bounds),
        "w_branch_a": (w_branch_a, m_w_branch_a, v_w_branch_a), "w_branch_b": (w_branch_b, m_w_branch_b, v_w_branch_b),
        "w_out": (w_out, m_w_out, v_w_out), "final_norm_w": (final_norm_w, m_final_norm_w, v_final_norm_w)}
    names = list(weights)
    deltas, new_m, new_v = [], [], []
    for nm in names:
        w, m, v = weights[nm]
        view = (1,) * (3 - w.ndim) + w.shape
        d, m2, v2 = _adamw(w.reshape(view), grads[nm].reshape(view), m.reshape(view), v.reshape(view), f"adamw_{nm}")
        deltas.append(d.reshape(w.shape))
        new_m.append(m2.reshape(w.shape))
        new_v.append(v2.reshape(w.shape))
    return (loss, grad_x, *[grads[nm].reshape(weights[nm][0].shape) for nm in names], *deltas, *new_m, *new_v)
```

```python
import functools

import jax
import jax.numpy as jnp
from jax import lax
from jax.experimental import pallas as pl
from jax.experimental.pallas import tpu as pltpu

F32 = jnp.float32
BF16 = jnp.bfloat16

D_MODEL = 1024
N_HEADS = 4
D_HEAD = 128
HEADS_W = N_HEADS * D_HEAD
N_META = 16
N_PAD = 48
GDN_CHUNK = 64
HGRN_CHUNK = 16
EPS = 1e-6
N_DEV = 8
LANES = 128
SUBLANES = 8
VMEM_LIMIT = 56 * 1024 * 1024

C_QKV, C_Z, C_BQ, C_BF, C_BI, C_BG, C_GA, C_GB, C_MISC = 0, 1536, 2048, 2560, 3072, 3584, 4096, 5120, 6144
PROJ_W = 6272
REF_W = 6152

ADAM_LR, ADAM_B1, ADAM_B2, ADAM_EPS, ADAM_WD, ADAM_STEP = 0.001, 0.9, 0.999, 1e-08, 0.01, 10

MESH = pl.DeviceIdType.MESH
SDS = jax.ShapeDtypeStruct
BS = pl.BlockSpec


def _params(n_axes):
    return pltpu.CompilerParams(dimension_semantics=("arbitrary",) * n_axes, vmem_limit_bytes=VMEM_LIMIT)


def _pick(n, cands):
    for c in cands:
        if n % c == 0:
            return c
    raise ValueError(f"no tile for {n} among {cands}")


def _iota2(shape, dim):
    return lax.broadcasted_iota(jnp.int32, shape, dim)


def _dg(a, b, dims):
    return lax.dot_general(a.astype(BF16), b.astype(BF16), (dims, ((), ())), preferred_element_type=F32)


def _bdg(a, b, ca, cb):
    return lax.dot_general(a.astype(BF16), b.astype(BF16), (((ca,), (cb,)), ((0,), (0,))), preferred_element_type=F32)


@jax.custom_vjp
def _bnn(a, b):
    return _bdg(a, b, 2, 1)


@jax.custom_vjp
def _bnt(a, b):
    return _bdg(a, b, 2, 2)


@jax.custom_vjp
def _btn(a, b):
    return _bdg(a, b, 1, 1)


_bnn.defvjp(lambda a, b: (_bnn(a, b), (a, b)), lambda r, g: (_bnt(g, r[1]), _btn(r[0], g)))
_bnt.defvjp(lambda a, b: (_bnt(a, b), (a, b)), lambda r, g: (_bnn(g, r[1]), _btn(g, r[0])))
_btn.defvjp(lambda a, b: (_btn(a, b), (a, b)), lambda r, g: (_bnt(r[1], g), _bnn(r[0], g)))


def _split2(x):
    hi = x.astype(BF16).astype(F32)
    return hi, x - hi


def _tri(bsz, n):
    return jnp.broadcast_to((_iota2((n, n), 0) >= _iota2((n, n), 1)).astype(F32), (bsz, n, n))


@jax.custom_vjp
def _cumsum_rows(x):
    tri = _tri(x.shape[0], x.shape[1])
    hi, lo = _split2(x)
    return _bdg(tri, hi, 2, 1) + _bdg(tri, lo, 2, 1)


def _cumsum_rows_bwd(_, g):
    tri = _tri(g.shape[0], g.shape[1])
    hi, lo = _split2(g)
    return (_bdg(tri, hi, 1, 1) + _bdg(tri, lo, 1, 1),)


_cumsum_rows.defvjp(lambda x: (_cumsum_rows(x), None), _cumsum_rows_bwd)


def _sigmoid(x):
    return jax.nn.sigmoid(x)


def _silu(x):
    return x * _sigmoid(x)


def _softplus(x):
    return jnp.maximum(x, 0.0) + jnp.log1p(jnp.exp(-jnp.abs(x)))


def _rms(x, w):
    return x * lax.rsqrt(jnp.mean(x * x, axis=-1, keepdims=True) + EPS) * w


@jax.custom_vjp
def _inv_unit_lower(lm):
    n = lm.shape[1]
    a = (_iota2((n, n), 0) == _iota2((n, n), 1)).astype(F32)[None] - lm
    steps = max(1, (n - 1).bit_length()) - 1
    p = _bnn(lm, lm)
    for i in range(steps):
        if i == steps - 1:
            a = a + _bnn(a, p)
        else:
            both = _bnn(jnp.concatenate([a, p], axis=1), p)
            a, p = a + both[:, :n], both[:, n:]
    return a


_inv_unit_lower.defvjp(lambda lm: (lambda a: (a, a))(_inv_unit_lower(lm)),
                       lambda a, g: (-_bnt(_btn(a, g), a),))


def _gdn_chunk(q, k, v, b_b, g_b, s):
    n, dv = q.shape[1], v.shape[2]
    r, c = _iota2((n, n), 0), _iota2((n, n), 1)
    causal, strict, eye = (r >= c)[None], (r > c)[None], (r == c)[None]
    g_cum = _cumsum_rows(g_b)
    g_i = g_cum[:, :, :n]
    g_j = jnp.sum(jnp.where(eye, g_i, 0.0), axis=1, keepdims=True)
    decay = jnp.where(causal, jnp.exp(jnp.where(causal, g_i - g_j, 0.0)), 0.0)
    e_g = jnp.exp(g_cum)
    kb = k * b_b
    kk = _bnt(jnp.concatenate([kb, q], axis=1), k)
    a_inv = _inv_unit_lower(jnp.where(strict, kk[:, :n] * decay, 0.0))
    uw = _bnn(a_inv, jnp.concatenate([v * b_b, kb * e_g], axis=2))
    ws = _bnn(jnp.concatenate([uw[:, :, dv:], q * e_g], axis=1), s)
    v_new = uw[:, :, :dv] - ws[:, :n]
    o = ws[:, n:] + _bnn(kk[:, n:] * decay, v_new)
    g_last = g_cum[:, n - 1:n, :]
    s_new = s * jnp.exp(g_last) +_btn(k * jnp.exp(g_last - g_cum), v_new)
    return o, s_new


@functools.partial(jax.custom_vjp, nondiff_argnums=(1, 2))
def _row(x, j, n):
    return x[:, j:j + 1, :]


def _row_bwd(j, n, _, g):
    return (jnp.where(_iota2((1, n, 1), 1) == j, g, 0.0),)


_row.defvjp(lambda x, j, n: (_row(x, j, n), None), _row_bwd)


def _hgrn_pairs(q, k, v, b_cum):
    n = q.shape[1]
    half = n // 2 if n > SUBLANES else n
    parts = []
    for lo in range(0, n, half):
        qs, bs = q[:, lo:], b_cum[:, lo:]
        rows = _iota2((1, n - lo, 1), 1) + lo
        acc = jnp.zeros_like(qs)
        for j in range(lo, lo + half):
            p = jnp.exp(jnp.where(rows >= j, bs - _row(b_cum, j, n), -1e30))
            acc = acc + jnp.sum(qs * _row(k, j, n) * p, axis=2, keepdims=True) * _row(v, j, n)
        parts.append(acc)
    if len(parts) == 1:
        return parts[0]
    return parts[0] + jnp.concatenate([jnp.zeros_like(parts[1]), parts[1]], axis=1)


def _hgrn_block(q, k, v, lf, st, group=HGRN_CHUNK):
    n, rows = group, q.shape[1]
    b_cum = _cumsum_rows(lf)
    outs = []
    for c in range(rows // n):
        rs = slice(c * n, (c + 1) * n)
        o = _hgrn_pairs(q[:, rs], k[:, rs], v[:, rs], b_cum[:, rs])
        if c:
            b_c = _row(b_cum, c * n - 1, rows)
            scores = _bnt(q[:, rs] * jnp.exp(b_cum[:, rs] - b_c), k[:, :c * n] * jnp.exp(b_c - b_cum[:, :c * n]))
            o = o + _bnn(scores, v[:, :c * n])
        outs.append(o)
    b_last = _row(b_cum, rows - 1, rows)
    o = _bnt(q * jnp.exp(b_cum), st) + jnp.concatenate(outs, axis=1)
    return o, st * jnp.exp(b_last) + _btn(v, k * jnp.exp(b_last - b_cum))


def _l2n_act(y, scale):
    a = _silu(y)
    return a * lax.rsqrt(jnp.sum(a * a, axis=-1, keepdims=True) + EPS) * scale


def _col(x, lane):
    return jnp.sum(jnp.where(_iota2(x.shape, 1) == lane, x, 0.0), axis=1, keepdims=True)


def _elem(x, row, lane):
    m = (_iota2(x.shape, 0) == row) & (_iota2(x.shape, 1) == lane)
    return jnp.sum(jnp.sum(jnp.where(m, x, 0.0), axis=1, keepdims=True), axis=0, keepdims=True)


def _gdn_gates(misc, aux, real, head):
    beta = _sigmoid(_col(misc, head))
    g = -jnp.exp(_elem(aux, 0, head)) * _softplus(_col(misc, N_HEADS + head) + _elem(aux, 1, head))
    g = jnp.where(real, g, 0.0)
    shape = (misc.shape[0], D_HEAD)
    return jnp.broadcast_to(beta, shape), jnp.broadcast_to(g, shape)


def _hgrn_prep(bq, bf, lb, real):
    qb = _silu(bq) * (D_HEAD ** -0.5)
    log_sig = jnp.minimum(bf, 0.0) - jnp.log1p(jnp.exp(-jnp.abs(bf)))
    pos = lb > 0.0
    lbs = jnp.where(pos, lb, 0.5)
    a = jnp.log(lbs)
    b = jnp.log1p(-lbs) + log_sig
    lae = jnp.maximum(a, b) + jnp.log1p(jnp.exp(-jnp.abs(a - b)))
    lf = jnp.where(pos, lae, log_sig)
    kb = jnp.where(pos, 1.0 - lbs, 1.0) * _sigmoid(-bf)
    return qb, jnp.where(real, kb, 0.0), jnp.where(real, lf, 0.0)


def _gated_norm(o, z, gw):
    return o * lax.rsqrt(jnp.mean(o * o, axis=-1, keepdims=True) + EPS) * gw * _silu(z)


def _shift_down(x, j):
    return x if j == 0 else pltpu.roll(x, j, 0)


def _shift_up(x, j):
    return x if j == 0 else pltpu.roll(x, x.shape[0] - j, 0)


def _all_gather_hbm(blocks, name):
    na = len(blocks)

    def body(*refs):
        x_refs, out_refs = refs[:na], refs[na:2 * na]
        send_sems, recv_sems, local_sems = refs[2 * na:]
        mx, my, mc = lax.axis_index("x"), lax.axis_index("y"), lax.axis_index("c")
        me, sibling = (mx, my, mc), (mx, my, 1 - mc)
        chips = [(1 - mx, my), (mx, 1 - my), (1 - mx, 1 - my)]

        def slab(a, px, py, pc):
            return out_refs[a].at[4 * px + 2 * py + pc]

        def copy(a, k, blk, to, own=False):
            return pltpu.make_async_remote_copy(
                src_ref=x_refs[a] if own else slab(a, *blk), dst_ref=slab(a, *blk),
                send_sem=send_sems.at[7 * a + k], recv_sem=recv_sems.at[7 * a + k], device_id=to, device_id_type=MESH)

        mine = [pltpu.make_async_copy(x_refs[a], slab(a, *me), local_sems.at[a]) for a in range(na)]
        for cp in mine:
            cp.start()
        first = [copy(a, 0, me, sibling, own=True) for a in range(na)]
        first += [copy(a, 1 + j, me, (*chip, mc), own=True) for j, chip in enumerate(chips) for a in range(na)]
        for cp in first:
            cp.start()
        passed = []
        for j, chip in enumerate(chips):
            for a in range(na):
                copy(a, 1 + j, (*chip, mc), me).wait_recv()
                passed.append(copy(a, 4 + j, (*chip, mc), sibling))
                passed[-1].start()
        for a in range(na):
            copy(a, 0, sibling, me).wait_recv()
            for j, chip in enumerate(chips):
                copy(a, 4 + j, (*chip, 1 - mc), me).wait_recv()
        for cp in first + passed:
            cp.wait_send()
        for cp in mine:
            cp.wait()

    hbm = BS(memory_space=pl.ANY)
    return pl.pallas_call(
        body, name=name, out_shape=[SDS((N_DEV, *b.shape), b.dtype) for b in blocks],
        in_specs=[hbm] * na, out_specs=[hbm] * na,
        scratch_shapes=[pltpu.SemaphoreType.DMA((7 * na,)), pltpu.SemaphoreType.DMA((7 * na,)),
                        pltpu.SemaphoreType.DMA((na,))],
    )(*blocks)


def _all_reduce_small(block, name):
    r, c = block.shape

    def body(x_ref, out_ref, buf, send_sems, recv_sems):
        mx, my, mc = lax.axis_index("x"), lax.axis_index("y"), lax.axis_index("c")
        me, sibling = (mx, my, mc), (mx, my, 1 - mc)
        chips = [(1 - mx, my), (mx, 1 - my), (1 - mx, 1 - my)]

        def slab(px, py, pc):
            return buf.at[4 * px + 2 * py + pc]

        def copy(k, blk, to, src=None):
            return pltpu.make_async_remote_copy(
                src_ref=slab(*blk) if src is None else src, dst_ref=slab(*blk),
                send_sem=send_sems.at[k], recv_sem=recv_sems.at[k], device_id=to, device_id_type=MESH)

        first = [copy(0, me, sibling, src=x_ref)]
        first += [copy(1 + j, me, (*chip, mc), src=x_ref) for j, chip in enumerate(chips)]
        for cp in first:
            cp.start()
        passed = [copy(4 + j, (*chip, mc), sibling) for j, chip in enumerate(chips)]
        for j, chip in enumerate(chips):
            copy(1 + j, (*chip, mc), me).wait_recv()
            passed[j].start()
        copy(0, sibling, me).wait_recv()
        for j, chip in enumerate(chips):
            copy(4 + j, (*chip, 1 - mc), me).wait_recv()
        for cp in first + passed:
            cp.wait_send()
        buf[4 * mx + 2 * my + mc] = x_ref[...]
        acc = buf[0]
        for d in range(1, N_DEV):
            acc = acc + buf[d]
        out_ref[...] = acc

    return pl.pallas_call(
        body, name=name, out_shape=SDS((r, c), F32),
        in_specs=[BS(memory_space=pltpu.VMEM)], out_specs=BS(memory_space=pltpu.VMEM),
        scratch_shapes=[pltpu.VMEM((N_DEV, r, c), F32), pltpu.SemaphoreType.DMA((7,)), pltpu.SemaphoreType.DMA((7,))],
    )(block)


HBM_SPEC = BS(memory_space=pltpu.HBM)
SEM_SPEC = BS(memory_space=pltpu.SEMAPHORE)
SIDE_EFFECT = pltpu.SideEffectType.DATAFLOW_SIDE_EFFECTING


def _peer(rel):
    flip = lambda v, bit: 1 - v if bit else v
    return (flip(lax.axis_index("x"), rel >> 2 & 1), flip(lax.axis_index("y"), rel >> 1 & 1),
            flip(lax.axis_index("c"), rel & 1))


def _send_all_start(blocks, scatter, name, after=None):
    na = len(blocks)
    shapes = [b.shape[1:] if scatter else b.shape for b in blocks]
    n_in = 2 * na + (after is not None)

    def body(*refs):
        srcs, lands = refs[:na], refs[na:2 * na]
        send_sems, recv_sems, token = refs[n_in], refs[n_in + 1], refs[-1]
        me = 4 * lax.axis_index("x") + 2 * lax.axis_index("y") + lax.axis_index("c")
        for a in range(na):
            for rel in range(1, N_DEV):
                px, py, pc = _peer(rel)
                pltpu.make_async_remote_copy(
                    src_ref=srcs[a].at[4 * px + 2 * py + pc] if scatter else srcs[a], dst_ref=lands[a].at[me],
                    send_sem=send_sems.at[7 * a + rel - 1], recv_sem=recv_sems.at[7 * a + rel - 1],
                    device_id=(px, py, pc), device_id_type=MESH).start()
        token[...] = jnp.zeros_like(token)

    lands = [lax.empty((N_DEV, *s), b.dtype) for s, b in zip(shapes, blocks)]
    res = pl.pallas_call(
        body, name=name,
        out_shape=([pltpu.SemaphoreType.DMA((7 * na,)), pltpu.SemaphoreType.DMA((7 * na,))]
                   + [pltpu.HBM(b.shape, b.dtype) for b in blocks] + [pltpu.HBM(ld.shape, ld.dtype) for ld in lands]
                   + [SDS((SUBLANES, LANES), F32)]),
        in_specs=[HBM_SPEC] * (2 * na) + [BS(memory_space=pl.ANY)] * (after is not None),
        out_specs=[SEM_SPEC, SEM_SPEC] + [HBM_SPEC] * (2 * na) + [BS(memory_space=pltpu.VMEM)],
        input_output_aliases={i: 2 + i for i in range(2 * na)},
        compiler_params=pltpu.CompilerParams(has_side_effects=SIDE_EFFECT),
    )(*[pltpu.with_memory_space_constraint(b, pltpu.HBM) for b in blocks],
      *[pltpu.with_memory_space_constraint(ld, pltpu.HBM) for ld in lands], *([] if after is None else [after]))
    return dict(send=res[0], recv=res[1], srcs=res[2:2 + na], lands=res[2 + na:2 + 2 * na], scatter=scatter), res[-1]


def _send_all_wait(flight, after, name):
    na = len(flight["srcs"])

    def body(*refs):
        srcs, lands = refs[:na], refs[na:2 * na]
        send_sems, recv_sems = refs[2 * na], refs[2 * na + 1]
        for a in range(na):
            for rel in range(1, N_DEV):
                cp = pltpu.make_async_remote_copy(
                    src_ref=srcs[a].at[0] if flight["scatter"] else srcs[a], dst_ref=lands[a].at[0],
                    send_sem=send_sems.at[7 * a + rel - 1], recv_sem=recv_sems.at[7 * a + rel - 1],
                    device_id=_peer(rel), device_id_type=MESH)
                cp.wait_send()
                cp.wait_recv()

    arrays = list(flight["srcs"]) + list(flight["lands"])
    res = pl.pallas_call(
        body, name=name, out_shape=[pltpu.HBM(a.shape, a.dtype) for a in arrays],
        in_specs=[HBM_SPEC] * (2 * na) + [SEM_SPEC, SEM_SPEC, BS(memory_space=pl.ANY)], out_specs=[HBM_SPEC] * (2 * na),
        input_output_aliases={i: i for i in range(2 * na)},
        compiler_params=pltpu.CompilerParams(has_side_effects=SIDE_EFFECT),
    )(*arrays, flight["send"], flight["recv"], after)
    return res[:na], res[na:]


def _sum_slabs(land, name):
    _, r, c = land.shape
    tr = _pick(r, (256, 128, 64, 32, 16, 8))

    def body(l_ref, o_ref):
        acc = l_ref[0].astype(F32)
        for d in range(1, N_DEV):
            acc = acc + l_ref[d].astype(F32)
        o_ref[...] = acc

    return pl.pallas_call(
        body, name=name, grid=(r // tr,), out_shape=SDS((r, c), F32),
        in_specs=[BS((N_DEV, tr, c), lambda j: (0, j, 0))], out_specs=BS((tr, c), lambda j: (j, 0)),
        compiler_params=_params(1),
    )(land)


def _proj_fwd(h, nw8, wp, tag):
    n = h.shape[0]
    tm = _pick(n, (2112, 1408, 768, 512, 384, 256, 192, 128, 64))
    tn = 896

    def body(h_ref, nw_ref, w_ref, proj_ref, xn_ref):
        @pl.when(pl.program_id(1) == 0)
        def _():
            xn_ref[...] = _rms(h_ref[...], nw_ref[0:1, :]).astype(BF16)

        proj_ref[...] = jnp.dot(xn_ref[...], w_ref[...], preferred_element_type=F32)

    return pl.pallas_call(
        body, name=f"proj_fwd_{tag}", grid=(n // tm, PROJ_W // tn),
        in_specs=[BS((tm, D_MODEL), lambda i, j: (i, 0)), BS((SUBLANES, D_MODEL), lambda i, j: (0, 0)),
                  BS((D_MODEL, tn), lambda i, j: (0, j))],
        out_specs=[BS((tm, tn), lambda i, j: (i, j)), BS((tm, D_MODEL), lambda i, j: (i, 0))],
        out_shape=[SDS((n, PROJ_W), F32), SDS((n, D_MODEL), BF16)], compiler_params=_params(2),
    )(h, nw8, wp)


def _conv_ext(x_ext, cw_ref):
    y = x_ext * cw_ref[3:4, :]
    for k in range(3):
        y = y + _shift_down(x_ext, 3 - k) * cw_ref[k:k + 1, :]
    return y[SUBLANES:]


def _prep_fwd(proj, cw8, aux, lb8, nseq, t_len, tag):
    n = proj.shape[0]
    tt = _pick(t_len, (192, 128, 64))
    nt_ = t_len // tt
    qkv_w = 3 * HEADS_W

    def body(cur_ref, prev_ref, misc_ref, bq_ref, bf_ref, cw_ref, aux_ref, lb_ref,
             q_ref, k_ref, v_ref, b_ref, g_ref, qb_ref, kb_ref, lf_ref, ext_ref):
        t = pl.program_id(1)
        ext_ref[0:SUBLANES, :] = jnp.where(t == 0, 0.0, prev_ref[...])
        ext_ref[SUBLANES:, :] = cur_ref[...]
        y = ext_ref[SUBLANES:, :] * cw_ref[3:4, :]
        for kk in range(3):
            y = y + ext_ref[SUBLANES - 3 + kk:SUBLANES - 3 + kk + tt, :] * cw_ref[kk:kk + 1, :]
        real = (t * tt + _iota2((tt, 1), 0)) >= N_PAD
        misc = misc_ref[...]
        auxv = aux_ref[...]
        for hd in range(N_HEADS):
            sl = slice(hd * D_HEAD, (hd + 1) * D_HEAD)
            q_ref[:, sl] = _l2n_act(y[:, sl], D_HEAD ** -0.5)
            k_ref[:, sl] = _l2n_act(y[:, HEADS_W + hd * D_HEAD:HEADS_W + (hd + 1) * D_HEAD], 1.0)
            v_ref[:, sl] = _silu(y[:, 2 * HEADS_W + hd * D_HEAD:2 * HEADS_W + (hd + 1) * D_HEAD])
            b_ref[:, sl], g_ref[:, sl] = _gdn_gates(misc, auxv, real, hd)
        qb_ref[...], kb_ref[...], lf_ref[...] = _hgrn_prep(bq_ref[...], bf_ref[...], lb_ref[0:1, :], real)

    rb = tt // SUBLANES
    row = lambda s, t: s * nt_ + t
    wide = BS((tt, HEADS_W), lambda s, t: (row(s, t), 0))
    return pl.pallas_call(
        body, name=f"prep_fwd_{tag}", grid=(nseq, nt_),
        in_specs=[BS((tt, qkv_w), lambda s, t: (row(s, t), 0)),
                  BS((SUBLANES, qkv_w), lambda s, t: (jnp.maximum(row(s, t) * rb - 1, 0), 0)),
                  BS((tt, LANES), lambda s, t: (row(s, t), C_MISC // LANES)),
                  BS((tt, HEADS_W), lambda s, t: (row(s, t), C_BQ // HEADS_W)),
                  BS((tt, HEADS_W), lambda s, t: (row(s, t), C_BF // HEADS_W)),
                  BS((SUBLANES, qkv_w), lambda s, t: (0, 0)), BS((SUBLANES, LANES), lambda s, t: (0, 0)),
                  BS((SUBLANES, HEADS_W), lambda s, t: (0, 0))],
        out_specs=[wide] * 8, out_shape=[SDS((n, HEADS_W), F32)] * 8,
        scratch_shapes=[pltpu.VMEM((tt + SUBLANES, qkv_w), F32)], compiler_params=_params(2),
    )(proj, proj, proj, proj, proj, cw8, aux, lb8)


GDN_SEQS = 4
HGRN_SEQS = 2


def _seq_block(nseq, most):
    return max(s for s in (1, 2, 4) if s <= most and nseq % s == 0)


def _to_chains(x):
    return jnp.concatenate([x[:, :, hd * D_HEAD:(hd + 1) * D_HEAD] for hd in range(N_HEADS)], axis=0)


def _from_chains(ref, rows, val):
    sb = val.shape[0] // N_HEADS
    for hd in range(N_HEADS):
        ref[:, rows, hd * D_HEAD:(hd + 1) * D_HEAD] = val[hd * sb:(hd + 1) * sb].astype(ref.dtype)


def _mixers_fwd(q, k, v, b, g, qb, kb, vb, vb_col, lf, nseq, t_len, tag):
    sb, hs = _seq_block(nseq, GDN_SEQS), _seq_block(nseq, HGRN_SEQS)
    nc = t_len // GDN_CHUNK
    chains = N_HEADS * sb

    def body(q_ref, k_ref, v_ref, b_ref, g_ref, qb_ref, kb_ref, vb_ref, lf_ref, oa_ref, ob_ref, cka_ref, ckb_ref,
             sa_ref, sb_ref):
        @pl.when(pl.program_id(1) == 0)
        def _():
            sa_ref[...] = jnp.zeros_like(sa_ref)
            sb_ref[...] = jnp.zeros_like(sb_ref)

        s = sa_ref[...]
        cka_ref[...] = s
        o, s_new = _gdn_chunk(*[_to_chains(r[...]) for r in (q_ref, k_ref, v_ref, b_ref, g_ref)], s)
        _from_chains(oa_ref, slice(None), o)
        sa_ref[...] = s_new
        for part in range(sb // hs):
            seqs, ch = slice(part * hs, (part + 1) * hs), slice(part * N_HEADS * hs, (part + 1) * N_HEADS * hs)
            s = sb_ref[ch]
            ckb_ref[ch] = s
            o, s_new = _hgrn_block(*[_to_chains(r[seqs]) for r in (qb_ref, kb_ref, vb_ref, lf_ref)], s)
            for hd in range(N_HEADS):
                ob_ref[seqs, :, hd * D_HEAD:(hd + 1) * D_HEAD] = o[hd * hs:(hd + 1) * hs]
            sb_ref[ch] = s_new

    blk = lambda cb: BS((sb, GDN_CHUNK, HEADS_W), lambda p, c: (p, c, cb))
    ck_spec = BS((None, None, chains, D_HEAD, D_HEAD), lambda p, c: (p, c, 0, 0, 0))
    ck_shape = SDS((nseq // sb, nc, chains, D_HEAD, D_HEAD), F32)
    view = lambda a: a.reshape(nseq, t_len, a.shape[1])
    oa, ob, cka, ckb = pl.pallas_call(
        body, name=f"mixers_fwd_{tag}", grid=(nseq // sb, nc),
        in_specs=[blk(0)] * 7 + [blk(vb_col), blk(0)], out_specs=[blk(0), blk(0), ck_spec, ck_spec],
        out_shape=[SDS((nseq, t_len, HEADS_W), F32)] * 2 + [ck_shape] * 2,
        scratch_shapes=[pltpu.VMEM((chains, D_HEAD, D_HEAD), F32)] * 2, compiler_params=_params(2),
    )(*[view(a) for a in (q, k, v, b, g, qb, kb, vb, lf)])
    return oa.reshape(-1, HEADS_W), ob.reshape(-1, HEADS_W), cka, ckb


def _mixers_bwd(q, k, v, b, g, qb, kb, vb, vb_col, lf, cka, ckb, doa, dob, nseq, t_len, tag):
    sb, hs = _seq_block(nseq, GDN_SEQS), _seq_block(nseq, HGRN_SEQS)
    nc = t_len // GDN_CHUNK
    chains = N_HEADS * sb

    def body(q_ref, k_ref, v_ref, b_ref, g_ref, qb_ref, kb_ref, vb_ref, lf_ref, doa_ref, dob_ref, cka_ref, ckb_ref,
             dq_ref, dk_ref, dv_ref, db_ref, dg_ref, dqb_ref, dkb_ref, dvb_ref, dlf_ref, dsa_ref, dsb_ref):
        @pl.when(pl.program_id(1) == 0)
        def _():
            dsa_ref[...] = jnp.zeros_like(dsa_ref)
            dsb_ref[...] = jnp.zeros_like(dsb_ref)

        _, vjp = jax.vjp(_gdn_chunk, *[_to_chains(r[...]) for r in (q_ref, k_ref, v_ref, b_ref, g_ref)], cka_ref[...])
        grads = vjp((_to_chains(doa_ref[...]), dsa_ref[...]))
        for ref, val in zip((dq_ref, dk_ref, dv_ref, db_ref, dg_ref), grads[:5]):
            _from_chains(ref, slice(None), val)
        dsa_ref[...] = grads[5]
        for part in range(sb // hs):
            seqs, ch = slice(part * hs, (part + 1) * hs), slice(part * N_HEADS * hs, (part + 1) * N_HEADS * hs)
            _, vjp = jax.vjp(functools.partial(_hgrn_block, group=SUBLANES),
                             *[_to_chains(r[seqs]) for r in (qb_ref, kb_ref, vb_ref, lf_ref)], ckb_ref[ch])
            grads = vjp((_to_chains(dob_ref[seqs]), dsb_ref[ch]))
            for ref, val in zip((dqb_ref, dkb_ref, dvb_ref, dlf_ref), grads[:4]):
                for hd in range(N_HEADS):
                    ref[seqs, :, hd * D_HEAD:(hd + 1) * D_HEAD] = val[hd * hs:(hd + 1) * hs].astype(ref.dtype)
            dsb_ref[ch] = grads[4]

    blk = lambda cb: BS((sb, GDN_CHUNK, HEADS_W), lambda p, c: (p, nc - 1 - c, cb))
    ck_spec = BS((None, None, chains, D_HEAD, D_HEAD), lambda p, c: (p, nc - 1 - c, 0, 0, 0))
    view = lambda a: a.reshape(nseq, t_len, a.shape[1])
    dts = [F32] * 7 + [BF16, F32]
    res = pl.pallas_call(
        body, name=f"mixers_bwd_{tag}", grid=(nseq // sb, nc),
        in_specs=[blk(0)] * 7 + [blk(vb_col), blk(0), blk(0), blk(0), ck_spec, ck_spec], out_specs=[blk(0)] * 9,
        out_shape=[SDS((nseq, t_len, HEADS_W), dt) for dt in dts],
        scratch_shapes=[pltpu.VMEM((chains, D_HEAD, D_HEAD), F32)] * 2, compiler_params=_params(2),
    )(*[view(a) for a in (q, k, v, b, g, qb, kb, vb, lf, doa, dob)], cka, ckb)
    return [r.reshape(-1, HEADS_W) for r in res]


def _post_values(oa_ref, ob_ref, z_ref, bg_ref, ga_ref, gb_ref, gn_ref, wa_ref, wb_ref, ya_ref, yb_ref):
    for hd in range(N_HEADS):
        sl = slice(hd * D_HEAD, (hd + 1) * D_HEAD)
        ya_ref[:, sl] = _gated_norm(oa_ref[:, sl], z_ref[:, sl], gn_ref[0:1, :]).astype(BF16)
        yb_ref[:, sl] = _gated_norm(ob_ref[:, sl], bg_ref[:, sl], gn_ref[1:2, :]).astype(BF16)
    pa = jnp.dot(ya_ref[...], wa_ref[...], preferred_element_type=F32)
    pb = jnp.dot(yb_ref[...], wb_ref[...], preferred_element_type=F32)
    return pa, pb, _sigmoid(ga_ref[...]), _sigmoid(gb_ref[...])


def _post_specs(tm):
    r2 = lambda i: (i, 0)
    return [BS((tm, HEADS_W), r2), BS((tm, HEADS_W), r2),
            BS((tm, HEADS_W), lambda i: (i, C_Z // HEADS_W)), BS((tm, HEADS_W), lambda i: (i, C_BG // HEADS_W)),
            BS((tm, D_MODEL), lambda i: (i, C_GA // D_MODEL)), BS((tm, D_MODEL), lambda i: (i, C_GB // D_MODEL)),
            BS((tm, D_MODEL), r2), BS((SUBLANES, LANES), lambda i: (0, 0))]


def _post_fwd(oa, ob, proj, h, gn8, wa, wb, wout, tag):
    n = h.shape[0]
    tm = _pick(n, (768, 384, 256, 192, 128, 64))

    def body(oa_ref, ob_ref, z_ref, bg_ref, ga_ref, gb_ref, h_ref, gn_ref, wa_ref, wb_ref, wout_ref, out_ref,
             ya_ref, yb_ref):
        pa, pb, sa, sb = _post_values(oa_ref, ob_ref, z_ref, bg_ref, ga_ref, gb_ref, gn_ref, wa_ref, wb_ref,
                                      ya_ref, yb_ref)
        mixed = (sa * pa + sb * pb).astype(BF16)
        out_ref[...] = h_ref[...] + jnp.dot(mixed, wout_ref[...], preferred_element_type=F32)

    full = lambda i: (0, 0)
    return pl.pallas_call(
        body, name=f"post_fwd_{tag}", grid=(n // tm,),
        in_specs=_post_specs(tm) + [BS((HEADS_W, D_MODEL), full), BS((HEADS_W, D_MODEL), full),
                                    BS((D_MODEL, D_MODEL), full)],
        out_specs=BS((tm, D_MODEL), lambda i: (i, 0)), out_shape=SDS((n, D_MODEL), F32),
        scratch_shapes=[pltpu.VMEM((tm, HEADS_W), BF16), pltpu.VMEM((tm, HEADS_W), BF16)], compiler_params=_params(1),
    )(oa, ob, proj, proj, proj, proj, h, gn8, wa, wb, wout)


def _post_bwd(dh, oa, ob, proj, h, gn8, wa, wb, wout, tag):
    n = h.shape[0]
    tm = _pick(n, (384, 256, 192, 128, 64))

    def body(dh_ref, oa_ref, ob_ref, z_ref, bg_ref, ga_ref, gb_ref, h_ref, gn_ref, wa_ref, wb_ref, wout_ref,
             doa_ref, dob_ref, dz_ref, dbg_ref, dga_ref, dgb_ref, dwa_hbm, dwb_hbm, dwout_hbm, dgn_ref,
             ya_ref, yb_ref, dwa_ref, dwb_ref, dwout_ref):
        @pl.when(pl.program_id(0) == 0)
        def _():
            dwa_ref[...] = jnp.zeros_like(dwa_ref)
            dwb_ref[...] = jnp.zeros_like(dwb_ref)
            dwout_ref[...] = jnp.zeros_like(dwout_ref)
            dgn_ref[...] = jnp.zeros_like(dgn_ref)

        pa, pb, sa, sb = _post_values(oa_ref, ob_ref, z_ref, bg_ref, ga_ref, gb_ref, gn_ref, wa_ref, wb_ref,
                                      ya_ref, yb_ref)
        mixed = (sa * pa + sb * pb).astype(BF16)
        dout = dh_ref[...].astype(BF16)
        dwout_ref[...] += _dg(mixed, dout, ((0,), (0,)))
        dmixed = _dg(dout, wout_ref[...], ((1,), (1,)))
        dga_ref[...] = (dmixed * pa * sa * (1.0 - sa)).astype(BF16)
        dgb_ref[...] = (dmixed * pb * sb * (1.0 - sb)).astype(BF16)
        dpa = (dmixed * sa).astype(BF16)
        dpb = (dmixed * sb).astype(BF16)
        dwa_ref[...] += _dg(ya_ref[...], dpa, ((0,), (0,)))
        dwb_ref[...] += _dg(yb_ref[...], dpb, ((0,), (0,)))
        dya = _dg(dpa, wa_ref[...], ((1,), (1,)))
        dyb = _dg(dpb, wb_ref[...], ((1,), (1,)))
        dgn_a = jnp.zeros((1, D_HEAD), F32)
        dgn_b = jnp.zeros((1, D_HEAD), F32)
        for hd in range(N_HEADS):
            sl = slice(hd * D_HEAD, (hd + 1) * D_HEAD)
            _, vjp = jax.vjp(_gated_norm, oa_ref[:, sl], z_ref[:, sl], gn_ref[0:1, :])
            doa, dz, dgw = vjp(dya[:, sl])
            doa_ref[:, sl], dz_ref[:, sl], dgn_a = doa, dz.astype(BF16), dgn_a + dgw
            _, vjp = jax.vjp(_gated_norm, ob_ref[:, sl], bg_ref[:, sl], gn_ref[1:2, :])
            dob, dbg, dgw = vjp(dyb[:, sl])
            dob_ref[:, sl], dbg_ref[:, sl], dgn_b = dob, dbg.astype(BF16), dgn_b + dgw
        dgn_ref[0:1, :] += dgn_a
        dgn_ref[1:2, :] += dgn_b

        @pl.when(pl.program_id(0) == pl.num_programs(0) - 1)
        def _():
            pltpu.sync_copy(dwa_ref, dwa_hbm)
            pltpu.sync_copy(dwb_ref, dwb_hbm)
            pltpu.sync_copy(dwout_ref, dwout_hbm)

    full = lambda i: (0, 0)
    r2 = lambda i: (i, 0)
    return pl.pallas_call(
        body, name=f"post_bwd_{tag}", grid=(n // tm,),
        in_specs=[BS((tm, D_MODEL), r2)] + _post_specs(tm) + [
            BS((HEADS_W, D_MODEL), full), BS((HEADS_W, D_MODEL), full), BS((D_MODEL, D_MODEL), full)],
        out_specs=[BS((tm, HEADS_W), r2)] * 4 + [BS((tm, D_MODEL), r2)] * 2 + [BS(memory_space=pl.ANY)] * 3 + [
            BS((SUBLANES, LANES), full)],
        out_shape=[SDS((n, HEADS_W), F32), SDS((n, HEADS_W), F32), SDS((n, HEADS_W), BF16), SDS((n, HEADS_W), BF16),
                   SDS((n, D_MODEL), BF16), SDS((n, D_MODEL), BF16), SDS((HEADS_W, D_MODEL), F32),
                   SDS((HEADS_W, D_MODEL), F32), SDS((D_MODEL, D_MODEL), F32), SDS((SUBLANES, LANES), F32)],
        scratch_shapes=[pltpu.VMEM((tm, HEADS_W), BF16), pltpu.VMEM((tm, HEADS_W), BF16),
                        pltpu.VMEM((HEADS_W, D_MODEL), F32), pltpu.VMEM((HEADS_W, D_MODEL), F32),
                        pltpu.VMEM((D_MODEL, D_MODEL), F32)], compiler_params=_params(1),
    )(dh, oa, ob, proj, proj, proj, proj, h, gn8, wa, wb, wout)


def _loss_head(h, fw8, target, nseq, t_len):
    n = h.shape[0]
    nc = t_len // GDN_CHUNK
    sub = _pick(nc, (11, 3, 1))
    tl, nt = sub * GDN_CHUNK, nc // sub
    inv_d = 1.0 / D_MODEL

    def body(h_ref, fw_ref, *rest):
        tgt_refs, (dh_ref, acc_ref) = rest[:sub], rest[sub:]

        @pl.when((pl.program_id(0) == 0) & (pl.program_id(1) == 0))
        def _():
            acc_ref[...] = jnp.zeros_like(acc_ref)

        frames = ((pl.program_id(1) * tl + _iota2((tl, 1), 0)) >= N_PAD + N_META).astype(F32)
        y, vjp = jax.vjp(_rms, h_ref[...], fw_ref[0:1, :])
        err = (y - jnp.concatenate([r[...] for r in tgt_refs], axis=0)) * frames
        dx, dfw = vjp(err * inv_d)
        dh_ref[...] = dx
        acc_ref[0:1, :] += dfw
        acc_ref[1:2, :] += (0.5 * inv_d) * jnp.sum(err * err, axis=0, keepdims=True)

    tgt_spec = lambda u: BS((None, GDN_CHUNK, D_MODEL), lambda s, t: (s, jnp.maximum(t * sub + u - 1, 0), 0))
    return pl.pallas_call(
        body, name="loss_head", grid=(nseq, nt),
        in_specs=[BS((tl, D_MODEL), lambda s, t: (s * nt + t, 0)), BS((SUBLANES, D_MODEL), lambda s, t: (0, 0))]
        + [tgt_spec(u) for u in range(sub)],
        out_specs=[BS((tl, D_MODEL), lambda s, t: (s * nt + t, 0)), BS((SUBLANES, D_MODEL), lambda s, t: (0, 0))],
        out_shape=[SDS((n, D_MODEL), F32), SDS((SUBLANES, D_MODEL), F32)], compiler_params=_params(2),
    )(h, fw8, *[target] * sub)


def _prep_bwd(proj, dq, dk, dv, db, dg, dqb, dkb, dlf, cw8, aux, lb8, nseq, t_len, tag):
    n = proj.shape[0]
    tt = _pick(t_len, (192, 128, 64))
    nt_ = t_len // tt
    qkv_w = 3 * HEADS_W
    rb = tt // SUBLANES
    ext = tt + SUBLANES

    def body(cur_ref, prev_ref, next_ref, misc_ref, bq_ref, bf_ref, dq_ref, dqn_ref, dk_ref, dkn_ref, dv_ref, dvn_ref,
             db_ref, dg_ref, dqb_ref, dkb_ref, dlf_ref, cw_ref, aux_ref, lb_ref,
             dqkv_ref, dmisc_ref, dbq_ref, dbf_ref, dcw_ref, daux_ref, dlb_ref, dy_ref):
        s, t = pl.program_id(0), pl.program_id(1)

        @pl.when((s == 0) & (t == 0))
        def _():
            dcw_ref[...] = jnp.zeros_like(dcw_ref)
            daux_ref[...] = jnp.zeros_like(daux_ref)
            dlb_ref[...] = jnp.zeros_like(dlb_ref)

        prev = jnp.where(t == 0, 0.0, prev_ref[...])
        x_ext = jnp.concatenate([prev, cur_ref[...], next_ref[...]], axis=0)
        y = _conv_ext(x_ext, cw_ref)
        inside = (t < nt_ - 1) | (_iota2((ext, 1), 0) < tt)
        dy_ref[0:SUBLANES, :] = jnp.zeros((SUBLANES, qkv_w), F32)
        for hd in range(N_HEADS):
            for grp, (g_ref, gn_ref, scale) in enumerate(((dq_ref, dqn_ref, D_HEAD ** -0.5), (dk_ref, dkn_ref, 1.0),
                                                          (dv_ref, dvn_ref, None))):
                lo = grp * HEADS_W + hd * D_HEAD
                sl = slice(hd * D_HEAD, (hd + 1) * D_HEAD)
                cot = jnp.concatenate([g_ref[:, sl], gn_ref[:, sl]], axis=0)
                fn = _silu if scale is None else functools.partial(_l2n_act, scale=scale)
                _, vjp = jax.vjp(fn, y[:, lo:lo + D_HEAD])
                dy_ref[SUBLANES:, lo:lo + D_HEAD] = jnp.where(inside, vjp(cot)[0], 0.0)
        dy_ext = dy_ref[...]
        dx = dy_ext * cw_ref[3:4, :]
        for kk in range(3):
            dx = dx + _shift_up(dy_ext, 3 - kk) * cw_ref[kk:kk + 1, :]
        dqkv_ref[...] = dx[SUBLANES:SUBLANES + tt].astype(BF16)
        dy_cur = dy_ext[SUBLANES:SUBLANES + tt]
        for kk in range(4):
            xs = _shift_down(x_ext, 3 - kk)[SUBLANES:SUBLANES + tt]
            dcw_ref[kk:kk + 1, :] += jnp.sum(xs * dy_cur, axis=0, keepdims=True)

        real = (t * tt + _iota2((tt, 1), 0)) >= N_PAD
        dmisc = jnp.zeros((tt, LANES), F32)
        daux = jnp.zeros((SUBLANES, LANES), F32)
        for hd in range(N_HEADS):
            sl = slice(hd * D_HEAD, (hd + 1) * D_HEAD)
            _, vjp = jax.vjp(lambda m, a: _gdn_gates(m, a, real, hd), misc_ref[...], aux_ref[...])
            dm, da = vjp((db_ref[:, sl], dg_ref[:, sl]))
            dmisc, daux = dmisc + dm, daux + da
        dmisc_ref[...] = dmisc.astype(BF16)
        daux_ref[...] += daux
        _, vjp = jax.vjp(lambda a, b, c: _hgrn_prep(a, b, c, real), bq_ref[...], bf_ref[...], lb_ref[0:1, :])
        dbq, dbf, dlb = vjp((dqb_ref[...], dkb_ref[...], dlf_ref[...]))
        dbq_ref[...], dbf_ref[...] = dbq.astype(BF16), dbf.astype(BF16)
        dlb_ref[0:1, :] += dlb

    row = lambda s, t: s * nt_ + t
    cur = lambda s, t: (row(s, t), 0)
    nxt = lambda s, t: (jnp.minimum((row(s, t) + 1) * rb, n // SUBLANES - 1), 0)
    wide = BS((tt, HEADS_W), cur)
    halo = BS((SUBLANES, HEADS_W), nxt)
    full = lambda s, t: (0, 0)
    return pl.pallas_call(
        body, name=f"prep_bwd_{tag}", grid=(nseq, nt_),
        in_specs=[BS((tt, qkv_w), cur), BS((SUBLANES, qkv_w), lambda s, t: (jnp.maximum(row(s, t) * rb - 1, 0), 0)),
                  BS((SUBLANES, qkv_w), nxt), BS((tt, LANES), lambda s, t: (row(s, t), C_MISC // LANES)),
                  BS((tt, HEADS_W), lambda s, t: (row(s, t), C_BQ // HEADS_W)),
                  BS((tt, HEADS_W), lambda s, t: (row(s, t), C_BF // HEADS_W)),
                  wide, halo, wide, halo, wide, halo, wide, wide, wide, wide, wide,
                  BS((SUBLANES, qkv_w), full), BS((SUBLANES, LANES), full), BS((SUBLANES, HEADS_W), full)],
        out_specs=[BS((tt, qkv_w), cur), BS((tt, LANES), cur), wide, wide,
                   BS((SUBLANES, qkv_w), full), BS((SUBLANES, LANES), full), BS((SUBLANES, HEADS_W), full)],
        out_shape=[SDS((n, qkv_w), BF16), SDS((n, LANES), BF16), SDS((n, HEADS_W), BF16), SDS((n, HEADS_W), BF16),
                   SDS((SUBLANES, qkv_w), F32), SDS((SUBLANES, LANES), F32), SDS((SUBLANES, HEADS_W), F32)],
        scratch_shapes=[pltpu.VMEM((tt + 2 * SUBLANES, qkv_w), F32)], compiler_params=_params(2),
    )(proj, proj, proj, proj, proj, proj, dq, dq, dk, dk, dv, dv, db, dg, dqb, dkb, dlf, cw8, aux, lb8)


def _proj_bwd_x(pieces, wp, h, nw8, dh_res, tag):
    n = h.shape[0]
    tm = _pick(n, (384, 256, 192, 128, 64))
    widths = [p.shape[1] for p in pieces]
    assert sum(widths) == PROJ_W

    def body(*refs):
        p_refs = refs[:len(pieces)]
        w_ref, h_ref, nw_ref, dres_ref, dh_ref, dnw_ref = refs[len(pieces):]

        @pl.when(pl.program_id(0) == 0)
        def _():
            dnw_ref[...] = jnp.zeros_like(dnw_ref)

        dxn, off = None, 0
        for p_ref, w in zip(p_refs, widths):
            part = _dg(p_ref[...], w_ref[:, off:off + w], ((1,), (1,)))
            dxn = part if dxn is None else dxn + part
            off += w
        _, vjp = jax.vjp(_rms, h_ref[...], nw_ref[0:1, :])
        dx, dnw = vjp(dxn)
        dh_ref[...] = dres_ref[...] + dx
        dnw_ref[0:1, :] += dnw

    r2 = lambda i: (i, 0)
    full = lambda i: (0, 0)
    return pl.pallas_call(
        body, name=f"proj_bwd_x_{tag}", grid=(n // tm,),
        in_specs=[BS((tm, w), r2) for w in widths] + [BS((D_MODEL, PROJ_W), full), BS((tm, D_MODEL), r2),
                                                      BS((SUBLANES, D_MODEL), full), BS((tm, D_MODEL), r2)],
        out_specs=[BS((tm, D_MODEL), r2), BS((SUBLANES, D_MODEL), full)],
        out_shape=[SDS((n, D_MODEL), F32), SDS((SUBLANES, D_MODEL), F32)], compiler_params=_params(1),
    )(*pieces, wp, h, nw8, dh_res)


def _proj_bwd_w(xn, pieces, tag):
    n = xn.shape[0]
    tm = _pick(n, (768, 384, 256, 192, 128, 64))
    widths = [p.shape[1] for p in pieces]
    assert sum(widths) == PROJ_W

    def body(*refs):
        x_ref, p_refs = refs[0], refs[1:1 + len(pieces)]
        o_ref, acc_ref = refs[1 + len(pieces):]

        @pl.when(pl.program_id(0) == 0)
        def _():
            acc_ref[...] = jnp.zeros_like(acc_ref)

        off = 0
        for p_ref, w in zip(p_refs, widths):
            acc_ref[:, off:off + w] += _dg(x_ref[...], p_ref[...], ((0,), (0,)))
            off += w

        @pl.when(pl.program_id(0) == pl.num_programs(0) - 1)
        def _():
            pltpu.sync_copy(acc_ref, o_ref)

    r2 = lambda i: (i, 0)
    return pl.pallas_call(
        body, name=f"proj_bwd_w_{tag}", grid=(n // tm,),
        in_specs=[BS((tm, D_MODEL), r2)] + [BS((tm, w), r2) for w in widths], out_specs=BS(memory_space=pl.ANY),
        out_shape=SDS((D_MODEL, PROJ_W), F32), scratch_shapes=[pltpu.VMEM((D_MODEL, PROJ_W), F32)],
        compiler_params=_params(1),
    )(xn, *pieces)


def _adamw(w, g, m, v, name):
    lead, rows, cols = w.shape
    tr = _pick(rows, (256, 128, 64, 32, 16, 8, 4, 2, 1)) if rows > 256 else rows

    def body(w_ref, g_ref, m_ref, v_ref, d_ref, nm_ref, nv_ref):
        gr = g_ref[...]
        m_new = ADAM_B1 * m_ref[...] + (1.0 - ADAM_B1) * gr
        v_new = ADAM_B2 * v_ref[...] + (1.0 - ADAM_B2) * jnp.square(gr)
        m_hat = m_new / (1.0 - ADAM_B1 ** ADAM_STEP)
        v_hat = v_new / (1.0 - ADAM_B2 ** ADAM_STEP)
        d_ref[...] = -ADAM_LR * (m_hat / (jnp.sqrt(v_hat) + ADAM_EPS) + ADAM_WD * w_ref[...])
        nm_ref[...] = m_new
        nv_ref[...] = v_new

    blk = BS((None, tr, cols), lambda a, i: (a, i, 0))
    return pl.pallas_call(
        body, name=name, grid=(lead, rows // tr), in_specs=[blk] * 4, out_specs=[blk] * 3,
        out_shape=[SDS((lead, rows, cols), F32)] * 3, compiler_params=_params(2),
    )(w, g, m, v)


def _row8(v, width):
    v = jnp.atleast_2d(v).astype(F32)
    return jnp.pad(v, ((0, SUBLANES - v.shape[0]), (0, width - v.shape[1])))


REF_MISC = 1536
N_MISC = 2 * N_HEADS
LAYOUT_RUNS = ((0, REF_MISC, 0), (REF_MISC + N_MISC, REF_W, REF_MISC), (REF_MISC, REF_MISC + N_MISC, C_MISC))


def _to_layout(slabs, tag):
    n_slabs, rows, width = slabs.shape
    tr = _pick(rows, (256, 128, 64, 32, 16))

    def body(x_ref, o_ref):
        off = 0
        for lo, hi, _ in sorted(LAYOUT_RUNS, key=lambda run: run[2]):
            for j in range(n_slabs):
                a, b = max(lo, j * width), min(hi, (j + 1) * width)
                if a < b:
                    o_ref[:, off:off + b - a] = x_ref[j, :, a - j * width:b - j * width]
                    off += b - a
        o_ref[:, off:] = jnp.zeros((tr, PROJ_W - off), o_ref.dtype)

    return pl.pallas_call(
        body, name=f"weights_layout_{tag}", grid=(rows // tr,), out_shape=SDS((rows, PROJ_W), slabs.dtype),
        in_specs=[BS((n_slabs, tr, width), lambda i: (0, i, 0))], out_specs=BS((tr, PROJ_W), lambda i: (i, 0)),
        compiler_params=_params(1),
    )(slabs)


def _from_layout(dw, n_slabs):
    width = REF_W // n_slabs
    slabs = []
    for j in range(n_slabs):
        pieces = []
        for lo, hi, at in sorted(LAYOUT_RUNS):
            a, b = max(lo, j * width), min(hi, (j + 1) * width)
            if a < b:
                pieces.append(dw[:, at + a - lo:at + b - lo])
        slabs.append(jnp.concatenate(pieces, axis=1))
    return slabs


def _lower_bounds(lb):
    sm = jax.nn.softmax(lb.astype(F32), axis=0)
    return jnp.cumsum(sm, axis=0) - sm[0]


def kernel(x, meta_tokens, norm_w, w_in, conv_w, a_log, dt_bias, gnorm_a, gnorm_b, hgrn_lower_bounds, w_branch_a, w_branch_b, w_out, final_norm_w, loss_target, m_meta_tokens, m_norm_w, m_w_in, m_conv_w, m_a_log, m_dt_bias, m_gnorm_a, m_gnorm_b, m_hgrn_lower_bounds, m_w_branch_a, m_w_branch_b, m_w_out, m_final_norm_w, v_meta_tokens, v_norm_w, v_w_in, v_conv_w, v_a_log, v_dt_bias, v_gnorm_a, v_gnorm_b, v_hgrn_lower_bounds, v_w_branch_a, v_w_branch_b, v_w_out, v_final_norm_w):
    nseq, seq, _ = x.shape
    depth = norm_w.shape[0]
    t_len = N_PAD + N_META + seq
    n = nseq * t_len
    conv_c = conv_w.shape[2]
    my = 4 * lax.axis_index("x") + 2 * lax.axis_index("y") + lax.axis_index("c")

    assert depth >= 2
    by_cols = lambda g: g.transpose(1, 2, 0, 3).reshape(g.shape[1], g.shape[2], N_DEV * g.shape[3])
    first = _all_gather_hbm([w_in[:1].astype(BF16), conv_w, meta_tokens], "gather_first")
    later_flight, later_token = _send_all_start(
        [w_in[1:].astype(BF16), w_branch_a.astype(BF16), w_branch_b.astype(BF16), w_out.astype(BF16)], False,
        "gather_later_start", after=first[0])
    w_in_slabs = [first[0]]
    conv_full = by_cols(first[1])
    meta_full = first[2].transpose(1, 0, 2).reshape(N_META, D_MODEL)

    lb_all, lb_vjp = jax.vjp(_lower_bounds, hgrn_lower_bounds)

    h = jnp.concatenate([jnp.zeros((nseq, N_PAD, D_MODEL), F32),
                         jnp.broadcast_to(meta_full[None], (nseq, N_META, D_MODEL)), x], axis=1).reshape(n, D_MODEL)
    saved = []
    for l in range(depth):
        wp = _to_layout(w_in_slabs[0][:, 0] if l == 0 else w_in_slabs[1][:, l - 1], l)
        nw8 = _row8(norm_w[l], D_MODEL)
        if l == 0:
            nw8 = nw8 + later_token[0:1, 0:1]
        cw8 = _row8(conv_full[l], 3 * HEADS_W)
        aux = _row8(jnp.stack([a_log[l], dt_bias[l]]), LANES)
        lb8 = _row8(lb_all[l], HEADS_W)
        gn8 = _row8(jnp.stack([gnorm_a[l], gnorm_b[l]]), LANES)
        proj, xn = _proj_fwd(h, nw8, wp, l)
        q, k, v, b, g, qb, kb, lf = _prep_fwd(proj, cw8, aux, lb8, nseq, t_len, l)
        oa, ob, sck_a, sck_b = _mixers_fwd(q, k, v, b, g, qb, kb, proj, C_BI // HEADS_W, lf, nseq, t_len, l)
        if l == 0:
            sent, landed = _send_all_wait(later_flight, ob, "gather_later_wait")
            landed = [lax.dynamic_update_slice(ld, own[None], (my,) + (0,) * own.ndim) for ld, own in zip(landed, sent)]
            w_in_slabs.append(landed[0])
            wa_full, wb_full = by_cols(landed[1]), by_cols(landed[2])
            wout_full = landed[3].transpose(1, 0, 2, 3).reshape(depth, D_MODEL, D_MODEL)
        wa_l, wb_l, wout_l = wa_full[l], wb_full[l], wout_full[l]
        h_next = _post_fwd(oa, ob, proj, h, gn8, wa_l, wb_l, wout_l, l)
        saved.append(dict(h=h, wp=wp, nw8=nw8, cw8=cw8, aux=aux, lb8=lb8, gn8=gn8, proj=proj, xn=xn, q=q, k=k, v=v, b=b,
                          wa=wa_l, wb=wb_l, wout=wout_l,
                          g=g, qb=qb, kb=kb, lf=lf, oa=oa, ob=ob, sck_a=sck_a, sck_b=sck_b))
        h = h_next

    dh, acc = _loss_head(h, _row8(final_norm_w, D_MODEL), loss_target, nseq, t_len)

    g_win, g_wa, g_wb, g_wout, g_conv, small = [], [], [], [], [], []

    def mixer_slabs(dwa_s, dwb_s, dwout_s):
        nl = len(dwa_s)
        rows = lambda a: jnp.stack(a).reshape(nl * HEADS_W, N_DEV, LANES).transpose(1, 0, 2)
        wout = jnp.stack(dwout_s).reshape(nl, N_DEV, LANES, D_MODEL).transpose(1, 0, 2, 3)
        return [jnp.concatenate([rows(dwa_s), rows(dwb_s)], axis=1).astype(BF16),
                wout.reshape(N_DEV, nl * LANES, D_MODEL).astype(BF16)]

    def win_slabs(per_layer, dtype):
        return jnp.stack([jnp.concatenate([sl[j] for sl in per_layer], axis=0) for j in range(N_DEV)]).astype(dtype)

    for l in reversed(range(depth)):
        s = saved[l]
        gn8, aux = s["gn8"], s["aux"]
        if l == 0:
            later_flight, later_token = _send_all_start(
                [win_slabs(g_win[::-1], BF16)] + mixer_slabs(g_wa[::-1], g_wb[::-1], g_wout[::-1]), True,
                "scatter_later_start")
            gn8 = gn8 + later_token[0:1, 0:1]
        doa, dob, dz, dbg, dga, dgb, dwa, dwb, dwout, dgn = _post_bwd(
            dh, s["oa"], s["ob"], s["proj"], s["h"], gn8, s["wa"], s["wb"], s["wout"], l)
        dq, dk, dv, db, dg, dqb, dkb, dbi, dlf = _mixers_bwd(
            s["q"], s["k"], s["v"], s["b"], s["g"], s["qb"], s["kb"], s["proj"], C_BI // HEADS_W, s["lf"], s["sck_a"],
            s["sck_b"], doa, dob, nseq, t_len, l)
        if l == 0:
            mixer_flight, mixer_token = _send_all_start(mixer_slabs([dwa], [dwb], [dwout]), True, "scatter_first_start")
            aux = aux + mixer_token[0:1, 0:1]
        dqkv, dmisc, dbq, dbf, dcw, daux, dlb = _prep_bwd(s["proj"], dq, dk, dv, db, dg, dqb, dkb, dlf, s["cw8"], aux,
                                                          s["lb8"], nseq, t_len, l)
        pieces = [dqkv, dz, dbq, dbf, dbi, dbg, dga, dgb, dmisc]
        g_win.append(_from_layout(_proj_bwd_w(s["xn"], pieces, l), N_DEV))
        g_conv.append(dcw[:4])
        nw8 = s["nw8"]
        if l == 0:
            dconv = jnp.stack(g_conv[::-1])
            conv_slabs = dconv.reshape(depth * dconv.shape[1], N_DEV, conv_c).transpose(1, 0, 2)
            win_flight, win_token = _send_all_start([win_slabs(g_win[-1:], BF16), conv_slabs], True, "scatter_win_start")
            nw8 = nw8 + win_token[0:1, 0:1]
        dh, dnw = _proj_bwd_x(pieces, s["wp"], s["h"], nw8, dh, l)
        g_wa.append(dwa)
        g_wb.append(dwb)
        g_wout.append(dwout)
        small.append((dnw[0], dgn[0], dgn[1], daux[0, :N_HEADS], daux[1, :N_HEADS], dlb[0]))
    small.reverse()
    dh = dh.reshape(nseq, t_len, D_MODEL)
    grad_x = dh[:, N_PAD + N_META:]

    packed = jnp.concatenate([small[0][1], small[1][1], small[0][2], small[1][2], small[0][3], small[1][3],
                              small[0][4], small[1][4]])
    tile = jnp.concatenate([
        jnp.sum(dh[:, N_PAD:N_PAD + N_META], axis=0), _row8(jnp.stack([small[0][0], small[1][0], acc[0]]), D_MODEL),
        _row8(jnp.stack([small[0][5], small[1][5]]), D_MODEL), _row8(packed, D_MODEL), _row8(acc[1], D_MODEL)], axis=0)
    tile = _all_reduce_small(tile, "reduce_small")
    loss = jnp.sum(tile[40])
    g_meta = lax.dynamic_slice_in_dim(tile[0:N_META], my * LANES, LANES, axis=1)
    g_norm, g_final = tile[16:18], tile[18]
    (g_lb,) = lb_vjp(tile[24:26, :HEADS_W])
    r21 = tile[32]
    g_gna, g_gnb = r21[0:256].reshape(2, LANES), r21[256:512].reshape(2, LANES)
    g_alog, g_dtb = r21[512:520].reshape(2, N_HEADS), r21[520:528].reshape(2, N_HEADS)

    def landed_sums(flight, tag):
        sent, landed = _send_all_wait(flight, dh, f"{tag}_wait")
        landed = [lax.dynamic_update_slice(ld, lax.dynamic_index_in_dim(src, my, 0, keepdims=True), (my, 0, 0))
                  for ld, src in zip(landed, sent)]
        return [_sum_slabs(ld, f"{tag}_sum{i}") for i, ld in enumerate(landed)]

    l_win, l_ab, l_wout = landed_sums(later_flight, "scatter_later")
    r_ab, r_wout = landed_sums(mixer_flight, "scatter_first")
    r_win, r_conv = landed_sums(win_flight, "scatter_win")
    both = lambda a, b, shape: jnp.concatenate([a.reshape(1, *shape[1:]), b.reshape(depth - 1, *shape[1:])])
    half, half_l = HEADS_W, (depth - 1) * HEADS_W
    mine = [both(r_win, l_win, w_in.shape), both(r_ab[:half], l_ab[:half_l], w_branch_a.shape),
            both(r_ab[half:], l_ab[half_l:], w_branch_b.shape), both(r_wout, l_wout, w_out.shape), r_conv]
    gseg = lambda i, shape: mine[i].reshape(shape)
    grads = {
        "meta_tokens": g_meta, "norm_w": g_norm, "w_in": gseg(0, w_in.shape), "conv_w": gseg(4, conv_w.shape),
        "a_log": g_alog, "dt_bias": g_dtb, "gnorm_a": g_gna, "gnorm_b": g_gnb, "hgrn_lower_bounds": g_lb,
        "w_branch_a": gseg(1, w_branch_a.shape), "w_branch_b": gseg(2, w_branch_b.shape), "w_out": gseg(3, w_out.shape),
        "final_norm_w": g_final}
    weights = {
        "meta_tokens": (meta_tokens, m_meta_tokens, v_meta_tokens), "norm_w": (norm_w, m_norm_w, v_norm_w),
        "w_in": (w_in, m_w_in, v_w_in), "conv_w": (conv_w, m_conv_w, v_conv_w), "a_log": (a_log, m_a_log, v_a_log),
        "dt_bias": (dt_bias, m_dt_bias, v_dt_bias), "gnorm_a": (gnorm_a, m_gnorm_a, v_gnorm_a),
        "gnorm_b": (gnorm_b, m_gnorm_b, v_gnorm_b),
        "hgrn_lower_bounds": (hgrn_lower_bounds, m_hgrn_lower_bounds, v_hgrn_lower_bounds),
        "w_branch_a": (w_branch_a, m_w_branch_a, v_w_branch_a), "w_branch_b": (w_branch_b, m_w_branch_b, v_w_branch_b),
        "w_out": (w_out, m_w_out, v_w_out), "final_norm_w": (final_norm_w, m_final_norm_w, v_final_norm_w)}
    names = list(weights)
    deltas, new_m, new_v = [], [], []
    for nm in names:
        w, m, v = weights[nm]
        view = (1,) * (3 - w.ndim) + w.shape
        d, m2, v2 = _adamw(w.reshape(view), grads[nm].reshape(view), m.reshape(view), v.reshape(view), f"adamw_{nm}")
        deltas.append(d.reshape(w.shape))
        new_m.append(m2.reshape(w.shape))
        new_v.append(v2.reshape(w.shape))
    return (loss, grad_x, *[grads[nm].reshape(weights[nm][0].shape) for nm in names], *deltas, *new_m, *new_v)
```
